```python
import math
import jax, jax.numpy as jnp
from jax import lax
import numpy as np

D_MODEL = 1024
BATCH = 8
SEQ = 8192
DEPTH = 1

N_HEADS = 8
HEAD_DIM = 128
ATTN_WIDTH = N_HEADS * HEAD_DIM
CONV_WIDTH = D_MODEL
CONV_K = 3
D_FF = 2816
PLE_DIM = 256
Q_BLOCK = 128
NORM_EPS = 1e-6
MIX_COLS = 3 * CONV_WIDTH + 3 * ATTN_WIDTH + 2 * D_MODEL

kernel_name = "hybrid_shortconv_stickbreaking_macaron_block"


def rms_norm(x, g):
    xf = x.astype(jnp.float32)
    y = xf * lax.rsqrt(jnp.mean(xf * xf, axis=-1, keepdims=True) + NORM_EPS)
    return (y * g.astype(jnp.float32)).astype(x.dtype)


def swiglu(x, w_in, w_out):
    gate, up = jnp.split(x @ w_in, 2, axis=-1)
    return (jax.nn.silu(gate) * up) @ w_out


def causal_depthwise_conv(x, w):
    return lax.conv_general_dilated(
        x, w[:, None, :].astype(x.dtype), window_strides=(1,),
        padding=[(CONV_K - 1, 0)], dimension_numbers=('NWC', 'WIO', 'NWC'),
        feature_group_count=x.shape[-1])


def stick_breaking_attention(q, k, v):
    b, h, s, d = q.shape
    nblk = s // Q_BLOCK
    scale = 1.0 / math.sqrt(d)
    k_pos = jnp.arange(s)
    vf = v.astype(jnp.float32)
    qb = q.reshape(b, h, nblk, Q_BLOCK, d).transpose(2, 0, 1, 3, 4)

    def block(args):
        q_blk, blk_idx = args
        q_pos = blk_idx * Q_BLOCK + jnp.arange(Q_BLOCK)
        z = jnp.einsum('bhqd,bhkd->bhqk', q_blk, k,
                       preferred_element_type=jnp.float32) * scale
        mask = k_pos[None, :] < q_pos[:, None]
        log_1m_beta = jnp.where(mask, jax.nn.log_sigmoid(-z), 0.0)
        tail = lax.cumsum(log_1m_beta, axis=3, reverse=True) - log_1m_beta
        a = jnp.where(mask, jnp.exp(jax.nn.log_sigmoid(z) + tail), 0.0)
        return jnp.einsum('bhqk,bhkd->bhqd', a, vf).astype(q.dtype)

    out = lax.map(block, (qb, jnp.arange(nblk)))
    return out.transpose(1, 2, 0, 3, 4).reshape(b, h, s, d)


def mix_split_points():
    widths = [CONV_WIDTH, CONV_WIDTH, CONV_WIDTH, ATTN_WIDTH, ATTN_WIDTH, ATTN_WIDTH, D_MODEL, D_MODEL]
    pts, acc = [], 0
    for w in widths[:-1]:
        acc += w
        pts.append(acc)
    return pts


def _fwd_setup_inputs(seed: int = 0) -> dict:
    key = jax.random.key(seed)
    ks = jax.random.split(key, 20)

    def w(k, shape, fan_in):
        return jax.random.normal(k, shape, jnp.float32) * (fan_in ** -0.5)

    def gain(k, shape):
        return 1.0 + 0.01 * jax.random.normal(k, shape, jnp.float32)

    return {
        "x": jax.random.normal(ks[0], (BATCH, SEQ, D_MODEL), jnp.float32),
        "p": jax.random.normal(ks[1], (DEPTH, BATCH, SEQ, PLE_DIM), jnp.float32),
        "ffn1_norm": gain(ks[2], (DEPTH, D_MODEL)),
        "ffn1_w_in": w(ks[3], (DEPTH, D_MODEL, 2 * D_FF), D_MODEL),
        "ffn1_w_out": w(ks[4], (DEPTH, D_FF, D_MODEL), D_FF),
        "mix_norm": gain(ks[5], (DEPTH, D_MODEL)),
        "w_mix_in": w(ks[6], (DEPTH, D_MODEL, MIX_COLS), D_MODEL),
        "conv_w": w(ks[7], (DEPTH, CONV_K, CONV_WIDTH), CONV_K),
        "w_conv_out": w(ks[8], (DEPTH, CONV_WIDTH, D_MODEL), CONV_WIDTH),
        "w_attn_out": w(ks[9], (DEPTH, ATTN_WIDTH, D_MODEL), ATTN_WIDTH),
        "w_mix_out": w(ks[10], (DEPTH, D_MODEL, D_MODEL), D_MODEL),
        "ffn2_norm": gain(ks[11], (DEPTH, D_MODEL)),
        "ffn2_w_in": w(ks[12], (DEPTH, D_MODEL, 2 * D_FF), D_MODEL),
        "ffn2_w_out": w(ks[13], (DEPTH, D_FF, D_MODEL), D_FF),
        "ple_norm": gain(ks[14], (DEPTH, D_MODEL)),
        "w_ple_gate": w(ks[15], (DEPTH, D_MODEL, D_MODEL), D_MODEL),
        "w_ple_proj": w(ks[16], (DEPTH, PLE_DIM, D_MODEL), PLE_DIM),
        "final_norm": gain(ks[17], (D_MODEL,)),
    }


def _fwd_reference(x, p, ffn1_norm, ffn1_w_in, ffn1_w_out, mix_norm, w_mix_in, conv_w,
              w_conv_out, w_attn_out, w_mix_out, ffn2_norm, ffn2_w_in, ffn2_w_out,
              ple_norm, w_ple_gate, w_ple_proj, final_norm):
    b, s, _ = x.shape
    splits = mix_split_points()
    h = x
    for i in range(DEPTH):
        h = h + 0.5 * swiglu(rms_norm(h, ffn1_norm[i]), ffn1_w_in[i], ffn1_w_out[i])

        u = rms_norm(h, mix_norm[i])
        c_b, c_c, c_x, q, k, v, g_conv, g_attn = jnp.split(u @ w_mix_in[i], splits, axis=-1)

        y_conv = (c_b * causal_depthwise_conv(c_c * c_x, conv_w[i])) @ w_conv_out[i]

        def heads(t):
            return t.reshape(b, s, N_HEADS, HEAD_DIM).transpose(0, 2, 1, 3)
        o = stick_breaking_attention(heads(q), heads(k), heads(v))
        y_attn = o.transpose(0, 2, 1, 3).reshape(b, s, ATTN_WIDTH) @ w_attn_out[i]

        merged = jax.nn.sigmoid(g_conv) * y_conv + jax.nn.sigmoid(g_attn) * y_attn
        h = h + merged @ w_mix_out[i]

        h = h + 0.5 * swiglu(rms_norm(h, ffn2_norm[i]), ffn2_w_in[i], ffn2_w_out[i])

        ple_gate = jax.nn.sigmoid(rms_norm(h, ple_norm[i]) @ w_ple_gate[i])
        h = h + ple_gate * (p[i] @ w_ple_proj[i])

    return rms_norm(h, final_norm)


import jax as _jax
import jax.numpy as _jnp

TWIN_FORMAT = 'train_step'
FWD_PARAMS = ['x', 'p', 'ffn1_norm', 'ffn1_w_in', 'ffn1_w_out', 'mix_norm', 'w_mix_in', 'conv_w', 'w_conv_out', 'w_attn_out', 'w_mix_out', 'ffn2_norm', 'ffn2_w_in', 'ffn2_w_out', 'ple_norm', 'w_ple_gate', 'w_ple_proj', 'final_norm']
TWIN_WEIGHTS = ['ffn1_norm', 'ffn1_w_in', 'ffn1_w_out', 'mix_norm', 'w_mix_in', 'conv_w', 'w_conv_out', 'w_attn_out', 'w_mix_out', 'ffn2_norm', 'ffn2_w_in', 'ffn2_w_out', 'ple_norm', 'w_ple_gate', 'w_ple_proj', 'final_norm']
TWIN_DIFF_INPUT = 'x'
TWIN_INPUTS = ['x', 'p', 'ffn1_norm', 'ffn1_w_in', 'ffn1_w_out', 'mix_norm', 'w_mix_in', 'conv_w', 'w_conv_out', 'w_attn_out', 'w_mix_out', 'ffn2_norm', 'ffn2_w_in', 'ffn2_w_out', 'ple_norm', 'w_ple_gate', 'w_ple_proj', 'final_norm', 'loss_target', 'm_ffn1_norm', 'm_ffn1_w_in', 'm_ffn1_w_out', 'm_mix_norm', 'm_w_mix_in', 'm_conv_w', 'm_w_conv_out', 'm_w_attn_out', 'm_w_mix_out', 'm_ffn2_norm', 'm_ffn2_w_in', 'm_ffn2_w_out', 'm_ple_norm', 'm_w_ple_gate', 'm_w_ple_proj', 'm_final_norm', 'v_ffn1_norm', 'v_ffn1_w_in', 'v_ffn1_w_out', 'v_mix_norm', 'v_w_mix_in', 'v_conv_w', 'v_w_conv_out', 'v_w_attn_out', 'v_w_mix_out', 'v_ffn2_norm', 'v_ffn2_w_in', 'v_ffn2_w_out', 'v_ple_norm', 'v_w_ple_gate', 'v_w_ple_proj', 'v_final_norm']
TWIN_OUTPUTS = ['loss', 'grad_x', 'grad_ffn1_norm', 'grad_ffn1_w_in', 'grad_ffn1_w_out', 'grad_mix_norm', 'grad_w_mix_in', 'grad_conv_w', 'grad_w_conv_out', 'grad_w_attn_out', 'grad_w_mix_out', 'grad_ffn2_norm', 'grad_ffn2_w_in', 'grad_ffn2_w_out', 'grad_ple_norm', 'grad_w_ple_gate', 'grad_w_ple_proj', 'grad_final_norm', 'delta_ffn1_norm', 'delta_ffn1_w_in', 'delta_ffn1_w_out', 'delta_mix_norm', 'delta_w_mix_in', 'delta_conv_w', 'delta_w_conv_out', 'delta_w_attn_out', 'delta_w_mix_out', 'delta_ffn2_norm', 'delta_ffn2_w_in', 'delta_ffn2_w_out', 'delta_ple_norm', 'delta_w_ple_gate', 'delta_w_ple_proj', 'delta_final_norm', 'new_m_ffn1_norm', 'new_m_ffn1_w_in', 'new_m_ffn1_w_out', 'new_m_mix_norm', 'new_m_w_mix_in', 'new_m_conv_w', 'new_m_w_conv_out', 'new_m_w_attn_out', 'new_m_w_mix_out', 'new_m_ffn2_norm', 'new_m_ffn2_w_in', 'new_m_ffn2_w_out', 'new_m_ple_norm', 'new_m_w_ple_gate', 'new_m_w_ple_proj', 'new_m_final_norm', 'new_v_ffn1_norm', 'new_v_ffn1_w_in', 'new_v_ffn1_w_out', 'new_v_mix_norm', 'new_v_w_mix_in', 'new_v_conv_w', 'new_v_w_conv_out', 'new_v_w_attn_out', 'new_v_w_mix_out', 'new_v_ffn2_norm', 'new_v_ffn2_w_in', 'new_v_ffn2_w_out', 'new_v_ple_norm', 'new_v_w_ple_gate', 'new_v_w_ple_proj', 'new_v_final_norm']
TWIN_LEAF_KINDS = {'loss': 'loss', 'grad_x': 'grad_x', 'grad_ffn1_norm': 'grad_w', 'grad_ffn1_w_in': 'grad_w', 'grad_ffn1_w_out': 'grad_w', 'grad_mix_norm': 'grad_w', 'grad_w_mix_in': 'grad_w', 'grad_conv_w': 'grad_w', 'grad_w_conv_out': 'grad_w', 'grad_w_attn_out': 'grad_w', 'grad_w_mix_out': 'grad_w', 'grad_ffn2_norm': 'grad_w', 'grad_ffn2_w_in': 'grad_w', 'grad_ffn2_w_out': 'grad_w', 'grad_ple_norm': 'grad_w', 'grad_w_ple_gate': 'grad_w', 'grad_w_ple_proj': 'grad_w', 'grad_final_norm': 'grad_w', 'delta_ffn1_norm': 'delta_w', 'delta_ffn1_w_in': 'delta_w', 'delta_ffn1_w_out': 'delta_w', 'delta_mix_norm': 'delta_w', 'delta_w_mix_in': 'delta_w', 'delta_conv_w': 'delta_w', 'delta_w_conv_out': 'delta_w', 'delta_w_attn_out': 'delta_w', 'delta_w_mix_out': 'delta_w', 'delta_ffn2_norm': 'delta_w', 'delta_ffn2_w_in': 'delta_w', 'delta_ffn2_w_out': 'delta_w', 'delta_ple_norm': 'delta_w', 'delta_w_ple_gate': 'delta_w', 'delta_w_ple_proj': 'delta_w', 'delta_final_norm': 'delta_w', 'new_m_ffn1_norm': 'new_m', 'new_m_ffn1_w_in': 'new_m', 'new_m_ffn1_w_out': 'new_m', 'new_m_mix_norm': 'new_m', 'new_m_w_mix_in': 'new_m', 'new_m_conv_w': 'new_m', 'new_m_w_conv_out': 'new_m', 'new_m_w_attn_out': 'new_m', 'new_m_w_mix_out': 'new_m', 'new_m_ffn2_norm': 'new_m', 'new_m_ffn2_w_in': 'new_m', 'new_m_ffn2_w_out': 'new_m', 'new_m_ple_norm': 'new_m', 'new_m_w_ple_gate': 'new_m', 'new_m_w_ple_proj': 'new_m', 'new_m_final_norm': 'new_m', 'new_v_ffn1_norm': 'new_v', 'new_v_ffn1_w_in': 'new_v', 'new_v_ffn1_w_out': 'new_v', 'new_v_mix_norm': 'new_v', 'new_v_w_mix_in': 'new_v', 'new_v_conv_w': 'new_v', 'new_v_w_conv_out': 'new_v', 'new_v_w_attn_out': 'new_v', 'new_v_w_mix_out': 'new_v', 'new_v_ffn2_norm': 'new_v', 'new_v_ffn2_w_in': 'new_v', 'new_v_ffn2_w_out': 'new_v', 'new_v_ple_norm': 'new_v', 'new_v_w_ple_gate': 'new_v', 'new_v_w_ple_proj': 'new_v', 'new_v_final_norm': 'new_v'}


def _forward(args):
    return _fwd_reference(*[args[k] for k in FWD_PARAMS])


def _output_shape():
    def fwd():
        inp = _fwd_setup_inputs(0)
        return _fwd_reference(*[inp[k] for k in FWD_PARAMS])
    out = _jax.eval_shape(fwd)
    return out.shape, out.dtype

N_MICROBATCH = 1
ADAM_LR = 0.001
ADAM_B1 = 0.9
ADAM_B2 = 0.999
ADAM_EPS = 1e-08
ADAM_WD = 0.01
ADAM_STEP = 10
PER_EXAMPLE_BATCH_AXIS = {'x': 0, 'p': 1, 'loss_target': 0}
SHARED_INPUTS = []
_WEIGHT_DTYPES = {'ffn1_norm': _jnp.float32, 'ffn1_w_in': _jnp.float32, 'ffn1_w_out': _jnp.float32, 'mix_norm': _jnp.float32, 'w_mix_in': _jnp.float32, 'conv_w': _jnp.float32, 'w_conv_out': _jnp.float32, 'w_attn_out': _jnp.float32, 'w_mix_out': _jnp.float32, 'ffn2_norm': _jnp.float32, 'ffn2_w_in': _jnp.float32, 'ffn2_w_out': _jnp.float32, 'ple_norm': _jnp.float32, 'w_ple_gate': _jnp.float32, 'w_ple_proj': _jnp.float32, 'final_norm': _jnp.float32}
MOMENT_SCALE = {'ffn1_norm': 1.248504e-01, 'ffn1_w_in': 5.099752e-02, 'ffn1_w_out': 8.327582e-02, 'mix_norm': 2.197014e-01, 'w_mix_in': 7.338005e-02, 'conv_w': 1.061013e-01, 'w_conv_out': 1.056747e-01, 'w_attn_out': 6.867152e-02, 'w_mix_out': 1.263085e-01, 'ffn2_norm': 8.080027e-02, 'ffn2_w_in': 3.381701e-02, 'ffn2_w_out': 5.524413e-02, 'ple_norm': 4.066787e-02, 'w_ple_gate': 3.873675e-02, 'w_ple_proj': 9.902903e-02, 'final_norm': 6.402261e+01}


def _to_microbatches(a, axis):
    t = _jnp.moveaxis(a, axis, 0)
    t = t.reshape((N_MICROBATCH, t.shape[0] // N_MICROBATCH) + t.shape[1:])
    return _jnp.moveaxis(t, 1, axis + 1)


def setup_inputs(seed: int = 0) -> dict:
    inp = _fwd_setup_inputs(seed)
    key = _jax.random.fold_in(_jax.random.key(seed), 7919)
    shape, _ = _output_shape()
    out = dict(inp)
    out["loss_target"] = _jax.random.normal(_jax.random.fold_in(key, 0), shape, _jnp.float32)
    for i, name in enumerate(TWIN_WEIGHTS):
        w = inp[name].astype(_jnp.float32)
        if MOMENT_SCALE is None:
            s = _jnp.sqrt(_jnp.mean(_jnp.square(w)) + 1e-30)
        else:
            s = MOMENT_SCALE[name]
        km, kv = _jax.random.split(_jax.random.fold_in(key, i + 1))
        out[name] = w
        out["m_" + name] = s * _jax.random.normal(km, w.shape, _jnp.float32)
        out["v_" + name] = (s * s) * _jax.random.uniform(kv, w.shape, _jnp.float32, 0.5, 1.5)
    if N_MICROBATCH > 1:
        for name, axis in PER_EXAMPLE_BATCH_AXIS.items():
            out[name] = _to_microbatches(out[name], axis)
    return {'x': out['x'], 'p': out['p'], 'ffn1_norm': out['ffn1_norm'], 'ffn1_w_in': out['ffn1_w_in'], 'ffn1_w_out': out['ffn1_w_out'], 'mix_norm': out['mix_norm'], 'w_mix_in': out['w_mix_in'], 'conv_w': out['conv_w'], 'w_conv_out': out['w_conv_out'], 'w_attn_out': out['w_attn_out'], 'w_mix_out': out['w_mix_out'], 'ffn2_norm': out['ffn2_norm'], 'ffn2_w_in': out['ffn2_w_in'], 'ffn2_w_out': out['ffn2_w_out'], 'ple_norm': out['ple_norm'], 'w_ple_gate': out['w_ple_gate'], 'w_ple_proj': out['w_ple_proj'], 'final_norm': out['final_norm'], 'loss_target': out['loss_target'], 'm_ffn1_norm': out['m_ffn1_norm'], 'm_ffn1_w_in': out['m_ffn1_w_in'], 'm_ffn1_w_out': out['m_ffn1_w_out'], 'm_mix_norm': out['m_mix_norm'], 'm_w_mix_in': out['m_w_mix_in'], 'm_conv_w': out['m_conv_w'], 'm_w_conv_out': out['m_w_conv_out'], 'm_w_attn_out': out['m_w_attn_out'], 'm_w_mix_out': out['m_w_mix_out'], 'm_ffn2_norm': out['m_ffn2_norm'], 'm_ffn2_w_in': out['m_ffn2_w_in'], 'm_ffn2_w_out': out['m_ffn2_w_out'], 'm_ple_norm': out['m_ple_norm'], 'm_w_ple_gate': out['m_w_ple_gate'], 'm_w_ple_proj': out['m_w_ple_proj'], 'm_final_norm': out['m_final_norm'], 'v_ffn1_norm': out['v_ffn1_norm'], 'v_ffn1_w_in': out['v_ffn1_w_in'], 'v_ffn1_w_out': out['v_ffn1_w_out'], 'v_mix_norm': out['v_mix_norm'], 'v_w_mix_in': out['v_w_mix_in'], 'v_conv_w': out['v_conv_w'], 'v_w_conv_out': out['v_w_conv_out'], 'v_w_attn_out': out['v_w_attn_out'], 'v_w_mix_out': out['v_w_mix_out'], 'v_ffn2_norm': out['v_ffn2_norm'], 'v_ffn2_w_in': out['v_ffn2_w_in'], 'v_ffn2_w_out': out['v_ffn2_w_out'], 'v_ple_norm': out['v_ple_norm'], 'v_w_ple_gate': out['v_w_ple_gate'], 'v_w_ple_proj': out['v_w_ple_proj'], 'v_final_norm': out['v_final_norm']}


def _loss(weights, diff, rest, loss_target):
    with _jax.named_scope("forward"):
        args = {**rest, TWIN_DIFF_INPUT: diff, **{k: w.astype(_WEIGHT_DTYPES[k]) for k, w in weights.items()}}
        y = _forward(args)
    with _jax.named_scope("loss_head"):
        err = _jnp.square(y.astype(_jnp.float32) - loss_target)
        return 0.5 * _jnp.sum(_jnp.mean(err, axis=-1)) if err.ndim else 0.5 * err


def _adamw(w, g, m, v):
    m = ADAM_B1 * m + (1.0 - ADAM_B1) * g
    v = ADAM_B2 * v + (1.0 - ADAM_B2) * _jnp.square(g)
    m_hat = m / (1.0 - ADAM_B1 ** ADAM_STEP)
    v_hat = v / (1.0 - ADAM_B2 ** ADAM_STEP)
    delta = -ADAM_LR * (m_hat / (_jnp.sqrt(v_hat) + ADAM_EPS) + ADAM_WD * w)
    return delta, m, v


def reference(x, p, ffn1_norm, ffn1_w_in, ffn1_w_out, mix_norm, w_mix_in, conv_w, w_conv_out, w_attn_out, w_mix_out, ffn2_norm, ffn2_w_in, ffn2_w_out, ple_norm, w_ple_gate, w_ple_proj, final_norm, loss_target, m_ffn1_norm, m_ffn1_w_in, m_ffn1_w_out, m_mix_norm, m_w_mix_in, m_conv_w, m_w_conv_out, m_w_attn_out, m_w_mix_out, m_ffn2_norm, m_ffn2_w_in, m_ffn2_w_out, m_ple_norm, m_w_ple_gate, m_w_ple_proj, m_final_norm, v_ffn1_norm, v_ffn1_w_in, v_ffn1_w_out, v_mix_norm, v_w_mix_in, v_conv_w, v_w_conv_out, v_w_attn_out, v_w_mix_out, v_ffn2_norm, v_ffn2_w_in, v_ffn2_w_out, v_ple_norm, v_w_ple_gate, v_w_ple_proj, v_final_norm):
    given = dict(x=x, p=p, ffn1_norm=ffn1_norm, ffn1_w_in=ffn1_w_in, ffn1_w_out=ffn1_w_out, mix_norm=mix_norm, w_mix_in=w_mix_in, conv_w=conv_w, w_conv_out=w_conv_out, w_attn_out=w_attn_out, w_mix_out=w_mix_out, ffn2_norm=ffn2_norm, ffn2_w_in=ffn2_w_in, ffn2_w_out=ffn2_w_out, ple_norm=ple_norm, w_ple_gate=w_ple_gate, w_ple_proj=w_ple_proj, final_norm=final_norm, loss_target=loss_target, m_ffn1_norm=m_ffn1_norm, m_ffn1_w_in=m_ffn1_w_in, m_ffn1_w_out=m_ffn1_w_out, m_mix_norm=m_mix_norm, m_w_mix_in=m_w_mix_in, m_conv_w=m_conv_w, m_w_conv_out=m_w_conv_out, m_w_attn_out=m_w_attn_out, m_w_mix_out=m_w_mix_out, m_ffn2_norm=m_ffn2_norm, m_ffn2_w_in=m_ffn2_w_in, m_ffn2_w_out=m_ffn2_w_out, m_ple_norm=m_ple_norm, m_w_ple_gate=m_w_ple_gate, m_w_ple_proj=m_w_ple_proj, m_final_norm=m_final_norm, v_ffn1_norm=v_ffn1_norm, v_ffn1_w_in=v_ffn1_w_in, v_ffn1_w_out=v_ffn1_w_out, v_mix_norm=v_mix_norm, v_w_mix_in=v_w_mix_in, v_conv_w=v_conv_w, v_w_conv_out=v_w_conv_out, v_w_attn_out=v_w_attn_out, v_w_mix_out=v_w_mix_out, v_ffn2_norm=v_ffn2_norm, v_ffn2_w_in=v_ffn2_w_in, v_ffn2_w_out=v_ffn2_w_out, v_ple_norm=v_ple_norm, v_w_ple_gate=v_w_ple_gate, v_w_ple_proj=v_w_ple_proj, v_final_norm=v_final_norm)
    weights = {n: given[n] for n in TWIN_WEIGHTS}
    shared = {n: given[n] for n in SHARED_INPUTS}
    per_example = {n: given[n] for n in ['x', 'p']}
    grad_fn = _jax.value_and_grad(_loss, argnums=(0, 1))

    def one_microbatch(ex, loss_target):
        ex = dict(ex)
        diff = ex.pop(TWIN_DIFF_INPUT)
        return grad_fn(weights, diff, {**shared, **ex}, loss_target)

    if N_MICROBATCH == 1:
        loss, (grad_w, grad_x) = one_microbatch(per_example, given["loss_target"])
    else:
        def body(carry, xs):
            loss_sum, grad_sum = carry
            l_k, (gw_k, gx_k) = one_microbatch(xs[0], xs[1])
            with _jax.named_scope("update"):
                return (loss_sum + l_k, _jax.tree.map(_jnp.add, grad_sum, gw_k)), gx_k

        init = (_jnp.zeros((), _jnp.float32), _jax.tree.map(_jnp.zeros_like, weights))
        (loss, grad_w), grad_x = _jax.lax.scan(body, init, (per_example, given["loss_target"]))
    with _jax.named_scope("update"):
        delta_w, new_m, new_v = {}, {}, {}
        for n in TWIN_WEIGHTS:
            delta_w[n], new_m[n], new_v[n] = _adamw(weights[n], grad_w[n], given["m_" + n], given["v_" + n])
    return (loss, grad_x, *[grad_w[n] for n in TWIN_WEIGHTS], *[delta_w[n] for n in TWIN_WEIGHTS],
            *[new_m[n] for n in TWIN_WEIGHTS], *[new_v[n] for n in TWIN_WEIGHTS])
```

```python
import functools
import math

import jax
import jax.numpy as jnp
from jax import lax
from jax.experimental import pallas as pl
from jax.experimental.pallas import tpu as pltpu

D_MODEL = 1024
D_FF = 2816
N_SHARDS = 8
FF_CHUNK = 2 * D_FF // N_SHARDS
N_FF_CHUNKS = D_FF // FF_CHUNK
N_HEADS = 8
HEAD_DIM = 128
PLE_DIM = 256
NORM_EPS = 1e-6
N_MIX = 8
ADAM_LR, ADAM_B1, ADAM_B2, ADAM_EPS, ADAM_WD, ADAM_STEP = 0.001, 0.9, 0.999, 1e-08, 0.01, 10

TOKEN_TILE = 256
WGRAD_TILE = 1024
ATTN_Q = 256
ATTN_K = 2 * ATTN_Q
ATTN_SKIP_BELOW = -90.0

BF = jnp.bfloat16
F32 = jnp.float32
MESH = pl.DeviceIdType.MESH
NT = (((1,), (1,)), ((), ()))
TN = (((0,), (0,)), ((), ()))
S = jax.ShapeDtypeStruct
ANY = pl.BlockSpec(memory_space=pl.ANY)


def _const_spec(shape):
    nd = len(shape)
    return pl.BlockSpec(shape, lambda *_: (0,) * nd, pipeline_mode=pl.Buffered(1))


def _rows(tm, cols):
    return pl.BlockSpec((tm, cols), lambda i: (i, 0))


def _chunks(tm):
    return pl.BlockSpec((N_FF_CHUNKS, tm, FF_CHUNK), lambda i: (0, i, 0))


def _acc_spec(shape):
    nd = len(shape)
    return pl.BlockSpec(shape, lambda *_: (0,) * nd)


def _dot(a, b):
    return jnp.dot(a, b, preferred_element_type=F32)


def _dot_nt(a, b):
    return lax.dot_general(a, b, NT, preferred_element_type=F32)


def _dot_tn(a, b):
    return lax.dot_general(a, b, TN, preferred_element_type=F32)


def _rms(h, g):
    r = lax.rsqrt(jnp.mean(h * h, axis=-1, keepdims=True) + NORM_EPS)
    return h * r * g


def _rms_bwd(dn, h, g):
    r = lax.rsqrt(jnp.mean(h * h, axis=-1, keepdims=True) + NORM_EPS)
    nh = h * r
    gd = dn * g
    dh = r * (gd - nh * jnp.mean(gd * nh, axis=-1, keepdims=True))
    return dh, jnp.sum(dn * nh, axis=0, keepdims=True)


def _accumulate(ref, val):
    @pl.when(pl.program_id(0) == 0)
    def _():
        ref[...] = jnp.zeros_like(ref)
    ref[...] += val


def _place():
    x, y, c = lax.axis_index("x"), lax.axis_index("y"), lax.axis_index("c")
    return x, y, c


def _slot(px, py, pc):
    return 4 * px + 2 * py + pc


def _all_gather(shards, name):
    n = len(shards)

    def body(*refs):
        ins, outs = refs[:n], refs[n:2 * n]
        send_sems, recv_sems, local_sems = refs[2 * n:]
        x, y, c = _place()
        me, sibling = (x, y, c), (x, y, 1 - c)
        chips = [(1 - x, y), (x, 1 - y), (1 - x, 1 - y)]

        def copy(a, k, block, to, src=None):
            dst = outs[a].at[_slot(*block)]
            return pltpu.make_async_remote_copy(
                src_ref=dst if src is None else src, dst_ref=dst,
                send_sem=send_sems.at[a, k], recv_sem=recv_sems.at[a, k],
                device_id=to, device_id_type=MESH)

        mine = [pltpu.make_async_copy(ins[a], outs[a].at[_slot(*me)], local_sems.at[a]) for a in range(n)]
        for cp in mine:
            cp.start()
        first = []
        for a in range(n):
            first.append(copy(a, 0, me, sibling, src=ins[a]))
            first += [copy(a, 1 + j, me, (*chip, c), src=ins[a]) for j, chip in enumerate(chips)]
        for cp in first:
            cp.start()
        passed = [[copy(a, 4 + j, (*chip, c), sibling) for j, chip in enumerate(chips)] for a in range(n)]
        for j, chip in enumerate(chips):
            for a in range(n):
                copy(a, 1 + j, (*chip, c), me).wait_recv()
                passed[a][j].start()
        for a in range(n):
            copy(a, 0, sibling, me).wait_recv()
            for j, chip in enumerate(chips):
                copy(a, 4 + j, (*chip, 1 - c), me).wait_recv()
        for cp in first + [cp for row in passed for cp in row]:
            cp.wait_send()
        for cp in mine:
            cp.wait()

    return pl.pallas_call(
        body, name=name,
        in_specs=[ANY] * n, out_specs=[ANY] * n,
        out_shape=[S((N_SHARDS,) + s.shape, s.dtype) for s in shards],
        scratch_shapes=[pltpu.SemaphoreType.DMA((n, 7)), pltpu.SemaphoreType.DMA((n, 7)), pltpu.SemaphoreType.DMA((n,))],
    )(*shards)


def _reduce_scatter_exchange(parts, name):
    n = len(parts)

    def body(*refs):
        ins, outs = refs[:n], refs[n:2 * n]
        send_sems, recv_sems, local_sems = refs[2 * n:]
        x, y, c = _place()
        me = _slot(x, y, c)
        mine = [pltpu.make_async_copy(ins[a].at[me], outs[a].at[me], local_sems.at[a]) for a in range(n)]
        for cp in mine:
            cp.start()
        copies = []
        for k in range(1, N_SHARDS):
            px = 1 - x if k & 4 else x
            py = 1 - y if k & 2 else y
            pc = 1 - c if k & 1 else c
            for a in range(n):
                copies.append(pltpu.make_async_remote_copy(
                    src_ref=ins[a].at[_slot(px, py, pc)], dst_ref=outs[a].at[me],
                    send_sem=send_sems.at[a, k - 1], recv_sem=recv_sems.at[a, k - 1],
                    device_id=(px, py, pc), device_id_type=MESH))
        for cp in copies:
            cp.start()
        for cp in copies:
            cp.wait()
        for cp in mine:
            cp.wait()

    return pl.pallas_call(
        body, name=name,
        in_specs=[ANY] * n, out_specs=[ANY] * n,
        out_shape=[S(p.shape, p.dtype) for p in parts],
        scratch_shapes=[pltpu.SemaphoreType.DMA((n, 7)), pltpu.SemaphoreType.DMA((n, 7)), pltpu.SemaphoreType.DMA((n,))],
    )(*parts)


def _prenorm(x, g):
    t = x.shape[0]
    tm = min(TOKEN_TILE, t)

    def body(x_ref, g_ref, n_ref):
        n_ref[...] = _rms(x_ref[...], g_ref[...]).astype(BF)

    return pl.pallas_call(
        body, name="prenorm", grid=(t // tm,),
        in_specs=[_rows(tm, D_MODEL), _const_spec((1, D_MODEL))], out_specs=_rows(tm, D_MODEL),
        out_shape=S((t, D_MODEL), BF))(x, g)


def _ffn_fwd(h, n, w_in, w_out, g_next, name):
    t = h.shape[0]
    tm = min(TOKEN_TILE, t)

    def body(h_ref, n_ref, win_ref, wout_ref, g_ref, ho_ref, no_ref, gate_ref, up_ref):
        nb = n_ref[...]
        acc = jnp.zeros((tm, D_MODEL), F32)
        for c in range(N_FF_CHUNKS):
            gate = _dot(nb, win_ref[c])
            up = _dot(nb, win_ref[N_FF_CHUNKS + c])
            gate_ref[c] = gate.astype(BF)
            up_ref[c] = up.astype(BF)
            act = (gate * jax.nn.sigmoid(gate) * up).astype(BF)
            acc = acc + _dot(act, wout_ref[c])
        ho = h_ref[...] + 0.5 * acc
        ho_ref[...] = ho
        no_ref[...] = _rms(ho, g_ref[...]).astype(BF)

    return pl.pallas_call(
        body, name=name, grid=(t // tm,),
        in_specs=[_rows(tm, D_MODEL), _rows(tm, D_MODEL), _const_spec(w_in.shape), _const_spec(w_out.shape),
                  _const_spec((1, D_MODEL))],
        out_specs=[_rows(tm, D_MODEL), _rows(tm, D_MODEL), _chunks(tm), _chunks(tm)],
        out_shape=[S((t, D_MODEL), F32), S((t, D_MODEL), BF),
                   S((N_FF_CHUNKS, t, FF_CHUNK), BF), S((N_FF_CHUNKS, t, FF_CHUNK), BF)],
    )(h, n, w_in, w_out, g_next)


def _mix_proj(u, w_mix):
    t = u.shape[0]
    tm = min(1024, t)

    def body(u_ref, w_ref, o_ref):
        o_ref[0] = _dot(u_ref[...], w_ref[0]).astype(BF)

    return pl.pallas_call(
        body, name="mix_proj", grid=(N_MIX, t // tm),
        in_specs=[pl.BlockSpec((tm, D_MODEL), lambda d, i: (i, 0)), pl.BlockSpec((1, D_MODEL, D_MODEL), lambda d, i: (d, 0, 0))],
        out_specs=pl.BlockSpec((1, tm, D_MODEL), lambda d, i: (d, i, 0)),
        out_shape=S((N_MIX, t, D_MODEL), BF))(u, w_mix)


HALO = 16


def _piece(d, tm):
    return pl.BlockSpec((1, tm, D_MODEL), lambda i: (d, i, 0))


def _prev_halo(d, tm):
    return pl.BlockSpec((1, HALO, D_MODEL), lambda i: (d, jnp.maximum(i * (tm // HALO) - 1, 0), 0))


def _shift_down(m, prev_tail, k):
    tm = m.shape[0]
    out = pltpu.roll(m, k, 0)
    row = lax.broadcasted_iota(jnp.int32, (tm, 1), 0)
    for j in range(k):
        out = jnp.where(row == j, prev_tail[HALO - k + j:HALO - k + j + 1, :], out)
    return out


def _conv_inputs(cc_ref, cx_ref, cch_ref, cxh_ref):
    m = cc_ref[0].astype(F32) * cx_ref[0].astype(F32)
    mh = cch_ref[0].astype(F32) * cxh_ref[0].astype(F32)
    mh = jnp.where(pl.program_id(0) == 0, 0.0, mh)
    return m, _shift_down(m, mh, 1), _shift_down(m, mh, 2)


def _mixer_out(proj, o, h1, conv_w, w_co, w_ao, w_mo, g_next):
    t = h1.shape[0]
    tm = min(TOKEN_TILE, t)

    def body(cb_ref, cc_ref, cx_ref, gc_ref, ga_ref, cch_ref, cxh_ref, o_ref, h_ref, cw_ref, wco_ref, wao_ref, wmo_ref,
             g_ref, ho_ref, no_ref, ycin_ref, yc_ref, ya_ref, mg_ref):
        m, m1, m2 = _conv_inputs(cc_ref, cx_ref, cch_ref, cxh_ref)
        cw = cw_ref[...]
        cv = cw[0:1, :] * m2 + cw[1:2, :] * m1 + cw[2:3, :] * m
        ycin = (cb_ref[0].astype(F32) * cv).astype(BF)
        ycin_ref[...] = ycin
        yc = _dot(ycin, wco_ref[...])
        ya = _dot(o_ref[...].astype(BF), wao_ref[...])
        yc_ref[...] = yc.astype(BF)
        ya_ref[...] = ya.astype(BF)
        merged = (jax.nn.sigmoid(gc_ref[0].astype(F32)) * yc + jax.nn.sigmoid(ga_ref[0].astype(F32)) * ya).astype(BF)
        mg_ref[...] = merged
        ho = h_ref[...] + _dot(merged, wmo_ref[...])
        ho_ref[...] = ho
        no_ref[...] = _rms(ho, g_ref[...]).astype(BF)

    sq = (D_MODEL, D_MODEL)
    return pl.pallas_call(
        body, name="mixer_out", grid=(t // tm,),
        in_specs=[_piece(0, tm), _piece(1, tm), _piece(2, tm), _piece(6, tm), _piece(7, tm), _prev_halo(1, tm), _prev_halo(2, tm),
                  _rows(tm, D_MODEL), _rows(tm, D_MODEL), _const_spec((3, D_MODEL)), _const_spec(sq), _const_spec(sq),
                  _const_spec(sq), _const_spec((1, D_MODEL))],
        out_specs=[_rows(tm, D_MODEL)] * 6,
        out_shape=[S((t, D_MODEL), F32)] + [S((t, D_MODEL), BF)] * 5,
    )(proj, proj, proj, proj, proj, proj, proj, o, h1, conv_w, w_co, w_ao, w_mo, g_next)


def _suffix_sum(vals, tri):
    hi = vals.astype(BF)
    lo = (vals - hi.astype(F32)).astype(BF)
    return _dot(hi, tri) + _dot(lo, tri)


def _attn_step(q, kb, start, bound, row):
    z = _dot_nt(q, kb) * (1.0 / math.sqrt(HEAD_DIM))
    col = start + lax.broadcasted_iota(jnp.int32, (1, ATTN_K), 1)
    mask = jnp.logical_and(col < row, col < bound)
    soft = jnp.log1p(jnp.exp(-jnp.abs(z)))
    log_beta = jnp.minimum(z, 0.0) - soft
    log_rest = jnp.where(mask, jnp.minimum(-z, 0.0) - soft, 0.0)
    return z, mask, log_beta, log_rest


def _attn_sweep_start(i, t):
    return jnp.maximum(i + 1 - ATTN_K // ATTN_Q, 0), jnp.int32(t)


def _attn_keys(blk):
    return pl.multiple_of(blk * ATTN_Q, ATTN_Q)


def _attn_next(blk):
    return jnp.maximum(blk - ATTN_K // ATTN_Q, 0), blk * ATTN_Q


def _tri(strict):
    r = lax.broadcasted_iota(jnp.int32, (ATTN_K, ATTN_K), 0)
    c = lax.broadcasted_iota(jnp.int32, (ATTN_K, ATTN_K), 1)
    return (r > c if strict else r >= c).astype(BF)


def _head_cols(piece):
    return lambda t: pl.BlockSpec((1, t, HEAD_DIM), lambda h, i: (piece, 0, h))


def _attn_fwd(proj):
    t = proj.shape[1]
    nq = t // ATTN_Q
    tri = _tri(strict=True)

    def body(q_ref, k_ref, v_ref, tri_ref, o_ref):
        i = pl.program_id(1)
        q = q_ref[0]
        row = i * ATTN_Q + lax.broadcasted_iota(jnp.int32, (ATTN_Q, 1), 0)

        def cond(carry):
            _, bound, _, _, r_max = carry
            return jnp.logical_and(bound > 0, r_max > ATTN_SKIP_BELOW)

        def step(carry):
            blk, bound, run, acc, _ = carry
            start = _attn_keys(blk)
            kb = k_ref[0, pl.ds(start, ATTN_K), :]
            vb = v_ref[0, pl.ds(start, ATTN_K), :]
            _, mask, log_beta, log_rest = _attn_step(q, kb, start, bound, row)
            tail = run + _suffix_sum(log_rest, tri_ref[...])
            a = jnp.where(mask, jnp.exp(log_beta + tail), 0.0)
            acc = acc + _dot(a.astype(BF), vb)
            run = run + jnp.sum(log_rest, axis=1, keepdims=True)
            nblk, nbound = _attn_next(blk)
            return nblk, nbound, run, acc, jnp.max(run)

        blk0, bound = _attn_sweep_start(i, t)
        init = (blk0, bound, jnp.zeros((ATTN_Q, 1), F32), jnp.zeros((ATTN_Q, HEAD_DIM), F32), jnp.float32(0.0))
        o_ref[...] = lax.while_loop(cond, step, init)[3]

    qspec = pl.BlockSpec((1, ATTN_Q, HEAD_DIM), lambda h, i: (3, i, h))
    return pl.pallas_call(
        body, name="attn_fwd", grid=(N_HEADS, nq),
        in_specs=[qspec, _head_cols(4)(t), _head_cols(5)(t), pl.BlockSpec((ATTN_K, ATTN_K), lambda h, i: (0, 0))],
        out_specs=pl.BlockSpec((ATTN_Q, HEAD_DIM), lambda h, i: (i, h)),
        out_shape=S((t, D_MODEL), F32))(proj, proj, proj, tri)


def _attn_bwd(proj, o, d_o):
    t = proj.shape[1]
    nq = t // ATTN_Q
    tri_strict, tri_incl = _tri(strict=True), _tri(strict=False)

    def body(q_ref, k_ref, v_ref, o_ref, do_ref, tris_ref, trii_ref, dq_ref, dk_ref, dv_ref, dk_acc, dv_acc):
        i = pl.program_id(1)

        @pl.when(i == 0)
        def _():
            dk_acc[...] = jnp.zeros_like(dk_acc)
            dv_acc[...] = jnp.zeros_like(dv_acc)

        q = q_ref[0]
        do = do_ref[...]
        total = jnp.sum(do.astype(F32) * o_ref[...], axis=1, keepdims=True)
        row = i * ATTN_Q + lax.broadcasted_iota(jnp.int32, (ATTN_Q, 1), 0)

        def cond(carry):
            _, bound, _, _, _, r_max = carry
            return jnp.logical_and(bound > 0, r_max > ATTN_SKIP_BELOW)

        def step(carry):
            blk, bound, run, seen, dq, _ = carry
            start = _attn_keys(blk)
            kb = k_ref[0, pl.ds(start, ATTN_K), :]
            vb = v_ref[0, pl.ds(start, ATTN_K), :]
            z, mask, log_beta, log_rest = _attn_step(q, kb, start, bound, row)
            tail = run + _suffix_sum(log_rest, tris_ref[...])
            a = jnp.where(mask, jnp.exp(log_beta + tail), 0.0).astype(BF)
            de = _dot_nt(do, vb) * a.astype(F32)
            left = total - seen - _suffix_sum(de, trii_ref[...])
            beta = jax.nn.sigmoid(z)
            dz = jnp.where(mask, de * (1.0 - beta) - left * beta, 0.0) * (1.0 / math.sqrt(HEAD_DIM))
            dzb = dz.astype(BF)
            dv_acc[pl.ds(start, ATTN_K), :] += _dot_tn(a, do)
            dk_acc[pl.ds(start, ATTN_K), :] += _dot_tn(dzb, q)
            dq = dq + _dot(dzb, kb)
            run = run + jnp.sum(log_rest, axis=1, keepdims=True)
            seen = seen + jnp.sum(de, axis=1, keepdims=True)
            nblk, nbound = _attn_next(blk)
            return nblk, nbound, run, seen, dq, jnp.max(run)

        blk0, bound = _attn_sweep_start(i, t)
        zero = jnp.zeros((ATTN_Q, 1), F32)
        init = (blk0, bound, zero, zero, jnp.zeros((ATTN_Q, HEAD_DIM), F32), jnp.float32(0.0))
        dq_ref[...] = lax.while_loop(cond, step, init)[4].astype(BF)

        @pl.when(i == nq - 1)
        def _():
            dk_ref[...] = dk_acc[...].astype(BF)
            dv_ref[...] = dv_acc[...].astype(BF)

    qspec = pl.BlockSpec((1, ATTN_Q, HEAD_DIM), lambda h, i: (3, i, h))
    rowblk = pl.BlockSpec((ATTN_Q, HEAD_DIM), lambda h, i: (i, h))
    head = pl.BlockSpec((t, HEAD_DIM), lambda h, i: (0, h))
    trispec = pl.BlockSpec((ATTN_K, ATTN_K), lambda h, i: (0, 0))
    return pl.pallas_call(
        body, name="attn_bwd", grid=(N_HEADS, nq),
        in_specs=[qspec, _head_cols(4)(t), _head_cols(5)(t), rowblk, rowblk, trispec, trispec],
        out_specs=[rowblk, head, head],
        out_shape=[S((t, D_MODEL), BF)] * 3,
        scratch_shapes=[pltpu.VMEM((t, HEAD_DIM), F32), pltpu.VMEM((t, HEAD_DIM), F32)],
    )(proj, proj, proj, o, d_o, tri_strict, tri_incl)


def _tail(h3, n4, p, w_pg, w_pp, g_ple, g_final, target):
    t = h3.shape[0]
    tm = min(TOKEN_TILE, t)
    steps = t // tm

    def body(h_ref, n_ref, p_ref, wpg_ref, wpp_ref, gp_ref, gf_ref, tgt_ref,
             dh_ref, ds_ref, dpp_ref, loss_ref, dgf_ref, dgp_ref):
        pg = jax.nn.sigmoid(_dot(n_ref[...], wpg_ref[...]))
        pp = _dot(p_ref[...].astype(BF), wpp_ref[...])
        h3v = h_ref[...]
        h4 = h3v + pg * pp
        gf = gf_ref[...]
        diff = _rms(h4, gf) - tgt_ref[...]
        _accumulate(loss_ref, jnp.sum(diff * diff, axis=0, keepdims=True))
        dh4, dgf = _rms_bwd(diff * (1.0 / D_MODEL), h4, gf)
        _accumulate(dgf_ref, dgf)
        dpp_ref[...] = (dh4 * pg).astype(BF)
        ds = (dh4 * pp * pg * (1.0 - pg)).astype(BF)
        ds_ref[...] = ds
        dh3, dgp = _rms_bwd(_dot_nt(ds, wpg_ref[...]), h3v, gp_ref[...])
        _accumulate(dgp_ref, dgp)
        dh_ref[...] = dh4 + dh3

        @pl.when(pl.program_id(0) == steps - 1)
        def _():
            loss_ref[...] = jnp.full(loss_ref.shape, 0.5 / D_MODEL * jnp.sum(loss_ref[...]), F32)

    vec = (1, D_MODEL)
    return pl.pallas_call(
        body, name="tail", grid=(steps,),
        in_specs=[_rows(tm, D_MODEL), _rows(tm, D_MODEL), _rows(tm, PLE_DIM), _const_spec((D_MODEL, D_MODEL)),
                  _const_spec((PLE_DIM, D_MODEL)), _const_spec(vec), _const_spec(vec), _rows(tm, D_MODEL)],
        out_specs=[_rows(tm, D_MODEL)] * 3 + [_acc_spec(vec)] * 3,
        out_shape=[S((t, D_MODEL), F32), S((t, D_MODEL), BF), S((t, D_MODEL), BF)] + [S(vec, F32)] * 3,
    )(h3, n4, p, w_pg, w_pp, g_ple, g_final, target)


def _wgrad(xs, ys, name):
    bx, t, k = xs.shape
    by, _, n = ys.shape
    b = max(bx, by)
    tt = min(WGRAD_TILE, t)
    steps = t // tt

    def body(x_ref, y_ref, o_ref, acc_ref):
        s = pl.program_id(1)

        @pl.when(s == 0)
        def _():
            acc_ref[...] = jnp.zeros_like(acc_ref)
        acc_ref[...] += _dot_tn(x_ref[0].astype(BF), y_ref[0].astype(BF))

        @pl.when(s == steps - 1)
        def _():
            o_ref[0] = acc_ref[...].astype(BF)

    return pl.pallas_call(
        body, name=name, grid=(b, steps),
        in_specs=[pl.BlockSpec((1, tt, k), (lambda j, s: (j, s, 0)) if bx > 1 else (lambda j, s: (0, s, 0))),
                  pl.BlockSpec((1, tt, n), (lambda j, s: (j, s, 0)) if by > 1 else (lambda j, s: (0, s, 0)))],
        out_specs=pl.BlockSpec((1, k, n), lambda j, s: (j, 0, 0)),
        out_shape=S((b, k, n), BF),
        scratch_shapes=[pltpu.VMEM((k, n), F32)],
    )(xs, ys)


def _ffn_bwd(dh, h_in, g, gate, up, w_in, w_out, name):
    t = dh.shape[0]
    tm = min(TOKEN_TILE, t)

    def body(dh_ref, h_ref, g_ref, gate_ref, up_ref, win_ref, wout_ref, dhi_ref, df_ref, act_ref, dgate_ref, dup_ref, dg_ref):
        dhv = dh_ref[...]
        df = (0.5 * dhv).astype(BF)
        df_ref[...] = df
        dn = jnp.zeros((tm, D_MODEL), F32)
        for c in range(N_FF_CHUNKS):
            gt = gate_ref[c].astype(F32)
            u = up_ref[c].astype(F32)
            sg = jax.nn.sigmoid(gt)
            silu = gt * sg
            act_ref[c] = (silu * u).astype(BF)
            dact = _dot_nt(df, wout_ref[c])
            dgate = (dact * u * (sg * (1.0 + gt * (1.0 - sg)))).astype(BF)
            dup = (dact * silu).astype(BF)
            dgate_ref[c] = dgate
            dup_ref[c] = dup
            dn = dn + _dot_nt(dgate, win_ref[c]) + _dot_nt(dup, win_ref[N_FF_CHUNKS + c])
        dhi, dg = _rms_bwd(dn, h_ref[...], g_ref[...])
        _accumulate(dg_ref, dg)
        dhi_ref[...] = dhv + dhi

    vec = (1, D_MODEL)
    return pl.pallas_call(
        body, name=name, grid=(t // tm,),
        in_specs=[_rows(tm, D_MODEL), _rows(tm, D_MODEL), _const_spec(vec), _chunks(tm), _chunks(tm),
                  _const_spec(w_in.shape), _const_spec(w_out.shape)],
        out_specs=[_rows(tm, D_MODEL), _rows(tm, D_MODEL), _chunks(tm), _chunks(tm), _chunks(tm), _acc_spec(vec)],
        out_shape=[S((t, D_MODEL), F32), S((t, D_MODEL), BF)] + [S((N_FF_CHUNKS, t, FF_CHUNK), BF)] * 3 + [S(vec, F32)],
    )(dh, h_in, g, gate, up, w_in, w_out)


def _mixer_bwd(dh2, proj, yc, ya, conv_w, w_co, w_ao, w_mo):
    t = dh2.shape[0]
    tm = min(TOKEN_TILE, t)

    def body(dh_ref, cb_ref, cc_ref, cx_ref, gc_ref, ga_ref, cch_ref, cxh_ref, yc_ref, ya_ref, cw_ref, wco_ref, wao_ref, wmo_ref,
             dhb_ref, dyc_ref, dya_ref, dgc_ref, dga_ref, dcb_ref, dcv_ref, do_ref):
        dhb = dh_ref[...].astype(BF)
        dhb_ref[...] = dhb
        dmerged = _dot_nt(dhb, wmo_ref[...])
        sc = jax.nn.sigmoid(gc_ref[0].astype(F32))
        sa = jax.nn.sigmoid(ga_ref[0].astype(F32))
        dyc = (dmerged * sc).astype(BF)
        dya = (dmerged * sa).astype(BF)
        dyc_ref[...] = dyc
        dya_ref[...] = dya
        dgc_ref[...] = (dmerged * yc_ref[...].astype(F32) * sc * (1.0 - sc)).astype(BF)
        dga_ref[...] = (dmerged * ya_ref[...].astype(F32) * sa * (1.0 - sa)).astype(BF)
        m, m1, m2 = _conv_inputs(cc_ref, cx_ref, cch_ref, cxh_ref)
        cw = cw_ref[...]
        cv = cw[0:1, :] * m2 + cw[1:2, :] * m1 + cw[2:3, :] * m
        dycin = _dot_nt(dyc, wco_ref[...])
        dcb_ref[...] = (dycin * cv).astype(BF)
        dcv_ref[...] = dycin * cb_ref[0].astype(F32)
        do_ref[...] = _dot_nt(dya, wao_ref[...]).astype(BF)

    sq = (D_MODEL, D_MODEL)
    return pl.pallas_call(
        body, name="mixer_bwd", grid=(t // tm,),
        in_specs=[_rows(tm, D_MODEL), _piece(0, tm), _piece(1, tm), _piece(2, tm), _piece(6, tm), _piece(7, tm),
                  _prev_halo(1, tm), _prev_halo(2, tm), _rows(tm, D_MODEL), _rows(tm, D_MODEL),
                  _const_spec((3, D_MODEL)), _const_spec(sq), _const_spec(sq), _const_spec(sq)],
        out_specs=[_rows(tm, D_MODEL)] * 8,
        out_shape=[S((t, D_MODEL), BF)] * 6 + [S((t, D_MODEL), F32), S((t, D_MODEL), BF)],
    )(dh2, proj, proj, proj, proj, proj, proj, proj, yc, ya, conv_w, w_co, w_ao, w_mo)


F32_HALO = 8


def _conv_bwd(dcv, proj, conv_w):
    t = dcv.shape[0]
    tm = min(TOKEN_TILE, t)
    steps = t // tm

    def body(dcv_ref, nxt_ref, cc_ref, cx_ref, cch_ref, cxh_ref, cw_ref, dcc_ref, dcx_ref, dw_ref):
        i = pl.program_id(0)
        m, m1, m2 = _conv_inputs(cc_ref, cx_ref, cch_ref, cxh_ref)
        d0 = dcv_ref[...]
        nxt = jnp.where(i == steps - 1, 0.0, nxt_ref[...])
        row = lax.broadcasted_iota(jnp.int32, (tm, 1), 0)
        d1 = jnp.where(row == tm - 1, nxt[0:1, :], pltpu.roll(d0, tm - 1, 0))
        d2 = pltpu.roll(d0, tm - 2, 0)
        d2 = jnp.where(row == tm - 2, nxt[0:1, :], jnp.where(row == tm - 1, nxt[1:2, :], d2))
        cw = cw_ref[...]
        dm = cw[2:3, :] * d0 + cw[1:2, :] * d1 + cw[0:1, :] * d2
        dcc_ref[...] = (dm * cx_ref[0].astype(F32)).astype(BF)
        dcx_ref[...] = (dm * cc_ref[0].astype(F32)).astype(BF)
        tap_row = lax.broadcasted_iota(jnp.int32, (F32_HALO, 1), 0)
        dw = jnp.zeros((F32_HALO, D_MODEL), F32)
        for j, mk in enumerate((m2, m1, m)):
            dw = jnp.where(tap_row == j, jnp.sum(d0 * mk, axis=0, keepdims=True), dw)
        _accumulate(dw_ref, dw)

    nxt_spec = pl.BlockSpec((F32_HALO, D_MODEL), lambda i: (jnp.minimum((i + 1) * (tm // F32_HALO), t // F32_HALO - 1), 0))
    return pl.pallas_call(
        body, name="conv_bwd", grid=(steps,),
        in_specs=[_rows(tm, D_MODEL), nxt_spec, _piece(1, tm), _piece(2, tm), _prev_halo(1, tm), _prev_halo(2, tm),
                  _const_spec((3, D_MODEL))],
        out_specs=[_rows(tm, D_MODEL), _rows(tm, D_MODEL), _acc_spec((F32_HALO, D_MODEL))],
        out_shape=[S((t, D_MODEL), BF), S((t, D_MODEL), BF), S((F32_HALO, D_MODEL), F32)],
    )(dcv, dcv, proj, proj, proj, proj, conv_w)


def _mix_bwd(dpieces, w_mix, h1, dh2, g):
    t = h1.shape[0]
    tm = min(TOKEN_TILE, t)

    def body(*refs):
        pieces, (w_ref, h_ref, dh_ref, g_ref, dhi_ref, dg_ref) = refs[:N_MIX], refs[N_MIX:]
        du = jnp.zeros((tm, D_MODEL), F32)
        for d in range(N_MIX):
            du = du + _dot_nt(pieces[d][...], w_ref[d])
        dhi, dg = _rms_bwd(du, h_ref[...], g_ref[...])
        _accumulate(dg_ref, dg)
        dhi_ref[...] = dh_ref[...] + dhi

    vec = (1, D_MODEL)
    return pl.pallas_call(
        body, name="mix_bwd", grid=(t // tm,),
        in_specs=[_rows(tm, D_MODEL)] * N_MIX + [_const_spec(w_mix.shape), _rows(tm, D_MODEL), _rows(tm, D_MODEL), _const_spec(vec)],
        out_specs=[_rows(tm, D_MODEL), _acc_spec(vec)],
        out_shape=[S((t, D_MODEL), F32), S(vec, F32)],
    )(*dpieces, w_mix, h1, dh2, g)


def _adamw(partials, w, m, v, name):
    r, c = w.shape
    tr = min(r, 512)
    c1 = 1.0 - ADAM_B1 ** ADAM_STEP
    c2 = 1.0 - ADAM_B2 ** ADAM_STEP

    def body(p_ref, w_ref, m_ref, v_ref, g_ref, d_ref, mo_ref, vo_ref):
        g = p_ref[0].astype(F32)
        for s in range(1, N_SHARDS):
            g = g + p_ref[s].astype(F32)
        mn = ADAM_B1 * m_ref[...] + (1.0 - ADAM_B1) * g
        vn = ADAM_B2 * v_ref[...] + (1.0 - ADAM_B2) * (g * g)
        g_ref[...] = g
        mo_ref[...] = mn
        vo_ref[...] = vn
        d_ref[...] = -ADAM_LR * ((mn / c1) / (jnp.sqrt(vn / c2) + ADAM_EPS) + ADAM_WD * w_ref[...])

    blk = pl.BlockSpec((tr, c), lambda i: (i, 0))
    return pl.pallas_call(
        body, name=name, grid=(r // tr,),
        in_specs=[pl.BlockSpec((N_SHARDS, tr, c), lambda i: (0, i, 0)), blk, blk, blk],
        out_specs=[blk] * 4, out_shape=[S((r, c), F32)] * 4,
    )(partials, w, m, v)


_MATRICES = ("ffn1_w_in", "ffn1_w_out", "w_mix_in", "conv_w", "w_conv_out", "w_attn_out", "w_mix_out",
             "ffn2_w_in", "ffn2_w_out", "w_ple_gate", "w_ple_proj")
_GAINS = ("ffn1_norm", "mix_norm", "ffn2_norm", "ple_norm", "final_norm")
_WEIGHTS = ("ffn1_norm", "ffn1_w_in", "ffn1_w_out", "mix_norm", "w_mix_in", "conv_w", "w_conv_out", "w_attn_out", "w_mix_out",
            "ffn2_norm", "ffn2_w_in", "ffn2_w_out", "ple_norm", "w_ple_gate", "w_ple_proj", "final_norm")
CONV_ROWS = 8


def _columns_from_shards(g):
    return jnp.transpose(g, (1, 0, 2)).reshape(g.shape[1], N_SHARDS * g.shape[2])


def _shards_from_columns(a):
    r, c = a.shape
    return jnp.transpose(a.reshape(r, N_SHARDS, c // N_SHARDS), (1, 0, 2))


def kernel(x, p, ffn1_norm, ffn1_w_in, ffn1_w_out, mix_norm, w_mix_in, conv_w, w_conv_out, w_attn_out, w_mix_out, ffn2_norm, ffn2_w_in, ffn2_w_out, ple_norm, w_ple_gate, w_ple_proj, final_norm, loss_target, m_ffn1_norm, m_ffn1_w_in, m_ffn1_w_out, m_mix_norm, m_w_mix_in, m_conv_w, m_w_conv_out, m_w_attn_out, m_w_mix_out, m_ffn2_norm, m_ffn2_w_in, m_ffn2_w_out, m_ple_norm, m_w_ple_gate, m_w_ple_proj, m_final_norm, v_ffn1_norm, v_ffn1_w_in, v_ffn1_w_out, v_mix_norm, v_w_mix_in, v_conv_w, v_w_conv_out, v_w_attn_out, v_w_mix_out, v_ffn2_norm, v_ffn2_w_in, v_ffn2_w_out, v_ple_norm, v_w_ple_gate, v_w_ple_proj, v_final_norm):
    given = dict(locals())
    t = x.shape[1]
    xs = x.reshape(t, D_MODEL)
    ps = p.reshape(t, PLE_DIM)
    target = loss_target.reshape(t, D_MODEL)
    shard = {k: given[k].reshape(given[k].shape[-2:]) for k in _MATRICES}
    gain = {k: given[k].reshape(1, D_MODEL) for k in _GAINS}

    send = [shard[k].astype(BF) for k in _MATRICES]
    send[_MATRICES.index("conv_w")] = jnp.pad(shard["conv_w"], ((0, CONV_ROWS - 3), (0, 0)))
    full = dict(zip(_MATRICES, _all_gather(send, "gather_weights")))

    loss_vec, dx, grads, gain_grads = _forward_backward(xs, ps, target, gain, full)

    parts = [grads[k] for k in _MATRICES]
    landed = dict(zip(_MATRICES, _reduce_scatter_exchange(parts, "scatter_gradients")))
    gain_rows = jnp.concatenate([gain_grads[k] for k in _GAINS] + [jnp.zeros((8 - len(_GAINS), D_MODEL), F32)], axis=0)
    gain_parts, = _all_gather([gain_rows], "gather_gain_gradients")

    out = {}
    for k in _MATRICES:
        w, m, v = shard[k], given["m_" + k].reshape(shard[k].shape), given["v_" + k].reshape(shard[k].shape)
        part = landed[k]
        if k == "conv_w":
            pad = ((0, CONV_ROWS - 3), (0, 0))
            w, m, v = jnp.pad(w, pad), jnp.pad(m, pad), jnp.pad(v, pad, constant_values=1.0)
        res = _adamw(part, w, m, v, "adamw_" + k)
        out[k] = [r[:3] if k == "conv_w" else r for r in res]
    stack = lambda pre: jnp.concatenate([given[pre + k].reshape(1, D_MODEL) for k in _GAINS] + [jnp.ones((8 - len(_GAINS), D_MODEL), F32)], axis=0)
    res = _adamw(gain_parts, stack(""), stack("m_"), stack("v_"), "adamw_gains")
    for j, k in enumerate(_GAINS):
        out[k] = [r[j:j + 1] for r in res]

    loss = lax.psum(loss_vec[0, 0], ("x", "y", "c"))
    per_kind = [[out[k][j].reshape(given[k].shape) for k in _WEIGHTS] for j in range(4)]
    return (loss, dx.reshape(x.shape), *per_kind[0], *per_kind[1], *per_kind[2], *per_kind[3])


def _forward_backward(xs, ps, target, gain, full):
    w1_in, w2_in = full["ffn1_w_in"], full["ffn2_w_in"]
    w1_out = full["ffn1_w_out"].reshape(N_FF_CHUNKS, FF_CHUNK, D_MODEL)
    w2_out = full["ffn2_w_out"].reshape(N_FF_CHUNKS, FF_CHUNK, D_MODEL)
    w_mix = full["w_mix_in"]
    w_co, w_ao, w_mo, w_pg = (full[k].reshape(D_MODEL, D_MODEL) for k in ("w_conv_out", "w_attn_out", "w_mix_out", "w_ple_gate"))
    taps = _columns_from_shards(full["conv_w"][:, :3, :])
    w_pp = _columns_from_shards(full["w_ple_proj"])

    n1 = _prenorm(xs, gain["ffn1_norm"])
    h1, u, gate1, up1 = _ffn_fwd(xs, n1, w1_in, w1_out, gain["mix_norm"], "ffn1_fwd")
    proj = _mix_proj(u, w_mix)
    o = _attn_fwd(proj)
    h2, n3, ycin, yc, ya, merged = _mixer_out(proj, o, h1, taps, w_co, w_ao, w_mo, gain["ffn2_norm"])
    h3, n4, gate2, up2 = _ffn_fwd(h2, n3, w2_in, w2_out, gain["ple_norm"], "ffn2_fwd")
    dh3, ds, dpp, loss_vec, dg_final, dg_ple = _tail(h3, n4, ps, w_pg, w_pp, gain["ple_norm"], gain["final_norm"], target)

    one = lambda a: a[None]
    grads = {}
    grads["w_ple_gate"] = _wgrad(one(n4), one(ds), "wgrad_ple_gate").reshape(N_SHARDS, D_MODEL // N_SHARDS, D_MODEL)
    grads["w_ple_proj"] = _shards_from_columns(_wgrad(one(ps), one(dpp), "wgrad_ple_proj")[0])
    dh2, df2, act2, dgate2, dup2, dg_ffn2 = _ffn_bwd(dh3, h2, gain["ffn2_norm"], gate2, up2, w2_in, w2_out, "ffn2_bwd")
    grads["ffn2_w_out"] = _wgrad(act2, one(df2), "wgrad_ffn2_out").reshape(N_SHARDS, D_FF // N_SHARDS, D_MODEL)
    grads["ffn2_w_in"] = jnp.concatenate([_wgrad(one(n3), dgate2, "wgrad_ffn2_gate"), _wgrad(one(n3), dup2, "wgrad_ffn2_up")], axis=0)
    dh2b, dyc, dya, dgc, dga, dcb, dcv, d_o = _mixer_bwd(dh2, proj, yc, ya, taps, w_co, w_ao, w_mo)
    grads["w_mix_out"] = _wgrad(one(merged), one(dh2b), "wgrad_mix_out").reshape(N_SHARDS, D_MODEL // N_SHARDS, D_MODEL)
    grads["w_conv_out"] = _wgrad(one(ycin), one(dyc), "wgrad_conv_out").reshape(N_SHARDS, D_MODEL // N_SHARDS, D_MODEL)
    grads["w_attn_out"] = _wgrad(one(o), one(dya), "wgrad_attn_out").reshape(N_SHARDS, D_MODEL // N_SHARDS, D_MODEL)
    dcc, dcx, dtaps = _conv_bwd(dcv, proj, taps)
    grads["conv_w"] = jnp.pad(_shards_from_columns(dtaps[:3]), ((0, 0), (0, CONV_ROWS - 3), (0, 0)))
    dq, dk, dv = _attn_bwd(proj, o, d_o)
    dpieces = [dcb, dcc, dcx, dq, dk, dv, dgc, dga]
    grads["w_mix_in"] = jnp.concatenate([_wgrad(one(u), one(dp), f"wgrad_mix_in_{d}") for d, dp in enumerate(dpieces)], axis=0)
    dh1, dg_mix = _mix_bwd(dpieces, w_mix, h1, dh2, gain["mix_norm"])
    dx, df1, act1, dgate1, dup1, dg_ffn1 = _ffn_bwd(dh1, xs, gain["ffn1_norm"], gate1, up1, w1_in, w1_out, "ffn1_bwd")
    grads["ffn1_w_out"] = _wgrad(act1, one(df1), "wgrad_ffn1_out").reshape(N_SHARDS, D_FF // N_SHARDS, D_MODEL)
    grads["ffn1_w_in"] = jnp.concatenate([_wgrad(one(n1), dgate1, "wgrad_ffn1_gate"), _wgrad(one(n1), dup1, "wgrad_ffn1_up")], axis=0)
    gain_grads = dict(ffn1_norm=dg_ffn1, mix_norm=dg_mix, ffn2_norm=dg_ffn2, ple_norm=dg_ple, final_norm=dg_final)
    return loss_vec, dx, grads, gain_grads
```

```python
import functools
import math

import jax
import jax.numpy as jnp
from jax import lax
from jax.experimental import pallas as pl
from jax.experimental.pallas import tpu as pltpu

D_MODEL = 1024
D_FF = 2816
N_SHARDS = 8
FF_CHUNK = 2 * D_FF // N_SHARDS
N_FF_CHUNKS = D_FF // FF_CHUNK
N_HEADS = 8
HEAD_DIM = 128
PLE_DIM = 256
NORM_EPS = 1e-6
N_MIX = 8
ADAM_LR, ADAM_B1, ADAM_B2, ADAM_EPS, ADAM_WD, ADAM_STEP = 0.001, 0.9, 0.999, 1e-08, 0.01, 10

TOKEN_TILE = 256
WGRAD_TILE = 1024
ATTN_Q = 256
ATTN_K = 2 * ATTN_Q
ATTN_SKIP_BELOW = -90.0

BF = jnp.bfloat16
F32 = jnp.float32
MESH = pl.DeviceIdType.MESH
NT = (((1,), (1,)), ((), ()))
TN = (((0,), (0,)), ((), ()))
S = jax.ShapeDtypeStruct
ANY = pl.BlockSpec(memory_space=pl.ANY)


def _const_spec(shape):
    nd = len(shape)
    return pl.BlockSpec(shape, lambda *_: (0,) * nd, pipeline_mode=pl.Buffered(1))


def _rows(tm, cols):
    return pl.BlockSpec((tm, cols), lambda i: (i, 0))


def _chunks(tm):
    return pl.BlockSpec((N_FF_CHUNKS, tm, FF_CHUNK), lambda i: (0, i, 0))


def _acc_spec(shape):
    nd = len(shape)
    return pl.BlockSpec(shape, lambda *_: (0,) * nd)


def _dot(a, b):
    return jnp.dot(a, b, preferred_element_type=F32)


def _dot_nt(a, b):
    return lax.dot_general(a, b, NT, preferred_element_type=F32)


def _dot_tn(a, b):
    return lax.dot_general(a, b, TN, preferred_element_type=F32)


def _rms(h, g):
    r = lax.rsqrt(jnp.mean(h * h, axis=-1, keepdims=True) + NORM_EPS)
    return h * r * g


def _rms_bwd(dn, h, g):
    r = lax.rsqrt(jnp.mean(h * h, axis=-1, keepdims=True) + NORM_EPS)
    nh = h * r
    gd = dn * g
    dh = r * (gd - nh * jnp.mean(gd * nh, axis=-1, keepdims=True))
    return dh, jnp.sum(dn * nh, axis=0, keepdims=True)


def _accumulate(ref, val):
    @pl.when(pl.program_id(0) == 0)
    def _():
        ref[...] = jnp.zeros_like(ref)
    ref[...] += val


def _place():
    x, y, c = lax.axis_index("x"), lax.axis_index("y"), lax.axis_index("c")
    return x, y, c


def _slot(px, py, pc):
    return 4 * px + 2 * py + pc


def _gather_phases(ins, outs, send_sems, recv_sems, local_sems):
    n = len(ins)

    def parties():
        x, y, c = _place()
        return (x, y, c), (x, y, 1 - c), [(1 - x, y), (x, 1 - y), (1 - x, 1 - y)], c

    def copy(a, k, block, to, src=None):
        dst = outs[a].at[_slot(*block)]
        return pltpu.make_async_remote_copy(
            src_ref=dst if src is None else src, dst_ref=dst,
            send_sem=send_sems.at[a, k], recv_sem=recv_sems.at[a, k],
            device_id=to, device_id_type=MESH)

    def own(a, me):
        return pltpu.make_async_copy(ins[a], outs[a].at[_slot(*me)], local_sems.at[a])

    def first(a, me, sibling, chips, c):
        return [copy(a, 0, me, sibling, src=ins[a])] + [copy(a, 1 + j, me, (*chip, c), src=ins[a]) for j, chip in enumerate(chips)]

    def start():
        me, sibling, chips, c = parties()
        for a in range(n):
            own(a, me).start()
        for a in range(n):
            for cp in first(a, me, sibling, chips, c):
                cp.start()

    def forward():
        me, sibling, chips, c = parties()
        for j, chip in enumerate(chips):
            for a in range(n):
                copy(a, 1 + j, (*chip, c), me).wait_recv()
                copy(a, 4 + j, (*chip, c), sibling).start()

    def finish():
        me, sibling, chips, c = parties()
        for a in range(n):
            copy(a, 0, sibling, me).wait_recv()
            for j, chip in enumerate(chips):
                copy(a, 4 + j, (*chip, 1 - c), me).wait_recv()
        for a in range(n):
            for cp in first(a, me, sibling, chips, c) + [copy(a, 4 + j, (*chip, c), sibling) for j, chip in enumerate(chips)]:
                cp.wait_send()
            own(a, me).wait()

    return [start, forward, finish]


def _scatter_phases(ins, outs, send_sems, recv_sems, local_sems):
    n = len(ins)

    def copies():
        x, y, c = _place()
        me = _slot(x, y, c)
        out = [pltpu.make_async_copy(ins[a].at[me], outs[a].at[me], local_sems.at[a]) for a in range(n)]
        for k in range(1, N_SHARDS):
            px = 1 - x if k & 4 else x
            py = 1 - y if k & 2 else y
            pc = 1 - c if k & 1 else c
            for a in range(n):
                out.append(pltpu.make_async_remote_copy(
                    src_ref=ins[a].at[_slot(px, py, pc)], dst_ref=outs[a].at[me],
                    send_sem=send_sems.at[a, k - 1], recv_sem=recv_sems.at[a, k - 1],
                    device_id=(px, py, pc), device_id_type=MESH))
        return out

    def start():
        for cp in copies():
            cp.start()

    def finish():
        for cp in copies():
            cp.wait()

    return [start, finish]


def _pallas(body, *, name, grid, in_specs, out_specs, out_shape, args, scratch_shapes=(), ride=None):
    if ride is None:
        outs = pl.pallas_call(body, name=name, grid=grid, in_specs=in_specs, out_specs=out_specs, out_shape=out_shape,
                              scratch_shapes=list(scratch_shapes))(*args)
        return list(outs), []
    kind, arrays = ride
    n, n_in, n_out, n_scr = len(arrays), len(in_specs), len(out_specs), len(scratch_shapes)
    total = math.prod(grid)
    middle = (2 * total) // 3
    landed_shape = [S((N_SHARDS,) + a.shape if kind == "gather" else a.shape, a.dtype) for a in arrays]

    def with_exchange(*refs):
        ins, riders_in = refs[:n_in], refs[n_in:n_in + n]
        outs, riders_out = refs[n_in + n:n_in + n + n_out], refs[n_in + n + n_out:n_in + 2 * n + n_out]
        scratch, sems = refs[n_in + 2 * n + n_out:n_in + 2 * n + n_out + n_scr], refs[n_in + 2 * n + n_out + n_scr:]
        step = 0
        for axis, size in enumerate(grid):
            step = step * size + pl.program_id(axis)
        phases = (_gather_phases if kind == "gather" else _scatter_phases)(riders_in, riders_out, *sems)
        pl.when(step == 0)(phases[0])
        body(*ins, *outs, *scratch)
        for phase in phases[1:-1]:
            pl.when(step == middle)(phase)
        pl.when(step == total - 1)(phases[-1])

    outs = pl.pallas_call(
        with_exchange, name=name, grid=grid,
        in_specs=list(in_specs) + [ANY] * n, out_specs=list(out_specs) + [ANY] * n,
        out_shape=list(out_shape) + landed_shape,
        scratch_shapes=list(scratch_shapes) + [pltpu.SemaphoreType.DMA((n, 7)), pltpu.SemaphoreType.DMA((n, 7)),
                                               pltpu.SemaphoreType.DMA((n,))],
    )(*args, *arrays)
    return list(outs[:n_out]), list(outs[n_out:])


def _exchange_alone(kind, arrays, name):
    return _pallas(lambda: None, name=name, grid=(1,), in_specs=[], out_specs=[], out_shape=[], args=[], ride=(kind, arrays))[1]


def _prenorm(x, g):
    t = x.shape[0]
    tm = min(TOKEN_TILE, t)

    def body(x_ref, g_ref, n_ref):
        n_ref[...] = _rms(x_ref[...], g_ref[...]).astype(BF)

    return pl.pallas_call(
        body, name="prenorm", grid=(t // tm,),
        in_specs=[_rows(tm, D_MODEL), _const_spec((1, D_MODEL))], out_specs=_rows(tm, D_MODEL),
        out_shape=S((t, D_MODEL), BF))(x, g)


def _ffn_fwd(h, n, w_in, w_out, g_next, name, ride=None):
    t = h.shape[0]
    tm = min(TOKEN_TILE, t)

    def body(h_ref, n_ref, win_ref, wout_ref, g_ref, ho_ref, no_ref, gate_ref, up_ref):
        nb = n_ref[...]
        acc = jnp.zeros((tm, D_MODEL), F32)
        for c in range(N_FF_CHUNKS):
            gate = _dot(nb, win_ref[c])
            up = _dot(nb, win_ref[N_FF_CHUNKS + c])
            gate_ref[c] = gate.astype(BF)
            up_ref[c] = up.astype(BF)
            act = (gate * jax.nn.sigmoid(gate) * up).astype(BF)
            acc = acc + _dot(act, wout_ref[c])
        ho = h_ref[...] + 0.5 * acc
        ho_ref[...] = ho
        no_ref[...] = _rms(ho, g_ref[...]).astype(BF)

    return _pallas(
        body, name=name, grid=(t // tm,), ride=ride,
        in_specs=[_rows(tm, D_MODEL), _rows(tm, D_MODEL), _const_spec(w_in.shape), _const_spec(w_out.shape),
                  _const_spec((1, D_MODEL))],
        out_specs=[_rows(tm, D_MODEL), _rows(tm, D_MODEL), _chunks(tm), _chunks(tm)],
        out_shape=[S((t, D_MODEL), F32), S((t, D_MODEL), BF),
                   S((N_FF_CHUNKS, t, FF_CHUNK), BF), S((N_FF_CHUNKS, t, FF_CHUNK), BF)],
        args=[h, n, w_in, w_out, g_next])


def _mix_proj(u, w_mix, ride=None):
    t = u.shape[0]
    tm = min(1024, t)

    def body(u_ref, w_ref, o_ref):
        o_ref[0] = _dot(u_ref[...], w_ref[0]).astype(BF)

    return _pallas(
        body, name="mix_proj", grid=(N_MIX, t // tm), ride=ride,
        in_specs=[pl.BlockSpec((tm, D_MODEL), lambda d, i: (i, 0)), pl.BlockSpec((1, D_MODEL, D_MODEL), lambda d, i: (d, 0, 0))],
        out_specs=[pl.BlockSpec((1, tm, D_MODEL), lambda d, i: (d, i, 0))],
        out_shape=[S((N_MIX, t, D_MODEL), BF)], args=[u, w_mix])


HALO = 16


def _piece(d, tm):
    return pl.BlockSpec((1, tm, D_MODEL), lambda i: (d, i, 0))


def _prev_halo(d, tm):
    return pl.BlockSpec((1, HALO, D_MODEL), lambda i: (d, jnp.maximum(i * (tm // HALO) - 1, 0), 0))


def _shift_down(m, prev_tail, k):
    tm = m.shape[0]
    out = pltpu.roll(m, k, 0)
    row = lax.broadcasted_iota(jnp.int32, (tm, 1), 0)
    for j in range(k):
        out = jnp.where(row == j, prev_tail[HALO - k + j:HALO - k + j + 1, :], out)
    return out


def _conv_inputs(cc_ref, cx_ref, cch_ref, cxh_ref):
    m = cc_ref[0].astype(F32) * cx_ref[0].astype(F32)
    mh = cch_ref[0].astype(F32) * cxh_ref[0].astype(F32)
    mh = jnp.where(pl.program_id(0) == 0, 0.0, mh)
    return m, _shift_down(m, mh, 1), _shift_down(m, mh, 2)


def _mixer_out(proj, o, h1, conv_w, w_co, w_ao, w_mo, g_next):
    t = h1.shape[0]
    tm = min(TOKEN_TILE, t)

    def body(cb_ref, cc_ref, cx_ref, gc_ref, ga_ref, cch_ref, cxh_ref, o_ref, h_ref, cw_ref, wco_ref, wao_ref, wmo_ref,
             g_ref, ho_ref, no_ref, ycin_ref, yc_ref, ya_ref, mg_ref):
        m, m1, m2 = _conv_inputs(cc_ref, cx_ref, cch_ref, cxh_ref)
        cw = cw_ref[...]
        cv = cw[0:1, :] * m2 + cw[1:2, :] * m1 + cw[2:3, :] * m
        ycin = (cb_ref[0].astype(F32) * cv).astype(BF)
        ycin_ref[...] = ycin
        yc = _dot(ycin, wco_ref[...])
        ya = _dot(o_ref[...].astype(BF), wao_ref[...])
        yc_ref[...] = yc.astype(BF)
        ya_ref[...] = ya.astype(BF)
        merged = (jax.nn.sigmoid(gc_ref[0].astype(F32)) * yc + jax.nn.sigmoid(ga_ref[0].astype(F32)) * ya).astype(BF)
        mg_ref[...] = merged
        ho = h_ref[...] + _dot(merged, wmo_ref[...])
        ho_ref[...] = ho
        no_ref[...] = _rms(ho, g_ref[...]).astype(BF)

    sq = (D_MODEL, D_MODEL)
    return pl.pallas_call(
        body, name="mixer_out", grid=(t // tm,),
        in_specs=[_piece(0, tm), _piece(1, tm), _piece(2, tm), _piece(6, tm), _piece(7, tm), _prev_halo(1, tm), _prev_halo(2, tm),
                  _rows(tm, D_MODEL), _rows(tm, D_MODEL), _const_spec((3, D_MODEL)), _const_spec(sq), _const_spec(sq),
                  _const_spec(sq), _const_spec((1, D_MODEL))],
        out_specs=[_rows(tm, D_MODEL)] * 6,
        out_shape=[S((t, D_MODEL), F32)] + [S((t, D_MODEL), BF)] * 5,
    )(proj, proj, proj, proj, proj, proj, proj, o, h1, conv_w, w_co, w_ao, w_mo, g_next)


def _suffix_sum(vals, tri):
    hi = vals.astype(BF)
    lo = (vals - hi.astype(F32)).astype(BF)
    return _dot(hi, tri) + _dot(lo, tri)


def _attn_step(q, kb, start, bound, row):
    z = _dot_nt(q, kb) * (1.0 / math.sqrt(HEAD_DIM))
    col = start + lax.broadcasted_iota(jnp.int32, (1, ATTN_K), 1)
    mask = jnp.logical_and(col < row, col < bound)
    soft = jnp.log1p(jnp.exp(-jnp.abs(z)))
    log_beta = jnp.minimum(z, 0.0) - soft
    log_rest = jnp.where(mask, jnp.minimum(-z, 0.0) - soft, 0.0)
    return z, mask, log_beta, log_rest


def _attn_sweep_start(i, t):
    return jnp.maximum(i + 1 - ATTN_K // ATTN_Q, 0), jnp.int32(t)


def _attn_keys(blk):
    return pl.multiple_of(blk * ATTN_Q, ATTN_Q)


def _attn_next(blk):
    return jnp.maximum(blk - ATTN_K // ATTN_Q, 0), blk * ATTN_Q


def _tri(strict):
    r = lax.broadcasted_iota(jnp.int32, (ATTN_K, ATTN_K), 0)
    c = lax.broadcasted_iota(jnp.int32, (ATTN_K, ATTN_K), 1)
    return (r > c if strict else r >= c).astype(BF)


def _head_cols(piece):
    return lambda t: pl.BlockSpec((1, t, HEAD_DIM), lambda h, i: (piece, 0, h))


def _attn_fwd(proj):
    t = proj.shape[1]
    nq = t // ATTN_Q
    tri = _tri(strict=True)

    def body(q_ref, k_ref, v_ref, tri_ref, o_ref):
        i = pl.program_id(1)
        q = q_ref[0]
        row = i * ATTN_Q + lax.broadcasted_iota(jnp.int32, (ATTN_Q, 1), 0)

        def cond(carry):
            _, bound, _, _, r_max = carry
            return jnp.logical_and(bound > 0, r_max > ATTN_SKIP_BELOW)

        def step(carry):
            blk, bound, run, acc, _ = carry
            start = _attn_keys(blk)
            kb = k_ref[0, pl.ds(start, ATTN_K), :]
            vb = v_ref[0, pl.ds(start, ATTN_K), :]
            _, mask, log_beta, log_rest = _attn_step(q, kb, start, bound, row)
            tail = run + _suffix_sum(log_rest, tri_ref[...])
            a = jnp.where(mask, jnp.exp(log_beta + tail), 0.0)
            acc = acc + _dot(a.astype(BF), vb)
            run = run + jnp.sum(log_rest, axis=1, keepdims=True)
            nblk, nbound = _attn_next(blk)
            return nblk, nbound, run, acc, jnp.max(run)

        blk0, bound = _attn_sweep_start(i, t)
        init = (blk0, bound, jnp.zeros((ATTN_Q, 1), F32), jnp.zeros((ATTN_Q, HEAD_DIM), F32), jnp.float32(0.0))
        o_ref[...] = lax.while_loop(cond, step, init)[3]

    qspec = pl.BlockSpec((1, ATTN_Q, HEAD_DIM), lambda h, i: (3, i, h))
    return pl.pallas_call(
        body, name="attn_fwd", grid=(N_HEADS, nq),
        in_specs=[qspec, _head_cols(4)(t), _head_cols(5)(t), pl.BlockSpec((ATTN_K, ATTN_K), lambda h, i: (0, 0))],
        out_specs=pl.BlockSpec((ATTN_Q, HEAD_DIM), lambda h, i: (i, h)),
        out_shape=S((t, D_MODEL), F32))(proj, proj, proj, tri)


def _attn_bwd(proj, o, d_o, ride=None):
    t = proj.shape[1]
    nq = t // ATTN_Q
    tri_strict, tri_incl = _tri(strict=True), _tri(strict=False)

    def body(q_ref, k_ref, v_ref, o_ref, do_ref, tris_ref, trii_ref, dq_ref, dk_ref, dv_ref, dk_acc, dv_acc):
        i = pl.program_id(1)

        @pl.when(i == 0)
        def _():
            dk_acc[...] = jnp.zeros_like(dk_acc)
            dv_acc[...] = jnp.zeros_like(dv_acc)

        q = q_ref[0]
        do = do_ref[...]
        total = jnp.sum(do.astype(F32) * o_ref[...], axis=1, keepdims=True)
        row = i * ATTN_Q + lax.broadcasted_iota(jnp.int32, (ATTN_Q, 1), 0)

        def cond(carry):
            _, bound, _, _, _, r_max = carry
            return jnp.logical_and(bound > 0, r_max > ATTN_SKIP_BELOW)

        def step(carry):
            blk, bound, run, seen, dq, _ = carry
            start = _attn_keys(blk)
            kb = k_ref[0, pl.ds(start, ATTN_K), :]
            vb = v_ref[0, pl.ds(start, ATTN_K), :]
            z, mask, log_beta, log_rest = _attn_step(q, kb, start, bound, row)
            tail = run + _suffix_sum(log_rest, tris_ref[...])
            a = jnp.where(mask, jnp.exp(log_beta + tail), 0.0).astype(BF)
            de = _dot_nt(do, vb) * a.astype(F32)
            left = total - seen - _suffix_sum(de, trii_ref[...])
            beta = jax.nn.sigmoid(z)
            dz = jnp.where(mask, de * (1.0 - beta) - left * beta, 0.0) * (1.0 / math.sqrt(HEAD_DIM))
            dzb = dz.astype(BF)
            dv_acc[pl.ds(start, ATTN_K), :] += _dot_tn(a, do)
            dk_acc[pl.ds(start, ATTN_K), :] += _dot_tn(dzb, q)
            dq = dq + _dot(dzb, kb)
            run = run + jnp.sum(log_rest, axis=1, keepdims=True)
            seen = seen + jnp.sum(de, axis=1, keepdims=True)
            nblk, nbound = _attn_next(blk)
            return nblk, nbound, run, seen, dq, jnp.max(run)

        blk0, bound = _attn_sweep_start(i, t)
        zero = jnp.zeros((ATTN_Q, 1), F32)
        init = (blk0, bound, zero, zero, jnp.zeros((ATTN_Q, HEAD_DIM), F32), jnp.float32(0.0))
        dq_ref[...] = lax.while_loop(cond, step, init)[4].astype(BF)

        @pl.when(i == nq - 1)
        def _():
            dk_ref[...] = dk_acc[...].astype(BF)
            dv_ref[...] = dv_acc[...].astype(BF)

    qspec = pl.BlockSpec((1, ATTN_Q, HEAD_DIM), lambda h, i: (3, i, h))
    rowblk = pl.BlockSpec((ATTN_Q, HEAD_DIM), lambda h, i: (i, h))
    head = pl.BlockSpec((t, HEAD_DIM), lambda h, i: (0, h))
    trispec = pl.BlockSpec((ATTN_K, ATTN_K), lambda h, i: (0, 0))
    return _pallas(
        body, name="attn_bwd", grid=(N_HEADS, nq), ride=ride,
        in_specs=[qspec, _head_cols(4)(t), _head_cols(5)(t), rowblk, rowblk, trispec, trispec],
        out_specs=[rowblk, head, head],
        out_shape=[S((t, D_MODEL), BF)] * 3,
        scratch_shapes=[pltpu.VMEM((t, HEAD_DIM), F32), pltpu.VMEM((t, HEAD_DIM), F32)],
        args=[proj, proj, proj, o, d_o, tri_strict, tri_incl])


def _tail(h3, n4, p, w_pg, w_pp, g_ple, g_final, target):
    t = h3.shape[0]
    tm = min(TOKEN_TILE, t)
    steps = t // tm

    def body(h_ref, n_ref, p_ref, wpg_ref, wpp_ref, gp_ref, gf_ref, tgt_ref,
             dh_ref, ds_ref, dpp_ref, loss_ref, dgf_ref, dgp_ref):
        pg = jax.nn.sigmoid(_dot(n_ref[...], wpg_ref[...]))
        pp = _dot(p_ref[...].astype(BF), wpp_ref[...])
        h3v = h_ref[...]
        h4 = h3v + pg * pp
        gf = gf_ref[...]
        diff = _rms(h4, gf) - tgt_ref[...]
        _accumulate(loss_ref, jnp.sum(diff * diff, axis=0, keepdims=True))
        dh4, dgf = _rms_bwd(diff * (1.0 / D_MODEL), h4, gf)
        _accumulate(dgf_ref, dgf)
        dpp_ref[...] = (dh4 * pg).astype(BF)
        ds = (dh4 * pp * pg * (1.0 - pg)).astype(BF)
        ds_ref[...] = ds
        dh3, dgp = _rms_bwd(_dot_nt(ds, wpg_ref[...]), h3v, gp_ref[...])
        _accumulate(dgp_ref, dgp)
        dh_ref[...] = dh4 + dh3

        @pl.when(pl.program_id(0) == steps - 1)
        def _():
            loss_ref[...] = jnp.full(loss_ref.shape, 0.5 / D_MODEL * jnp.sum(loss_ref[...]), F32)

    vec = (1, D_MODEL)
    return pl.pallas_call(
        body, name="tail", grid=(steps,),
        in_specs=[_rows(tm, D_MODEL), _rows(tm, D_MODEL), _rows(tm, PLE_DIM), _const_spec((D_MODEL, D_MODEL)),
                  _const_spec((PLE_DIM, D_MODEL)), _const_spec(vec), _const_spec(vec), _rows(tm, D_MODEL)],
        out_specs=[_rows(tm, D_MODEL)] * 3 + [_acc_spec(vec)] * 3,
        out_shape=[S((t, D_MODEL), F32), S((t, D_MODEL), BF), S((t, D_MODEL), BF)] + [S(vec, F32)] * 3,
    )(h3, n4, p, w_pg, w_pp, g_ple, g_final, target)


def _wgrad(xs, ys, name, ride=None):
    bx, t, k = xs.shape
    by, _, n = ys.shape
    b = max(bx, by)
    tt = min(WGRAD_TILE, t)
    steps = t // tt

    def body(x_ref, y_ref, o_ref, acc_ref):
        s = pl.program_id(1)

        @pl.when(s == 0)
        def _():
            acc_ref[...] = jnp.zeros_like(acc_ref)
        acc_ref[...] += _dot_tn(x_ref[0].astype(BF), y_ref[0].astype(BF))

        @pl.when(s == steps - 1)
        def _():
            o_ref[0] = acc_ref[...].astype(BF)

    (out,), landed = _pallas(
        body, name=name, grid=(b, steps), ride=ride,
        in_specs=[pl.BlockSpec((1, tt, k), (lambda j, s: (j, s, 0)) if bx > 1 else (lambda j, s: (0, s, 0))),
                  pl.BlockSpec((1, tt, n), (lambda j, s: (j, s, 0)) if by > 1 else (lambda j, s: (0, s, 0)))],
        out_specs=[pl.BlockSpec((1, k, n), lambda j, s: (j, 0, 0))],
        out_shape=[S((b, k, n), BF)],
        scratch_shapes=[pltpu.VMEM((k, n), F32)],
        args=[xs, ys])
    return (out, landed) if ride is not None else out


def _ffn_bwd(dh, h_in, g, gate, up, w_in, w_out, name, ride=None):
    t = dh.shape[0]
    tm = min(TOKEN_TILE, t)

    def body(dh_ref, h_ref, g_ref, gate_ref, up_ref, win_ref, wout_ref, dhi_ref, df_ref, act_ref, dgate_ref, dup_ref, dg_ref):
        dhv = dh_ref[...]
        df = (0.5 * dhv).astype(BF)
        df_ref[...] = df
        dn = jnp.zeros((tm, D_MODEL), F32)
        for c in range(N_FF_CHUNKS):
            gt = gate_ref[c].astype(F32)
            u = up_ref[c].astype(F32)
            sg = jax.nn.sigmoid(gt)
            silu = gt * sg
            act_ref[c] = (silu * u).astype(BF)
            dact = _dot_nt(df, wout_ref[c])
            dgate = (dact * u * (sg * (1.0 + gt * (1.0 - sg)))).astype(BF)
            dup = (dact * silu).astype(BF)
            dgate_ref[c] = dgate
            dup_ref[c] = dup
            dn = dn + _dot_nt(dgate, win_ref[c]) + _dot_nt(dup, win_ref[N_FF_CHUNKS + c])
        dhi, dg = _rms_bwd(dn, h_ref[...], g_ref[...])
        _accumulate(dg_ref, dg)
        dhi_ref[...] = dhv + dhi

    vec = (1, D_MODEL)
    return _pallas(
        body, name=name, grid=(t // tm,), ride=ride,
        in_specs=[_rows(tm, D_MODEL), _rows(tm, D_MODEL), _const_spec(vec), _chunks(tm), _chunks(tm),
                  _const_spec(w_in.shape), _const_spec(w_out.shape)],
        out_specs=[_rows(tm, D_MODEL), _rows(tm, D_MODEL), _chunks(tm), _chunks(tm), _chunks(tm), _acc_spec(vec)],
        out_shape=[S((t, D_MODEL), F32), S((t, D_MODEL), BF)] + [S((N_FF_CHUNKS, t, FF_CHUNK), BF)] * 3 + [S(vec, F32)],
        args=[dh, h_in, g, gate, up, w_in, w_out])


def _mixer_bwd(dh2, proj, yc, ya, conv_w, w_co, w_ao, w_mo, ride=None):
    t = dh2.shape[0]
    tm = min(TOKEN_TILE, t)

    def body(dh_ref, cb_ref, cc_ref, cx_ref, gc_ref, ga_ref, cch_ref, cxh_ref, yc_ref, ya_ref, cw_ref, wco_ref, wao_ref, wmo_ref,
             dhb_ref, dyc_ref, dya_ref, dgc_ref, dga_ref, dcb_ref, dcv_ref, do_ref):
        dhb = dh_ref[...].astype(BF)
        dhb_ref[...] = dhb
        dmerged = _dot_nt(dhb, wmo_ref[...])
        sc = jax.nn.sigmoid(gc_ref[0].astype(F32))
        sa = jax.nn.sigmoid(ga_ref[0].astype(F32))
        dyc = (dmerged * sc).astype(BF)
        dya = (dmerged * sa).astype(BF)
        dyc_ref[...] = dyc
        dya_ref[...] = dya
        dgc_ref[...] = (dmerged * yc_ref[...].astype(F32) * sc * (1.0 - sc)).astype(BF)
        dga_ref[...] = (dmerged * ya_ref[...].astype(F32) * sa * (1.0 - sa)).astype(BF)
        m, m1, m2 = _conv_inputs(cc_ref, cx_ref, cch_ref, cxh_ref)
        cw = cw_ref[...]
        cv = cw[0:1, :] * m2 + cw[1:2, :] * m1 + cw[2:3, :] * m
        dycin = _dot_nt(dyc, wco_ref[...])
        dcb_ref[...] = (dycin * cv).astype(BF)
        dcv_ref[...] = dycin * cb_ref[0].astype(F32)
        do_ref[...] = _dot_nt(dya, wao_ref[...]).astype(BF)

    sq = (D_MODEL, D_MODEL)
    return _pallas(
        body, name="mixer_bwd", grid=(t // tm,), ride=ride,
        in_specs=[_rows(tm, D_MODEL), _piece(0, tm), _piece(1, tm), _piece(2, tm), _piece(6, tm), _piece(7, tm),
                  _prev_halo(1, tm), _prev_halo(2, tm), _rows(tm, D_MODEL), _rows(tm, D_MODEL),
                  _const_spec((3, D_MODEL)), _const_spec(sq), _const_spec(sq), _const_spec(sq)],
        out_specs=[_rows(tm, D_MODEL)] * 8,
        out_shape=[S((t, D_MODEL), BF)] * 6 + [S((t, D_MODEL), F32), S((t, D_MODEL), BF)],
        args=[dh2, proj, proj, proj, proj, proj, proj, proj, yc, ya, conv_w, w_co, w_ao, w_mo])


F32_HALO = 8


def _conv_bwd(dcv, proj, conv_w):
    t = dcv.shape[0]
    tm = min(TOKEN_TILE, t)
    steps = t // tm

    def body(dcv_ref, nxt_ref, cc_ref, cx_ref, cch_ref, cxh_ref, cw_ref, dcc_ref, dcx_ref, dw_ref):
        i = pl.program_id(0)
        m, m1, m2 = _conv_inputs(cc_ref, cx_ref, cch_ref, cxh_ref)
        d0 = dcv_ref[...]
        nxt = jnp.where(i == steps - 1, 0.0, nxt_ref[...])
        row = lax.broadcasted_iota(jnp.int32, (tm, 1), 0)
        d1 = jnp.where(row == tm - 1, nxt[0:1, :], pltpu.roll(d0, tm - 1, 0))
        d2 = pltpu.roll(d0, tm - 2, 0)
        d2 = jnp.where(row == tm - 2, nxt[0:1, :], jnp.where(row == tm - 1, nxt[1:2, :], d2))
        cw = cw_ref[...]
        dm = cw[2:3, :] * d0 + cw[1:2, :] * d1 + cw[0:1, :] * d2
        dcc_ref[...] = (dm * cx_ref[0].astype(F32)).astype(BF)
        dcx_ref[...] = (dm * cc_ref[0].astype(F32)).astype(BF)
        tap_row = lax.broadcasted_iota(jnp.int32, (F32_HALO, 1), 0)
        dw = jnp.zeros((F32_HALO, D_MODEL), F32)
        for j, mk in enumerate((m2, m1, m)):
            dw = jnp.where(tap_row == j, jnp.sum(d0 * mk, axis=0, keepdims=True), dw)
        _accumulate(dw_ref, dw)

    nxt_spec = pl.BlockSpec((F32_HALO, D_MODEL), lambda i: (jnp.minimum((i + 1) * (tm // F32_HALO), t // F32_HALO - 1), 0))
    return pl.pallas_call(
        body, name="conv_bwd", grid=(steps,),
        in_specs=[_rows(tm, D_MODEL), nxt_spec, _piece(1, tm), _piece(2, tm), _prev_halo(1, tm), _prev_halo(2, tm),
                  _const_spec((3, D_MODEL))],
        out_specs=[_rows(tm, D_MODEL), _rows(tm, D_MODEL), _acc_spec((F32_HALO, D_MODEL))],
        out_shape=[S((t, D_MODEL), BF), S((t, D_MODEL), BF), S((F32_HALO, D_MODEL), F32)],
    )(dcv, dcv, proj, proj, proj, proj, conv_w)


def _mix_bwd(dpieces, w_mix, h1, dh2, g):
    t = h1.shape[0]
    tm = min(TOKEN_TILE, t)

    def body(*refs):
        pieces, (w_ref, h_ref, dh_ref, g_ref, dhi_ref, dg_ref) = refs[:N_MIX], refs[N_MIX:]
        du = jnp.zeros((tm, D_MODEL), F32)
        for d in range(N_MIX):
            du = du + _dot_nt(pieces[d][...], w_ref[d])
        dhi, dg = _rms_bwd(du, h_ref[...], g_ref[...])
        _accumulate(dg_ref, dg)
        dhi_ref[...] = dh_ref[...] + dhi

    vec = (1, D_MODEL)
    return pl.pallas_call(
        body, name="mix_bwd", grid=(t // tm,),
        in_specs=[_rows(tm, D_MODEL)] * N_MIX + [_const_spec(w_mix.shape), _rows(tm, D_MODEL), _rows(tm, D_MODEL), _const_spec(vec)],
        out_specs=[_rows(tm, D_MODEL), _acc_spec(vec)],
        out_shape=[S((t, D_MODEL), F32), S(vec, F32)],
    )(*dpieces, w_mix, h1, dh2, g)


def _adamw(partials, w, m, v, name):
    r, c = w.shape
    tr = min(r, 512)
    c1 = 1.0 - ADAM_B1 ** ADAM_STEP
    c2 = 1.0 - ADAM_B2 ** ADAM_STEP

    def body(p_ref, w_ref, m_ref, v_ref, g_ref, d_ref, mo_ref, vo_ref):
        g = p_ref[0].astype(F32)
        for s in range(1, N_SHARDS):
            g = g + p_ref[s].astype(F32)
        mn = ADAM_B1 * m_ref[...] + (1.0 - ADAM_B1) * g
        vn = ADAM_B2 * v_ref[...] + (1.0 - ADAM_B2) * (g * g)
        g_ref[...] = g
        mo_ref[...] = mn
        vo_ref[...] = vn
        d_ref[...] = -ADAM_LR * ((mn / c1) / (jnp.sqrt(vn / c2) + ADAM_EPS) + ADAM_WD * w_ref[...])

    blk = pl.BlockSpec((tr, c), lambda i: (i, 0))
    return pl.pallas_call(
        body, name=name, grid=(r // tr,),
        in_specs=[pl.BlockSpec((N_SHARDS, tr, c), lambda i: (0, i, 0)), blk, blk, blk],
        out_specs=[blk] * 4, out_shape=[S((r, c), F32)] * 4,
    )(partials, w, m, v)


_MATRICES = ("ffn1_w_in", "ffn1_w_out", "w_mix_in", "conv_w", "w_conv_out", "w_attn_out", "w_mix_out",
             "ffn2_w_in", "ffn2_w_out", "w_ple_gate", "w_ple_proj")
_GAINS = ("ffn1_norm", "mix_norm", "ffn2_norm", "ple_norm", "final_norm")
_WEIGHTS = ("ffn1_norm", "ffn1_w_in", "ffn1_w_out", "mix_norm", "w_mix_in", "conv_w", "w_conv_out", "w_attn_out", "w_mix_out",
            "ffn2_norm", "ffn2_w_in", "ffn2_w_out", "ple_norm", "w_ple_gate", "w_ple_proj", "final_norm")
CONV_ROWS = 8


def _columns_from_shards(g):
    return jnp.transpose(g, (1, 0, 2)).reshape(g.shape[1], N_SHARDS * g.shape[2])


def _shards_from_columns(a):
    r, c = a.shape
    return jnp.transpose(a.reshape(r, N_SHARDS, c // N_SHARDS), (1, 0, 2))


def kernel(x, p, ffn1_norm, ffn1_w_in, ffn1_w_out, mix_norm, w_mix_in, conv_w, w_conv_out, w_attn_out, w_mix_out, ffn2_norm, ffn2_w_in, ffn2_w_out, ple_norm, w_ple_gate, w_ple_proj, final_norm, loss_target, m_ffn1_norm, m_ffn1_w_in, m_ffn1_w_out, m_mix_norm, m_w_mix_in, m_conv_w, m_w_conv_out, m_w_attn_out, m_w_mix_out, m_ffn2_norm, m_ffn2_w_in, m_ffn2_w_out, m_ple_norm, m_w_ple_gate, m_w_ple_proj, m_final_norm, v_ffn1_norm, v_ffn1_w_in, v_ffn1_w_out, v_mix_norm, v_w_mix_in, v_conv_w, v_w_conv_out, v_w_attn_out, v_w_mix_out, v_ffn2_norm, v_ffn2_w_in, v_ffn2_w_out, v_ple_norm, v_w_ple_gate, v_w_ple_proj, v_final_norm):
    given = dict(locals())
    t = x.shape[1]
    xs = x.reshape(t, D_MODEL)
    ps = p.reshape(t, PLE_DIM)
    target = loss_target.reshape(t, D_MODEL)
    shard = {k: given[k].reshape(given[k].shape[-2:]) for k in _MATRICES}
    gain = {k: given[k].reshape(1, D_MODEL) for k in _GAINS}

    send = {k: shard[k].astype(BF) for k in _MATRICES}
    send["conv_w"] = jnp.pad(shard["conv_w"], ((0, CONV_ROWS - 3), (0, 0)))
    loss_vec, dx, landed, gain_grads = _forward_backward(xs, ps, target, gain, send)
    gain_rows = jnp.concatenate([gain_grads[k] for k in _GAINS] + [jnp.zeros((8 - len(_GAINS), D_MODEL), F32)], axis=0)
    gain_parts, = _exchange_alone("gather", [gain_rows], "gather_gain_gradients")

    out = {}
    for k in _MATRICES:
        w, m, v = shard[k], given["m_" + k].reshape(shard[k].shape), given["v_" + k].reshape(shard[k].shape)
        part = landed[k]
        if k == "conv_w":
            pad = ((0, CONV_ROWS - 3), (0, 0))
            w, m, v = jnp.pad(w, pad), jnp.pad(m, pad), jnp.pad(v, pad, constant_values=1.0)
        res = _adamw(part, w, m, v, "adamw_" + k)
        out[k] = [r[:3] if k == "conv_w" else r for r in res]
    stack = lambda pre: jnp.concatenate([given[pre + k].reshape(1, D_MODEL) for k in _GAINS] + [jnp.ones((8 - len(_GAINS), D_MODEL), F32)], axis=0)
    res = _adamw(gain_parts, stack(""), stack("m_"), stack("v_"), "adamw_gains")
    for j, k in enumerate(_GAINS):
        out[k] = [r[j:j + 1] for r in res]

    loss = lax.psum(loss_vec[0, 0], ("x", "y", "c"))
    per_kind = [[out[k][j].reshape(given[k].shape) for k in _WEIGHTS] for j in range(4)]
    return (loss, dx.reshape(x.shape), *per_kind[0], *per_kind[1], *per_kind[2], *per_kind[3])


def _forward_backward(xs, ps, target, gain, send, full=None):
    exchange = full is None
    full = dict(full or {})
    grads, landed = {}, {}

    def gather(names):
        return ("gather", [send[k] for k in names]) if exchange else None

    def scatter(names):
        return ("scatter", [grads[k] for k in names]) if exchange else None

    def keep(into, names, got):
        into.update(zip(names, got))

    first = ("ffn1_w_in", "ffn1_w_out")
    if exchange:
        keep(full, first, _exchange_alone("gather", [send[k] for k in first], "gather_ffn1"))
    w1_in, w1_out = full["ffn1_w_in"], full["ffn1_w_out"].reshape(N_FF_CHUNKS, FF_CHUNK, D_MODEL)
    n1 = _prenorm(xs, gain["ffn1_norm"])
    mixer = ("w_mix_in", "conv_w", "w_conv_out", "w_attn_out", "w_mix_out")
    (h1, u, gate1, up1), got = _ffn_fwd(xs, n1, w1_in, w1_out, gain["mix_norm"], "ffn1_fwd", ride=gather(mixer))
    keep(full, mixer, got)
    w_mix = full["w_mix_in"]
    w_co, w_ao, w_mo = (full[k].reshape(D_MODEL, D_MODEL) for k in ("w_conv_out", "w_attn_out", "w_mix_out"))
    taps = _columns_from_shards(full["conv_w"][:, :3, :])
    rest = ("ffn2_w_in", "ffn2_w_out", "w_ple_gate", "w_ple_proj")
    (proj,), got = _mix_proj(u, w_mix, ride=gather(rest))
    keep(full, rest, got)
    w2_in, w2_out = full["ffn2_w_in"], full["ffn2_w_out"].reshape(N_FF_CHUNKS, FF_CHUNK, D_MODEL)
    w_pg = full["w_ple_gate"].reshape(D_MODEL, D_MODEL)
    w_pp = _columns_from_shards(full["w_ple_proj"])
    o = _attn_fwd(proj)
    h2, n3, ycin, yc, ya, merged = _mixer_out(proj, o, h1, taps, w_co, w_ao, w_mo, gain["ffn2_norm"])
    (h3, n4, gate2, up2), _ = _ffn_fwd(h2, n3, w2_in, w2_out, gain["ple_norm"], "ffn2_fwd")
    dh3, ds, dpp, loss_vec, dg_final, dg_ple = _tail(h3, n4, ps, w_pg, w_pp, gain["ple_norm"], gain["final_norm"], target)

    one = lambda a: a[None]
    by_rows = lambda g, rows: g.reshape(N_SHARDS, rows // N_SHARDS, D_MODEL)
    grads["w_ple_gate"] = by_rows(_wgrad(one(n4), one(ds), "wgrad_ple_gate"), D_MODEL)
    grads["w_ple_proj"] = _shards_from_columns(_wgrad(one(ps), one(dpp), "wgrad_ple_proj")[0])
    ple = ("w_ple_gate", "w_ple_proj")
    (dh2, df2, act2, dgate2, dup2, dg_ffn2), got = _ffn_bwd(dh3, h2, gain["ffn2_norm"], gate2, up2, w2_in, w2_out, "ffn2_bwd",
                                                             ride=scatter(ple))
    keep(landed, ple, got)
    grads["ffn2_w_out"] = by_rows(_wgrad(act2, one(df2), "wgrad_ffn2_out"), D_FF)
    grads["ffn2_w_in"] = jnp.concatenate([_wgrad(one(n3), dgate2, "wgrad_ffn2_gate"), _wgrad(one(n3), dup2, "wgrad_ffn2_up")], axis=0)
    (dh2b, dyc, dya, dgc, dga, dcb, dcv, d_o), got = _mixer_bwd(dh2, proj, yc, ya, taps, w_co, w_ao, w_mo, ride=scatter(("ffn2_w_out",)))
    keep(landed, ("ffn2_w_out",), got)
    grads["w_mix_out"] = by_rows(_wgrad(one(merged), one(dh2b), "wgrad_mix_out"), D_MODEL)
    grads["w_conv_out"] = by_rows(_wgrad(one(ycin), one(dyc), "wgrad_conv_out"), D_MODEL)
    grads["w_attn_out"] = by_rows(_wgrad(one(o), one(dya), "wgrad_attn_out"), D_MODEL)
    dcc, dcx, dtaps = _conv_bwd(dcv, proj, taps)
    grads["conv_w"] = jnp.pad(_shards_from_columns(dtaps[:3]), ((0, 0), (0, CONV_ROWS - 3), (0, 0)))
    behind_attn = ("ffn2_w_in", "w_mix_out", "w_conv_out", "w_attn_out", "conv_w")
    (dq, dk, dv), got = _attn_bwd(proj, o, d_o, ride=scatter(behind_attn))
    keep(landed, behind_attn, got)
    dpieces = [dcb, dcc, dcx, dq, dk, dv, dgc, dga]
    grads["w_mix_in"] = jnp.concatenate([_wgrad(one(u), one(dp), f"wgrad_mix_in_{d}") for d, dp in enumerate(dpieces)], axis=0)
    dh1, dg_mix = _mix_bwd(dpieces, w_mix, h1, dh2, gain["mix_norm"])
    (dx, df1, act1, dgate1, dup1, dg_ffn1), got = _ffn_bwd(dh1, xs, gain["ffn1_norm"], gate1, up1, w1_in, w1_out, "ffn1_bwd",
                                                           ride=scatter(("w_mix_in",)))
    keep(landed, ("w_mix_in",), got)
    grads["ffn1_w_out"] = by_rows(_wgrad(act1, one(df1), "wgrad_ffn1_out"), D_FF)
    if exchange:
        dgate_w, got = _wgrad(one(n1), dgate1, "wgrad_ffn1_gate", ride=scatter(("ffn1_w_out",)))
        keep(landed, ("ffn1_w_out",), got)
    else:
        dgate_w = _wgrad(one(n1), dgate1, "wgrad_ffn1_gate")
    grads["ffn1_w_in"] = jnp.concatenate([dgate_w, _wgrad(one(n1), dup1, "wgrad_ffn1_up")], axis=0)
    if exchange:
        keep(landed, ("ffn1_w_in",), _exchange_alone("scatter", [grads["ffn1_w_in"]], "scatter_ffn1_in"))
    gain_grads = dict(ffn1_norm=dg_ffn1, mix_norm=dg_mix, ffn2_norm=dg_ffn2, ple_norm=dg_ple, final_norm=dg_final)
    return loss_vec, dx, (landed if exchange else grads), gain_grads
```

```python
import functools
import math

import jax
import jax.numpy as jnp
from jax import lax
from jax.experimental import pallas as pl
from jax.experimental.pallas import tpu as pltpu

D_MODEL = 1024
D_FF = 2816
N_SHARDS = 8
FF_CHUNK = 2 * D_FF // N_SHARDS
N_FF_CHUNKS = D_FF // FF_CHUNK
N_HEADS = 8
HEAD_DIM = 128
PLE_DIM = 256
NORM_EPS = 1e-6
N_MIX = 8
ADAM_LR, ADAM_B1, ADAM_B2, ADAM_EPS, ADAM_WD, ADAM_STEP = 0.001, 0.9, 0.999, 1e-08, 0.01, 10

TOKEN_TILE = 512
FFN_BWD_TILE = 256
WGRAD_TILE = 1024
ATTN_ROWS = 512
ATTN_Q = 128
ATTN_SUB = 128
ATTN_K = 3 * ATTN_SUB
ATTN_SKIP_BELOW = -90.0

BF = jnp.bfloat16
F32 = jnp.float32
MESH = pl.DeviceIdType.MESH
NT = (((1,), (1,)), ((), ()))
TN = (((0,), (0,)), ((), ()))
S = jax.ShapeDtypeStruct
ANY = pl.BlockSpec(memory_space=pl.ANY)


def _const_spec(shape):
    nd = len(shape)
    return pl.BlockSpec(shape, lambda *_: (0,) * nd, pipeline_mode=pl.Buffered(1))


def _rows(tm, cols):
    return pl.BlockSpec((tm, cols), lambda i: (i, 0))


def _chunks(tm):
    return pl.BlockSpec((N_FF_CHUNKS, tm, FF_CHUNK), lambda i: (0, i, 0))


def _acc_spec(shape):
    nd = len(shape)
    return pl.BlockSpec(shape, lambda *_: (0,) * nd)


def _dot(a, b):
    return jnp.dot(a, b, preferred_element_type=F32)


def _dot_nt(a, b):
    return lax.dot_general(a, b, NT, preferred_element_type=F32)


def _dot_tn(a, b):
    return lax.dot_general(a, b, TN, preferred_element_type=F32)


def _rms(h, g):
    r = lax.rsqrt(jnp.mean(h * h, axis=-1, keepdims=True) + NORM_EPS)
    return h * r * g


def _rms_bwd(dn, h, g):
    r = lax.rsqrt(jnp.mean(h * h, axis=-1, keepdims=True) + NORM_EPS)
    nh = h * r
    gd = dn * g
    dh = r * (gd - nh * jnp.mean(gd * nh, axis=-1, keepdims=True))
    return dh, jnp.sum(dn * nh, axis=0, keepdims=True)


def _accumulate(ref, val):
    @pl.when(pl.program_id(0) == 0)
    def _():
        ref[...] = jnp.zeros_like(ref)
    ref[...] += val


def _place():
    x, y, c = lax.axis_index("x"), lax.axis_index("y"), lax.axis_index("c")
    return x, y, c


def _slot(px, py, pc):
    return 4 * px + 2 * py + pc


def _gather_phases(ins, outs, send_sems, recv_sems, local_sems):
    n = len(ins)

    def parties():
        x, y, c = _place()
        return (x, y, c), (x, y, 1 - c), [(1 - x, y), (x, 1 - y), (1 - x, 1 - y)], c

    def copy(a, k, block, to, src=None):
        dst = outs[a].at[_slot(*block)]
        return pltpu.make_async_remote_copy(
            src_ref=dst if src is None else src, dst_ref=dst,
            send_sem=send_sems.at[a, k], recv_sem=recv_sems.at[a, k],
            device_id=to, device_id_type=MESH)

    def own(a, me):
        return pltpu.make_async_copy(ins[a], outs[a].at[_slot(*me)], local_sems.at[a])

    def first(a, me, sibling, chips, c):
        return [copy(a, 0, me, sibling, src=ins[a])] + [copy(a, 1 + j, me, (*chip, c), src=ins[a]) for j, chip in enumerate(chips)]

    def start():
        me, sibling, chips, c = parties()
        for a in range(n):
            own(a, me).start()
        for a in range(n):
            for cp in first(a, me, sibling, chips, c):
                cp.start()

    def forward():
        me, sibling, chips, c = parties()
        for j, chip in enumerate(chips):
            for a in range(n):
                copy(a, 1 + j, (*chip, c), me).wait_recv()
                copy(a, 4 + j, (*chip, c), sibling).start()

    def finish():
        me, sibling, chips, c = parties()
        for a in range(n):
            copy(a, 0, sibling, me).wait_recv()
            for j, chip in enumerate(chips):
                copy(a, 4 + j, (*chip, 1 - c), me).wait_recv()
        for a in range(n):
            for cp in first(a, me, sibling, chips, c) + [copy(a, 4 + j, (*chip, c), sibling) for j, chip in enumerate(chips)]:
                cp.wait_send()
            own(a, me).wait()

    return [start, forward, finish]


def _scatter_phases(ins, outs, send_sems, recv_sems, local_sems):
    n = len(ins)

    def copies():
        x, y, c = _place()
        me = _slot(x, y, c)
        out = [pltpu.make_async_copy(ins[a].at[me], outs[a].at[me], local_sems.at[a]) for a in range(n)]
        for k in range(1, N_SHARDS):
            px = 1 - x if k & 4 else x
            py = 1 - y if k & 2 else y
            pc = 1 - c if k & 1 else c
            for a in range(n):
                out.append(pltpu.make_async_remote_copy(
                    src_ref=ins[a].at[_slot(px, py, pc)], dst_ref=outs[a].at[me],
                    send_sem=send_sems.at[a, k - 1], recv_sem=recv_sems.at[a, k - 1],
                    device_id=(px, py, pc), device_id_type=MESH))
        return out

    def start():
        for cp in copies():
            cp.start()

    def finish():
        for cp in copies():
            cp.wait()

    return [start, finish]


def _pallas(body, *, name, grid, in_specs, out_specs, out_shape, args, scratch_shapes=(), ride=None):
    if ride is None:
        outs = pl.pallas_call(body, name=name, grid=grid, in_specs=in_specs, out_specs=out_specs, out_shape=out_shape,
                              scratch_shapes=list(scratch_shapes))(*args)
        return list(outs), []
    kind, arrays = ride
    n, n_in, n_out, n_scr = len(arrays), len(in_specs), len(out_specs), len(scratch_shapes)
    total = math.prod(grid)
    middle = (2 * total) // 3
    landed_shape = [S((N_SHARDS,) + a.shape if kind == "gather" else a.shape, a.dtype) for a in arrays]

    def with_exchange(*refs):
        ins, riders_in = refs[:n_in], refs[n_in:n_in + n]
        outs, riders_out = refs[n_in + n:n_in + n + n_out], refs[n_in + n + n_out:n_in + 2 * n + n_out]
        scratch, sems = refs[n_in + 2 * n + n_out:n_in + 2 * n + n_out + n_scr], refs[n_in + 2 * n + n_out + n_scr:]
        step = 0
        for axis, size in enumerate(grid):
            step = step * size + pl.program_id(axis)
        phases = (_gather_phases if kind == "gather" else _scatter_phases)(riders_in, riders_out, *sems)
        pl.when(step == 0)(phases[0])
        body(*ins, *outs, *scratch)
        for phase in phases[1:-1]:
            pl.when(step == middle)(phase)
        pl.when(step == total - 1)(phases[-1])

    outs = pl.pallas_call(
        with_exchange, name=name, grid=grid,
        in_specs=list(in_specs) + [ANY] * n, out_specs=list(out_specs) + [ANY] * n,
        out_shape=list(out_shape) + landed_shape,
        scratch_shapes=list(scratch_shapes) + [pltpu.SemaphoreType.DMA((n, 7)), pltpu.SemaphoreType.DMA((n, 7)),
                                               pltpu.SemaphoreType.DMA((n,))],
    )(*args, *arrays)
    return list(outs[:n_out]), list(outs[n_out:])


def _exchange_alone(kind, arrays, name):
    return _pallas(lambda: None, name=name, grid=(1,), in_specs=[], out_specs=[], out_shape=[], args=[], ride=(kind, arrays))[1]


def _prenorm(x, g, ride=None):
    t = x.shape[0]
    tm = min(TOKEN_TILE, t)

    def body(x_ref, g_ref, n_ref):
        n_ref[...] = _rms(x_ref[...], g_ref[...]).astype(BF)

    return _pallas(
        body, name="prenorm", grid=(t // tm,), ride=ride,
        in_specs=[_rows(tm, D_MODEL), _const_spec((1, D_MODEL))], out_specs=[_rows(tm, D_MODEL)],
        out_shape=[S((t, D_MODEL), BF)], args=[x, g])


def _ffn_fwd(h, n, w_in, w_out, g_next, name, ride=None):
    t = h.shape[0]
    tm = min(TOKEN_TILE, t)

    def body(h_ref, n_ref, win_ref, wout_ref, g_ref, ho_ref, no_ref, gate_ref, up_ref):
        nb = n_ref[...]
        acc = jnp.zeros((tm, D_MODEL), F32)
        for c in range(N_FF_CHUNKS):
            gate = _dot(nb, win_ref[c])
            up = _dot(nb, win_ref[N_FF_CHUNKS + c])
            gate_ref[c] = gate.astype(BF)
            up_ref[c] = up.astype(BF)
            act = (gate * jax.nn.sigmoid(gate) * up).astype(BF)
            acc = acc + _dot(act, wout_ref[c])
        ho = h_ref[...] + 0.5 * acc
        ho_ref[...] = ho
        no_ref[...] = _rms(ho, g_ref[...]).astype(BF)

    return _pallas(
        body, name=name, grid=(t // tm,), ride=ride,
        in_specs=[_rows(tm, D_MODEL), _rows(tm, D_MODEL), _const_spec(w_in.shape), _const_spec(w_out.shape),
                  _const_spec((1, D_MODEL))],
        out_specs=[_rows(tm, D_MODEL), _rows(tm, D_MODEL), _chunks(tm), _chunks(tm)],
        out_shape=[S((t, D_MODEL), F32), S((t, D_MODEL), BF),
                   S((N_FF_CHUNKS, t, FF_CHUNK), BF), S((N_FF_CHUNKS, t, FF_CHUNK), BF)],
        args=[h, n, w_in, w_out, g_next])


def _mix_proj(u, w_mix, ride=None):
    t = u.shape[0]
    tm = min(1024, t)

    def body(u_ref, w_ref, o_ref):
        o_ref[0] = _dot(u_ref[...], w_ref[0]).astype(BF)

    return _pallas(
        body, name="mix_proj", grid=(N_MIX, t // tm), ride=ride,
        in_specs=[pl.BlockSpec((tm, D_MODEL), lambda d, i: (i, 0)), pl.BlockSpec((1, D_MODEL, D_MODEL), lambda d, i: (d, 0, 0))],
        out_specs=[pl.BlockSpec((1, tm, D_MODEL), lambda d, i: (d, i, 0))],
        out_shape=[S((N_MIX, t, D_MODEL), BF)], args=[u, w_mix])


HALO = 16


def _piece(d, tm):
    return pl.BlockSpec((1, tm, D_MODEL), lambda i: (d, i, 0))


def _prev_halo(d, tm):
    return pl.BlockSpec((1, HALO, D_MODEL), lambda i: (d, jnp.maximum(i * (tm // HALO) - 1, 0), 0))


def _shift_down(m, prev_tail, k):
    tm = m.shape[0]
    out = pltpu.roll(m, k, 0)
    row = lax.broadcasted_iota(jnp.int32, (tm, 1), 0)
    for j in range(k):
        out = jnp.where(row == j, prev_tail[HALO - k + j:HALO - k + j + 1, :], out)
    return out


def _conv_inputs(cc_ref, cx_ref, cch_ref, cxh_ref):
    m = cc_ref[0].astype(F32) * cx_ref[0].astype(F32)
    mh = cch_ref[0].astype(F32) * cxh_ref[0].astype(F32)
    mh = jnp.where(pl.program_id(0) == 0, 0.0, mh)
    return m, _shift_down(m, mh, 1), _shift_down(m, mh, 2)


def _mixer_out(proj, o, h1, conv_w, w_co, w_ao, w_mo, g_next):
    t = h1.shape[0]
    tm = min(TOKEN_TILE, t)

    def body(cb_ref, cc_ref, cx_ref, gc_ref, ga_ref, cch_ref, cxh_ref, o_ref, h_ref, cw_ref, wco_ref, wao_ref, wmo_ref,
             g_ref, ho_ref, no_ref, ycin_ref, yc_ref, ya_ref, mg_ref):
        m, m1, m2 = _conv_inputs(cc_ref, cx_ref, cch_ref, cxh_ref)
        cw = cw_ref[...]
        cv = cw[0:1, :] * m2 + cw[1:2, :] * m1 + cw[2:3, :] * m
        ycin = (cb_ref[0].astype(F32) * cv).astype(BF)
        ycin_ref[...] = ycin
        yc = _dot(ycin, wco_ref[...])
        ya = _dot(o_ref[...].astype(BF), wao_ref[...])
        yc_ref[...] = yc.astype(BF)
        ya_ref[...] = ya.astype(BF)
        merged = (jax.nn.sigmoid(gc_ref[0].astype(F32)) * yc + jax.nn.sigmoid(ga_ref[0].astype(F32)) * ya).astype(BF)
        mg_ref[...] = merged
        ho = h_ref[...] + _dot(merged, wmo_ref[...])
        ho_ref[...] = ho
        no_ref[...] = _rms(ho, g_ref[...]).astype(BF)

    sq = (D_MODEL, D_MODEL)
    return pl.pallas_call(
        body, name="mixer_out", grid=(t // tm,),
        in_specs=[_piece(0, tm), _piece(1, tm), _piece(2, tm), _piece(6, tm), _piece(7, tm), _prev_halo(1, tm), _prev_halo(2, tm),
                  _rows(tm, D_MODEL), _rows(tm, D_MODEL), _const_spec((3, D_MODEL)), _const_spec(sq), _const_spec(sq),
                  _const_spec(sq), _const_spec((1, D_MODEL))],
        out_specs=[_rows(tm, D_MODEL)] * 6,
        out_shape=[S((t, D_MODEL), F32)] + [S((t, D_MODEL), BF)] * 5,
    )(proj, proj, proj, proj, proj, proj, proj, o, h1, conv_w, w_co, w_ao, w_mo, g_next)


def _suffix_sums(vals, tri, before):
    out, right = [], before
    for b in reversed(range(ATTN_K // ATTN_SUB)):
        v = vals[:, b * ATTN_SUB:(b + 1) * ATTN_SUB]
        hi = v.astype(BF)
        lo = (v - hi.astype(F32)).astype(BF)
        out.append(_dot(hi, tri) + _dot(lo, tri) + right)
        right = right + jnp.sum(v, axis=1, keepdims=True)
    return jnp.concatenate(out[::-1], axis=1), right


ATTN_UNITS = ATTN_ROWS // ATTN_Q


def _unit_rows(x, u):
    return x[u * ATTN_Q:(u + 1) * ATTN_Q]


def _per_unit(fn):
    return jnp.concatenate([fn(u) for u in range(ATTN_UNITS)], axis=0)


def _per_row(vals):
    local = lax.broadcasted_iota(jnp.int32, (ATTN_ROWS, 1), 0)
    out = jnp.full((ATTN_ROWS, 1), vals[0], jnp.int32)
    for u in range(1, ATTN_UNITS):
        out = jnp.where(local >= u * ATTN_Q, vals[u], out)
    return out


def _attn_step(q, k_ref, starts, bounds, row):
    z = _per_unit(lambda u: _dot_nt(_unit_rows(q, u), k_ref[0, pl.ds(starts[u], ATTN_K), :])) * (1.0 / math.sqrt(HEAD_DIM))
    col = _per_row(starts) + lax.broadcasted_iota(jnp.int32, (1, ATTN_K), 1)
    mask = jnp.logical_and(col < row, col < _per_row(bounds))
    soft = jnp.log(1.0 + jnp.exp(-jnp.abs(z)))
    log_beta = jnp.minimum(z, 0.0) - soft
    log_rest = jnp.where(mask, jnp.minimum(-z, 0.0) - soft, 0.0)
    return z, mask, log_beta, log_rest


def _attn_sweep_start(i, t):
    blks = tuple(jnp.maximum(i * ATTN_UNITS + u + 1 - ATTN_K // ATTN_SUB, 0) for u in range(ATTN_UNITS))
    return blks, tuple(jnp.int32(t) for _ in range(ATTN_UNITS))


def _attn_keys(blks):
    return [pl.multiple_of(b * ATTN_SUB, ATTN_SUB) for b in blks]


def _attn_next(blks):
    return tuple(jnp.maximum(b - ATTN_K // ATTN_SUB, 0) for b in blks), tuple(b * ATTN_SUB for b in blks)


def _attn_more(carry):
    return jnp.logical_and(carry[1][ATTN_UNITS - 1] > 0, carry[-1] > ATTN_SKIP_BELOW)


def _tri(strict):
    r = lax.broadcasted_iota(jnp.int32, (ATTN_SUB, ATTN_SUB), 0)
    c = lax.broadcasted_iota(jnp.int32, (ATTN_SUB, ATTN_SUB), 1)
    return (r > c if strict else r >= c).astype(BF)


def _head_cols(piece):
    return lambda t: pl.BlockSpec((1, t, HEAD_DIM), lambda h, i: (piece, 0, h))


def _attn_fwd(proj):
    t = proj.shape[1]
    nq = t // ATTN_ROWS
    tri = _tri(strict=True)

    def body(q_ref, k_ref, v_ref, tri_ref, o_ref):
        i = pl.program_id(1)
        q = q_ref[0]
        row = i * ATTN_ROWS + lax.broadcasted_iota(jnp.int32, (ATTN_ROWS, 1), 0)

        def step(carry):
            blks, bounds, acc, run, _ = carry
            starts = _attn_keys(blks)
            _, mask, log_beta, log_rest = _attn_step(q, k_ref, starts, bounds, row)
            tail, run = _suffix_sums(log_rest, tri_ref[...], run)
            a = jnp.where(mask, jnp.exp(log_beta + tail), 0.0).astype(BF)
            acc = acc + _per_unit(lambda u: _dot(_unit_rows(a, u), v_ref[0, pl.ds(starts[u], ATTN_K), :]))
            return (*_attn_next(blks), acc, run, jnp.max(run))

        first = (*_attn_sweep_start(i, t), jnp.zeros((ATTN_ROWS, HEAD_DIM), F32), jnp.zeros((ATTN_ROWS, 1), F32), jnp.float32(0.0))
        o_ref[...] = lax.while_loop(_attn_more, step, step(first))[2]

    qspec = pl.BlockSpec((1, ATTN_ROWS, HEAD_DIM), lambda h, i: (3, i, h))
    return pl.pallas_call(
        body, name="attn_fwd", grid=(N_HEADS, nq),
        in_specs=[qspec, _head_cols(4)(t), _head_cols(5)(t), pl.BlockSpec((ATTN_SUB, ATTN_SUB), lambda h, i: (0, 0))],
        out_specs=pl.BlockSpec((ATTN_ROWS, HEAD_DIM), lambda h, i: (i, h)),
        out_shape=S((t, D_MODEL), F32))(proj, proj, proj, tri)


def _attn_bwd(proj, o, d_o, ride=None):
    t = proj.shape[1]
    nq = t // ATTN_ROWS
    tri_strict, tri_incl = _tri(strict=True), _tri(strict=False)

    def body(q_ref, k_ref, v_ref, o_ref, do_ref, tris_ref, trii_ref, dq_ref, dk_ref, dv_ref, dk_acc, dv_acc):
        i = pl.program_id(1)

        @pl.when(i == 0)
        def _():
            dk_acc[...] = jnp.zeros_like(dk_acc)
            dv_acc[...] = jnp.zeros_like(dv_acc)

        q = q_ref[0]
        do = do_ref[...]
        total = jnp.sum(do.astype(F32) * o_ref[...], axis=1, keepdims=True)
        row = i * ATTN_ROWS + lax.broadcasted_iota(jnp.int32, (ATTN_ROWS, 1), 0)

        def step(carry):
            blks, bounds, dq, seen, run, _ = carry
            starts = _attn_keys(blks)
            keys = lambda ref, u: ref[0, pl.ds(starts[u], ATTN_K), :]
            z, mask, log_beta, log_rest = _attn_step(q, k_ref, starts, bounds, row)
            tail, run = _suffix_sums(log_rest, tris_ref[...], run)
            a = jnp.where(mask, jnp.exp(log_beta + tail), 0.0).astype(BF)
            de = _per_unit(lambda u: _dot_nt(_unit_rows(do, u), keys(v_ref, u))) * a.astype(F32)
            right, seen = _suffix_sums(de, trii_ref[...], seen)
            beta = jax.nn.sigmoid(z)
            dz = jnp.where(mask, de * (1.0 - beta) - (total - right) * beta, 0.0) * (1.0 / math.sqrt(HEAD_DIM))
            dzb = dz.astype(BF)
            for u in range(ATTN_UNITS):
                dv_acc[pl.ds(starts[u], ATTN_K), :] += _dot_tn(_unit_rows(a, u), _unit_rows(do, u))
                dk_acc[pl.ds(starts[u], ATTN_K), :] += _dot_tn(_unit_rows(dzb, u), _unit_rows(q, u))
            dq = dq + _per_unit(lambda u: _dot(_unit_rows(dzb, u), keys(k_ref, u)))
            return (*_attn_next(blks), dq, seen, run, jnp.max(run))

        zero = jnp.zeros((ATTN_ROWS, 1), F32)
        first = (*_attn_sweep_start(i, t), jnp.zeros((ATTN_ROWS, HEAD_DIM), F32), zero, zero, jnp.float32(0.0))
        dq_ref[...] = lax.while_loop(_attn_more, step, step(first))[2].astype(BF)

        @pl.when(i == nq - 1)
        def _():
            dk_ref[...] = dk_acc[...].astype(BF)
            dv_ref[...] = dv_acc[...].astype(BF)

    qspec = pl.BlockSpec((1, ATTN_ROWS, HEAD_DIM), lambda h, i: (3, i, h))
    rowblk = pl.BlockSpec((ATTN_ROWS, HEAD_DIM), lambda h, i: (i, h))
    head = pl.BlockSpec((t, HEAD_DIM), lambda h, i: (0, h))
    trispec = pl.BlockSpec((ATTN_SUB, ATTN_SUB), lambda h, i: (0, 0))
    return _pallas(
        body, name="attn_bwd", grid=(N_HEADS, nq), ride=ride,
        in_specs=[qspec, _head_cols(4)(t), _head_cols(5)(t), rowblk, rowblk, trispec, trispec],
        out_specs=[rowblk, head, head],
        out_shape=[S((t, D_MODEL), BF)] * 3,
        scratch_shapes=[pltpu.VMEM((t, HEAD_DIM), F32), pltpu.VMEM((t, HEAD_DIM), F32)],
        args=[proj, proj, proj, o, d_o, tri_strict, tri_incl])


def _tail(h3, n4, p, w_pg, w_pp, g_ple, g_final, target):
    t = h3.shape[0]
    tm = min(TOKEN_TILE, t)
    steps = t // tm

    def body(h_ref, n_ref, p_ref, wpg_ref, wpp_ref, gp_ref, gf_ref, tgt_ref,
             dh_ref, ds_ref, dpp_ref, loss_ref, dgf_ref, dgp_ref):
        pg = jax.nn.sigmoid(_dot(n_ref[...], wpg_ref[...]))
        pp = _dot(p_ref[...].astype(BF), wpp_ref[...])
        h3v = h_ref[...]
        h4 = h3v + pg * pp
        gf = gf_ref[...]
        diff = _rms(h4, gf) - tgt_ref[...]
        _accumulate(loss_ref, jnp.sum(diff * diff, axis=0, keepdims=True))
        dh4, dgf = _rms_bwd(diff * (1.0 / D_MODEL), h4, gf)
        _accumulate(dgf_ref, dgf)
        dpp_ref[...] = (dh4 * pg).astype(BF)
        ds = (dh4 * pp * pg * (1.0 - pg)).astype(BF)
        ds_ref[...] = ds
        dh3, dgp = _rms_bwd(_dot_nt(ds, wpg_ref[...]), h3v, gp_ref[...])
        _accumulate(dgp_ref, dgp)
        dh_ref[...] = dh4 + dh3

        @pl.when(pl.program_id(0) == steps - 1)
        def _():
            loss_ref[...] = jnp.full(loss_ref.shape, 0.5 / D_MODEL * jnp.sum(loss_ref[...]), F32)

    vec = (1, D_MODEL)
    return pl.pallas_call(
        body, name="tail", grid=(steps,),
        in_specs=[_rows(tm, D_MODEL), _rows(tm, D_MODEL), _rows(tm, PLE_DIM), _const_spec((D_MODEL, D_MODEL)),
                  _const_spec((PLE_DIM, D_MODEL)), _const_spec(vec), _const_spec(vec), _rows(tm, D_MODEL)],
        out_specs=[_rows(tm, D_MODEL)] * 3 + [_acc_spec(vec)] * 3,
        out_shape=[S((t, D_MODEL), F32), S((t, D_MODEL), BF), S((t, D_MODEL), BF)] + [S(vec, F32)] * 3,
    )(h3, n4, p, w_pg, w_pp, g_ple, g_final, target)


def _wgrad(xs, ys, name, ride=None):
    bx, t, k = xs.shape
    by, _, n = ys.shape
    b = max(bx, by)
    tt = min(WGRAD_TILE, t)
    steps = t // tt

    def body(x_ref, y_ref, o_ref, acc_ref):
        s = pl.program_id(1)

        @pl.when(s == 0)
        def _():
            acc_ref[...] = jnp.zeros_like(acc_ref)
        acc_ref[...] += _dot_tn(x_ref[0].astype(BF), y_ref[0].astype(BF))

        @pl.when(s == steps - 1)
        def _():
            o_ref[0] = acc_ref[...].astype(BF)

    (out,), landed = _pallas(
        body, name=name, grid=(b, steps), ride=ride,
        in_specs=[pl.BlockSpec((1, tt, k), (lambda j, s: (j, s, 0)) if bx > 1 else (lambda j, s: (0, s, 0))),
                  pl.BlockSpec((1, tt, n), (lambda j, s: (j, s, 0)) if by > 1 else (lambda j, s: (0, s, 0)))],
        out_specs=[pl.BlockSpec((1, k, n), lambda j, s: (j, 0, 0))],
        out_shape=[S((b, k, n), BF)],
        scratch_shapes=[pltpu.VMEM((k, n), F32)],
        args=[xs, ys])
    return (out, landed) if ride is not None else out


def _ffn_bwd(dh, h_in, g, gate, up, w_in, w_out, name, ride=None):
    t = dh.shape[0]
    tm = min(FFN_BWD_TILE, t)

    def body(dh_ref, h_ref, g_ref, gate_ref, up_ref, win_ref, wout_ref, dhi_ref, df_ref, act_ref, dgate_ref, dup_ref, dg_ref):
        dhv = dh_ref[...]
        df = (0.5 * dhv).astype(BF)
        df_ref[...] = df
        dn = jnp.zeros((tm, D_MODEL), F32)
        for c in range(N_FF_CHUNKS):
            gt = gate_ref[c].astype(F32)
            u = up_ref[c].astype(F32)
            sg = jax.nn.sigmoid(gt)
            silu = gt * sg
            act_ref[c] = (silu * u).astype(BF)
            dact = _dot_nt(df, wout_ref[c])
            dgate = (dact * u * (sg * (1.0 + gt * (1.0 - sg)))).astype(BF)
            dup = (dact * silu).astype(BF)
            dgate_ref[c] = dgate
            dup_ref[c] = dup
            dn = dn + _dot_nt(dgate, win_ref[c]) + _dot_nt(dup, win_ref[N_FF_CHUNKS + c])
        dhi, dg = _rms_bwd(dn, h_ref[...], g_ref[...])
        _accumulate(dg_ref, dg)
        dhi_ref[...] = dhv + dhi

    vec = (1, D_MODEL)
    return _pallas(
        body, name=name, grid=(t // tm,), ride=ride,
        in_specs=[_rows(tm, D_MODEL), _rows(tm, D_MODEL), _const_spec(vec), _chunks(tm), _chunks(tm),
                  _const_spec(w_in.shape), _const_spec(w_out.shape)],
        out_specs=[_rows(tm, D_MODEL), _rows(tm, D_MODEL), _chunks(tm), _chunks(tm), _chunks(tm), _acc_spec(vec)],
        out_shape=[S((t, D_MODEL), F32), S((t, D_MODEL), BF)] + [S((N_FF_CHUNKS, t, FF_CHUNK), BF)] * 3 + [S(vec, F32)],
        args=[dh, h_in, g, gate, up, w_in, w_out])


def _mixer_bwd(dh2, proj, yc, ya, conv_w, w_co, w_ao, w_mo, ride=None):
    t = dh2.shape[0]
    tm = min(TOKEN_TILE, t)

    def body(dh_ref, cb_ref, cc_ref, cx_ref, gc_ref, ga_ref, cch_ref, cxh_ref, yc_ref, ya_ref, cw_ref, wco_ref, wao_ref, wmo_ref,
             dhb_ref, dyc_ref, dya_ref, dgc_ref, dga_ref, dcb_ref, dcv_ref, do_ref):
        dhb = dh_ref[...].astype(BF)
        dhb_ref[...] = dhb
        dmerged = _dot_nt(dhb, wmo_ref[...])
        sc = jax.nn.sigmoid(gc_ref[0].astype(F32))
        sa = jax.nn.sigmoid(ga_ref[0].astype(F32))
        dyc = (dmerged * sc).astype(BF)
        dya = (dmerged * sa).astype(BF)
        dyc_ref[...] = dyc
        dya_ref[...] = dya
        dgc_ref[...] = (dmerged * yc_ref[...].astype(F32) * sc * (1.0 - sc)).astype(BF)
        dga_ref[...] = (dmerged * ya_ref[...].astype(F32) * sa * (1.0 - sa)).astype(BF)
        m, m1, m2 = _conv_inputs(cc_ref, cx_ref, cch_ref, cxh_ref)
        cw = cw_ref[...]
        cv = cw[0:1, :] * m2 + cw[1:2, :] * m1 + cw[2:3, :] * m
        dycin = _dot_nt(dyc, wco_ref[...])
        dcb_ref[...] = (dycin * cv).astype(BF)
        dcv_ref[...] = dycin * cb_ref[0].astype(F32)
        do_ref[...] = _dot_nt(dya, wao_ref[...]).astype(BF)

    sq = (D_MODEL, D_MODEL)
    return _pallas(
        body, name="mixer_bwd", grid=(t // tm,), ride=ride,
        in_specs=[_rows(tm, D_MODEL), _piece(0, tm), _piece(1, tm), _piece(2, tm), _piece(6, tm), _piece(7, tm),
                  _prev_halo(1, tm), _prev_halo(2, tm), _rows(tm, D_MODEL), _rows(tm, D_MODEL),
                  _const_spec((3, D_MODEL)), _const_spec(sq), _const_spec(sq), _const_spec(sq)],
        out_specs=[_rows(tm, D_MODEL)] * 8,
        out_shape=[S((t, D_MODEL), BF)] * 6 + [S((t, D_MODEL), F32), S((t, D_MODEL), BF)],
        args=[dh2, proj, proj, proj, proj, proj, proj, proj, yc, ya, conv_w, w_co, w_ao, w_mo])


F32_HALO = 8


def _conv_bwd(dcv, proj, conv_w):
    t = dcv.shape[0]
    tm = min(TOKEN_TILE, t)
    steps = t // tm

    def body(dcv_ref, nxt_ref, cc_ref, cx_ref, cch_ref, cxh_ref, cw_ref, dcc_ref, dcx_ref, dw_ref):
        i = pl.program_id(0)
        m, m1, m2 = _conv_inputs(cc_ref, cx_ref, cch_ref, cxh_ref)
        d0 = dcv_ref[...]
        nxt = jnp.where(i == steps - 1, 0.0, nxt_ref[...])
        row = lax.broadcasted_iota(jnp.int32, (tm, 1), 0)
        d1 = jnp.where(row == tm - 1, nxt[0:1, :], pltpu.roll(d0, tm - 1, 0))
        d2 = pltpu.roll(d0, tm - 2, 0)
        d2 = jnp.where(row == tm - 2, nxt[0:1, :], jnp.where(row == tm - 1, nxt[1:2, :], d2))
        cw = cw_ref[...]
        dm = cw[2:3, :] * d0 + cw[1:2, :] * d1 + cw[0:1, :] * d2
        dcc_ref[...] = (dm * cx_ref[0].astype(F32)).astype(BF)
        dcx_ref[...] = (dm * cc_ref[0].astype(F32)).astype(BF)
        tap_row = lax.broadcasted_iota(jnp.int32, (F32_HALO, 1), 0)
        dw = jnp.zeros((F32_HALO, D_MODEL), F32)
        for j, mk in enumerate((m2, m1, m)):
            dw = jnp.where(tap_row == j, jnp.sum(d0 * mk, axis=0, keepdims=True), dw)
        _accumulate(dw_ref, dw)

    nxt_spec = pl.BlockSpec((F32_HALO, D_MODEL), lambda i: (jnp.minimum((i + 1) * (tm // F32_HALO), t // F32_HALO - 1), 0))
    return pl.pallas_call(
        body, name="conv_bwd", grid=(steps,),
        in_specs=[_rows(tm, D_MODEL), nxt_spec, _piece(1, tm), _piece(2, tm), _prev_halo(1, tm), _prev_halo(2, tm),
                  _const_spec((3, D_MODEL))],
        out_specs=[_rows(tm, D_MODEL), _rows(tm, D_MODEL), _acc_spec((F32_HALO, D_MODEL))],
        out_shape=[S((t, D_MODEL), BF), S((t, D_MODEL), BF), S((F32_HALO, D_MODEL), F32)],
    )(dcv, dcv, proj, proj, proj, proj, conv_w)


def _mix_bwd(dpieces, w_mix, h1, dh2, g):
    t = h1.shape[0]
    tm = min(TOKEN_TILE, t)

    def body(*refs):
        pieces, (w_ref, h_ref, dh_ref, g_ref, dhi_ref, dg_ref) = refs[:N_MIX], refs[N_MIX:]
        du = jnp.zeros((tm, D_MODEL), F32)
        for d in range(N_MIX):
            du = du + _dot_nt(pieces[d][...], w_ref[d])
        dhi, dg = _rms_bwd(du, h_ref[...], g_ref[...])
        _accumulate(dg_ref, dg)
        dhi_ref[...] = dh_ref[...] + dhi

    vec = (1, D_MODEL)
    return pl.pallas_call(
        body, name="mix_bwd", grid=(t // tm,),
        in_specs=[_rows(tm, D_MODEL)] * N_MIX + [_const_spec(w_mix.shape), _rows(tm, D_MODEL), _rows(tm, D_MODEL), _const_spec(vec)],
        out_specs=[_rows(tm, D_MODEL), _acc_spec(vec)],
        out_shape=[S((t, D_MODEL), F32), S(vec, F32)],
    )(*dpieces, w_mix, h1, dh2, g)


def _adamw(partials, w, m, v, name):
    r, c = w.shape
    tr = min(r, 512)
    c1 = 1.0 - ADAM_B1 ** ADAM_STEP
    c2 = 1.0 - ADAM_B2 ** ADAM_STEP

    def body(p_ref, w_ref, m_ref, v_ref, g_ref, d_ref, mo_ref, vo_ref):
        g = p_ref[0].astype(F32)
        for s in range(1, N_SHARDS):
            g = g + p_ref[s].astype(F32)
        mn = ADAM_B1 * m_ref[...] + (1.0 - ADAM_B1) * g
        vn = ADAM_B2 * v_ref[...] + (1.0 - ADAM_B2) * (g * g)
        g_ref[...] = g
        mo_ref[...] = mn
        vo_ref[...] = vn
        d_ref[...] = -ADAM_LR * ((mn / c1) / (jnp.sqrt(vn / c2) + ADAM_EPS) + ADAM_WD * w_ref[...])

    blk = pl.BlockSpec((tr, c), lambda i: (i, 0))
    return pl.pallas_call(
        body, name=name, grid=(r // tr,),
        in_specs=[pl.BlockSpec((N_SHARDS, tr, c), lambda i: (0, i, 0)), blk, blk, blk],
        out_specs=[blk] * 4, out_shape=[S((r, c), F32)] * 4,
    )(partials, w, m, v)


_MATRICES = ("ffn1_w_in", "ffn1_w_out", "w_mix_in", "conv_w", "w_conv_out", "w_attn_out", "w_mix_out",
             "ffn2_w_in", "ffn2_w_out", "w_ple_gate", "w_ple_proj")
_GAINS = ("ffn1_norm", "mix_norm", "ffn2_norm", "ple_norm", "final_norm")
_WEIGHTS = ("ffn1_norm", "ffn1_w_in", "ffn1_w_out", "mix_norm", "w_mix_in", "conv_w", "w_conv_out", "w_attn_out", "w_mix_out",
            "ffn2_norm", "ffn2_w_in", "ffn2_w_out", "ple_norm", "w_ple_gate", "w_ple_proj", "final_norm")
CONV_ROWS = 8


def _columns_from_shards(g):
    return jnp.transpose(g, (1, 0, 2)).reshape(g.shape[1], N_SHARDS * g.shape[2])


def _shards_from_columns(a):
    r, c = a.shape
    return jnp.transpose(a.reshape(r, N_SHARDS, c // N_SHARDS), (1, 0, 2))


def kernel(x, p, ffn1_norm, ffn1_w_in, ffn1_w_out, mix_norm, w_mix_in, conv_w, w_conv_out, w_attn_out, w_mix_out, ffn2_norm, ffn2_w_in, ffn2_w_out, ple_norm, w_ple_gate, w_ple_proj, final_norm, loss_target, m_ffn1_norm, m_ffn1_w_in, m_ffn1_w_out, m_mix_norm, m_w_mix_in, m_conv_w, m_w_conv_out, m_w_attn_out, m_w_mix_out, m_ffn2_norm, m_ffn2_w_in, m_ffn2_w_out, m_ple_norm, m_w_ple_gate, m_w_ple_proj, m_final_norm, v_ffn1_norm, v_ffn1_w_in, v_ffn1_w_out, v_mix_norm, v_w_mix_in, v_conv_w, v_w_conv_out, v_w_attn_out, v_w_mix_out, v_ffn2_norm, v_ffn2_w_in, v_ffn2_w_out, v_ple_norm, v_w_ple_gate, v_w_ple_proj, v_final_norm):
    given = dict(locals())
    t = x.shape[1]
    xs = x.reshape(t, D_MODEL)
    ps = p.reshape(t, PLE_DIM)
    target = loss_target.reshape(t, D_MODEL)
    shard = {k: given[k].reshape(given[k].shape[-2:]) for k in _MATRICES}
    gain = {k: given[k].reshape(1, D_MODEL) for k in _GAINS}

    send = {k: shard[k].astype(BF) for k in _MATRICES}
    send["conv_w"] = jnp.pad(shard["conv_w"], ((0, CONV_ROWS - 3), (0, 0)))
    loss_vec, dx, landed, gain_grads = _forward_backward(xs, ps, target, gain, send)
    gain_rows = jnp.concatenate([gain_grads[k] for k in _GAINS] + [jnp.zeros((8 - len(_GAINS), D_MODEL), F32)], axis=0)
    gain_parts, = _exchange_alone("gather", [gain_rows], "gather_gain_gradients")

    out = {}
    for k in _MATRICES:
        w, m, v = shard[k], given["m_" + k].reshape(shard[k].shape), given["v_" + k].reshape(shard[k].shape)
        part = landed[k]
        if k == "conv_w":
            pad = ((0, CONV_ROWS - 3), (0, 0))
            w, m, v = jnp.pad(w, pad), jnp.pad(m, pad), jnp.pad(v, pad, constant_values=1.0)
        res = _adamw(part, w, m, v, "adamw_" + k)
        out[k] = [r[:3] if k == "conv_w" else r for r in res]
    stack = lambda pre: jnp.concatenate([given[pre + k].reshape(1, D_MODEL) for k in _GAINS] + [jnp.ones((8 - len(_GAINS), D_MODEL), F32)], axis=0)
    res = _adamw(gain_parts, stack(""), stack("m_"), stack("v_"), "adamw_gains")
    for j, k in enumerate(_GAINS):
        out[k] = [r[j:j + 1] for r in res]

    loss = lax.psum(loss_vec[0, 0], ("x", "y", "c"))
    per_kind = [[out[k][j].reshape(given[k].shape) for k in _WEIGHTS] for j in range(4)]
    return (loss, dx.reshape(x.shape), *per_kind[0], *per_kind[1], *per_kind[2], *per_kind[3])


def _forward_backward(xs, ps, target, gain, send, full=None):
    exchange = full is None
    full = dict(full or {})
    grads, landed = {}, {}

    def gather(names):
        return ("gather", [send[k] for k in names]) if exchange else None

    def scatter(names):
        return ("scatter", [grads[k] for k in names]) if exchange else None

    def keep(into, names, got):
        into.update(zip(names, got))

    first = ("ffn1_w_in", "ffn1_w_out")
    (n1,), got = _prenorm(xs, gain["ffn1_norm"], ride=gather(first))
    keep(full, first, got)
    w1_in, w1_out = full["ffn1_w_in"], full["ffn1_w_out"].reshape(N_FF_CHUNKS, FF_CHUNK, D_MODEL)
    mixer = ("w_mix_in", "conv_w", "w_conv_out", "w_attn_out", "w_mix_out")
    (h1, u, gate1, up1), got = _ffn_fwd(xs, n1, w1_in, w1_out, gain["mix_norm"], "ffn1_fwd", ride=gather(mixer))
    keep(full, mixer, got)
    w_mix = full["w_mix_in"]
    w_co, w_ao, w_mo = (full[k].reshape(D_MODEL, D_MODEL) for k in ("w_conv_out", "w_attn_out", "w_mix_out"))
    taps = _columns_from_shards(full["conv_w"][:, :3, :])
    rest = ("ffn2_w_in", "ffn2_w_out", "w_ple_gate", "w_ple_proj")
    (proj,), got = _mix_proj(u, w_mix, ride=gather(rest))
    keep(full, rest, got)
    w2_in, w2_out = full["ffn2_w_in"], full["ffn2_w_out"].reshape(N_FF_CHUNKS, FF_CHUNK, D_MODEL)
    w_pg = full["w_ple_gate"].reshape(D_MODEL, D_MODEL)
    w_pp = _columns_from_shards(full["w_ple_proj"])
    o = _attn_fwd(proj)
    h2, n3, ycin, yc, ya, merged = _mixer_out(proj, o, h1, taps, w_co, w_ao, w_mo, gain["ffn2_norm"])
    (h3, n4, gate2, up2), _ = _ffn_fwd(h2, n3, w2_in, w2_out, gain["ple_norm"], "ffn2_fwd")
    dh3, ds, dpp, loss_vec, dg_final, dg_ple = _tail(h3, n4, ps, w_pg, w_pp, gain["ple_norm"], gain["final_norm"], target)

    one = lambda a: a[None]
    by_rows = lambda g, rows: g.reshape(N_SHARDS, rows // N_SHARDS, D_MODEL)
    grads["w_ple_gate"] = by_rows(_wgrad(one(n4), one(ds), "wgrad_ple_gate"), D_MODEL)
    grads["w_ple_proj"] = _shards_from_columns(_wgrad(one(ps), one(dpp), "wgrad_ple_proj")[0])
    ple = ("w_ple_gate", "w_ple_proj")
    (dh2, df2, act2, dgate2, dup2, dg_ffn2), got = _ffn_bwd(dh3, h2, gain["ffn2_norm"], gate2, up2, w2_in, w2_out, "ffn2_bwd",
                                                             ride=scatter(ple))
    keep(landed, ple, got)
    grads["ffn2_w_out"] = by_rows(_wgrad(act2, one(df2), "wgrad_ffn2_out"), D_FF)
    grads["ffn2_w_in"] = jnp.concatenate([_wgrad(one(n3), dgate2, "wgrad_ffn2_gate"), _wgrad(one(n3), dup2, "wgrad_ffn2_up")], axis=0)
    (dh2b, dyc, dya, dgc, dga, dcb, dcv, d_o), got = _mixer_bwd(dh2, proj, yc, ya, taps, w_co, w_ao, w_mo, ride=scatter(("ffn2_w_out",)))
    keep(landed, ("ffn2_w_out",), got)
    grads["w_mix_out"] = by_rows(_wgrad(one(merged), one(dh2b), "wgrad_mix_out"), D_MODEL)
    grads["w_conv_out"] = by_rows(_wgrad(one(ycin), one(dyc), "wgrad_conv_out"), D_MODEL)
    grads["w_attn_out"] = by_rows(_wgrad(one(o), one(dya), "wgrad_attn_out"), D_MODEL)
    dcc, dcx, dtaps = _conv_bwd(dcv, proj, taps)
    grads["conv_w"] = jnp.pad(_shards_from_columns(dtaps[:3]), ((0, 0), (0, CONV_ROWS - 3), (0, 0)))
    behind_attn = ("ffn2_w_in", "w_mix_out", "w_conv_out", "w_attn_out", "conv_w")
    (dq, dk, dv), got = _attn_bwd(proj, o, d_o, ride=scatter(behind_attn))
    keep(landed, behind_attn, got)
    dpieces = [dcb, dcc, dcx, dq, dk, dv, dgc, dga]
    grads["w_mix_in"] = jnp.concatenate([_wgrad(one(u), one(dp), f"wgrad_mix_in_{d}") for d, dp in enumerate(dpieces)], axis=0)
    dh1, dg_mix = _mix_bwd(dpieces, w_mix, h1, dh2, gain["mix_norm"])
    (dx, df1, act1, dgate1, dup1, dg_ffn1), got = _ffn_bwd(dh1, xs, gain["ffn1_norm"], gate1, up1, w1_in, w1_out, "ffn1_bwd",
                                                           ride=scatter(("w_mix_in",)))
    keep(landed, ("w_mix_in",), got)
    grads["ffn1_w_out"] = by_rows(_wgrad(act1, one(df1), "wgrad_ffn1_out"), D_FF)
    if exchange:
        dgate_w, got = _wgrad(one(n1), dgate1, "wgrad_ffn1_gate", ride=scatter(("ffn1_w_out",)))
        keep(landed, ("ffn1_w_out",), got)
    else:
        dgate_w = _wgrad(one(n1), dgate1, "wgrad_ffn1_gate")
    grads["ffn1_w_in"] = jnp.concatenate([dgate_w, _wgrad(one(n1), dup1, "wgrad_ffn1_up")], axis=0)
    if exchange:
        keep(landed, ("ffn1_w_in",), _exchange_alone("scatter", [grads["ffn1_w_in"]], "scatter_ffn1_in"))
    gain_grads = dict(ffn1_norm=dg_ffn1, mix_norm=dg_mix, ffn2_norm=dg_ffn2, ple_norm=dg_ple, final_norm=dg_final)
    return loss_vec, dx, (landed if exchange else grads), gain_grads
```

```python
import functools
import math

import jax
import jax.numpy as jnp
from jax import lax
from jax.experimental import pallas as pl
from jax.experimental.pallas import tpu as pltpu

D_MODEL = 1024
D_FF = 2816
N_SHARDS = 8
FF_CHUNK = 2 * D_FF // N_SHARDS
N_FF_CHUNKS = D_FF // FF_CHUNK
N_HEADS = 8
HEAD_DIM = 128
PLE_DIM = 256
NORM_EPS = 1e-6
N_MIX = 8
ADAM_LR, ADAM_B1, ADAM_B2, ADAM_EPS, ADAM_WD, ADAM_STEP = 0.001, 0.9, 0.999, 1e-08, 0.01, 10

TOKEN_TILE = 512
FFN_BWD_TILE = 256
WGRAD_TILE = 4096
PROJ_TILE = 2048
ATTN_ROWS = 512
ATTN_Q = 128
ATTN_SUB = 128
ATTN_K = 3 * ATTN_SUB
ATTN_SKIP_BELOW = -90.0

BF = jnp.bfloat16
F32 = jnp.float32
MESH = pl.DeviceIdType.MESH
NT = (((1,), (1,)), ((), ()))
TN = (((0,), (0,)), ((), ()))
S = jax.ShapeDtypeStruct
ANY = pl.BlockSpec(memory_space=pl.ANY)


def _const_spec(shape):
    nd = len(shape)
    return pl.BlockSpec(shape, lambda *_: (0,) * nd, pipeline_mode=pl.Buffered(1))


def _rows(tm, cols):
    return pl.BlockSpec((tm, cols), lambda i: (i, 0))


def _chunks(tm):
    return pl.BlockSpec((N_FF_CHUNKS, tm, FF_CHUNK), lambda i: (0, i, 0))


def _acc_spec(shape):
    nd = len(shape)
    return pl.BlockSpec(shape, lambda *_: (0,) * nd)


def _dot(a, b):
    return jnp.dot(a, b, preferred_element_type=F32)


def _dot_nt(a, b):
    return lax.dot_general(a, b, NT, preferred_element_type=F32)


def _dot_tn(a, b):
    return lax.dot_general(a, b, TN, preferred_element_type=F32)


def _rms(h, g):
    r = lax.rsqrt(jnp.mean(h * h, axis=-1, keepdims=True) + NORM_EPS)
    return h * r * g


def _rms_bwd(dn, h, g):
    r = lax.rsqrt(jnp.mean(h * h, axis=-1, keepdims=True) + NORM_EPS)
    nh = h * r
    gd = dn * g
    dh = r * (gd - nh * jnp.mean(gd * nh, axis=-1, keepdims=True))
    return dh, jnp.sum(dn * nh, axis=0, keepdims=True)


def _accumulate(ref, val):
    @pl.when(pl.program_id(0) == 0)
    def _():
        ref[...] = jnp.zeros_like(ref)
    ref[...] += val


def _place():
    x, y, c = lax.axis_index("x"), lax.axis_index("y"), lax.axis_index("c")
    return x, y, c


def _slot(px, py, pc):
    return 4 * px + 2 * py + pc


def _gather_phases(ins, outs, send_sems, recv_sems, local_sems):
    n = len(ins)

    def parties():
        x, y, c = _place()
        return (x, y, c), (x, y, 1 - c), [(1 - x, y), (x, 1 - y), (1 - x, 1 - y)], c

    def copy(a, k, block, to, src=None):
        dst = outs[a].at[_slot(*block)]
        return pltpu.make_async_remote_copy(
            src_ref=dst if src is None else src, dst_ref=dst,
            send_sem=send_sems.at[a, k], recv_sem=recv_sems.at[a, k],
            device_id=to, device_id_type=MESH)

    def own(a, me):
        return pltpu.make_async_copy(ins[a], outs[a].at[_slot(*me)], local_sems.at[a])

    def first(a, me, sibling, chips, c):
        return [copy(a, 0, me, sibling, src=ins[a])] + [copy(a, 1 + j, me, (*chip, c), src=ins[a]) for j, chip in enumerate(chips)]

    def start():
        me, sibling, chips, c = parties()
        for a in range(n):
            own(a, me).start()
        for a in range(n):
            for cp in first(a, me, sibling, chips, c):
                cp.start()

    def forward():
        me, sibling, chips, c = parties()
        for j, chip in enumerate(chips):
            for a in range(n):
                copy(a, 1 + j, (*chip, c), me).wait_recv()
                copy(a, 4 + j, (*chip, c), sibling).start()

    def finish():
        me, sibling, chips, c = parties()
        for a in range(n):
            copy(a, 0, sibling, me).wait_recv()
            for j, chip in enumerate(chips):
                copy(a, 4 + j, (*chip, 1 - c), me).wait_recv()
        for a in range(n):
            for cp in first(a, me, sibling, chips, c) + [copy(a, 4 + j, (*chip, c), sibling) for j, chip in enumerate(chips)]:
                cp.wait_send()
            own(a, me).wait()

    return [start, forward, finish]


def _scatter_phases(ins, outs, send_sems, recv_sems, local_sems):
    n = len(ins)

    def copies():
        x, y, c = _place()
        me = _slot(x, y, c)
        out = [pltpu.make_async_copy(ins[a].at[me], outs[a].at[me], local_sems.at[a]) for a in range(n)]
        for k in range(1, N_SHARDS):
            px = 1 - x if k & 4 else x
            py = 1 - y if k & 2 else y
            pc = 1 - c if k & 1 else c
            for a in range(n):
                out.append(pltpu.make_async_remote_copy(
                    src_ref=ins[a].at[_slot(px, py, pc)], dst_ref=outs[a].at[me],
                    send_sem=send_sems.at[a, k - 1], recv_sem=recv_sems.at[a, k - 1],
                    device_id=(px, py, pc), device_id_type=MESH))
        return out

    def start():
        for cp in copies():
            cp.start()

    def finish():
        for cp in copies():
            cp.wait()

    return [start, finish]


def _pallas(body, *, name, grid, in_specs, out_specs, out_shape, args, scratch_shapes=(), ride=None):
    if ride is None:
        outs = pl.pallas_call(body, name=name, grid=grid, in_specs=in_specs, out_specs=out_specs, out_shape=out_shape,
                              scratch_shapes=list(scratch_shapes))(*args)
        return list(outs), []
    kind, arrays = ride
    n, n_in, n_out, n_scr = len(arrays), len(in_specs), len(out_specs), len(scratch_shapes)
    total = math.prod(grid)
    middle = (2 * total) // 3
    landed_shape = [S((N_SHARDS,) + a.shape if kind == "gather" else a.shape, a.dtype) for a in arrays]

    def with_exchange(*refs):
        ins, riders_in = refs[:n_in], refs[n_in:n_in + n]
        outs, riders_out = refs[n_in + n:n_in + n + n_out], refs[n_in + n + n_out:n_in + 2 * n + n_out]
        scratch, sems = refs[n_in + 2 * n + n_out:n_in + 2 * n + n_out + n_scr], refs[n_in + 2 * n + n_out + n_scr:]
        step = 0
        for axis, size in enumerate(grid):
            step = step * size + pl.program_id(axis)
        phases = (_gather_phases if kind == "gather" else _scatter_phases)(riders_in, riders_out, *sems)
        pl.when(step == 0)(phases[0])
        body(*ins, *outs, *scratch)
        for phase in phases[1:-1]:
            pl.when(step == middle)(phase)
        pl.when(step == total - 1)(phases[-1])

    outs = pl.pallas_call(
        with_exchange, name=name, grid=grid,
        in_specs=list(in_specs) + [ANY] * n, out_specs=list(out_specs) + [ANY] * n,
        out_shape=list(out_shape) + landed_shape,
        scratch_shapes=list(scratch_shapes) + [pltpu.SemaphoreType.DMA((n, 7)), pltpu.SemaphoreType.DMA((n, 7)),
                                               pltpu.SemaphoreType.DMA((n,))],
    )(*args, *arrays)
    return list(outs[:n_out]), list(outs[n_out:])


def _exchange_alone(kind, arrays, name):
    return _pallas(lambda: None, name=name, grid=(1,), in_specs=[], out_specs=[], out_shape=[], args=[], ride=(kind, arrays))[1]


def _prenorm(x, g, ride=None):
    t = x.shape[0]
    tm = min(TOKEN_TILE, t)

    def body(x_ref, g_ref, n_ref):
        n_ref[...] = _rms(x_ref[...], g_ref[...]).astype(BF)

    return _pallas(
        body, name="prenorm", grid=(t // tm,), ride=ride,
        in_specs=[_rows(tm, D_MODEL), _const_spec((1, D_MODEL))], out_specs=[_rows(tm, D_MODEL)],
        out_shape=[S((t, D_MODEL), BF)], args=[x, g])


def _ffn_fwd(h, n, w_in, w_out, g_next, name, ride=None):
    t = h.shape[0]
    tm = min(TOKEN_TILE, t)

    def body(h_ref, n_ref, win_ref, wout_ref, g_ref, ho_ref, no_ref, gate_ref, up_ref):
        nb = n_ref[...]
        acc = jnp.zeros((tm, D_MODEL), F32)
        for c in range(N_FF_CHUNKS):
            gate = _dot(nb, win_ref[c])
            up = _dot(nb, win_ref[N_FF_CHUNKS + c])
            gate_ref[c] = gate.astype(BF)
            up_ref[c] = up.astype(BF)
            act = (gate * jax.nn.sigmoid(gate) * up).astype(BF)
            acc = acc + _dot(act, wout_ref[c])
        ho = h_ref[...] + 0.5 * acc
        ho_ref[...] = ho
        no_ref[...] = _rms(ho, g_ref[...]).astype(BF)

    return _pallas(
        body, name=name, grid=(t // tm,), ride=ride,
        in_specs=[_rows(tm, D_MODEL), _rows(tm, D_MODEL), _const_spec(w_in.shape), _const_spec(w_out.shape),
                  _const_spec((1, D_MODEL))],
        out_specs=[_rows(tm, D_MODEL), _rows(tm, D_MODEL), _chunks(tm), _chunks(tm)],
        out_shape=[S((t, D_MODEL), F32), S((t, D_MODEL), BF),
                   S((N_FF_CHUNKS, t, FF_CHUNK), BF), S((N_FF_CHUNKS, t, FF_CHUNK), BF)],
        args=[h, n, w_in, w_out, g_next])


def _mix_proj(u, w_mix, ride=None):
    t = u.shape[0]
    tm = min(PROJ_TILE, t)

    def body(u_ref, w_ref, o_ref):
        o_ref[0] = _dot(u_ref[...], w_ref[0]).astype(BF)

    return _pallas(
        body, name="mix_proj", grid=(N_MIX, t // tm), ride=ride,
        in_specs=[pl.BlockSpec((tm, D_MODEL), lambda d, i: (i, 0)), pl.BlockSpec((1, D_MODEL, D_MODEL), lambda d, i: (d, 0, 0))],
        out_specs=[pl.BlockSpec((1, tm, D_MODEL), lambda d, i: (d, i, 0))],
        out_shape=[S((N_MIX, t, D_MODEL), BF)], args=[u, w_mix])


HALO = 16


def _piece(d, tm):
    return pl.BlockSpec((1, tm, D_MODEL), lambda i: (d, i, 0))


def _prev_halo(d, tm):
    return pl.BlockSpec((1, HALO, D_MODEL), lambda i: (d, jnp.maximum(i * (tm // HALO) - 1, 0), 0))


def _shift_down(m, prev_tail, k):
    tm = m.shape[0]
    out = pltpu.roll(m, k, 0)
    row = lax.broadcasted_iota(jnp.int32, (tm, 1), 0)
    for j in range(k):
        out = jnp.where(row == j, prev_tail[HALO - k + j:HALO - k + j + 1, :], out)
    return out


def _conv_inputs(cc_ref, cx_ref, cch_ref, cxh_ref):
    m = cc_ref[0].astype(F32) * cx_ref[0].astype(F32)
    mh = cch_ref[0].astype(F32) * cxh_ref[0].astype(F32)
    mh = jnp.where(pl.program_id(0) == 0, 0.0, mh)
    return m, _shift_down(m, mh, 1), _shift_down(m, mh, 2)


def _mixer_out(proj, o, h1, conv_w, w_co, w_ao, w_mo, g_next):
    t = h1.shape[0]
    tm = min(TOKEN_TILE, t)

    def body(cb_ref, cc_ref, cx_ref, gc_ref, ga_ref, cch_ref, cxh_ref, o_ref, h_ref, cw_ref, wco_ref, wao_ref, wmo_ref,
             g_ref, ho_ref, no_ref, ycin_ref, yc_ref, ya_ref, mg_ref):
        m, m1, m2 = _conv_inputs(cc_ref, cx_ref, cch_ref, cxh_ref)
        cw = cw_ref[...]
        cv = cw[0:1, :] * m2 + cw[1:2, :] * m1 + cw[2:3, :] * m
        ycin = (cb_ref[0].astype(F32) * cv).astype(BF)
        ycin_ref[...] = ycin
        yc = _dot(ycin, wco_ref[...])
        ya = _dot(o_ref[...].astype(BF), wao_ref[...])
        yc_ref[...] = yc.astype(BF)
        ya_ref[...] = ya.astype(BF)
        merged = (jax.nn.sigmoid(gc_ref[0].astype(F32)) * yc + jax.nn.sigmoid(ga_ref[0].astype(F32)) * ya).astype(BF)
        mg_ref[...] = merged
        ho = h_ref[...] + _dot(merged, wmo_ref[...])
        ho_ref[...] = ho
        no_ref[...] = _rms(ho, g_ref[...]).astype(BF)

    sq = (D_MODEL, D_MODEL)
    return pl.pallas_call(
        body, name="mixer_out", grid=(t // tm,),
        in_specs=[_piece(0, tm), _piece(1, tm), _piece(2, tm), _piece(6, tm), _piece(7, tm), _prev_halo(1, tm), _prev_halo(2, tm),
                  _rows(tm, D_MODEL), _rows(tm, D_MODEL), _const_spec((3, D_MODEL)), _const_spec(sq), _const_spec(sq),
                  _const_spec(sq), _const_spec((1, D_MODEL))],
        out_specs=[_rows(tm, D_MODEL)] * 6,
        out_shape=[S((t, D_MODEL), F32)] + [S((t, D_MODEL), BF)] * 5,
    )(proj, proj, proj, proj, proj, proj, proj, o, h1, conv_w, w_co, w_ao, w_mo, g_next)


def _suffix_sums(vals, tri, before):
    out, right = [], before
    for b in reversed(range(ATTN_K // ATTN_SUB)):
        v = vals[:, b * ATTN_SUB:(b + 1) * ATTN_SUB]
        hi = v.astype(BF)
        lo = (v - hi.astype(F32)).astype(BF)
        out.append(_dot(hi, tri) + _dot(lo, tri) + right)
        right = right + jnp.sum(v, axis=1, keepdims=True)
    return jnp.concatenate(out[::-1], axis=1), right


ATTN_UNITS = ATTN_ROWS // ATTN_Q


def _unit_rows(x, u):
    return x[u * ATTN_Q:(u + 1) * ATTN_Q]


def _per_unit(fn):
    return jnp.concatenate([fn(u) for u in range(ATTN_UNITS)], axis=0)


def _per_row(vals):
    local = lax.broadcasted_iota(jnp.int32, (ATTN_ROWS, 1), 0)
    out = jnp.full((ATTN_ROWS, 1), vals[0], jnp.int32)
    for u in range(1, ATTN_UNITS):
        out = jnp.where(local >= u * ATTN_Q, vals[u], out)
    return out


def _attn_step(q, k_ref, starts, bounds, row):
    z = _per_unit(lambda u: _dot_nt(_unit_rows(q, u), k_ref[0, pl.ds(starts[u], ATTN_K), :])) * (1.0 / math.sqrt(HEAD_DIM))
    col = _per_row(starts) + lax.broadcasted_iota(jnp.int32, (1, ATTN_K), 1)
    mask = jnp.logical_and(col < row, col < _per_row(bounds))
    soft = jnp.log(1.0 + jnp.exp(-jnp.abs(z)))
    log_beta = jnp.minimum(z, 0.0) - soft
    log_rest = jnp.where(mask, jnp.minimum(-z, 0.0) - soft, 0.0)
    return z, mask, log_beta, log_rest


def _attn_sweep_start(i, t):
    blks = tuple(jnp.maximum(i * ATTN_UNITS + u + 1 - ATTN_K // ATTN_SUB, 0) for u in range(ATTN_UNITS))
    return blks, tuple(jnp.int32(t) for _ in range(ATTN_UNITS))


def _attn_keys(blks):
    return [pl.multiple_of(b * ATTN_SUB, ATTN_SUB) for b in blks]


def _attn_next(blks):
    return tuple(jnp.maximum(b - ATTN_K // ATTN_SUB, 0) for b in blks), tuple(b * ATTN_SUB for b in blks)


def _attn_more(carry):
    return jnp.logical_and(carry[1][ATTN_UNITS - 1] > 0, carry[-1] > ATTN_SKIP_BELOW)


def _tri(strict):
    r = lax.broadcasted_iota(jnp.int32, (ATTN_SUB, ATTN_SUB), 0)
    c = lax.broadcasted_iota(jnp.int32, (ATTN_SUB, ATTN_SUB), 1)
    return (r > c if strict else r >= c).astype(BF)


def _head_cols(piece):
    return lambda t: pl.BlockSpec((1, t, HEAD_DIM), lambda h, i: (piece, 0, h))


def _attn_fwd(proj):
    t = proj.shape[1]
    nq = t // ATTN_ROWS
    tri = _tri(strict=True)

    def body(q_ref, k_ref, v_ref, tri_ref, o_ref):
        i = pl.program_id(1)
        q = q_ref[0]
        row = i * ATTN_ROWS + lax.broadcasted_iota(jnp.int32, (ATTN_ROWS, 1), 0)

        def step(carry):
            blks, bounds, acc, run, _ = carry
            starts = _attn_keys(blks)
            _, mask, log_beta, log_rest = _attn_step(q, k_ref, starts, bounds, row)
            tail, run = _suffix_sums(log_rest, tri_ref[...], run)
            a = jnp.where(mask, jnp.exp(log_beta + tail), 0.0).astype(BF)
            acc = acc + _per_unit(lambda u: _dot(_unit_rows(a, u), v_ref[0, pl.ds(starts[u], ATTN_K), :]))
            return (*_attn_next(blks), acc, run, jnp.max(run))

        first = (*_attn_sweep_start(i, t), jnp.zeros((ATTN_ROWS, HEAD_DIM), F32), jnp.zeros((ATTN_ROWS, 1), F32), jnp.float32(0.0))
        o_ref[...] = lax.while_loop(_attn_more, step, step(first))[2]

    qspec = pl.BlockSpec((1, ATTN_ROWS, HEAD_DIM), lambda h, i: (3, i, h))
    return pl.pallas_call(
        body, name="attn_fwd", grid=(N_HEADS, nq),
        in_specs=[qspec, _head_cols(4)(t), _head_cols(5)(t), pl.BlockSpec((ATTN_SUB, ATTN_SUB), lambda h, i: (0, 0))],
        out_specs=pl.BlockSpec((ATTN_ROWS, HEAD_DIM), lambda h, i: (i, h)),
        out_shape=S((t, D_MODEL), F32))(proj, proj, proj, tri)


def _attn_bwd(proj, o, d_o, ride=None):
    t = proj.shape[1]
    nq = t // ATTN_ROWS
    tri_strict, tri_incl = _tri(strict=True), _tri(strict=False)

    def body(q_ref, k_ref, v_ref, o_ref, do_ref, tris_ref, trii_ref, dq_ref, dk_ref, dv_ref, dk_acc, dv_acc):
        i = pl.program_id(1)

        @pl.when(i == 0)
        def _():
            dk_acc[...] = jnp.zeros_like(dk_acc)
            dv_acc[...] = jnp.zeros_like(dv_acc)

        q = q_ref[0]
        do = do_ref[...]
        total = jnp.sum(do.astype(F32) * o_ref[...], axis=1, keepdims=True)
        row = i * ATTN_ROWS + lax.broadcasted_iota(jnp.int32, (ATTN_ROWS, 1), 0)

        def step(carry):
            blks, bounds, dq, seen, run, _ = carry
            starts = _attn_keys(blks)
            keys = lambda ref, u: ref[0, pl.ds(starts[u], ATTN_K), :]
            z, mask, log_beta, log_rest = _attn_step(q, k_ref, starts, bounds, row)
            tail, run = _suffix_sums(log_rest, tris_ref[...], run)
            a = jnp.where(mask, jnp.exp(log_beta + tail), 0.0).astype(BF)
            de = _per_unit(lambda u: _dot_nt(_unit_rows(do, u), keys(v_ref, u))) * a.astype(F32)
            right, seen = _suffix_sums(de, trii_ref[...], seen)
            beta = jax.nn.sigmoid(z)
            dz = jnp.where(mask, de * (1.0 - beta) - (total - right) * beta, 0.0) * (1.0 / math.sqrt(HEAD_DIM))
            dzb = dz.astype(BF)
            for u in range(ATTN_UNITS):
                dv_acc[pl.ds(starts[u], ATTN_K), :] += _dot_tn(_unit_rows(a, u), _unit_rows(do, u))
                dk_acc[pl.ds(starts[u], ATTN_K), :] += _dot_tn(_unit_rows(dzb, u), _unit_rows(q, u))
            dq = dq + _per_unit(lambda u: _dot(_unit_rows(dzb, u), keys(k_ref, u)))
            return (*_attn_next(blks), dq, seen, run, jnp.max(run))

        zero = jnp.zeros((ATTN_ROWS, 1), F32)
        first = (*_attn_sweep_start(i, t), jnp.zeros((ATTN_ROWS, HEAD_DIM), F32), zero, zero, jnp.float32(0.0))
        dq_ref[...] = lax.while_loop(_attn_more, step, step(first))[2].astype(BF)

        @pl.when(i == nq - 1)
        def _():
            dk_ref[...] = dk_acc[...].astype(BF)
            dv_ref[...] = dv_acc[...].astype(BF)

    qspec = pl.BlockSpec((1, ATTN_ROWS, HEAD_DIM), lambda h, i: (3, i, h))
    rowblk = pl.BlockSpec((ATTN_ROWS, HEAD_DIM), lambda h, i: (i, h))
    head = pl.BlockSpec((t, HEAD_DIM), lambda h, i: (0, h))
    trispec = pl.BlockSpec((ATTN_SUB, ATTN_SUB), lambda h, i: (0, 0))
    return _pallas(
        body, name="attn_bwd", grid=(N_HEADS, nq), ride=ride,
        in_specs=[qspec, _head_cols(4)(t), _head_cols(5)(t), rowblk, rowblk, trispec, trispec],
        out_specs=[rowblk, head, head],
        out_shape=[S((t, D_MODEL), BF)] * 3,
        scratch_shapes=[pltpu.VMEM((t, HEAD_DIM), F32), pltpu.VMEM((t, HEAD_DIM), F32)],
        args=[proj, proj, proj, o, d_o, tri_strict, tri_incl])


def _tail(h3, n4, p, w_pg, w_pp, g_ple, g_final, target):
    t = h3.shape[0]
    tm = min(TOKEN_TILE, t)
    steps = t // tm

    def body(h_ref, n_ref, p_ref, wpg_ref, wpp_ref, gp_ref, gf_ref, tgt_ref,
             dh_ref, ds_ref, dpp_ref, loss_ref, dgf_ref, dgp_ref):
        pg = jax.nn.sigmoid(_dot(n_ref[...], wpg_ref[...]))
        pp = _dot(p_ref[...].astype(BF), wpp_ref[...])
        h3v = h_ref[...]
        h4 = h3v + pg * pp
        gf = gf_ref[...]
        diff = _rms(h4, gf) - tgt_ref[...]
        _accumulate(loss_ref, jnp.sum(diff * diff, axis=0, keepdims=True))
        dh4, dgf = _rms_bwd(diff * (1.0 / D_MODEL), h4, gf)
        _accumulate(dgf_ref, dgf)
        dpp_ref[...] = (dh4 * pg).astype(BF)
        ds = (dh4 * pp * pg * (1.0 - pg)).astype(BF)
        ds_ref[...] = ds
        dh3, dgp = _rms_bwd(_dot_nt(ds, wpg_ref[...]), h3v, gp_ref[...])
        _accumulate(dgp_ref, dgp)
        dh_ref[...] = dh4 + dh3

        @pl.when(pl.program_id(0) == steps - 1)
        def _():
            loss_ref[...] = jnp.full(loss_ref.shape, 0.5 / D_MODEL * jnp.sum(loss_ref[...]), F32)

    vec = (1, D_MODEL)
    return pl.pallas_call(
        body, name="tail", grid=(steps,),
        in_specs=[_rows(tm, D_MODEL), _rows(tm, D_MODEL), _rows(tm, PLE_DIM), _const_spec((D_MODEL, D_MODEL)),
                  _const_spec((PLE_DIM, D_MODEL)), _const_spec(vec), _const_spec(vec), _rows(tm, D_MODEL)],
        out_specs=[_rows(tm, D_MODEL)] * 3 + [_acc_spec(vec)] * 3,
        out_shape=[S((t, D_MODEL), F32), S((t, D_MODEL), BF), S((t, D_MODEL), BF)] + [S(vec, F32)] * 3,
    )(h3, n4, p, w_pg, w_pp, g_ple, g_final, target)


def _wgrad(xs, ys, name, ride=None):
    bx, t, k = xs.shape
    by, _, n = ys.shape
    b = max(bx, by)
    tt = min(WGRAD_TILE * 2 // xs.dtype.itemsize, t)
    steps = t // tt

    def body(x_ref, y_ref, o_ref, acc_ref):
        s = pl.program_id(1)

        @pl.when(s == 0)
        def _():
            acc_ref[...] = jnp.zeros_like(acc_ref)
        acc_ref[...] += _dot_tn(x_ref[0].astype(BF), y_ref[0].astype(BF))

        @pl.when(s == steps - 1)
        def _():
            o_ref[0] = acc_ref[...].astype(BF)

    (out,), landed = _pallas(
        body, name=name, grid=(b, steps), ride=ride,
        in_specs=[pl.BlockSpec((1, tt, k), (lambda j, s: (j, s, 0)) if bx > 1 else (lambda j, s: (0, s, 0))),
                  pl.BlockSpec((1, tt, n), (lambda j, s: (j, s, 0)) if by > 1 else (lambda j, s: (0, s, 0)))],
        out_specs=[pl.BlockSpec((1, k, n), lambda j, s: (j, 0, 0))],
        out_shape=[S((b, k, n), BF)],
        scratch_shapes=[pltpu.VMEM((k, n), F32)],
        args=[xs, ys])
    return (out, landed) if ride is not None else out


def _wgrad_pieces(x, ys, name, ride=None, tile=None):
    t, k = x.shape
    n = ys[0].shape[2]
    counts = [y.shape[0] for y in ys]
    offsets = [sum(counts[:j]) for j in range(len(ys))]
    total = sum(counts)
    tt = min(tile or WGRAD_TILE, t)
    steps = t // tt

    def body(x_ref, *refs):
        y_refs, o_ref, acc_ref = refs[:len(ys)], refs[len(ys)], refs[len(ys) + 1]
        p, s = pl.program_id(0), pl.program_id(1)

        @pl.when(s == 0)
        def _():
            acc_ref[...] = jnp.zeros_like(acc_ref)
        for j, y_ref in enumerate(y_refs):
            @pl.when(jnp.logical_and(p >= offsets[j], p < offsets[j] + counts[j]))
            def _(y_ref=y_ref):
                acc_ref[...] += _dot_tn(x_ref[...], y_ref[0])

        @pl.when(s == steps - 1)
        def _():
            o_ref[0] = acc_ref[...].astype(BF)

    def turn(j):
        lo, hi = offsets[j], offsets[j] + counts[j]
        return lambda p, s: (jnp.clip(p - lo, 0, counts[j] - 1), jnp.where(p < lo, 0, jnp.where(p >= hi, steps - 1, s)), 0)

    (out,), landed = _pallas(
        body, name=name, grid=(total, steps), ride=ride,
        in_specs=[pl.BlockSpec((tt, k), lambda p, s: (s, 0))] + [pl.BlockSpec((1, tt, n), turn(j)) for j in range(len(ys))],
        out_specs=[pl.BlockSpec((1, k, n), lambda p, s: (p, 0, 0))],
        out_shape=[S((total, k, n), BF)],
        scratch_shapes=[pltpu.VMEM((k, n), F32)],
        args=[x, *ys])
    return (out, landed) if ride is not None else out


def _ffn_bwd(dh, h_in, g, gate, up, w_in, w_out, name, ride=None):
    t = dh.shape[0]
    tm = min(FFN_BWD_TILE, t)

    def body(dh_ref, h_ref, g_ref, gate_ref, up_ref, win_ref, wout_ref, dhi_ref, df_ref, act_ref, dgate_ref, dup_ref, dg_ref):
        dhv = dh_ref[...]
        df = (0.5 * dhv).astype(BF)
        df_ref[...] = df
        dn = jnp.zeros((tm, D_MODEL), F32)
        for c in range(N_FF_CHUNKS):
            gt = gate_ref[c].astype(F32)
            u = up_ref[c].astype(F32)
            sg = jax.nn.sigmoid(gt)
            silu = gt * sg
            act_ref[c] = (silu * u).astype(BF)
            dact = _dot_nt(df, wout_ref[c])
            dgate = (dact * u * (sg * (1.0 + gt * (1.0 - sg)))).astype(BF)
            dup = (dact * silu).astype(BF)
            dgate_ref[c] = dgate
            dup_ref[c] = dup
            dn = dn + _dot_nt(dgate, win_ref[c]) + _dot_nt(dup, win_ref[N_FF_CHUNKS + c])
        dhi, dg = _rms_bwd(dn, h_ref[...], g_ref[...])
        _accumulate(dg_ref, dg)
        dhi_ref[...] = dhv + dhi

    vec = (1, D_MODEL)
    return _pallas(
        body, name=name, grid=(t // tm,), ride=ride,
        in_specs=[_rows(tm, D_MODEL), _rows(tm, D_MODEL), _const_spec(vec), _chunks(tm), _chunks(tm),
                  _const_spec(w_in.shape), _const_spec(w_out.shape)],
        out_specs=[_rows(tm, D_MODEL), _rows(tm, D_MODEL), _chunks(tm), _chunks(tm), _chunks(tm), _acc_spec(vec)],
        out_shape=[S((t, D_MODEL), F32), S((t, D_MODEL), BF)] + [S((N_FF_CHUNKS, t, FF_CHUNK), BF)] * 3 + [S(vec, F32)],
        args=[dh, h_in, g, gate, up, w_in, w_out])


def _mixer_bwd(dh2, proj, yc, ya, conv_w, w_co, w_ao, w_mo, ride=None):
    t = dh2.shape[0]
    tm = min(TOKEN_TILE, t)

    def body(dh_ref, cb_ref, cc_ref, cx_ref, gc_ref, ga_ref, cch_ref, cxh_ref, yc_ref, ya_ref, cw_ref, wco_ref, wao_ref, wmo_ref,
             dhb_ref, dyc_ref, dya_ref, dgc_ref, dga_ref, dcb_ref, dcv_ref, do_ref):
        dhb = dh_ref[...].astype(BF)
        dhb_ref[...] = dhb
        dmerged = _dot_nt(dhb, wmo_ref[...])
        sc = jax.nn.sigmoid(gc_ref[0].astype(F32))
        sa = jax.nn.sigmoid(ga_ref[0].astype(F32))
        dyc = (dmerged * sc).astype(BF)
        dya = (dmerged * sa).astype(BF)
        dyc_ref[...] = dyc
        dya_ref[...] = dya
        dgc_ref[...] = (dmerged * yc_ref[...].astype(F32) * sc * (1.0 - sc)).astype(BF)
        dga_ref[...] = (dmerged * ya_ref[...].astype(F32) * sa * (1.0 - sa)).astype(BF)
        m, m1, m2 = _conv_inputs(cc_ref, cx_ref, cch_ref, cxh_ref)
        cw = cw_ref[...]
        cv = cw[0:1, :] * m2 + cw[1:2, :] * m1 + cw[2:3, :] * m
        dycin = _dot_nt(dyc, wco_ref[...])
        dcb_ref[...] = (dycin * cv).astype(BF)
        dcv_ref[...] = dycin * cb_ref[0].astype(F32)
        do_ref[...] = _dot_nt(dya, wao_ref[...]).astype(BF)

    sq = (D_MODEL, D_MODEL)
    return _pallas(
        body, name="mixer_bwd", grid=(t // tm,), ride=ride,
        in_specs=[_rows(tm, D_MODEL), _piece(0, tm), _piece(1, tm), _piece(2, tm), _piece(6, tm), _piece(7, tm),
                  _prev_halo(1, tm), _prev_halo(2, tm), _rows(tm, D_MODEL), _rows(tm, D_MODEL),
                  _const_spec((3, D_MODEL)), _const_spec(sq), _const_spec(sq), _const_spec(sq)],
        out_specs=[_rows(tm, D_MODEL)] * 8,
        out_shape=[S((t, D_MODEL), BF)] * 6 + [S((t, D_MODEL), F32), S((t, D_MODEL), BF)],
        args=[dh2, proj, proj, proj, proj, proj, proj, proj, yc, ya, conv_w, w_co, w_ao, w_mo])


F32_HALO = 8


def _conv_bwd(dcv, proj, conv_w):
    t = dcv.shape[0]
    tm = min(TOKEN_TILE, t)
    steps = t // tm

    def body(dcv_ref, nxt_ref, cc_ref, cx_ref, cch_ref, cxh_ref, cw_ref, dcc_ref, dcx_ref, dw_ref):
        i = pl.program_id(0)
        m, m1, m2 = _conv_inputs(cc_ref, cx_ref, cch_ref, cxh_ref)
        d0 = dcv_ref[...]
        nxt = jnp.where(i == steps - 1, 0.0, nxt_ref[...])
        row = lax.broadcasted_iota(jnp.int32, (tm, 1), 0)
        d1 = jnp.where(row == tm - 1, nxt[0:1, :], pltpu.roll(d0, tm - 1, 0))
        d2 = pltpu.roll(d0, tm - 2, 0)
        d2 = jnp.where(row == tm - 2, nxt[0:1, :], jnp.where(row == tm - 1, nxt[1:2, :], d2))
        cw = cw_ref[...]
        dm = cw[2:3, :] * d0 + cw[1:2, :] * d1 + cw[0:1, :] * d2
        dcc_ref[...] = (dm * cx_ref[0].astype(F32)).astype(BF)
        dcx_ref[...] = (dm * cc_ref[0].astype(F32)).astype(BF)
        tap_row = lax.broadcasted_iota(jnp.int32, (F32_HALO, 1), 0)
        dw = jnp.zeros((F32_HALO, D_MODEL), F32)
        for j, mk in enumerate((m2, m1, m)):
            dw = jnp.where(tap_row == j, jnp.sum(d0 * mk, axis=0, keepdims=True), dw)
        _accumulate(dw_ref, dw)

    nxt_spec = pl.BlockSpec((F32_HALO, D_MODEL), lambda i: (jnp.minimum((i + 1) * (tm // F32_HALO), t // F32_HALO - 1), 0))
    return pl.pallas_call(
        body, name="conv_bwd", grid=(steps,),
        in_specs=[_rows(tm, D_MODEL), nxt_spec, _piece(1, tm), _piece(2, tm), _prev_halo(1, tm), _prev_halo(2, tm),
                  _const_spec((3, D_MODEL))],
        out_specs=[_rows(tm, D_MODEL), _rows(tm, D_MODEL), _acc_spec((F32_HALO, D_MODEL))],
        out_shape=[S((t, D_MODEL), BF), S((t, D_MODEL), BF), S((F32_HALO, D_MODEL), F32)],
    )(dcv, dcv, proj, proj, proj, proj, conv_w)


def _mix_bwd(dpieces, w_mix, h1, dh2, g, ride=None):
    t = h1.shape[0]
    tm = min(TOKEN_TILE, t)

    def body(*refs):
        pieces, (w_ref, h_ref, dh_ref, g_ref, dhi_ref, dg_ref) = refs[:N_MIX], refs[N_MIX:]
        du = jnp.zeros((tm, D_MODEL), F32)
        for d in range(N_MIX):
            du = du + _dot_nt(pieces[d][...], w_ref[d])
        dhi, dg = _rms_bwd(du, h_ref[...], g_ref[...])
        _accumulate(dg_ref, dg)
        dhi_ref[...] = dh_ref[...] + dhi

    vec = (1, D_MODEL)
    return _pallas(
        body, name="mix_bwd", grid=(t // tm,), ride=ride,
        in_specs=[_rows(tm, D_MODEL)] * N_MIX + [_const_spec(w_mix.shape), _rows(tm, D_MODEL), _rows(tm, D_MODEL), _const_spec(vec)],
        out_specs=[_rows(tm, D_MODEL), _acc_spec(vec)],
        out_shape=[S((t, D_MODEL), F32), S(vec, F32)],
        args=[*dpieces, w_mix, h1, dh2, g])


def _adamw(partials, w, m, v, name):
    r, c = w.shape
    tr = min(r, 512)
    c1 = 1.0 - ADAM_B1 ** ADAM_STEP
    c2 = 1.0 - ADAM_B2 ** ADAM_STEP

    def body(p_ref, w_ref, m_ref, v_ref, g_ref, d_ref, mo_ref, vo_ref):
        g = p_ref[0].astype(F32)
        for s in range(1, N_SHARDS):
            g = g + p_ref[s].astype(F32)
        mn = ADAM_B1 * m_ref[...] + (1.0 - ADAM_B1) * g
        vn = ADAM_B2 * v_ref[...] + (1.0 - ADAM_B2) * (g * g)
        g_ref[...] = g
        mo_ref[...] = mn
        vo_ref[...] = vn
        d_ref[...] = -ADAM_LR * ((mn / c1) / (jnp.sqrt(vn / c2) + ADAM_EPS) + ADAM_WD * w_ref[...])

    blk = pl.BlockSpec((tr, c), lambda i: (i, 0))
    return pl.pallas_call(
        body, name=name, grid=(r // tr,),
        in_specs=[pl.BlockSpec((N_SHARDS, tr, c), lambda i: (0, i, 0)), blk, blk, blk],
        out_specs=[blk] * 4, out_shape=[S((r, c), F32)] * 4,
    )(partials, w, m, v)


_MATRICES = ("ffn1_w_in", "ffn1_w_out", "w_mix_in", "conv_w", "w_conv_out", "w_attn_out", "w_mix_out",
             "ffn2_w_in", "ffn2_w_out", "w_ple_gate", "w_ple_proj")
_GAINS = ("ffn1_norm", "mix_norm", "ffn2_norm", "ple_norm", "final_norm")
_WEIGHTS = ("ffn1_norm", "ffn1_w_in", "ffn1_w_out", "mix_norm", "w_mix_in", "conv_w", "w_conv_out", "w_attn_out", "w_mix_out",
            "ffn2_norm", "ffn2_w_in", "ffn2_w_out", "ple_norm", "w_ple_gate", "w_ple_proj", "final_norm")
CONV_ROWS = 8


def _columns_from_shards(g):
    return jnp.transpose(g, (1, 0, 2)).reshape(g.shape[1], N_SHARDS * g.shape[2])


def _shards_from_columns(a):
    r, c = a.shape
    return jnp.transpose(a.reshape(r, N_SHARDS, c // N_SHARDS), (1, 0, 2))


def kernel(x, p, ffn1_norm, ffn1_w_in, ffn1_w_out, mix_norm, w_mix_in, conv_w, w_conv_out, w_attn_out, w_mix_out, ffn2_norm, ffn2_w_in, ffn2_w_out, ple_norm, w_ple_gate, w_ple_proj, final_norm, loss_target, m_ffn1_norm, m_ffn1_w_in, m_ffn1_w_out, m_mix_norm, m_w_mix_in, m_conv_w, m_w_conv_out, m_w_attn_out, m_w_mix_out, m_ffn2_norm, m_ffn2_w_in, m_ffn2_w_out, m_ple_norm, m_w_ple_gate, m_w_ple_proj, m_final_norm, v_ffn1_norm, v_ffn1_w_in, v_ffn1_w_out, v_mix_norm, v_w_mix_in, v_conv_w, v_w_conv_out, v_w_attn_out, v_w_mix_out, v_ffn2_norm, v_ffn2_w_in, v_ffn2_w_out, v_ple_norm, v_w_ple_gate, v_w_ple_proj, v_final_norm):
    given = dict(locals())
    t = x.shape[1]
    xs = x.reshape(t, D_MODEL)
    ps = p.reshape(t, PLE_DIM)
    target = loss_target.reshape(t, D_MODEL)
    shard = {k: given[k].reshape(given[k].shape[-2:]) for k in _MATRICES}
    gain = {k: given[k].reshape(1, D_MODEL) for k in _GAINS}

    send = {k: shard[k].astype(BF) for k in _MATRICES}
    send["conv_w"] = jnp.pad(shard["conv_w"], ((0, CONV_ROWS - 3), (0, 0)))
    loss_vec, dx, landed, gain_grads = _forward_backward(xs, ps, target, gain, send)
    gain_rows = jnp.concatenate([gain_grads[k] for k in _GAINS] + [jnp.zeros((8 - len(_GAINS), D_MODEL), F32)], axis=0)
    gain_parts, = _exchange_alone("gather", [gain_rows], "gather_gain_gradients")

    out = {}
    for k in _MATRICES:
        w, m, v = shard[k], given["m_" + k].reshape(shard[k].shape), given["v_" + k].reshape(shard[k].shape)
        part = landed[k]
        if k == "conv_w":
            pad = ((0, CONV_ROWS - 3), (0, 0))
            w, m, v = jnp.pad(w, pad), jnp.pad(m, pad), jnp.pad(v, pad, constant_values=1.0)
        res = _adamw(part, w, m, v, "adamw_" + k)
        out[k] = [r[:3] if k == "conv_w" else r for r in res]
    stack = lambda pre: jnp.concatenate([given[pre + k].reshape(1, D_MODEL) for k in _GAINS] + [jnp.ones((8 - len(_GAINS), D_MODEL), F32)], axis=0)
    res = _adamw(gain_parts, stack(""), stack("m_"), stack("v_"), "adamw_gains")
    for j, k in enumerate(_GAINS):
        out[k] = [r[j:j + 1] for r in res]

    loss = lax.psum(loss_vec[0, 0], ("x", "y", "c"))
    per_kind = [[out[k][j].reshape(given[k].shape) for k in _WEIGHTS] for j in range(4)]
    return (loss, dx.reshape(x.shape), *per_kind[0], *per_kind[1], *per_kind[2], *per_kind[3])


def _forward_backward(xs, ps, target, gain, send, full=None):
    exchange = full is None
    full = dict(full or {})
    grads, landed = {}, {}

    def gather(names):
        return ("gather", [send[k] for k in names]) if exchange else None

    def scatter(names):
        return ("scatter", [grads[k] for k in names]) if exchange else None

    def keep(into, names, got):
        into.update(zip(names, got))

    first = ("ffn1_w_in", "ffn1_w_out")
    (n1,), got = _prenorm(xs, gain["ffn1_norm"], ride=gather(first))
    keep(full, first, got)
    w1_in, w1_out = full["ffn1_w_in"], full["ffn1_w_out"].reshape(N_FF_CHUNKS, FF_CHUNK, D_MODEL)
    mixer = ("w_mix_in", "conv_w", "w_conv_out", "w_attn_out", "w_mix_out")
    (h1, u, gate1, up1), got = _ffn_fwd(xs, n1, w1_in, w1_out, gain["mix_norm"], "ffn1_fwd", ride=gather(mixer))
    keep(full, mixer, got)
    w_mix = full["w_mix_in"]
    w_co, w_ao, w_mo = (full[k].reshape(D_MODEL, D_MODEL) for k in ("w_conv_out", "w_attn_out", "w_mix_out"))
    taps = _columns_from_shards(full["conv_w"][:, :3, :])
    rest = ("ffn2_w_in", "ffn2_w_out", "w_ple_gate", "w_ple_proj")
    (proj,), got = _mix_proj(u, w_mix, ride=gather(rest))
    keep(full, rest, got)
    w2_in, w2_out = full["ffn2_w_in"], full["ffn2_w_out"].reshape(N_FF_CHUNKS, FF_CHUNK, D_MODEL)
    w_pg = full["w_ple_gate"].reshape(D_MODEL, D_MODEL)
    w_pp = _columns_from_shards(full["w_ple_proj"])
    o = _attn_fwd(proj)
    h2, n3, ycin, yc, ya, merged = _mixer_out(proj, o, h1, taps, w_co, w_ao, w_mo, gain["ffn2_norm"])
    (h3, n4, gate2, up2), _ = _ffn_fwd(h2, n3, w2_in, w2_out, gain["ple_norm"], "ffn2_fwd")
    dh3, ds, dpp, loss_vec, dg_final, dg_ple = _tail(h3, n4, ps, w_pg, w_pp, gain["ple_norm"], gain["final_norm"], target)

    one = lambda a: a[None]
    by_rows = lambda g, rows: g.reshape(N_SHARDS, rows // N_SHARDS, D_MODEL)
    grads["w_ple_gate"] = by_rows(_wgrad(one(n4), one(ds), "wgrad_ple_gate"), D_MODEL)
    grads["w_ple_proj"] = _shards_from_columns(_wgrad(one(ps), one(dpp), "wgrad_ple_proj")[0])
    ple = ("w_ple_gate", "w_ple_proj")
    (dh2, df2, act2, dgate2, dup2, dg_ffn2), got = _ffn_bwd(dh3, h2, gain["ffn2_norm"], gate2, up2, w2_in, w2_out, "ffn2_bwd",
                                                             ride=scatter(ple))
    keep(landed, ple, got)
    grads["ffn2_w_out"] = by_rows(_wgrad(act2, one(df2), "wgrad_ffn2_out"), D_FF)
    grads["ffn2_w_in"] = _wgrad_pieces(n3, [dgate2, dup2], "wgrad_ffn2_in")
    (dh2b, dyc, dya, dgc, dga, dcb, dcv, d_o), got = _mixer_bwd(dh2, proj, yc, ya, taps, w_co, w_ao, w_mo, ride=scatter(("ffn2_w_out",)))
    keep(landed, ("ffn2_w_out",), got)
    grads["w_mix_out"] = by_rows(_wgrad(one(merged), one(dh2b), "wgrad_mix_out"), D_MODEL)
    grads["w_conv_out"] = by_rows(_wgrad(one(ycin), one(dyc), "wgrad_conv_out"), D_MODEL)
    grads["w_attn_out"] = by_rows(_wgrad(one(o), one(dya), "wgrad_attn_out"), D_MODEL)
    dcc, dcx, dtaps = _conv_bwd(dcv, proj, taps)
    grads["conv_w"] = jnp.pad(_shards_from_columns(dtaps[:3]), ((0, 0), (0, CONV_ROWS - 3), (0, 0)))
    behind_attn = ("ffn2_w_in", "w_mix_out", "w_conv_out", "w_attn_out", "conv_w")
    (dq, dk, dv), got = _attn_bwd(proj, o, d_o, ride=scatter(behind_attn))
    keep(landed, behind_attn, got)
    dpieces = [dcb, dcc, dcx, dq, dk, dv, dgc, dga]
    half = N_MIX // 2
    grads["w_mix_in"] = jnp.concatenate([_wgrad_pieces(u, [one(dp) for dp in dpieces[:half]], "wgrad_mix_in_a", tile=WGRAD_TILE // 2),
                                         _wgrad_pieces(u, [one(dp) for dp in dpieces[half:]], "wgrad_mix_in_b", tile=WGRAD_TILE // 2)], axis=0)
    (dh1, dg_mix), got = _mix_bwd(dpieces, w_mix, h1, dh2, gain["mix_norm"], ride=scatter(("w_mix_in",)))
    keep(landed, ("w_mix_in",), got)
    (dx, df1, act1, dgate1, dup1, dg_ffn1), _ = _ffn_bwd(dh1, xs, gain["ffn1_norm"], gate1, up1, w1_in, w1_out, "ffn1_bwd")
    grads["ffn1_w_out"] = by_rows(_wgrad(act1, one(df1), "wgrad_ffn1_out"), D_FF)
    if exchange:
        grads["ffn1_w_in"], got = _wgrad_pieces(n1, [dgate1, dup1], "wgrad_ffn1_in", ride=scatter(("ffn1_w_out",)))
        keep(landed, ("ffn1_w_out",), got)
    else:
        grads["ffn1_w_in"] = _wgrad_pieces(n1, [dgate1, dup1], "wgrad_ffn1_in")
    if exchange:
        keep(landed, ("ffn1_w_in",), _exchange_alone("scatter", [grads["ffn1_w_in"]], "scatter_ffn1_in"))
    gain_grads = dict(ffn1_norm=dg_ffn1, mix_norm=dg_mix, ffn2_norm=dg_ffn2, ple_norm=dg_ple, final_norm=dg_final)
    return loss_vec, dx, (landed if exchange else grads), gain_grads
```

```python
import functools
import math

import jax
import jax.numpy as jnp
from jax import lax
from jax.experimental import pallas as pl
from jax.experimental.pallas import tpu as pltpu

D_MODEL = 1024
D_FF = 2816
N_SHARDS = 8
FF_CHUNK = 2 * D_FF // N_SHARDS
N_FF_CHUNKS = D_FF // FF_CHUNK
N_HEADS = 8
HEAD_DIM = 128
PLE_DIM = 256
NORM_EPS = 1e-6
N_MIX = 8
ADAM_LR, ADAM_B1, ADAM_B2, ADAM_EPS, ADAM_WD, ADAM_STEP = 0.001, 0.9, 0.999, 1e-08, 0.01, 10

TOKEN_TILE = 512
WGRAD_TILE = 4096
PROJ_TILE = 2048
ATTN_ROWS = 512
ATTN_Q = 128
ATTN_SUB = 128
ATTN_K = 3 * ATTN_SUB
ATTN_SKIP_BELOW = -90.0

BF = jnp.bfloat16
F32 = jnp.float32
MESH = pl.DeviceIdType.MESH
NT = (((1,), (1,)), ((), ()))
TN = (((0,), (0,)), ((), ()))
S = jax.ShapeDtypeStruct
ANY = pl.BlockSpec(memory_space=pl.ANY)


def _const_spec(shape):
    nd = len(shape)
    return pl.BlockSpec(shape, lambda *_: (0,) * nd, pipeline_mode=pl.Buffered(1))


def _rows(tm, cols):
    return pl.BlockSpec((tm, cols), lambda i: (i, 0))


def _chunks(tm):
    return pl.BlockSpec((N_FF_CHUNKS, tm, FF_CHUNK), lambda i: (0, i, 0))


def _acc_spec(shape):
    nd = len(shape)
    return pl.BlockSpec(shape, lambda *_: (0,) * nd)


def _dot(a, b):
    return jnp.dot(a, b, preferred_element_type=F32)


def _dot_nt(a, b):
    return lax.dot_general(a, b, NT, preferred_element_type=F32)


def _dot_tn(a, b):
    return lax.dot_general(a, b, TN, preferred_element_type=F32)


def _rms(h, g):
    r = lax.rsqrt(jnp.mean(h * h, axis=-1, keepdims=True) + NORM_EPS)
    return h * r * g


def _rms_bwd(dn, h, g):
    r = lax.rsqrt(jnp.mean(h * h, axis=-1, keepdims=True) + NORM_EPS)
    nh = h * r
    gd = dn * g
    dh = r * (gd - nh * jnp.mean(gd * nh, axis=-1, keepdims=True))
    return dh, jnp.sum(dn * nh, axis=0, keepdims=True)


def _accumulate(ref, val):
    @pl.when(pl.program_id(0) == 0)
    def _():
        ref[...] = jnp.zeros_like(ref)
    ref[...] += val


def _place():
    x, y, c = lax.axis_index("x"), lax.axis_index("y"), lax.axis_index("c")
    return x, y, c


def _slot(px, py, pc):
    return 4 * px + 2 * py + pc


def _gather_phases(ins, outs, send_sems, recv_sems, local_sems):
    n = len(ins)

    def parties():
        x, y, c = _place()
        return (x, y, c), (x, y, 1 - c), [(1 - x, y), (x, 1 - y), (1 - x, 1 - y)], c

    def copy(a, k, block, to, src=None):
        dst = outs[a].at[_slot(*block)]
        return pltpu.make_async_remote_copy(
            src_ref=dst if src is None else src, dst_ref=dst,
            send_sem=send_sems.at[a, k], recv_sem=recv_sems.at[a, k],
            device_id=to, device_id_type=MESH)

    def own(a, me):
        return pltpu.make_async_copy(ins[a], outs[a].at[_slot(*me)], local_sems.at[a])

    def first(a, me, sibling, chips, c):
        return [copy(a, 0, me, sibling, src=ins[a])] + [copy(a, 1 + j, me, (*chip, c), src=ins[a]) for j, chip in enumerate(chips)]

    def start():
        me, sibling, chips, c = parties()
        for a in range(n):
            own(a, me).start()
        for a in range(n):
            for cp in first(a, me, sibling, chips, c):
                cp.start()

    def forward():
        me, sibling, chips, c = parties()
        for j, chip in enumerate(chips):
            for a in range(n):
                copy(a, 1 + j, (*chip, c), me).wait_recv()
                copy(a, 4 + j, (*chip, c), sibling).start()

    def finish():
        me, sibling, chips, c = parties()
        for a in range(n):
            copy(a, 0, sibling, me).wait_recv()
            for j, chip in enumerate(chips):
                copy(a, 4 + j, (*chip, 1 - c), me).wait_recv()
        for a in range(n):
            for cp in first(a, me, sibling, chips, c) + [copy(a, 4 + j, (*chip, c), sibling) for j, chip in enumerate(chips)]:
                cp.wait_send()
            own(a, me).wait()

    return [start, forward, finish]


def _scatter_phases(ins, outs, send_sems, recv_sems, local_sems):
    n = len(ins)

    def copies():
        x, y, c = _place()
        me = _slot(x, y, c)
        out = [pltpu.make_async_copy(ins[a].at[me], outs[a].at[me], local_sems.at[a]) for a in range(n)]
        for k in range(1, N_SHARDS):
            px = 1 - x if k & 4 else x
            py = 1 - y if k & 2 else y
            pc = 1 - c if k & 1 else c
            for a in range(n):
                out.append(pltpu.make_async_remote_copy(
                    src_ref=ins[a].at[_slot(px, py, pc)], dst_ref=outs[a].at[me],
                    send_sem=send_sems.at[a, k - 1], recv_sem=recv_sems.at[a, k - 1],
                    device_id=(px, py, pc), device_id_type=MESH))
        return out

    def start():
        for cp in copies():
            cp.start()

    def finish():
        for cp in copies():
            cp.wait()

    return [start, finish]


def _pallas(body, *, name, grid, in_specs, out_specs, out_shape, args, scratch_shapes=(), ride=None):
    if ride is None:
        outs = pl.pallas_call(body, name=name, grid=grid, in_specs=in_specs, out_specs=out_specs, out_shape=out_shape,
                              scratch_shapes=list(scratch_shapes))(*args)
        return list(outs), []
    kind, arrays = ride
    n, n_in, n_out, n_scr = len(arrays), len(in_specs), len(out_specs), len(scratch_shapes)
    total = math.prod(grid)
    middle = (2 * total) // 3
    landed_shape = [S((N_SHARDS,) + a.shape if kind == "gather" else a.shape, a.dtype) for a in arrays]

    def with_exchange(*refs):
        ins, riders_in = refs[:n_in], refs[n_in:n_in + n]
        outs, riders_out = refs[n_in + n:n_in + n + n_out], refs[n_in + n + n_out:n_in + 2 * n + n_out]
        scratch, sems = refs[n_in + 2 * n + n_out:n_in + 2 * n + n_out + n_scr], refs[n_in + 2 * n + n_out + n_scr:]
        step = 0
        for axis, size in enumerate(grid):
            step = step * size + pl.program_id(axis)
        phases = (_gather_phases if kind == "gather" else _scatter_phases)(riders_in, riders_out, *sems)
        pl.when(step == 0)(phases[0])
        body(*ins, *outs, *scratch)
        for phase in phases[1:-1]:
            pl.when(step == middle)(phase)
        pl.when(step == total - 1)(phases[-1])

    outs = pl.pallas_call(
        with_exchange, name=name, grid=grid,
        in_specs=list(in_specs) + [ANY] * n, out_specs=list(out_specs) + [ANY] * n,
        out_shape=list(out_shape) + landed_shape,
        scratch_shapes=list(scratch_shapes) + [pltpu.SemaphoreType.DMA((n, 7)), pltpu.SemaphoreType.DMA((n, 7)),
                                               pltpu.SemaphoreType.DMA((n,))],
    )(*args, *arrays)
    return list(outs[:n_out]), list(outs[n_out:])


def _exchange_alone(kind, arrays, name):
    return _pallas(lambda: None, name=name, grid=(1,), in_specs=[], out_specs=[], out_shape=[], args=[], ride=(kind, arrays))[1]


def _prenorm(x, g, ride=None):
    t = x.shape[0]
    tm = min(TOKEN_TILE, t)

    def body(x_ref, g_ref, n_ref):
        n_ref[...] = _rms(x_ref[...], g_ref[...]).astype(BF)

    return _pallas(
        body, name="prenorm", grid=(t // tm,), ride=ride,
        in_specs=[_rows(tm, D_MODEL), _const_spec((1, D_MODEL))], out_specs=[_rows(tm, D_MODEL)],
        out_shape=[S((t, D_MODEL), BF)], args=[x, g])


def _ffn_fwd(h, n, w_in, w_out, g_next, name, ride=None):
    t = h.shape[0]
    tm = min(TOKEN_TILE, t)

    def body(h_ref, n_ref, win_ref, wout_ref, g_ref, ho_ref, no_ref, gate_ref, up_ref):
        nb = n_ref[...]
        acc = jnp.zeros((tm, D_MODEL), F32)
        for c in range(N_FF_CHUNKS):
            gate = _dot(nb, win_ref[c])
            up = _dot(nb, win_ref[N_FF_CHUNKS + c])
            gate_ref[c] = gate.astype(BF)
            up_ref[c] = up.astype(BF)
            act = (gate * jax.nn.sigmoid(gate) * up).astype(BF)
            acc = acc + _dot(act, wout_ref[c])
        ho = h_ref[...] + 0.5 * acc
        ho_ref[...] = ho
        no_ref[...] = _rms(ho, g_ref[...]).astype(BF)

    return _pallas(
        body, name=name, grid=(t // tm,), ride=ride,
        in_specs=[_rows(tm, D_MODEL), _rows(tm, D_MODEL), _const_spec(w_in.shape), _const_spec(w_out.shape),
                  _const_spec((1, D_MODEL))],
        out_specs=[_rows(tm, D_MODEL), _rows(tm, D_MODEL), _chunks(tm), _chunks(tm)],
        out_shape=[S((t, D_MODEL), F32), S((t, D_MODEL), BF),
                   S((N_FF_CHUNKS, t, FF_CHUNK), BF), S((N_FF_CHUNKS, t, FF_CHUNK), BF)],
        args=[h, n, w_in, w_out, g_next])


def _mix_proj(u, w_mix, ride=None):
    t = u.shape[0]
    tm = min(PROJ_TILE, t)

    def body(u_ref, w_ref, o_ref):
        o_ref[0] = _dot(u_ref[...], w_ref[0]).astype(BF)

    return _pallas(
        body, name="mix_proj", grid=(N_MIX, t // tm), ride=ride,
        in_specs=[pl.BlockSpec((tm, D_MODEL), lambda d, i: (i, 0)), pl.BlockSpec((1, D_MODEL, D_MODEL), lambda d, i: (d, 0, 0))],
        out_specs=[pl.BlockSpec((1, tm, D_MODEL), lambda d, i: (d, i, 0))],
        out_shape=[S((N_MIX, t, D_MODEL), BF)], args=[u, w_mix])


HALO = 16


def _piece(d, tm):
    return pl.BlockSpec((1, tm, D_MODEL), lambda i: (d, i, 0))


def _prev_halo(d, tm):
    return pl.BlockSpec((1, HALO, D_MODEL), lambda i: (d, jnp.maximum(i * (tm // HALO) - 1, 0), 0))


def _shift_down(m, prev_tail, k):
    tm = m.shape[0]
    out = pltpu.roll(m, k, 0)
    row = lax.broadcasted_iota(jnp.int32, (tm, 1), 0)
    for j in range(k):
        out = jnp.where(row == j, prev_tail[HALO - k + j:HALO - k + j + 1, :], out)
    return out


def _conv_inputs(cc_ref, cx_ref, cch_ref, cxh_ref):
    m = cc_ref[0].astype(F32) * cx_ref[0].astype(F32)
    mh = cch_ref[0].astype(F32) * cxh_ref[0].astype(F32)
    mh = jnp.where(pl.program_id(0) == 0, 0.0, mh)
    return m, _shift_down(m, mh, 1), _shift_down(m, mh, 2)


def _mixer_out(proj, o, h1, conv_w, w_co, w_ao, w_mo, g_next):
    t = h1.shape[0]
    tm = min(TOKEN_TILE, t)

    def body(cb_ref, cc_ref, cx_ref, gc_ref, ga_ref, cch_ref, cxh_ref, o_ref, h_ref, cw_ref, wco_ref, wao_ref, wmo_ref,
             g_ref, ho_ref, no_ref, ycin_ref, yc_ref, ya_ref, mg_ref):
        m, m1, m2 = _conv_inputs(cc_ref, cx_ref, cch_ref, cxh_ref)
        cw = cw_ref[...]
        cv = cw[0:1, :] * m2 + cw[1:2, :] * m1 + cw[2:3, :] * m
        ycin = (cb_ref[0].astype(F32) * cv).astype(BF)
        ycin_ref[...] = ycin
        yc = _dot(ycin, wco_ref[...])
        ya = _dot(o_ref[...].astype(BF), wao_ref[...])
        yc_ref[...] = yc.astype(BF)
        ya_ref[...] = ya.astype(BF)
        merged = (jax.nn.sigmoid(gc_ref[0].astype(F32)) * yc + jax.nn.sigmoid(ga_ref[0].astype(F32)) * ya).astype(BF)
        mg_ref[...] = merged
        ho = h_ref[...] + _dot(merged, wmo_ref[...])
        ho_ref[...] = ho
        no_ref[...] = _rms(ho, g_ref[...]).astype(BF)

    sq = (D_MODEL, D_MODEL)
    return pl.pallas_call(
        body, name="mixer_out", grid=(t // tm,),
        in_specs=[_piece(0, tm), _piece(1, tm), _piece(2, tm), _piece(6, tm), _piece(7, tm), _prev_halo(1, tm), _prev_halo(2, tm),
                  _rows(tm, D_MODEL), _rows(tm, D_MODEL), _const_spec((3, D_MODEL)), _const_spec(sq), _const_spec(sq),
                  _const_spec(sq), _const_spec((1, D_MODEL))],
        out_specs=[_rows(tm, D_MODEL)] * 6,
        out_shape=[S((t, D_MODEL), F32)] + [S((t, D_MODEL), BF)] * 5,
    )(proj, proj, proj, proj, proj, proj, proj, o, h1, conv_w, w_co, w_ao, w_mo, g_next)


def _suffix_sums(vals, tri, before):
    out, right = [], before
    for b in reversed(range(ATTN_K // ATTN_SUB)):
        v = vals[:, b * ATTN_SUB:(b + 1) * ATTN_SUB]
        hi = v.astype(BF)
        lo = (v - hi.astype(F32)).astype(BF)
        out.append(_dot(hi, tri) + _dot(lo, tri) + right)
        right = right + jnp.sum(v, axis=1, keepdims=True)
    return jnp.concatenate(out[::-1], axis=1), right


ATTN_UNITS = ATTN_ROWS // ATTN_Q


def _unit_rows(x, u):
    return x[u * ATTN_Q:(u + 1) * ATTN_Q]


def _per_unit(fn):
    return jnp.concatenate([fn(u) for u in range(ATTN_UNITS)], axis=0)


def _per_row(vals):
    local = lax.broadcasted_iota(jnp.int32, (ATTN_ROWS, 1), 0)
    out = jnp.full((ATTN_ROWS, 1), vals[0], jnp.int32)
    for u in range(1, ATTN_UNITS):
        out = jnp.where(local >= u * ATTN_Q, vals[u], out)
    return out


def _attn_step(q, k_ref, starts, bounds, row):
    z = _per_unit(lambda u: _dot_nt(_unit_rows(q, u), k_ref[0, pl.ds(starts[u], ATTN_K), :])) * (1.0 / math.sqrt(HEAD_DIM))
    col = _per_row(starts) + lax.broadcasted_iota(jnp.int32, (1, ATTN_K), 1)
    mask = jnp.logical_and(col < row, col < _per_row(bounds))
    soft = jnp.log(1.0 + jnp.exp(-jnp.abs(z)))
    log_beta = jnp.minimum(z, 0.0) - soft
    log_rest = jnp.where(mask, jnp.minimum(-z, 0.0) - soft, 0.0)
    return z, mask, log_beta, log_rest


def _attn_sweep_start(i, t):
    blks = tuple(jnp.maximum(i * ATTN_UNITS + u + 1 - ATTN_K // ATTN_SUB, 0) for u in range(ATTN_UNITS))
    return blks, tuple(jnp.int32(t) for _ in range(ATTN_UNITS))


def _attn_keys(blks):
    return [pl.multiple_of(b * ATTN_SUB, ATTN_SUB) for b in blks]


def _attn_next(blks):
    return tuple(jnp.maximum(b - ATTN_K // ATTN_SUB, 0) for b in blks), tuple(b * ATTN_SUB for b in blks)


def _attn_more(carry):
    return jnp.logical_and(carry[1][ATTN_UNITS - 1] > 0, carry[-1] > ATTN_SKIP_BELOW)


def _tri(strict):
    r = lax.broadcasted_iota(jnp.int32, (ATTN_SUB, ATTN_SUB), 0)
    c = lax.broadcasted_iota(jnp.int32, (ATTN_SUB, ATTN_SUB), 1)
    return (r > c if strict else r >= c).astype(BF)


def _head_cols(piece):
    return lambda t: pl.BlockSpec((1, t, HEAD_DIM), lambda h, i: (piece, 0, h))


def _attn_fwd(proj):
    t = proj.shape[1]
    nq = t // ATTN_ROWS
    tri = _tri(strict=True)

    def body(q_ref, k_ref, v_ref, tri_ref, o_ref):
        i = pl.program_id(1)
        q = q_ref[0]
        row = i * ATTN_ROWS + lax.broadcasted_iota(jnp.int32, (ATTN_ROWS, 1), 0)

        def step(carry):
            blks, bounds, acc, run, _ = carry
            starts = _attn_keys(blks)
            _, mask, log_beta, log_rest = _attn_step(q, k_ref, starts, bounds, row)
            tail, run = _suffix_sums(log_rest, tri_ref[...], run)
            a = jnp.where(mask, jnp.exp(log_beta + tail), 0.0).astype(BF)
            acc = acc + _per_unit(lambda u: _dot(_unit_rows(a, u), v_ref[0, pl.ds(starts[u], ATTN_K), :]))
            return (*_attn_next(blks), acc, run, jnp.max(run))

        first = (*_attn_sweep_start(i, t), jnp.zeros((ATTN_ROWS, HEAD_DIM), F32), jnp.zeros((ATTN_ROWS, 1), F32), jnp.float32(0.0))
        o_ref[...] = lax.while_loop(_attn_more, step, step(first))[2]

    qspec = pl.BlockSpec((1, ATTN_ROWS, HEAD_DIM), lambda h, i: (3, i, h))
    return pl.pallas_call(
        body, name="attn_fwd", grid=(N_HEADS, nq),
        in_specs=[qspec, _head_cols(4)(t), _head_cols(5)(t), pl.BlockSpec((ATTN_SUB, ATTN_SUB), lambda h, i: (0, 0))],
        out_specs=pl.BlockSpec((ATTN_ROWS, HEAD_DIM), lambda h, i: (i, h)),
        out_shape=S((t, D_MODEL), F32))(proj, proj, proj, tri)


def _attn_bwd(proj, o, d_o, ride=None):
    t = proj.shape[1]
    nq = t // ATTN_ROWS
    tri_strict, tri_incl = _tri(strict=True), _tri(strict=False)

    def body(q_ref, k_ref, v_ref, o_ref, do_ref, tris_ref, trii_ref, dq_ref, dk_ref, dv_ref, dk_acc, dv_acc):
        i = pl.program_id(1)

        @pl.when(i == 0)
        def _():
            dk_acc[...] = jnp.zeros_like(dk_acc)
            dv_acc[...] = jnp.zeros_like(dv_acc)

        q = q_ref[0]
        do = do_ref[...]
        total = jnp.sum(do.astype(F32) * o_ref[...], axis=1, keepdims=True)
        row = i * ATTN_ROWS + lax.broadcasted_iota(jnp.int32, (ATTN_ROWS, 1), 0)

        def step(carry):
            blks, bounds, dq, seen, run, _ = carry
            starts = _attn_keys(blks)
            keys = lambda ref, u: ref[0, pl.ds(starts[u], ATTN_K), :]
            z, mask, log_beta, log_rest = _attn_step(q, k_ref, starts, bounds, row)
            tail, run = _suffix_sums(log_rest, tris_ref[...], run)
            a = jnp.where(mask, jnp.exp(log_beta + tail), 0.0).astype(BF)
            de = _per_unit(lambda u: _dot_nt(_unit_rows(do, u), keys(v_ref, u))) * a.astype(F32)
            right, seen = _suffix_sums(de, trii_ref[...], seen)
            beta = jax.nn.sigmoid(z)
            dz = jnp.where(mask, de * (1.0 - beta) - (total - right) * beta, 0.0) * (1.0 / math.sqrt(HEAD_DIM))
            dzb = dz.astype(BF)
            for u in range(ATTN_UNITS):
                dv_acc[pl.ds(starts[u], ATTN_K), :] += _dot_tn(_unit_rows(a, u), _unit_rows(do, u))
                dk_acc[pl.ds(starts[u], ATTN_K), :] += _dot_tn(_unit_rows(dzb, u), _unit_rows(q, u))
            dq = dq + _per_unit(lambda u: _dot(_unit_rows(dzb, u), keys(k_ref, u)))
            return (*_attn_next(blks), dq, seen, run, jnp.max(run))

        zero = jnp.zeros((ATTN_ROWS, 1), F32)
        first = (*_attn_sweep_start(i, t), jnp.zeros((ATTN_ROWS, HEAD_DIM), F32), zero, zero, jnp.float32(0.0))
        dq_ref[...] = lax.while_loop(_attn_more, step, step(first))[2].astype(BF)

        @pl.when(i == nq - 1)
        def _():
            dk_ref[...] = dk_acc[...].astype(BF)
            dv_ref[...] = dv_acc[...].astype(BF)

    qspec = pl.BlockSpec((1, ATTN_ROWS, HEAD_DIM), lambda h, i: (3, i, h))
    rowblk = pl.BlockSpec((ATTN_ROWS, HEAD_DIM), lambda h, i: (i, h))
    head = pl.BlockSpec((t, HEAD_DIM), lambda h, i: (0, h))
    trispec = pl.BlockSpec((ATTN_SUB, ATTN_SUB), lambda h, i: (0, 0))
    return _pallas(
        body, name="attn_bwd", grid=(N_HEADS, nq), ride=ride,
        in_specs=[qspec, _head_cols(4)(t), _head_cols(5)(t), rowblk, rowblk, trispec, trispec],
        out_specs=[rowblk, head, head],
        out_shape=[S((t, D_MODEL), BF)] * 3,
        scratch_shapes=[pltpu.VMEM((t, HEAD_DIM), F32), pltpu.VMEM((t, HEAD_DIM), F32)],
        args=[proj, proj, proj, o, d_o, tri_strict, tri_incl])


def _tail(h3, n4, p, w_pg, w_pp, g_ple, g_final, target):
    t = h3.shape[0]
    tm = min(TOKEN_TILE, t)
    steps = t // tm

    def body(h_ref, n_ref, p_ref, wpg_ref, wpp_ref, gp_ref, gf_ref, tgt_ref,
             dh_ref, ds_ref, dpp_ref, loss_ref, dgf_ref, dgp_ref):
        pg = jax.nn.sigmoid(_dot(n_ref[...], wpg_ref[...]))
        pp = _dot(p_ref[...].astype(BF), wpp_ref[...])
        h3v = h_ref[...]
        h4 = h3v + pg * pp
        gf = gf_ref[...]
        diff = _rms(h4, gf) - tgt_ref[...]
        _accumulate(loss_ref, jnp.sum(diff * diff, axis=0, keepdims=True))
        dh4, dgf = _rms_bwd(diff * (1.0 / D_MODEL), h4, gf)
        _accumulate(dgf_ref, dgf)
        dpp_ref[...] = (dh4 * pg).astype(BF)
        ds = (dh4 * pp * pg * (1.0 - pg)).astype(BF)
        ds_ref[...] = ds
        dh3, dgp = _rms_bwd(_dot_nt(ds, wpg_ref[...]), h3v, gp_ref[...])
        _accumulate(dgp_ref, dgp)
        dh_ref[...] = dh4 + dh3

        @pl.when(pl.program_id(0) == steps - 1)
        def _():
            loss_ref[...] = jnp.full(loss_ref.shape, 0.5 / D_MODEL * jnp.sum(loss_ref[...]), F32)

    vec = (1, D_MODEL)
    return pl.pallas_call(
        body, name="tail", grid=(steps,),
        in_specs=[_rows(tm, D_MODEL), _rows(tm, D_MODEL), _rows(tm, PLE_DIM), _const_spec((D_MODEL, D_MODEL)),
                  _const_spec((PLE_DIM, D_MODEL)), _const_spec(vec), _const_spec(vec), _rows(tm, D_MODEL)],
        out_specs=[_rows(tm, D_MODEL)] * 3 + [_acc_spec(vec)] * 3,
        out_shape=[S((t, D_MODEL), F32), S((t, D_MODEL), BF), S((t, D_MODEL), BF)] + [S(vec, F32)] * 3,
    )(h3, n4, p, w_pg, w_pp, g_ple, g_final, target)


def _wgrad(xs, ys, name, ride=None):
    bx, t, k = xs.shape
    by, _, n = ys.shape
    b = max(bx, by)
    tt = min(WGRAD_TILE * 2 // xs.dtype.itemsize, t)
    steps = t // tt

    def body(x_ref, y_ref, o_ref, acc_ref):
        s = pl.program_id(1)

        @pl.when(s == 0)
        def _():
            acc_ref[...] = jnp.zeros_like(acc_ref)
        acc_ref[...] += _dot_tn(x_ref[0].astype(BF), y_ref[0].astype(BF))

        @pl.when(s == steps - 1)
        def _():
            o_ref[0] = acc_ref[...].astype(BF)

    (out,), landed = _pallas(
        body, name=name, grid=(b, steps), ride=ride,
        in_specs=[pl.BlockSpec((1, tt, k), (lambda j, s: (j, s, 0)) if bx > 1 else (lambda j, s: (0, s, 0))),
                  pl.BlockSpec((1, tt, n), (lambda j, s: (j, s, 0)) if by > 1 else (lambda j, s: (0, s, 0)))],
        out_specs=[pl.BlockSpec((1, k, n), lambda j, s: (j, 0, 0))],
        out_shape=[S((b, k, n), BF)],
        scratch_shapes=[pltpu.VMEM((k, n), F32)],
        args=[xs, ys])
    return (out, landed) if ride is not None else out


def _wgrad_pieces(x, ys, name, ride=None, tile=None):
    t, k = x.shape
    n = ys[0].shape[2]
    counts = [y.shape[0] for y in ys]
    offsets = [sum(counts[:j]) for j in range(len(ys))]
    total = sum(counts)
    tt = min(tile or WGRAD_TILE, t)
    steps = t // tt

    def body(x_ref, *refs):
        y_refs, o_ref, acc_ref = refs[:len(ys)], refs[len(ys)], refs[len(ys) + 1]
        p, s = pl.program_id(0), pl.program_id(1)

        @pl.when(s == 0)
        def _():
            acc_ref[...] = jnp.zeros_like(acc_ref)
        for j, y_ref in enumerate(y_refs):
            @pl.when(jnp.logical_and(p >= offsets[j], p < offsets[j] + counts[j]))
            def _(y_ref=y_ref):
                acc_ref[...] += _dot_tn(x_ref[...], y_ref[0])

        @pl.when(s == steps - 1)
        def _():
            o_ref[0] = acc_ref[...].astype(BF)

    def turn(j):
        lo, hi = offsets[j], offsets[j] + counts[j]
        return lambda p, s: (jnp.clip(p - lo, 0, counts[j] - 1), jnp.where(p < lo, 0, jnp.where(p >= hi, steps - 1, s)), 0)

    (out,), landed = _pallas(
        body, name=name, grid=(total, steps), ride=ride,
        in_specs=[pl.BlockSpec((tt, k), lambda p, s: (s, 0))] + [pl.BlockSpec((1, tt, n), turn(j)) for j in range(len(ys))],
        out_specs=[pl.BlockSpec((1, k, n), lambda p, s: (p, 0, 0))],
        out_shape=[S((total, k, n), BF)],
        scratch_shapes=[pltpu.VMEM((k, n), F32)],
        args=[x, *ys])
    return (out, landed) if ride is not None else out


def _ffn_bwd_hidden(dh, gate, up, w_out, name, ride=None):
    t = dh.shape[0]
    tm = min(TOKEN_TILE, t)

    def body(dh_ref, gate_ref, up_ref, wout_ref, df_ref, act_ref, dgate_ref, dup_ref):
        df = (0.5 * dh_ref[...]).astype(BF)
        df_ref[...] = df
        for c in range(N_FF_CHUNKS):
            gt = gate_ref[c].astype(F32)
            u = up_ref[c].astype(F32)
            sg = jax.nn.sigmoid(gt)
            silu = gt * sg
            act_ref[c] = (silu * u).astype(BF)
            dact = _dot_nt(df, wout_ref[c])
            dgate_ref[c] = (dact * u * (sg * (1.0 + gt * (1.0 - sg)))).astype(BF)
            dup_ref[c] = (dact * silu).astype(BF)

    return _pallas(
        body, name=name, grid=(t // tm,), ride=ride,
        in_specs=[_rows(tm, D_MODEL), _chunks(tm), _chunks(tm), _const_spec(w_out.shape)],
        out_specs=[_rows(tm, D_MODEL), _chunks(tm), _chunks(tm), _chunks(tm)],
        out_shape=[S((t, D_MODEL), BF)] + [S((N_FF_CHUNKS, t, FF_CHUNK), BF)] * 3,
        args=[dh, gate, up, w_out])


def _ffn_bwd_input(dh, h_in, g, dgate, dup, w_in, name, ride=None):
    t = dh.shape[0]
    tm = min(TOKEN_TILE, t)

    def body(dh_ref, h_ref, g_ref, dgate_ref, dup_ref, win_ref, dhi_ref, dg_ref):
        dn = jnp.zeros((tm, D_MODEL), F32)
        for c in range(N_FF_CHUNKS):
            dn = dn + _dot_nt(dgate_ref[c], win_ref[c]) + _dot_nt(dup_ref[c], win_ref[N_FF_CHUNKS + c])
        dhi, dg = _rms_bwd(dn, h_ref[...], g_ref[...])
        _accumulate(dg_ref, dg)
        dhi_ref[...] = dh_ref[...] + dhi

    vec = (1, D_MODEL)
    return _pallas(
        body, name=name, grid=(t // tm,), ride=ride,
        in_specs=[_rows(tm, D_MODEL), _rows(tm, D_MODEL), _const_spec(vec), _chunks(tm), _chunks(tm), _const_spec(w_in.shape)],
        out_specs=[_rows(tm, D_MODEL), _acc_spec(vec)],
        out_shape=[S((t, D_MODEL), F32), S(vec, F32)],
        args=[dh, h_in, g, dgate, dup, w_in])


def _mixer_bwd(dh2, proj, yc, ya, conv_w, w_co, w_ao, w_mo, ride=None):
    t = dh2.shape[0]
    tm = min(TOKEN_TILE, t)

    def body(dh_ref, cb_ref, cc_ref, cx_ref, gc_ref, ga_ref, cch_ref, cxh_ref, yc_ref, ya_ref, cw_ref, wco_ref, wao_ref, wmo_ref,
             dhb_ref, dyc_ref, dya_ref, dgc_ref, dga_ref, dcb_ref, dcv_ref, do_ref):
        dhb = dh_ref[...].astype(BF)
        dhb_ref[...] = dhb
        dmerged = _dot_nt(dhb, wmo_ref[...])
        sc = jax.nn.sigmoid(gc_ref[0].astype(F32))
        sa = jax.nn.sigmoid(ga_ref[0].astype(F32))
        dyc = (dmerged * sc).astype(BF)
        dya = (dmerged * sa).astype(BF)
        dyc_ref[...] = dyc
        dya_ref[...] = dya
        dgc_ref[...] = (dmerged * yc_ref[...].astype(F32) * sc * (1.0 - sc)).astype(BF)
        dga_ref[...] = (dmerged * ya_ref[...].astype(F32) * sa * (1.0 - sa)).astype(BF)
        m, m1, m2 = _conv_inputs(cc_ref, cx_ref, cch_ref, cxh_ref)
        cw = cw_ref[...]
        cv = cw[0:1, :] * m2 + cw[1:2, :] * m1 + cw[2:3, :] * m
        dycin = _dot_nt(dyc, wco_ref[...])
        dcb_ref[...] = (dycin * cv).astype(BF)
        dcv_ref[...] = dycin * cb_ref[0].astype(F32)
        do_ref[...] = _dot_nt(dya, wao_ref[...]).astype(BF)

    sq = (D_MODEL, D_MODEL)
    return _pallas(
        body, name="mixer_bwd", grid=(t // tm,), ride=ride,
        in_specs=[_rows(tm, D_MODEL), _piece(0, tm), _piece(1, tm), _piece(2, tm), _piece(6, tm), _piece(7, tm),
                  _prev_halo(1, tm), _prev_halo(2, tm), _rows(tm, D_MODEL), _rows(tm, D_MODEL),
                  _const_spec((3, D_MODEL)), _const_spec(sq), _const_spec(sq), _const_spec(sq)],
        out_specs=[_rows(tm, D_MODEL)] * 8,
        out_shape=[S((t, D_MODEL), BF)] * 6 + [S((t, D_MODEL), F32), S((t, D_MODEL), BF)],
        args=[dh2, proj, proj, proj, proj, proj, proj, proj, yc, ya, conv_w, w_co, w_ao, w_mo])


F32_HALO = 8


def _conv_bwd(dcv, proj, conv_w):
    t = dcv.shape[0]
    tm = min(TOKEN_TILE, t)
    steps = t // tm

    def body(dcv_ref, nxt_ref, cc_ref, cx_ref, cch_ref, cxh_ref, cw_ref, dcc_ref, dcx_ref, dw_ref):
        i = pl.program_id(0)
        m, m1, m2 = _conv_inputs(cc_ref, cx_ref, cch_ref, cxh_ref)
        d0 = dcv_ref[...]
        nxt = jnp.where(i == steps - 1, 0.0, nxt_ref[...])
        row = lax.broadcasted_iota(jnp.int32, (tm, 1), 0)
        d1 = jnp.where(row == tm - 1, nxt[0:1, :], pltpu.roll(d0, tm - 1, 0))
        d2 = pltpu.roll(d0, tm - 2, 0)
        d2 = jnp.where(row == tm - 2, nxt[0:1, :], jnp.where(row == tm - 1, nxt[1:2, :], d2))
        cw = cw_ref[...]
        dm = cw[2:3, :] * d0 + cw[1:2, :] * d1 + cw[0:1, :] * d2
        dcc_ref[...] = (dm * cx_ref[0].astype(F32)).astype(BF)
        dcx_ref[...] = (dm * cc_ref[0].astype(F32)).astype(BF)
        tap_row = lax.broadcasted_iota(jnp.int32, (F32_HALO, 1), 0)
        dw = jnp.zeros((F32_HALO, D_MODEL), F32)
        for j, mk in enumerate((m2, m1, m)):
            dw = jnp.where(tap_row == j, jnp.sum(d0 * mk, axis=0, keepdims=True), dw)
        _accumulate(dw_ref, dw)

    nxt_spec = pl.BlockSpec((F32_HALO, D_MODEL), lambda i: (jnp.minimum((i + 1) * (tm // F32_HALO), t // F32_HALO - 1), 0))
    return pl.pallas_call(
        body, name="conv_bwd", grid=(steps,),
        in_specs=[_rows(tm, D_MODEL), nxt_spec, _piece(1, tm), _piece(2, tm), _prev_halo(1, tm), _prev_halo(2, tm),
                  _const_spec((3, D_MODEL))],
        out_specs=[_rows(tm, D_MODEL), _rows(tm, D_MODEL), _acc_spec((F32_HALO, D_MODEL))],
        out_shape=[S((t, D_MODEL), BF), S((t, D_MODEL), BF), S((F32_HALO, D_MODEL), F32)],
    )(dcv, dcv, proj, proj, proj, proj, conv_w)


def _mix_bwd(dpieces, w_mix, h1, dh2, g, ride=None):
    t = h1.shape[0]
    tm = min(TOKEN_TILE, t)

    def body(*refs):
        pieces, (w_ref, h_ref, dh_ref, g_ref, dhi_ref, dg_ref) = refs[:N_MIX], refs[N_MIX:]
        du = jnp.zeros((tm, D_MODEL), F32)
        for d in range(N_MIX):
            du = du + _dot_nt(pieces[d][...], w_ref[d])
        dhi, dg = _rms_bwd(du, h_ref[...], g_ref[...])
        _accumulate(dg_ref, dg)
        dhi_ref[...] = dh_ref[...] + dhi

    vec = (1, D_MODEL)
    return _pallas(
        body, name="mix_bwd", grid=(t // tm,), ride=ride,
        in_specs=[_rows(tm, D_MODEL)] * N_MIX + [_const_spec(w_mix.shape), _rows(tm, D_MODEL), _rows(tm, D_MODEL), _const_spec(vec)],
        out_specs=[_rows(tm, D_MODEL), _acc_spec(vec)],
        out_shape=[S((t, D_MODEL), F32), S(vec, F32)],
        args=[*dpieces, w_mix, h1, dh2, g])


def _adamw(partials, w, m, v, name):
    r, c = w.shape
    tr = min(r, 512)
    c1 = 1.0 - ADAM_B1 ** ADAM_STEP
    c2 = 1.0 - ADAM_B2 ** ADAM_STEP

    def body(p_ref, w_ref, m_ref, v_ref, g_ref, d_ref, mo_ref, vo_ref):
        g = p_ref[0].astype(F32)
        for s in range(1, N_SHARDS):
            g = g + p_ref[s].astype(F32)
        mn = ADAM_B1 * m_ref[...] + (1.0 - ADAM_B1) * g
        vn = ADAM_B2 * v_ref[...] + (1.0 - ADAM_B2) * (g * g)
        g_ref[...] = g
        mo_ref[...] = mn
        vo_ref[...] = vn
        d_ref[...] = -ADAM_LR * ((mn / c1) / (jnp.sqrt(vn / c2) + ADAM_EPS) + ADAM_WD * w_ref[...])

    blk = pl.BlockSpec((tr, c), lambda i: (i, 0))
    return pl.pallas_call(
        body, name=name, grid=(r // tr,),
        in_specs=[pl.BlockSpec((N_SHARDS, tr, c), lambda i: (0, i, 0)), blk, blk, blk],
        out_specs=[blk] * 4, out_shape=[S((r, c), F32)] * 4,
    )(partials, w, m, v)


_MATRICES = ("ffn1_w_in", "ffn1_w_out", "w_mix_in", "conv_w", "w_conv_out", "w_attn_out", "w_mix_out",
             "ffn2_w_in", "ffn2_w_out", "w_ple_gate", "w_ple_proj")
_GAINS = ("ffn1_norm", "mix_norm", "ffn2_norm", "ple_norm", "final_norm")
_WEIGHTS = ("ffn1_norm", "ffn1_w_in", "ffn1_w_out", "mix_norm", "w_mix_in", "conv_w", "w_conv_out", "w_attn_out", "w_mix_out",
            "ffn2_norm", "ffn2_w_in", "ffn2_w_out", "ple_norm", "w_ple_gate", "w_ple_proj", "final_norm")
CONV_ROWS = 8


def _columns_from_shards(g):
    return jnp.transpose(g, (1, 0, 2)).reshape(g.shape[1], N_SHARDS * g.shape[2])


def _shards_from_columns(a):
    r, c = a.shape
    return jnp.transpose(a.reshape(r, N_SHARDS, c // N_SHARDS), (1, 0, 2))


def kernel(x, p, ffn1_norm, ffn1_w_in, ffn1_w_out, mix_norm, w_mix_in, conv_w, w_conv_out, w_attn_out, w_mix_out, ffn2_norm, ffn2_w_in, ffn2_w_out, ple_norm, w_ple_gate, w_ple_proj, final_norm, loss_target, m_ffn1_norm, m_ffn1_w_in, m_ffn1_w_out, m_mix_norm, m_w_mix_in, m_conv_w, m_w_conv_out, m_w_attn_out, m_w_mix_out, m_ffn2_norm, m_ffn2_w_in, m_ffn2_w_out, m_ple_norm, m_w_ple_gate, m_w_ple_proj, m_final_norm, v_ffn1_norm, v_ffn1_w_in, v_ffn1_w_out, v_mix_norm, v_w_mix_in, v_conv_w, v_w_conv_out, v_w_attn_out, v_w_mix_out, v_ffn2_norm, v_ffn2_w_in, v_ffn2_w_out, v_ple_norm, v_w_ple_gate, v_w_ple_proj, v_final_norm):
    given = dict(locals())
    t = x.shape[1]
    xs = x.reshape(t, D_MODEL)
    ps = p.reshape(t, PLE_DIM)
    target = loss_target.reshape(t, D_MODEL)
    shard = {k: given[k].reshape(given[k].shape[-2:]) for k in _MATRICES}
    gain = {k: given[k].reshape(1, D_MODEL) for k in _GAINS}

    send = {k: shard[k].astype(BF) for k in _MATRICES}
    send["conv_w"] = jnp.pad(shard["conv_w"], ((0, CONV_ROWS - 3), (0, 0)))
    loss_vec, dx, landed, gain_grads = _forward_backward(xs, ps, target, gain, send)
    gain_rows = jnp.concatenate([gain_grads[k] for k in _GAINS] + [jnp.zeros((8 - len(_GAINS), D_MODEL), F32)], axis=0)
    gain_parts, = _exchange_alone("gather", [gain_rows], "gather_gain_gradients")

    out = {}
    for k in _MATRICES:
        w, m, v = shard[k], given["m_" + k].reshape(shard[k].shape), given["v_" + k].reshape(shard[k].shape)
        part = landed[k]
        if k == "conv_w":
            pad = ((0, CONV_ROWS - 3), (0, 0))
            w, m, v = jnp.pad(w, pad), jnp.pad(m, pad), jnp.pad(v, pad, constant_values=1.0)
        res = _adamw(part, w, m, v, "adamw_" + k)
        out[k] = [r[:3] if k == "conv_w" else r for r in res]
    stack = lambda pre: jnp.concatenate([given[pre + k].reshape(1, D_MODEL) for k in _GAINS] + [jnp.ones((8 - len(_GAINS), D_MODEL), F32)], axis=0)
    res = _adamw(gain_parts, stack(""), stack("m_"), stack("v_"), "adamw_gains")
    for j, k in enumerate(_GAINS):
        out[k] = [r[j:j + 1] for r in res]

    loss = lax.psum(loss_vec[0, 0], ("x", "y", "c"))
    per_kind = [[out[k][j].reshape(given[k].shape) for k in _WEIGHTS] for j in range(4)]
    return (loss, dx.reshape(x.shape), *per_kind[0], *per_kind[1], *per_kind[2], *per_kind[3])


def _forward_backward(xs, ps, target, gain, send, full=None):
    exchange = full is None
    full = dict(full or {})
    grads, landed = {}, {}

    def gather(names):
        return ("gather", [send[k] for k in names]) if exchange else None

    def scatter(names):
        return ("scatter", [grads[k] for k in names]) if exchange else None

    def keep(into, names, got):
        into.update(zip(names, got))

    first = ("ffn1_w_in", "ffn1_w_out")
    (n1,), got = _prenorm(xs, gain["ffn1_norm"], ride=gather(first))
    keep(full, first, got)
    w1_in, w1_out = full["ffn1_w_in"], full["ffn1_w_out"].reshape(N_FF_CHUNKS, FF_CHUNK, D_MODEL)
    mixer = ("w_mix_in", "conv_w", "w_conv_out", "w_attn_out", "w_mix_out")
    (h1, u, gate1, up1), got = _ffn_fwd(xs, n1, w1_in, w1_out, gain["mix_norm"], "ffn1_fwd", ride=gather(mixer))
    keep(full, mixer, got)
    w_mix = full["w_mix_in"]
    w_co, w_ao, w_mo = (full[k].reshape(D_MODEL, D_MODEL) for k in ("w_conv_out", "w_attn_out", "w_mix_out"))
    taps = _columns_from_shards(full["conv_w"][:, :3, :])
    rest = ("ffn2_w_in", "ffn2_w_out", "w_ple_gate", "w_ple_proj")
    (proj,), got = _mix_proj(u, w_mix, ride=gather(rest))
    keep(full, rest, got)
    w2_in, w2_out = full["ffn2_w_in"], full["ffn2_w_out"].reshape(N_FF_CHUNKS, FF_CHUNK, D_MODEL)
    w_pg = full["w_ple_gate"].reshape(D_MODEL, D_MODEL)
    w_pp = _columns_from_shards(full["w_ple_proj"])
    o = _attn_fwd(proj)
    h2, n3, ycin, yc, ya, merged = _mixer_out(proj, o, h1, taps, w_co, w_ao, w_mo, gain["ffn2_norm"])
    (h3, n4, gate2, up2), _ = _ffn_fwd(h2, n3, w2_in, w2_out, gain["ple_norm"], "ffn2_fwd")
    dh3, ds, dpp, loss_vec, dg_final, dg_ple = _tail(h3, n4, ps, w_pg, w_pp, gain["ple_norm"], gain["final_norm"], target)

    one = lambda a: a[None]
    by_rows = lambda g, rows: g.reshape(N_SHARDS, rows // N_SHARDS, D_MODEL)
    grads["w_ple_gate"] = by_rows(_wgrad(one(n4), one(ds), "wgrad_ple_gate"), D_MODEL)
    grads["w_ple_proj"] = _shards_from_columns(_wgrad(one(ps), one(dpp), "wgrad_ple_proj")[0])
    ple = ("w_ple_gate", "w_ple_proj")
    (df2, act2, dgate2, dup2), got = _ffn_bwd_hidden(dh3, gate2, up2, w2_out, "ffn2_bwd_hidden", ride=scatter(ple))
    keep(landed, ple, got)
    grads["ffn2_w_out"] = by_rows(_wgrad(act2, one(df2), "wgrad_ffn2_out"), D_FF)
    grads["ffn2_w_in"] = _wgrad_pieces(n3, [dgate2, dup2], "wgrad_ffn2_in")
    (dh2, dg_ffn2), got = _ffn_bwd_input(dh3, h2, gain["ffn2_norm"], dgate2, dup2, w2_in, "ffn2_bwd_input", ride=scatter(("ffn2_w_out",)))
    keep(landed, ("ffn2_w_out",), got)
    (dh2b, dyc, dya, dgc, dga, dcb, dcv, d_o), _ = _mixer_bwd(dh2, proj, yc, ya, taps, w_co, w_ao, w_mo)
    grads["w_mix_out"] = by_rows(_wgrad(one(merged), one(dh2b), "wgrad_mix_out"), D_MODEL)
    grads["w_conv_out"] = by_rows(_wgrad(one(ycin), one(dyc), "wgrad_conv_out"), D_MODEL)
    grads["w_attn_out"] = by_rows(_wgrad(one(o), one(dya), "wgrad_attn_out"), D_MODEL)
    dcc, dcx, dtaps = _conv_bwd(dcv, proj, taps)
    grads["conv_w"] = jnp.pad(_shards_from_columns(dtaps[:3]), ((0, 0), (0, CONV_ROWS - 3), (0, 0)))
    behind_attn = ("ffn2_w_in", "w_mix_out", "w_conv_out", "w_attn_out", "conv_w")
    (dq, dk, dv), got = _attn_bwd(proj, o, d_o, ride=scatter(behind_attn))
    keep(landed, behind_attn, got)
    dpieces = [dcb, dcc, dcx, dq, dk, dv, dgc, dga]
    half = N_MIX // 2
    grads["w_mix_in"] = jnp.concatenate([_wgrad_pieces(u, [one(dp) for dp in dpieces[:half]], "wgrad_mix_in_a", tile=WGRAD_TILE // 2),
                                         _wgrad_pieces(u, [one(dp) for dp in dpieces[half:]], "wgrad_mix_in_b", tile=WGRAD_TILE // 2)], axis=0)
    (dh1, dg_mix), got = _mix_bwd(dpieces, w_mix, h1, dh2, gain["mix_norm"], ride=scatter(("w_mix_in",)))
    keep(landed, ("w_mix_in",), got)
    (df1, act1, dgate1, dup1), _ = _ffn_bwd_hidden(dh1, gate1, up1, w1_out, "ffn1_bwd_hidden")
    grads["ffn1_w_out"] = by_rows(_wgrad(act1, one(df1), "wgrad_ffn1_out"), D_FF)
    if exchange:
        grads["ffn1_w_in"], got = _wgrad_pieces(n1, [dgate1, dup1], "wgrad_ffn1_in", ride=scatter(("ffn1_w_out",)))
        keep(landed, ("ffn1_w_out",), got)
    else:
        grads["ffn1_w_in"] = _wgrad_pieces(n1, [dgate1, dup1], "wgrad_ffn1_in")
    (dx, dg_ffn1), got = _ffn_bwd_input(dh1, xs, gain["ffn1_norm"], dgate1, dup1, w1_in, "ffn1_bwd_input", ride=scatter(("ffn1_w_in",)))
    keep(landed, ("ffn1_w_in",), got)
    gain_grads = dict(ffn1_norm=dg_ffn1, mix_norm=dg_mix, ffn2_norm=dg_ffn2, ple_norm=dg_ple, final_norm=dg_final)
    return loss_vec, dx, (landed if exchange else grads), gain_grads
```

```python
import functools
import math

import jax
import jax.numpy as jnp
from jax import lax
from jax.experimental import pallas as pl
from jax.experimental.pallas import tpu as pltpu

D_MODEL = 1024
D_FF = 2816
N_SHARDS = 8
FF_CHUNK = 2 * D_FF // N_SHARDS
N_FF_CHUNKS = D_FF // FF_CHUNK
N_HEADS = 8
HEAD_DIM = 128
PLE_DIM = 256
NORM_EPS = 1e-6
N_MIX = 8
ADAM_LR, ADAM_B1, ADAM_B2, ADAM_EPS, ADAM_WD, ADAM_STEP = 0.001, 0.9, 0.999, 1e-08, 0.01, 10

TOKEN_TILE = 512
WGRAD_TILE = 4096
PROJ_TILE = 2048
ATTN_ROWS = 512
ATTN_Q = 128
ATTN_SUB = 128
ATTN_K = 3 * ATTN_SUB
ATTN_SKIP_BELOW = -90.0

BF = jnp.bfloat16
F32 = jnp.float32
MESH = pl.DeviceIdType.MESH
NT = (((1,), (1,)), ((), ()))
TN = (((0,), (0,)), ((), ()))
S = jax.ShapeDtypeStruct
ANY = pl.BlockSpec(memory_space=pl.ANY)


def _const_spec(shape):
    nd = len(shape)
    return pl.BlockSpec(shape, lambda *_: (0,) * nd, pipeline_mode=pl.Buffered(1))


def _rows(tm, cols):
    return pl.BlockSpec((tm, cols), lambda i: (i, 0))


def _chunks(tm):
    return pl.BlockSpec((N_FF_CHUNKS, tm, FF_CHUNK), lambda i: (0, i, 0))


def _acc_spec(shape):
    nd = len(shape)
    return pl.BlockSpec(shape, lambda *_: (0,) * nd)


def _dot(a, b):
    return jnp.dot(a, b, preferred_element_type=F32)


def _dot_nt(a, b):
    return lax.dot_general(a, b, NT, preferred_element_type=F32)


def _dot_tn(a, b):
    return lax.dot_general(a, b, TN, preferred_element_type=F32)


def _rms(h, g):
    r = lax.rsqrt(jnp.mean(h * h, axis=-1, keepdims=True) + NORM_EPS)
    return h * r * g


def _rms_bwd(dn, h, g):
    r = lax.rsqrt(jnp.mean(h * h, axis=-1, keepdims=True) + NORM_EPS)
    nh = h * r
    gd = dn * g
    dh = r * (gd - nh * jnp.mean(gd * nh, axis=-1, keepdims=True))
    return dh, jnp.sum(dn * nh, axis=0, keepdims=True)


def _accumulate(ref, val):
    @pl.when(pl.program_id(0) == 0)
    def _():
        ref[...] = jnp.zeros_like(ref)
    ref[...] += val


def _place():
    x, y, c = lax.axis_index("x"), lax.axis_index("y"), lax.axis_index("c")
    return x, y, c


def _slot(px, py, pc):
    return 4 * px + 2 * py + pc


def _gather_phases(ins, outs, send_sems, recv_sems, local_sems):
    n = len(ins)

    def parties():
        x, y, c = _place()
        return (x, y, c), (x, y, 1 - c), [(1 - x, y), (x, 1 - y), (1 - x, 1 - y)], c

    def copy(a, k, block, to, src=None):
        dst = outs[a].at[_slot(*block)]
        return pltpu.make_async_remote_copy(
            src_ref=dst if src is None else src, dst_ref=dst,
            send_sem=send_sems.at[a, k], recv_sem=recv_sems.at[a, k],
            device_id=to, device_id_type=MESH)

    def own(a, me):
        return pltpu.make_async_copy(ins[a], outs[a].at[_slot(*me)], local_sems.at[a])

    def first(a, me, sibling, chips, c):
        return [copy(a, 0, me, sibling, src=ins[a])] + [copy(a, 1 + j, me, (*chip, c), src=ins[a]) for j, chip in enumerate(chips)]

    def start():
        me, sibling, chips, c = parties()
        for a in range(n):
            own(a, me).start()
        for a in range(n):
            for cp in first(a, me, sibling, chips, c):
                cp.start()

    def forward():
        me, sibling, chips, c = parties()
        for j, chip in enumerate(chips):
            for a in range(n):
                copy(a, 1 + j, (*chip, c), me).wait_recv()
                copy(a, 4 + j, (*chip, c), sibling).start()

    def finish():
        me, sibling, chips, c = parties()
        for a in range(n):
            copy(a, 0, sibling, me).wait_recv()
            for j, chip in enumerate(chips):
                copy(a, 4 + j, (*chip, 1 - c), me).wait_recv()
        for a in range(n):
            for cp in first(a, me, sibling, chips, c) + [copy(a, 4 + j, (*chip, c), sibling) for j, chip in enumerate(chips)]:
                cp.wait_send()
            own(a, me).wait()

    return [start, forward, finish]


def _scatter_phases(ins, outs, send_sems, recv_sems, local_sems):
    n = len(ins)

    def copies():
        x, y, c = _place()
        me = _slot(x, y, c)
        out = [pltpu.make_async_copy(ins[a].at[me], outs[a].at[me], local_sems.at[a]) for a in range(n)]
        for k in range(1, N_SHARDS):
            px = 1 - x if k & 4 else x
            py = 1 - y if k & 2 else y
            pc = 1 - c if k & 1 else c
            for a in range(n):
                out.append(pltpu.make_async_remote_copy(
                    src_ref=ins[a].at[_slot(px, py, pc)], dst_ref=outs[a].at[me],
                    send_sem=send_sems.at[a, k - 1], recv_sem=recv_sems.at[a, k - 1],
                    device_id=(px, py, pc), device_id_type=MESH))
        return out

    def start():
        for cp in copies():
            cp.start()

    def finish():
        for cp in copies():
            cp.wait()

    return [start, finish]


def _pallas(body, *, name, grid, in_specs, out_specs, out_shape, args, scratch_shapes=(), ride=None):
    if ride is None:
        outs = pl.pallas_call(body, name=name, grid=grid, in_specs=in_specs, out_specs=out_specs, out_shape=out_shape,
                              scratch_shapes=list(scratch_shapes))(*args)
        return list(outs), []
    kind, arrays = ride
    n, n_in, n_out, n_scr = len(arrays), len(in_specs), len(out_specs), len(scratch_shapes)
    total = math.prod(grid)
    middle = (2 * total) // 3
    landed_shape = [S((N_SHARDS,) + a.shape if kind == "gather" else a.shape, a.dtype) for a in arrays]

    def with_exchange(*refs):
        ins, riders_in = refs[:n_in], refs[n_in:n_in + n]
        outs, riders_out = refs[n_in + n:n_in + n + n_out], refs[n_in + n + n_out:n_in + 2 * n + n_out]
        scratch, sems = refs[n_in + 2 * n + n_out:n_in + 2 * n + n_out + n_scr], refs[n_in + 2 * n + n_out + n_scr:]
        step = 0
        for axis, size in enumerate(grid):
            step = step * size + pl.program_id(axis)
        phases = (_gather_phases if kind == "gather" else _scatter_phases)(riders_in, riders_out, *sems)
        pl.when(step == 0)(phases[0])
        body(*ins, *outs, *scratch)
        for phase in phases[1:-1]:
            pl.when(step == middle)(phase)
        pl.when(step == total - 1)(phases[-1])

    outs = pl.pallas_call(
        with_exchange, name=name, grid=grid,
        in_specs=list(in_specs) + [ANY] * n, out_specs=list(out_specs) + [ANY] * n,
        out_shape=list(out_shape) + landed_shape,
        scratch_shapes=list(scratch_shapes) + [pltpu.SemaphoreType.DMA((n, 7)), pltpu.SemaphoreType.DMA((n, 7)),
                                               pltpu.SemaphoreType.DMA((n,))],
    )(*args, *arrays)
    return list(outs[:n_out]), list(outs[n_out:])


def _exchange_alone(kind, arrays, name):
    return _pallas(lambda: None, name=name, grid=(1,), in_specs=[], out_specs=[], out_shape=[], args=[], ride=(kind, arrays))[1]


def _prenorm(x, g, ride=None):
    t = x.shape[0]
    tm = min(TOKEN_TILE, t)

    def body(x_ref, g_ref, n_ref):
        n_ref[...] = _rms(x_ref[...], g_ref[...]).astype(BF)

    return _pallas(
        body, name="prenorm", grid=(t // tm,), ride=ride,
        in_specs=[_rows(tm, D_MODEL), _const_spec((1, D_MODEL))], out_specs=[_rows(tm, D_MODEL)],
        out_shape=[S((t, D_MODEL), BF)], args=[x, g])


def _ffn_fwd(h, n, w_in, w_out, g_next, name, ride=None):
    t = h.shape[0]
    tm = min(TOKEN_TILE, t)

    def body(h_ref, n_ref, win_ref, wout_ref, g_ref, ho_ref, no_ref, gate_ref, up_ref):
        nb = n_ref[...]
        acc = jnp.zeros((tm, D_MODEL), F32)
        for c in range(N_FF_CHUNKS):
            gate = _dot(nb, win_ref[c])
            up = _dot(nb, win_ref[N_FF_CHUNKS + c])
            gate_ref[c] = gate.astype(BF)
            up_ref[c] = up.astype(BF)
            act = (gate * jax.nn.sigmoid(gate) * up).astype(BF)
            acc = acc + _dot(act, wout_ref[c])
        ho = h_ref[...] + 0.5 * acc
        ho_ref[...] = ho
        no_ref[...] = _rms(ho, g_ref[...]).astype(BF)

    return _pallas(
        body, name=name, grid=(t // tm,), ride=ride,
        in_specs=[_rows(tm, D_MODEL), _rows(tm, D_MODEL), _const_spec(w_in.shape), _const_spec(w_out.shape),
                  _const_spec((1, D_MODEL))],
        out_specs=[_rows(tm, D_MODEL), _rows(tm, D_MODEL), _chunks(tm), _chunks(tm)],
        out_shape=[S((t, D_MODEL), F32), S((t, D_MODEL), BF),
                   S((N_FF_CHUNKS, t, FF_CHUNK), BF), S((N_FF_CHUNKS, t, FF_CHUNK), BF)],
        args=[h, n, w_in, w_out, g_next])


def _mix_proj(u, w_mix, ride=None):
    t = u.shape[0]
    tm = min(PROJ_TILE, t)

    def body(u_ref, w_ref, o_ref):
        o_ref[0] = _dot(u_ref[...], w_ref[0]).astype(BF)

    return _pallas(
        body, name="mix_proj", grid=(N_MIX, t // tm), ride=ride,
        in_specs=[pl.BlockSpec((tm, D_MODEL), lambda d, i: (i, 0)), pl.BlockSpec((1, D_MODEL, D_MODEL), lambda d, i: (d, 0, 0))],
        out_specs=[pl.BlockSpec((1, tm, D_MODEL), lambda d, i: (d, i, 0))],
        out_shape=[S((N_MIX, t, D_MODEL), BF)], args=[u, w_mix])


HALO = 16


def _piece(d, tm):
    return pl.BlockSpec((1, tm, D_MODEL), lambda i: (d, i, 0))


def _prev_halo(d, tm):
    return pl.BlockSpec((1, HALO, D_MODEL), lambda i: (d, jnp.maximum(i * (tm // HALO) - 1, 0), 0))


def _shift_down(m, prev_tail, k):
    tm = m.shape[0]
    out = pltpu.roll(m, k, 0)
    row = lax.broadcasted_iota(jnp.int32, (tm, 1), 0)
    for j in range(k):
        out = jnp.where(row == j, prev_tail[HALO - k + j:HALO - k + j + 1, :], out)
    return out


def _conv_inputs(cc_ref, cx_ref, cch_ref, cxh_ref):
    m = cc_ref[0].astype(F32) * cx_ref[0].astype(F32)
    mh = cch_ref[0].astype(F32) * cxh_ref[0].astype(F32)
    mh = jnp.where(pl.program_id(0) == 0, 0.0, mh)
    return m, _shift_down(m, mh, 1), _shift_down(m, mh, 2)


def _mixer_out(proj, o, h1, conv_w, w_co, w_ao, w_mo, g_next):
    t = h1.shape[0]
    tm = min(TOKEN_TILE, t)

    def body(cb_ref, cc_ref, cx_ref, gc_ref, ga_ref, cch_ref, cxh_ref, o_ref, h_ref, cw_ref, wco_ref, wao_ref, wmo_ref,
             g_ref, ho_ref, no_ref, ycin_ref, yc_ref, ya_ref, mg_ref):
        m, m1, m2 = _conv_inputs(cc_ref, cx_ref, cch_ref, cxh_ref)
        cw = cw_ref[...]
        cv = cw[0:1, :] * m2 + cw[1:2, :] * m1 + cw[2:3, :] * m
        ycin = (cb_ref[0].astype(F32) * cv).astype(BF)
        ycin_ref[...] = ycin
        yc = _dot(ycin, wco_ref[...])
        ya = _dot(o_ref[...].astype(BF), wao_ref[...])
        yc_ref[...] = yc.astype(BF)
        ya_ref[...] = ya.astype(BF)
        merged = (jax.nn.sigmoid(gc_ref[0].astype(F32)) * yc + jax.nn.sigmoid(ga_ref[0].astype(F32)) * ya).astype(BF)
        mg_ref[...] = merged
        ho = h_ref[...] + _dot(merged, wmo_ref[...])
        ho_ref[...] = ho
        no_ref[...] = _rms(ho, g_ref[...]).astype(BF)

    sq = (D_MODEL, D_MODEL)
    return pl.pallas_call(
        body, name="mixer_out", grid=(t // tm,),
        in_specs=[_piece(0, tm), _piece(1, tm), _piece(2, tm), _piece(6, tm), _piece(7, tm), _prev_halo(1, tm), _prev_halo(2, tm),
                  _rows(tm, D_MODEL), _rows(tm, D_MODEL), _const_spec((3, D_MODEL)), _const_spec(sq), _const_spec(sq),
                  _const_spec(sq), _const_spec((1, D_MODEL))],
        out_specs=[_rows(tm, D_MODEL)] * 6,
        out_shape=[S((t, D_MODEL), F32)] + [S((t, D_MODEL), BF)] * 5,
    )(proj, proj, proj, proj, proj, proj, proj, o, h1, conv_w, w_co, w_ao, w_mo, g_next)


def _suffix_sums(vals, tri, before):
    out, right = [], before
    for b in reversed(range(ATTN_K // ATTN_SUB)):
        v = vals[:, b * ATTN_SUB:(b + 1) * ATTN_SUB]
        hi = v.astype(BF)
        lo = (v - hi.astype(F32)).astype(BF)
        out.append(_dot(hi, tri) + _dot(lo, tri) + right)
        right = right + jnp.sum(v, axis=1, keepdims=True)
    return jnp.concatenate(out[::-1], axis=1), right


ATTN_UNITS = ATTN_ROWS // ATTN_Q


def _unit_rows(x, u):
    return x[u * ATTN_Q:(u + 1) * ATTN_Q]


def _per_unit(fn):
    return jnp.concatenate([fn(u) for u in range(ATTN_UNITS)], axis=0)


def _per_row(vals):
    local = lax.broadcasted_iota(jnp.int32, (ATTN_ROWS, 1), 0)
    out = jnp.full((ATTN_ROWS, 1), vals[0], jnp.int32)
    for u in range(1, ATTN_UNITS):
        out = jnp.where(local >= u * ATTN_Q, vals[u], out)
    return out


def _attn_step(q, k_ref, starts, bounds, row):
    z = _per_unit(lambda u: _dot_nt(_unit_rows(q, u), k_ref[0, pl.ds(starts[u], ATTN_K), :])) * (1.0 / math.sqrt(HEAD_DIM))
    col = _per_row(starts) + lax.broadcasted_iota(jnp.int32, (1, ATTN_K), 1)
    mask = jnp.logical_and(col < row, col < _per_row(bounds))
    soft = jnp.log(1.0 + jnp.exp(-jnp.abs(z)))
    log_beta = jnp.minimum(z, 0.0) - soft
    log_rest = jnp.where(mask, jnp.minimum(-z, 0.0) - soft, 0.0)
    return z, mask, log_beta, log_rest


def _attn_sweep_start(i, t):
    blks = tuple(jnp.maximum(i * ATTN_UNITS + u + 1 - ATTN_K // ATTN_SUB, 0) for u in range(ATTN_UNITS))
    return blks, tuple(jnp.int32(t) for _ in range(ATTN_UNITS))


def _attn_keys(blks):
    return [pl.multiple_of(b * ATTN_SUB, ATTN_SUB) for b in blks]


def _attn_next(blks):
    return tuple(jnp.maximum(b - ATTN_K // ATTN_SUB, 0) for b in blks), tuple(b * ATTN_SUB for b in blks)


def _attn_more(carry):
    return jnp.logical_and(carry[1][ATTN_UNITS - 1] > 0, carry[-1] > ATTN_SKIP_BELOW)


def _tri(strict):
    r = lax.broadcasted_iota(jnp.int32, (ATTN_SUB, ATTN_SUB), 0)
    c = lax.broadcasted_iota(jnp.int32, (ATTN_SUB, ATTN_SUB), 1)
    return (r > c if strict else r >= c).astype(BF)


REACH_TILE = (8, 128)


def _first_step_spec():
    return pl.BlockSpec((1, ATTN_ROWS, ATTN_K), lambda h, i: (h, i, 0))


def _reach_spec():
    return pl.BlockSpec((1, 1) + REACH_TILE, lambda h, i: (h, i, 0, 0))


def _head_cols(piece):
    return lambda t: pl.BlockSpec((1, t, HEAD_DIM), lambda h, i: (piece, 0, h))


def _attn_fwd(proj):
    t = proj.shape[1]
    nq = t // ATTN_ROWS
    tri = _tri(strict=True)

    def body(q_ref, k_ref, v_ref, tri_ref, o_ref, a_ref, beta_ref, reach_ref):
        i = pl.program_id(1)
        q = q_ref[0]
        row = i * ATTN_ROWS + lax.broadcasted_iota(jnp.int32, (ATTN_ROWS, 1), 0)

        def step(carry, keep=False):
            blks, bounds, acc, run, _ = carry
            starts = _attn_keys(blks)
            _, mask, log_beta, log_rest = _attn_step(q, k_ref, starts, bounds, row)
            tail, run = _suffix_sums(log_rest, tri_ref[...], run)
            a = jnp.where(mask, jnp.exp(log_beta + tail), 0.0).astype(BF)
            if keep:
                a_ref[0] = a
                beta_ref[0] = jnp.where(mask, jnp.exp(log_beta), 0.0).astype(BF)
            acc = acc + _per_unit(lambda u: _dot(_unit_rows(a, u), v_ref[0, pl.ds(starts[u], ATTN_K), :]))
            return (*_attn_next(blks), acc, run, jnp.max(run))

        first = (*_attn_sweep_start(i, t), jnp.zeros((ATTN_ROWS, HEAD_DIM), F32), jnp.zeros((ATTN_ROWS, 1), F32), jnp.float32(0.0))
        after_first = step(first, keep=True)
        reach_ref[...] = jnp.full(reach_ref.shape, after_first[-1], F32)
        o_ref[...] = lax.while_loop(_attn_more, step, after_first)[2]

    qspec = pl.BlockSpec((1, ATTN_ROWS, HEAD_DIM), lambda h, i: (3, i, h))
    return pl.pallas_call(
        body, name="attn_fwd", grid=(N_HEADS, nq),
        in_specs=[qspec, _head_cols(4)(t), _head_cols(5)(t), pl.BlockSpec((ATTN_SUB, ATTN_SUB), lambda h, i: (0, 0))],
        out_specs=[pl.BlockSpec((ATTN_ROWS, HEAD_DIM), lambda h, i: (i, h)), _first_step_spec(), _first_step_spec(), _reach_spec()],
        out_shape=[S((t, D_MODEL), F32), S((N_HEADS, t, ATTN_K), BF), S((N_HEADS, t, ATTN_K), BF), S((N_HEADS, nq) + REACH_TILE, F32)],
    )(proj, proj, proj, tri)


def _attn_bwd(proj, o, d_o, a_first, beta_first, reach, ride=None):
    t = proj.shape[1]
    nq = t // ATTN_ROWS
    tri_strict, tri_incl = _tri(strict=True), _tri(strict=False)
    scale = 1.0 / math.sqrt(HEAD_DIM)

    def body(q_ref, k_ref, v_ref, o_ref, do_ref, a_ref, beta_ref, reach_ref, tris_ref, trii_ref, dq_ref, dk_ref, dv_ref, dk_acc, dv_acc):
        i = pl.program_id(1)

        @pl.when(i == 0)
        def _():
            dk_acc[...] = jnp.zeros_like(dk_acc)
            dv_acc[...] = jnp.zeros_like(dv_acc)

        q = q_ref[0]
        do = do_ref[...]
        total = jnp.sum(do.astype(F32) * o_ref[...], axis=1, keepdims=True)
        zero = jnp.zeros((ATTN_ROWS, 1), F32)
        blks0, bounds0 = _attn_sweep_start(i, t)

        def finish(starts, a, dz, dq):
            dzb = (dz * scale).astype(BF)
            for u in range(ATTN_UNITS):
                dv_acc[pl.ds(starts[u], ATTN_K), :] += _dot_tn(_unit_rows(a, u), _unit_rows(do, u))
                dk_acc[pl.ds(starts[u], ATTN_K), :] += _dot_tn(_unit_rows(dzb, u), _unit_rows(q, u))
            return dq + _per_unit(lambda u: _dot(_unit_rows(dzb, u), k_ref[0, pl.ds(starts[u], ATTN_K), :]))

        def grad_a(starts, a):
            return _per_unit(lambda u: _dot_nt(_unit_rows(do, u), v_ref[0, pl.ds(starts[u], ATTN_K), :])) * a.astype(F32)

        one_step = jnp.max(reach_ref[...]) <= ATTN_SKIP_BELOW

        @pl.when(one_step)
        def _():
            starts = _attn_keys(blks0)
            a = a_ref[0]
            beta = beta_ref[0].astype(F32)
            de = grad_a(starts, a)
            right, _ = _suffix_sums(de, trii_ref[...], zero)
            dz = de * (1.0 - beta) - (total - right) * beta
            dq_ref[...] = finish(starts, a, dz, jnp.zeros((ATTN_ROWS, HEAD_DIM), F32)).astype(BF)

        @pl.when(jnp.logical_not(one_step))
        def _():
            row = i * ATTN_ROWS + lax.broadcasted_iota(jnp.int32, (ATTN_ROWS, 1), 0)

            def step(carry):
                blks, bounds, dq, seen, run, _ = carry
                starts = _attn_keys(blks)
                z, mask, log_beta, log_rest = _attn_step(q, k_ref, starts, bounds, row)
                tail, run = _suffix_sums(log_rest, tris_ref[...], run)
                a = jnp.where(mask, jnp.exp(log_beta + tail), 0.0).astype(BF)
                de = grad_a(starts, a)
                right, seen = _suffix_sums(de, trii_ref[...], seen)
                beta = jax.nn.sigmoid(z)
                dz = jnp.where(mask, de * (1.0 - beta) - (total - right) * beta, 0.0)
                return (*_attn_next(blks), finish(starts, a, dz, dq), seen, run, jnp.max(run))

            first = (blks0, bounds0, jnp.zeros((ATTN_ROWS, HEAD_DIM), F32), zero, zero, jnp.float32(0.0))
            dq_ref[...] = lax.while_loop(_attn_more, step, step(first))[2].astype(BF)

        @pl.when(i == nq - 1)
        def _():
            dk_ref[...] = dk_acc[...].astype(BF)
            dv_ref[...] = dv_acc[...].astype(BF)

    qspec = pl.BlockSpec((1, ATTN_ROWS, HEAD_DIM), lambda h, i: (3, i, h))
    rowblk = pl.BlockSpec((ATTN_ROWS, HEAD_DIM), lambda h, i: (i, h))
    head = pl.BlockSpec((t, HEAD_DIM), lambda h, i: (0, h))
    trispec = pl.BlockSpec((ATTN_SUB, ATTN_SUB), lambda h, i: (0, 0))
    return _pallas(
        body, name="attn_bwd", grid=(N_HEADS, nq), ride=ride,
        in_specs=[qspec, _head_cols(4)(t), _head_cols(5)(t), rowblk, rowblk, _first_step_spec(), _first_step_spec(), _reach_spec(),
                  trispec, trispec],
        out_specs=[rowblk, head, head],
        out_shape=[S((t, D_MODEL), BF)] * 3,
        scratch_shapes=[pltpu.VMEM((t, HEAD_DIM), F32), pltpu.VMEM((t, HEAD_DIM), F32)],
        args=[proj, proj, proj, o, d_o, a_first, beta_first, reach, tri_strict, tri_incl])


def _tail(h3, n4, p, w_pg, w_pp, g_ple, g_final, target):
    t = h3.shape[0]
    tm = min(TOKEN_TILE, t)
    steps = t // tm

    def body(h_ref, n_ref, p_ref, wpg_ref, wpp_ref, gp_ref, gf_ref, tgt_ref,
             dh_ref, ds_ref, dpp_ref, loss_ref, dgf_ref, dgp_ref):
        pg = jax.nn.sigmoid(_dot(n_ref[...], wpg_ref[...]))
        pp = _dot(p_ref[...].astype(BF), wpp_ref[...])
        h3v = h_ref[...]
        h4 = h3v + pg * pp
        gf = gf_ref[...]
        diff = _rms(h4, gf) - tgt_ref[...]
        _accumulate(loss_ref, jnp.sum(diff * diff, axis=0, keepdims=True))
        dh4, dgf = _rms_bwd(diff * (1.0 / D_MODEL), h4, gf)
        _accumulate(dgf_ref, dgf)
        dpp_ref[...] = (dh4 * pg).astype(BF)
        ds = (dh4 * pp * pg * (1.0 - pg)).astype(BF)
        ds_ref[...] = ds
        dh3, dgp = _rms_bwd(_dot_nt(ds, wpg_ref[...]), h3v, gp_ref[...])
        _accumulate(dgp_ref, dgp)
        dh_ref[...] = dh4 + dh3

        @pl.when(pl.program_id(0) == steps - 1)
        def _():
            loss_ref[...] = jnp.full(loss_ref.shape, 0.5 / D_MODEL * jnp.sum(loss_ref[...]), F32)

    vec = (1, D_MODEL)
    return pl.pallas_call(
        body, name="tail", grid=(steps,),
        in_specs=[_rows(tm, D_MODEL), _rows(tm, D_MODEL), _rows(tm, PLE_DIM), _const_spec((D_MODEL, D_MODEL)),
                  _const_spec((PLE_DIM, D_MODEL)), _const_spec(vec), _const_spec(vec), _rows(tm, D_MODEL)],
        out_specs=[_rows(tm, D_MODEL)] * 3 + [_acc_spec(vec)] * 3,
        out_shape=[S((t, D_MODEL), F32), S((t, D_MODEL), BF), S((t, D_MODEL), BF)] + [S(vec, F32)] * 3,
    )(h3, n4, p, w_pg, w_pp, g_ple, g_final, target)


def _wgrad(xs, ys, name, ride=None):
    bx, t, k = xs.shape
    by, _, n = ys.shape
    b = max(bx, by)
    tt = min(WGRAD_TILE * 2 // xs.dtype.itemsize, t)
    steps = t // tt

    def body(x_ref, y_ref, o_ref, acc_ref):
        s = pl.program_id(1)

        @pl.when(s == 0)
        def _():
            acc_ref[...] = jnp.zeros_like(acc_ref)
        acc_ref[...] += _dot_tn(x_ref[0].astype(BF), y_ref[0].astype(BF))

        @pl.when(s == steps - 1)
        def _():
            o_ref[0] = acc_ref[...].astype(BF)

    (out,), landed = _pallas(
        body, name=name, grid=(b, steps), ride=ride,
        in_specs=[pl.BlockSpec((1, tt, k), (lambda j, s: (j, s, 0)) if bx > 1 else (lambda j, s: (0, s, 0))),
                  pl.BlockSpec((1, tt, n), (lambda j, s: (j, s, 0)) if by > 1 else (lambda j, s: (0, s, 0)))],
        out_specs=[pl.BlockSpec((1, k, n), lambda j, s: (j, 0, 0))],
        out_shape=[S((b, k, n), BF)],
        scratch_shapes=[pltpu.VMEM((k, n), F32)],
        args=[xs, ys])
    return (out, landed) if ride is not None else out


def _wgrad_pieces(x, ys, name, ride=None, tile=None):
    t, k = x.shape
    n = ys[0].shape[2]
    counts = [y.shape[0] for y in ys]
    offsets = [sum(counts[:j]) for j in range(len(ys))]
    total = sum(counts)
    tt = min(tile or WGRAD_TILE, t)
    steps = t // tt

    def body(x_ref, *refs):
        y_refs, o_ref, acc_ref = refs[:len(ys)], refs[len(ys)], refs[len(ys) + 1]
        p, s = pl.program_id(0), pl.program_id(1)

        @pl.when(s == 0)
        def _():
            acc_ref[...] = jnp.zeros_like(acc_ref)
        for j, y_ref in enumerate(y_refs):
            @pl.when(jnp.logical_and(p >= offsets[j], p < offsets[j] + counts[j]))
            def _(y_ref=y_ref):
                acc_ref[...] += _dot_tn(x_ref[...], y_ref[0])

        @pl.when(s == steps - 1)
        def _():
            o_ref[0] = acc_ref[...].astype(BF)

    def turn(j):
        lo, hi = offsets[j], offsets[j] + counts[j]
        return lambda p, s: (jnp.clip(p - lo, 0, counts[j] - 1), jnp.where(p < lo, 0, jnp.where(p >= hi, steps - 1, s)), 0)

    (out,), landed = _pallas(
        body, name=name, grid=(total, steps), ride=ride,
        in_specs=[pl.BlockSpec((tt, k), lambda p, s: (s, 0))] + [pl.BlockSpec((1, tt, n), turn(j)) for j in range(len(ys))],
        out_specs=[pl.BlockSpec((1, k, n), lambda p, s: (p, 0, 0))],
        out_shape=[S((total, k, n), BF)],
        scratch_shapes=[pltpu.VMEM((k, n), F32)],
        args=[x, *ys])
    return (out, landed) if ride is not None else out


def _ffn_bwd_hidden(dh, gate, up, w_out, name, ride=None):
    t = dh.shape[0]
    tm = min(TOKEN_TILE, t)

    def body(dh_ref, gate_ref, up_ref, wout_ref, df_ref, act_ref, dgate_ref, dup_ref):
        df = (0.5 * dh_ref[...]).astype(BF)
        df_ref[...] = df
        for c in range(N_FF_CHUNKS):
            gt = gate_ref[c].astype(F32)
            u = up_ref[c].astype(F32)
            sg = jax.nn.sigmoid(gt)
            silu = gt * sg
            act_ref[c] = (silu * u).astype(BF)
            dact = _dot_nt(df, wout_ref[c])
            dgate_ref[c] = (dact * u * (sg * (1.0 + gt * (1.0 - sg)))).astype(BF)
            dup_ref[c] = (dact * silu).astype(BF)

    return _pallas(
        body, name=name, grid=(t // tm,), ride=ride,
        in_specs=[_rows(tm, D_MODEL), _chunks(tm), _chunks(tm), _const_spec(w_out.shape)],
        out_specs=[_rows(tm, D_MODEL), _chunks(tm), _chunks(tm), _chunks(tm)],
        out_shape=[S((t, D_MODEL), BF)] + [S((N_FF_CHUNKS, t, FF_CHUNK), BF)] * 3,
        args=[dh, gate, up, w_out])


def _ffn_bwd_input(dh, h_in, g, dgate, dup, w_in, name, ride=None):
    t = dh.shape[0]
    tm = min(TOKEN_TILE, t)

    def body(dh_ref, h_ref, g_ref, dgate_ref, dup_ref, win_ref, dhi_ref, dg_ref):
        dn = jnp.zeros((tm, D_MODEL), F32)
        for c in range(N_FF_CHUNKS):
            dn = dn + _dot_nt(dgate_ref[c], win_ref[c]) + _dot_nt(dup_ref[c], win_ref[N_FF_CHUNKS + c])
        dhi, dg = _rms_bwd(dn, h_ref[...], g_ref[...])
        _accumulate(dg_ref, dg)
        dhi_ref[...] = dh_ref[...] + dhi

    vec = (1, D_MODEL)
    return _pallas(
        body, name=name, grid=(t // tm,), ride=ride,
        in_specs=[_rows(tm, D_MODEL), _rows(tm, D_MODEL), _const_spec(vec), _chunks(tm), _chunks(tm), _const_spec(w_in.shape)],
        out_specs=[_rows(tm, D_MODEL), _acc_spec(vec)],
        out_shape=[S((t, D_MODEL), F32), S(vec, F32)],
        args=[dh, h_in, g, dgate, dup, w_in])


def _mixer_bwd(dh2, proj, yc, ya, conv_w, w_co, w_ao, w_mo, ride=None):
    t = dh2.shape[0]
    tm = min(TOKEN_TILE, t)

    def body(dh_ref, cb_ref, cc_ref, cx_ref, gc_ref, ga_ref, cch_ref, cxh_ref, yc_ref, ya_ref, cw_ref, wco_ref, wao_ref, wmo_ref,
             dhb_ref, dyc_ref, dya_ref, dgc_ref, dga_ref, dcb_ref, dcv_ref, do_ref):
        dhb = dh_ref[...].astype(BF)
        dhb_ref[...] = dhb
        dmerged = _dot_nt(dhb, wmo_ref[...])
        sc = jax.nn.sigmoid(gc_ref[0].astype(F32))
        sa = jax.nn.sigmoid(ga_ref[0].astype(F32))
        dyc = (dmerged * sc).astype(BF)
        dya = (dmerged * sa).astype(BF)
        dyc_ref[...] = dyc
        dya_ref[...] = dya
        dgc_ref[...] = (dmerged * yc_ref[...].astype(F32) * sc * (1.0 - sc)).astype(BF)
        dga_ref[...] = (dmerged * ya_ref[...].astype(F32) * sa * (1.0 - sa)).astype(BF)
        m, m1, m2 = _conv_inputs(cc_ref, cx_ref, cch_ref, cxh_ref)
        cw = cw_ref[...]
        cv = cw[0:1, :] * m2 + cw[1:2, :] * m1 + cw[2:3, :] * m
        dycin = _dot_nt(dyc, wco_ref[...])
        dcb_ref[...] = (dycin * cv).astype(BF)
        dcv_ref[...] = dycin * cb_ref[0].astype(F32)
        do_ref[...] = _dot_nt(dya, wao_ref[...]).astype(BF)

    sq = (D_MODEL, D_MODEL)
    return _pallas(
        body, name="mixer_bwd", grid=(t // tm,), ride=ride,
        in_specs=[_rows(tm, D_MODEL), _piece(0, tm), _piece(1, tm), _piece(2, tm), _piece(6, tm), _piece(7, tm),
                  _prev_halo(1, tm), _prev_halo(2, tm), _rows(tm, D_MODEL), _rows(tm, D_MODEL),
                  _const_spec((3, D_MODEL)), _const_spec(sq), _const_spec(sq), _const_spec(sq)],
        out_specs=[_rows(tm, D_MODEL)] * 8,
        out_shape=[S((t, D_MODEL), BF)] * 6 + [S((t, D_MODEL), F32), S((t, D_MODEL), BF)],
        args=[dh2, proj, proj, proj, proj, proj, proj, proj, yc, ya, conv_w, w_co, w_ao, w_mo])


F32_HALO = 8


def _conv_bwd(dcv, proj, conv_w):
    t = dcv.shape[0]
    tm = min(TOKEN_TILE, t)
    steps = t // tm

    def body(dcv_ref, nxt_ref, cc_ref, cx_ref, cch_ref, cxh_ref, cw_ref, dcc_ref, dcx_ref, dw_ref):
        i = pl.program_id(0)
        m, m1, m2 = _conv_inputs(cc_ref, cx_ref, cch_ref, cxh_ref)
        d0 = dcv_ref[...]
        nxt = jnp.where(i == steps - 1, 0.0, nxt_ref[...])
        row = lax.broadcasted_iota(jnp.int32, (tm, 1), 0)
        d1 = jnp.where(row == tm - 1, nxt[0:1, :], pltpu.roll(d0, tm - 1, 0))
        d2 = pltpu.roll(d0, tm - 2, 0)
        d2 = jnp.where(row == tm - 2, nxt[0:1, :], jnp.where(row == tm - 1, nxt[1:2, :], d2))
        cw = cw_ref[...]
        dm = cw[2:3, :] * d0 + cw[1:2, :] * d1 + cw[0:1, :] * d2
        dcc_ref[...] = (dm * cx_ref[0].astype(F32)).astype(BF)
        dcx_ref[...] = (dm * cc_ref[0].astype(F32)).astype(BF)
        tap_row = lax.broadcasted_iota(jnp.int32, (F32_HALO, 1), 0)
        dw = jnp.zeros((F32_HALO, D_MODEL), F32)
        for j, mk in enumerate((m2, m1, m)):
            dw = jnp.where(tap_row == j, jnp.sum(d0 * mk, axis=0, keepdims=True), dw)
        _accumulate(dw_ref, dw)

    nxt_spec = pl.BlockSpec((F32_HALO, D_MODEL), lambda i: (jnp.minimum((i + 1) * (tm // F32_HALO), t // F32_HALO - 1), 0))
    return pl.pallas_call(
        body, name="conv_bwd", grid=(steps,),
        in_specs=[_rows(tm, D_MODEL), nxt_spec, _piece(1, tm), _piece(2, tm), _prev_halo(1, tm), _prev_halo(2, tm),
                  _const_spec((3, D_MODEL))],
        out_specs=[_rows(tm, D_MODEL), _rows(tm, D_MODEL), _acc_spec((F32_HALO, D_MODEL))],
        out_shape=[S((t, D_MODEL), BF), S((t, D_MODEL), BF), S((F32_HALO, D_MODEL), F32)],
    )(dcv, dcv, proj, proj, proj, proj, conv_w)


def _mix_bwd(dpieces, w_mix, h1, dh2, g, ride=None):
    t = h1.shape[0]
    tm = min(TOKEN_TILE, t)

    def body(*refs):
        pieces, (w_ref, h_ref, dh_ref, g_ref, dhi_ref, dg_ref) = refs[:N_MIX], refs[N_MIX:]
        du = jnp.zeros((tm, D_MODEL), F32)
        for d in range(N_MIX):
            du = du + _dot_nt(pieces[d][...], w_ref[d])
        dhi, dg = _rms_bwd(du, h_ref[...], g_ref[...])
        _accumulate(dg_ref, dg)
        dhi_ref[...] = dh_ref[...] + dhi

    vec = (1, D_MODEL)
    return _pallas(
        body, name="mix_bwd", grid=(t // tm,), ride=ride,
        in_specs=[_rows(tm, D_MODEL)] * N_MIX + [_const_spec(w_mix.shape), _rows(tm, D_MODEL), _rows(tm, D_MODEL), _const_spec(vec)],
        out_specs=[_rows(tm, D_MODEL), _acc_spec(vec)],
        out_shape=[S((t, D_MODEL), F32), S(vec, F32)],
        args=[*dpieces, w_mix, h1, dh2, g])


def _adamw(partials, w, m, v, name):
    r, c = w.shape
    tr = min(r, 512)
    c1 = 1.0 - ADAM_B1 ** ADAM_STEP
    c2 = 1.0 - ADAM_B2 ** ADAM_STEP

    def body(p_ref, w_ref, m_ref, v_ref, g_ref, d_ref, mo_ref, vo_ref):
        g = p_ref[0].astype(F32)
        for s in range(1, N_SHARDS):
            g = g + p_ref[s].astype(F32)
        mn = ADAM_B1 * m_ref[...] + (1.0 - ADAM_B1) * g
        vn = ADAM_B2 * v_ref[...] + (1.0 - ADAM_B2) * (g * g)
        g_ref[...] = g
        mo_ref[...] = mn
        vo_ref[...] = vn
        d_ref[...] = -ADAM_LR * ((mn / c1) / (jnp.sqrt(vn / c2) + ADAM_EPS) + ADAM_WD * w_ref[...])

    blk = pl.BlockSpec((tr, c), lambda i: (i, 0))
    return pl.pallas_call(
        body, name=name, grid=(r // tr,),
        in_specs=[pl.BlockSpec((N_SHARDS, tr, c), lambda i: (0, i, 0)), blk, blk, blk],
        out_specs=[blk] * 4, out_shape=[S((r, c), F32)] * 4,
    )(partials, w, m, v)


_MATRICES = ("ffn1_w_in", "ffn1_w_out", "w_mix_in", "conv_w", "w_conv_out", "w_attn_out", "w_mix_out",
             "ffn2_w_in", "ffn2_w_out", "w_ple_gate", "w_ple_proj")
_GAINS = ("ffn1_norm", "mix_norm", "ffn2_norm", "ple_norm", "final_norm")
_WEIGHTS = ("ffn1_norm", "ffn1_w_in", "ffn1_w_out", "mix_norm", "w_mix_in", "conv_w", "w_conv_out", "w_attn_out", "w_mix_out",
            "ffn2_norm", "ffn2_w_in", "ffn2_w_out", "ple_norm", "w_ple_gate", "w_ple_proj", "final_norm")
CONV_ROWS = 8


def _columns_from_shards(g):
    return jnp.transpose(g, (1, 0, 2)).reshape(g.shape[1], N_SHARDS * g.shape[2])


def _shards_from_columns(a):
    r, c = a.shape
    return jnp.transpose(a.reshape(r, N_SHARDS, c // N_SHARDS), (1, 0, 2))


def kernel(x, p, ffn1_norm, ffn1_w_in, ffn1_w_out, mix_norm, w_mix_in, conv_w, w_conv_out, w_attn_out, w_mix_out, ffn2_norm, ffn2_w_in, ffn2_w_out, ple_norm, w_ple_gate, w_ple_proj, final_norm, loss_target, m_ffn1_norm, m_ffn1_w_in, m_ffn1_w_out, m_mix_norm, m_w_mix_in, m_conv_w, m_w_conv_out, m_w_attn_out, m_w_mix_out, m_ffn2_norm, m_ffn2_w_in, m_ffn2_w_out, m_ple_norm, m_w_ple_gate, m_w_ple_proj, m_final_norm, v_ffn1_norm, v_ffn1_w_in, v_ffn1_w_out, v_mix_norm, v_w_mix_in, v_conv_w, v_w_conv_out, v_w_attn_out, v_w_mix_out, v_ffn2_norm, v_ffn2_w_in, v_ffn2_w_out, v_ple_norm, v_w_ple_gate, v_w_ple_proj, v_final_norm):
    given = dict(locals())
    t = x.shape[1]
    xs = x.reshape(t, D_MODEL)
    ps = p.reshape(t, PLE_DIM)
    target = loss_target.reshape(t, D_MODEL)
    shard = {k: given[k].reshape(given[k].shape[-2:]) for k in _MATRICES}
    gain = {k: given[k].reshape(1, D_MODEL) for k in _GAINS}

    send = {k: shard[k].astype(BF) for k in _MATRICES}
    send["conv_w"] = jnp.pad(shard["conv_w"], ((0, CONV_ROWS - 3), (0, 0)))
    loss_vec, dx, landed, gain_grads = _forward_backward(xs, ps, target, gain, send)
    gain_rows = jnp.concatenate([gain_grads[k] for k in _GAINS] + [jnp.zeros((8 - len(_GAINS), D_MODEL), F32)], axis=0)
    gain_parts, = _exchange_alone("gather", [gain_rows], "gather_gain_gradients")

    out = {}
    for k in _MATRICES:
        w, m, v = shard[k], given["m_" + k].reshape(shard[k].shape), given["v_" + k].reshape(shard[k].shape)
        part = landed[k]
        if k == "conv_w":
            pad = ((0, CONV_ROWS - 3), (0, 0))
            w, m, v = jnp.pad(w, pad), jnp.pad(m, pad), jnp.pad(v, pad, constant_values=1.0)
        res = _adamw(part, w, m, v, "adamw_" + k)
        out[k] = [r[:3] if k == "conv_w" else r for r in res]
    stack = lambda pre: jnp.concatenate([given[pre + k].reshape(1, D_MODEL) for k in _GAINS] + [jnp.ones((8 - len(_GAINS), D_MODEL), F32)], axis=0)
    res = _adamw(gain_parts, stack(""), stack("m_"), stack("v_"), "adamw_gains")
    for j, k in enumerate(_GAINS):
        out[k] = [r[j:j + 1] for r in res]

    loss = lax.psum(loss_vec[0, 0], ("x", "y", "c"))
    per_kind = [[out[k][j].reshape(given[k].shape) for k in _WEIGHTS] for j in range(4)]
    return (loss, dx.reshape(x.shape), *per_kind[0], *per_kind[1], *per_kind[2], *per_kind[3])


def _forward_backward(xs, ps, target, gain, send, full=None):
    exchange = full is None
    full = dict(full or {})
    grads, landed = {}, {}

    def gather(names):
        return ("gather", [send[k] for k in names]) if exchange else None

    def scatter(names):
        return ("scatter", [grads[k] for k in names]) if exchange else None

    def keep(into, names, got):
        into.update(zip(names, got))

    first = ("ffn1_w_in", "ffn1_w_out")
    (n1,), got = _prenorm(xs, gain["ffn1_norm"], ride=gather(first))
    keep(full, first, got)
    w1_in, w1_out = full["ffn1_w_in"], full["ffn1_w_out"].reshape(N_FF_CHUNKS, FF_CHUNK, D_MODEL)
    mixer = ("w_mix_in", "conv_w", "w_conv_out", "w_attn_out", "w_mix_out")
    (h1, u, gate1, up1), got = _ffn_fwd(xs, n1, w1_in, w1_out, gain["mix_norm"], "ffn1_fwd", ride=gather(mixer))
    keep(full, mixer, got)
    w_mix = full["w_mix_in"]
    w_co, w_ao, w_mo = (full[k].reshape(D_MODEL, D_MODEL) for k in ("w_conv_out", "w_attn_out", "w_mix_out"))
    taps = _columns_from_shards(full["conv_w"][:, :3, :])
    rest = ("ffn2_w_in", "ffn2_w_out", "w_ple_gate", "w_ple_proj")
    (proj,), got = _mix_proj(u, w_mix, ride=gather(rest))
    keep(full, rest, got)
    w2_in, w2_out = full["ffn2_w_in"], full["ffn2_w_out"].reshape(N_FF_CHUNKS, FF_CHUNK, D_MODEL)
    w_pg = full["w_ple_gate"].reshape(D_MODEL, D_MODEL)
    w_pp = _columns_from_shards(full["w_ple_proj"])
    o, a_first, beta_first, reach = _attn_fwd(proj)
    h2, n3, ycin, yc, ya, merged = _mixer_out(proj, o, h1, taps, w_co, w_ao, w_mo, gain["ffn2_norm"])
    (h3, n4, gate2, up2), _ = _ffn_fwd(h2, n3, w2_in, w2_out, gain["ple_norm"], "ffn2_fwd")
    dh3, ds, dpp, loss_vec, dg_final, dg_ple = _tail(h3, n4, ps, w_pg, w_pp, gain["ple_norm"], gain["final_norm"], target)

    one = lambda a: a[None]
    by_rows = lambda g, rows: g.reshape(N_SHARDS, rows // N_SHARDS, D_MODEL)
    grads["w_ple_gate"] = by_rows(_wgrad(one(n4), one(ds), "wgrad_ple_gate"), D_MODEL)
    grads["w_ple_proj"] = _shards_from_columns(_wgrad(one(ps), one(dpp), "wgrad_ple_proj")[0])
    ple = ("w_ple_gate", "w_ple_proj")
    (df2, act2, dgate2, dup2), got = _ffn_bwd_hidden(dh3, gate2, up2, w2_out, "ffn2_bwd_hidden", ride=scatter(ple))
    keep(landed, ple, got)
    grads["ffn2_w_out"] = by_rows(_wgrad(act2, one(df2), "wgrad_ffn2_out"), D_FF)
    grads["ffn2_w_in"] = _wgrad_pieces(n3, [dgate2, dup2], "wgrad_ffn2_in")
    (dh2, dg_ffn2), got = _ffn_bwd_input(dh3, h2, gain["ffn2_norm"], dgate2, dup2, w2_in, "ffn2_bwd_input", ride=scatter(("ffn2_w_out",)))
    keep(landed, ("ffn2_w_out",), got)
    (dh2b, dyc, dya, dgc, dga, dcb, dcv, d_o), _ = _mixer_bwd(dh2, proj, yc, ya, taps, w_co, w_ao, w_mo)
    grads["w_mix_out"] = by_rows(_wgrad(one(merged), one(dh2b), "wgrad_mix_out"), D_MODEL)
    grads["w_conv_out"] = by_rows(_wgrad(one(ycin), one(dyc), "wgrad_conv_out"), D_MODEL)
    grads["w_attn_out"] = by_rows(_wgrad(one(o), one(dya), "wgrad_attn_out"), D_MODEL)
    dcc, dcx, dtaps = _conv_bwd(dcv, proj, taps)
    grads["conv_w"] = jnp.pad(_shards_from_columns(dtaps[:3]), ((0, 0), (0, CONV_ROWS - 3), (0, 0)))
    behind_attn = ("ffn2_w_in", "w_mix_out", "w_conv_out", "w_attn_out", "conv_w")
    (dq, dk, dv), got = _attn_bwd(proj, o, d_o, a_first, beta_first, reach, ride=scatter(behind_attn))
    keep(landed, behind_attn, got)
    dpieces = [dcb, dcc, dcx, dq, dk, dv, dgc, dga]
    half = N_MIX // 2
    grads["w_mix_in"] = jnp.concatenate([_wgrad_pieces(u, [one(dp) for dp in dpieces[:half]], "wgrad_mix_in_a", tile=WGRAD_TILE // 2),
                                         _wgrad_pieces(u, [one(dp) for dp in dpieces[half:]], "wgrad_mix_in_b", tile=WGRAD_TILE // 2)], axis=0)
    (dh1, dg_mix), got = _mix_bwd(dpieces, w_mix, h1, dh2, gain["mix_norm"], ride=scatter(("w_mix_in",)))
    keep(landed, ("w_mix_in",), got)
    (df1, act1, dgate1, dup1), _ = _ffn_bwd_hidden(dh1, gate1, up1, w1_out, "ffn1_bwd_hidden")
    grads["ffn1_w_out"] = by_rows(_wgrad(act1, one(df1), "wgrad_ffn1_out"), D_FF)
    if exchange:
        grads["ffn1_w_in"], got = _wgrad_pieces(n1, [dgate1, dup1], "wgrad_ffn1_in", ride=scatter(("ffn1_w_out",)))
        keep(landed, ("ffn1_w_out",), got)
    else:
        grads["ffn1_w_in"] = _wgrad_pieces(n1, [dgate1, dup1], "wgrad_ffn1_in")
    (dx, dg_ffn1), got = _ffn_bwd_input(dh1, xs, gain["ffn1_norm"], dgate1, dup1, w1_in, "ffn1_bwd_input", ride=scatter(("ffn1_w_in",)))
    keep(landed, ("ffn1_w_in",), got)
    gain_grads = dict(ffn1_norm=dg_ffn1, mix_norm=dg_mix, ffn2_norm=dg_ffn2, ple_norm=dg_ple, final_norm=dg_final)
    return loss_vec, dx, (landed if exchange else grads), gain_grads
```

```python
import functools
import math

import jax
import jax.numpy as jnp
from jax import lax
from jax.experimental import pallas as pl
from jax.experimental.pallas import tpu as pltpu

D_MODEL = 1024
D_FF = 2816
N_SHARDS = 8
FF_CHUNK = 2 * D_FF // N_SHARDS
N_FF_CHUNKS = D_FF // FF_CHUNK
N_HEADS = 8
HEAD_DIM = 128
PLE_DIM = 256
NORM_EPS = 1e-6
N_MIX = 8
ADAM_LR, ADAM_B1, ADAM_B2, ADAM_EPS, ADAM_WD, ADAM_STEP = 0.001, 0.9, 0.999, 1e-08, 0.01, 10

TOKEN_TILE = 512
WGRAD_TILE = 4096
PROJ_TILE = 2048
ATTN_ROWS = 512
ATTN_Q = 128
ATTN_SUB = 128
ATTN_K = 3 * ATTN_SUB
ATTN_SKIP_BELOW = -90.0

BF = jnp.bfloat16
F32 = jnp.float32
MESH = pl.DeviceIdType.MESH
NT = (((1,), (1,)), ((), ()))
TN = (((0,), (0,)), ((), ()))
S = jax.ShapeDtypeStruct
ANY = pl.BlockSpec(memory_space=pl.ANY)


def _const_spec(shape):
    nd = len(shape)
    return pl.BlockSpec(shape, lambda *_: (0,) * nd, pipeline_mode=pl.Buffered(1))


def _rows(tm, cols):
    return pl.BlockSpec((tm, cols), lambda i: (i, 0))


def _chunks(tm):
    return pl.BlockSpec((N_FF_CHUNKS, tm, FF_CHUNK), lambda i: (0, i, 0))


def _acc_spec(shape):
    nd = len(shape)
    return pl.BlockSpec(shape, lambda *_: (0,) * nd)


def _dot(a, b):
    return jnp.dot(a, b, preferred_element_type=F32)


def _dot_nt(a, b):
    return lax.dot_general(a, b, NT, preferred_element_type=F32)


def _dot_tn(a, b):
    return lax.dot_general(a, b, TN, preferred_element_type=F32)


def _rms(h, g):
    r = lax.rsqrt(jnp.mean(h * h, axis=-1, keepdims=True) + NORM_EPS)
    return h * r * g


def _rms_bwd(dn, h, g):
    r = lax.rsqrt(jnp.mean(h * h, axis=-1, keepdims=True) + NORM_EPS)
    nh = h * r
    gd = dn * g
    dh = r * (gd - nh * jnp.mean(gd * nh, axis=-1, keepdims=True))
    return dh, jnp.sum(dn * nh, axis=0, keepdims=True)


def _accumulate(ref, val):
    @pl.when(pl.program_id(0) == 0)
    def _():
        ref[...] = jnp.zeros_like(ref)
    ref[...] += val


def _place():
    x, y, c = lax.axis_index("x"), lax.axis_index("y"), lax.axis_index("c")
    return x, y, c


def _slot(px, py, pc):
    return 4 * px + 2 * py + pc


def _gather_phases(ins, outs, send_sems, recv_sems, local_sems):
    n = len(ins)

    def parties():
        x, y, c = _place()
        return (x, y, c), (x, y, 1 - c), [(1 - x, y), (x, 1 - y), (1 - x, 1 - y)], c

    def copy(a, k, block, to, src=None):
        dst = outs[a].at[_slot(*block)]
        return pltpu.make_async_remote_copy(
            src_ref=dst if src is None else src, dst_ref=dst,
            send_sem=send_sems.at[a, k], recv_sem=recv_sems.at[a, k],
            device_id=to, device_id_type=MESH)

    def own(a, me):
        return pltpu.make_async_copy(ins[a], outs[a].at[_slot(*me)], local_sems.at[a])

    def first(a, me, sibling, chips, c):
        return [copy(a, 0, me, sibling, src=ins[a])] + [copy(a, 1 + j, me, (*chip, c), src=ins[a]) for j, chip in enumerate(chips)]

    def start():
        me, sibling, chips, c = parties()
        for a in range(n):
            own(a, me).start()
        for a in range(n):
            for cp in first(a, me, sibling, chips, c):
                cp.start()

    def forward():
        me, sibling, chips, c = parties()
        for j, chip in enumerate(chips):
            for a in range(n):
                copy(a, 1 + j, (*chip, c), me).wait_recv()
                copy(a, 4 + j, (*chip, c), sibling).start()

    def finish():
        me, sibling, chips, c = parties()
        for a in range(n):
            copy(a, 0, sibling, me).wait_recv()
            for j, chip in enumerate(chips):
                copy(a, 4 + j, (*chip, 1 - c), me).wait_recv()
        for a in range(n):
            for cp in first(a, me, sibling, chips, c) + [copy(a, 4 + j, (*chip, c), sibling) for j, chip in enumerate(chips)]:
                cp.wait_send()
            own(a, me).wait()

    return [start, forward, finish]


def _scatter_phases(ins, outs, send_sems, recv_sems, local_sems):
    n = len(ins)

    def copies():
        x, y, c = _place()
        me = _slot(x, y, c)
        out = [pltpu.make_async_copy(ins[a].at[me], outs[a].at[me], local_sems.at[a]) for a in range(n)]
        for k in range(1, N_SHARDS):
            px = 1 - x if k & 4 else x
            py = 1 - y if k & 2 else y
            pc = 1 - c if k & 1 else c
            for a in range(n):
                out.append(pltpu.make_async_remote_copy(
                    src_ref=ins[a].at[_slot(px, py, pc)], dst_ref=outs[a].at[me],
                    send_sem=send_sems.at[a, k - 1], recv_sem=recv_sems.at[a, k - 1],
                    device_id=(px, py, pc), device_id_type=MESH))
        return out

    def start():
        for cp in copies():
            cp.start()

    def finish():
        for cp in copies():
            cp.wait()

    return [start, finish]


def _pallas(body, *, name, grid, in_specs, out_specs, out_shape, args, scratch_shapes=(), ride=None):
    if ride is None:
        outs = pl.pallas_call(body, name=name, grid=grid, in_specs=in_specs, out_specs=out_specs, out_shape=out_shape,
                              scratch_shapes=list(scratch_shapes))(*args)
        return list(outs), []
    kind, arrays = ride
    n, n_in, n_out, n_scr = len(arrays), len(in_specs), len(out_specs), len(scratch_shapes)
    total = math.prod(grid)
    middle = (2 * total) // 3
    landed_shape = [S((N_SHARDS,) + a.shape if kind == "gather" else a.shape, a.dtype) for a in arrays]

    def with_exchange(*refs):
        ins, riders_in = refs[:n_in], refs[n_in:n_in + n]
        outs, riders_out = refs[n_in + n:n_in + n + n_out], refs[n_in + n + n_out:n_in + 2 * n + n_out]
        scratch, sems = refs[n_in + 2 * n + n_out:n_in + 2 * n + n_out + n_scr], refs[n_in + 2 * n + n_out + n_scr:]
        step = 0
        for axis, size in enumerate(grid):
            step = step * size + pl.program_id(axis)
        phases = (_gather_phases if kind == "gather" else _scatter_phases)(riders_in, riders_out, *sems)
        pl.when(step == 0)(phases[0])
        body(*ins, *outs, *scratch)
        for phase in phases[1:-1]:
            pl.when(step == middle)(phase)
        pl.when(step == total - 1)(phases[-1])

    outs = pl.pallas_call(
        with_exchange, name=name, grid=grid,
        in_specs=list(in_specs) + [ANY] * n, out_specs=list(out_specs) + [ANY] * n,
        out_shape=list(out_shape) + landed_shape,
        scratch_shapes=list(scratch_shapes) + [pltpu.SemaphoreType.DMA((n, 7)), pltpu.SemaphoreType.DMA((n, 7)),
                                               pltpu.SemaphoreType.DMA((n,))],
    )(*args, *arrays)
    return list(outs[:n_out]), list(outs[n_out:])


def _exchange_alone(kind, arrays, name):
    return _pallas(lambda: None, name=name, grid=(1,), in_specs=[], out_specs=[], out_shape=[], args=[], ride=(kind, arrays))[1]


def _prenorm(x, g, ride=None):
    t = x.shape[0]
    tm = min(TOKEN_TILE, t)

    def body(x_ref, g_ref, n_ref):
        n_ref[...] = _rms(x_ref[...], g_ref[...]).astype(BF)

    return _pallas(
        body, name="prenorm", grid=(t // tm,), ride=ride,
        in_specs=[_rows(tm, D_MODEL), _const_spec((1, D_MODEL))], out_specs=[_rows(tm, D_MODEL)],
        out_shape=[S((t, D_MODEL), BF)], args=[x, g])


def _ffn_fwd(h, n, w_in, w_out, g_next, name, ride=None):
    t = h.shape[0]
    tm = min(TOKEN_TILE, t)

    def body(h_ref, n_ref, win_ref, wout_ref, g_ref, ho_ref, no_ref, act_ref, to_gate_ref, to_up_ref):
        nb = n_ref[...]
        acc = jnp.zeros((tm, D_MODEL), F32)
        for c in range(N_FF_CHUNKS):
            gate = _dot(nb, win_ref[c])
            up = _dot(nb, win_ref[N_FF_CHUNKS + c])
            sg = jax.nn.sigmoid(gate)
            silu = gate * sg
            act = (silu * up).astype(BF)
            act_ref[c] = act
            to_gate_ref[c] = (up * (sg * (1.0 + gate * (1.0 - sg)))).astype(BF)
            to_up_ref[c] = silu.astype(BF)
            acc = acc + _dot(act, wout_ref[c])
        ho = h_ref[...] + 0.5 * acc
        ho_ref[...] = ho
        no_ref[...] = _rms(ho, g_ref[...]).astype(BF)

    return _pallas(
        body, name=name, grid=(t // tm,), ride=ride,
        in_specs=[_rows(tm, D_MODEL), _rows(tm, D_MODEL), _const_spec(w_in.shape), _const_spec(w_out.shape),
                  _const_spec((1, D_MODEL))],
        out_specs=[_rows(tm, D_MODEL), _rows(tm, D_MODEL), _chunks(tm), _chunks(tm), _chunks(tm)],
        out_shape=[S((t, D_MODEL), F32), S((t, D_MODEL), BF)] + [S((N_FF_CHUNKS, t, FF_CHUNK), BF)] * 3,
        args=[h, n, w_in, w_out, g_next])


def _mix_proj(u, w_mix, ride=None):
    t = u.shape[0]
    tm = min(PROJ_TILE, t)

    def body(u_ref, w_ref, o_ref):
        o_ref[0] = _dot(u_ref[...], w_ref[0]).astype(BF)

    return _pallas(
        body, name="mix_proj", grid=(N_MIX, t // tm), ride=ride,
        in_specs=[pl.BlockSpec((tm, D_MODEL), lambda d, i: (i, 0)), pl.BlockSpec((1, D_MODEL, D_MODEL), lambda d, i: (d, 0, 0))],
        out_specs=[pl.BlockSpec((1, tm, D_MODEL), lambda d, i: (d, i, 0))],
        out_shape=[S((N_MIX, t, D_MODEL), BF)], args=[u, w_mix])


HALO = 16


def _piece(d, tm):
    return pl.BlockSpec((1, tm, D_MODEL), lambda i: (d, i, 0))


def _prev_halo(d, tm):
    return pl.BlockSpec((1, HALO, D_MODEL), lambda i: (d, jnp.maximum(i * (tm // HALO) - 1, 0), 0))


def _shift_down(m, prev_tail, k):
    tm = m.shape[0]
    out = pltpu.roll(m, k, 0)
    row = lax.broadcasted_iota(jnp.int32, (tm, 1), 0)
    for j in range(k):
        out = jnp.where(row == j, prev_tail[HALO - k + j:HALO - k + j + 1, :], out)
    return out


def _conv_inputs(cc_ref, cx_ref, cch_ref, cxh_ref):
    m = cc_ref[0].astype(F32) * cx_ref[0].astype(F32)
    mh = cch_ref[0].astype(F32) * cxh_ref[0].astype(F32)
    mh = jnp.where(pl.program_id(0) == 0, 0.0, mh)
    return m, _shift_down(m, mh, 1), _shift_down(m, mh, 2)


def _mixer_out(proj, o, h1, conv_w, w_co, w_ao, w_mo, g_next):
    t = h1.shape[0]
    tm = min(TOKEN_TILE, t)

    def body(cb_ref, cc_ref, cx_ref, gc_ref, ga_ref, cch_ref, cxh_ref, o_ref, h_ref, cw_ref, wco_ref, wao_ref, wmo_ref,
             g_ref, ho_ref, no_ref, ycin_ref, yc_ref, ya_ref, mg_ref):
        m, m1, m2 = _conv_inputs(cc_ref, cx_ref, cch_ref, cxh_ref)
        cw = cw_ref[...]
        cv = cw[0:1, :] * m2 + cw[1:2, :] * m1 + cw[2:3, :] * m
        ycin = (cb_ref[0].astype(F32) * cv).astype(BF)
        ycin_ref[...] = ycin
        yc = _dot(ycin, wco_ref[...])
        ya = _dot(o_ref[...].astype(BF), wao_ref[...])
        yc_ref[...] = yc.astype(BF)
        ya_ref[...] = ya.astype(BF)
        merged = (jax.nn.sigmoid(gc_ref[0].astype(F32)) * yc + jax.nn.sigmoid(ga_ref[0].astype(F32)) * ya).astype(BF)
        mg_ref[...] = merged
        ho = h_ref[...] + _dot(merged, wmo_ref[...])
        ho_ref[...] = ho
        no_ref[...] = _rms(ho, g_ref[...]).astype(BF)

    sq = (D_MODEL, D_MODEL)
    return pl.pallas_call(
        body, name="mixer_out", grid=(t // tm,),
        in_specs=[_piece(0, tm), _piece(1, tm), _piece(2, tm), _piece(6, tm), _piece(7, tm), _prev_halo(1, tm), _prev_halo(2, tm),
                  _rows(tm, D_MODEL), _rows(tm, D_MODEL), _const_spec((3, D_MODEL)), _const_spec(sq), _const_spec(sq),
                  _const_spec(sq), _const_spec((1, D_MODEL))],
        out_specs=[_rows(tm, D_MODEL)] * 6,
        out_shape=[S((t, D_MODEL), F32)] + [S((t, D_MODEL), BF)] * 5,
    )(proj, proj, proj, proj, proj, proj, proj, o, h1, conv_w, w_co, w_ao, w_mo, g_next)


def _suffix_sums(vals, tri, before):
    out, right = [], before
    for b in reversed(range(ATTN_K // ATTN_SUB)):
        v = vals[:, b * ATTN_SUB:(b + 1) * ATTN_SUB]
        hi = v.astype(BF)
        lo = (v - hi.astype(F32)).astype(BF)
        out.append(_dot(hi, tri) + _dot(lo, tri) + right)
        right = right + jnp.sum(v, axis=1, keepdims=True)
    return jnp.concatenate(out[::-1], axis=1), right


ATTN_UNITS = ATTN_ROWS // ATTN_Q


def _unit_rows(x, u):
    return x[u * ATTN_Q:(u + 1) * ATTN_Q]


def _per_unit(fn):
    return jnp.concatenate([fn(u) for u in range(ATTN_UNITS)], axis=0)


def _per_row(vals):
    local = lax.broadcasted_iota(jnp.int32, (ATTN_ROWS, 1), 0)
    out = jnp.full((ATTN_ROWS, 1), vals[0], jnp.int32)
    for u in range(1, ATTN_UNITS):
        out = jnp.where(local >= u * ATTN_Q, vals[u], out)
    return out


def _attn_step(q, k_ref, starts, bounds, row):
    z = _per_unit(lambda u: _dot_nt(_unit_rows(q, u), k_ref[0, pl.ds(starts[u], ATTN_K), :])) * (1.0 / math.sqrt(HEAD_DIM))
    col = _per_row(starts) + lax.broadcasted_iota(jnp.int32, (1, ATTN_K), 1)
    mask = jnp.logical_and(col < row, col < _per_row(bounds))
    soft = jnp.log(1.0 + jnp.exp(-jnp.abs(z)))
    log_beta = jnp.minimum(z, 0.0) - soft
    log_rest = jnp.where(mask, jnp.minimum(-z, 0.0) - soft, 0.0)
    return z, mask, log_beta, log_rest


def _attn_sweep_start(i, t):
    blks = tuple(jnp.maximum(i * ATTN_UNITS + u + 1 - ATTN_K // ATTN_SUB, 0) for u in range(ATTN_UNITS))
    return blks, tuple(jnp.int32(t) for _ in range(ATTN_UNITS))


def _attn_keys(blks):
    return [pl.multiple_of(b * ATTN_SUB, ATTN_SUB) for b in blks]


def _attn_next(blks):
    return tuple(jnp.maximum(b - ATTN_K // ATTN_SUB, 0) for b in blks), tuple(b * ATTN_SUB for b in blks)


def _attn_more(carry):
    return jnp.logical_and(carry[1][ATTN_UNITS - 1] > 0, carry[-1] > ATTN_SKIP_BELOW)


def _tri(strict):
    r = lax.broadcasted_iota(jnp.int32, (ATTN_SUB, ATTN_SUB), 0)
    c = lax.broadcasted_iota(jnp.int32, (ATTN_SUB, ATTN_SUB), 1)
    return (r > c if strict else r >= c).astype(BF)


REACH_TILE = (8, 128)


def _first_step_spec():
    return pl.BlockSpec((1, ATTN_ROWS, ATTN_K), lambda h, i: (h, i, 0))


def _reach_spec():
    return pl.BlockSpec((1, 1) + REACH_TILE, lambda h, i: (h, i, 0, 0))


def _head_cols(piece):
    return lambda t: pl.BlockSpec((1, t, HEAD_DIM), lambda h, i: (piece, 0, h))


def _attn_fwd(proj):
    t = proj.shape[1]
    nq = t // ATTN_ROWS
    tri = _tri(strict=True)

    def body(q_ref, k_ref, v_ref, tri_ref, o_ref, a_ref, beta_ref, reach_ref):
        i = pl.program_id(1)
        q = q_ref[0]
        row = i * ATTN_ROWS + lax.broadcasted_iota(jnp.int32, (ATTN_ROWS, 1), 0)

        def step(carry, keep=False):
            blks, bounds, acc, run, _ = carry
            starts = _attn_keys(blks)
            _, mask, log_beta, log_rest = _attn_step(q, k_ref, starts, bounds, row)
            tail, run = _suffix_sums(log_rest, tri_ref[...], run)
            a = jnp.where(mask, jnp.exp(log_beta + tail), 0.0).astype(BF)
            if keep:
                a_ref[0] = a
                beta_ref[0] = jnp.where(mask, jnp.exp(log_beta), 0.0).astype(BF)
            acc = acc + _per_unit(lambda u: _dot(_unit_rows(a, u), v_ref[0, pl.ds(starts[u], ATTN_K), :]))
            return (*_attn_next(blks), acc, run, jnp.max(run))

        first = (*_attn_sweep_start(i, t), jnp.zeros((ATTN_ROWS, HEAD_DIM), F32), jnp.zeros((ATTN_ROWS, 1), F32), jnp.float32(0.0))
        after_first = step(first, keep=True)
        reach_ref[...] = jnp.full(reach_ref.shape, after_first[-1], F32)
        o_ref[...] = lax.while_loop(_attn_more, step, after_first)[2]

    qspec = pl.BlockSpec((1, ATTN_ROWS, HEAD_DIM), lambda h, i: (3, i, h))
    return pl.pallas_call(
        body, name="attn_fwd", grid=(N_HEADS, nq),
        in_specs=[qspec, _head_cols(4)(t), _head_cols(5)(t), pl.BlockSpec((ATTN_SUB, ATTN_SUB), lambda h, i: (0, 0))],
        out_specs=[pl.BlockSpec((ATTN_ROWS, HEAD_DIM), lambda h, i: (i, h)), _first_step_spec(), _first_step_spec(), _reach_spec()],
        out_shape=[S((t, D_MODEL), F32), S((N_HEADS, t, ATTN_K), BF), S((N_HEADS, t, ATTN_K), BF), S((N_HEADS, nq) + REACH_TILE, F32)],
    )(proj, proj, proj, tri)


def _attn_bwd(proj, o, d_o, a_first, beta_first, reach, ride=None):
    t = proj.shape[1]
    nq = t // ATTN_ROWS
    tri_strict, tri_incl = _tri(strict=True), _tri(strict=False)
    scale = 1.0 / math.sqrt(HEAD_DIM)

    def body(q_ref, k_ref, v_ref, o_ref, do_ref, a_ref, beta_ref, reach_ref, tris_ref, trii_ref, dq_ref, dk_ref, dv_ref, dk_acc, dv_acc):
        i = pl.program_id(1)

        @pl.when(i == 0)
        def _():
            dk_acc[...] = jnp.zeros_like(dk_acc)
            dv_acc[...] = jnp.zeros_like(dv_acc)

        q = q_ref[0]
        do = do_ref[...]
        total = jnp.sum(do.astype(F32) * o_ref[...], axis=1, keepdims=True)
        zero = jnp.zeros((ATTN_ROWS, 1), F32)
        blks0, bounds0 = _attn_sweep_start(i, t)

        def finish(starts, a, dz, dq):
            dzb = (dz * scale).astype(BF)
            for u in range(ATTN_UNITS):
                dv_acc[pl.ds(starts[u], ATTN_K), :] += _dot_tn(_unit_rows(a, u), _unit_rows(do, u))
                dk_acc[pl.ds(starts[u], ATTN_K), :] += _dot_tn(_unit_rows(dzb, u), _unit_rows(q, u))
            return dq + _per_unit(lambda u: _dot(_unit_rows(dzb, u), k_ref[0, pl.ds(starts[u], ATTN_K), :]))

        def grad_a(starts, a):
            return _per_unit(lambda u: _dot_nt(_unit_rows(do, u), v_ref[0, pl.ds(starts[u], ATTN_K), :])) * a.astype(F32)

        one_step = jnp.max(reach_ref[...]) <= ATTN_SKIP_BELOW

        @pl.when(one_step)
        def _():
            starts = _attn_keys(blks0)
            a = a_ref[0]
            beta = beta_ref[0].astype(F32)
            de = grad_a(starts, a)
            right, _ = _suffix_sums(de, trii_ref[...], zero)
            dz = de * (1.0 - beta) - (total - right) * beta
            dq_ref[...] = finish(starts, a, dz, jnp.zeros((ATTN_ROWS, HEAD_DIM), F32)).astype(BF)

        @pl.when(jnp.logical_not(one_step))
        def _():
            row = i * ATTN_ROWS + lax.broadcasted_iota(jnp.int32, (ATTN_ROWS, 1), 0)

            def step(carry):
                blks, bounds, dq, seen, run, _ = carry
                starts = _attn_keys(blks)
                z, mask, log_beta, log_rest = _attn_step(q, k_ref, starts, bounds, row)
                tail, run = _suffix_sums(log_rest, tris_ref[...], run)
                a = jnp.where(mask, jnp.exp(log_beta + tail), 0.0).astype(BF)
                de = grad_a(starts, a)
                right, seen = _suffix_sums(de, trii_ref[...], seen)
                beta = jax.nn.sigmoid(z)
                dz = jnp.where(mask, de * (1.0 - beta) - (total - right) * beta, 0.0)
                return (*_attn_next(blks), finish(starts, a, dz, dq), seen, run, jnp.max(run))

            first = (blks0, bounds0, jnp.zeros((ATTN_ROWS, HEAD_DIM), F32), zero, zero, jnp.float32(0.0))
            dq_ref[...] = lax.while_loop(_attn_more, step, step(first))[2].astype(BF)

        @pl.when(i == nq - 1)
        def _():
            dk_ref[...] = dk_acc[...].astype(BF)
            dv_ref[...] = dv_acc[...].astype(BF)

    qspec = pl.BlockSpec((1, ATTN_ROWS, HEAD_DIM), lambda h, i: (3, i, h))
    rowblk = pl.BlockSpec((ATTN_ROWS, HEAD_DIM), lambda h, i: (i, h))
    head = pl.BlockSpec((t, HEAD_DIM), lambda h, i: (0, h))
    trispec = pl.BlockSpec((ATTN_SUB, ATTN_SUB), lambda h, i: (0, 0))
    return _pallas(
        body, name="attn_bwd", grid=(N_HEADS, nq), ride=ride,
        in_specs=[qspec, _head_cols(4)(t), _head_cols(5)(t), rowblk, rowblk, _first_step_spec(), _first_step_spec(), _reach_spec(),
                  trispec, trispec],
        out_specs=[rowblk, head, head],
        out_shape=[S((t, D_MODEL), BF)] * 3,
        scratch_shapes=[pltpu.VMEM((t, HEAD_DIM), F32), pltpu.VMEM((t, HEAD_DIM), F32)],
        args=[proj, proj, proj, o, d_o, a_first, beta_first, reach, tri_strict, tri_incl])


def _tail(h3, n4, p, w_pg, w_pp, g_ple, g_final, target):
    t = h3.shape[0]
    tm = min(TOKEN_TILE, t)
    steps = t // tm

    def body(h_ref, n_ref, p_ref, wpg_ref, wpp_ref, gp_ref, gf_ref, tgt_ref,
             dh_ref, ds_ref, dpp_ref, loss_ref, dgf_ref, dgp_ref):
        pg = jax.nn.sigmoid(_dot(n_ref[...], wpg_ref[...]))
        pp = _dot(p_ref[...].astype(BF), wpp_ref[...])
        h3v = h_ref[...]
        h4 = h3v + pg * pp
        gf = gf_ref[...]
        diff = _rms(h4, gf) - tgt_ref[...]
        _accumulate(loss_ref, jnp.sum(diff * diff, axis=0, keepdims=True))
        dh4, dgf = _rms_bwd(diff * (1.0 / D_MODEL), h4, gf)
        _accumulate(dgf_ref, dgf)
        dpp_ref[...] = (dh4 * pg).astype(BF)
        ds = (dh4 * pp * pg * (1.0 - pg)).astype(BF)
        ds_ref[...] = ds
        dh3, dgp = _rms_bwd(_dot_nt(ds, wpg_ref[...]), h3v, gp_ref[...])
        _accumulate(dgp_ref, dgp)
        dh_ref[...] = dh4 + dh3

        @pl.when(pl.program_id(0) == steps - 1)
        def _():
            loss_ref[...] = jnp.full(loss_ref.shape, 0.5 / D_MODEL * jnp.sum(loss_ref[...]), F32)

    vec = (1, D_MODEL)
    return pl.pallas_call(
        body, name="tail", grid=(steps,),
        in_specs=[_rows(tm, D_MODEL), _rows(tm, D_MODEL), _rows(tm, PLE_DIM), _const_spec((D_MODEL, D_MODEL)),
                  _const_spec((PLE_DIM, D_MODEL)), _const_spec(vec), _const_spec(vec), _rows(tm, D_MODEL)],
        out_specs=[_rows(tm, D_MODEL)] * 3 + [_acc_spec(vec)] * 3,
        out_shape=[S((t, D_MODEL), F32), S((t, D_MODEL), BF), S((t, D_MODEL), BF)] + [S(vec, F32)] * 3,
    )(h3, n4, p, w_pg, w_pp, g_ple, g_final, target)


def _wgrad(xs, ys, name, ride=None):
    bx, t, k = xs.shape
    by, _, n = ys.shape
    b = max(bx, by)
    tt = min(WGRAD_TILE * 2 // xs.dtype.itemsize, t)
    steps = t // tt

    def body(x_ref, y_ref, o_ref, acc_ref):
        s = pl.program_id(1)

        @pl.when(s == 0)
        def _():
            acc_ref[...] = jnp.zeros_like(acc_ref)
        acc_ref[...] += _dot_tn(x_ref[0].astype(BF), y_ref[0].astype(BF))

        @pl.when(s == steps - 1)
        def _():
            o_ref[0] = acc_ref[...].astype(BF)

    (out,), landed = _pallas(
        body, name=name, grid=(b, steps), ride=ride,
        in_specs=[pl.BlockSpec((1, tt, k), (lambda j, s: (j, s, 0)) if bx > 1 else (lambda j, s: (0, s, 0))),
                  pl.BlockSpec((1, tt, n), (lambda j, s: (j, s, 0)) if by > 1 else (lambda j, s: (0, s, 0)))],
        out_specs=[pl.BlockSpec((1, k, n), lambda j, s: (j, 0, 0))],
        out_shape=[S((b, k, n), BF)],
        scratch_shapes=[pltpu.VMEM((k, n), F32)],
        args=[xs, ys])
    return (out, landed) if ride is not None else out


def _wgrad_pieces(x, ys, name, ride=None, tile=None):
    t, k = x.shape
    n = ys[0].shape[2]
    counts = [y.shape[0] for y in ys]
    offsets = [sum(counts[:j]) for j in range(len(ys))]
    total = sum(counts)
    tt = min(tile or WGRAD_TILE, t)
    steps = t // tt

    def body(x_ref, *refs):
        y_refs, o_ref, acc_ref = refs[:len(ys)], refs[len(ys)], refs[len(ys) + 1]
        p, s = pl.program_id(0), pl.program_id(1)

        @pl.when(s == 0)
        def _():
            acc_ref[...] = jnp.zeros_like(acc_ref)
        for j, y_ref in enumerate(y_refs):
            @pl.when(jnp.logical_and(p >= offsets[j], p < offsets[j] + counts[j]))
            def _(y_ref=y_ref):
                acc_ref[...] += _dot_tn(x_ref[...], y_ref[0])

        @pl.when(s == steps - 1)
        def _():
            o_ref[0] = acc_ref[...].astype(BF)

    def turn(j):
        lo, hi = offsets[j], offsets[j] + counts[j]
        return lambda p, s: (jnp.clip(p - lo, 0, counts[j] - 1), jnp.where(p < lo, 0, jnp.where(p >= hi, steps - 1, s)), 0)

    (out,), landed = _pallas(
        body, name=name, grid=(total, steps), ride=ride,
        in_specs=[pl.BlockSpec((tt, k), lambda p, s: (s, 0))] + [pl.BlockSpec((1, tt, n), turn(j)) for j in range(len(ys))],
        out_specs=[pl.BlockSpec((1, k, n), lambda p, s: (p, 0, 0))],
        out_shape=[S((total, k, n), BF)],
        scratch_shapes=[pltpu.VMEM((k, n), F32)],
        args=[x, *ys])
    return (out, landed) if ride is not None else out


def _ffn_bwd_hidden(dh, to_gate, to_up, w_out, name, ride=None):
    t = dh.shape[0]
    tm = min(TOKEN_TILE, t)

    def body(dh_ref, to_gate_ref, to_up_ref, wout_ref, df_ref, dgate_ref, dup_ref):
        df = (0.5 * dh_ref[...]).astype(BF)
        df_ref[...] = df
        for c in range(N_FF_CHUNKS):
            dact = _dot_nt(df, wout_ref[c])
            dgate_ref[c] = (dact * to_gate_ref[c].astype(F32)).astype(BF)
            dup_ref[c] = (dact * to_up_ref[c].astype(F32)).astype(BF)

    return _pallas(
        body, name=name, grid=(t // tm,), ride=ride,
        in_specs=[_rows(tm, D_MODEL), _chunks(tm), _chunks(tm), _const_spec(w_out.shape)],
        out_specs=[_rows(tm, D_MODEL), _chunks(tm), _chunks(tm)],
        out_shape=[S((t, D_MODEL), BF)] + [S((N_FF_CHUNKS, t, FF_CHUNK), BF)] * 2,
        args=[dh, to_gate, to_up, w_out])


def _ffn_bwd_input(dh, h_in, g, dgate, dup, w_in, name, ride=None):
    t = dh.shape[0]
    tm = min(TOKEN_TILE, t)

    def body(dh_ref, h_ref, g_ref, dgate_ref, dup_ref, win_ref, dhi_ref, dg_ref):
        dn = jnp.zeros((tm, D_MODEL), F32)
        for c in range(N_FF_CHUNKS):
            dn = dn + _dot_nt(dgate_ref[c], win_ref[c]) + _dot_nt(dup_ref[c], win_ref[N_FF_CHUNKS + c])
        dhi, dg = _rms_bwd(dn, h_ref[...], g_ref[...])
        _accumulate(dg_ref, dg)
        dhi_ref[...] = dh_ref[...] + dhi

    vec = (1, D_MODEL)
    return _pallas(
        body, name=name, grid=(t // tm,), ride=ride,
        in_specs=[_rows(tm, D_MODEL), _rows(tm, D_MODEL), _const_spec(vec), _chunks(tm), _chunks(tm), _const_spec(w_in.shape)],
        out_specs=[_rows(tm, D_MODEL), _acc_spec(vec)],
        out_shape=[S((t, D_MODEL), F32), S(vec, F32)],
        args=[dh, h_in, g, dgate, dup, w_in])


def _mixer_bwd(dh2, proj, yc, ya, conv_w, w_co, w_ao, w_mo, ride=None):
    t = dh2.shape[0]
    tm = min(TOKEN_TILE, t)

    def body(dh_ref, cb_ref, cc_ref, cx_ref, gc_ref, ga_ref, cch_ref, cxh_ref, yc_ref, ya_ref, cw_ref, wco_ref, wao_ref, wmo_ref,
             dhb_ref, dyc_ref, dya_ref, dgc_ref, dga_ref, dcb_ref, dcv_ref, do_ref):
        dhb = dh_ref[...].astype(BF)
        dhb_ref[...] = dhb
        dmerged = _dot_nt(dhb, wmo_ref[...])
        sc = jax.nn.sigmoid(gc_ref[0].astype(F32))
        sa = jax.nn.sigmoid(ga_ref[0].astype(F32))
        dyc = (dmerged * sc).astype(BF)
        dya = (dmerged * sa).astype(BF)
        dyc_ref[...] = dyc
        dya_ref[...] = dya
        dgc_ref[...] = (dmerged * yc_ref[...].astype(F32) * sc * (1.0 - sc)).astype(BF)
        dga_ref[...] = (dmerged * ya_ref[...].astype(F32) * sa * (1.0 - sa)).astype(BF)
        m, m1, m2 = _conv_inputs(cc_ref, cx_ref, cch_ref, cxh_ref)
        cw = cw_ref[...]
        cv = cw[0:1, :] * m2 + cw[1:2, :] * m1 + cw[2:3, :] * m
        dycin = _dot_nt(dyc, wco_ref[...])
        dcb_ref[...] = (dycin * cv).astype(BF)
        dcv_ref[...] = dycin * cb_ref[0].astype(F32)
        do_ref[...] = _dot_nt(dya, wao_ref[...]).astype(BF)

    sq = (D_MODEL, D_MODEL)
    return _pallas(
        body, name="mixer_bwd", grid=(t // tm,), ride=ride,
        in_specs=[_rows(tm, D_MODEL), _piece(0, tm), _piece(1, tm), _piece(2, tm), _piece(6, tm), _piece(7, tm),
                  _prev_halo(1, tm), _prev_halo(2, tm), _rows(tm, D_MODEL), _rows(tm, D_MODEL),
                  _const_spec((3, D_MODEL)), _const_spec(sq), _const_spec(sq), _const_spec(sq)],
        out_specs=[_rows(tm, D_MODEL)] * 8,
        out_shape=[S((t, D_MODEL), BF)] * 6 + [S((t, D_MODEL), F32), S((t, D_MODEL), BF)],
        args=[dh2, proj, proj, proj, proj, proj, proj, proj, yc, ya, conv_w, w_co, w_ao, w_mo])


F32_HALO = 8


def _conv_bwd(dcv, proj, conv_w):
    t = dcv.shape[0]
    tm = min(TOKEN_TILE, t)
    steps = t // tm

    def body(dcv_ref, nxt_ref, cc_ref, cx_ref, cch_ref, cxh_ref, cw_ref, dcc_ref, dcx_ref, dw_ref):
        i = pl.program_id(0)
        m, m1, m2 = _conv_inputs(cc_ref, cx_ref, cch_ref, cxh_ref)
        d0 = dcv_ref[...]
        nxt = jnp.where(i == steps - 1, 0.0, nxt_ref[...])
        row = lax.broadcasted_iota(jnp.int32, (tm, 1), 0)
        d1 = jnp.where(row == tm - 1, nxt[0:1, :], pltpu.roll(d0, tm - 1, 0))
        d2 = pltpu.roll(d0, tm - 2, 0)
        d2 = jnp.where(row == tm - 2, nxt[0:1, :], jnp.where(row == tm - 1, nxt[1:2, :], d2))
        cw = cw_ref[...]
        dm = cw[2:3, :] * d0 + cw[1:2, :] * d1 + cw[0:1, :] * d2
        dcc_ref[...] = (dm * cx_ref[0].astype(F32)).astype(BF)
        dcx_ref[...] = (dm * cc_ref[0].astype(F32)).astype(BF)
        tap_row = lax.broadcasted_iota(jnp.int32, (F32_HALO, 1), 0)
        dw = jnp.zeros((F32_HALO, D_MODEL), F32)
        for j, mk in enumerate((m2, m1, m)):
            dw = jnp.where(tap_row == j, jnp.sum(d0 * mk, axis=0, keepdims=True), dw)
        _accumulate(dw_ref, dw)

    nxt_spec = pl.BlockSpec((F32_HALO, D_MODEL), lambda i: (jnp.minimum((i + 1) * (tm // F32_HALO), t // F32_HALO - 1), 0))
    return pl.pallas_call(
        body, name="conv_bwd", grid=(steps,),
        in_specs=[_rows(tm, D_MODEL), nxt_spec, _piece(1, tm), _piece(2, tm), _prev_halo(1, tm), _prev_halo(2, tm),
                  _const_spec((3, D_MODEL))],
        out_specs=[_rows(tm, D_MODEL), _rows(tm, D_MODEL), _acc_spec((F32_HALO, D_MODEL))],
        out_shape=[S((t, D_MODEL), BF), S((t, D_MODEL), BF), S((F32_HALO, D_MODEL), F32)],
    )(dcv, dcv, proj, proj, proj, proj, conv_w)


def _mix_bwd(dpieces, w_mix, h1, dh2, g, ride=None):
    t = h1.shape[0]
    tm = min(TOKEN_TILE, t)

    def body(*refs):
        pieces, (w_ref, h_ref, dh_ref, g_ref, dhi_ref, dg_ref) = refs[:N_MIX], refs[N_MIX:]
        du = jnp.zeros((tm, D_MODEL), F32)
        for d in range(N_MIX):
            du = du + _dot_nt(pieces[d][...], w_ref[d])
        dhi, dg = _rms_bwd(du, h_ref[...], g_ref[...])
        _accumulate(dg_ref, dg)
        dhi_ref[...] = dh_ref[...] + dhi

    vec = (1, D_MODEL)
    return _pallas(
        body, name="mix_bwd", grid=(t // tm,), ride=ride,
        in_specs=[_rows(tm, D_MODEL)] * N_MIX + [_const_spec(w_mix.shape), _rows(tm, D_MODEL), _rows(tm, D_MODEL), _const_spec(vec)],
        out_specs=[_rows(tm, D_MODEL), _acc_spec(vec)],
        out_shape=[S((t, D_MODEL), F32), S(vec, F32)],
        args=[*dpieces, w_mix, h1, dh2, g])


def _adamw(partials, w, m, v, name):
    r, c = w.shape
    tr = min(r, 512)
    c1 = 1.0 - ADAM_B1 ** ADAM_STEP
    c2 = 1.0 - ADAM_B2 ** ADAM_STEP

    def body(p_ref, w_ref, m_ref, v_ref, g_ref, d_ref, mo_ref, vo_ref):
        g = p_ref[0].astype(F32)
        for s in range(1, N_SHARDS):
            g = g + p_ref[s].astype(F32)
        mn = ADAM_B1 * m_ref[...] + (1.0 - ADAM_B1) * g
        vn = ADAM_B2 * v_ref[...] + (1.0 - ADAM_B2) * (g * g)
        g_ref[...] = g
        mo_ref[...] = mn
        vo_ref[...] = vn
        d_ref[...] = -ADAM_LR * ((mn / c1) / (jnp.sqrt(vn / c2) + ADAM_EPS) + ADAM_WD * w_ref[...])

    blk = pl.BlockSpec((tr, c), lambda i: (i, 0))
    return pl.pallas_call(
        body, name=name, grid=(r // tr,),
        in_specs=[pl.BlockSpec((N_SHARDS, tr, c), lambda i: (0, i, 0)), blk, blk, blk],
        out_specs=[blk] * 4, out_shape=[S((r, c), F32)] * 4,
    )(partials, w, m, v)


_MATRICES = ("ffn1_w_in", "ffn1_w_out", "w_mix_in", "conv_w", "w_conv_out", "w_attn_out", "w_mix_out",
             "ffn2_w_in", "ffn2_w_out", "w_ple_gate", "w_ple_proj")
_GAINS = ("ffn1_norm", "mix_norm", "ffn2_norm", "ple_norm", "final_norm")
_WEIGHTS = ("ffn1_norm", "ffn1_w_in", "ffn1_w_out", "mix_norm", "w_mix_in", "conv_w", "w_conv_out", "w_attn_out", "w_mix_out",
            "ffn2_norm", "ffn2_w_in", "ffn2_w_out", "ple_norm", "w_ple_gate", "w_ple_proj", "final_norm")
CONV_ROWS = 8


def _columns_from_shards(g):
    return jnp.transpose(g, (1, 0, 2)).reshape(g.shape[1], N_SHARDS * g.shape[2])


def _shards_from_columns(a):
    r, c = a.shape
    return jnp.transpose(a.reshape(r, N_SHARDS, c // N_SHARDS), (1, 0, 2))


def kernel(x, p, ffn1_norm, ffn1_w_in, ffn1_w_out, mix_norm, w_mix_in, conv_w, w_conv_out, w_attn_out, w_mix_out, ffn2_norm, ffn2_w_in, ffn2_w_out, ple_norm, w_ple_gate, w_ple_proj, final_norm, loss_target, m_ffn1_norm, m_ffn1_w_in, m_ffn1_w_out, m_mix_norm, m_w_mix_in, m_conv_w, m_w_conv_out, m_w_attn_out, m_w_mix_out, m_ffn2_norm, m_ffn2_w_in, m_ffn2_w_out, m_ple_norm, m_w_ple_gate, m_w_ple_proj, m_final_norm, v_ffn1_norm, v_ffn1_w_in, v_ffn1_w_out, v_mix_norm, v_w_mix_in, v_conv_w, v_w_conv_out, v_w_attn_out, v_w_mix_out, v_ffn2_norm, v_ffn2_w_in, v_ffn2_w_out, v_ple_norm, v_w_ple_gate, v_w_ple_proj, v_final_norm):
    given = dict(locals())
    t = x.shape[1]
    xs = x.reshape(t, D_MODEL)
    ps = p.reshape(t, PLE_DIM)
    target = loss_target.reshape(t, D_MODEL)
    shard = {k: given[k].reshape(given[k].shape[-2:]) for k in _MATRICES}
    gain = {k: given[k].reshape(1, D_MODEL) for k in _GAINS}

    send = {k: shard[k].astype(BF) for k in _MATRICES}
    send["conv_w"] = jnp.pad(shard["conv_w"], ((0, CONV_ROWS - 3), (0, 0)))
    loss_vec, dx, landed, gain_grads = _forward_backward(xs, ps, target, gain, send)
    gain_rows = jnp.concatenate([gain_grads[k] for k in _GAINS] + [jnp.zeros((8 - len(_GAINS), D_MODEL), F32)], axis=0)
    gain_parts, = _exchange_alone("gather", [gain_rows], "gather_gain_gradients")

    out = {}
    for k in _MATRICES:
        w, m, v = shard[k], given["m_" + k].reshape(shard[k].shape), given["v_" + k].reshape(shard[k].shape)
        part = landed[k]
        if k == "conv_w":
            pad = ((0, CONV_ROWS - 3), (0, 0))
            w, m, v = jnp.pad(w, pad), jnp.pad(m, pad), jnp.pad(v, pad, constant_values=1.0)
        res = _adamw(part, w, m, v, "adamw_" + k)
        out[k] = [r[:3] if k == "conv_w" else r for r in res]
    stack = lambda pre: jnp.concatenate([given[pre + k].reshape(1, D_MODEL) for k in _GAINS] + [jnp.ones((8 - len(_GAINS), D_MODEL), F32)], axis=0)
    res = _adamw(gain_parts, stack(""), stack("m_"), stack("v_"), "adamw_gains")
    for j, k in enumerate(_GAINS):
        out[k] = [r[j:j + 1] for r in res]

    loss = lax.psum(loss_vec[0, 0], ("x", "y", "c"))
    per_kind = [[out[k][j].reshape(given[k].shape) for k in _WEIGHTS] for j in range(4)]
    return (loss, dx.reshape(x.shape), *per_kind[0], *per_kind[1], *per_kind[2], *per_kind[3])


def _forward_backward(xs, ps, target, gain, send, full=None):
    exchange = full is None
    full = dict(full or {})
    grads, landed = {}, {}

    def gather(names):
        return ("gather", [send[k] for k in names]) if exchange else None

    def scatter(names):
        return ("scatter", [grads[k] for k in names]) if exchange else None

    def keep(into, names, got):
        into.update(zip(names, got))

    first = ("ffn1_w_in", "ffn1_w_out")
    (n1,), got = _prenorm(xs, gain["ffn1_norm"], ride=gather(first))
    keep(full, first, got)
    w1_in, w1_out = full["ffn1_w_in"], full["ffn1_w_out"].reshape(N_FF_CHUNKS, FF_CHUNK, D_MODEL)
    mixer = ("w_mix_in", "conv_w", "w_conv_out", "w_attn_out", "w_mix_out")
    (h1, u, act1, to_gate1, to_up1), got = _ffn_fwd(xs, n1, w1_in, w1_out, gain["mix_norm"], "ffn1_fwd", ride=gather(mixer))
    keep(full, mixer, got)
    w_mix = full["w_mix_in"]
    w_co, w_ao, w_mo = (full[k].reshape(D_MODEL, D_MODEL) for k in ("w_conv_out", "w_attn_out", "w_mix_out"))
    taps = _columns_from_shards(full["conv_w"][:, :3, :])
    rest = ("ffn2_w_in", "ffn2_w_out", "w_ple_gate", "w_ple_proj")
    (proj,), got = _mix_proj(u, w_mix, ride=gather(rest))
    keep(full, rest, got)
    w2_in, w2_out = full["ffn2_w_in"], full["ffn2_w_out"].reshape(N_FF_CHUNKS, FF_CHUNK, D_MODEL)
    w_pg = full["w_ple_gate"].reshape(D_MODEL, D_MODEL)
    w_pp = _columns_from_shards(full["w_ple_proj"])
    o, a_first, beta_first, reach = _attn_fwd(proj)
    h2, n3, ycin, yc, ya, merged = _mixer_out(proj, o, h1, taps, w_co, w_ao, w_mo, gain["ffn2_norm"])
    (h3, n4, act2, to_gate2, to_up2), _ = _ffn_fwd(h2, n3, w2_in, w2_out, gain["ple_norm"], "ffn2_fwd")
    dh3, ds, dpp, loss_vec, dg_final, dg_ple = _tail(h3, n4, ps, w_pg, w_pp, gain["ple_norm"], gain["final_norm"], target)

    one = lambda a: a[None]
    by_rows = lambda g, rows: g.reshape(N_SHARDS, rows // N_SHARDS, D_MODEL)
    grads["w_ple_gate"] = by_rows(_wgrad(one(n4), one(ds), "wgrad_ple_gate"), D_MODEL)
    grads["w_ple_proj"] = _shards_from_columns(_wgrad(one(ps), one(dpp), "wgrad_ple_proj")[0])
    ple = ("w_ple_gate", "w_ple_proj")
    (df2, dgate2, dup2), got = _ffn_bwd_hidden(dh3, to_gate2, to_up2, w2_out, "ffn2_bwd_hidden", ride=scatter(ple))
    keep(landed, ple, got)
    grads["ffn2_w_out"] = by_rows(_wgrad(act2, one(df2), "wgrad_ffn2_out"), D_FF)
    grads["ffn2_w_in"] = _wgrad_pieces(n3, [dgate2, dup2], "wgrad_ffn2_in")
    (dh2, dg_ffn2), got = _ffn_bwd_input(dh3, h2, gain["ffn2_norm"], dgate2, dup2, w2_in, "ffn2_bwd_input", ride=scatter(("ffn2_w_out",)))
    keep(landed, ("ffn2_w_out",), got)
    (dh2b, dyc, dya, dgc, dga, dcb, dcv, d_o), got = _mixer_bwd(dh2, proj, yc, ya, taps, w_co, w_ao, w_mo, ride=scatter(("ffn2_w_in",)))
    keep(landed, ("ffn2_w_in",), got)
    grads["w_mix_out"] = by_rows(_wgrad(one(merged), one(dh2b), "wgrad_mix_out"), D_MODEL)
    grads["w_conv_out"] = by_rows(_wgrad(one(ycin), one(dyc), "wgrad_conv_out"), D_MODEL)
    grads["w_attn_out"] = by_rows(_wgrad(one(o), one(dya), "wgrad_attn_out"), D_MODEL)
    dcc, dcx, dtaps = _conv_bwd(dcv, proj, taps)
    grads["conv_w"] = jnp.pad(_shards_from_columns(dtaps[:3]), ((0, 0), (0, CONV_ROWS - 3), (0, 0)))
    behind_attn = ("w_mix_out", "w_conv_out", "w_attn_out", "conv_w")
    (dq, dk, dv), got = _attn_bwd(proj, o, d_o, a_first, beta_first, reach, ride=scatter(behind_attn))
    keep(landed, behind_attn, got)
    dpieces = [dcb, dcc, dcx, dq, dk, dv, dgc, dga]
    half = N_MIX // 2
    grads["w_mix_in"] = jnp.concatenate([_wgrad_pieces(u, [one(dp) for dp in dpieces[:half]], "wgrad_mix_in_a", tile=WGRAD_TILE // 2),
                                         _wgrad_pieces(u, [one(dp) for dp in dpieces[half:]], "wgrad_mix_in_b", tile=WGRAD_TILE // 2)], axis=0)
    (dh1, dg_mix), got = _mix_bwd(dpieces, w_mix, h1, dh2, gain["mix_norm"], ride=scatter(("w_mix_in",)))
    keep(landed, ("w_mix_in",), got)
    (df1, dgate1, dup1), _ = _ffn_bwd_hidden(dh1, to_gate1, to_up1, w1_out, "ffn1_bwd_hidden")
    grads["ffn1_w_out"] = by_rows(_wgrad(act1, one(df1), "wgrad_ffn1_out"), D_FF)
    if exchange:
        grads["ffn1_w_in"], got = _wgrad_pieces(n1, [dgate1, dup1], "wgrad_ffn1_in", ride=scatter(("ffn1_w_out",)))
        keep(landed, ("ffn1_w_out",), got)
    else:
        grads["ffn1_w_in"] = _wgrad_pieces(n1, [dgate1, dup1], "wgrad_ffn1_in")
    (dx, dg_ffn1), got = _ffn_bwd_input(dh1, xs, gain["ffn1_norm"], dgate1, dup1, w1_in, "ffn1_bwd_input", ride=scatter(("ffn1_w_in",)))
    keep(landed, ("ffn1_w_in",), got)
    gain_grads = dict(ffn1_norm=dg_ffn1, mix_norm=dg_mix, ffn2_norm=dg_ffn2, ple_norm=dg_ple, final_norm=dg_final)
    return loss_vec, dx, (landed if exchange else grads), gain_grads
```

```python
import functools
import math

import jax
import jax.numpy as jnp
from jax import lax
from jax.experimental import pallas as pl
from jax.experimental.pallas import tpu as pltpu

D_MODEL = 1024
D_FF = 2816
N_SHARDS = 8
FF_CHUNK = 2 * D_FF // N_SHARDS
N_FF_CHUNKS = D_FF // FF_CHUNK
N_HEADS = 8
HEAD_DIM = 128
PLE_DIM = 256
NORM_EPS = 1e-6
N_MIX = 8
ADAM_LR, ADAM_B1, ADAM_B2, ADAM_EPS, ADAM_WD, ADAM_STEP = 0.001, 0.9, 0.999, 1e-08, 0.01, 10

TOKEN_TILE = 512
WGRAD_TILE = 4096
PROJ_TILE = 2048
ATTN_ROWS = 512
ATTN_Q = 128
ATTN_SUB = 128
ATTN_K = 3 * ATTN_SUB
ATTN_SKIP_BELOW = -90.0

BF = jnp.bfloat16
F32 = jnp.float32
MESH = pl.DeviceIdType.MESH
NT = (((1,), (1,)), ((), ()))
TN = (((0,), (0,)), ((), ()))
S = jax.ShapeDtypeStruct
ANY = pl.BlockSpec(memory_space=pl.ANY)


def _const_spec(shape):
    nd = len(shape)
    return pl.BlockSpec(shape, lambda *_: (0,) * nd, pipeline_mode=pl.Buffered(1))


def _rows(tm, cols):
    return pl.BlockSpec((tm, cols), lambda i: (i, 0))


def _chunks(tm):
    return pl.BlockSpec((N_FF_CHUNKS, tm, FF_CHUNK), lambda i: (0, i, 0))


def _acc_spec(shape):
    nd = len(shape)
    return pl.BlockSpec(shape, lambda *_: (0,) * nd)


def _dot(a, b):
    return jnp.dot(a, b, preferred_element_type=F32)


def _dot_nt(a, b):
    return lax.dot_general(a, b, NT, preferred_element_type=F32)


def _dot_tn(a, b):
    return lax.dot_general(a, b, TN, preferred_element_type=F32)


def _rms(h, g):
    r = lax.rsqrt(jnp.mean(h * h, axis=-1, keepdims=True) + NORM_EPS)
    return h * r * g


def _rms_bwd(dn, h, g):
    r = lax.rsqrt(jnp.mean(h * h, axis=-1, keepdims=True) + NORM_EPS)
    nh = h * r
    gd = dn * g
    dh = r * (gd - nh * jnp.mean(gd * nh, axis=-1, keepdims=True))
    return dh, jnp.sum(dn * nh, axis=0, keepdims=True)


def _accumulate(ref, val):
    @pl.when(pl.program_id(0) == 0)
    def _():
        ref[...] = jnp.zeros_like(ref)
    ref[...] += val


def _place():
    x, y, c = lax.axis_index("x"), lax.axis_index("y"), lax.axis_index("c")
    return x, y, c


def _slot(px, py, pc):
    return 4 * px + 2 * py + pc


def _gather_phases(ins, outs, send_sems, recv_sems, local_sems):
    n = len(ins)

    def parties():
        x, y, c = _place()
        return (x, y, c), (x, y, 1 - c), [(1 - x, y), (x, 1 - y), (1 - x, 1 - y)], c

    def copy(a, k, block, to, src=None):
        dst = outs[a].at[_slot(*block)]
        return pltpu.make_async_remote_copy(
            src_ref=dst if src is None else src, dst_ref=dst,
            send_sem=send_sems.at[a, k], recv_sem=recv_sems.at[a, k],
            device_id=to, device_id_type=MESH)

    def own(a, me):
        return pltpu.make_async_copy(ins[a], outs[a].at[_slot(*me)], local_sems.at[a])

    def first(a, me, sibling, chips, c):
        return [copy(a, 0, me, sibling, src=ins[a])] + [copy(a, 1 + j, me, (*chip, c), src=ins[a]) for j, chip in enumerate(chips)]

    def start():
        me, sibling, chips, c = parties()
        for a in range(n):
            own(a, me).start()
        for a in range(n):
            for cp in first(a, me, sibling, chips, c):
                cp.start()

    def forward():
        me, sibling, chips, c = parties()
        for j, chip in enumerate(chips):
            for a in range(n):
                copy(a, 1 + j, (*chip, c), me).wait_recv()
                copy(a, 4 + j, (*chip, c), sibling).start()

    def finish():
        me, sibling, chips, c = parties()
        for a in range(n):
            copy(a, 0, sibling, me).wait_recv()
            for j, chip in enumerate(chips):
                copy(a, 4 + j, (*chip, 1 - c), me).wait_recv()
        for a in range(n):
            for cp in first(a, me, sibling, chips, c) + [copy(a, 4 + j, (*chip, c), sibling) for j, chip in enumerate(chips)]:
                cp.wait_send()
            own(a, me).wait()

    return [start, forward, finish]


def _scatter_phases(ins, outs, send_sems, recv_sems, local_sems):
    n = len(ins)

    def copies():
        x, y, c = _place()
        me = _slot(x, y, c)
        out = [pltpu.make_async_copy(ins[a].at[me], outs[a].at[me], local_sems.at[a]) for a in range(n)]
        for k in range(1, N_SHARDS):
            px = 1 - x if k & 4 else x
            py = 1 - y if k & 2 else y
            pc = 1 - c if k & 1 else c
            for a in range(n):
                out.append(pltpu.make_async_remote_copy(
                    src_ref=ins[a].at[_slot(px, py, pc)], dst_ref=outs[a].at[me],
                    send_sem=send_sems.at[a, k - 1], recv_sem=recv_sems.at[a, k - 1],
                    device_id=(px, py, pc), device_id_type=MESH))
        return out

    def start():
        for cp in copies():
            cp.start()

    def finish():
        for cp in copies():
            cp.wait()

    return [start, finish]


def _pallas(body, *, name, grid, in_specs, out_specs, out_shape, args, scratch_shapes=(), ride=None):
    if ride is None:
        outs = pl.pallas_call(body, name=name, grid=grid, in_specs=in_specs, out_specs=out_specs, out_shape=out_shape,
                              scratch_shapes=list(scratch_shapes))(*args)
        return list(outs), []
    kind, arrays = ride
    n, n_in, n_out, n_scr = len(arrays), len(in_specs), len(out_specs), len(scratch_shapes)
    total = math.prod(grid)
    middle = (2 * total) // 3
    landed_shape = [S((N_SHARDS,) + a.shape if kind == "gather" else a.shape, a.dtype) for a in arrays]

    def with_exchange(*refs):
        ins, riders_in = refs[:n_in], refs[n_in:n_in + n]
        outs, riders_out = refs[n_in + n:n_in + n + n_out], refs[n_in + n + n_out:n_in + 2 * n + n_out]
        scratch, sems = refs[n_in + 2 * n + n_out:n_in + 2 * n + n_out + n_scr], refs[n_in + 2 * n + n_out + n_scr:]
        step = 0
        for axis, size in enumerate(grid):
            step = step * size + pl.program_id(axis)
        phases = (_gather_phases if kind == "gather" else _scatter_phases)(riders_in, riders_out, *sems)
        pl.when(step == 0)(phases[0])
        body(*ins, *outs, *scratch)
        for phase in phases[1:-1]:
            pl.when(step == middle)(phase)
        pl.when(step == total - 1)(phases[-1])

    outs = pl.pallas_call(
        with_exchange, name=name, grid=grid,
        in_specs=list(in_specs) + [ANY] * n, out_specs=list(out_specs) + [ANY] * n,
        out_shape=list(out_shape) + landed_shape,
        scratch_shapes=list(scratch_shapes) + [pltpu.SemaphoreType.DMA((n, 7)), pltpu.SemaphoreType.DMA((n, 7)),
                                               pltpu.SemaphoreType.DMA((n,))],
    )(*args, *arrays)
    return list(outs[:n_out]), list(outs[n_out:])


def _exchange_alone(kind, arrays, name):
    return _pallas(lambda: None, name=name, grid=(1,), in_specs=[], out_specs=[], out_shape=[], args=[], ride=(kind, arrays))[1]


def _prenorm(x, g, ride=None):
    t = x.shape[0]
    tm = min(TOKEN_TILE, t)

    def body(x_ref, g_ref, n_ref):
        n_ref[...] = _rms(x_ref[...], g_ref[...]).astype(BF)

    return _pallas(
        body, name="prenorm", grid=(t // tm,), ride=ride,
        in_specs=[_rows(tm, D_MODEL), _const_spec((1, D_MODEL))], out_specs=[_rows(tm, D_MODEL)],
        out_shape=[S((t, D_MODEL), BF)], args=[x, g])


def _ffn_fwd(h, n, w_in, w_out, g_next, name, ride=None):
    t = h.shape[0]
    tm = min(TOKEN_TILE, t)

    def body(h_ref, n_ref, win_ref, wout_ref, g_ref, ho_ref, no_ref, act_ref, to_gate_ref, to_up_ref):
        nb = n_ref[...]
        acc = jnp.zeros((tm, D_MODEL), F32)
        for c in range(N_FF_CHUNKS):
            gate = _dot(nb, win_ref[c])
            up = _dot(nb, win_ref[N_FF_CHUNKS + c])
            sg = jax.nn.sigmoid(gate)
            silu = gate * sg
            act = (silu * up).astype(BF)
            act_ref[c] = act
            to_gate_ref[c] = (up * (sg * (1.0 + gate * (1.0 - sg)))).astype(BF)
            to_up_ref[c] = silu.astype(BF)
            acc = acc + _dot(act, wout_ref[c])
        ho = h_ref[...] + 0.5 * acc
        ho_ref[...] = ho
        no_ref[...] = _rms(ho, g_ref[...]).astype(BF)

    return _pallas(
        body, name=name, grid=(t // tm,), ride=ride,
        in_specs=[_rows(tm, D_MODEL), _rows(tm, D_MODEL), _const_spec(w_in.shape), _const_spec(w_out.shape),
                  _const_spec((1, D_MODEL))],
        out_specs=[_rows(tm, D_MODEL), _rows(tm, D_MODEL), _chunks(tm), _chunks(tm), _chunks(tm)],
        out_shape=[S((t, D_MODEL), F32), S((t, D_MODEL), BF)] + [S((N_FF_CHUNKS, t, FF_CHUNK), BF)] * 3,
        args=[h, n, w_in, w_out, g_next])


def _mix_proj(u, w_mix, ride=None):
    t = u.shape[0]
    tm = min(PROJ_TILE, t)

    def body(u_ref, w_ref, o_ref):
        o_ref[0] = _dot(u_ref[...], w_ref[0]).astype(BF)

    return _pallas(
        body, name="mix_proj", grid=(N_MIX, t // tm), ride=ride,
        in_specs=[pl.BlockSpec((tm, D_MODEL), lambda d, i: (i, 0)), pl.BlockSpec((1, D_MODEL, D_MODEL), lambda d, i: (d, 0, 0))],
        out_specs=[pl.BlockSpec((1, tm, D_MODEL), lambda d, i: (d, i, 0))],
        out_shape=[S((N_MIX, t, D_MODEL), BF)], args=[u, w_mix])


HALO = 16


def _piece(d, tm):
    return pl.BlockSpec((1, tm, D_MODEL), lambda i: (d, i, 0))


def _prev_halo(d, tm):
    return pl.BlockSpec((1, HALO, D_MODEL), lambda i: (d, jnp.maximum(i * (tm // HALO) - 1, 0), 0))


def _shift_down(m, prev_tail, k):
    tm = m.shape[0]
    out = pltpu.roll(m, k, 0)
    row = lax.broadcasted_iota(jnp.int32, (tm, 1), 0)
    for j in range(k):
        out = jnp.where(row == j, prev_tail[HALO - k + j:HALO - k + j + 1, :], out)
    return out


def _conv_inputs(cc_ref, cx_ref, cch_ref, cxh_ref):
    m = cc_ref[0].astype(F32) * cx_ref[0].astype(F32)
    mh = cch_ref[0].astype(F32) * cxh_ref[0].astype(F32)
    mh = jnp.where(pl.program_id(0) == 0, 0.0, mh)
    return m, _shift_down(m, mh, 1), _shift_down(m, mh, 2)


def _mixer_out(proj, o, h1, conv_w, w_co, w_ao, w_mo, g_next):
    t = h1.shape[0]
    tm = min(TOKEN_TILE, t)

    def body(cb_ref, cc_ref, cx_ref, gc_ref, ga_ref, cch_ref, cxh_ref, o_ref, h_ref, cw_ref, wco_ref, wao_ref, wmo_ref,
             g_ref, ho_ref, no_ref, ycin_ref, yc_ref, ya_ref, mg_ref):
        m, m1, m2 = _conv_inputs(cc_ref, cx_ref, cch_ref, cxh_ref)
        cw = cw_ref[...]
        cv = cw[0:1, :] * m2 + cw[1:2, :] * m1 + cw[2:3, :] * m
        ycin = (cb_ref[0].astype(F32) * cv).astype(BF)
        ycin_ref[...] = ycin
        yc = _dot(ycin, wco_ref[...])
        ya = _dot(o_ref[...].astype(BF), wao_ref[...])
        yc_ref[...] = yc.astype(BF)
        ya_ref[...] = ya.astype(BF)
        merged = (jax.nn.sigmoid(gc_ref[0].astype(F32)) * yc + jax.nn.sigmoid(ga_ref[0].astype(F32)) * ya).astype(BF)
        mg_ref[...] = merged
        ho = h_ref[...] + _dot(merged, wmo_ref[...])
        ho_ref[...] = ho
        no_ref[...] = _rms(ho, g_ref[...]).astype(BF)

    sq = (D_MODEL, D_MODEL)
    return pl.pallas_call(
        body, name="mixer_out", grid=(t // tm,),
        in_specs=[_piece(0, tm), _piece(1, tm), _piece(2, tm), _piece(6, tm), _piece(7, tm), _prev_halo(1, tm), _prev_halo(2, tm),
                  _rows(tm, D_MODEL), _rows(tm, D_MODEL), _const_spec((3, D_MODEL)), _const_spec(sq), _const_spec(sq),
                  _const_spec(sq), _const_spec((1, D_MODEL))],
        out_specs=[_rows(tm, D_MODEL)] * 6,
        out_shape=[S((t, D_MODEL), F32)] + [S((t, D_MODEL), BF)] * 5,
    )(proj, proj, proj, proj, proj, proj, proj, o, h1, conv_w, w_co, w_ao, w_mo, g_next)


def _suffix_sums(vals, tri, before):
    out, right = [], before
    for b in reversed(range(ATTN_K // ATTN_SUB)):
        v = vals[:, b * ATTN_SUB:(b + 1) * ATTN_SUB]
        hi = v.astype(BF)
        lo = (v - hi.astype(F32)).astype(BF)
        out.append(_dot(hi, tri) + _dot(lo, tri) + right)
        right = right + jnp.sum(v, axis=1, keepdims=True)
    return jnp.concatenate(out[::-1], axis=1), right


ATTN_UNITS = ATTN_ROWS // ATTN_Q


def _unit_rows(x, u):
    return x[u * ATTN_Q:(u + 1) * ATTN_Q]


def _per_unit(fn):
    return jnp.concatenate([fn(u) for u in range(ATTN_UNITS)], axis=0)


def _per_row(vals):
    local = lax.broadcasted_iota(jnp.int32, (ATTN_ROWS, 1), 0)
    out = jnp.full((ATTN_ROWS, 1), vals[0], jnp.int32)
    for u in range(1, ATTN_UNITS):
        out = jnp.where(local >= u * ATTN_Q, vals[u], out)
    return out


def _attn_step(q, k_ref, starts, bounds, row):
    z = _per_unit(lambda u: _dot_nt(_unit_rows(q, u), k_ref[0, pl.ds(starts[u], ATTN_K), :])) * (1.0 / math.sqrt(HEAD_DIM))
    col = _per_row(starts) + lax.broadcasted_iota(jnp.int32, (1, ATTN_K), 1)
    mask = jnp.logical_and(col < row, col < _per_row(bounds))
    soft = jnp.log(1.0 + jnp.exp(-jnp.abs(z)))
    log_beta = jnp.minimum(z, 0.0) - soft
    log_rest = jnp.where(mask, jnp.minimum(-z, 0.0) - soft, 0.0)
    return z, mask, log_beta, log_rest


def _attn_sweep_start(i, t):
    blks = tuple(jnp.maximum(i * ATTN_UNITS + u + 1 - ATTN_K // ATTN_SUB, 0) for u in range(ATTN_UNITS))
    return blks, tuple(jnp.int32(t) for _ in range(ATTN_UNITS))


def _attn_keys(blks):
    return [pl.multiple_of(b * ATTN_SUB, ATTN_SUB) for b in blks]


def _attn_next(blks):
    return tuple(jnp.maximum(b - ATTN_K // ATTN_SUB, 0) for b in blks), tuple(b * ATTN_SUB for b in blks)


def _attn_more(carry):
    return jnp.logical_and(carry[1][ATTN_UNITS - 1] > 0, carry[-1] > ATTN_SKIP_BELOW)


def _tri(strict):
    r = lax.broadcasted_iota(jnp.int32, (ATTN_SUB, ATTN_SUB), 0)
    c = lax.broadcasted_iota(jnp.int32, (ATTN_SUB, ATTN_SUB), 1)
    return (r > c if strict else r >= c).astype(BF)


REACH_TILE = (8, 128)


def _first_step_spec():
    return pl.BlockSpec((1, ATTN_ROWS, ATTN_K), lambda h, i: (h, i, 0))


def _reach_spec():
    return pl.BlockSpec((1, 1) + REACH_TILE, lambda h, i: (h, i, 0, 0))


def _head_cols(piece):
    return lambda t: pl.BlockSpec((1, t, HEAD_DIM), lambda h, i: (piece, 0, h))


def _attn_fwd(proj):
    t = proj.shape[1]
    nq = t // ATTN_ROWS
    tri = _tri(strict=True)

    def body(q_ref, k_ref, v_ref, tri_ref, o_ref, a_ref, beta_ref, reach_ref):
        i = pl.program_id(1)
        q = q_ref[0]
        row = i * ATTN_ROWS + lax.broadcasted_iota(jnp.int32, (ATTN_ROWS, 1), 0)

        def step(carry, keep=False):
            blks, bounds, acc, run, _ = carry
            starts = _attn_keys(blks)
            _, mask, log_beta, log_rest = _attn_step(q, k_ref, starts, bounds, row)
            tail, run = _suffix_sums(log_rest, tri_ref[...], run)
            a = jnp.where(mask, jnp.exp(log_beta + tail), 0.0).astype(BF)
            if keep:
                a_ref[0] = a
                beta_ref[0] = jnp.where(mask, jnp.exp(log_beta), 0.0).astype(BF)
            acc = acc + _per_unit(lambda u: _dot(_unit_rows(a, u), v_ref[0, pl.ds(starts[u], ATTN_K), :]))
            return (*_attn_next(blks), acc, run, jnp.max(run))

        first = (*_attn_sweep_start(i, t), jnp.zeros((ATTN_ROWS, HEAD_DIM), F32), jnp.zeros((ATTN_ROWS, 1), F32), jnp.float32(0.0))
        after_first = step(first, keep=True)
        reach_ref[...] = jnp.full(reach_ref.shape, after_first[-1], F32)
        o_ref[...] = lax.while_loop(_attn_more, step, after_first)[2]

    qspec = pl.BlockSpec((1, ATTN_ROWS, HEAD_DIM), lambda h, i: (3, i, h))
    return pl.pallas_call(
        body, name="attn_fwd", grid=(N_HEADS, nq),
        in_specs=[qspec, _head_cols(4)(t), _head_cols(5)(t), pl.BlockSpec((ATTN_SUB, ATTN_SUB), lambda h, i: (0, 0))],
        out_specs=[pl.BlockSpec((ATTN_ROWS, HEAD_DIM), lambda h, i: (i, h)), _first_step_spec(), _first_step_spec(), _reach_spec()],
        out_shape=[S((t, D_MODEL), F32), S((N_HEADS, t, ATTN_K), BF), S((N_HEADS, t, ATTN_K), BF), S((N_HEADS, nq) + REACH_TILE, F32)],
    )(proj, proj, proj, tri)


def _attn_bwd(proj, o, d_o, a_first, beta_first, reach, ride=None):
    t = proj.shape[1]
    nq = t // ATTN_ROWS
    tri_strict, tri_incl = _tri(strict=True), _tri(strict=False)
    scale = 1.0 / math.sqrt(HEAD_DIM)

    def body(q_ref, k_ref, v_ref, o_ref, do_ref, a_ref, beta_ref, reach_ref, tris_ref, trii_ref, dq_ref, dk_ref, dv_ref, dk_acc, dv_acc):
        i = pl.program_id(1)

        @pl.when(i == 0)
        def _():
            dk_acc[...] = jnp.zeros_like(dk_acc)
            dv_acc[...] = jnp.zeros_like(dv_acc)

        q = q_ref[0]
        do = do_ref[...]
        total = jnp.sum(do.astype(F32) * o_ref[...], axis=1, keepdims=True)
        zero = jnp.zeros((ATTN_ROWS, 1), F32)
        blks0, bounds0 = _attn_sweep_start(i, t)

        def finish(starts, a, dz, dq):
            dzb = (dz * scale).astype(BF)
            for u in range(ATTN_UNITS):
                dv_acc[pl.ds(starts[u], ATTN_K), :] += _dot_tn(_unit_rows(a, u), _unit_rows(do, u))
                dk_acc[pl.ds(starts[u], ATTN_K), :] += _dot_tn(_unit_rows(dzb, u), _unit_rows(q, u))
            return dq + _per_unit(lambda u: _dot(_unit_rows(dzb, u), k_ref[0, pl.ds(starts[u], ATTN_K), :]))

        def grad_a(starts, a):
            return _per_unit(lambda u: _dot_nt(_unit_rows(do, u), v_ref[0, pl.ds(starts[u], ATTN_K), :])) * a.astype(F32)

        one_step = jnp.max(reach_ref[...]) <= ATTN_SKIP_BELOW

        @pl.when(one_step)
        def _():
            starts = _attn_keys(blks0)
            a = a_ref[0]
            beta = beta_ref[0].astype(F32)
            de = grad_a(starts, a)
            right, _ = _suffix_sums(de, trii_ref[...], zero)
            dz = de * (1.0 - beta) - (total - right) * beta
            dq_ref[...] = finish(starts, a, dz, jnp.zeros((ATTN_ROWS, HEAD_DIM), F32)).astype(BF)

        @pl.when(jnp.logical_not(one_step))
        def _():
            row = i * ATTN_ROWS + lax.broadcasted_iota(jnp.int32, (ATTN_ROWS, 1), 0)

            def step(carry):
                blks, bounds, dq, seen, run, _ = carry
                starts = _attn_keys(blks)
                z, mask, log_beta, log_rest = _attn_step(q, k_ref, starts, bounds, row)
                tail, run = _suffix_sums(log_rest, tris_ref[...], run)
                a = jnp.where(mask, jnp.exp(log_beta + tail), 0.0).astype(BF)
                de = grad_a(starts, a)
                right, seen = _suffix_sums(de, trii_ref[...], seen)
                beta = jax.nn.sigmoid(z)
                dz = jnp.where(mask, de * (1.0 - beta) - (total - right) * beta, 0.0)
                return (*_attn_next(blks), finish(starts, a, dz, dq), seen, run, jnp.max(run))

            first = (blks0, bounds0, jnp.zeros((ATTN_ROWS, HEAD_DIM), F32), zero, zero, jnp.float32(0.0))
            dq_ref[...] = lax.while_loop(_attn_more, step, step(first))[2].astype(BF)

        @pl.when(i == nq - 1)
        def _():
            dk_ref[...] = dk_acc[...].astype(BF)
            dv_ref[...] = dv_acc[...].astype(BF)

    qspec = pl.BlockSpec((1, ATTN_ROWS, HEAD_DIM), lambda h, i: (3, i, h))
    rowblk = pl.BlockSpec((ATTN_ROWS, HEAD_DIM), lambda h, i: (i, h))
    head = pl.BlockSpec((t, HEAD_DIM), lambda h, i: (0, h))
    trispec = pl.BlockSpec((ATTN_SUB, ATTN_SUB), lambda h, i: (0, 0))
    return _pallas(
        body, name="attn_bwd", grid=(N_HEADS, nq), ride=ride,
        in_specs=[qspec, _head_cols(4)(t), _head_cols(5)(t), rowblk, rowblk, _first_step_spec(), _first_step_spec(), _reach_spec(),
                  trispec, trispec],
        out_specs=[rowblk, head, head],
        out_shape=[S((t, D_MODEL), BF)] * 3,
        scratch_shapes=[pltpu.VMEM((t, HEAD_DIM), F32), pltpu.VMEM((t, HEAD_DIM), F32)],
        args=[proj, proj, proj, o, d_o, a_first, beta_first, reach, tri_strict, tri_incl])


def _tail(h3, n4, p, w_pg, w_pp, g_ple, g_final, target):
    t = h3.shape[0]
    tm = min(TOKEN_TILE, t)
    steps = t // tm

    def body(h_ref, n_ref, p_ref, wpg_ref, wpp_ref, gp_ref, gf_ref, tgt_ref,
             dh_ref, ds_ref, dpp_ref, loss_ref, dgf_ref, dgp_ref):
        pg = jax.nn.sigmoid(_dot(n_ref[...], wpg_ref[...]))
        pp = _dot(p_ref[...].astype(BF), wpp_ref[...])
        h3v = h_ref[...]
        h4 = h3v + pg * pp
        gf = gf_ref[...]
        diff = _rms(h4, gf) - tgt_ref[...]
        _accumulate(loss_ref, jnp.sum(diff * diff, axis=0, keepdims=True))
        dh4, dgf = _rms_bwd(diff * (1.0 / D_MODEL), h4, gf)
        _accumulate(dgf_ref, dgf)
        dpp_ref[...] = (dh4 * pg).astype(BF)
        ds = (dh4 * pp * pg * (1.0 - pg)).astype(BF)
        ds_ref[...] = ds
        dh3, dgp = _rms_bwd(_dot_nt(ds, wpg_ref[...]), h3v, gp_ref[...])
        _accumulate(dgp_ref, dgp)
        dh_ref[...] = dh4 + dh3

        @pl.when(pl.program_id(0) == steps - 1)
        def _():
            loss_ref[...] = jnp.full(loss_ref.shape, 0.5 / D_MODEL * jnp.sum(loss_ref[...]), F32)

    vec = (1, D_MODEL)
    return pl.pallas_call(
        body, name="tail", grid=(steps,),
        in_specs=[_rows(tm, D_MODEL), _rows(tm, D_MODEL), _rows(tm, PLE_DIM), _const_spec((D_MODEL, D_MODEL)),
                  _const_spec((PLE_DIM, D_MODEL)), _const_spec(vec), _const_spec(vec), _rows(tm, D_MODEL)],
        out_specs=[_rows(tm, D_MODEL)] * 3 + [_acc_spec(vec)] * 3,
        out_shape=[S((t, D_MODEL), F32), S((t, D_MODEL), BF), S((t, D_MODEL), BF)] + [S(vec, F32)] * 3,
    )(h3, n4, p, w_pg, w_pp, g_ple, g_final, target)


def _wgrad(xs, ys, name, ride=None):
    bx, t, k = xs.shape
    by, _, n = ys.shape
    b = max(bx, by)
    tt = min(WGRAD_TILE * 2 // xs.dtype.itemsize, t)
    steps = t // tt

    def body(x_ref, y_ref, o_ref, acc_ref):
        s = pl.program_id(1)

        @pl.when(s == 0)
        def _():
            acc_ref[...] = jnp.zeros_like(acc_ref)
        acc_ref[...] += _dot_tn(x_ref[0].astype(BF), y_ref[0].astype(BF))

        @pl.when(s == steps - 1)
        def _():
            o_ref[0] = acc_ref[...].astype(BF)

    (out,), landed = _pallas(
        body, name=name, grid=(b, steps), ride=ride,
        in_specs=[pl.BlockSpec((1, tt, k), (lambda j, s: (j, s, 0)) if bx > 1 else (lambda j, s: (0, s, 0))),
                  pl.BlockSpec((1, tt, n), (lambda j, s: (j, s, 0)) if by > 1 else (lambda j, s: (0, s, 0)))],
        out_specs=[pl.BlockSpec((1, k, n), lambda j, s: (j, 0, 0))],
        out_shape=[S((b, k, n), BF)],
        scratch_shapes=[pltpu.VMEM((k, n), F32)],
        args=[xs, ys])
    return (out, landed) if ride is not None else out


def _wgrad_pieces(x, ys, name, ride=None, tile=None, row_parts=1):
    t, k = x.shape
    n = ys[0].shape[2]
    counts = [y.shape[0] for y in ys]
    offsets = [sum(counts[:j]) for j in range(len(ys))]
    total = sum(counts)
    tt = min(tile or WGRAD_TILE, t)
    steps = t // tt
    kp = k // row_parts

    def body(x_ref, *refs):
        y_refs, o_refs, acc_ref = refs[:len(ys)], refs[len(ys):len(ys) + row_parts], refs[len(ys) + row_parts]
        p, s = pl.program_id(0), pl.program_id(1)

        @pl.when(s == 0)
        def _():
            acc_ref[...] = jnp.zeros_like(acc_ref)
        for j, y_ref in enumerate(y_refs):
            @pl.when(jnp.logical_and(p >= offsets[j], p < offsets[j] + counts[j]))
            def _(y_ref=y_ref):
                acc_ref[...] += _dot_tn(x_ref[...], y_ref[0])

        @pl.when(s == steps - 1)
        def _():
            for part, o_ref in enumerate(o_refs):
                o_ref[0] = acc_ref[part * kp:(part + 1) * kp, :].astype(BF)

    def turn(j):
        lo, hi = offsets[j], offsets[j] + counts[j]
        return lambda p, s: (jnp.clip(p - lo, 0, counts[j] - 1), jnp.where(p < lo, 0, jnp.where(p >= hi, steps - 1, s)), 0)

    outs, landed = _pallas(
        body, name=name, grid=(total, steps), ride=ride,
        in_specs=[pl.BlockSpec((tt, k), lambda p, s: (s, 0))] + [pl.BlockSpec((1, tt, n), turn(j)) for j in range(len(ys))],
        out_specs=[pl.BlockSpec((1, kp, n), lambda p, s: (p, 0, 0))] * row_parts,
        out_shape=[S((total, kp, n), BF)] * row_parts,
        scratch_shapes=[pltpu.VMEM((k, n), F32)],
        args=[x, *ys])
    out = outs[0] if row_parts == 1 else outs
    return (out, landed) if ride is not None else out


def _ffn_bwd_hidden(dh, to_gate, to_up, w_out, name, ride=None):
    t = dh.shape[0]
    tm = min(TOKEN_TILE, t)

    def body(dh_ref, to_gate_ref, to_up_ref, wout_ref, df_ref, dgate_ref, dup_ref):
        df = (0.5 * dh_ref[...]).astype(BF)
        df_ref[...] = df
        for c in range(N_FF_CHUNKS):
            dact = _dot_nt(df, wout_ref[c])
            dgate_ref[c] = (dact * to_gate_ref[c].astype(F32)).astype(BF)
            dup_ref[c] = (dact * to_up_ref[c].astype(F32)).astype(BF)

    return _pallas(
        body, name=name, grid=(t // tm,), ride=ride,
        in_specs=[_rows(tm, D_MODEL), _chunks(tm), _chunks(tm), _const_spec(w_out.shape)],
        out_specs=[_rows(tm, D_MODEL), _chunks(tm), _chunks(tm)],
        out_shape=[S((t, D_MODEL), BF)] + [S((N_FF_CHUNKS, t, FF_CHUNK), BF)] * 2,
        args=[dh, to_gate, to_up, w_out])


def _ffn_bwd_input(dh, h_in, g, dgate, dup, w_in, name, ride=None):
    t = dh.shape[0]
    tm = min(TOKEN_TILE, t)

    def body(dh_ref, h_ref, g_ref, dgate_ref, dup_ref, win_ref, dhi_ref, dg_ref):
        dn = jnp.zeros((tm, D_MODEL), F32)
        for c in range(N_FF_CHUNKS):
            dn = dn + _dot_nt(dgate_ref[c], win_ref[c]) + _dot_nt(dup_ref[c], win_ref[N_FF_CHUNKS + c])
        dhi, dg = _rms_bwd(dn, h_ref[...], g_ref[...])
        _accumulate(dg_ref, dg)
        dhi_ref[...] = dh_ref[...] + dhi

    vec = (1, D_MODEL)
    return _pallas(
        body, name=name, grid=(t // tm,), ride=ride,
        in_specs=[_rows(tm, D_MODEL), _rows(tm, D_MODEL), _const_spec(vec), _chunks(tm), _chunks(tm), _const_spec(w_in.shape)],
        out_specs=[_rows(tm, D_MODEL), _acc_spec(vec)],
        out_shape=[S((t, D_MODEL), F32), S(vec, F32)],
        args=[dh, h_in, g, dgate, dup, w_in])


def _mixer_bwd(dh2, proj, yc, ya, conv_w, w_co, w_ao, w_mo, ride=None):
    t = dh2.shape[0]
    tm = min(TOKEN_TILE, t)

    def body(dh_ref, cb_ref, cc_ref, cx_ref, gc_ref, ga_ref, cch_ref, cxh_ref, yc_ref, ya_ref, cw_ref, wco_ref, wao_ref, wmo_ref,
             dhb_ref, dyc_ref, dya_ref, dgc_ref, dga_ref, dcb_ref, dcv_ref, do_ref):
        dhb = dh_ref[...].astype(BF)
        dhb_ref[...] = dhb
        dmerged = _dot_nt(dhb, wmo_ref[...])
        sc = jax.nn.sigmoid(gc_ref[0].astype(F32))
        sa = jax.nn.sigmoid(ga_ref[0].astype(F32))
        dyc = (dmerged * sc).astype(BF)
        dya = (dmerged * sa).astype(BF)
        dyc_ref[...] = dyc
        dya_ref[...] = dya
        dgc_ref[...] = (dmerged * yc_ref[...].astype(F32) * sc * (1.0 - sc)).astype(BF)
        dga_ref[...] = (dmerged * ya_ref[...].astype(F32) * sa * (1.0 - sa)).astype(BF)
        m, m1, m2 = _conv_inputs(cc_ref, cx_ref, cch_ref, cxh_ref)
        cw = cw_ref[...]
        cv = cw[0:1, :] * m2 + cw[1:2, :] * m1 + cw[2:3, :] * m
        dycin = _dot_nt(dyc, wco_ref[...])
        dcb_ref[...] = (dycin * cv).astype(BF)
        dcv_ref[...] = dycin * cb_ref[0].astype(F32)
        do_ref[...] = _dot_nt(dya, wao_ref[...]).astype(BF)

    sq = (D_MODEL, D_MODEL)
    return _pallas(
        body, name="mixer_bwd", grid=(t // tm,), ride=ride,
        in_specs=[_rows(tm, D_MODEL), _piece(0, tm), _piece(1, tm), _piece(2, tm), _piece(6, tm), _piece(7, tm),
                  _prev_halo(1, tm), _prev_halo(2, tm), _rows(tm, D_MODEL), _rows(tm, D_MODEL),
                  _const_spec((3, D_MODEL)), _const_spec(sq), _const_spec(sq), _const_spec(sq)],
        out_specs=[_rows(tm, D_MODEL)] * 8,
        out_shape=[S((t, D_MODEL), BF)] * 6 + [S((t, D_MODEL), F32), S((t, D_MODEL), BF)],
        args=[dh2, proj, proj, proj, proj, proj, proj, proj, yc, ya, conv_w, w_co, w_ao, w_mo])


F32_HALO = 8


def _conv_bwd(dcv, proj, conv_w):
    t = dcv.shape[0]
    tm = min(TOKEN_TILE, t)
    steps = t // tm

    def body(dcv_ref, nxt_ref, cc_ref, cx_ref, cch_ref, cxh_ref, cw_ref, dcc_ref, dcx_ref, dw_ref):
        i = pl.program_id(0)
        m, m1, m2 = _conv_inputs(cc_ref, cx_ref, cch_ref, cxh_ref)
        d0 = dcv_ref[...]
        nxt = jnp.where(i == steps - 1, 0.0, nxt_ref[...])
        row = lax.broadcasted_iota(jnp.int32, (tm, 1), 0)
        d1 = jnp.where(row == tm - 1, nxt[0:1, :], pltpu.roll(d0, tm - 1, 0))
        d2 = pltpu.roll(d0, tm - 2, 0)
        d2 = jnp.where(row == tm - 2, nxt[0:1, :], jnp.where(row == tm - 1, nxt[1:2, :], d2))
        cw = cw_ref[...]
        dm = cw[2:3, :] * d0 + cw[1:2, :] * d1 + cw[0:1, :] * d2
        dcc_ref[...] = (dm * cx_ref[0].astype(F32)).astype(BF)
        dcx_ref[...] = (dm * cc_ref[0].astype(F32)).astype(BF)
        tap_row = lax.broadcasted_iota(jnp.int32, (F32_HALO, 1), 0)
        dw = jnp.zeros((F32_HALO, D_MODEL), F32)
        for j, mk in enumerate((m2, m1, m)):
            dw = jnp.where(tap_row == j, jnp.sum(d0 * mk, axis=0, keepdims=True), dw)
        _accumulate(dw_ref, dw)

    nxt_spec = pl.BlockSpec((F32_HALO, D_MODEL), lambda i: (jnp.minimum((i + 1) * (tm // F32_HALO), t // F32_HALO - 1), 0))
    return pl.pallas_call(
        body, name="conv_bwd", grid=(steps,),
        in_specs=[_rows(tm, D_MODEL), nxt_spec, _piece(1, tm), _piece(2, tm), _prev_halo(1, tm), _prev_halo(2, tm),
                  _const_spec((3, D_MODEL))],
        out_specs=[_rows(tm, D_MODEL), _rows(tm, D_MODEL), _acc_spec((F32_HALO, D_MODEL))],
        out_shape=[S((t, D_MODEL), BF), S((t, D_MODEL), BF), S((F32_HALO, D_MODEL), F32)],
    )(dcv, dcv, proj, proj, proj, proj, conv_w)


def _mix_bwd(dpieces, w_mix, h1, dh2, g, ride=None):
    t = h1.shape[0]
    tm = min(TOKEN_TILE, t)

    def body(*refs):
        pieces, (w_ref, h_ref, dh_ref, g_ref, dhi_ref, dg_ref) = refs[:N_MIX], refs[N_MIX:]
        du = jnp.zeros((tm, D_MODEL), F32)
        for d in range(N_MIX):
            du = du + _dot_nt(pieces[d][...], w_ref[d])
        dhi, dg = _rms_bwd(du, h_ref[...], g_ref[...])
        _accumulate(dg_ref, dg)
        dhi_ref[...] = dh_ref[...] + dhi

    vec = (1, D_MODEL)
    return _pallas(
        body, name="mix_bwd", grid=(t // tm,), ride=ride,
        in_specs=[_rows(tm, D_MODEL)] * N_MIX + [_const_spec(w_mix.shape), _rows(tm, D_MODEL), _rows(tm, D_MODEL), _const_spec(vec)],
        out_specs=[_rows(tm, D_MODEL), _acc_spec(vec)],
        out_shape=[S((t, D_MODEL), F32), S(vec, F32)],
        args=[*dpieces, w_mix, h1, dh2, g])


def _adamw(partials, w, m, v, name):
    parts = list(partials) if isinstance(partials, (list, tuple)) else [partials]
    r, c = w.shape
    tr = min(r, 512 // len(parts))
    first_tile = [sum(p.shape[1] for p in parts[:j]) // tr for j in range(len(parts))]
    c1 = 1.0 - ADAM_B1 ** ADAM_STEP
    c2 = 1.0 - ADAM_B2 ** ADAM_STEP

    def body(*refs):
        p_refs, (w_ref, m_ref, v_ref, g_ref, d_ref, mo_ref, vo_ref) = refs[:len(parts)], refs[len(parts):]
        g = None
        for j, p_ref in enumerate(p_refs):
            gj = p_ref[0].astype(F32)
            for s in range(1, N_SHARDS):
                gj = gj + p_ref[s].astype(F32)
            g = gj if g is None else jnp.where(pl.program_id(0) >= first_tile[j], gj, g)
        mn = ADAM_B1 * m_ref[...] + (1.0 - ADAM_B1) * g
        vn = ADAM_B2 * v_ref[...] + (1.0 - ADAM_B2) * (g * g)
        g_ref[...] = g
        mo_ref[...] = mn
        vo_ref[...] = vn
        d_ref[...] = -ADAM_LR * ((mn / c1) / (jnp.sqrt(vn / c2) + ADAM_EPS) + ADAM_WD * w_ref[...])

    def rows_of(j):
        last = parts[j].shape[1] // tr - 1
        return lambda i: (0, jnp.clip(i - first_tile[j], 0, last), 0)

    blk = pl.BlockSpec((tr, c), lambda i: (i, 0))
    return pl.pallas_call(
        body, name=name, grid=(r // tr,),
        in_specs=[pl.BlockSpec((N_SHARDS, tr, c), rows_of(j)) for j in range(len(parts))] + [blk, blk, blk],
        out_specs=[blk] * 4, out_shape=[S((r, c), F32)] * 4,
    )(*parts, w, m, v)


_MATRICES = ("ffn1_w_in", "ffn1_w_out", "w_mix_in", "conv_w", "w_conv_out", "w_attn_out", "w_mix_out",
             "ffn2_w_in", "ffn2_w_out", "w_ple_gate", "w_ple_proj")
_GAINS = ("ffn1_norm", "mix_norm", "ffn2_norm", "ple_norm", "final_norm")
_WEIGHTS = ("ffn1_norm", "ffn1_w_in", "ffn1_w_out", "mix_norm", "w_mix_in", "conv_w", "w_conv_out", "w_attn_out", "w_mix_out",
            "ffn2_norm", "ffn2_w_in", "ffn2_w_out", "ple_norm", "w_ple_gate", "w_ple_proj", "final_norm")
CONV_ROWS = 8


def _columns_from_shards(g):
    return jnp.transpose(g, (1, 0, 2)).reshape(g.shape[1], N_SHARDS * g.shape[2])


def _shards_from_columns(a):
    r, c = a.shape
    return jnp.transpose(a.reshape(r, N_SHARDS, c // N_SHARDS), (1, 0, 2))


def kernel(x, p, ffn1_norm, ffn1_w_in, ffn1_w_out, mix_norm, w_mix_in, conv_w, w_conv_out, w_attn_out, w_mix_out, ffn2_norm, ffn2_w_in, ffn2_w_out, ple_norm, w_ple_gate, w_ple_proj, final_norm, loss_target, m_ffn1_norm, m_ffn1_w_in, m_ffn1_w_out, m_mix_norm, m_w_mix_in, m_conv_w, m_w_conv_out, m_w_attn_out, m_w_mix_out, m_ffn2_norm, m_ffn2_w_in, m_ffn2_w_out, m_ple_norm, m_w_ple_gate, m_w_ple_proj, m_final_norm, v_ffn1_norm, v_ffn1_w_in, v_ffn1_w_out, v_mix_norm, v_w_mix_in, v_conv_w, v_w_conv_out, v_w_attn_out, v_w_mix_out, v_ffn2_norm, v_ffn2_w_in, v_ffn2_w_out, v_ple_norm, v_w_ple_gate, v_w_ple_proj, v_final_norm):
    given = dict(locals())
    t = x.shape[1]
    xs = x.reshape(t, D_MODEL)
    ps = p.reshape(t, PLE_DIM)
    target = loss_target.reshape(t, D_MODEL)
    shard = {k: given[k].reshape(given[k].shape[-2:]) for k in _MATRICES}
    gain = {k: given[k].reshape(1, D_MODEL) for k in _GAINS}

    send = {k: shard[k].astype(BF) for k in _MATRICES}
    send["conv_w"] = jnp.pad(shard["conv_w"], ((0, CONV_ROWS - 3), (0, 0)))
    loss_vec, dx, landed, gain_grads = _forward_backward(xs, ps, target, gain, send)
    gain_rows = jnp.concatenate([gain_grads[k] for k in _GAINS] + [jnp.zeros((8 - len(_GAINS), D_MODEL), F32)], axis=0)
    gain_parts, = _exchange_alone("gather", [gain_rows], "gather_gain_gradients")

    out = {}
    for k in _MATRICES:
        w, m, v = shard[k], given["m_" + k].reshape(shard[k].shape), given["v_" + k].reshape(shard[k].shape)
        part = landed[k]
        if k == "conv_w":
            pad = ((0, CONV_ROWS - 3), (0, 0))
            w, m, v = jnp.pad(w, pad), jnp.pad(m, pad), jnp.pad(v, pad, constant_values=1.0)
        res = _adamw(part, w, m, v, "adamw_" + k)
        out[k] = [r[:3] if k == "conv_w" else r for r in res]
    stack = lambda pre: jnp.concatenate([given[pre + k].reshape(1, D_MODEL) for k in _GAINS] + [jnp.ones((8 - len(_GAINS), D_MODEL), F32)], axis=0)
    res = _adamw(gain_parts, stack(""), stack("m_"), stack("v_"), "adamw_gains")
    for j, k in enumerate(_GAINS):
        out[k] = [r[j:j + 1] for r in res]

    loss = lax.psum(loss_vec[0, 0], ("x", "y", "c"))
    per_kind = [[out[k][j].reshape(given[k].shape) for k in _WEIGHTS] for j in range(4)]
    return (loss, dx.reshape(x.shape), *per_kind[0], *per_kind[1], *per_kind[2], *per_kind[3])


def _forward_backward(xs, ps, target, gain, send, full=None):
    exchange = full is None
    full = dict(full or {})
    grads, landed = {}, {}

    def gather(names):
        return ("gather", [send[k] for k in names]) if exchange else None

    def scatter(names):
        return ("scatter", [grads[k] for k in names]) if exchange else None

    def keep(into, names, got):
        into.update(zip(names, got))

    first = ("ffn1_w_in", "ffn1_w_out")
    (n1,), got = _prenorm(xs, gain["ffn1_norm"], ride=gather(first))
    keep(full, first, got)
    w1_in, w1_out = full["ffn1_w_in"], full["ffn1_w_out"].reshape(N_FF_CHUNKS, FF_CHUNK, D_MODEL)
    mixer = ("w_mix_in", "conv_w", "w_conv_out", "w_attn_out", "w_mix_out")
    (h1, u, act1, to_gate1, to_up1), got = _ffn_fwd(xs, n1, w1_in, w1_out, gain["mix_norm"], "ffn1_fwd", ride=gather(mixer))
    keep(full, mixer, got)
    w_mix = full["w_mix_in"]
    w_co, w_ao, w_mo = (full[k].reshape(D_MODEL, D_MODEL) for k in ("w_conv_out", "w_attn_out", "w_mix_out"))
    taps = _columns_from_shards(full["conv_w"][:, :3, :])
    rest = ("ffn2_w_in", "ffn2_w_out", "w_ple_gate", "w_ple_proj")
    (proj,), got = _mix_proj(u, w_mix, ride=gather(rest))
    keep(full, rest, got)
    w2_in, w2_out = full["ffn2_w_in"], full["ffn2_w_out"].reshape(N_FF_CHUNKS, FF_CHUNK, D_MODEL)
    w_pg = full["w_ple_gate"].reshape(D_MODEL, D_MODEL)
    w_pp = _columns_from_shards(full["w_ple_proj"])
    o, a_first, beta_first, reach = _attn_fwd(proj)
    h2, n3, ycin, yc, ya, merged = _mixer_out(proj, o, h1, taps, w_co, w_ao, w_mo, gain["ffn2_norm"])
    (h3, n4, act2, to_gate2, to_up2), _ = _ffn_fwd(h2, n3, w2_in, w2_out, gain["ple_norm"], "ffn2_fwd")
    dh3, ds, dpp, loss_vec, dg_final, dg_ple = _tail(h3, n4, ps, w_pg, w_pp, gain["ple_norm"], gain["final_norm"], target)

    one = lambda a: a[None]
    by_rows = lambda g, rows: g.reshape(N_SHARDS, rows // N_SHARDS, D_MODEL)
    grads["w_ple_gate"] = by_rows(_wgrad(one(n4), one(ds), "wgrad_ple_gate"), D_MODEL)
    grads["w_ple_proj"] = _shards_from_columns(_wgrad(one(ps), one(dpp), "wgrad_ple_proj")[0])
    ple = ("w_ple_gate", "w_ple_proj")
    (df2, dgate2, dup2), got = _ffn_bwd_hidden(dh3, to_gate2, to_up2, w2_out, "ffn2_bwd_hidden", ride=scatter(ple))
    keep(landed, ple, got)
    grads["ffn2_w_out"] = by_rows(_wgrad(act2, one(df2), "wgrad_ffn2_out"), D_FF)
    grads["ffn2_w_in"] = _wgrad_pieces(n3, [dgate2, dup2], "wgrad_ffn2_in")
    (dh2, dg_ffn2), got = _ffn_bwd_input(dh3, h2, gain["ffn2_norm"], dgate2, dup2, w2_in, "ffn2_bwd_input", ride=scatter(("ffn2_w_out",)))
    keep(landed, ("ffn2_w_out",), got)
    (dh2b, dyc, dya, dgc, dga, dcb, dcv, d_o), _ = _mixer_bwd(dh2, proj, yc, ya, taps, w_co, w_ao, w_mo)
    grads["w_mix_out"] = by_rows(_wgrad(one(merged), one(dh2b), "wgrad_mix_out"), D_MODEL)
    grads["w_conv_out"] = by_rows(_wgrad(one(ycin), one(dyc), "wgrad_conv_out"), D_MODEL)
    grads["w_attn_out"] = by_rows(_wgrad(one(o), one(dya), "wgrad_attn_out"), D_MODEL)
    dcc, dcx, dtaps = _conv_bwd(dcv, proj, taps)
    grads["conv_w"] = jnp.pad(_shards_from_columns(dtaps[:3]), ((0, 0), (0, CONV_ROWS - 3), (0, 0)))
    behind_attn = ("ffn2_w_in", "w_mix_out", "w_conv_out", "w_attn_out", "conv_w")
    (dq, dk, dv), got = _attn_bwd(proj, o, d_o, a_first, beta_first, reach, ride=scatter(behind_attn))
    keep(landed, behind_attn, got)
    dpieces = [dcb, dcc, dcx, dq, dk, dv, dgc, dga]
    half = N_MIX // 2
    tops, bottoms = zip(_wgrad_pieces(u, [one(dp) for dp in dpieces[:half]], "wgrad_mix_in_a", tile=WGRAD_TILE // 2, row_parts=2),
                        _wgrad_pieces(u, [one(dp) for dp in dpieces[half:]], "wgrad_mix_in_b", tile=WGRAD_TILE // 2, row_parts=2))
    grads["w_mix_in top"], grads["w_mix_in bottom"] = jnp.concatenate(tops, axis=0), jnp.concatenate(bottoms, axis=0)
    (dh1, dg_mix), top = _mix_bwd(dpieces, w_mix, h1, dh2, gain["mix_norm"], ride=scatter(("w_mix_in top",)))
    (df1, dgate1, dup1), bottom = _ffn_bwd_hidden(dh1, to_gate1, to_up1, w1_out, "ffn1_bwd_hidden", ride=scatter(("w_mix_in bottom",)))
    if exchange:
        landed["w_mix_in"] = [top[0], bottom[0]]
    else:
        grads["w_mix_in"] = jnp.concatenate([grads.pop("w_mix_in top"), grads.pop("w_mix_in bottom")], axis=1)
    grads["ffn1_w_out"] = by_rows(_wgrad(act1, one(df1), "wgrad_ffn1_out"), D_FF)
    if exchange:
        grads["ffn1_w_in"], got = _wgrad_pieces(n1, [dgate1, dup1], "wgrad_ffn1_in", ride=scatter(("ffn1_w_out",)))
        keep(landed, ("ffn1_w_out",), got)
    else:
        grads["ffn1_w_in"] = _wgrad_pieces(n1, [dgate1, dup1], "wgrad_ffn1_in")
    (dx, dg_ffn1), got = _ffn_bwd_input(dh1, xs, gain["ffn1_norm"], dgate1, dup1, w1_in, "ffn1_bwd_input", ride=scatter(("ffn1_w_in",)))
    keep(landed, ("ffn1_w_in",), got)
    gain_grads = dict(ffn1_norm=dg_ffn1, mix_norm=dg_mix, ffn2_norm=dg_ffn2, ple_norm=dg_ple, final_norm=dg_final)
    return loss_vec, dx, (landed if exchange else grads), gain_grads
```

```python
import functools
import math

import jax
import jax.numpy as jnp
from jax import lax
from jax.experimental import pallas as pl
from jax.experimental.pallas import tpu as pltpu

D_MODEL = 1024
D_FF = 2816
N_SHARDS = 8
FF_CHUNK = 2 * D_FF // N_SHARDS
N_FF_CHUNKS = D_FF // FF_CHUNK
N_HEADS = 8
HEAD_DIM = 128
PLE_DIM = 256
NORM_EPS = 1e-6
N_MIX = 8
ADAM_LR, ADAM_B1, ADAM_B2, ADAM_EPS, ADAM_WD, ADAM_STEP = 0.001, 0.9, 0.999, 1e-08, 0.01, 10

TOKEN_TILE = 512
WGRAD_TILE = 4096
PROJ_TILE = 2048
ATTN_ROWS = 512
ATTN_Q = 128
ATTN_SUB = 128
ATTN_K = 3 * ATTN_SUB
ATTN_SKIP_BELOW = -90.0

BF = jnp.bfloat16
F32 = jnp.float32
MESH = pl.DeviceIdType.MESH
NT = (((1,), (1,)), ((), ()))
TN = (((0,), (0,)), ((), ()))
S = jax.ShapeDtypeStruct
ANY = pl.BlockSpec(memory_space=pl.ANY)


def _const_spec(shape):
    nd = len(shape)
    return pl.BlockSpec(shape, lambda *_: (0,) * nd, pipeline_mode=pl.Buffered(1))


def _rows(tm, cols):
    return pl.BlockSpec((tm, cols), lambda i: (i, 0))


def _chunks(tm):
    return pl.BlockSpec((N_FF_CHUNKS, tm, FF_CHUNK), lambda i: (0, i, 0))


def _acc_spec(shape):
    nd = len(shape)
    return pl.BlockSpec(shape, lambda *_: (0,) * nd)


def _dot(a, b):
    return jnp.dot(a, b, preferred_element_type=F32)


def _dot_nt(a, b):
    return lax.dot_general(a, b, NT, preferred_element_type=F32)


def _dot_tn(a, b):
    return lax.dot_general(a, b, TN, preferred_element_type=F32)


def _rms(h, g):
    r = lax.rsqrt(jnp.mean(h * h, axis=-1, keepdims=True) + NORM_EPS)
    return h * r * g


def _rms_bwd(dn, h, g):
    r = lax.rsqrt(jnp.mean(h * h, axis=-1, keepdims=True) + NORM_EPS)
    nh = h * r
    gd = dn * g
    dh = r * (gd - nh * jnp.mean(gd * nh, axis=-1, keepdims=True))
    return dh, jnp.sum(dn * nh, axis=0, keepdims=True)


def _accumulate(ref, val):
    @pl.when(pl.program_id(0) == 0)
    def _():
        ref[...] = jnp.zeros_like(ref)
    ref[...] += val


def _place():
    x, y, c = lax.axis_index("x"), lax.axis_index("y"), lax.axis_index("c")
    return x, y, c


def _slot(px, py, pc):
    return 4 * px + 2 * py + pc


def _gather_phases(ins, outs, send_sems, recv_sems, local_sems):
    n = len(ins)

    def parties():
        x, y, c = _place()
        return (x, y, c), (x, y, 1 - c), [(1 - x, y), (x, 1 - y), (1 - x, 1 - y)], c

    def copy(a, k, block, to, src=None):
        dst = outs[a].at[_slot(*block)]
        return pltpu.make_async_remote_copy(
            src_ref=dst if src is None else src, dst_ref=dst,
            send_sem=send_sems.at[a, k], recv_sem=recv_sems.at[a, k],
            device_id=to, device_id_type=MESH)

    def own(a, me):
        return pltpu.make_async_copy(ins[a], outs[a].at[_slot(*me)], local_sems.at[a])

    def first(a, me, sibling, chips, c):
        return [copy(a, 0, me, sibling, src=ins[a])] + [copy(a, 1 + j, me, (*chip, c), src=ins[a]) for j, chip in enumerate(chips)]

    def start():
        me, sibling, chips, c = parties()
        for a in range(n):
            own(a, me).start()
        for a in range(n):
            for cp in first(a, me, sibling, chips, c):
                cp.start()

    def forward():
        me, sibling, chips, c = parties()
        for j, chip in enumerate(chips):
            for a in range(n):
                copy(a, 1 + j, (*chip, c), me).wait_recv()
                copy(a, 4 + j, (*chip, c), sibling).start()

    def finish():
        me, sibling, chips, c = parties()
        for a in range(n):
            copy(a, 0, sibling, me).wait_recv()
            for j, chip in enumerate(chips):
                copy(a, 4 + j, (*chip, 1 - c), me).wait_recv()
        for a in range(n):
            for cp in first(a, me, sibling, chips, c) + [copy(a, 4 + j, (*chip, c), sibling) for j, chip in enumerate(chips)]:
                cp.wait_send()
            own(a, me).wait()

    return [start, forward, finish]


def _scatter_phases(ins, outs, send_sems, recv_sems, local_sems):
    n = len(ins)

    def copies():
        x, y, c = _place()
        me = _slot(x, y, c)
        out = [pltpu.make_async_copy(ins[a].at[me], outs[a].at[me], local_sems.at[a]) for a in range(n)]
        for k in range(1, N_SHARDS):
            px = 1 - x if k & 4 else x
            py = 1 - y if k & 2 else y
            pc = 1 - c if k & 1 else c
            for a in range(n):
                out.append(pltpu.make_async_remote_copy(
                    src_ref=ins[a].at[_slot(px, py, pc)], dst_ref=outs[a].at[me],
                    send_sem=send_sems.at[a, k - 1], recv_sem=recv_sems.at[a, k - 1],
                    device_id=(px, py, pc), device_id_type=MESH))
        return out

    def start():
        for cp in copies():
            cp.start()

    def finish():
        for cp in copies():
            cp.wait()

    return [start, finish]


def _pallas(body, *, name, grid, in_specs, out_specs, out_shape, args, scratch_shapes=(), ride=None):
    if ride is None:
        outs = pl.pallas_call(body, name=name, grid=grid, in_specs=in_specs, out_specs=out_specs, out_shape=out_shape,
                              scratch_shapes=list(scratch_shapes))(*args)
        return list(outs), []
    kind, arrays = ride
    n, n_in, n_out, n_scr = len(arrays), len(in_specs), len(out_specs), len(scratch_shapes)
    total = math.prod(grid)
    middle = (2 * total) // 3
    landed_shape = [S((N_SHARDS,) + a.shape if kind == "gather" else a.shape, a.dtype) for a in arrays]

    def with_exchange(*refs):
        ins, riders_in = refs[:n_in], refs[n_in:n_in + n]
        outs, riders_out = refs[n_in + n:n_in + n + n_out], refs[n_in + n + n_out:n_in + 2 * n + n_out]
        scratch, sems = refs[n_in + 2 * n + n_out:n_in + 2 * n + n_out + n_scr], refs[n_in + 2 * n + n_out + n_scr:]
        step = 0
        for axis, size in enumerate(grid):
            step = step * size + pl.program_id(axis)
        phases = (_gather_phases if kind == "gather" else _scatter_phases)(riders_in, riders_out, *sems)
        pl.when(step == 0)(phases[0])
        body(*ins, *outs, *scratch)
        for phase in phases[1:-1]:
            pl.when(step == middle)(phase)
        pl.when(step == total - 1)(phases[-1])

    outs = pl.pallas_call(
        with_exchange, name=name, grid=grid,
        in_specs=list(in_specs) + [ANY] * n, out_specs=list(out_specs) + [ANY] * n,
        out_shape=list(out_shape) + landed_shape,
        scratch_shapes=list(scratch_shapes) + [pltpu.SemaphoreType.DMA((n, 7)), pltpu.SemaphoreType.DMA((n, 7)),
                                               pltpu.SemaphoreType.DMA((n,))],
    )(*args, *arrays)
    return list(outs[:n_out]), list(outs[n_out:])


def _exchange_alone(kind, arrays, name):
    return _pallas(lambda: None, name=name, grid=(1,), in_specs=[], out_specs=[], out_shape=[], args=[], ride=(kind, arrays))[1]


def _prenorm(x, g, ride=None):
    t = x.shape[0]
    tm = min(TOKEN_TILE, t)

    def body(x_ref, g_ref, n_ref):
        n_ref[...] = _rms(x_ref[...], g_ref[...]).astype(BF)

    return _pallas(
        body, name="prenorm", grid=(t // tm,), ride=ride,
        in_specs=[_rows(tm, D_MODEL), _const_spec((1, D_MODEL))], out_specs=[_rows(tm, D_MODEL)],
        out_shape=[S((t, D_MODEL), BF)], args=[x, g])


def _ffn_up(n, w_in, name, ride=None):
    t = n.shape[0]
    tm = min(TOKEN_TILE, t)

    def body(n_ref, win_ref, act_ref, to_gate_ref, to_up_ref):
        nb = n_ref[...]
        for c in range(N_FF_CHUNKS):
            gate = _dot(nb, win_ref[c])
            up = _dot(nb, win_ref[N_FF_CHUNKS + c])
            sg = jax.nn.sigmoid(gate)
            silu = gate * sg
            act_ref[c] = (silu * up).astype(BF)
            to_gate_ref[c] = (up * (sg * (1.0 + gate * (1.0 - sg)))).astype(BF)
            to_up_ref[c] = silu.astype(BF)

    return _pallas(
        body, name=name, grid=(t // tm,), ride=ride,
        in_specs=[_rows(tm, D_MODEL), _const_spec(w_in.shape)],
        out_specs=[_chunks(tm)] * 3, out_shape=[S((N_FF_CHUNKS, t, FF_CHUNK), BF)] * 3,
        args=[n, w_in])


def _ffn_down(h, act, w_out, g_next, name, ride=None):
    t = h.shape[0]
    tm = min(TOKEN_TILE, t)

    def body(h_ref, act_ref, wout_ref, g_ref, ho_ref, no_ref):
        acc = jnp.zeros((tm, D_MODEL), F32)
        for c in range(N_FF_CHUNKS):
            acc = acc + _dot(act_ref[c], wout_ref[c])
        ho = h_ref[...] + 0.5 * acc
        ho_ref[...] = ho
        no_ref[...] = _rms(ho, g_ref[...]).astype(BF)

    return _pallas(
        body, name=name, grid=(t // tm,), ride=ride,
        in_specs=[_rows(tm, D_MODEL), _chunks(tm), _const_spec(w_out.shape), _const_spec((1, D_MODEL))],
        out_specs=[_rows(tm, D_MODEL)] * 2, out_shape=[S((t, D_MODEL), F32), S((t, D_MODEL), BF)],
        args=[h, act, w_out, g_next])


def _mix_proj(u, w_mix, ride=None):
    t = u.shape[0]
    tm = min(PROJ_TILE, t)

    def body(u_ref, w_ref, o_ref):
        o_ref[0] = _dot(u_ref[...], w_ref[0]).astype(BF)

    return _pallas(
        body, name="mix_proj", grid=(N_MIX, t // tm), ride=ride,
        in_specs=[pl.BlockSpec((tm, D_MODEL), lambda d, i: (i, 0)), pl.BlockSpec((1, D_MODEL, D_MODEL), lambda d, i: (d, 0, 0))],
        out_specs=[pl.BlockSpec((1, tm, D_MODEL), lambda d, i: (d, i, 0))],
        out_shape=[S((N_MIX, t, D_MODEL), BF)], args=[u, w_mix])


HALO = 16


def _piece(d, tm):
    return pl.BlockSpec((1, tm, D_MODEL), lambda i: (d, i, 0))


def _prev_halo(d, tm):
    return pl.BlockSpec((1, HALO, D_MODEL), lambda i: (d, jnp.maximum(i * (tm // HALO) - 1, 0), 0))


def _shift_down(m, prev_tail, k):
    tm = m.shape[0]
    out = pltpu.roll(m, k, 0)
    row = lax.broadcasted_iota(jnp.int32, (tm, 1), 0)
    for j in range(k):
        out = jnp.where(row == j, prev_tail[HALO - k + j:HALO - k + j + 1, :], out)
    return out


def _conv_inputs(cc_ref, cx_ref, cch_ref, cxh_ref):
    m = cc_ref[0].astype(F32) * cx_ref[0].astype(F32)
    mh = cch_ref[0].astype(F32) * cxh_ref[0].astype(F32)
    mh = jnp.where(pl.program_id(0) == 0, 0.0, mh)
    return m, _shift_down(m, mh, 1), _shift_down(m, mh, 2)


def _mixer_out(proj, o, h1, conv_w, w_co, w_ao, w_mo, g_next):
    t = h1.shape[0]
    tm = min(TOKEN_TILE, t)

    def body(cb_ref, cc_ref, cx_ref, gc_ref, ga_ref, cch_ref, cxh_ref, o_ref, h_ref, cw_ref, wco_ref, wao_ref, wmo_ref,
             g_ref, ho_ref, no_ref, ycin_ref, yc_ref, ya_ref, mg_ref):
        m, m1, m2 = _conv_inputs(cc_ref, cx_ref, cch_ref, cxh_ref)
        cw = cw_ref[...]
        cv = cw[0:1, :] * m2 + cw[1:2, :] * m1 + cw[2:3, :] * m
        ycin = (cb_ref[0].astype(F32) * cv).astype(BF)
        ycin_ref[...] = ycin
        yc = _dot(ycin, wco_ref[...])
        ya = _dot(o_ref[...].astype(BF), wao_ref[...])
        yc_ref[...] = yc.astype(BF)
        ya_ref[...] = ya.astype(BF)
        merged = (jax.nn.sigmoid(gc_ref[0].astype(F32)) * yc + jax.nn.sigmoid(ga_ref[0].astype(F32)) * ya).astype(BF)
        mg_ref[...] = merged
        ho = h_ref[...] + _dot(merged, wmo_ref[...])
        ho_ref[...] = ho
        no_ref[...] = _rms(ho, g_ref[...]).astype(BF)

    sq = (D_MODEL, D_MODEL)
    return pl.pallas_call(
        body, name="mixer_out", grid=(t // tm,),
        in_specs=[_piece(0, tm), _piece(1, tm), _piece(2, tm), _piece(6, tm), _piece(7, tm), _prev_halo(1, tm), _prev_halo(2, tm),
                  _rows(tm, D_MODEL), _rows(tm, D_MODEL), _const_spec((3, D_MODEL)), _const_spec(sq), _const_spec(sq),
                  _const_spec(sq), _const_spec((1, D_MODEL))],
        out_specs=[_rows(tm, D_MODEL)] * 6,
        out_shape=[S((t, D_MODEL), F32)] + [S((t, D_MODEL), BF)] * 5,
    )(proj, proj, proj, proj, proj, proj, proj, o, h1, conv_w, w_co, w_ao, w_mo, g_next)


def _suffix_sums(vals, tri, before):
    out, right = [], before
    for b in reversed(range(ATTN_K // ATTN_SUB)):
        v = vals[:, b * ATTN_SUB:(b + 1) * ATTN_SUB]
        hi = v.astype(BF)
        lo = (v - hi.astype(F32)).astype(BF)
        out.append(_dot(hi, tri) + _dot(lo, tri) + right)
        right = right + jnp.sum(v, axis=1, keepdims=True)
    return jnp.concatenate(out[::-1], axis=1), right


ATTN_UNITS = ATTN_ROWS // ATTN_Q


def _unit_rows(x, u):
    return x[u * ATTN_Q:(u + 1) * ATTN_Q]


def _per_unit(fn):
    return jnp.concatenate([fn(u) for u in range(ATTN_UNITS)], axis=0)


def _per_row(vals):
    local = lax.broadcasted_iota(jnp.int32, (ATTN_ROWS, 1), 0)
    out = jnp.full((ATTN_ROWS, 1), vals[0], jnp.int32)
    for u in range(1, ATTN_UNITS):
        out = jnp.where(local >= u * ATTN_Q, vals[u], out)
    return out


def _attn_step(q, k_ref, starts, bounds, row):
    z = _per_unit(lambda u: _dot_nt(_unit_rows(q, u), k_ref[0, pl.ds(starts[u], ATTN_K), :])) * (1.0 / math.sqrt(HEAD_DIM))
    mask = lax.broadcasted_iota(jnp.int32, (1, ATTN_K), 1) < jnp.minimum(row, _per_row(bounds)) - _per_row(starts)
    log_beta = jnp.minimum(z, 0.0) - jnp.log(1.0 + jnp.exp(jnp.minimum(z, -z)))
    log_rest = jnp.where(mask, log_beta - z, 0.0)
    return z, mask, log_beta, log_rest


def _attn_sweep_start(i, t):
    blks = tuple(jnp.maximum(i * ATTN_UNITS + u + 1 - ATTN_K // ATTN_SUB, 0) for u in range(ATTN_UNITS))
    return blks, tuple(jnp.int32(t) for _ in range(ATTN_UNITS))


def _attn_keys(blks):
    return [pl.multiple_of(b * ATTN_SUB, ATTN_SUB) for b in blks]


def _attn_next(blks):
    return tuple(jnp.maximum(b - ATTN_K // ATTN_SUB, 0) for b in blks), tuple(b * ATTN_SUB for b in blks)


def _attn_more(carry):
    return jnp.logical_and(carry[1][ATTN_UNITS - 1] > 0, carry[-1] > ATTN_SKIP_BELOW)


def _tri(strict):
    r = lax.broadcasted_iota(jnp.int32, (ATTN_SUB, ATTN_SUB), 0)
    c = lax.broadcasted_iota(jnp.int32, (ATTN_SUB, ATTN_SUB), 1)
    return (r > c if strict else r >= c).astype(BF)


REACH_TILE = (8, 128)


def _first_step_spec():
    return pl.BlockSpec((1, ATTN_ROWS, ATTN_K), lambda h, i: (h, i, 0))


def _reach_spec():
    return pl.BlockSpec((1, 1) + REACH_TILE, lambda h, i: (h, i, 0, 0))


def _head_cols(piece):
    return lambda t: pl.BlockSpec((1, t, HEAD_DIM), lambda h, i: (piece, 0, h))


def _attn_fwd(proj):
    t = proj.shape[1]
    nq = t // ATTN_ROWS
    tri = _tri(strict=True)

    def body(q_ref, k_ref, v_ref, tri_ref, o_ref, a_ref, beta_ref, reach_ref):
        i = pl.program_id(1)
        q = q_ref[0]
        row = i * ATTN_ROWS + lax.broadcasted_iota(jnp.int32, (ATTN_ROWS, 1), 0)

        def step(carry, keep=False):
            blks, bounds, acc, run, _ = carry
            starts = _attn_keys(blks)
            _, mask, log_beta, log_rest = _attn_step(q, k_ref, starts, bounds, row)
            tail, run = _suffix_sums(log_rest, tri_ref[...], run)
            a = jnp.where(mask, jnp.exp(log_beta + tail), 0.0).astype(BF)
            if keep:
                a_ref[0] = a
                beta_ref[0] = jnp.where(mask, jnp.exp(log_beta), 0.0).astype(BF)
            acc = acc + _per_unit(lambda u: _dot(_unit_rows(a, u), v_ref[0, pl.ds(starts[u], ATTN_K), :]))
            return (*_attn_next(blks), acc, run, jnp.max(run))

        first = (*_attn_sweep_start(i, t), jnp.zeros((ATTN_ROWS, HEAD_DIM), F32), jnp.zeros((ATTN_ROWS, 1), F32), jnp.float32(0.0))
        after_first = step(first, keep=True)
        reach_ref[...] = jnp.full(reach_ref.shape, after_first[-1], F32)
        o_ref[...] = lax.while_loop(_attn_more, step, after_first)[2]

    qspec = pl.BlockSpec((1, ATTN_ROWS, HEAD_DIM), lambda h, i: (3, i, h))
    return pl.pallas_call(
        body, name="attn_fwd", grid=(N_HEADS, nq),
        in_specs=[qspec, _head_cols(4)(t), _head_cols(5)(t), pl.BlockSpec((ATTN_SUB, ATTN_SUB), lambda h, i: (0, 0))],
        out_specs=[pl.BlockSpec((ATTN_ROWS, HEAD_DIM), lambda h, i: (i, h)), _first_step_spec(), _first_step_spec(), _reach_spec()],
        out_shape=[S((t, D_MODEL), F32), S((N_HEADS, t, ATTN_K), BF), S((N_HEADS, t, ATTN_K), BF), S((N_HEADS, nq) + REACH_TILE, F32)],
    )(proj, proj, proj, tri)


def _attn_bwd(proj, o, d_o, a_first, beta_first, reach, ride=None):
    t = proj.shape[1]
    nq = t // ATTN_ROWS
    tri_strict, tri_incl = _tri(strict=True), _tri(strict=False)
    scale = 1.0 / math.sqrt(HEAD_DIM)

    def body(q_ref, k_ref, v_ref, o_ref, do_ref, a_ref, beta_ref, reach_ref, tris_ref, trii_ref, dq_ref, dk_ref, dv_ref, dk_acc, dv_acc):
        i = pl.program_id(1)

        @pl.when(i == 0)
        def _():
            dk_acc[...] = jnp.zeros_like(dk_acc)
            dv_acc[...] = jnp.zeros_like(dv_acc)

        q = q_ref[0]
        do = do_ref[...]
        total = jnp.sum(do.astype(F32) * o_ref[...], axis=1, keepdims=True)
        zero = jnp.zeros((ATTN_ROWS, 1), F32)
        blks0, bounds0 = _attn_sweep_start(i, t)

        def finish(starts, a, dz, dq):
            dzb = (dz * scale).astype(BF)
            for u in range(ATTN_UNITS):
                dv_acc[pl.ds(starts[u], ATTN_K), :] += _dot_tn(_unit_rows(a, u), _unit_rows(do, u))
                dk_acc[pl.ds(starts[u], ATTN_K), :] += _dot_tn(_unit_rows(dzb, u), _unit_rows(q, u))
            return dq + _per_unit(lambda u: _dot(_unit_rows(dzb, u), k_ref[0, pl.ds(starts[u], ATTN_K), :]))

        def grad_a(starts, a):
            return _per_unit(lambda u: _dot_nt(_unit_rows(do, u), v_ref[0, pl.ds(starts[u], ATTN_K), :])) * a.astype(F32)

        one_step = jnp.max(reach_ref[...]) <= ATTN_SKIP_BELOW

        @pl.when(one_step)
        def _():
            starts = _attn_keys(blks0)
            a = a_ref[0]
            beta = beta_ref[0].astype(F32)
            de = grad_a(starts, a)
            right, _ = _suffix_sums(de, trii_ref[...], zero)
            dz = de * (1.0 - beta) - (total - right) * beta
            dq_ref[...] = finish(starts, a, dz, jnp.zeros((ATTN_ROWS, HEAD_DIM), F32)).astype(BF)

        @pl.when(jnp.logical_not(one_step))
        def _():
            row = i * ATTN_ROWS + lax.broadcasted_iota(jnp.int32, (ATTN_ROWS, 1), 0)

            def step(carry):
                blks, bounds, dq, seen, run, _ = carry
                starts = _attn_keys(blks)
                z, mask, log_beta, log_rest = _attn_step(q, k_ref, starts, bounds, row)
                tail, run = _suffix_sums(log_rest, tris_ref[...], run)
                a = jnp.where(mask, jnp.exp(log_beta + tail), 0.0).astype(BF)
                de = grad_a(starts, a)
                right, seen = _suffix_sums(de, trii_ref[...], seen)
                beta = jax.nn.sigmoid(z)
                dz = jnp.where(mask, de * (1.0 - beta) - (total - right) * beta, 0.0)
                return (*_attn_next(blks), finish(starts, a, dz, dq), seen, run, jnp.max(run))

            first = (blks0, bounds0, jnp.zeros((ATTN_ROWS, HEAD_DIM), F32), zero, zero, jnp.float32(0.0))
            dq_ref[...] = lax.while_loop(_attn_more, step, step(first))[2].astype(BF)

        @pl.when(i == nq - 1)
        def _():
            dk_ref[...] = dk_acc[...].astype(BF)
            dv_ref[...] = dv_acc[...].astype(BF)

    qspec = pl.BlockSpec((1, ATTN_ROWS, HEAD_DIM), lambda h, i: (3, i, h))
    rowblk = pl.BlockSpec((ATTN_ROWS, HEAD_DIM), lambda h, i: (i, h))
    head = pl.BlockSpec((t, HEAD_DIM), lambda h, i: (0, h))
    trispec = pl.BlockSpec((ATTN_SUB, ATTN_SUB), lambda h, i: (0, 0))
    return _pallas(
        body, name="attn_bwd", grid=(N_HEADS, nq), ride=ride,
        in_specs=[qspec, _head_cols(4)(t), _head_cols(5)(t), rowblk, rowblk, _first_step_spec(), _first_step_spec(), _reach_spec(),
                  trispec, trispec],
        out_specs=[rowblk, head, head],
        out_shape=[S((t, D_MODEL), BF)] * 3,
        scratch_shapes=[pltpu.VMEM((t, HEAD_DIM), F32), pltpu.VMEM((t, HEAD_DIM), F32)],
        args=[proj, proj, proj, o, d_o, a_first, beta_first, reach, tri_strict, tri_incl])


def _tail(h3, n4, p, w_pg, w_pp, g_ple, g_final, target):
    t = h3.shape[0]
    tm = min(TOKEN_TILE, t)
    steps = t // tm

    def body(h_ref, n_ref, p_ref, wpg_ref, wpp_ref, gp_ref, gf_ref, tgt_ref,
             dh_ref, ds_ref, dpp_ref, loss_ref, dgf_ref, dgp_ref):
        pg = jax.nn.sigmoid(_dot(n_ref[...], wpg_ref[...]))
        pp = _dot(p_ref[...].astype(BF), wpp_ref[...])
        h3v = h_ref[...]
        h4 = h3v + pg * pp
        gf = gf_ref[...]
        diff = _rms(h4, gf) - tgt_ref[...]
        _accumulate(loss_ref, jnp.sum(diff * diff, axis=0, keepdims=True))
        dh4, dgf = _rms_bwd(diff * (1.0 / D_MODEL), h4, gf)
        _accumulate(dgf_ref, dgf)
        dpp_ref[...] = (dh4 * pg).astype(BF)
        ds = (dh4 * pp * pg * (1.0 - pg)).astype(BF)
        ds_ref[...] = ds
        dh3, dgp = _rms_bwd(_dot_nt(ds, wpg_ref[...]), h3v, gp_ref[...])
        _accumulate(dgp_ref, dgp)
        dh_ref[...] = dh4 + dh3

        @pl.when(pl.program_id(0) == steps - 1)
        def _():
            loss_ref[...] = jnp.full(loss_ref.shape, 0.5 / D_MODEL * jnp.sum(loss_ref[...]), F32)

    vec = (1, D_MODEL)
    return pl.pallas_call(
        body, name="tail", grid=(steps,),
        in_specs=[_rows(tm, D_MODEL), _rows(tm, D_MODEL), _rows(tm, PLE_DIM), _const_spec((D_MODEL, D_MODEL)),
                  _const_spec((PLE_DIM, D_MODEL)), _const_spec(vec), _const_spec(vec), _rows(tm, D_MODEL)],
        out_specs=[_rows(tm, D_MODEL)] * 3 + [_acc_spec(vec)] * 3,
        out_shape=[S((t, D_MODEL), F32), S((t, D_MODEL), BF), S((t, D_MODEL), BF)] + [S(vec, F32)] * 3,
    )(h3, n4, p, w_pg, w_pp, g_ple, g_final, target)


def _wgrad(xs, ys, name, ride=None):
    bx, t, k = xs.shape
    by, _, n = ys.shape
    b = max(bx, by)
    tt = min(WGRAD_TILE * 2 // xs.dtype.itemsize, t)
    steps = t // tt

    def body(x_ref, y_ref, o_ref, acc_ref):
        s = pl.program_id(1)

        @pl.when(s == 0)
        def _():
            acc_ref[...] = jnp.zeros_like(acc_ref)
        acc_ref[...] += _dot_tn(x_ref[0].astype(BF), y_ref[0].astype(BF))

        @pl.when(s == steps - 1)
        def _():
            o_ref[0] = acc_ref[...].astype(BF)

    (out,), landed = _pallas(
        body, name=name, grid=(b, steps), ride=ride,
        in_specs=[pl.BlockSpec((1, tt, k), (lambda j, s: (j, s, 0)) if bx > 1 else (lambda j, s: (0, s, 0))),
                  pl.BlockSpec((1, tt, n), (lambda j, s: (j, s, 0)) if by > 1 else (lambda j, s: (0, s, 0)))],
        out_specs=[pl.BlockSpec((1, k, n), lambda j, s: (j, 0, 0))],
        out_shape=[S((b, k, n), BF)],
        scratch_shapes=[pltpu.VMEM((k, n), F32)],
        args=[xs, ys])
    return (out, landed) if ride is not None else out


def _wgrad_pieces(x, ys, name, ride=None, tile=None, row_parts=1):
    t, k = x.shape
    n = ys[0].shape[2]
    counts = [y.shape[0] for y in ys]
    offsets = [sum(counts[:j]) for j in range(len(ys))]
    total = sum(counts)
    tt = min(tile or WGRAD_TILE, t)
    steps = t // tt
    kp = k // row_parts

    def body(x_ref, *refs):
        y_refs, o_refs, acc_ref = refs[:len(ys)], refs[len(ys):len(ys) + row_parts], refs[len(ys) + row_parts]
        p, s = pl.program_id(0), pl.program_id(1)

        @pl.when(s == 0)
        def _():
            acc_ref[...] = jnp.zeros_like(acc_ref)
        for j, y_ref in enumerate(y_refs):
            @pl.when(jnp.logical_and(p >= offsets[j], p < offsets[j] + counts[j]))
            def _(y_ref=y_ref):
                acc_ref[...] += _dot_tn(x_ref[...], y_ref[0])

        @pl.when(s == steps - 1)
        def _():
            for part, o_ref in enumerate(o_refs):
                o_ref[0] = acc_ref[part * kp:(part + 1) * kp, :].astype(BF)

    def turn(j):
        lo, hi = offsets[j], offsets[j] + counts[j]
        return lambda p, s: (jnp.clip(p - lo, 0, counts[j] - 1), jnp.where(p < lo, 0, jnp.where(p >= hi, steps - 1, s)), 0)

    outs, landed = _pallas(
        body, name=name, grid=(total, steps), ride=ride,
        in_specs=[pl.BlockSpec((tt, k), lambda p, s: (s, 0))] + [pl.BlockSpec((1, tt, n), turn(j)) for j in range(len(ys))],
        out_specs=[pl.BlockSpec((1, kp, n), lambda p, s: (p, 0, 0))] * row_parts,
        out_shape=[S((total, kp, n), BF)] * row_parts,
        scratch_shapes=[pltpu.VMEM((k, n), F32)],
        args=[x, *ys])
    out = outs[0] if row_parts == 1 else outs
    return (out, landed) if ride is not None else out


def _ffn_bwd_hidden(dh, to_gate, to_up, w_out, name, ride=None):
    t = dh.shape[0]
    tm = min(TOKEN_TILE, t)

    def body(dh_ref, to_gate_ref, to_up_ref, wout_ref, df_ref, dgate_ref, dup_ref):
        df = (0.5 * dh_ref[...]).astype(BF)
        df_ref[...] = df
        for c in range(N_FF_CHUNKS):
            dact = _dot_nt(df, wout_ref[c])
            dgate_ref[c] = (dact * to_gate_ref[c].astype(F32)).astype(BF)
            dup_ref[c] = (dact * to_up_ref[c].astype(F32)).astype(BF)

    return _pallas(
        body, name=name, grid=(t // tm,), ride=ride,
        in_specs=[_rows(tm, D_MODEL), _chunks(tm), _chunks(tm), _const_spec(w_out.shape)],
        out_specs=[_rows(tm, D_MODEL), _chunks(tm), _chunks(tm)],
        out_shape=[S((t, D_MODEL), BF)] + [S((N_FF_CHUNKS, t, FF_CHUNK), BF)] * 2,
        args=[dh, to_gate, to_up, w_out])


def _ffn_bwd_input(dh, h_in, g, dgate, dup, w_in, name, ride=None):
    t = dh.shape[0]
    tm = min(TOKEN_TILE, t)

    def body(dh_ref, h_ref, g_ref, dgate_ref, dup_ref, win_ref, dhi_ref, dg_ref):
        dn = jnp.zeros((tm, D_MODEL), F32)
        for c in range(N_FF_CHUNKS):
            dn = dn + _dot_nt(dgate_ref[c], win_ref[c]) + _dot_nt(dup_ref[c], win_ref[N_FF_CHUNKS + c])
        dhi, dg = _rms_bwd(dn, h_ref[...], g_ref[...])
        _accumulate(dg_ref, dg)
        dhi_ref[...] = dh_ref[...] + dhi

    vec = (1, D_MODEL)
    return _pallas(
        body, name=name, grid=(t // tm,), ride=ride,
        in_specs=[_rows(tm, D_MODEL), _rows(tm, D_MODEL), _const_spec(vec), _chunks(tm), _chunks(tm), _const_spec(w_in.shape)],
        out_specs=[_rows(tm, D_MODEL), _acc_spec(vec)],
        out_shape=[S((t, D_MODEL), F32), S(vec, F32)],
        args=[dh, h_in, g, dgate, dup, w_in])


def _mixer_bwd(dh2, proj, yc, ya, conv_w, w_co, w_ao, w_mo, ride=None):
    t = dh2.shape[0]
    tm = min(TOKEN_TILE, t)

    def body(dh_ref, cb_ref, cc_ref, cx_ref, gc_ref, ga_ref, cch_ref, cxh_ref, yc_ref, ya_ref, cw_ref, wco_ref, wao_ref, wmo_ref,
             dhb_ref, dyc_ref, dya_ref, dgc_ref, dga_ref, dcb_ref, dcv_ref, do_ref):
        dhb = dh_ref[...].astype(BF)
        dhb_ref[...] = dhb
        dmerged = _dot_nt(dhb, wmo_ref[...])
        sc = jax.nn.sigmoid(gc_ref[0].astype(F32))
        sa = jax.nn.sigmoid(ga_ref[0].astype(F32))
        dyc = (dmerged * sc).astype(BF)
        dya = (dmerged * sa).astype(BF)
        dyc_ref[...] = dyc
        dya_ref[...] = dya
        dgc_ref[...] = (dmerged * yc_ref[...].astype(F32) * sc * (1.0 - sc)).astype(BF)
        dga_ref[...] = (dmerged * ya_ref[...].astype(F32) * sa * (1.0 - sa)).astype(BF)
        m, m1, m2 = _conv_inputs(cc_ref, cx_ref, cch_ref, cxh_ref)
        cw = cw_ref[...]
        cv = cw[0:1, :] * m2 + cw[1:2, :] * m1 + cw[2:3, :] * m
        dycin = _dot_nt(dyc, wco_ref[...])
        dcb_ref[...] = (dycin * cv).astype(BF)
        dcv_ref[...] = dycin * cb_ref[0].astype(F32)
        do_ref[...] = _dot_nt(dya, wao_ref[...]).astype(BF)

    sq = (D_MODEL, D_MODEL)
    return _pallas(
        body, name="mixer_bwd", grid=(t // tm,), ride=ride,
        in_specs=[_rows(tm, D_MODEL), _piece(0, tm), _piece(1, tm), _piece(2, tm), _piece(6, tm), _piece(7, tm),
                  _prev_halo(1, tm), _prev_halo(2, tm), _rows(tm, D_MODEL), _rows(tm, D_MODEL),
                  _const_spec((3, D_MODEL)), _const_spec(sq), _const_spec(sq), _const_spec(sq)],
        out_specs=[_rows(tm, D_MODEL)] * 8,
        out_shape=[S((t, D_MODEL), BF)] * 6 + [S((t, D_MODEL), F32), S((t, D_MODEL), BF)],
        args=[dh2, proj, proj, proj, proj, proj, proj, proj, yc, ya, conv_w, w_co, w_ao, w_mo])


F32_HALO = 8


def _conv_bwd(dcv, proj, conv_w):
    t = dcv.shape[0]
    tm = min(TOKEN_TILE, t)
    steps = t // tm

    def body(dcv_ref, nxt_ref, cc_ref, cx_ref, cch_ref, cxh_ref, cw_ref, dcc_ref, dcx_ref, dw_ref):
        i = pl.program_id(0)
        m, m1, m2 = _conv_inputs(cc_ref, cx_ref, cch_ref, cxh_ref)
        d0 = dcv_ref[...]
        nxt = jnp.where(i == steps - 1, 0.0, nxt_ref[...])
        row = lax.broadcasted_iota(jnp.int32, (tm, 1), 0)
        d1 = jnp.where(row == tm - 1, nxt[0:1, :], pltpu.roll(d0, tm - 1, 0))
        d2 = pltpu.roll(d0, tm - 2, 0)
        d2 = jnp.where(row == tm - 2, nxt[0:1, :], jnp.where(row == tm - 1, nxt[1:2, :], d2))
        cw = cw_ref[...]
        dm = cw[2:3, :] * d0 + cw[1:2, :] * d1 + cw[0:1, :] * d2
        dcc_ref[...] = (dm * cx_ref[0].astype(F32)).astype(BF)
        dcx_ref[...] = (dm * cc_ref[0].astype(F32)).astype(BF)
        tap_row = lax.broadcasted_iota(jnp.int32, (F32_HALO, 1), 0)
        dw = jnp.zeros((F32_HALO, D_MODEL), F32)
        for j, mk in enumerate((m2, m1, m)):
            dw = jnp.where(tap_row == j, jnp.sum(d0 * mk, axis=0, keepdims=True), dw)
        _accumulate(dw_ref, dw)

    nxt_spec = pl.BlockSpec((F32_HALO, D_MODEL), lambda i: (jnp.minimum((i + 1) * (tm // F32_HALO), t // F32_HALO - 1), 0))
    return pl.pallas_call(
        body, name="conv_bwd", grid=(steps,),
        in_specs=[_rows(tm, D_MODEL), nxt_spec, _piece(1, tm), _piece(2, tm), _prev_halo(1, tm), _prev_halo(2, tm),
                  _const_spec((3, D_MODEL))],
        out_specs=[_rows(tm, D_MODEL), _rows(tm, D_MODEL), _acc_spec((F32_HALO, D_MODEL))],
        out_shape=[S((t, D_MODEL), BF), S((t, D_MODEL), BF), S((F32_HALO, D_MODEL), F32)],
    )(dcv, dcv, proj, proj, proj, proj, conv_w)


def _mix_bwd(dpieces, w_mix, h1, dh2, g, ride=None):
    t = h1.shape[0]
    tm = min(TOKEN_TILE, t)

    def body(*refs):
        pieces, (w_ref, h_ref, dh_ref, g_ref, dhi_ref, dg_ref) = refs[:N_MIX], refs[N_MIX:]
        du = jnp.zeros((tm, D_MODEL), F32)
        for d in range(N_MIX):
            du = du + _dot_nt(pieces[d][...], w_ref[d])
        dhi, dg = _rms_bwd(du, h_ref[...], g_ref[...])
        _accumulate(dg_ref, dg)
        dhi_ref[...] = dh_ref[...] + dhi

    vec = (1, D_MODEL)
    return _pallas(
        body, name="mix_bwd", grid=(t // tm,), ride=ride,
        in_specs=[_rows(tm, D_MODEL)] * N_MIX + [_const_spec(w_mix.shape), _rows(tm, D_MODEL), _rows(tm, D_MODEL), _const_spec(vec)],
        out_specs=[_rows(tm, D_MODEL), _acc_spec(vec)],
        out_shape=[S((t, D_MODEL), F32), S(vec, F32)],
        args=[*dpieces, w_mix, h1, dh2, g])


def _adamw(partials, w, m, v, name):
    parts = list(partials) if isinstance(partials, (list, tuple)) else [partials]
    r, c = w.shape
    tr = min(r, 512 // len(parts))
    first_tile = [sum(p.shape[1] for p in parts[:j]) // tr for j in range(len(parts))]
    c1 = 1.0 - ADAM_B1 ** ADAM_STEP
    c2 = 1.0 - ADAM_B2 ** ADAM_STEP

    def body(*refs):
        p_refs, (w_ref, m_ref, v_ref, g_ref, d_ref, mo_ref, vo_ref) = refs[:len(parts)], refs[len(parts):]
        g = None
        for j, p_ref in enumerate(p_refs):
            gj = p_ref[0].astype(F32)
            for s in range(1, N_SHARDS):
                gj = gj + p_ref[s].astype(F32)
            g = gj if g is None else jnp.where(pl.program_id(0) >= first_tile[j], gj, g)
        mn = ADAM_B1 * m_ref[...] + (1.0 - ADAM_B1) * g
        vn = ADAM_B2 * v_ref[...] + (1.0 - ADAM_B2) * (g * g)
        g_ref[...] = g
        mo_ref[...] = mn
        vo_ref[...] = vn
        d_ref[...] = -ADAM_LR * ((mn / c1) / (jnp.sqrt(vn / c2) + ADAM_EPS) + ADAM_WD * w_ref[...])

    def rows_of(j):
        last = parts[j].shape[1] // tr - 1
        return lambda i: (0, jnp.clip(i - first_tile[j], 0, last), 0)

    blk = pl.BlockSpec((tr, c), lambda i: (i, 0))
    return pl.pallas_call(
        body, name=name, grid=(r // tr,),
        in_specs=[pl.BlockSpec((N_SHARDS, tr, c), rows_of(j)) for j in range(len(parts))] + [blk, blk, blk],
        out_specs=[blk] * 4, out_shape=[S((r, c), F32)] * 4,
    )(*parts, w, m, v)


_MATRICES = ("ffn1_w_in", "ffn1_w_out", "w_mix_in", "conv_w", "w_conv_out", "w_attn_out", "w_mix_out",
             "ffn2_w_in", "ffn2_w_out", "w_ple_gate", "w_ple_proj")
_GAINS = ("ffn1_norm", "mix_norm", "ffn2_norm", "ple_norm", "final_norm")
_WEIGHTS = ("ffn1_norm", "ffn1_w_in", "ffn1_w_out", "mix_norm", "w_mix_in", "conv_w", "w_conv_out", "w_attn_out", "w_mix_out",
            "ffn2_norm", "ffn2_w_in", "ffn2_w_out", "ple_norm", "w_ple_gate", "w_ple_proj", "final_norm")
CONV_ROWS = 8


def _columns_from_shards(g):
    return jnp.transpose(g, (1, 0, 2)).reshape(g.shape[1], N_SHARDS * g.shape[2])


def _shards_from_columns(a):
    r, c = a.shape
    return jnp.transpose(a.reshape(r, N_SHARDS, c // N_SHARDS), (1, 0, 2))


def kernel(x, p, ffn1_norm, ffn1_w_in, ffn1_w_out, mix_norm, w_mix_in, conv_w, w_conv_out, w_attn_out, w_mix_out, ffn2_norm, ffn2_w_in, ffn2_w_out, ple_norm, w_ple_gate, w_ple_proj, final_norm, loss_target, m_ffn1_norm, m_ffn1_w_in, m_ffn1_w_out, m_mix_norm, m_w_mix_in, m_conv_w, m_w_conv_out, m_w_attn_out, m_w_mix_out, m_ffn2_norm, m_ffn2_w_in, m_ffn2_w_out, m_ple_norm, m_w_ple_gate, m_w_ple_proj, m_final_norm, v_ffn1_norm, v_ffn1_w_in, v_ffn1_w_out, v_mix_norm, v_w_mix_in, v_conv_w, v_w_conv_out, v_w_attn_out, v_w_mix_out, v_ffn2_norm, v_ffn2_w_in, v_ffn2_w_out, v_ple_norm, v_w_ple_gate, v_w_ple_proj, v_final_norm):
    given = dict(locals())
    t = x.shape[1]
    xs = x.reshape(t, D_MODEL)
    ps = p.reshape(t, PLE_DIM)
    target = loss_target.reshape(t, D_MODEL)
    shard = {k: given[k].reshape(given[k].shape[-2:]) for k in _MATRICES}
    gain = {k: given[k].reshape(1, D_MODEL) for k in _GAINS}

    send = {k: shard[k].astype(BF) for k in _MATRICES}
    send["conv_w"] = jnp.pad(shard["conv_w"], ((0, CONV_ROWS - 3), (0, 0)))
    loss_vec, dx, landed, gain_grads = _forward_backward(xs, ps, target, gain, send)
    gain_rows = jnp.concatenate([gain_grads[k] for k in _GAINS] + [jnp.zeros((8 - len(_GAINS), D_MODEL), F32)], axis=0)
    gain_parts, = _exchange_alone("gather", [gain_rows], "gather_gain_gradients")

    out = {}
    for k in _MATRICES:
        w, m, v = shard[k], given["m_" + k].reshape(shard[k].shape), given["v_" + k].reshape(shard[k].shape)
        part = landed[k]
        if k == "conv_w":
            pad = ((0, CONV_ROWS - 3), (0, 0))
            w, m, v = jnp.pad(w, pad), jnp.pad(m, pad), jnp.pad(v, pad, constant_values=1.0)
        res = _adamw(part, w, m, v, "adamw_" + k)
        out[k] = [r[:3] if k == "conv_w" else r for r in res]
    stack = lambda pre: jnp.concatenate([given[pre + k].reshape(1, D_MODEL) for k in _GAINS] + [jnp.ones((8 - len(_GAINS), D_MODEL), F32)], axis=0)
    res = _adamw(gain_parts, stack(""), stack("m_"), stack("v_"), "adamw_gains")
    for j, k in enumerate(_GAINS):
        out[k] = [r[j:j + 1] for r in res]

    loss = lax.psum(loss_vec[0, 0], ("x", "y", "c"))
    per_kind = [[out[k][j].reshape(given[k].shape) for k in _WEIGHTS] for j in range(4)]
    return (loss, dx.reshape(x.shape), *per_kind[0], *per_kind[1], *per_kind[2], *per_kind[3])


def _forward_backward(xs, ps, target, gain, send, full=None):
    exchange = full is None
    full = dict(full or {})
    grads, landed = {}, {}

    def gather(names):
        return ("gather", [send[k] for k in names]) if exchange else None

    def scatter(names):
        return ("scatter", [grads[k] for k in names]) if exchange else None

    def keep(into, names, got):
        into.update(zip(names, got))

    first = ("ffn1_w_in",)
    (n1,), got = _prenorm(xs, gain["ffn1_norm"], ride=gather(first))
    keep(full, first, got)
    w1_in = full["ffn1_w_in"]
    second = ("ffn1_w_out", "w_mix_in")
    (act1, to_gate1, to_up1), got = _ffn_up(n1, w1_in, "ffn1_up", ride=gather(second))
    keep(full, second, got)
    w1_out = full["ffn1_w_out"].reshape(N_FF_CHUNKS, FF_CHUNK, D_MODEL)
    third = ("conv_w", "w_conv_out", "w_attn_out", "w_mix_out")
    (h1, u), got = _ffn_down(xs, act1, w1_out, gain["mix_norm"], "ffn1_down", ride=gather(third))
    keep(full, third, got)
    w_mix = full["w_mix_in"]
    w_co, w_ao, w_mo = (full[k].reshape(D_MODEL, D_MODEL) for k in ("w_conv_out", "w_attn_out", "w_mix_out"))
    taps = _columns_from_shards(full["conv_w"][:, :3, :])
    rest = ("ffn2_w_in", "ffn2_w_out", "w_ple_gate", "w_ple_proj")
    (proj,), got = _mix_proj(u, w_mix, ride=gather(rest))
    keep(full, rest, got)
    w2_in, w2_out = full["ffn2_w_in"], full["ffn2_w_out"].reshape(N_FF_CHUNKS, FF_CHUNK, D_MODEL)
    w_pg = full["w_ple_gate"].reshape(D_MODEL, D_MODEL)
    w_pp = _columns_from_shards(full["w_ple_proj"])
    o, a_first, beta_first, reach = _attn_fwd(proj)
    h2, n3, ycin, yc, ya, merged = _mixer_out(proj, o, h1, taps, w_co, w_ao, w_mo, gain["ffn2_norm"])
    (act2, to_gate2, to_up2), _ = _ffn_up(n3, w2_in, "ffn2_up")
    (h3, n4), _ = _ffn_down(h2, act2, w2_out, gain["ple_norm"], "ffn2_down")
    dh3, ds, dpp, loss_vec, dg_final, dg_ple = _tail(h3, n4, ps, w_pg, w_pp, gain["ple_norm"], gain["final_norm"], target)

    one = lambda a: a[None]
    by_rows = lambda g, rows: g.reshape(N_SHARDS, rows // N_SHARDS, D_MODEL)
    grads["w_ple_gate"] = by_rows(_wgrad(one(n4), one(ds), "wgrad_ple_gate"), D_MODEL)
    grads["w_ple_proj"] = _shards_from_columns(_wgrad(one(ps), one(dpp), "wgrad_ple_proj")[0])
    ple = ("w_ple_gate", "w_ple_proj")
    (df2, dgate2, dup2), got = _ffn_bwd_hidden(dh3, to_gate2, to_up2, w2_out, "ffn2_bwd_hidden", ride=scatter(ple))
    keep(landed, ple, got)
    grads["ffn2_w_out"] = by_rows(_wgrad(act2, one(df2), "wgrad_ffn2_out"), D_FF)
    grads["ffn2_w_in"] = _wgrad_pieces(n3, [dgate2, dup2], "wgrad_ffn2_in")
    (dh2, dg_ffn2), got = _ffn_bwd_input(dh3, h2, gain["ffn2_norm"], dgate2, dup2, w2_in, "ffn2_bwd_input", ride=scatter(("ffn2_w_out",)))
    keep(landed, ("ffn2_w_out",), got)
    (dh2b, dyc, dya, dgc, dga, dcb, dcv, d_o), _ = _mixer_bwd(dh2, proj, yc, ya, taps, w_co, w_ao, w_mo)
    grads["w_mix_out"] = by_rows(_wgrad(one(merged), one(dh2b), "wgrad_mix_out"), D_MODEL)
    grads["w_conv_out"] = by_rows(_wgrad(one(ycin), one(dyc), "wgrad_conv_out"), D_MODEL)
    grads["w_attn_out"] = by_rows(_wgrad(one(o), one(dya), "wgrad_attn_out"), D_MODEL)
    dcc, dcx, dtaps = _conv_bwd(dcv, proj, taps)
    grads["conv_w"] = jnp.pad(_shards_from_columns(dtaps[:3]), ((0, 0), (0, CONV_ROWS - 3), (0, 0)))
    behind_attn = ("ffn2_w_in", "w_mix_out", "w_conv_out", "w_attn_out", "conv_w")
    (dq, dk, dv), got = _attn_bwd(proj, o, d_o, a_first, beta_first, reach, ride=scatter(behind_attn))
    keep(landed, behind_attn, got)
    dpieces = [dcb, dcc, dcx, dq, dk, dv, dgc, dga]
    half = N_MIX // 2
    tops, bottoms = zip(_wgrad_pieces(u, [one(dp) for dp in dpieces[:half]], "wgrad_mix_in_a", tile=WGRAD_TILE // 2, row_parts=2),
                        _wgrad_pieces(u, [one(dp) for dp in dpieces[half:]], "wgrad_mix_in_b", tile=WGRAD_TILE // 2, row_parts=2))
    grads["w_mix_in top"], grads["w_mix_in bottom"] = jnp.concatenate(tops, axis=0), jnp.concatenate(bottoms, axis=0)
    (dh1, dg_mix), top = _mix_bwd(dpieces, w_mix, h1, dh2, gain["mix_norm"], ride=scatter(("w_mix_in top",)))
    (df1, dgate1, dup1), bottom = _ffn_bwd_hidden(dh1, to_gate1, to_up1, w1_out, "ffn1_bwd_hidden", ride=scatter(("w_mix_in bottom",)))
    if exchange:
        landed["w_mix_in"] = [top[0], bottom[0]]
    else:
        grads["w_mix_in"] = jnp.concatenate([grads.pop("w_mix_in top"), grads.pop("w_mix_in bottom")], axis=1)
    grads["ffn1_w_out"] = by_rows(_wgrad(act1, one(df1), "wgrad_ffn1_out"), D_FF)
    if exchange:
        grads["ffn1_w_in"], got = _wgrad_pieces(n1, [dgate1, dup1], "wgrad_ffn1_in", ride=scatter(("ffn1_w_out",)))
        keep(landed, ("ffn1_w_out",), got)
    else:
        grads["ffn1_w_in"] = _wgrad_pieces(n1, [dgate1, dup1], "wgrad_ffn1_in")
    (dx, dg_ffn1), got = _ffn_bwd_input(dh1, xs, gain["ffn1_norm"], dgate1, dup1, w1_in, "ffn1_bwd_input", ride=scatter(("ffn1_w_in",)))
    keep(landed, ("ffn1_w_in",), got)
    gain_grads = dict(ffn1_norm=dg_ffn1, mix_norm=dg_mix, ffn2_norm=dg_ffn2, ple_norm=dg_ple, final_norm=dg_final)
    return loss_vec, dx, (landed if exchange else grads), gain_grads
```

```python
import functools
import math

import jax
import jax.numpy as jnp
from jax import lax
from jax.experimental import pallas as pl
from jax.experimental.pallas import tpu as pltpu

D_MODEL = 1024
D_FF = 2816
N_SHARDS = 8
FF_CHUNK = 2 * D_FF // N_SHARDS
N_FF_CHUNKS = D_FF // FF_CHUNK
N_HEADS = 8
HEAD_DIM = 128
PLE_DIM = 256
NORM_EPS = 1e-6
N_MIX = 8
ADAM_LR, ADAM_B1, ADAM_B2, ADAM_EPS, ADAM_WD, ADAM_STEP = 0.001, 0.9, 0.999, 1e-08, 0.01, 10

TOKEN_TILE = 512
WGRAD_TILE = 4096
PROJ_TILE = 2048
ATTN_ROWS = 512
ATTN_Q = 128
ATTN_SUB = 128
ATTN_K = 3 * ATTN_SUB
ATTN_SKIP_BELOW = -90.0

BF = jnp.bfloat16
F32 = jnp.float32
MESH = pl.DeviceIdType.MESH
NT = (((1,), (1,)), ((), ()))
TN = (((0,), (0,)), ((), ()))
S = jax.ShapeDtypeStruct
ANY = pl.BlockSpec(memory_space=pl.ANY)


def _const_spec(shape):
    nd = len(shape)
    return pl.BlockSpec(shape, lambda *_: (0,) * nd, pipeline_mode=pl.Buffered(1))


def _rows(tm, cols):
    return pl.BlockSpec((tm, cols), lambda i: (i, 0))


def _chunks(tm):
    return pl.BlockSpec((N_FF_CHUNKS, tm, FF_CHUNK), lambda i: (0, i, 0))


def _acc_spec(shape):
    nd = len(shape)
    return pl.BlockSpec(shape, lambda *_: (0,) * nd)


def _dot(a, b):
    return jnp.dot(a, b, preferred_element_type=F32)


def _dot_nt(a, b):
    return lax.dot_general(a, b, NT, preferred_element_type=F32)


def _dot_tn(a, b):
    return lax.dot_general(a, b, TN, preferred_element_type=F32)


def _rms(h, g):
    r = lax.rsqrt(jnp.mean(h * h, axis=-1, keepdims=True) + NORM_EPS)
    return h * r * g


def _rms_bwd(dn, h, g):
    r = lax.rsqrt(jnp.mean(h * h, axis=-1, keepdims=True) + NORM_EPS)
    nh = h * r
    gd = dn * g
    dh = r * (gd - nh * jnp.mean(gd * nh, axis=-1, keepdims=True))
    return dh, jnp.sum(dn * nh, axis=0, keepdims=True)


def _accumulate(ref, val):
    @pl.when(pl.program_id(0) == 0)
    def _():
        ref[...] = jnp.zeros_like(ref)
    ref[...] += val


def _place():
    x, y, c = lax.axis_index("x"), lax.axis_index("y"), lax.axis_index("c")
    return x, y, c


def _slot(px, py, pc):
    return 4 * px + 2 * py + pc


def _gather_phases(ins, outs, send_sems, recv_sems, local_sems):
    n = len(ins)

    def parties():
        x, y, c = _place()
        return (x, y, c), (x, y, 1 - c), [(1 - x, y), (x, 1 - y), (1 - x, 1 - y)], c

    def copy(a, k, block, to, src=None):
        dst = outs[a].at[_slot(*block)]
        return pltpu.make_async_remote_copy(
            src_ref=dst if src is None else src, dst_ref=dst,
            send_sem=send_sems.at[a, k], recv_sem=recv_sems.at[a, k],
            device_id=to, device_id_type=MESH)

    def own(a, me):
        return pltpu.make_async_copy(ins[a], outs[a].at[_slot(*me)], local_sems.at[a])

    def first(a, me, sibling, chips, c):
        return [copy(a, 0, me, sibling, src=ins[a])] + [copy(a, 1 + j, me, (*chip, c), src=ins[a]) for j, chip in enumerate(chips)]

    def start():
        me, sibling, chips, c = parties()
        for a in range(n):
            own(a, me).start()
        for a in range(n):
            for cp in first(a, me, sibling, chips, c):
                cp.start()

    def forward():
        me, sibling, chips, c = parties()
        for j, chip in enumerate(chips):
            for a in range(n):
                copy(a, 1 + j, (*chip, c), me).wait_recv()
                copy(a, 4 + j, (*chip, c), sibling).start()

    def finish():
        me, sibling, chips, c = parties()
        for a in range(n):
            copy(a, 0, sibling, me).wait_recv()
            for j, chip in enumerate(chips):
                copy(a, 4 + j, (*chip, 1 - c), me).wait_recv()
        for a in range(n):
            for cp in first(a, me, sibling, chips, c) + [copy(a, 4 + j, (*chip, c), sibling) for j, chip in enumerate(chips)]:
                cp.wait_send()
            own(a, me).wait()

    return [start, forward, finish]


def _scatter_phases(ins, outs, send_sems, recv_sems, local_sems):
    n = len(ins)

    def copies():
        x, y, c = _place()
        me = _slot(x, y, c)
        out = [pltpu.make_async_copy(ins[a].at[me], outs[a].at[me], local_sems.at[a]) for a in range(n)]
        for k in range(1, N_SHARDS):
            px = 1 - x if k & 4 else x
            py = 1 - y if k & 2 else y
            pc = 1 - c if k & 1 else c
            for a in range(n):
                out.append(pltpu.make_async_remote_copy(
                    src_ref=ins[a].at[_slot(px, py, pc)], dst_ref=outs[a].at[me],
                    send_sem=send_sems.at[a, k - 1], recv_sem=recv_sems.at[a, k - 1],
                    device_id=(px, py, pc), device_id_type=MESH))
        return out

    def start():
        for cp in copies():
            cp.start()

    def finish():
        for cp in copies():
            cp.wait()

    return [start, finish]


def _pallas(body, *, name, grid, in_specs, out_specs, out_shape, args, scratch_shapes=(), ride=None):
    if ride is None:
        outs = pl.pallas_call(body, name=name, grid=grid, in_specs=in_specs, out_specs=out_specs, out_shape=out_shape,
                              scratch_shapes=list(scratch_shapes))(*args)
        return list(outs), []
    kind, arrays = ride
    n, n_in, n_out, n_scr = len(arrays), len(in_specs), len(out_specs), len(scratch_shapes)
    total = math.prod(grid)
    middle = (9 * total) // 10
    landed_shape = [S((N_SHARDS,) + a.shape if kind == "gather" else a.shape, a.dtype) for a in arrays]

    def with_exchange(*refs):
        ins, riders_in = refs[:n_in], refs[n_in:n_in + n]
        outs, riders_out = refs[n_in + n:n_in + n + n_out], refs[n_in + n + n_out:n_in + 2 * n + n_out]
        scratch, sems = refs[n_in + 2 * n + n_out:n_in + 2 * n + n_out + n_scr], refs[n_in + 2 * n + n_out + n_scr:]
        step = 0
        for axis, size in enumerate(grid):
            step = step * size + pl.program_id(axis)
        phases = (_gather_phases if kind == "gather" else _scatter_phases)(riders_in, riders_out, *sems)
        pl.when(step == 0)(phases[0])
        body(*ins, *outs, *scratch)
        for phase in phases[1:-1]:
            pl.when(step == middle)(phase)
        pl.when(step == total - 1)(phases[-1])

    outs = pl.pallas_call(
        with_exchange, name=name, grid=grid,
        in_specs=list(in_specs) + [ANY] * n, out_specs=list(out_specs) + [ANY] * n,
        out_shape=list(out_shape) + landed_shape,
        scratch_shapes=list(scratch_shapes) + [pltpu.SemaphoreType.DMA((n, 7)), pltpu.SemaphoreType.DMA((n, 7)),
                                               pltpu.SemaphoreType.DMA((n,))],
    )(*args, *arrays)
    return list(outs[:n_out]), list(outs[n_out:])


def _exchange_alone(kind, arrays, name):
    return _pallas(lambda: None, name=name, grid=(1,), in_specs=[], out_specs=[], out_shape=[], args=[], ride=(kind, arrays))[1]


def _prenorm(x, g, ride=None):
    t = x.shape[0]
    tm = min(TOKEN_TILE, t)

    def body(x_ref, g_ref, n_ref):
        n_ref[...] = _rms(x_ref[...], g_ref[...]).astype(BF)

    return _pallas(
        body, name="prenorm", grid=(t // tm,), ride=ride,
        in_specs=[_rows(tm, D_MODEL), _const_spec((1, D_MODEL))], out_specs=[_rows(tm, D_MODEL)],
        out_shape=[S((t, D_MODEL), BF)], args=[x, g])


def _ffn_up(n, w_in, name, ride=None):
    t = n.shape[0]
    tm = min(TOKEN_TILE, t)

    def body(n_ref, win_ref, act_ref, to_gate_ref, to_up_ref):
        nb = n_ref[...]
        for c in range(N_FF_CHUNKS):
            gate = _dot(nb, win_ref[c])
            up = _dot(nb, win_ref[N_FF_CHUNKS + c])
            sg = jax.nn.sigmoid(gate)
            silu = gate * sg
            act_ref[c] = (silu * up).astype(BF)
            to_gate_ref[c] = (up * (sg * (1.0 + gate * (1.0 - sg)))).astype(BF)
            to_up_ref[c] = silu.astype(BF)

    return _pallas(
        body, name=name, grid=(t // tm,), ride=ride,
        in_specs=[_rows(tm, D_MODEL), _const_spec(w_in.shape)],
        out_specs=[_chunks(tm)] * 3, out_shape=[S((N_FF_CHUNKS, t, FF_CHUNK), BF)] * 3,
        args=[n, w_in])


def _ffn_down(h, act, w_out, g_next, name, ride=None):
    t = h.shape[0]
    tm = min(TOKEN_TILE, t)

    def body(h_ref, act_ref, wout_ref, g_ref, ho_ref, no_ref):
        acc = jnp.zeros((tm, D_MODEL), F32)
        for c in range(N_FF_CHUNKS):
            acc = acc + _dot(act_ref[c], wout_ref[c])
        ho = h_ref[...] + 0.5 * acc
        ho_ref[...] = ho
        no_ref[...] = _rms(ho, g_ref[...]).astype(BF)

    return _pallas(
        body, name=name, grid=(t // tm,), ride=ride,
        in_specs=[_rows(tm, D_MODEL), _chunks(tm), _const_spec(w_out.shape), _const_spec((1, D_MODEL))],
        out_specs=[_rows(tm, D_MODEL)] * 2, out_shape=[S((t, D_MODEL), F32), S((t, D_MODEL), BF)],
        args=[h, act, w_out, g_next])


def _mix_proj(u, w_mix, ride=None):
    t = u.shape[0]
    tm = min(PROJ_TILE, t)

    def body(u_ref, w_ref, o_ref):
        o_ref[0] = _dot(u_ref[...], w_ref[0]).astype(BF)

    return _pallas(
        body, name="mix_proj", grid=(N_MIX, t // tm), ride=ride,
        in_specs=[pl.BlockSpec((tm, D_MODEL), lambda d, i: (i, 0)), pl.BlockSpec((1, D_MODEL, D_MODEL), lambda d, i: (d, 0, 0))],
        out_specs=[pl.BlockSpec((1, tm, D_MODEL), lambda d, i: (d, i, 0))],
        out_shape=[S((N_MIX, t, D_MODEL), BF)], args=[u, w_mix])


HALO = 16


def _piece(d, tm):
    return pl.BlockSpec((1, tm, D_MODEL), lambda i: (d, i, 0))


def _prev_halo(d, tm):
    return pl.BlockSpec((1, HALO, D_MODEL), lambda i: (d, jnp.maximum(i * (tm // HALO) - 1, 0), 0))


def _shift_down(m, prev_tail, k):
    tm = m.shape[0]
    out = pltpu.roll(m, k, 0)
    row = lax.broadcasted_iota(jnp.int32, (tm, 1), 0)
    for j in range(k):
        out = jnp.where(row == j, prev_tail[HALO - k + j:HALO - k + j + 1, :], out)
    return out


def _conv_inputs(cc_ref, cx_ref, cch_ref, cxh_ref):
    m = cc_ref[0].astype(F32) * cx_ref[0].astype(F32)
    mh = cch_ref[0].astype(F32) * cxh_ref[0].astype(F32)
    mh = jnp.where(pl.program_id(0) == 0, 0.0, mh)
    return m, _shift_down(m, mh, 1), _shift_down(m, mh, 2)


def _mixer_out(proj, o, h1, conv_w, w_co, w_ao, w_mo, g_next):
    t = h1.shape[0]
    tm = min(TOKEN_TILE, t)

    def body(cb_ref, cc_ref, cx_ref, gc_ref, ga_ref, cch_ref, cxh_ref, o_ref, h_ref, cw_ref, wco_ref, wao_ref, wmo_ref,
             g_ref, ho_ref, no_ref, ycin_ref, yc_ref, ya_ref, mg_ref):
        m, m1, m2 = _conv_inputs(cc_ref, cx_ref, cch_ref, cxh_ref)
        cw = cw_ref[...]
        cv = cw[0:1, :] * m2 + cw[1:2, :] * m1 + cw[2:3, :] * m
        ycin = (cb_ref[0].astype(F32) * cv).astype(BF)
        ycin_ref[...] = ycin
        yc = _dot(ycin, wco_ref[...])
        ya = _dot(o_ref[...].astype(BF), wao_ref[...])
        yc_ref[...] = yc.astype(BF)
        ya_ref[...] = ya.astype(BF)
        merged = (jax.nn.sigmoid(gc_ref[0].astype(F32)) * yc + jax.nn.sigmoid(ga_ref[0].astype(F32)) * ya).astype(BF)
        mg_ref[...] = merged
        ho = h_ref[...] + _dot(merged, wmo_ref[...])
        ho_ref[...] = ho
        no_ref[...] = _rms(ho, g_ref[...]).astype(BF)

    sq = (D_MODEL, D_MODEL)
    return pl.pallas_call(
        body, name="mixer_out", grid=(t // tm,),
        in_specs=[_piece(0, tm), _piece(1, tm), _piece(2, tm), _piece(6, tm), _piece(7, tm), _prev_halo(1, tm), _prev_halo(2, tm),
                  _rows(tm, D_MODEL), _rows(tm, D_MODEL), _const_spec((3, D_MODEL)), _const_spec(sq), _const_spec(sq),
                  _const_spec(sq), _const_spec((1, D_MODEL))],
        out_specs=[_rows(tm, D_MODEL)] * 6,
        out_shape=[S((t, D_MODEL), F32)] + [S((t, D_MODEL), BF)] * 5,
    )(proj, proj, proj, proj, proj, proj, proj, o, h1, conv_w, w_co, w_ao, w_mo, g_next)


def _suffix_sums(vals, tri, before):
    out, right = [], before
    for b in reversed(range(ATTN_K // ATTN_SUB)):
        v = vals[:, b * ATTN_SUB:(b + 1) * ATTN_SUB]
        hi = v.astype(BF)
        lo = (v - hi.astype(F32)).astype(BF)
        out.append(_dot(hi, tri) + _dot(lo, tri) + right)
        right = right + jnp.sum(v, axis=1, keepdims=True)
    return jnp.concatenate(out[::-1], axis=1), right


ATTN_UNITS = ATTN_ROWS // ATTN_Q


def _unit_rows(x, u):
    return x[u * ATTN_Q:(u + 1) * ATTN_Q]


def _per_unit(fn):
    return jnp.concatenate([fn(u) for u in range(ATTN_UNITS)], axis=0)


def _per_row(vals):
    local = lax.broadcasted_iota(jnp.int32, (ATTN_ROWS, 1), 0)
    out = jnp.full((ATTN_ROWS, 1), vals[0], jnp.int32)
    for u in range(1, ATTN_UNITS):
        out = jnp.where(local >= u * ATTN_Q, vals[u], out)
    return out


def _attn_step(q, k_ref, starts, bounds, row):
    z = _per_unit(lambda u: _dot_nt(_unit_rows(q, u), k_ref[0, pl.ds(starts[u], ATTN_K), :])) * (1.0 / math.sqrt(HEAD_DIM))
    mask = lax.broadcasted_iota(jnp.int32, (1, ATTN_K), 1) < jnp.minimum(row, _per_row(bounds)) - _per_row(starts)
    log_beta = jnp.minimum(z, 0.0) - jnp.log(1.0 + jnp.exp(jnp.minimum(z, -z)))
    log_rest = jnp.where(mask, log_beta - z, 0.0)
    return z, mask, log_beta, log_rest


def _attn_sweep_start(i, t):
    blks = tuple(jnp.maximum(i * ATTN_UNITS + u + 1 - ATTN_K // ATTN_SUB, 0) for u in range(ATTN_UNITS))
    return blks, tuple(jnp.int32(t) for _ in range(ATTN_UNITS))


def _attn_keys(blks):
    return [pl.multiple_of(b * ATTN_SUB, ATTN_SUB) for b in blks]


def _attn_next(blks):
    return tuple(jnp.maximum(b - ATTN_K // ATTN_SUB, 0) for b in blks), tuple(b * ATTN_SUB for b in blks)


def _attn_more(carry):
    return jnp.logical_and(carry[1][ATTN_UNITS - 1] > 0, carry[-1] > ATTN_SKIP_BELOW)


def _tri(strict):
    r = lax.broadcasted_iota(jnp.int32, (ATTN_SUB, ATTN_SUB), 0)
    c = lax.broadcasted_iota(jnp.int32, (ATTN_SUB, ATTN_SUB), 1)
    return (r > c if strict else r >= c).astype(BF)


REACH_TILE = (8, 128)


def _first_step_spec():
    return pl.BlockSpec((1, ATTN_ROWS, ATTN_K), lambda h, i: (h, i, 0))


def _reach_spec():
    return pl.BlockSpec((1, 1) + REACH_TILE, lambda h, i: (h, i, 0, 0))


def _head_cols(piece):
    return lambda t: pl.BlockSpec((1, t, HEAD_DIM), lambda h, i: (piece, 0, h))


def _attn_fwd(proj):
    t = proj.shape[1]
    nq = t // ATTN_ROWS
    tri = _tri(strict=True)

    def body(q_ref, k_ref, v_ref, tri_ref, o_ref, a_ref, beta_ref, reach_ref):
        i = pl.program_id(1)
        q = q_ref[0]
        row = i * ATTN_ROWS + lax.broadcasted_iota(jnp.int32, (ATTN_ROWS, 1), 0)

        def step(carry, keep=False):
            blks, bounds, acc, run, _ = carry
            starts = _attn_keys(blks)
            _, mask, log_beta, log_rest = _attn_step(q, k_ref, starts, bounds, row)
            tail, run = _suffix_sums(log_rest, tri_ref[...], run)
            a = jnp.where(mask, jnp.exp(log_beta + tail), 0.0).astype(BF)
            if keep:
                a_ref[0] = a
                beta_ref[0] = jnp.where(mask, jnp.exp(log_beta), 0.0).astype(BF)
            acc = acc + _per_unit(lambda u: _dot(_unit_rows(a, u), v_ref[0, pl.ds(starts[u], ATTN_K), :]))
            return (*_attn_next(blks), acc, run, jnp.max(run))

        first = (*_attn_sweep_start(i, t), jnp.zeros((ATTN_ROWS, HEAD_DIM), F32), jnp.zeros((ATTN_ROWS, 1), F32), jnp.float32(0.0))
        after_first = step(first, keep=True)
        reach_ref[...] = jnp.full(reach_ref.shape, after_first[-1], F32)
        o_ref[...] = lax.while_loop(_attn_more, step, after_first)[2]

    qspec = pl.BlockSpec((1, ATTN_ROWS, HEAD_DIM), lambda h, i: (3, i, h))
    return pl.pallas_call(
        body, name="attn_fwd", grid=(N_HEADS, nq),
        in_specs=[qspec, _head_cols(4)(t), _head_cols(5)(t), pl.BlockSpec((ATTN_SUB, ATTN_SUB), lambda h, i: (0, 0))],
        out_specs=[pl.BlockSpec((ATTN_ROWS, HEAD_DIM), lambda h, i: (i, h)), _first_step_spec(), _first_step_spec(), _reach_spec()],
        out_shape=[S((t, D_MODEL), F32), S((N_HEADS, t, ATTN_K), BF), S((N_HEADS, t, ATTN_K), BF), S((N_HEADS, nq) + REACH_TILE, F32)],
    )(proj, proj, proj, tri)


def _attn_bwd(proj, o, d_o, a_first, beta_first, reach, ride=None):
    t = proj.shape[1]
    nq = t // ATTN_ROWS
    tri_strict, tri_incl = _tri(strict=True), _tri(strict=False)
    scale = 1.0 / math.sqrt(HEAD_DIM)

    def body(q_ref, k_ref, v_ref, o_ref, do_ref, a_ref, beta_ref, reach_ref, tris_ref, trii_ref, dq_ref, dk_ref, dv_ref, dk_acc, dv_acc):
        i = pl.program_id(1)

        @pl.when(i == 0)
        def _():
            dk_acc[...] = jnp.zeros_like(dk_acc)
            dv_acc[...] = jnp.zeros_like(dv_acc)

        q = q_ref[0]
        do = do_ref[...]
        total = jnp.sum(do.astype(F32) * o_ref[...], axis=1, keepdims=True)
        zero = jnp.zeros((ATTN_ROWS, 1), F32)
        blks0, bounds0 = _attn_sweep_start(i, t)

        def finish(starts, a, dz, dq):
            dzb = (dz * scale).astype(BF)
            for u in range(ATTN_UNITS):
                dv_acc[pl.ds(starts[u], ATTN_K), :] += _dot_tn(_unit_rows(a, u), _unit_rows(do, u))
                dk_acc[pl.ds(starts[u], ATTN_K), :] += _dot_tn(_unit_rows(dzb, u), _unit_rows(q, u))
            return dq + _per_unit(lambda u: _dot(_unit_rows(dzb, u), k_ref[0, pl.ds(starts[u], ATTN_K), :]))

        def grad_a(starts, a):
            return _per_unit(lambda u: _dot_nt(_unit_rows(do, u), v_ref[0, pl.ds(starts[u], ATTN_K), :])) * a.astype(F32)

        one_step = jnp.max(reach_ref[...]) <= ATTN_SKIP_BELOW

        @pl.when(one_step)
        def _():
            starts = _attn_keys(blks0)
            a = a_ref[0]
            beta = beta_ref[0].astype(F32)
            de = grad_a(starts, a)
            right, _ = _suffix_sums(de, trii_ref[...], zero)
            dz = de * (1.0 - beta) - (total - right) * beta
            dq_ref[...] = finish(starts, a, dz, jnp.zeros((ATTN_ROWS, HEAD_DIM), F32)).astype(BF)

        @pl.when(jnp.logical_not(one_step))
        def _():
            row = i * ATTN_ROWS + lax.broadcasted_iota(jnp.int32, (ATTN_ROWS, 1), 0)

            def step(carry):
                blks, bounds, dq, seen, run, _ = carry
                starts = _attn_keys(blks)
                z, mask, log_beta, log_rest = _attn_step(q, k_ref, starts, bounds, row)
                tail, run = _suffix_sums(log_rest, tris_ref[...], run)
                a = jnp.where(mask, jnp.exp(log_beta + tail), 0.0).astype(BF)
                de = grad_a(starts, a)
                right, seen = _suffix_sums(de, trii_ref[...], seen)
                beta = jax.nn.sigmoid(z)
                dz = jnp.where(mask, de * (1.0 - beta) - (total - right) * beta, 0.0)
                return (*_attn_next(blks), finish(starts, a, dz, dq), seen, run, jnp.max(run))

            first = (blks0, bounds0, jnp.zeros((ATTN_ROWS, HEAD_DIM), F32), zero, zero, jnp.float32(0.0))
            dq_ref[...] = lax.while_loop(_attn_more, step, step(first))[2].astype(BF)

        @pl.when(i == nq - 1)
        def _():
            dk_ref[...] = dk_acc[...].astype(BF)
            dv_ref[...] = dv_acc[...].astype(BF)

    qspec = pl.BlockSpec((1, ATTN_ROWS, HEAD_DIM), lambda h, i: (3, i, h))
    rowblk = pl.BlockSpec((ATTN_ROWS, HEAD_DIM), lambda h, i: (i, h))
    head = pl.BlockSpec((t, HEAD_DIM), lambda h, i: (0, h))
    trispec = pl.BlockSpec((ATTN_SUB, ATTN_SUB), lambda h, i: (0, 0))
    return _pallas(
        body, name="attn_bwd", grid=(N_HEADS, nq), ride=ride,
        in_specs=[qspec, _head_cols(4)(t), _head_cols(5)(t), rowblk, rowblk, _first_step_spec(), _first_step_spec(), _reach_spec(),
                  trispec, trispec],
        out_specs=[rowblk, head, head],
        out_shape=[S((t, D_MODEL), BF)] * 3,
        scratch_shapes=[pltpu.VMEM((t, HEAD_DIM), F32), pltpu.VMEM((t, HEAD_DIM), F32)],
        args=[proj, proj, proj, o, d_o, a_first, beta_first, reach, tri_strict, tri_incl])


def _tail(h3, n4, p, w_pg, w_pp, g_ple, g_final, target):
    t = h3.shape[0]
    tm = min(TOKEN_TILE, t)
    steps = t // tm

    def body(h_ref, n_ref, p_ref, wpg_ref, wpp_ref, gp_ref, gf_ref, tgt_ref,
             dh_ref, ds_ref, dpp_ref, loss_ref, dgf_ref, dgp_ref):
        pg = jax.nn.sigmoid(_dot(n_ref[...], wpg_ref[...]))
        pp = _dot(p_ref[...].astype(BF), wpp_ref[...])
        h3v = h_ref[...]
        h4 = h3v + pg * pp
        gf = gf_ref[...]
        diff = _rms(h4, gf) - tgt_ref[...]
        _accumulate(loss_ref, jnp.sum(diff * diff, axis=0, keepdims=True))
        dh4, dgf = _rms_bwd(diff * (1.0 / D_MODEL), h4, gf)
        _accumulate(dgf_ref, dgf)
        dpp_ref[...] = (dh4 * pg).astype(BF)
        ds = (dh4 * pp * pg * (1.0 - pg)).astype(BF)
        ds_ref[...] = ds
        dh3, dgp = _rms_bwd(_dot_nt(ds, wpg_ref[...]), h3v, gp_ref[...])
        _accumulate(dgp_ref, dgp)
        dh_ref[...] = dh4 + dh3

        @pl.when(pl.program_id(0) == steps - 1)
        def _():
            loss_ref[...] = jnp.full(loss_ref.shape, 0.5 / D_MODEL * jnp.sum(loss_ref[...]), F32)

    vec = (1, D_MODEL)
    return pl.pallas_call(
        body, name="tail", grid=(steps,),
        in_specs=[_rows(tm, D_MODEL), _rows(tm, D_MODEL), _rows(tm, PLE_DIM), _const_spec((D_MODEL, D_MODEL)),
                  _const_spec((PLE_DIM, D_MODEL)), _const_spec(vec), _const_spec(vec), _rows(tm, D_MODEL)],
        out_specs=[_rows(tm, D_MODEL)] * 3 + [_acc_spec(vec)] * 3,
        out_shape=[S((t, D_MODEL), F32), S((t, D_MODEL), BF), S((t, D_MODEL), BF)] + [S(vec, F32)] * 3,
    )(h3, n4, p, w_pg, w_pp, g_ple, g_final, target)


def _wgrad(xs, ys, name, ride=None):
    bx, t, k = xs.shape
    by, _, n = ys.shape
    b = max(bx, by)
    tt = min(WGRAD_TILE * 2 // xs.dtype.itemsize, t)
    steps = t // tt

    def body(x_ref, y_ref, o_ref, acc_ref):
        s = pl.program_id(1)

        @pl.when(s == 0)
        def _():
            acc_ref[...] = jnp.zeros_like(acc_ref)
        acc_ref[...] += _dot_tn(x_ref[0].astype(BF), y_ref[0].astype(BF))

        @pl.when(s == steps - 1)
        def _():
            o_ref[0] = acc_ref[...].astype(BF)

    (out,), landed = _pallas(
        body, name=name, grid=(b, steps), ride=ride,
        in_specs=[pl.BlockSpec((1, tt, k), (lambda j, s: (j, s, 0)) if bx > 1 else (lambda j, s: (0, s, 0))),
                  pl.BlockSpec((1, tt, n), (lambda j, s: (j, s, 0)) if by > 1 else (lambda j, s: (0, s, 0)))],
        out_specs=[pl.BlockSpec((1, k, n), lambda j, s: (j, 0, 0))],
        out_shape=[S((b, k, n), BF)],
        scratch_shapes=[pltpu.VMEM((k, n), F32)],
        args=[xs, ys])
    return (out, landed) if ride is not None else out


def _wgrad_pieces(x, ys, name, ride=None, tile=None, row_parts=1):
    t, k = x.shape
    n = ys[0].shape[2]
    counts = [y.shape[0] for y in ys]
    offsets = [sum(counts[:j]) for j in range(len(ys))]
    total = sum(counts)
    tt = min(tile or WGRAD_TILE, t)
    steps = t // tt
    kp = k // row_parts

    def body(x_ref, *refs):
        y_refs, o_refs, acc_ref = refs[:len(ys)], refs[len(ys):len(ys) + row_parts], refs[len(ys) + row_parts]
        p, s = pl.program_id(0), pl.program_id(1)

        @pl.when(s == 0)
        def _():
            acc_ref[...] = jnp.zeros_like(acc_ref)
        for j, y_ref in enumerate(y_refs):
            @pl.when(jnp.logical_and(p >= offsets[j], p < offsets[j] + counts[j]))
            def _(y_ref=y_ref):
                acc_ref[...] += _dot_tn(x_ref[...], y_ref[0])

        @pl.when(s == steps - 1)
        def _():
            for part, o_ref in enumerate(o_refs):
                o_ref[0] = acc_ref[part * kp:(part + 1) * kp, :].astype(BF)

    def turn(j):
        lo, hi = offsets[j], offsets[j] + counts[j]
        return lambda p, s: (jnp.clip(p - lo, 0, counts[j] - 1), jnp.where(p < lo, 0, jnp.where(p >= hi, steps - 1, s)), 0)

    outs, landed = _pallas(
        body, name=name, grid=(total, steps), ride=ride,
        in_specs=[pl.BlockSpec((tt, k), lambda p, s: (s, 0))] + [pl.BlockSpec((1, tt, n), turn(j)) for j in range(len(ys))],
        out_specs=[pl.BlockSpec((1, kp, n), lambda p, s: (p, 0, 0))] * row_parts,
        out_shape=[S((total, kp, n), BF)] * row_parts,
        scratch_shapes=[pltpu.VMEM((k, n), F32)],
        args=[x, *ys])
    out = outs[0] if row_parts == 1 else outs
    return (out, landed) if ride is not None else out


def _ffn_bwd_hidden(dh, to_gate, to_up, w_out, name, ride=None):
    t = dh.shape[0]
    tm = min(TOKEN_TILE, t)

    def body(dh_ref, to_gate_ref, to_up_ref, wout_ref, df_ref, dgate_ref, dup_ref):
        df = (0.5 * dh_ref[...]).astype(BF)
        df_ref[...] = df
        for c in range(N_FF_CHUNKS):
            dact = _dot_nt(df, wout_ref[c])
            dgate_ref[c] = (dact * to_gate_ref[c].astype(F32)).astype(BF)
            dup_ref[c] = (dact * to_up_ref[c].astype(F32)).astype(BF)

    return _pallas(
        body, name=name, grid=(t // tm,), ride=ride,
        in_specs=[_rows(tm, D_MODEL), _chunks(tm), _chunks(tm), _const_spec(w_out.shape)],
        out_specs=[_rows(tm, D_MODEL), _chunks(tm), _chunks(tm)],
        out_shape=[S((t, D_MODEL), BF)] + [S((N_FF_CHUNKS, t, FF_CHUNK), BF)] * 2,
        args=[dh, to_gate, to_up, w_out])


def _ffn_bwd_input(dh, h_in, g, dgate, dup, w_in, name, ride=None):
    t = dh.shape[0]
    tm = min(TOKEN_TILE, t)

    def body(dh_ref, h_ref, g_ref, dgate_ref, dup_ref, win_ref, dhi_ref, dg_ref):
        dn = jnp.zeros((tm, D_MODEL), F32)
        for c in range(N_FF_CHUNKS):
            dn = dn + _dot_nt(dgate_ref[c], win_ref[c]) + _dot_nt(dup_ref[c], win_ref[N_FF_CHUNKS + c])
        dhi, dg = _rms_bwd(dn, h_ref[...], g_ref[...])
        _accumulate(dg_ref, dg)
        dhi_ref[...] = dh_ref[...] + dhi

    vec = (1, D_MODEL)
    return _pallas(
        body, name=name, grid=(t // tm,), ride=ride,
        in_specs=[_rows(tm, D_MODEL), _rows(tm, D_MODEL), _const_spec(vec), _chunks(tm), _chunks(tm), _const_spec(w_in.shape)],
        out_specs=[_rows(tm, D_MODEL), _acc_spec(vec)],
        out_shape=[S((t, D_MODEL), F32), S(vec, F32)],
        args=[dh, h_in, g, dgate, dup, w_in])


def _mixer_bwd(dh2, proj, yc, ya, conv_w, w_co, w_ao, w_mo, ride=None):
    t = dh2.shape[0]
    tm = min(TOKEN_TILE, t)

    def body(dh_ref, cb_ref, cc_ref, cx_ref, gc_ref, ga_ref, cch_ref, cxh_ref, yc_ref, ya_ref, cw_ref, wco_ref, wao_ref, wmo_ref,
             dhb_ref, dyc_ref, dya_ref, dgc_ref, dga_ref, dcb_ref, dcv_ref, do_ref):
        dhb = dh_ref[...].astype(BF)
        dhb_ref[...] = dhb
        dmerged = _dot_nt(dhb, wmo_ref[...])
        sc = jax.nn.sigmoid(gc_ref[0].astype(F32))
        sa = jax.nn.sigmoid(ga_ref[0].astype(F32))
        dyc = (dmerged * sc).astype(BF)
        dya = (dmerged * sa).astype(BF)
        dyc_ref[...] = dyc
        dya_ref[...] = dya
        dgc_ref[...] = (dmerged * yc_ref[...].astype(F32) * sc * (1.0 - sc)).astype(BF)
        dga_ref[...] = (dmerged * ya_ref[...].astype(F32) * sa * (1.0 - sa)).astype(BF)
        m, m1, m2 = _conv_inputs(cc_ref, cx_ref, cch_ref, cxh_ref)
        cw = cw_ref[...]
        cv = cw[0:1, :] * m2 + cw[1:2, :] * m1 + cw[2:3, :] * m
        dycin = _dot_nt(dyc, wco_ref[...])
        dcb_ref[...] = (dycin * cv).astype(BF)
        dcv_ref[...] = dycin * cb_ref[0].astype(F32)
        do_ref[...] = _dot_nt(dya, wao_ref[...]).astype(BF)

    sq = (D_MODEL, D_MODEL)
    return _pallas(
        body, name="mixer_bwd", grid=(t // tm,), ride=ride,
        in_specs=[_rows(tm, D_MODEL), _piece(0, tm), _piece(1, tm), _piece(2, tm), _piece(6, tm), _piece(7, tm),
                  _prev_halo(1, tm), _prev_halo(2, tm), _rows(tm, D_MODEL), _rows(tm, D_MODEL),
                  _const_spec((3, D_MODEL)), _const_spec(sq), _const_spec(sq), _const_spec(sq)],
        out_specs=[_rows(tm, D_MODEL)] * 8,
        out_shape=[S((t, D_MODEL), BF)] * 6 + [S((t, D_MODEL), F32), S((t, D_MODEL), BF)],
        args=[dh2, proj, proj, proj, proj, proj, proj, proj, yc, ya, conv_w, w_co, w_ao, w_mo])


F32_HALO = 8


def _conv_bwd(dcv, proj, conv_w):
    t = dcv.shape[0]
    tm = min(TOKEN_TILE, t)
    steps = t // tm

    def body(dcv_ref, nxt_ref, cc_ref, cx_ref, cch_ref, cxh_ref, cw_ref, dcc_ref, dcx_ref, dw_ref):
        i = pl.program_id(0)
        m, m1, m2 = _conv_inputs(cc_ref, cx_ref, cch_ref, cxh_ref)
        d0 = dcv_ref[...]
        nxt = jnp.where(i == steps - 1, 0.0, nxt_ref[...])
        row = lax.broadcasted_iota(jnp.int32, (tm, 1), 0)
        d1 = jnp.where(row == tm - 1, nxt[0:1, :], pltpu.roll(d0, tm - 1, 0))
        d2 = pltpu.roll(d0, tm - 2, 0)
        d2 = jnp.where(row == tm - 2, nxt[0:1, :], jnp.where(row == tm - 1, nxt[1:2, :], d2))
        cw = cw_ref[...]
        dm = cw[2:3, :] * d0 + cw[1:2, :] * d1 + cw[0:1, :] * d2
        dcc_ref[...] = (dm * cx_ref[0].astype(F32)).astype(BF)
        dcx_ref[...] = (dm * cc_ref[0].astype(F32)).astype(BF)
        tap_row = lax.broadcasted_iota(jnp.int32, (F32_HALO, 1), 0)
        dw = jnp.zeros((F32_HALO, D_MODEL), F32)
        for j, mk in enumerate((m2, m1, m)):
            dw = jnp.where(tap_row == j, jnp.sum(d0 * mk, axis=0, keepdims=True), dw)
        _accumulate(dw_ref, dw)

    nxt_spec = pl.BlockSpec((F32_HALO, D_MODEL), lambda i: (jnp.minimum((i + 1) * (tm // F32_HALO), t // F32_HALO - 1), 0))
    return pl.pallas_call(
        body, name="conv_bwd", grid=(steps,),
        in_specs=[_rows(tm, D_MODEL), nxt_spec, _piece(1, tm), _piece(2, tm), _prev_halo(1, tm), _prev_halo(2, tm),
                  _const_spec((3, D_MODEL))],
        out_specs=[_rows(tm, D_MODEL), _rows(tm, D_MODEL), _acc_spec((F32_HALO, D_MODEL))],
        out_shape=[S((t, D_MODEL), BF), S((t, D_MODEL), BF), S((F32_HALO, D_MODEL), F32)],
    )(dcv, dcv, proj, proj, proj, proj, conv_w)


def _mix_bwd(dpieces, w_mix, h1, dh2, g, ride=None):
    t = h1.shape[0]
    tm = min(TOKEN_TILE, t)

    def body(*refs):
        pieces, (w_ref, h_ref, dh_ref, g_ref, dhi_ref, dg_ref) = refs[:N_MIX], refs[N_MIX:]
        du = jnp.zeros((tm, D_MODEL), F32)
        for d in range(N_MIX):
            du = du + _dot_nt(pieces[d][...], w_ref[d])
        dhi, dg = _rms_bwd(du, h_ref[...], g_ref[...])
        _accumulate(dg_ref, dg)
        dhi_ref[...] = dh_ref[...] + dhi

    vec = (1, D_MODEL)
    return _pallas(
        body, name="mix_bwd", grid=(t // tm,), ride=ride,
        in_specs=[_rows(tm, D_MODEL)] * N_MIX + [_const_spec(w_mix.shape), _rows(tm, D_MODEL), _rows(tm, D_MODEL), _const_spec(vec)],
        out_specs=[_rows(tm, D_MODEL), _acc_spec(vec)],
        out_shape=[S((t, D_MODEL), F32), S(vec, F32)],
        args=[*dpieces, w_mix, h1, dh2, g])


def _adamw(partials, w, m, v, name):
    parts = list(partials) if isinstance(partials, (list, tuple)) else [partials]
    r, c = w.shape
    tr = min(r, 512 // len(parts))
    first_tile = [sum(p.shape[1] for p in parts[:j]) // tr for j in range(len(parts))]
    c1 = 1.0 - ADAM_B1 ** ADAM_STEP
    c2 = 1.0 - ADAM_B2 ** ADAM_STEP

    def body(*refs):
        p_refs, (w_ref, m_ref, v_ref, g_ref, d_ref, mo_ref, vo_ref) = refs[:len(parts)], refs[len(parts):]
        g = None
        for j, p_ref in enumerate(p_refs):
            gj = p_ref[0].astype(F32)
            for s in range(1, N_SHARDS):
                gj = gj + p_ref[s].astype(F32)
            g = gj if g is None else jnp.where(pl.program_id(0) >= first_tile[j], gj, g)
        mn = ADAM_B1 * m_ref[...] + (1.0 - ADAM_B1) * g
        vn = ADAM_B2 * v_ref[...] + (1.0 - ADAM_B2) * (g * g)
        g_ref[...] = g
        mo_ref[...] = mn
        vo_ref[...] = vn
        d_ref[...] = -ADAM_LR * ((mn / c1) / (jnp.sqrt(vn / c2) + ADAM_EPS) + ADAM_WD * w_ref[...])

    def rows_of(j):
        last = parts[j].shape[1] // tr - 1
        return lambda i: (0, jnp.clip(i - first_tile[j], 0, last), 0)

    blk = pl.BlockSpec((tr, c), lambda i: (i, 0))
    return pl.pallas_call(
        body, name=name, grid=(r // tr,),
        in_specs=[pl.BlockSpec((N_SHARDS, tr, c), rows_of(j)) for j in range(len(parts))] + [blk, blk, blk],
        out_specs=[blk] * 4, out_shape=[S((r, c), F32)] * 4,
    )(*parts, w, m, v)


_MATRICES = ("ffn1_w_in", "ffn1_w_out", "w_mix_in", "conv_w", "w_conv_out", "w_attn_out", "w_mix_out",
             "ffn2_w_in", "ffn2_w_out", "w_ple_gate", "w_ple_proj")
_GAINS = ("ffn1_norm", "mix_norm", "ffn2_norm", "ple_norm", "final_norm")
_WEIGHTS = ("ffn1_norm", "ffn1_w_in", "ffn1_w_out", "mix_norm", "w_mix_in", "conv_w", "w_conv_out", "w_attn_out", "w_mix_out",
            "ffn2_norm", "ffn2_w_in", "ffn2_w_out", "ple_norm", "w_ple_gate", "w_ple_proj", "final_norm")
CONV_ROWS = 8


def _columns_from_shards(g):
    return jnp.transpose(g, (1, 0, 2)).reshape(g.shape[1], N_SHARDS * g.shape[2])


def _shards_from_columns(a):
    r, c = a.shape
    return jnp.transpose(a.reshape(r, N_SHARDS, c // N_SHARDS), (1, 0, 2))


def kernel(x, p, ffn1_norm, ffn1_w_in, ffn1_w_out, mix_norm, w_mix_in, conv_w, w_conv_out, w_attn_out, w_mix_out, ffn2_norm, ffn2_w_in, ffn2_w_out, ple_norm, w_ple_gate, w_ple_proj, final_norm, loss_target, m_ffn1_norm, m_ffn1_w_in, m_ffn1_w_out, m_mix_norm, m_w_mix_in, m_conv_w, m_w_conv_out, m_w_attn_out, m_w_mix_out, m_ffn2_norm, m_ffn2_w_in, m_ffn2_w_out, m_ple_norm, m_w_ple_gate, m_w_ple_proj, m_final_norm, v_ffn1_norm, v_ffn1_w_in, v_ffn1_w_out, v_mix_norm, v_w_mix_in, v_conv_w, v_w_conv_out, v_w_attn_out, v_w_mix_out, v_ffn2_norm, v_ffn2_w_in, v_ffn2_w_out, v_ple_norm, v_w_ple_gate, v_w_ple_proj, v_final_norm):
    given = dict(locals())
    t = x.shape[1]
    xs = x.reshape(t, D_MODEL)
    ps = p.reshape(t, PLE_DIM)
    target = loss_target.reshape(t, D_MODEL)
    shard = {k: given[k].reshape(given[k].shape[-2:]) for k in _MATRICES}
    gain = {k: given[k].reshape(1, D_MODEL) for k in _GAINS}

    send = {k: shard[k].astype(BF) for k in _MATRICES}
    send["conv_w"] = jnp.pad(shard["conv_w"], ((0, CONV_ROWS - 3), (0, 0)))
    loss_vec, dx, landed, gain_grads = _forward_backward(xs, ps, target, gain, send)
    gain_rows = jnp.concatenate([gain_grads[k] for k in _GAINS] + [jnp.zeros((8 - len(_GAINS), D_MODEL), F32)], axis=0)
    gain_parts, = _exchange_alone("gather", [gain_rows], "gather_gain_gradients")

    out = {}
    for k in _MATRICES:
        w, m, v = shard[k], given["m_" + k].reshape(shard[k].shape), given["v_" + k].reshape(shard[k].shape)
        part = landed[k]
        if k == "conv_w":
            pad = ((0, CONV_ROWS - 3), (0, 0))
            w, m, v = jnp.pad(w, pad), jnp.pad(m, pad), jnp.pad(v, pad, constant_values=1.0)
        res = _adamw(part, w, m, v, "adamw_" + k)
        out[k] = [r[:3] if k == "conv_w" else r for r in res]
    stack = lambda pre: jnp.concatenate([given[pre + k].reshape(1, D_MODEL) for k in _GAINS] + [jnp.ones((8 - len(_GAINS), D_MODEL), F32)], axis=0)
    res = _adamw(gain_parts, stack(""), stack("m_"), stack("v_"), "adamw_gains")
    for j, k in enumerate(_GAINS):
        out[k] = [r[j:j + 1] for r in res]

    loss = lax.psum(loss_vec[0, 0], ("x", "y", "c"))
    per_kind = [[out[k][j].reshape(given[k].shape) for k in _WEIGHTS] for j in range(4)]
    return (loss, dx.reshape(x.shape), *per_kind[0], *per_kind[1], *per_kind[2], *per_kind[3])


def _forward_backward(xs, ps, target, gain, send, full=None):
    exchange = full is None
    full = dict(full or {})
    grads, landed = {}, {}

    def gather(names):
        return ("gather", [send[k] for k in names]) if exchange else None

    def scatter(names):
        return ("scatter", [grads[k] for k in names]) if exchange else None

    def keep(into, names, got):
        into.update(zip(names, got))

    first = ("ffn1_w_in",)
    (n1,), got = _prenorm(xs, gain["ffn1_norm"], ride=gather(first))
    keep(full, first, got)
    w1_in = full["ffn1_w_in"]
    second = ("ffn1_w_out", "w_mix_in")
    (act1, to_gate1, to_up1), got = _ffn_up(n1, w1_in, "ffn1_up", ride=gather(second))
    keep(full, second, got)
    w1_out = full["ffn1_w_out"].reshape(N_FF_CHUNKS, FF_CHUNK, D_MODEL)
    third = ("conv_w", "w_conv_out", "w_attn_out", "w_mix_out")
    (h1, u), got = _ffn_down(xs, act1, w1_out, gain["mix_norm"], "ffn1_down", ride=gather(third))
    keep(full, third, got)
    w_mix = full["w_mix_in"]
    w_co, w_ao, w_mo = (full[k].reshape(D_MODEL, D_MODEL) for k in ("w_conv_out", "w_attn_out", "w_mix_out"))
    taps = _columns_from_shards(full["conv_w"][:, :3, :])
    rest = ("ffn2_w_in", "ffn2_w_out", "w_ple_gate", "w_ple_proj")
    (proj,), got = _mix_proj(u, w_mix, ride=gather(rest))
    keep(full, rest, got)
    w2_in, w2_out = full["ffn2_w_in"], full["ffn2_w_out"].reshape(N_FF_CHUNKS, FF_CHUNK, D_MODEL)
    w_pg = full["w_ple_gate"].reshape(D_MODEL, D_MODEL)
    w_pp = _columns_from_shards(full["w_ple_proj"])
    o, a_first, beta_first, reach = _attn_fwd(proj)
    h2, n3, ycin, yc, ya, merged = _mixer_out(proj, o, h1, taps, w_co, w_ao, w_mo, gain["ffn2_norm"])
    (act2, to_gate2, to_up2), _ = _ffn_up(n3, w2_in, "ffn2_up")
    (h3, n4), _ = _ffn_down(h2, act2, w2_out, gain["ple_norm"], "ffn2_down")
    dh3, ds, dpp, loss_vec, dg_final, dg_ple = _tail(h3, n4, ps, w_pg, w_pp, gain["ple_norm"], gain["final_norm"], target)

    one = lambda a: a[None]
    by_rows = lambda g, rows: g.reshape(N_SHARDS, rows // N_SHARDS, D_MODEL)
    grads["w_ple_gate"] = by_rows(_wgrad(one(n4), one(ds), "wgrad_ple_gate"), D_MODEL)
    grads["w_ple_proj"] = _shards_from_columns(_wgrad(one(ps), one(dpp), "wgrad_ple_proj")[0])
    ple = ("w_ple_gate", "w_ple_proj")
    (df2, dgate2, dup2), got = _ffn_bwd_hidden(dh3, to_gate2, to_up2, w2_out, "ffn2_bwd_hidden", ride=scatter(ple))
    keep(landed, ple, got)
    grads["ffn2_w_out"] = by_rows(_wgrad(act2, one(df2), "wgrad_ffn2_out"), D_FF)
    grads["ffn2_w_in"] = _wgrad_pieces(n3, [dgate2, dup2], "wgrad_ffn2_in")
    (dh2, dg_ffn2), got = _ffn_bwd_input(dh3, h2, gain["ffn2_norm"], dgate2, dup2, w2_in, "ffn2_bwd_input", ride=scatter(("ffn2_w_out",)))
    keep(landed, ("ffn2_w_out",), got)
    (dh2b, dyc, dya, dgc, dga, dcb, dcv, d_o), _ = _mixer_bwd(dh2, proj, yc, ya, taps, w_co, w_ao, w_mo)
    grads["w_mix_out"] = by_rows(_wgrad(one(merged), one(dh2b), "wgrad_mix_out"), D_MODEL)
    grads["w_conv_out"] = by_rows(_wgrad(one(ycin), one(dyc), "wgrad_conv_out"), D_MODEL)
    grads["w_attn_out"] = by_rows(_wgrad(one(o), one(dya), "wgrad_attn_out"), D_MODEL)
    dcc, dcx, dtaps = _conv_bwd(dcv, proj, taps)
    grads["conv_w"] = jnp.pad(_shards_from_columns(dtaps[:3]), ((0, 0), (0, CONV_ROWS - 3), (0, 0)))
    behind_attn = ("ffn2_w_in", "w_mix_out", "w_conv_out", "w_attn_out", "conv_w")
    (dq, dk, dv), got = _attn_bwd(proj, o, d_o, a_first, beta_first, reach, ride=scatter(behind_attn))
    keep(landed, behind_attn, got)
    dpieces = [dcb, dcc, dcx, dq, dk, dv, dgc, dga]
    half = N_MIX // 2
    tops, bottoms = zip(_wgrad_pieces(u, [one(dp) for dp in dpieces[:half]], "wgrad_mix_in_a", tile=WGRAD_TILE // 2, row_parts=2),
                        _wgrad_pieces(u, [one(dp) for dp in dpieces[half:]], "wgrad_mix_in_b", tile=WGRAD_TILE // 2, row_parts=2))
    grads["w_mix_in top"], grads["w_mix_in bottom"] = jnp.concatenate(tops, axis=0), jnp.concatenate(bottoms, axis=0)
    (dh1, dg_mix), top = _mix_bwd(dpieces, w_mix, h1, dh2, gain["mix_norm"], ride=scatter(("w_mix_in top",)))
    (df1, dgate1, dup1), bottom = _ffn_bwd_hidden(dh1, to_gate1, to_up1, w1_out, "ffn1_bwd_hidden", ride=scatter(("w_mix_in bottom",)))
    if exchange:
        landed["w_mix_in"] = [top[0], bottom[0]]
    else:
        grads["w_mix_in"] = jnp.concatenate([grads.pop("w_mix_in top"), grads.pop("w_mix_in bottom")], axis=1)
    grads["ffn1_w_out"] = by_rows(_wgrad(act1, one(df1), "wgrad_ffn1_out"), D_FF)
    if exchange:
        grads["ffn1_w_in"], got = _wgrad_pieces(n1, [dgate1, dup1], "wgrad_ffn1_in", ride=scatter(("ffn1_w_out",)))
        keep(landed, ("ffn1_w_out",), got)
    else:
        grads["ffn1_w_in"] = _wgrad_pieces(n1, [dgate1, dup1], "wgrad_ffn1_in")
    (dx, dg_ffn1), got = _ffn_bwd_input(dh1, xs, gain["ffn1_norm"], dgate1, dup1, w1_in, "ffn1_bwd_input", ride=scatter(("ffn1_w_in",)))
    keep(landed, ("ffn1_w_in",), got)
    gain_grads = dict(ffn1_norm=dg_ffn1, mix_norm=dg_mix, ffn2_norm=dg_ffn2, ple_norm=dg_ple, final_norm=dg_final)
    return loss_vec, dx, (landed if exchange else grads), gain_grads
```

```python
import functools
import math

import jax
import jax.numpy as jnp
from jax import lax
from jax.experimental import pallas as pl
from jax.experimental.pallas import tpu as pltpu

D_MODEL = 1024
D_FF = 2816
N_SHARDS = 8
FF_CHUNK = 2 * D_FF // N_SHARDS
N_FF_CHUNKS = D_FF // FF_CHUNK
N_HEADS = 8
HEAD_DIM = 128
PLE_DIM = 256
NORM_EPS = 1e-6
N_MIX = 8
ADAM_LR, ADAM_B1, ADAM_B2, ADAM_EPS, ADAM_WD, ADAM_STEP = 0.001, 0.9, 0.999, 1e-08, 0.01, 10

TOKEN_TILE = 512
WGRAD_TILE = 4096
PROJ_TILE = 2048
ATTN_ROWS = 512
ATTN_Q = 128
ATTN_SUB = 128
ATTN_K = 3 * ATTN_SUB
ATTN_SKIP_BELOW = -90.0

BF = jnp.bfloat16
F32 = jnp.float32
MESH = pl.DeviceIdType.MESH
NT = (((1,), (1,)), ((), ()))
TN = (((0,), (0,)), ((), ()))
S = jax.ShapeDtypeStruct
ANY = pl.BlockSpec(memory_space=pl.ANY)


def _const_spec(shape):
    nd = len(shape)
    return pl.BlockSpec(shape, lambda *_: (0,) * nd, pipeline_mode=pl.Buffered(1))


def _rows(tm, cols):
    return pl.BlockSpec((tm, cols), lambda i: (i, 0))


def _chunks(tm):
    return pl.BlockSpec((N_FF_CHUNKS, tm, FF_CHUNK), lambda i: (0, i, 0))


def _acc_spec(shape):
    nd = len(shape)
    return pl.BlockSpec(shape, lambda *_: (0,) * nd)


def _dot(a, b):
    return jnp.dot(a, b, preferred_element_type=F32)


def _dot_nt(a, b):
    return lax.dot_general(a, b, NT, preferred_element_type=F32)


def _dot_tn(a, b):
    return lax.dot_general(a, b, TN, preferred_element_type=F32)


def _rms(h, g):
    r = lax.rsqrt(jnp.mean(h * h, axis=-1, keepdims=True) + NORM_EPS)
    return h * r * g


def _rms_bwd(dn, h, g):
    r = lax.rsqrt(jnp.mean(h * h, axis=-1, keepdims=True) + NORM_EPS)
    nh = h * r
    gd = dn * g
    dh = r * (gd - nh * jnp.mean(gd * nh, axis=-1, keepdims=True))
    return dh, jnp.sum(dn * nh, axis=0, keepdims=True)


def _accumulate(ref, val):
    @pl.when(pl.program_id(0) == 0)
    def _():
        ref[...] = jnp.zeros_like(ref)
    ref[...] += val


def _place():
    x, y, c = lax.axis_index("x"), lax.axis_index("y"), lax.axis_index("c")
    return x, y, c


def _slot(px, py, pc):
    return 4 * px + 2 * py + pc


def _gather_phases(ins, outs, send_sems, recv_sems, local_sems):
    n = len(ins)

    def parties():
        x, y, c = _place()
        return (x, y, c), (x, y, 1 - c), [(1 - x, y), (x, 1 - y), (1 - x, 1 - y)], c

    def copy(a, k, block, to, src=None):
        dst = outs[a].at[_slot(*block)]
        return pltpu.make_async_remote_copy(
            src_ref=dst if src is None else src, dst_ref=dst,
            send_sem=send_sems.at[a, k], recv_sem=recv_sems.at[a, k],
            device_id=to, device_id_type=MESH)

    def own(a, me):
        return pltpu.make_async_copy(ins[a], outs[a].at[_slot(*me)], local_sems.at[a])

    def first(a, me, sibling, chips, c):
        return [copy(a, 0, me, sibling, src=ins[a])] + [copy(a, 1 + j, me, (*chip, c), src=ins[a]) for j, chip in enumerate(chips)]

    def start():
        me, sibling, chips, c = parties()
        for a in range(n):
            own(a, me).start()
        for a in range(n):
            for cp in first(a, me, sibling, chips, c):
                cp.start()

    def forward():
        me, sibling, chips, c = parties()
        for j, chip in enumerate(chips):
            for a in range(n):
                copy(a, 1 + j, (*chip, c), me).wait_recv()
                copy(a, 4 + j, (*chip, c), sibling).start()

    def finish():
        me, sibling, chips, c = parties()
        for a in range(n):
            copy(a, 0, sibling, me).wait_recv()
            for j, chip in enumerate(chips):
                copy(a, 4 + j, (*chip, 1 - c), me).wait_recv()
        for a in range(n):
            for cp in first(a, me, sibling, chips, c) + [copy(a, 4 + j, (*chip, c), sibling) for j, chip in enumerate(chips)]:
                cp.wait_send()
            own(a, me).wait()

    return [start, forward, finish]


def _scatter_phases(ins, outs, send_sems, recv_sems, local_sems):
    n = len(ins)

    def copies():
        x, y, c = _place()
        me = _slot(x, y, c)
        out = [pltpu.make_async_copy(ins[a].at[me], outs[a].at[me], local_sems.at[a]) for a in range(n)]
        for k in range(1, N_SHARDS):
            px = 1 - x if k & 4 else x
            py = 1 - y if k & 2 else y
            pc = 1 - c if k & 1 else c
            for a in range(n):
                out.append(pltpu.make_async_remote_copy(
                    src_ref=ins[a].at[_slot(px, py, pc)], dst_ref=outs[a].at[me],
                    send_sem=send_sems.at[a, k - 1], recv_sem=recv_sems.at[a, k - 1],
                    device_id=(px, py, pc), device_id_type=MESH))
        return out

    def start():
        for cp in copies():
            cp.start()

    def finish():
        for cp in copies():
            cp.wait()

    return [start, finish]


def _pallas(body, *, name, grid, in_specs, out_specs, out_shape, args, scratch_shapes=(), ride=None):
    if ride is None:
        outs = pl.pallas_call(body, name=name, grid=grid, in_specs=in_specs, out_specs=out_specs, out_shape=out_shape,
                              scratch_shapes=list(scratch_shapes))(*args)
        return list(outs), []
    kind, arrays = ride
    n, n_in, n_out, n_scr = len(arrays), len(in_specs), len(out_specs), len(scratch_shapes)
    total = math.prod(grid)
    middle = (9 * total) // 10
    landed_shape = [S((N_SHARDS,) + a.shape if kind == "gather" else a.shape, a.dtype) for a in arrays]

    def with_exchange(*refs):
        ins, riders_in = refs[:n_in], refs[n_in:n_in + n]
        outs, riders_out = refs[n_in + n:n_in + n + n_out], refs[n_in + n + n_out:n_in + 2 * n + n_out]
        scratch, sems = refs[n_in + 2 * n + n_out:n_in + 2 * n + n_out + n_scr], refs[n_in + 2 * n + n_out + n_scr:]
        step = 0
        for axis, size in enumerate(grid):
            step = step * size + pl.program_id(axis)
        phases = (_gather_phases if kind == "gather" else _scatter_phases)(riders_in, riders_out, *sems)
        pl.when(step == 0)(phases[0])
        body(*ins, *outs, *scratch)
        for phase in phases[1:-1]:
            pl.when(step == middle)(phase)
        pl.when(step == total - 1)(phases[-1])

    outs = pl.pallas_call(
        with_exchange, name=name, grid=grid,
        in_specs=list(in_specs) + [ANY] * n, out_specs=list(out_specs) + [ANY] * n,
        out_shape=list(out_shape) + landed_shape,
        scratch_shapes=list(scratch_shapes) + [pltpu.SemaphoreType.DMA((n, 7)), pltpu.SemaphoreType.DMA((n, 7)),
                                               pltpu.SemaphoreType.DMA((n,))],
    )(*args, *arrays)
    return list(outs[:n_out]), list(outs[n_out:])


def _exchange_alone(kind, arrays, name):
    return _pallas(lambda: None, name=name, grid=(1,), in_specs=[], out_specs=[], out_shape=[], args=[], ride=(kind, arrays))[1]


def _prenorm(x, g, ride=None):
    t = x.shape[0]
    tm = min(TOKEN_TILE, t)

    def body(x_ref, g_ref, n_ref):
        n_ref[...] = _rms(x_ref[...], g_ref[...]).astype(BF)

    return _pallas(
        body, name="prenorm", grid=(t // tm,), ride=ride,
        in_specs=[_rows(tm, D_MODEL), _const_spec((1, D_MODEL))], out_specs=[_rows(tm, D_MODEL)],
        out_shape=[S((t, D_MODEL), BF)], args=[x, g])


def _ffn_up(n, w_in, name, ride=None):
    t = n.shape[0]
    tm = min(TOKEN_TILE, t)

    def body(n_ref, win_ref, act_ref, to_gate_ref, to_up_ref):
        nb = n_ref[...]
        for c in range(N_FF_CHUNKS):
            gate = _dot(nb, win_ref[c])
            up = _dot(nb, win_ref[N_FF_CHUNKS + c])
            sg = jax.nn.sigmoid(gate)
            silu = gate * sg
            act_ref[c] = (silu * up).astype(BF)
            to_gate_ref[c] = (up * (sg * (1.0 + gate * (1.0 - sg)))).astype(BF)
            to_up_ref[c] = silu.astype(BF)

    return _pallas(
        body, name=name, grid=(t // tm,), ride=ride,
        in_specs=[_rows(tm, D_MODEL), _const_spec(w_in.shape)],
        out_specs=[_chunks(tm)] * 3, out_shape=[S((N_FF_CHUNKS, t, FF_CHUNK), BF)] * 3,
        args=[n, w_in])


def _ffn_down(h, act, w_out, g_next, name, ride=None):
    t = h.shape[0]
    tm = min(TOKEN_TILE, t)

    def body(h_ref, act_ref, wout_ref, g_ref, ho_ref, no_ref):
        acc = jnp.zeros((tm, D_MODEL), F32)
        for c in range(N_FF_CHUNKS):
            acc = acc + _dot(act_ref[c], wout_ref[c])
        ho = h_ref[...] + 0.5 * acc
        ho_ref[...] = ho
        no_ref[...] = _rms(ho, g_ref[...]).astype(BF)

    return _pallas(
        body, name=name, grid=(t // tm,), ride=ride,
        in_specs=[_rows(tm, D_MODEL), _chunks(tm), _const_spec(w_out.shape), _const_spec((1, D_MODEL))],
        out_specs=[_rows(tm, D_MODEL)] * 2, out_shape=[S((t, D_MODEL), F32), S((t, D_MODEL), BF)],
        args=[h, act, w_out, g_next])


def _mix_proj(u, w_mix, ride=None):
    t = u.shape[0]
    tm = min(PROJ_TILE, t)

    def body(u_ref, w_ref, o_ref):
        o_ref[0] = _dot(u_ref[...], w_ref[0]).astype(BF)

    return _pallas(
        body, name="mix_proj", grid=(N_MIX, t // tm), ride=ride,
        in_specs=[pl.BlockSpec((tm, D_MODEL), lambda d, i: (i, 0)), pl.BlockSpec((1, D_MODEL, D_MODEL), lambda d, i: (d, 0, 0))],
        out_specs=[pl.BlockSpec((1, tm, D_MODEL), lambda d, i: (d, i, 0))],
        out_shape=[S((N_MIX, t, D_MODEL), BF)], args=[u, w_mix])


HALO = 16


def _piece(d, tm):
    return pl.BlockSpec((1, tm, D_MODEL), lambda i: (d, i, 0))


def _prev_halo(d, tm):
    return pl.BlockSpec((1, HALO, D_MODEL), lambda i: (d, jnp.maximum(i * (tm // HALO) - 1, 0), 0))


def _shift_down(m, prev_tail, k):
    tm = m.shape[0]
    out = pltpu.roll(m, k, 0)
    row = lax.broadcasted_iota(jnp.int32, (tm, 1), 0)
    for j in range(k):
        out = jnp.where(row == j, prev_tail[HALO - k + j:HALO - k + j + 1, :], out)
    return out


def _conv_inputs(cc_ref, cx_ref, cch_ref, cxh_ref):
    m = cc_ref[0].astype(F32) * cx_ref[0].astype(F32)
    mh = cch_ref[0].astype(F32) * cxh_ref[0].astype(F32)
    mh = jnp.where(pl.program_id(0) == 0, 0.0, mh)
    return m, _shift_down(m, mh, 1), _shift_down(m, mh, 2)


def _mixer_out(proj, o, h1, conv_w, w_co, w_ao, w_mo, g_next):
    t = h1.shape[0]
    tm = min(TOKEN_TILE, t)

    def body(cb_ref, cc_ref, cx_ref, gc_ref, ga_ref, cch_ref, cxh_ref, o_ref, h_ref, cw_ref, wco_ref, wao_ref, wmo_ref,
             g_ref, ho_ref, no_ref, ycin_ref, yc_ref, ya_ref, mg_ref):
        m, m1, m2 = _conv_inputs(cc_ref, cx_ref, cch_ref, cxh_ref)
        cw = cw_ref[...]
        cv = cw[0:1, :] * m2 + cw[1:2, :] * m1 + cw[2:3, :] * m
        ycin = (cb_ref[0].astype(F32) * cv).astype(BF)
        ycin_ref[...] = ycin
        yc = _dot(ycin, wco_ref[...])
        ya = _dot(o_ref[...].astype(BF), wao_ref[...])
        yc_ref[...] = yc.astype(BF)
        ya_ref[...] = ya.astype(BF)
        merged = (jax.nn.sigmoid(gc_ref[0].astype(F32)) * yc + jax.nn.sigmoid(ga_ref[0].astype(F32)) * ya).astype(BF)
        mg_ref[...] = merged
        ho = h_ref[...] + _dot(merged, wmo_ref[...])
        ho_ref[...] = ho
        no_ref[...] = _rms(ho, g_ref[...]).astype(BF)

    sq = (D_MODEL, D_MODEL)
    return pl.pallas_call(
        body, name="mixer_out", grid=(t // tm,),
        in_specs=[_piece(0, tm), _piece(1, tm), _piece(2, tm), _piece(6, tm), _piece(7, tm), _prev_halo(1, tm), _prev_halo(2, tm),
                  _rows(tm, D_MODEL), _rows(tm, D_MODEL), _const_spec((3, D_MODEL)), _const_spec(sq), _const_spec(sq),
                  _const_spec(sq), _const_spec((1, D_MODEL))],
        out_specs=[_rows(tm, D_MODEL)] * 6,
        out_shape=[S((t, D_MODEL), F32)] + [S((t, D_MODEL), BF)] * 5,
    )(proj, proj, proj, proj, proj, proj, proj, o, h1, conv_w, w_co, w_ao, w_mo, g_next)


def _suffix_sums(vals, tri, before):
    out, right = [], before
    for b in reversed(range(ATTN_K // ATTN_SUB)):
        v = vals[:, b * ATTN_SUB:(b + 1) * ATTN_SUB]
        hi = v.astype(BF)
        lo = (v - hi.astype(F32)).astype(BF)
        out.append(_dot(hi, tri) + _dot(lo, tri) + right)
        right = right + jnp.sum(v, axis=1, keepdims=True)
    return jnp.concatenate(out[::-1], axis=1), right


ATTN_UNITS = ATTN_ROWS // ATTN_Q


def _unit_rows(x, u):
    return x[u * ATTN_Q:(u + 1) * ATTN_Q]


def _per_unit(fn):
    return jnp.concatenate([fn(u) for u in range(ATTN_UNITS)], axis=0)


def _per_row(vals):
    local = lax.broadcasted_iota(jnp.int32, (ATTN_ROWS, 1), 0)
    out = jnp.full((ATTN_ROWS, 1), vals[0], jnp.int32)
    for u in range(1, ATTN_UNITS):
        out = jnp.where(local >= u * ATTN_Q, vals[u], out)
    return out


def _attn_step(q, k_ref, starts, bounds, row):
    z = _per_unit(lambda u: _dot_nt(_unit_rows(q, u), k_ref[0, pl.ds(starts[u], ATTN_K), :])) * (1.0 / math.sqrt(HEAD_DIM))
    mask = lax.broadcasted_iota(jnp.int32, (1, ATTN_K), 1) < jnp.minimum(row, _per_row(bounds)) - _per_row(starts)
    log_beta = jnp.minimum(z, 0.0) - jnp.log(1.0 + jnp.exp(jnp.minimum(z, -z)))
    log_rest = jnp.where(mask, log_beta - z, 0.0)
    return z, mask, log_beta, log_rest


def _attn_sweep_start(i, t):
    blks = tuple(jnp.maximum(i * ATTN_UNITS + u + 1 - ATTN_K // ATTN_SUB, 0) for u in range(ATTN_UNITS))
    return blks, tuple(jnp.int32(t) for _ in range(ATTN_UNITS))


def _attn_keys(blks):
    return [pl.multiple_of(b * ATTN_SUB, ATTN_SUB) for b in blks]


def _attn_next(blks):
    return tuple(jnp.maximum(b - ATTN_K // ATTN_SUB, 0) for b in blks), tuple(b * ATTN_SUB for b in blks)


def _attn_more(carry):
    return jnp.logical_and(carry[1][ATTN_UNITS - 1] > 0, carry[-1] > ATTN_SKIP_BELOW)


def _tri(strict):
    r = lax.broadcasted_iota(jnp.int32, (ATTN_SUB, ATTN_SUB), 0)
    c = lax.broadcasted_iota(jnp.int32, (ATTN_SUB, ATTN_SUB), 1)
    return (r > c if strict else r >= c).astype(BF)


REACH_TILE = (8, 128)


def _first_step_spec():
    return pl.BlockSpec((1, ATTN_ROWS, ATTN_K), lambda h, i: (h, i, 0))


def _reach_spec():
    return pl.BlockSpec((1, 1) + REACH_TILE, lambda h, i: (h, i, 0, 0))


def _head_cols(piece):
    return lambda t: pl.BlockSpec((1, t, HEAD_DIM), lambda h, i: (piece, 0, h))


def _attn_fwd(proj):
    t = proj.shape[1]
    nq = t // ATTN_ROWS
    tri = _tri(strict=True)

    def body(q_ref, k_ref, v_ref, tri_ref, o_ref, a_ref, beta_ref, reach_ref):
        i = pl.program_id(1)
        q = q_ref[0]
        row = i * ATTN_ROWS + lax.broadcasted_iota(jnp.int32, (ATTN_ROWS, 1), 0)

        def step(carry, keep=False):
            blks, bounds, acc, run, _ = carry
            starts = _attn_keys(blks)
            _, mask, log_beta, log_rest = _attn_step(q, k_ref, starts, bounds, row)
            tail, run = _suffix_sums(log_rest, tri_ref[...], run)
            a = jnp.where(mask, jnp.exp(log_beta + tail), 0.0).astype(BF)
            if keep:
                a_ref[0] = a
                beta_ref[0] = jnp.where(mask, jnp.exp(log_beta), 0.0).astype(BF)
            acc = acc + _per_unit(lambda u: _dot(_unit_rows(a, u), v_ref[0, pl.ds(starts[u], ATTN_K), :]))
            return (*_attn_next(blks), acc, run, jnp.max(run))

        first = (*_attn_sweep_start(i, t), jnp.zeros((ATTN_ROWS, HEAD_DIM), F32), jnp.zeros((ATTN_ROWS, 1), F32), jnp.float32(0.0))
        after_first = step(first, keep=True)
        reach_ref[...] = jnp.full(reach_ref.shape, after_first[-1], F32)
        o_ref[...] = lax.while_loop(_attn_more, step, after_first)[2]

    qspec = pl.BlockSpec((1, ATTN_ROWS, HEAD_DIM), lambda h, i: (3, i, h))
    return pl.pallas_call(
        body, name="attn_fwd", grid=(N_HEADS, nq),
        in_specs=[qspec, _head_cols(4)(t), _head_cols(5)(t), pl.BlockSpec((ATTN_SUB, ATTN_SUB), lambda h, i: (0, 0))],
        out_specs=[pl.BlockSpec((ATTN_ROWS, HEAD_DIM), lambda h, i: (i, h)), _first_step_spec(), _first_step_spec(), _reach_spec()],
        out_shape=[S((t, D_MODEL), F32), S((N_HEADS, t, ATTN_K), BF), S((N_HEADS, t, ATTN_K), BF), S((N_HEADS, nq) + REACH_TILE, F32)],
    )(proj, proj, proj, tri)


def _attn_bwd(proj, o, d_o, a_first, beta_first, reach, ride=None):
    t = proj.shape[1]
    nq = t // ATTN_ROWS
    tri_strict, tri_incl = _tri(strict=True), _tri(strict=False)
    scale = 1.0 / math.sqrt(HEAD_DIM)

    def body(q_ref, k_ref, v_ref, o_ref, do_ref, a_ref, beta_ref, reach_ref, tris_ref, trii_ref, dq_ref, dk_ref, dv_ref, dk_acc, dv_acc):
        i = pl.program_id(1)

        @pl.when(i == 0)
        def _():
            dk_acc[...] = jnp.zeros_like(dk_acc)
            dv_acc[...] = jnp.zeros_like(dv_acc)

        q = q_ref[0]
        do = do_ref[...]
        total = jnp.sum(do.astype(F32) * o_ref[...], axis=1, keepdims=True)
        zero = jnp.zeros((ATTN_ROWS, 1), F32)
        blks0, bounds0 = _attn_sweep_start(i, t)

        def finish(starts, a, dz, dq):
            dzb = (dz * scale).astype(BF)
            for u in range(ATTN_UNITS):
                dv_acc[pl.ds(starts[u], ATTN_K), :] += _dot_tn(_unit_rows(a, u), _unit_rows(do, u))
                dk_acc[pl.ds(starts[u], ATTN_K), :] += _dot_tn(_unit_rows(dzb, u), _unit_rows(q, u))
            return dq + _per_unit(lambda u: _dot(_unit_rows(dzb, u), k_ref[0, pl.ds(starts[u], ATTN_K), :]))

        def grad_a(starts, a):
            return _per_unit(lambda u: _dot_nt(_unit_rows(do, u), v_ref[0, pl.ds(starts[u], ATTN_K), :])) * a.astype(F32)

        one_step = jnp.max(reach_ref[...]) <= ATTN_SKIP_BELOW

        @pl.when(one_step)
        def _():
            starts = _attn_keys(blks0)
            a = a_ref[0]
            beta = beta_ref[0].astype(F32)
            de = grad_a(starts, a)
            right, _ = _suffix_sums(de, trii_ref[...], zero)
            dz = de * (1.0 - beta) - (total - right) * beta
            dq_ref[...] = finish(starts, a, dz, jnp.zeros((ATTN_ROWS, HEAD_DIM), F32)).astype(BF)

        @pl.when(jnp.logical_not(one_step))
        def _():
            row = i * ATTN_ROWS + lax.broadcasted_iota(jnp.int32, (ATTN_ROWS, 1), 0)

            def step(carry):
                blks, bounds, dq, seen, run, _ = carry
                starts = _attn_keys(blks)
                z, mask, log_beta, log_rest = _attn_step(q, k_ref, starts, bounds, row)
                tail, run = _suffix_sums(log_rest, tris_ref[...], run)
                a = jnp.where(mask, jnp.exp(log_beta + tail), 0.0).astype(BF)
                de = grad_a(starts, a)
                right, seen = _suffix_sums(de, trii_ref[...], seen)
                beta = jax.nn.sigmoid(z)
                dz = jnp.where(mask, de * (1.0 - beta) - (total - right) * beta, 0.0)
                return (*_attn_next(blks), finish(starts, a, dz, dq), seen, run, jnp.max(run))

            first = (blks0, bounds0, jnp.zeros((ATTN_ROWS, HEAD_DIM), F32), zero, zero, jnp.float32(0.0))
            dq_ref[...] = lax.while_loop(_attn_more, step, step(first))[2].astype(BF)

        @pl.when(i == nq - 1)
        def _():
            dk_ref[...] = dk_acc[...].astype(BF)
            dv_ref[...] = dv_acc[...].astype(BF)

    qspec = pl.BlockSpec((1, ATTN_ROWS, HEAD_DIM), lambda h, i: (3, i, h))
    rowblk = pl.BlockSpec((ATTN_ROWS, HEAD_DIM), lambda h, i: (i, h))
    head = pl.BlockSpec((t, HEAD_DIM), lambda h, i: (0, h))
    trispec = pl.BlockSpec((ATTN_SUB, ATTN_SUB), lambda h, i: (0, 0))
    return _pallas(
        body, name="attn_bwd", grid=(N_HEADS, nq), ride=ride,
        in_specs=[qspec, _head_cols(4)(t), _head_cols(5)(t), rowblk, rowblk, _first_step_spec(), _first_step_spec(), _reach_spec(),
                  trispec, trispec],
        out_specs=[rowblk, head, head],
        out_shape=[S((t, D_MODEL), BF)] * 3,
        scratch_shapes=[pltpu.VMEM((t, HEAD_DIM), F32), pltpu.VMEM((t, HEAD_DIM), F32)],
        args=[proj, proj, proj, o, d_o, a_first, beta_first, reach, tri_strict, tri_incl])


def _tail(h3, n4, p, w_pg, w_pp, g_ple, g_final, target):
    t = h3.shape[0]
    tm = min(TOKEN_TILE, t)
    steps = t // tm

    def body(h_ref, n_ref, p_ref, wpg_ref, wpp_ref, gp_ref, gf_ref, tgt_ref,
             dh_ref, ds_ref, dpp_ref, loss_ref, dgf_ref, dgp_ref):
        pg = jax.nn.sigmoid(_dot(n_ref[...], wpg_ref[...]))
        pp = _dot(p_ref[...].astype(BF), wpp_ref[...])
        h3v = h_ref[...]
        h4 = h3v + pg * pp
        gf = gf_ref[...]
        diff = _rms(h4, gf) - tgt_ref[...]
        _accumulate(loss_ref, jnp.sum(diff * diff, axis=0, keepdims=True))
        dh4, dgf = _rms_bwd(diff * (1.0 / D_MODEL), h4, gf)
        _accumulate(dgf_ref, dgf)
        dpp_ref[...] = (dh4 * pg).astype(BF)
        ds = (dh4 * pp * pg * (1.0 - pg)).astype(BF)
        ds_ref[...] = ds
        dh3, dgp = _rms_bwd(_dot_nt(ds, wpg_ref[...]), h3v, gp_ref[...])
        _accumulate(dgp_ref, dgp)
        dh_ref[...] = dh4 + dh3

        @pl.when(pl.program_id(0) == steps - 1)
        def _():
            loss_ref[...] = jnp.full(loss_ref.shape, 0.5 / D_MODEL * jnp.sum(loss_ref[...]), F32)

    vec = (1, D_MODEL)
    return pl.pallas_call(
        body, name="tail", grid=(steps,),
        in_specs=[_rows(tm, D_MODEL), _rows(tm, D_MODEL), _rows(tm, PLE_DIM), _const_spec((D_MODEL, D_MODEL)),
                  _const_spec((PLE_DIM, D_MODEL)), _const_spec(vec), _const_spec(vec), _rows(tm, D_MODEL)],
        out_specs=[_rows(tm, D_MODEL)] * 3 + [_acc_spec(vec)] * 3,
        out_shape=[S((t, D_MODEL), F32), S((t, D_MODEL), BF), S((t, D_MODEL), BF)] + [S(vec, F32)] * 3,
    )(h3, n4, p, w_pg, w_pp, g_ple, g_final, target)


def _wgrad(xs, ys, name, ride=None, tile=None):
    bx, t, k = xs.shape
    by, _, n = ys.shape
    b = max(bx, by)
    tt = min(tile or WGRAD_TILE * 2 // xs.dtype.itemsize, t)
    steps = t // tt

    def body(x_ref, y_ref, o_ref, acc_ref):
        s = pl.program_id(1)

        @pl.when(s == 0)
        def _():
            acc_ref[...] = jnp.zeros_like(acc_ref)
        acc_ref[...] += _dot_tn(x_ref[0].astype(BF), y_ref[0].astype(BF))

        @pl.when(s == steps - 1)
        def _():
            o_ref[0] = acc_ref[...].astype(BF)

    (out,), landed = _pallas(
        body, name=name, grid=(b, steps), ride=ride,
        in_specs=[pl.BlockSpec((1, tt, k), (lambda j, s: (j, s, 0)) if bx > 1 else (lambda j, s: (0, s, 0))),
                  pl.BlockSpec((1, tt, n), (lambda j, s: (j, s, 0)) if by > 1 else (lambda j, s: (0, s, 0)))],
        out_specs=[pl.BlockSpec((1, k, n), lambda j, s: (j, 0, 0))],
        out_shape=[S((b, k, n), BF)],
        scratch_shapes=[pltpu.VMEM((k, n), F32)],
        args=[xs, ys])
    return (out, landed) if ride is not None else out


def _wgrad_pieces(x, ys, name, ride=None, tile=None, row_parts=1, transposed=False):
    t, k = x.shape
    n = ys[0].shape[2]
    counts = [y.shape[0] for y in ys]
    offsets = [sum(counts[:j]) for j in range(len(ys))]
    total = sum(counts)
    tt = min(tile or WGRAD_TILE, t)
    steps = t // tt
    rows, cols = (n, k) if transposed else (k, n)
    kp = rows // row_parts

    def body(x_ref, *refs):
        y_refs, o_refs, acc_ref = refs[:len(ys)], refs[len(ys):len(ys) + row_parts], refs[len(ys) + row_parts]
        p, s = pl.program_id(0), pl.program_id(1)

        @pl.when(s == 0)
        def _():
            acc_ref[...] = jnp.zeros_like(acc_ref)
        for j, y_ref in enumerate(y_refs):
            @pl.when(jnp.logical_and(p >= offsets[j], p < offsets[j] + counts[j]))
            def _(y_ref=y_ref):
                acc_ref[...] += _dot_tn(y_ref[0], x_ref[...]) if transposed else _dot_tn(x_ref[...], y_ref[0])

        @pl.when(s == steps - 1)
        def _():
            for part, o_ref in enumerate(o_refs):
                o_ref[0] = acc_ref[part * kp:(part + 1) * kp, :].astype(BF)

    def turn(j):
        lo, hi = offsets[j], offsets[j] + counts[j]
        return lambda p, s: (jnp.clip(p - lo, 0, counts[j] - 1), jnp.where(p < lo, 0, jnp.where(p >= hi, steps - 1, s)), 0)

    outs, landed = _pallas(
        body, name=name, grid=(total, steps), ride=ride,
        in_specs=[pl.BlockSpec((tt, k), lambda p, s: (s, 0))] + [pl.BlockSpec((1, tt, n), turn(j)) for j in range(len(ys))],
        out_specs=[pl.BlockSpec((1, kp, cols), lambda p, s: (p, 0, 0))] * row_parts,
        out_shape=[S((total, kp, cols), BF)] * row_parts,
        scratch_shapes=[pltpu.VMEM((rows, cols), F32)],
        args=[x, *ys])
    out = outs[0] if row_parts == 1 else outs
    return (out, landed) if ride is not None else out


def _ffn_bwd_hidden(dh, to_gate, to_up, w_out, name, ride=None):
    t = dh.shape[0]
    tm = min(TOKEN_TILE, t)

    def body(dh_ref, to_gate_ref, to_up_ref, wout_ref, df_ref, dgate_ref, dup_ref):
        df = (0.5 * dh_ref[...]).astype(BF)
        df_ref[...] = df
        for c in range(N_FF_CHUNKS):
            dact = _dot_nt(df, wout_ref[c])
            dgate_ref[c] = (dact * to_gate_ref[c].astype(F32)).astype(BF)
            dup_ref[c] = (dact * to_up_ref[c].astype(F32)).astype(BF)

    return _pallas(
        body, name=name, grid=(t // tm,), ride=ride,
        in_specs=[_rows(tm, D_MODEL), _chunks(tm), _chunks(tm), _const_spec(w_out.shape)],
        out_specs=[_rows(tm, D_MODEL), _chunks(tm), _chunks(tm)],
        out_shape=[S((t, D_MODEL), BF)] + [S((N_FF_CHUNKS, t, FF_CHUNK), BF)] * 2,
        args=[dh, to_gate, to_up, w_out])


def _ffn_bwd_input(dh, h_in, g, dgate, dup, w_in, name, ride=None):
    t = dh.shape[0]
    tm = min(TOKEN_TILE, t)

    def body(dh_ref, h_ref, g_ref, dgate_ref, dup_ref, win_ref, dhi_ref, dg_ref):
        dn = jnp.zeros((tm, D_MODEL), F32)
        for c in range(N_FF_CHUNKS):
            dn = dn + _dot_nt(dgate_ref[c], win_ref[c]) + _dot_nt(dup_ref[c], win_ref[N_FF_CHUNKS + c])
        dhi, dg = _rms_bwd(dn, h_ref[...], g_ref[...])
        _accumulate(dg_ref, dg)
        dhi_ref[...] = dh_ref[...] + dhi

    vec = (1, D_MODEL)
    return _pallas(
        body, name=name, grid=(t // tm,), ride=ride,
        in_specs=[_rows(tm, D_MODEL), _rows(tm, D_MODEL), _const_spec(vec), _chunks(tm), _chunks(tm), _const_spec(w_in.shape)],
        out_specs=[_rows(tm, D_MODEL), _acc_spec(vec)],
        out_shape=[S((t, D_MODEL), F32), S(vec, F32)],
        args=[dh, h_in, g, dgate, dup, w_in])


def _mixer_bwd(dh2, proj, yc, ya, conv_w, w_co, w_ao, w_mo, ride=None):
    t = dh2.shape[0]
    tm = min(TOKEN_TILE, t)

    def body(dh_ref, cb_ref, cc_ref, cx_ref, gc_ref, ga_ref, cch_ref, cxh_ref, yc_ref, ya_ref, cw_ref, wco_ref, wao_ref, wmo_ref,
             dhb_ref, dyc_ref, dya_ref, dgc_ref, dga_ref, dcb_ref, dcv_ref, do_ref):
        dhb = dh_ref[...].astype(BF)
        dhb_ref[...] = dhb
        dmerged = _dot_nt(dhb, wmo_ref[...])
        sc = jax.nn.sigmoid(gc_ref[0].astype(F32))
        sa = jax.nn.sigmoid(ga_ref[0].astype(F32))
        dyc = (dmerged * sc).astype(BF)
        dya = (dmerged * sa).astype(BF)
        dyc_ref[...] = dyc
        dya_ref[...] = dya
        dgc_ref[...] = (dmerged * yc_ref[...].astype(F32) * sc * (1.0 - sc)).astype(BF)
        dga_ref[...] = (dmerged * ya_ref[...].astype(F32) * sa * (1.0 - sa)).astype(BF)
        m, m1, m2 = _conv_inputs(cc_ref, cx_ref, cch_ref, cxh_ref)
        cw = cw_ref[...]
        cv = cw[0:1, :] * m2 + cw[1:2, :] * m1 + cw[2:3, :] * m
        dycin = _dot_nt(dyc, wco_ref[...])
        dcb_ref[...] = (dycin * cv).astype(BF)
        dcv_ref[...] = dycin * cb_ref[0].astype(F32)
        do_ref[...] = _dot_nt(dya, wao_ref[...]).astype(BF)

    sq = (D_MODEL, D_MODEL)
    return _pallas(
        body, name="mixer_bwd", grid=(t // tm,), ride=ride,
        in_specs=[_rows(tm, D_MODEL), _piece(0, tm), _piece(1, tm), _piece(2, tm), _piece(6, tm), _piece(7, tm),
                  _prev_halo(1, tm), _prev_halo(2, tm), _rows(tm, D_MODEL), _rows(tm, D_MODEL),
                  _const_spec((3, D_MODEL)), _const_spec(sq), _const_spec(sq), _const_spec(sq)],
        out_specs=[_rows(tm, D_MODEL)] * 8,
        out_shape=[S((t, D_MODEL), BF)] * 6 + [S((t, D_MODEL), F32), S((t, D_MODEL), BF)],
        args=[dh2, proj, proj, proj, proj, proj, proj, proj, yc, ya, conv_w, w_co, w_ao, w_mo])


F32_HALO = 8


def _conv_bwd(dcv, proj, conv_w):
    t = dcv.shape[0]
    tm = min(TOKEN_TILE, t)
    steps = t // tm

    def body(dcv_ref, nxt_ref, cc_ref, cx_ref, cch_ref, cxh_ref, cw_ref, dcc_ref, dcx_ref, dw_ref):
        i = pl.program_id(0)
        m, m1, m2 = _conv_inputs(cc_ref, cx_ref, cch_ref, cxh_ref)
        d0 = dcv_ref[...]
        nxt = jnp.where(i == steps - 1, 0.0, nxt_ref[...])
        row = lax.broadcasted_iota(jnp.int32, (tm, 1), 0)
        d1 = jnp.where(row == tm - 1, nxt[0:1, :], pltpu.roll(d0, tm - 1, 0))
        d2 = pltpu.roll(d0, tm - 2, 0)
        d2 = jnp.where(row == tm - 2, nxt[0:1, :], jnp.where(row == tm - 1, nxt[1:2, :], d2))
        cw = cw_ref[...]
        dm = cw[2:3, :] * d0 + cw[1:2, :] * d1 + cw[0:1, :] * d2
        dcc_ref[...] = (dm * cx_ref[0].astype(F32)).astype(BF)
        dcx_ref[...] = (dm * cc_ref[0].astype(F32)).astype(BF)
        tap_row = lax.broadcasted_iota(jnp.int32, (F32_HALO, 1), 0)
        dw = jnp.zeros((F32_HALO, D_MODEL), F32)
        for j, mk in enumerate((m2, m1, m)):
            dw = jnp.where(tap_row == j, jnp.sum(d0 * mk, axis=0, keepdims=True), dw)
        _accumulate(dw_ref, dw)

    nxt_spec = pl.BlockSpec((F32_HALO, D_MODEL), lambda i: (jnp.minimum((i + 1) * (tm // F32_HALO), t // F32_HALO - 1), 0))
    return pl.pallas_call(
        body, name="conv_bwd", grid=(steps,),
        in_specs=[_rows(tm, D_MODEL), nxt_spec, _piece(1, tm), _piece(2, tm), _prev_halo(1, tm), _prev_halo(2, tm),
                  _const_spec((3, D_MODEL))],
        out_specs=[_rows(tm, D_MODEL), _rows(tm, D_MODEL), _acc_spec((F32_HALO, D_MODEL))],
        out_shape=[S((t, D_MODEL), BF), S((t, D_MODEL), BF), S((F32_HALO, D_MODEL), F32)],
    )(dcv, dcv, proj, proj, proj, proj, conv_w)


def _mix_bwd(dpieces, w_mix, h1, dh2, g, ride=None):
    t = h1.shape[0]
    tm = min(TOKEN_TILE, t)

    def body(*refs):
        pieces, (w_ref, h_ref, dh_ref, g_ref, dhi_ref, dg_ref) = refs[:N_MIX], refs[N_MIX:]
        du = jnp.zeros((tm, D_MODEL), F32)
        for d in range(N_MIX):
            du = du + _dot_nt(pieces[d][...], w_ref[d])
        dhi, dg = _rms_bwd(du, h_ref[...], g_ref[...])
        _accumulate(dg_ref, dg)
        dhi_ref[...] = dh_ref[...] + dhi

    vec = (1, D_MODEL)
    return _pallas(
        body, name="mix_bwd", grid=(t // tm,), ride=ride,
        in_specs=[_rows(tm, D_MODEL)] * N_MIX + [_const_spec(w_mix.shape), _rows(tm, D_MODEL), _rows(tm, D_MODEL), _const_spec(vec)],
        out_specs=[_rows(tm, D_MODEL), _acc_spec(vec)],
        out_shape=[S((t, D_MODEL), F32), S(vec, F32)],
        args=[*dpieces, w_mix, h1, dh2, g])


def _adamw(partials, w, m, v, name):
    parts = list(partials) if isinstance(partials, (list, tuple)) else [partials]
    r, c = w.shape
    tr = next(d for d in (r, 512, 352, 256) if d <= 512 // len(parts) and r % d == 0)
    first_tile = [sum(p.shape[1] for p in parts[:j]) // tr for j in range(len(parts))]
    c1 = 1.0 - ADAM_B1 ** ADAM_STEP
    c2 = 1.0 - ADAM_B2 ** ADAM_STEP

    def body(*refs):
        p_refs, (w_ref, m_ref, v_ref, g_ref, d_ref, mo_ref, vo_ref) = refs[:len(parts)], refs[len(parts):]
        g = None
        for j, p_ref in enumerate(p_refs):
            gj = p_ref[0].astype(F32)
            for s in range(1, N_SHARDS):
                gj = gj + p_ref[s].astype(F32)
            g = gj if g is None else jnp.where(pl.program_id(0) >= first_tile[j], gj, g)
        mn = ADAM_B1 * m_ref[...] + (1.0 - ADAM_B1) * g
        vn = ADAM_B2 * v_ref[...] + (1.0 - ADAM_B2) * (g * g)
        g_ref[...] = g
        mo_ref[...] = mn
        vo_ref[...] = vn
        d_ref[...] = -ADAM_LR * ((mn / c1) / (jnp.sqrt(vn / c2) + ADAM_EPS) + ADAM_WD * w_ref[...])

    def rows_of(j):
        last = parts[j].shape[1] // tr - 1
        return lambda i: (0, jnp.clip(i - first_tile[j], 0, last), 0)

    blk = pl.BlockSpec((tr, c), lambda i: (i, 0))
    return pl.pallas_call(
        body, name=name, grid=(r // tr,),
        in_specs=[pl.BlockSpec((N_SHARDS, tr, c), rows_of(j)) for j in range(len(parts))] + [blk, blk, blk],
        out_specs=[blk] * 4, out_shape=[S((r, c), F32)] * 4,
    )(*parts, w, m, v)


_MATRICES = ("ffn1_w_in", "ffn1_w_out", "w_mix_in", "conv_w", "w_conv_out", "w_attn_out", "w_mix_out",
             "ffn2_w_in", "ffn2_w_out", "w_ple_gate", "w_ple_proj")
_GAINS = ("ffn1_norm", "mix_norm", "ffn2_norm", "ple_norm", "final_norm")
_WEIGHTS = ("ffn1_norm", "ffn1_w_in", "ffn1_w_out", "mix_norm", "w_mix_in", "conv_w", "w_conv_out", "w_attn_out", "w_mix_out",
            "ffn2_norm", "ffn2_w_in", "ffn2_w_out", "ple_norm", "w_ple_gate", "w_ple_proj", "final_norm")
CONV_ROWS = 8
_TRANSPOSED = ("ffn1_w_in", "ffn2_w_in")


def _columns_from_shards(g):
    return jnp.transpose(g, (1, 0, 2)).reshape(g.shape[1], N_SHARDS * g.shape[2])


def _shards_from_columns(a):
    r, c = a.shape
    return jnp.transpose(a.reshape(r, N_SHARDS, c // N_SHARDS), (1, 0, 2))


def kernel(x, p, ffn1_norm, ffn1_w_in, ffn1_w_out, mix_norm, w_mix_in, conv_w, w_conv_out, w_attn_out, w_mix_out, ffn2_norm, ffn2_w_in, ffn2_w_out, ple_norm, w_ple_gate, w_ple_proj, final_norm, loss_target, m_ffn1_norm, m_ffn1_w_in, m_ffn1_w_out, m_mix_norm, m_w_mix_in, m_conv_w, m_w_conv_out, m_w_attn_out, m_w_mix_out, m_ffn2_norm, m_ffn2_w_in, m_ffn2_w_out, m_ple_norm, m_w_ple_gate, m_w_ple_proj, m_final_norm, v_ffn1_norm, v_ffn1_w_in, v_ffn1_w_out, v_mix_norm, v_w_mix_in, v_conv_w, v_w_conv_out, v_w_attn_out, v_w_mix_out, v_ffn2_norm, v_ffn2_w_in, v_ffn2_w_out, v_ple_norm, v_w_ple_gate, v_w_ple_proj, v_final_norm):
    given = dict(locals())
    t = x.shape[1]
    xs = x.reshape(t, D_MODEL)
    ps = p.reshape(t, PLE_DIM)
    target = loss_target.reshape(t, D_MODEL)
    shard = {k: given[k].reshape(given[k].shape[-2:]) for k in _MATRICES}
    gain = {k: given[k].reshape(1, D_MODEL) for k in _GAINS}

    send = {k: shard[k].astype(BF) for k in _MATRICES}
    send["conv_w"] = jnp.pad(shard["conv_w"], ((0, CONV_ROWS - 3), (0, 0)))
    loss_vec, dx, landed, gain_grads = _forward_backward(xs, ps, target, gain, send)
    gain_rows = jnp.concatenate([gain_grads[k] for k in _GAINS] + [loss_vec, jnp.zeros((8 - len(_GAINS) - 1, D_MODEL), F32)], axis=0)
    gain_parts, = _exchange_alone("gather", [gain_rows], "gather_gain_gradients")

    out = {}
    for k in _MATRICES:
        w, m, v = shard[k], given["m_" + k].reshape(shard[k].shape), given["v_" + k].reshape(shard[k].shape)
        part = landed[k]
        if k == "conv_w":
            pad = ((0, CONV_ROWS - 3), (0, 0))
            w, m, v = jnp.pad(w, pad), jnp.pad(m, pad), jnp.pad(v, pad, constant_values=1.0)
        if k in _TRANSPOSED:
            w, m, v = w.T, m.T, v.T
        res = _adamw(part, w, m, v, "adamw_" + k)
        out[k] = [r[:3] if k == "conv_w" else (r.T if k in _TRANSPOSED else r) for r in res]
    stack = lambda pre: jnp.concatenate([given[pre + k].reshape(1, D_MODEL) for k in _GAINS] + [jnp.ones((8 - len(_GAINS), D_MODEL), F32)], axis=0)
    res = _adamw(gain_parts, stack(""), stack("m_"), stack("v_"), "adamw_gains")
    for j, k in enumerate(_GAINS):
        out[k] = [r[j:j + 1] for r in res]

    loss = jnp.sum(gain_parts[:, len(_GAINS), 0])
    per_kind = [[out[k][j].reshape(given[k].shape) for k in _WEIGHTS] for j in range(4)]
    return (loss, dx.reshape(x.shape), *per_kind[0], *per_kind[1], *per_kind[2], *per_kind[3])


def _forward_backward(xs, ps, target, gain, send, full=None):
    exchange = full is None
    full = dict(full or {})
    grads, landed = {}, {}

    def gather(names):
        return ("gather", [send[k] for k in names]) if exchange else None

    def scatter(names):
        return ("scatter", [grads[k] for k in names]) if exchange else None

    def keep(into, names, got):
        into.update(zip(names, got))

    first = ("ffn1_w_in",)
    (n1,), got = _prenorm(xs, gain["ffn1_norm"], ride=gather(first))
    keep(full, first, got)
    w1_in = full["ffn1_w_in"]
    second = ("ffn1_w_out", "w_mix_in")
    (act1, to_gate1, to_up1), got = _ffn_up(n1, w1_in, "ffn1_up", ride=gather(second))
    keep(full, second, got)
    w1_out = full["ffn1_w_out"].reshape(N_FF_CHUNKS, FF_CHUNK, D_MODEL)
    third = ("conv_w", "w_conv_out", "w_attn_out", "w_mix_out")
    (h1, u), got = _ffn_down(xs, act1, w1_out, gain["mix_norm"], "ffn1_down", ride=gather(third))
    keep(full, third, got)
    w_mix = full["w_mix_in"]
    w_co, w_ao, w_mo = (full[k].reshape(D_MODEL, D_MODEL) for k in ("w_conv_out", "w_attn_out", "w_mix_out"))
    taps = _columns_from_shards(full["conv_w"][:, :3, :])
    rest = ("ffn2_w_in", "ffn2_w_out", "w_ple_gate", "w_ple_proj")
    (proj,), got = _mix_proj(u, w_mix, ride=gather(rest))
    keep(full, rest, got)
    w2_in, w2_out = full["ffn2_w_in"], full["ffn2_w_out"].reshape(N_FF_CHUNKS, FF_CHUNK, D_MODEL)
    w_pg = full["w_ple_gate"].reshape(D_MODEL, D_MODEL)
    w_pp = _columns_from_shards(full["w_ple_proj"])
    o, a_first, beta_first, reach = _attn_fwd(proj)
    h2, n3, ycin, yc, ya, merged = _mixer_out(proj, o, h1, taps, w_co, w_ao, w_mo, gain["ffn2_norm"])
    (act2, to_gate2, to_up2), _ = _ffn_up(n3, w2_in, "ffn2_up")
    (h3, n4), _ = _ffn_down(h2, act2, w2_out, gain["ple_norm"], "ffn2_down")
    dh3, ds, dpp, loss_vec, dg_final, dg_ple = _tail(h3, n4, ps, w_pg, w_pp, gain["ple_norm"], gain["final_norm"], target)

    one = lambda a: a[None]
    by_rows = lambda g, rows: g.reshape(N_SHARDS, rows // N_SHARDS, D_MODEL)
    grads["w_ple_gate"] = by_rows(_wgrad(one(n4), one(ds), "wgrad_ple_gate"), D_MODEL)
    grads["w_ple_proj"] = _shards_from_columns(_wgrad(one(ps), one(dpp), "wgrad_ple_proj")[0])
    ple = ("w_ple_gate", "w_ple_proj")
    (df2, dgate2, dup2), got = _ffn_bwd_hidden(dh3, to_gate2, to_up2, w2_out, "ffn2_bwd_hidden", ride=scatter(ple))
    keep(landed, ple, got)
    grads["ffn2_w_out"] = by_rows(_wgrad(act2, one(df2), "wgrad_ffn2_out"), D_FF)
    grads["ffn2_w_in"] = _wgrad_pieces(n3, [dgate2, dup2], "wgrad_ffn2_in", transposed=True)
    (dh2, dg_ffn2), got = _ffn_bwd_input(dh3, h2, gain["ffn2_norm"], dgate2, dup2, w2_in, "ffn2_bwd_input", ride=scatter(("ffn2_w_out",)))
    keep(landed, ("ffn2_w_out",), got)
    (dh2b, dyc, dya, dgc, dga, dcb, dcv, d_o), _ = _mixer_bwd(dh2, proj, yc, ya, taps, w_co, w_ao, w_mo)
    grads["w_mix_out"] = by_rows(_wgrad(one(merged), one(dh2b), "wgrad_mix_out"), D_MODEL)
    grads["w_conv_out"] = by_rows(_wgrad(one(ycin), one(dyc), "wgrad_conv_out"), D_MODEL)
    grads["w_attn_out"] = by_rows(_wgrad(one(o), one(dya), "wgrad_attn_out"), D_MODEL)
    dcc, dcx, dtaps = _conv_bwd(dcv, proj, taps)
    grads["conv_w"] = jnp.pad(_shards_from_columns(dtaps[:3]), ((0, 0), (0, CONV_ROWS - 3), (0, 0)))
    behind_attn = ("ffn2_w_in", "w_mix_out", "w_conv_out", "w_attn_out", "conv_w")
    (dq, dk, dv), got = _attn_bwd(proj, o, d_o, a_first, beta_first, reach, ride=scatter(behind_attn))
    keep(landed, behind_attn, got)
    dpieces = [dcb, dcc, dcx, dq, dk, dv, dgc, dga]
    half = N_MIX // 2
    tops, bottoms = zip(_wgrad_pieces(u, [one(dp) for dp in dpieces[:half]], "wgrad_mix_in_a", tile=WGRAD_TILE // 2, row_parts=2),
                        _wgrad_pieces(u, [one(dp) for dp in dpieces[half:]], "wgrad_mix_in_b", tile=WGRAD_TILE // 2, row_parts=2))
    grads["w_mix_in top"], grads["w_mix_in bottom"] = jnp.concatenate(tops, axis=0), jnp.concatenate(bottoms, axis=0)
    (dh1, dg_mix), top = _mix_bwd(dpieces, w_mix, h1, dh2, gain["mix_norm"], ride=scatter(("w_mix_in top",)))
    (df1, dgate1, dup1), bottom = _ffn_bwd_hidden(dh1, to_gate1, to_up1, w1_out, "ffn1_bwd_hidden", ride=scatter(("w_mix_in bottom",)))
    if exchange:
        landed["w_mix_in"] = [top[0], bottom[0]]
    else:
        grads["w_mix_in"] = jnp.concatenate([grads.pop("w_mix_in top"), grads.pop("w_mix_in bottom")], axis=1)
    grads["ffn1_w_out"] = by_rows(_wgrad(act1, one(df1), "wgrad_ffn1_out"), D_FF)
    if exchange:
        grads["ffn1_w_in"], got = _wgrad_pieces(n1, [dgate1, dup1], "wgrad_ffn1_in", transposed=True, ride=scatter(("ffn1_w_out",)))
        keep(landed, ("ffn1_w_out",), got)
    else:
        grads["ffn1_w_in"] = _wgrad_pieces(n1, [dgate1, dup1], "wgrad_ffn1_in", transposed=True)
    (dx, dg_ffn1), got = _ffn_bwd_input(dh1, xs, gain["ffn1_norm"], dgate1, dup1, w1_in, "ffn1_bwd_input", ride=scatter(("ffn1_w_in",)))
    keep(landed, ("ffn1_w_in",), got)
    gain_grads = dict(ffn1_norm=dg_ffn1, mix_norm=dg_mix, ffn2_norm=dg_ffn2, ple_norm=dg_ple, final_norm=dg_final)
    return loss_vec, dx, (landed if exchange else grads), gain_grads
```

```python
import functools
import math

import jax
import jax.numpy as jnp
from jax import lax
from jax.experimental import pallas as pl
from jax.experimental.pallas import tpu as pltpu

D_MODEL = 1024
D_FF = 2816
N_SHARDS = 8
FF_CHUNK = 2 * D_FF // N_SHARDS
N_FF_CHUNKS = D_FF // FF_CHUNK
N_HEADS = 8
HEAD_DIM = 128
PLE_DIM = 256
NORM_EPS = 1e-6
N_MIX = 8
ADAM_LR, ADAM_B1, ADAM_B2, ADAM_EPS, ADAM_WD, ADAM_STEP = 0.001, 0.9, 0.999, 1e-08, 0.01, 10

TOKEN_TILE = 512
WGRAD_TILE = 4096
PROJ_TILE = 2048
ATTN_ROWS = 512
ATTN_Q = 128
ATTN_SUB = 128
ATTN_K = 3 * ATTN_SUB
ATTN_SKIP_BELOW = -90.0

BF = jnp.bfloat16
F32 = jnp.float32
MESH = pl.DeviceIdType.MESH
NT = (((1,), (1,)), ((), ()))
TN = (((0,), (0,)), ((), ()))
S = jax.ShapeDtypeStruct
ANY = pl.BlockSpec(memory_space=pl.ANY)


def _const_spec(shape):
    nd = len(shape)
    return pl.BlockSpec(shape, lambda *_: (0,) * nd, pipeline_mode=pl.Buffered(1))


def _rows(tm, cols):
    return pl.BlockSpec((tm, cols), lambda i: (i, 0))


def _chunks(tm):
    return pl.BlockSpec((N_FF_CHUNKS, tm, FF_CHUNK), lambda i: (0, i, 0))


def _acc_spec(shape):
    nd = len(shape)
    return pl.BlockSpec(shape, lambda *_: (0,) * nd)


def _dot(a, b):
    return jnp.dot(a, b, preferred_element_type=F32)


def _dot_nt(a, b):
    return lax.dot_general(a, b, NT, preferred_element_type=F32)


def _dot_tn(a, b):
    return lax.dot_general(a, b, TN, preferred_element_type=F32)


def _rms(h, g):
    r = lax.rsqrt(jnp.mean(h * h, axis=-1, keepdims=True) + NORM_EPS)
    return h * r * g


def _rms_bwd(dn, h, g):
    r = lax.rsqrt(jnp.mean(h * h, axis=-1, keepdims=True) + NORM_EPS)
    nh = h * r
    gd = dn * g
    dh = r * (gd - nh * jnp.mean(gd * nh, axis=-1, keepdims=True))
    return dh, jnp.sum(dn * nh, axis=0, keepdims=True)


def _accumulate(ref, val):
    @pl.when(pl.program_id(0) == 0)
    def _():
        ref[...] = jnp.zeros_like(ref)
    ref[...] += val


def _place():
    x, y, c = lax.axis_index("x"), lax.axis_index("y"), lax.axis_index("c")
    return x, y, c


def _slot(px, py, pc):
    return 4 * px + 2 * py + pc


def _gather_phases(ins, outs, send_sems, recv_sems, local_sems):
    n = len(ins)

    def parties():
        x, y, c = _place()
        return (x, y, c), (x, y, 1 - c), [(1 - x, y), (x, 1 - y), (1 - x, 1 - y)], c

    def copy(a, k, block, to, src=None):
        dst = outs[a].at[_slot(*block)]
        return pltpu.make_async_remote_copy(
            src_ref=dst if src is None else src, dst_ref=dst,
            send_sem=send_sems.at[a, k], recv_sem=recv_sems.at[a, k],
            device_id=to, device_id_type=MESH)

    def own(a, me):
        return pltpu.make_async_copy(ins[a], outs[a].at[_slot(*me)], local_sems.at[a])

    def first(a, me, sibling, chips, c):
        return [copy(a, 0, me, sibling, src=ins[a])] + [copy(a, 1 + j, me, (*chip, c), src=ins[a]) for j, chip in enumerate(chips)]

    def start():
        me, sibling, chips, c = parties()
        for a in range(n):
            own(a, me).start()
        for a in range(n):
            for cp in first(a, me, sibling, chips, c):
                cp.start()

    def forward():
        me, sibling, chips, c = parties()
        for j, chip in enumerate(chips):
            for a in range(n):
                copy(a, 1 + j, (*chip, c), me).wait_recv()
                copy(a, 4 + j, (*chip, c), sibling).start()

    def finish():
        me, sibling, chips, c = parties()
        for a in range(n):
            copy(a, 0, sibling, me).wait_recv()
            for j, chip in enumerate(chips):
                copy(a, 4 + j, (*chip, 1 - c), me).wait_recv()
        for a in range(n):
            for cp in first(a, me, sibling, chips, c) + [copy(a, 4 + j, (*chip, c), sibling) for j, chip in enumerate(chips)]:
                cp.wait_send()
            own(a, me).wait()

    return [start, forward, finish]


def _scatter_phases(ins, outs, send_sems, recv_sems, local_sems):
    n = len(ins)

    def copies():
        x, y, c = _place()
        me = _slot(x, y, c)
        out = [pltpu.make_async_copy(ins[a].at[me], outs[a].at[me], local_sems.at[a]) for a in range(n)]
        for k in range(1, N_SHARDS):
            px = 1 - x if k & 4 else x
            py = 1 - y if k & 2 else y
            pc = 1 - c if k & 1 else c
            for a in range(n):
                out.append(pltpu.make_async_remote_copy(
                    src_ref=ins[a].at[_slot(px, py, pc)], dst_ref=outs[a].at[me],
                    send_sem=send_sems.at[a, k - 1], recv_sem=recv_sems.at[a, k - 1],
                    device_id=(px, py, pc), device_id_type=MESH))
        return out

    def start():
        for cp in copies():
            cp.start()

    def finish():
        for cp in copies():
            cp.wait()

    return [start, finish]


def _pallas(body, *, name, grid, in_specs, out_specs, out_shape, args, scratch_shapes=(), ride=None):
    if ride is None:
        outs = pl.pallas_call(body, name=name, grid=grid, in_specs=in_specs, out_specs=out_specs, out_shape=out_shape,
                              scratch_shapes=list(scratch_shapes))(*args)
        return list(outs), []
    kind, arrays = ride
    n, n_in, n_out, n_scr = len(arrays), len(in_specs), len(out_specs), len(scratch_shapes)
    total = math.prod(grid)
    middle = (9 * total) // 10
    landed_shape = [S((N_SHARDS,) + a.shape if kind == "gather" else a.shape, a.dtype) for a in arrays]

    def with_exchange(*refs):
        ins, riders_in = refs[:n_in], refs[n_in:n_in + n]
        outs, riders_out = refs[n_in + n:n_in + n + n_out], refs[n_in + n + n_out:n_in + 2 * n + n_out]
        scratch, sems = refs[n_in + 2 * n + n_out:n_in + 2 * n + n_out + n_scr], refs[n_in + 2 * n + n_out + n_scr:]
        step = 0
        for axis, size in enumerate(grid):
            step = step * size + pl.program_id(axis)
        phases = (_gather_phases if kind == "gather" else _scatter_phases)(riders_in, riders_out, *sems)
        pl.when(step == 0)(phases[0])
        body(*ins, *outs, *scratch)
        for phase in phases[1:-1]:
            pl.when(step == middle)(phase)
        pl.when(step == total - 1)(phases[-1])

    outs = pl.pallas_call(
        with_exchange, name=name, grid=grid,
        in_specs=list(in_specs) + [ANY] * n, out_specs=list(out_specs) + [ANY] * n,
        out_shape=list(out_shape) + landed_shape,
        scratch_shapes=list(scratch_shapes) + [pltpu.SemaphoreType.DMA((n, 7)), pltpu.SemaphoreType.DMA((n, 7)),
                                               pltpu.SemaphoreType.DMA((n,))],
    )(*args, *arrays)
    return list(outs[:n_out]), list(outs[n_out:])


def _exchange_alone(kind, arrays, name):
    return _pallas(lambda: None, name=name, grid=(1,), in_specs=[], out_specs=[], out_shape=[], args=[], ride=(kind, arrays))[1]


def _prenorm(x, g, ride=None):
    t = x.shape[0]
    tm = min(TOKEN_TILE, t)

    def body(x_ref, g_ref, n_ref):
        n_ref[...] = _rms(x_ref[...], g_ref[...]).astype(BF)

    return _pallas(
        body, name="prenorm", grid=(t // tm,), ride=ride,
        in_specs=[_rows(tm, D_MODEL), _const_spec((1, D_MODEL))], out_specs=[_rows(tm, D_MODEL)],
        out_shape=[S((t, D_MODEL), BF)], args=[x, g])


def _ffn_up(n, w_in, name, ride=None):
    t = n.shape[0]
    tm = min(TOKEN_TILE, t)

    def body(n_ref, win_ref, act_ref, to_gate_ref, to_up_ref):
        nb = n_ref[...]
        for c in range(N_FF_CHUNKS):
            gate = _dot(nb, win_ref[c])
            up = _dot(nb, win_ref[N_FF_CHUNKS + c])
            sg = jax.nn.sigmoid(gate)
            silu = gate * sg
            act_ref[c] = (silu * up).astype(BF)
            to_gate_ref[c] = (up * (sg * (1.0 + gate * (1.0 - sg)))).astype(BF)
            to_up_ref[c] = silu.astype(BF)

    return _pallas(
        body, name=name, grid=(t // tm,), ride=ride,
        in_specs=[_rows(tm, D_MODEL), _const_spec(w_in.shape)],
        out_specs=[_chunks(tm)] * 3, out_shape=[S((N_FF_CHUNKS, t, FF_CHUNK), BF)] * 3,
        args=[n, w_in])


def _ffn_down(h, act, w_out, g_next, name, ride=None):
    t = h.shape[0]
    tm = min(TOKEN_TILE, t)

    def body(h_ref, act_ref, wout_ref, g_ref, ho_ref, no_ref):
        acc = jnp.zeros((tm, D_MODEL), F32)
        for c in range(N_FF_CHUNKS):
            acc = acc + _dot(act_ref[c], wout_ref[c])
        ho = h_ref[...] + 0.5 * acc
        ho_ref[...] = ho
        no_ref[...] = _rms(ho, g_ref[...]).astype(BF)

    return _pallas(
        body, name=name, grid=(t // tm,), ride=ride,
        in_specs=[_rows(tm, D_MODEL), _chunks(tm), _const_spec(w_out.shape), _const_spec((1, D_MODEL))],
        out_specs=[_rows(tm, D_MODEL)] * 2, out_shape=[S((t, D_MODEL), F32), S((t, D_MODEL), BF)],
        args=[h, act, w_out, g_next])


def _mix_proj(u, w_mix, ride=None):
    t = u.shape[0]
    tm = min(PROJ_TILE, t)

    def body(u_ref, w_ref, o_ref):
        o_ref[0] = _dot(u_ref[...], w_ref[0]).astype(BF)

    return _pallas(
        body, name="mix_proj", grid=(N_MIX, t // tm), ride=ride,
        in_specs=[pl.BlockSpec((tm, D_MODEL), lambda d, i: (i, 0)), pl.BlockSpec((1, D_MODEL, D_MODEL), lambda d, i: (d, 0, 0))],
        out_specs=[pl.BlockSpec((1, tm, D_MODEL), lambda d, i: (d, i, 0))],
        out_shape=[S((N_MIX, t, D_MODEL), BF)], args=[u, w_mix])


HALO = 16


def _piece(d, tm):
    return pl.BlockSpec((1, tm, D_MODEL), lambda i: (d, i, 0))


def _prev_halo(d, tm):
    return pl.BlockSpec((1, HALO, D_MODEL), lambda i: (d, jnp.maximum(i * (tm // HALO) - 1, 0), 0))


def _shift_down(m, prev_tail, k):
    tm = m.shape[0]
    out = pltpu.roll(m, k, 0)
    row = lax.broadcasted_iota(jnp.int32, (tm, 1), 0)
    for j in range(k):
        out = jnp.where(row == j, prev_tail[HALO - k + j:HALO - k + j + 1, :], out)
    return out


def _conv_inputs(cc_ref, cx_ref, cch_ref, cxh_ref):
    m = cc_ref[0].astype(F32) * cx_ref[0].astype(F32)
    mh = cch_ref[0].astype(F32) * cxh_ref[0].astype(F32)
    mh = jnp.where(pl.program_id(0) == 0, 0.0, mh)
    return m, _shift_down(m, mh, 1), _shift_down(m, mh, 2)


def _mixer_out(proj, o, h1, conv_w, w_co, w_ao, w_mo, g_next):
    t = h1.shape[0]
    tm = min(TOKEN_TILE, t)

    def body(cb_ref, cc_ref, cx_ref, gc_ref, ga_ref, cch_ref, cxh_ref, o_ref, h_ref, cw_ref, wco_ref, wao_ref, wmo_ref,
             g_ref, ho_ref, no_ref, ycin_ref, yc_ref, ya_ref, mg_ref):
        m, m1, m2 = _conv_inputs(cc_ref, cx_ref, cch_ref, cxh_ref)
        cw = cw_ref[...]
        cv = cw[0:1, :] * m2 + cw[1:2, :] * m1 + cw[2:3, :] * m
        ycin = (cb_ref[0].astype(F32) * cv).astype(BF)
        ycin_ref[...] = ycin
        yc = _dot(ycin, wco_ref[...])
        ya = _dot(o_ref[...], wao_ref[...])
        yc_ref[...] = yc.astype(BF)
        ya_ref[...] = ya.astype(BF)
        merged = (jax.nn.sigmoid(gc_ref[0].astype(F32)) * yc + jax.nn.sigmoid(ga_ref[0].astype(F32)) * ya).astype(BF)
        mg_ref[...] = merged
        ho = h_ref[...] + _dot(merged, wmo_ref[...])
        ho_ref[...] = ho
        no_ref[...] = _rms(ho, g_ref[...]).astype(BF)

    sq = (D_MODEL, D_MODEL)
    return pl.pallas_call(
        body, name="mixer_out", grid=(t // tm,),
        in_specs=[_piece(0, tm), _piece(1, tm), _piece(2, tm), _piece(6, tm), _piece(7, tm), _prev_halo(1, tm), _prev_halo(2, tm),
                  _rows(tm, D_MODEL), _rows(tm, D_MODEL), _const_spec((3, D_MODEL)), _const_spec(sq), _const_spec(sq),
                  _const_spec(sq), _const_spec((1, D_MODEL))],
        out_specs=[_rows(tm, D_MODEL)] * 6,
        out_shape=[S((t, D_MODEL), F32)] + [S((t, D_MODEL), BF)] * 5,
    )(proj, proj, proj, proj, proj, proj, proj, o, h1, conv_w, w_co, w_ao, w_mo, g_next)


def _suffix_sums(vals, tri, before):
    out, right = [], before
    for b in reversed(range(ATTN_K // ATTN_SUB)):
        v = vals[:, b * ATTN_SUB:(b + 1) * ATTN_SUB]
        hi = v.astype(BF)
        lo = (v - hi.astype(F32)).astype(BF)
        out.append(_dot(hi, tri) + _dot(lo, tri) + right)
        right = right + jnp.sum(v, axis=1, keepdims=True)
    return jnp.concatenate(out[::-1], axis=1), right


ATTN_UNITS = ATTN_ROWS // ATTN_Q


def _unit_rows(x, u):
    return x[u * ATTN_Q:(u + 1) * ATTN_Q]


def _per_unit(fn):
    return jnp.concatenate([fn(u) for u in range(ATTN_UNITS)], axis=0)


def _per_row(vals):
    local = lax.broadcasted_iota(jnp.int32, (ATTN_ROWS, 1), 0)
    out = jnp.full((ATTN_ROWS, 1), vals[0], jnp.int32)
    for u in range(1, ATTN_UNITS):
        out = jnp.where(local >= u * ATTN_Q, vals[u], out)
    return out


def _attn_step(q, k_ref, starts, bounds, row):
    z = _per_unit(lambda u: _dot_nt(_unit_rows(q, u), k_ref[0, pl.ds(starts[u], ATTN_K), :])) * (1.0 / math.sqrt(HEAD_DIM))
    mask = lax.broadcasted_iota(jnp.int32, (1, ATTN_K), 1) < jnp.minimum(row, _per_row(bounds)) - _per_row(starts)
    log_beta = jnp.minimum(z, 0.0) - jnp.log(1.0 + jnp.exp(jnp.minimum(z, -z)))
    log_rest = jnp.where(mask, log_beta - z, 0.0)
    return z, mask, log_beta, log_rest


def _attn_sweep_start(i, t):
    blks = tuple(jnp.maximum(i * ATTN_UNITS + u + 1 - ATTN_K // ATTN_SUB, 0) for u in range(ATTN_UNITS))
    return blks, tuple(jnp.int32(t) for _ in range(ATTN_UNITS))


def _attn_keys(blks):
    return [pl.multiple_of(b * ATTN_SUB, ATTN_SUB) for b in blks]


def _attn_next(blks):
    return tuple(jnp.maximum(b - ATTN_K // ATTN_SUB, 0) for b in blks), tuple(b * ATTN_SUB for b in blks)


def _attn_more(carry):
    return jnp.logical_and(carry[1][ATTN_UNITS - 1] > 0, carry[-1] > ATTN_SKIP_BELOW)


def _tri(strict):
    r = lax.broadcasted_iota(jnp.int32, (ATTN_SUB, ATTN_SUB), 0)
    c = lax.broadcasted_iota(jnp.int32, (ATTN_SUB, ATTN_SUB), 1)
    return (r > c if strict else r >= c).astype(BF)


REACH_TILE = (8, 128)


def _first_step_spec():
    return pl.BlockSpec((1, ATTN_ROWS, ATTN_K), lambda h, i: (h, i, 0))


def _reach_spec():
    return pl.BlockSpec((1, 1) + REACH_TILE, lambda h, i: (h, i, 0, 0))


def _head_cols(piece):
    return lambda t: pl.BlockSpec((1, t, HEAD_DIM), lambda h, i: (piece, 0, h))


def _attn_fwd(proj):
    t = proj.shape[1]
    nq = t // ATTN_ROWS
    tri = _tri(strict=True)

    def body(q_ref, k_ref, v_ref, tri_ref, o_ref, ob_ref, a_ref, beta_ref, reach_ref):
        i = pl.program_id(1)
        q = q_ref[0]
        row = i * ATTN_ROWS + lax.broadcasted_iota(jnp.int32, (ATTN_ROWS, 1), 0)

        def step(carry, keep=False):
            blks, bounds, acc, run, _ = carry
            starts = _attn_keys(blks)
            _, mask, log_beta, log_rest = _attn_step(q, k_ref, starts, bounds, row)
            tail, run = _suffix_sums(log_rest, tri_ref[...], run)
            a = jnp.where(mask, jnp.exp(log_beta + tail), 0.0).astype(BF)
            if keep:
                a_ref[0] = a
                beta_ref[0] = jnp.where(mask, jnp.exp(log_beta), 0.0).astype(BF)
            acc = acc + _per_unit(lambda u: _dot(_unit_rows(a, u), v_ref[0, pl.ds(starts[u], ATTN_K), :]))
            return (*_attn_next(blks), acc, run, jnp.max(run))

        first = (*_attn_sweep_start(i, t), jnp.zeros((ATTN_ROWS, HEAD_DIM), F32), jnp.zeros((ATTN_ROWS, 1), F32), jnp.float32(0.0))
        after_first = step(first, keep=True)
        reach_ref[...] = jnp.full(reach_ref.shape, after_first[-1], F32)
        o = lax.while_loop(_attn_more, step, after_first)[2]
        o_ref[...] = o
        ob_ref[...] = o.astype(BF)

    qspec = pl.BlockSpec((1, ATTN_ROWS, HEAD_DIM), lambda h, i: (3, i, h))
    rowblk = pl.BlockSpec((ATTN_ROWS, HEAD_DIM), lambda h, i: (i, h))
    return pl.pallas_call(
        body, name="attn_fwd", grid=(N_HEADS, nq),
        in_specs=[qspec, _head_cols(4)(t), _head_cols(5)(t), pl.BlockSpec((ATTN_SUB, ATTN_SUB), lambda h, i: (0, 0))],
        out_specs=[rowblk, rowblk, _first_step_spec(), _first_step_spec(), _reach_spec()],
        out_shape=[S((t, D_MODEL), F32), S((t, D_MODEL), BF), S((N_HEADS, t, ATTN_K), BF), S((N_HEADS, t, ATTN_K), BF),
                   S((N_HEADS, nq) + REACH_TILE, F32)],
    )(proj, proj, proj, tri)


def _attn_bwd(proj, o, d_o, a_first, beta_first, reach, ride=None):
    t = proj.shape[1]
    nq = t // ATTN_ROWS
    tri_strict, tri_incl = _tri(strict=True), _tri(strict=False)
    scale = 1.0 / math.sqrt(HEAD_DIM)

    def body(q_ref, k_ref, v_ref, o_ref, do_ref, a_ref, beta_ref, reach_ref, tris_ref, trii_ref, dq_ref, dk_ref, dv_ref, dk_acc, dv_acc):
        i = pl.program_id(1)

        @pl.when(i == 0)
        def _():
            dk_acc[...] = jnp.zeros_like(dk_acc)
            dv_acc[...] = jnp.zeros_like(dv_acc)

        q = q_ref[0]
        do = do_ref[...]
        total = jnp.sum(do.astype(F32) * o_ref[...], axis=1, keepdims=True)
        zero = jnp.zeros((ATTN_ROWS, 1), F32)
        blks0, bounds0 = _attn_sweep_start(i, t)

        def finish(starts, a, dz, dq):
            dzb = (dz * scale).astype(BF)
            for u in range(ATTN_UNITS):
                dv_acc[pl.ds(starts[u], ATTN_K), :] += _dot_tn(_unit_rows(a, u), _unit_rows(do, u))
                dk_acc[pl.ds(starts[u], ATTN_K), :] += _dot_tn(_unit_rows(dzb, u), _unit_rows(q, u))
            return dq + _per_unit(lambda u: _dot(_unit_rows(dzb, u), k_ref[0, pl.ds(starts[u], ATTN_K), :]))

        def grad_a(starts, a):
            return _per_unit(lambda u: _dot_nt(_unit_rows(do, u), v_ref[0, pl.ds(starts[u], ATTN_K), :])) * a.astype(F32)

        one_step = jnp.max(reach_ref[...]) <= ATTN_SKIP_BELOW

        @pl.when(one_step)
        def _():
            starts = _attn_keys(blks0)
            a = a_ref[0]
            beta = beta_ref[0].astype(F32)
            de = grad_a(starts, a)
            right, _ = _suffix_sums(de, trii_ref[...], zero)
            dz = de * (1.0 - beta) - (total - right) * beta
            dq_ref[...] = finish(starts, a, dz, jnp.zeros((ATTN_ROWS, HEAD_DIM), F32)).astype(BF)

        @pl.when(jnp.logical_not(one_step))
        def _():
            row = i * ATTN_ROWS + lax.broadcasted_iota(jnp.int32, (ATTN_ROWS, 1), 0)

            def step(carry):
                blks, bounds, dq, seen, run, _ = carry
                starts = _attn_keys(blks)
                z, mask, log_beta, log_rest = _attn_step(q, k_ref, starts, bounds, row)
                tail, run = _suffix_sums(log_rest, tris_ref[...], run)
                a = jnp.where(mask, jnp.exp(log_beta + tail), 0.0).astype(BF)
                de = grad_a(starts, a)
                right, seen = _suffix_sums(de, trii_ref[...], seen)
                beta = jax.nn.sigmoid(z)
                dz = jnp.where(mask, de * (1.0 - beta) - (total - right) * beta, 0.0)
                return (*_attn_next(blks), finish(starts, a, dz, dq), seen, run, jnp.max(run))

            first = (blks0, bounds0, jnp.zeros((ATTN_ROWS, HEAD_DIM), F32), zero, zero, jnp.float32(0.0))
            dq_ref[...] = lax.while_loop(_attn_more, step, step(first))[2].astype(BF)

        @pl.when(i == nq - 1)
        def _():
            dk_ref[...] = dk_acc[...].astype(BF)
            dv_ref[...] = dv_acc[...].astype(BF)

    qspec = pl.BlockSpec((1, ATTN_ROWS, HEAD_DIM), lambda h, i: (3, i, h))
    rowblk = pl.BlockSpec((ATTN_ROWS, HEAD_DIM), lambda h, i: (i, h))
    head = pl.BlockSpec((t, HEAD_DIM), lambda h, i: (0, h))
    trispec = pl.BlockSpec((ATTN_SUB, ATTN_SUB), lambda h, i: (0, 0))
    return _pallas(
        body, name="attn_bwd", grid=(N_HEADS, nq), ride=ride,
        in_specs=[qspec, _head_cols(4)(t), _head_cols(5)(t), rowblk, rowblk, _first_step_spec(), _first_step_spec(), _reach_spec(),
                  trispec, trispec],
        out_specs=[rowblk, head, head],
        out_shape=[S((t, D_MODEL), BF)] * 3,
        scratch_shapes=[pltpu.VMEM((t, HEAD_DIM), F32), pltpu.VMEM((t, HEAD_DIM), F32)],
        args=[proj, proj, proj, o, d_o, a_first, beta_first, reach, tri_strict, tri_incl])


def _tail(h3, n4, p, w_pg, w_pp, g_ple, g_final, target):
    t = h3.shape[0]
    tm = min(TOKEN_TILE, t)
    steps = t // tm

    def body(h_ref, n_ref, p_ref, wpg_ref, wpp_ref, gp_ref, gf_ref, tgt_ref,
             dh_ref, ds_ref, dpp_ref, loss_ref, dgf_ref, dgp_ref):
        pg = jax.nn.sigmoid(_dot(n_ref[...], wpg_ref[...]))
        pp = _dot(p_ref[...].astype(BF), wpp_ref[...])
        h3v = h_ref[...]
        h4 = h3v + pg * pp
        gf = gf_ref[...]
        diff = _rms(h4, gf) - tgt_ref[...]
        _accumulate(loss_ref, jnp.sum(diff * diff, axis=0, keepdims=True))
        dh4, dgf = _rms_bwd(diff * (1.0 / D_MODEL), h4, gf)
        _accumulate(dgf_ref, dgf)
        dpp_ref[...] = (dh4 * pg).astype(BF)
        ds = (dh4 * pp * pg * (1.0 - pg)).astype(BF)
        ds_ref[...] = ds
        dh3, dgp = _rms_bwd(_dot_nt(ds, wpg_ref[...]), h3v, gp_ref[...])
        _accumulate(dgp_ref, dgp)
        dh_ref[...] = dh4 + dh3

        @pl.when(pl.program_id(0) == steps - 1)
        def _():
            loss_ref[...] = jnp.full(loss_ref.shape, 0.5 / D_MODEL * jnp.sum(loss_ref[...]), F32)

    vec = (1, D_MODEL)
    return pl.pallas_call(
        body, name="tail", grid=(steps,),
        in_specs=[_rows(tm, D_MODEL), _rows(tm, D_MODEL), _rows(tm, PLE_DIM), _const_spec((D_MODEL, D_MODEL)),
                  _const_spec((PLE_DIM, D_MODEL)), _const_spec(vec), _const_spec(vec), _rows(tm, D_MODEL)],
        out_specs=[_rows(tm, D_MODEL)] * 3 + [_acc_spec(vec)] * 3,
        out_shape=[S((t, D_MODEL), F32), S((t, D_MODEL), BF), S((t, D_MODEL), BF)] + [S(vec, F32)] * 3,
    )(h3, n4, p, w_pg, w_pp, g_ple, g_final, target)


def _wgrad(xs, ys, name, ride=None, tile=None):
    bx, t, k = xs.shape
    by, _, n = ys.shape
    b = max(bx, by)
    tt = min(tile or WGRAD_TILE * 2 // xs.dtype.itemsize, t)
    steps = t // tt

    def body(x_ref, y_ref, o_ref, acc_ref):
        s = pl.program_id(1)

        @pl.when(s == 0)
        def _():
            acc_ref[...] = jnp.zeros_like(acc_ref)
        acc_ref[...] += _dot_tn(x_ref[0].astype(BF), y_ref[0].astype(BF))

        @pl.when(s == steps - 1)
        def _():
            o_ref[0] = acc_ref[...].astype(BF)

    (out,), landed = _pallas(
        body, name=name, grid=(b, steps), ride=ride,
        in_specs=[pl.BlockSpec((1, tt, k), (lambda j, s: (j, s, 0)) if bx > 1 else (lambda j, s: (0, s, 0))),
                  pl.BlockSpec((1, tt, n), (lambda j, s: (j, s, 0)) if by > 1 else (lambda j, s: (0, s, 0)))],
        out_specs=[pl.BlockSpec((1, k, n), lambda j, s: (j, 0, 0))],
        out_shape=[S((b, k, n), BF)],
        scratch_shapes=[pltpu.VMEM((k, n), F32)],
        args=[xs, ys])
    return (out, landed) if ride is not None else out


def _wgrad_pieces(x, ys, name, ride=None, tile=None, row_parts=1, transposed=False):
    t, k = x.shape
    n = ys[0].shape[2]
    counts = [y.shape[0] for y in ys]
    offsets = [sum(counts[:j]) for j in range(len(ys))]
    total = sum(counts)
    tt = min(tile or WGRAD_TILE, t)
    steps = t // tt
    rows, cols = (n, k) if transposed else (k, n)
    kp = rows // row_parts

    def body(x_ref, *refs):
        y_refs, o_refs, acc_ref = refs[:len(ys)], refs[len(ys):len(ys) + row_parts], refs[len(ys) + row_parts]
        p, s = pl.program_id(0), pl.program_id(1)

        @pl.when(s == 0)
        def _():
            acc_ref[...] = jnp.zeros_like(acc_ref)
        for j, y_ref in enumerate(y_refs):
            @pl.when(jnp.logical_and(p >= offsets[j], p < offsets[j] + counts[j]))
            def _(y_ref=y_ref):
                acc_ref[...] += _dot_tn(y_ref[0], x_ref[...]) if transposed else _dot_tn(x_ref[...], y_ref[0])

        @pl.when(s == steps - 1)
        def _():
            for part, o_ref in enumerate(o_refs):
                o_ref[0] = acc_ref[part * kp:(part + 1) * kp, :].astype(BF)

    def turn(j):
        lo, hi = offsets[j], offsets[j] + counts[j]
        return lambda p, s: (jnp.clip(p - lo, 0, counts[j] - 1), jnp.where(p < lo, 0, jnp.where(p >= hi, steps - 1, s)), 0)

    outs, landed = _pallas(
        body, name=name, grid=(total, steps), ride=ride,
        in_specs=[pl.BlockSpec((tt, k), lambda p, s: (s, 0))] + [pl.BlockSpec((1, tt, n), turn(j)) for j in range(len(ys))],
        out_specs=[pl.BlockSpec((1, kp, cols), lambda p, s: (p, 0, 0))] * row_parts,
        out_shape=[S((total, kp, cols), BF)] * row_parts,
        scratch_shapes=[pltpu.VMEM((rows, cols), F32)],
        args=[x, *ys])
    out = outs[0] if row_parts == 1 else outs
    return (out, landed) if ride is not None else out


def _ffn_bwd_hidden(dh, to_gate, to_up, w_out, name, ride=None):
    t = dh.shape[0]
    tm = min(TOKEN_TILE, t)

    def body(dh_ref, to_gate_ref, to_up_ref, wout_ref, df_ref, dgate_ref, dup_ref):
        df = (0.5 * dh_ref[...]).astype(BF)
        df_ref[...] = df
        for c in range(N_FF_CHUNKS):
            dact = _dot_nt(df, wout_ref[c])
            dgate_ref[c] = (dact * to_gate_ref[c].astype(F32)).astype(BF)
            dup_ref[c] = (dact * to_up_ref[c].astype(F32)).astype(BF)

    return _pallas(
        body, name=name, grid=(t // tm,), ride=ride,
        in_specs=[_rows(tm, D_MODEL), _chunks(tm), _chunks(tm), _const_spec(w_out.shape)],
        out_specs=[_rows(tm, D_MODEL), _chunks(tm), _chunks(tm)],
        out_shape=[S((t, D_MODEL), BF)] + [S((N_FF_CHUNKS, t, FF_CHUNK), BF)] * 2,
        args=[dh, to_gate, to_up, w_out])


def _ffn_bwd_input(dh, h_in, g, dgate, dup, w_in, name, ride=None):
    t = dh.shape[0]
    tm = min(TOKEN_TILE, t)

    def body(dh_ref, h_ref, g_ref, dgate_ref, dup_ref, win_ref, dhi_ref, dg_ref):
        dn = jnp.zeros((tm, D_MODEL), F32)
        for c in range(N_FF_CHUNKS):
            dn = dn + _dot_nt(dgate_ref[c], win_ref[c]) + _dot_nt(dup_ref[c], win_ref[N_FF_CHUNKS + c])
        dhi, dg = _rms_bwd(dn, h_ref[...], g_ref[...])
        _accumulate(dg_ref, dg)
        dhi_ref[...] = dh_ref[...] + dhi

    vec = (1, D_MODEL)
    return _pallas(
        body, name=name, grid=(t // tm,), ride=ride,
        in_specs=[_rows(tm, D_MODEL), _rows(tm, D_MODEL), _const_spec(vec), _chunks(tm), _chunks(tm), _const_spec(w_in.shape)],
        out_specs=[_rows(tm, D_MODEL), _acc_spec(vec)],
        out_shape=[S((t, D_MODEL), F32), S(vec, F32)],
        args=[dh, h_in, g, dgate, dup, w_in])


def _mixer_bwd(dh2, proj, yc, ya, conv_w, w_co, w_ao, w_mo, ride=None):
    t = dh2.shape[0]
    tm = min(TOKEN_TILE, t)

    def body(dh_ref, cb_ref, cc_ref, cx_ref, gc_ref, ga_ref, cch_ref, cxh_ref, yc_ref, ya_ref, cw_ref, wco_ref, wao_ref, wmo_ref,
             dhb_ref, dyc_ref, dya_ref, dgc_ref, dga_ref, dcb_ref, dcv_ref, do_ref):
        dhb = dh_ref[...].astype(BF)
        dhb_ref[...] = dhb
        dmerged = _dot_nt(dhb, wmo_ref[...])
        sc = jax.nn.sigmoid(gc_ref[0].astype(F32))
        sa = jax.nn.sigmoid(ga_ref[0].astype(F32))
        dyc = (dmerged * sc).astype(BF)
        dya = (dmerged * sa).astype(BF)
        dyc_ref[...] = dyc
        dya_ref[...] = dya
        dgc_ref[...] = (dmerged * yc_ref[...].astype(F32) * sc * (1.0 - sc)).astype(BF)
        dga_ref[...] = (dmerged * ya_ref[...].astype(F32) * sa * (1.0 - sa)).astype(BF)
        m, m1, m2 = _conv_inputs(cc_ref, cx_ref, cch_ref, cxh_ref)
        cw = cw_ref[...]
        cv = cw[0:1, :] * m2 + cw[1:2, :] * m1 + cw[2:3, :] * m
        dycin = _dot_nt(dyc, wco_ref[...])
        dcb_ref[...] = (dycin * cv).astype(BF)
        dcv_ref[...] = (dycin * cb_ref[0].astype(F32)).astype(BF)
        do_ref[...] = _dot_nt(dya, wao_ref[...]).astype(BF)

    sq = (D_MODEL, D_MODEL)
    return _pallas(
        body, name="mixer_bwd", grid=(t // tm,), ride=ride,
        in_specs=[_rows(tm, D_MODEL), _piece(0, tm), _piece(1, tm), _piece(2, tm), _piece(6, tm), _piece(7, tm),
                  _prev_halo(1, tm), _prev_halo(2, tm), _rows(tm, D_MODEL), _rows(tm, D_MODEL),
                  _const_spec((3, D_MODEL)), _const_spec(sq), _const_spec(sq), _const_spec(sq)],
        out_specs=[_rows(tm, D_MODEL)] * 8,
        out_shape=[S((t, D_MODEL), BF)] * 8,
        args=[dh2, proj, proj, proj, proj, proj, proj, proj, yc, ya, conv_w, w_co, w_ao, w_mo])


TAP_ROWS = 8


def _conv_bwd(dcv, proj, conv_w):
    t = dcv.shape[0]
    tm = min(TOKEN_TILE, t)
    steps = t // tm

    def body(dcv_ref, nxt_ref, cc_ref, cx_ref, cch_ref, cxh_ref, cw_ref, dcc_ref, dcx_ref, dw_ref):
        i = pl.program_id(0)
        m, m1, m2 = _conv_inputs(cc_ref, cx_ref, cch_ref, cxh_ref)
        d0 = dcv_ref[...].astype(F32)
        nxt = jnp.where(i == steps - 1, 0.0, nxt_ref[...].astype(F32))
        row = lax.broadcasted_iota(jnp.int32, (tm, 1), 0)
        d1 = jnp.where(row == tm - 1, nxt[0:1, :], pltpu.roll(d0, tm - 1, 0))
        d2 = pltpu.roll(d0, tm - 2, 0)
        d2 = jnp.where(row == tm - 2, nxt[0:1, :], jnp.where(row == tm - 1, nxt[1:2, :], d2))
        cw = cw_ref[...]
        dm = cw[2:3, :] * d0 + cw[1:2, :] * d1 + cw[0:1, :] * d2
        dcc_ref[...] = (dm * cx_ref[0].astype(F32)).astype(BF)
        dcx_ref[...] = (dm * cc_ref[0].astype(F32)).astype(BF)
        tap_row = lax.broadcasted_iota(jnp.int32, (TAP_ROWS, 1), 0)
        dw = jnp.zeros((TAP_ROWS, D_MODEL), F32)
        for j, mk in enumerate((m2, m1, m)):
            dw = jnp.where(tap_row == j, jnp.sum(d0 * mk, axis=0, keepdims=True), dw)
        _accumulate(dw_ref, dw)

    nxt_spec = pl.BlockSpec((HALO, D_MODEL), lambda i: (jnp.minimum((i + 1) * (tm // HALO), t // HALO - 1), 0))
    return pl.pallas_call(
        body, name="conv_bwd", grid=(steps,),
        in_specs=[_rows(tm, D_MODEL), nxt_spec, _piece(1, tm), _piece(2, tm), _prev_halo(1, tm), _prev_halo(2, tm),
                  _const_spec((3, D_MODEL))],
        out_specs=[_rows(tm, D_MODEL), _rows(tm, D_MODEL), _acc_spec((TAP_ROWS, D_MODEL))],
        out_shape=[S((t, D_MODEL), BF), S((t, D_MODEL), BF), S((TAP_ROWS, D_MODEL), F32)],
    )(dcv, dcv, proj, proj, proj, proj, conv_w)


def _mix_bwd(dpieces, w_mix, h1, dh2, g, ride=None):
    t = h1.shape[0]
    tm = min(TOKEN_TILE, t)

    def body(*refs):
        pieces, (w_ref, h_ref, dh_ref, g_ref, dhi_ref, dg_ref) = refs[:N_MIX], refs[N_MIX:]
        du = jnp.zeros((tm, D_MODEL), F32)
        for d in range(N_MIX):
            du = du + _dot_nt(pieces[d][...], w_ref[d])
        dhi, dg = _rms_bwd(du, h_ref[...], g_ref[...])
        _accumulate(dg_ref, dg)
        dhi_ref[...] = dh_ref[...] + dhi

    vec = (1, D_MODEL)
    return _pallas(
        body, name="mix_bwd", grid=(t // tm,), ride=ride,
        in_specs=[_rows(tm, D_MODEL)] * N_MIX + [_const_spec(w_mix.shape), _rows(tm, D_MODEL), _rows(tm, D_MODEL), _const_spec(vec)],
        out_specs=[_rows(tm, D_MODEL), _acc_spec(vec)],
        out_shape=[S((t, D_MODEL), F32), S(vec, F32)],
        args=[*dpieces, w_mix, h1, dh2, g])


def _adamw(partials, w, m, v, name):
    parts = list(partials) if isinstance(partials, (list, tuple)) else [partials]
    r, c = w.shape
    tr = next(d for d in (r, 512, 352, 256) if d <= 512 // len(parts) and r % d == 0)
    first_tile = [sum(p.shape[1] for p in parts[:j]) // tr for j in range(len(parts))]
    c1 = 1.0 - ADAM_B1 ** ADAM_STEP
    c2 = 1.0 - ADAM_B2 ** ADAM_STEP

    def body(*refs):
        p_refs, (w_ref, m_ref, v_ref, g_ref, d_ref, mo_ref, vo_ref) = refs[:len(parts)], refs[len(parts):]
        g = None
        for j, p_ref in enumerate(p_refs):
            gj = p_ref[0].astype(F32)
            for s in range(1, N_SHARDS):
                gj = gj + p_ref[s].astype(F32)
            g = gj if g is None else jnp.where(pl.program_id(0) >= first_tile[j], gj, g)
        mn = ADAM_B1 * m_ref[...] + (1.0 - ADAM_B1) * g
        vn = ADAM_B2 * v_ref[...] + (1.0 - ADAM_B2) * (g * g)
        g_ref[...] = g
        mo_ref[...] = mn
        vo_ref[...] = vn
        d_ref[...] = -ADAM_LR * ((mn / c1) / (jnp.sqrt(vn / c2) + ADAM_EPS) + ADAM_WD * w_ref[...])

    def rows_of(j):
        last = parts[j].shape[1] // tr - 1
        return lambda i: (0, jnp.clip(i - first_tile[j], 0, last), 0)

    blk = pl.BlockSpec((tr, c), lambda i: (i, 0))
    return pl.pallas_call(
        body, name=name, grid=(r // tr,),
        in_specs=[pl.BlockSpec((N_SHARDS, tr, c), rows_of(j)) for j in range(len(parts))] + [blk, blk, blk],
        out_specs=[blk] * 4, out_shape=[S((r, c), F32)] * 4,
    )(*parts, w, m, v)


_MATRICES = ("ffn1_w_in", "ffn1_w_out", "w_mix_in", "conv_w", "w_conv_out", "w_attn_out", "w_mix_out",
             "ffn2_w_in", "ffn2_w_out", "w_ple_gate", "w_ple_proj")
_GAINS = ("ffn1_norm", "mix_norm", "ffn2_norm", "ple_norm", "final_norm")
_WEIGHTS = ("ffn1_norm", "ffn1_w_in", "ffn1_w_out", "mix_norm", "w_mix_in", "conv_w", "w_conv_out", "w_attn_out", "w_mix_out",
            "ffn2_norm", "ffn2_w_in", "ffn2_w_out", "ple_norm", "w_ple_gate", "w_ple_proj", "final_norm")
CONV_ROWS = 8
_TRANSPOSED = ("ffn1_w_in", "ffn2_w_in")


def _columns_from_shards(g):
    return jnp.transpose(g, (1, 0, 2)).reshape(g.shape[1], N_SHARDS * g.shape[2])


def _shards_from_columns(a):
    r, c = a.shape
    return jnp.transpose(a.reshape(r, N_SHARDS, c // N_SHARDS), (1, 0, 2))


def kernel(x, p, ffn1_norm, ffn1_w_in, ffn1_w_out, mix_norm, w_mix_in, conv_w, w_conv_out, w_attn_out, w_mix_out, ffn2_norm, ffn2_w_in, ffn2_w_out, ple_norm, w_ple_gate, w_ple_proj, final_norm, loss_target, m_ffn1_norm, m_ffn1_w_in, m_ffn1_w_out, m_mix_norm, m_w_mix_in, m_conv_w, m_w_conv_out, m_w_attn_out, m_w_mix_out, m_ffn2_norm, m_ffn2_w_in, m_ffn2_w_out, m_ple_norm, m_w_ple_gate, m_w_ple_proj, m_final_norm, v_ffn1_norm, v_ffn1_w_in, v_ffn1_w_out, v_mix_norm, v_w_mix_in, v_conv_w, v_w_conv_out, v_w_attn_out, v_w_mix_out, v_ffn2_norm, v_ffn2_w_in, v_ffn2_w_out, v_ple_norm, v_w_ple_gate, v_w_ple_proj, v_final_norm):
    given = dict(locals())
    t = x.shape[1]
    xs = x.reshape(t, D_MODEL)
    ps = p.reshape(t, PLE_DIM)
    target = loss_target.reshape(t, D_MODEL)
    shard = {k: given[k].reshape(given[k].shape[-2:]) for k in _MATRICES}
    gain = {k: given[k].reshape(1, D_MODEL) for k in _GAINS}

    send = {k: shard[k].astype(BF) for k in _MATRICES}
    send["conv_w"] = jnp.pad(shard["conv_w"], ((0, CONV_ROWS - 3), (0, 0)))
    loss_vec, dx, landed, gain_grads = _forward_backward(xs, ps, target, gain, send)
    gain_rows = jnp.concatenate([gain_grads[k] for k in _GAINS] + [loss_vec, jnp.zeros((8 - len(_GAINS) - 1, D_MODEL), F32)], axis=0)
    gain_parts, = _exchange_alone("gather", [gain_rows], "gather_gain_gradients")

    out = {}
    for k in _MATRICES:
        w, m, v = shard[k], given["m_" + k].reshape(shard[k].shape), given["v_" + k].reshape(shard[k].shape)
        part = landed[k]
        if k == "conv_w":
            pad = ((0, CONV_ROWS - 3), (0, 0))
            w, m, v = jnp.pad(w, pad), jnp.pad(m, pad), jnp.pad(v, pad, constant_values=1.0)
        if k in _TRANSPOSED:
            w, m, v = w.T, m.T, v.T
        res = _adamw(part, w, m, v, "adamw_" + k)
        out[k] = [r[:3] if k == "conv_w" else (r.T if k in _TRANSPOSED else r) for r in res]
    stack = lambda pre: jnp.concatenate([given[pre + k].reshape(1, D_MODEL) for k in _GAINS] + [jnp.ones((8 - len(_GAINS), D_MODEL), F32)], axis=0)
    res = _adamw(gain_parts, stack(""), stack("m_"), stack("v_"), "adamw_gains")
    for j, k in enumerate(_GAINS):
        out[k] = [r[j:j + 1] for r in res]

    loss = jnp.sum(gain_parts[:, len(_GAINS), 0])
    per_kind = [[out[k][j].reshape(given[k].shape) for k in _WEIGHTS] for j in range(4)]
    return (loss, dx.reshape(x.shape), *per_kind[0], *per_kind[1], *per_kind[2], *per_kind[3])


def _forward_backward(xs, ps, target, gain, send, full=None):
    exchange = full is None
    full = dict(full or {})
    grads, landed = {}, {}

    def gather(names):
        return ("gather", [send[k] for k in names]) if exchange else None

    def scatter(names):
        return ("scatter", [grads[k] for k in names]) if exchange else None

    def keep(into, names, got):
        into.update(zip(names, got))

    first = ("ffn1_w_in",)
    (n1,), got = _prenorm(xs, gain["ffn1_norm"], ride=gather(first))
    keep(full, first, got)
    w1_in = full["ffn1_w_in"]
    second = ("ffn1_w_out", "w_mix_in")
    (act1, to_gate1, to_up1), got = _ffn_up(n1, w1_in, "ffn1_up", ride=gather(second))
    keep(full, second, got)
    w1_out = full["ffn1_w_out"].reshape(N_FF_CHUNKS, FF_CHUNK, D_MODEL)
    third = ("conv_w", "w_conv_out", "w_attn_out", "w_mix_out")
    (h1, u), got = _ffn_down(xs, act1, w1_out, gain["mix_norm"], "ffn1_down", ride=gather(third))
    keep(full, third, got)
    w_mix = full["w_mix_in"]
    w_co, w_ao, w_mo = (full[k].reshape(D_MODEL, D_MODEL) for k in ("w_conv_out", "w_attn_out", "w_mix_out"))
    taps = _columns_from_shards(full["conv_w"][:, :3, :])
    rest = ("ffn2_w_in", "ffn2_w_out", "w_ple_gate", "w_ple_proj")
    (proj,), got = _mix_proj(u, w_mix, ride=gather(rest))
    keep(full, rest, got)
    w2_in, w2_out = full["ffn2_w_in"], full["ffn2_w_out"].reshape(N_FF_CHUNKS, FF_CHUNK, D_MODEL)
    w_pg = full["w_ple_gate"].reshape(D_MODEL, D_MODEL)
    w_pp = _columns_from_shards(full["w_ple_proj"])
    o, o_bf, a_first, beta_first, reach = _attn_fwd(proj)
    h2, n3, ycin, yc, ya, merged = _mixer_out(proj, o_bf, h1, taps, w_co, w_ao, w_mo, gain["ffn2_norm"])
    (act2, to_gate2, to_up2), _ = _ffn_up(n3, w2_in, "ffn2_up")
    (h3, n4), _ = _ffn_down(h2, act2, w2_out, gain["ple_norm"], "ffn2_down")
    dh3, ds, dpp, loss_vec, dg_final, dg_ple = _tail(h3, n4, ps, w_pg, w_pp, gain["ple_norm"], gain["final_norm"], target)

    one = lambda a: a[None]
    by_rows = lambda g, rows: g.reshape(N_SHARDS, rows // N_SHARDS, D_MODEL)
    square = WGRAD_TILE // 2
    grads["w_ple_gate"] = by_rows(_wgrad(one(n4), one(ds), "wgrad_ple_gate", tile=square), D_MODEL)
    grads["w_ple_proj"] = _shards_from_columns(_wgrad(one(ps), one(dpp), "wgrad_ple_proj")[0])
    ple = ("w_ple_gate", "w_ple_proj")
    (df2, dgate2, dup2), got = _ffn_bwd_hidden(dh3, to_gate2, to_up2, w2_out, "ffn2_bwd_hidden", ride=scatter(ple))
    keep(landed, ple, got)
    grads["ffn2_w_out"] = by_rows(_wgrad(act2, one(df2), "wgrad_ffn2_out"), D_FF)
    grads["ffn2_w_in"] = _wgrad_pieces(n3, [dgate2, dup2], "wgrad_ffn2_in", transposed=True)
    (dh2, dg_ffn2), got = _ffn_bwd_input(dh3, h2, gain["ffn2_norm"], dgate2, dup2, w2_in, "ffn2_bwd_input", ride=scatter(("ffn2_w_out",)))
    keep(landed, ("ffn2_w_out",), got)
    (dh2b, dyc, dya, dgc, dga, dcb, dcv, d_o), _ = _mixer_bwd(dh2, proj, yc, ya, taps, w_co, w_ao, w_mo)
    grads["w_mix_out"] = by_rows(_wgrad(one(merged), one(dh2b), "wgrad_mix_out", tile=square), D_MODEL)
    grads["w_conv_out"] = by_rows(_wgrad(one(ycin), one(dyc), "wgrad_conv_out", tile=square), D_MODEL)
    grads["w_attn_out"] = by_rows(_wgrad(one(o_bf), one(dya), "wgrad_attn_out", tile=square), D_MODEL)
    dcc, dcx, dtaps = _conv_bwd(dcv, proj, taps)
    grads["conv_w"] = jnp.pad(_shards_from_columns(dtaps[:3]), ((0, 0), (0, CONV_ROWS - 3), (0, 0)))
    behind_attn = ("ffn2_w_in", "w_mix_out", "w_conv_out", "w_attn_out", "conv_w")
    (dq, dk, dv), got = _attn_bwd(proj, o, d_o, a_first, beta_first, reach, ride=scatter(behind_attn))
    keep(landed, behind_attn, got)
    dpieces = [dcb, dcc, dcx, dq, dk, dv, dgc, dga]
    half = N_MIX // 2
    tops, bottoms = zip(_wgrad_pieces(u, [one(dp) for dp in dpieces[:half]], "wgrad_mix_in_a", tile=WGRAD_TILE // 2, row_parts=2),
                        _wgrad_pieces(u, [one(dp) for dp in dpieces[half:]], "wgrad_mix_in_b", tile=WGRAD_TILE // 2, row_parts=2))
    grads["w_mix_in top"], grads["w_mix_in bottom"] = jnp.concatenate(tops, axis=0), jnp.concatenate(bottoms, axis=0)
    (dh1, dg_mix), top = _mix_bwd(dpieces, w_mix, h1, dh2, gain["mix_norm"], ride=scatter(("w_mix_in top",)))
    (df1, dgate1, dup1), bottom = _ffn_bwd_hidden(dh1, to_gate1, to_up1, w1_out, "ffn1_bwd_hidden", ride=scatter(("w_mix_in bottom",)))
    if exchange:
        landed["w_mix_in"] = [top[0], bottom[0]]
    else:
        grads["w_mix_in"] = jnp.concatenate([grads.pop("w_mix_in top"), grads.pop("w_mix_in bottom")], axis=1)
    grads["ffn1_w_out"] = by_rows(_wgrad(act1, one(df1), "wgrad_ffn1_out"), D_FF)
    if exchange:
        grads["ffn1_w_in"], got = _wgrad_pieces(n1, [dgate1, dup1], "wgrad_ffn1_in", transposed=True, ride=scatter(("ffn1_w_out",)))
        keep(landed, ("ffn1_w_out",), got)
    else:
        grads["ffn1_w_in"] = _wgrad_pieces(n1, [dgate1, dup1], "wgrad_ffn1_in", transposed=True)
    (dx, dg_ffn1), got = _ffn_bwd_input(dh1, xs, gain["ffn1_norm"], dgate1, dup1, w1_in, "ffn1_bwd_input", ride=scatter(("ffn1_w_in",)))
    keep(landed, ("ffn1_w_in",), got)
    gain_grads = dict(ffn1_norm=dg_ffn1, mix_norm=dg_mix, ffn2_norm=dg_ffn2, ple_norm=dg_ple, final_norm=dg_final)
    return loss_vec, dx, (landed if exchange else grads), gain_grads
```

```python
import functools
import math

import jax
import jax.numpy as jnp
from jax import lax
from jax.experimental import pallas as pl
from jax.experimental.pallas import tpu as pltpu

D_MODEL = 1024
D_FF = 2816
N_SHARDS = 8
FF_CHUNK = 2 * D_FF // N_SHARDS
N_FF_CHUNKS = D_FF // FF_CHUNK
N_HEADS = 8
HEAD_DIM = 128
PLE_DIM = 256
NORM_EPS = 1e-6
N_MIX = 8
ADAM_LR, ADAM_B1, ADAM_B2, ADAM_EPS, ADAM_WD, ADAM_STEP = 0.001, 0.9, 0.999, 1e-08, 0.01, 10

TOKEN_TILE = 512
WGRAD_TILE = 4096
PROJ_TILE = 2048
ATTN_ROWS = 1024
ATTN_Q = 128
ATTN_SUB = 128
ATTN_K = 3 * ATTN_SUB
ATTN_SKIP_BELOW = -90.0

BF = jnp.bfloat16
F32 = jnp.float32
MESH = pl.DeviceIdType.MESH
NT = (((1,), (1,)), ((), ()))
TN = (((0,), (0,)), ((), ()))
S = jax.ShapeDtypeStruct
ANY = pl.BlockSpec(memory_space=pl.ANY)


def _const_spec(shape):
    nd = len(shape)
    return pl.BlockSpec(shape, lambda *_: (0,) * nd, pipeline_mode=pl.Buffered(1))


def _rows(tm, cols):
    return pl.BlockSpec((tm, cols), lambda i: (i, 0))


def _chunks(tm):
    return pl.BlockSpec((N_FF_CHUNKS, tm, FF_CHUNK), lambda i: (0, i, 0))


def _acc_spec(shape):
    nd = len(shape)
    return pl.BlockSpec(shape, lambda *_: (0,) * nd)


def _dot(a, b):
    return jnp.dot(a, b, preferred_element_type=F32)


def _dot_nt(a, b):
    return lax.dot_general(a, b, NT, preferred_element_type=F32)


def _dot_tn(a, b):
    return lax.dot_general(a, b, TN, preferred_element_type=F32)


def _rms(h, g):
    r = lax.rsqrt(jnp.mean(h * h, axis=-1, keepdims=True) + NORM_EPS)
    return h * r * g


def _rms_bwd(dn, h, g):
    r = lax.rsqrt(jnp.mean(h * h, axis=-1, keepdims=True) + NORM_EPS)
    nh = h * r
    gd = dn * g
    dh = r * (gd - nh * jnp.mean(gd * nh, axis=-1, keepdims=True))
    return dh, jnp.sum(dn * nh, axis=0, keepdims=True)


def _accumulate(ref, val):
    @pl.when(pl.program_id(0) == 0)
    def _():
        ref[...] = jnp.zeros_like(ref)
    ref[...] += val


def _place():
    x, y, c = lax.axis_index("x"), lax.axis_index("y"), lax.axis_index("c")
    return x, y, c


def _slot(px, py, pc):
    return 4 * px + 2 * py + pc


def _gather_phases(ins, outs, send_sems, recv_sems, local_sems):
    n = len(ins)

    def parties():
        x, y, c = _place()
        return (x, y, c), (x, y, 1 - c), [(1 - x, y), (x, 1 - y), (1 - x, 1 - y)], c

    def copy(a, k, block, to, src=None):
        dst = outs[a].at[_slot(*block)]
        return pltpu.make_async_remote_copy(
            src_ref=dst if src is None else src, dst_ref=dst,
            send_sem=send_sems.at[a, k], recv_sem=recv_sems.at[a, k],
            device_id=to, device_id_type=MESH)

    def own(a, me):
        return pltpu.make_async_copy(ins[a], outs[a].at[_slot(*me)], local_sems.at[a])

    def first(a, me, sibling, chips, c):
        return [copy(a, 0, me, sibling, src=ins[a])] + [copy(a, 1 + j, me, (*chip, c), src=ins[a]) for j, chip in enumerate(chips)]

    def start():
        me, sibling, chips, c = parties()
        for a in range(n):
            own(a, me).start()
        for a in range(n):
            for cp in first(a, me, sibling, chips, c):
                cp.start()

    def forward():
        me, sibling, chips, c = parties()
        for j, chip in enumerate(chips):
            for a in range(n):
                copy(a, 1 + j, (*chip, c), me).wait_recv()
                copy(a, 4 + j, (*chip, c), sibling).start()

    def finish():
        me, sibling, chips, c = parties()
        for a in range(n):
            copy(a, 0, sibling, me).wait_recv()
            for j, chip in enumerate(chips):
                copy(a, 4 + j, (*chip, 1 - c), me).wait_recv()
        for a in range(n):
            for cp in first(a, me, sibling, chips, c) + [copy(a, 4 + j, (*chip, c), sibling) for j, chip in enumerate(chips)]:
                cp.wait_send()
            own(a, me).wait()

    return [start, forward, finish]


def _scatter_phases(ins, outs, send_sems, recv_sems, local_sems):
    n = len(ins)

    def copies():
        x, y, c = _place()
        me = _slot(x, y, c)
        out = [pltpu.make_async_copy(ins[a].at[me], outs[a].at[me], local_sems.at[a]) for a in range(n)]
        for k in range(1, N_SHARDS):
            px = 1 - x if k & 4 else x
            py = 1 - y if k & 2 else y
            pc = 1 - c if k & 1 else c
            for a in range(n):
                out.append(pltpu.make_async_remote_copy(
                    src_ref=ins[a].at[_slot(px, py, pc)], dst_ref=outs[a].at[me],
                    send_sem=send_sems.at[a, k - 1], recv_sem=recv_sems.at[a, k - 1],
                    device_id=(px, py, pc), device_id_type=MESH))
        return out

    def start():
        for cp in copies():
            cp.start()

    def finish():
        for cp in copies():
            cp.wait()

    return [start, finish]


def _pallas(body, *, name, grid, in_specs, out_specs, out_shape, args, scratch_shapes=(), ride=None):
    if ride is None:
        outs = pl.pallas_call(body, name=name, grid=grid, in_specs=in_specs, out_specs=out_specs, out_shape=out_shape,
                              scratch_shapes=list(scratch_shapes))(*args)
        return list(outs), []
    kind, arrays = ride
    n, n_in, n_out, n_scr = len(arrays), len(in_specs), len(out_specs), len(scratch_shapes)
    total = math.prod(grid)
    middle = (9 * total) // 10
    landed_shape = [S((N_SHARDS,) + a.shape if kind == "gather" else a.shape, a.dtype) for a in arrays]

    def with_exchange(*refs):
        ins, riders_in = refs[:n_in], refs[n_in:n_in + n]
        outs, riders_out = refs[n_in + n:n_in + n + n_out], refs[n_in + n + n_out:n_in + 2 * n + n_out]
        scratch, sems = refs[n_in + 2 * n + n_out:n_in + 2 * n + n_out + n_scr], refs[n_in + 2 * n + n_out + n_scr:]
        step = 0
        for axis, size in enumerate(grid):
            step = step * size + pl.program_id(axis)
        phases = (_gather_phases if kind == "gather" else _scatter_phases)(riders_in, riders_out, *sems)
        pl.when(step == 0)(phases[0])
        body(*ins, *outs, *scratch)
        for phase in phases[1:-1]:
            pl.when(step == middle)(phase)
        pl.when(step == total - 1)(phases[-1])

    outs = pl.pallas_call(
        with_exchange, name=name, grid=grid,
        in_specs=list(in_specs) + [ANY] * n, out_specs=list(out_specs) + [ANY] * n,
        out_shape=list(out_shape) + landed_shape,
        scratch_shapes=list(scratch_shapes) + [pltpu.SemaphoreType.DMA((n, 7)), pltpu.SemaphoreType.DMA((n, 7)),
                                               pltpu.SemaphoreType.DMA((n,))],
    )(*args, *arrays)
    return list(outs[:n_out]), list(outs[n_out:])


def _exchange_alone(kind, arrays, name):
    return _pallas(lambda: None, name=name, grid=(1,), in_specs=[], out_specs=[], out_shape=[], args=[], ride=(kind, arrays))[1]


def _prenorm(x, g, ride=None):
    t = x.shape[0]
    tm = min(TOKEN_TILE, t)

    def body(x_ref, g_ref, n_ref):
        n_ref[...] = _rms(x_ref[...], g_ref[...]).astype(BF)

    return _pallas(
        body, name="prenorm", grid=(t // tm,), ride=ride,
        in_specs=[_rows(tm, D_MODEL), _const_spec((1, D_MODEL))], out_specs=[_rows(tm, D_MODEL)],
        out_shape=[S((t, D_MODEL), BF)], args=[x, g])


def _ffn_up(n, w_in, name, ride=None):
    t = n.shape[0]
    tm = min(TOKEN_TILE, t)

    def body(n_ref, win_ref, act_ref, to_gate_ref, to_up_ref):
        nb = n_ref[...]
        for c in range(N_FF_CHUNKS):
            gate = _dot(nb, win_ref[c])
            up = _dot(nb, win_ref[N_FF_CHUNKS + c])
            sg = jax.nn.sigmoid(gate)
            silu = gate * sg
            act_ref[c] = (silu * up).astype(BF)
            to_gate_ref[c] = (up * (sg * (1.0 + gate * (1.0 - sg)))).astype(BF)
            to_up_ref[c] = silu.astype(BF)

    return _pallas(
        body, name=name, grid=(t // tm,), ride=ride,
        in_specs=[_rows(tm, D_MODEL), _const_spec(w_in.shape)],
        out_specs=[_chunks(tm)] * 3, out_shape=[S((N_FF_CHUNKS, t, FF_CHUNK), BF)] * 3,
        args=[n, w_in])


def _ffn_down(h, act, w_out, g_next, name, ride=None):
    t = h.shape[0]
    tm = min(TOKEN_TILE, t)

    def body(h_ref, act_ref, wout_ref, g_ref, ho_ref, no_ref):
        acc = jnp.zeros((tm, D_MODEL), F32)
        for c in range(N_FF_CHUNKS):
            acc = acc + _dot(act_ref[c], wout_ref[c])
        ho = h_ref[...] + 0.5 * acc
        ho_ref[...] = ho
        no_ref[...] = _rms(ho, g_ref[...]).astype(BF)

    return _pallas(
        body, name=name, grid=(t // tm,), ride=ride,
        in_specs=[_rows(tm, D_MODEL), _chunks(tm), _const_spec(w_out.shape), _const_spec((1, D_MODEL))],
        out_specs=[_rows(tm, D_MODEL)] * 2, out_shape=[S((t, D_MODEL), F32), S((t, D_MODEL), BF)],
        args=[h, act, w_out, g_next])


def _mix_proj(u, w_mix, ride=None):
    t = u.shape[0]
    tm = min(PROJ_TILE, t)

    def body(u_ref, w_ref, o_ref):
        o_ref[0] = _dot(u_ref[...], w_ref[0]).astype(BF)

    return _pallas(
        body, name="mix_proj", grid=(N_MIX, t // tm), ride=ride,
        in_specs=[pl.BlockSpec((tm, D_MODEL), lambda d, i: (i, 0)), pl.BlockSpec((1, D_MODEL, D_MODEL), lambda d, i: (d, 0, 0))],
        out_specs=[pl.BlockSpec((1, tm, D_MODEL), lambda d, i: (d, i, 0))],
        out_shape=[S((N_MIX, t, D_MODEL), BF)], args=[u, w_mix])


HALO = 16


def _piece(d, tm):
    return pl.BlockSpec((1, tm, D_MODEL), lambda i: (d, i, 0))


def _prev_halo(d, tm):
    return pl.BlockSpec((1, HALO, D_MODEL), lambda i: (d, jnp.maximum(i * (tm // HALO) - 1, 0), 0))


def _shift_down(m, prev_tail, k):
    tm = m.shape[0]
    out = pltpu.roll(m, k, 0)
    row = lax.broadcasted_iota(jnp.int32, (tm, 1), 0)
    for j in range(k):
        out = jnp.where(row == j, prev_tail[HALO - k + j:HALO - k + j + 1, :], out)
    return out


def _conv_inputs(cc_ref, cx_ref, cch_ref, cxh_ref):
    m = cc_ref[0].astype(F32) * cx_ref[0].astype(F32)
    mh = cch_ref[0].astype(F32) * cxh_ref[0].astype(F32)
    mh = jnp.where(pl.program_id(0) == 0, 0.0, mh)
    return m, _shift_down(m, mh, 1), _shift_down(m, mh, 2)


def _mixer_out(proj, o, h1, conv_w, w_co, w_ao, w_mo, g_next):
    t = h1.shape[0]
    tm = min(TOKEN_TILE, t)

    def body(cb_ref, cc_ref, cx_ref, gc_ref, ga_ref, cch_ref, cxh_ref, o_ref, h_ref, cw_ref, wco_ref, wao_ref, wmo_ref,
             g_ref, ho_ref, no_ref, ycin_ref, yc_ref, ya_ref, mg_ref):
        m, m1, m2 = _conv_inputs(cc_ref, cx_ref, cch_ref, cxh_ref)
        cw = cw_ref[...]
        cv = cw[0:1, :] * m2 + cw[1:2, :] * m1 + cw[2:3, :] * m
        ycin = (cb_ref[0].astype(F32) * cv).astype(BF)
        ycin_ref[...] = ycin
        yc = _dot(ycin, wco_ref[...])
        ya = _dot(o_ref[...], wao_ref[...])
        yc_ref[...] = yc.astype(BF)
        ya_ref[...] = ya.astype(BF)
        merged = (jax.nn.sigmoid(gc_ref[0].astype(F32)) * yc + jax.nn.sigmoid(ga_ref[0].astype(F32)) * ya).astype(BF)
        mg_ref[...] = merged
        ho = h_ref[...] + _dot(merged, wmo_ref[...])
        ho_ref[...] = ho
        no_ref[...] = _rms(ho, g_ref[...]).astype(BF)

    sq = (D_MODEL, D_MODEL)
    return pl.pallas_call(
        body, name="mixer_out", grid=(t // tm,),
        in_specs=[_piece(0, tm), _piece(1, tm), _piece(2, tm), _piece(6, tm), _piece(7, tm), _prev_halo(1, tm), _prev_halo(2, tm),
                  _rows(tm, D_MODEL), _rows(tm, D_MODEL), _const_spec((3, D_MODEL)), _const_spec(sq), _const_spec(sq),
                  _const_spec(sq), _const_spec((1, D_MODEL))],
        out_specs=[_rows(tm, D_MODEL)] * 6,
        out_shape=[S((t, D_MODEL), F32)] + [S((t, D_MODEL), BF)] * 5,
    )(proj, proj, proj, proj, proj, proj, proj, o, h1, conv_w, w_co, w_ao, w_mo, g_next)


def _suffix_sums(vals, tri, before):
    out, right = [], before
    for b in reversed(range(ATTN_K // ATTN_SUB)):
        v = vals[:, b * ATTN_SUB:(b + 1) * ATTN_SUB]
        hi = v.astype(BF)
        lo = (v - hi.astype(F32)).astype(BF)
        out.append(_dot(hi, tri) + _dot(lo, tri) + right)
        right = right + jnp.sum(v, axis=1, keepdims=True)
    return jnp.concatenate(out[::-1], axis=1), right


ATTN_UNITS = ATTN_ROWS // ATTN_Q


def _unit_rows(x, u):
    return x[u * ATTN_Q:(u + 1) * ATTN_Q]


def _per_unit(fn):
    return jnp.concatenate([fn(u) for u in range(ATTN_UNITS)], axis=0)


def _per_row(vals):
    local = lax.broadcasted_iota(jnp.int32, (ATTN_ROWS, 1), 0)
    out = jnp.full((ATTN_ROWS, 1), vals[0], jnp.int32)
    for u in range(1, ATTN_UNITS):
        out = jnp.where(local >= u * ATTN_Q, vals[u], out)
    return out


def _attn_step(q, k_ref, starts, bounds, row):
    z = _per_unit(lambda u: _dot_nt(_unit_rows(q, u), k_ref[0, pl.ds(starts[u], ATTN_K), :])) * (1.0 / math.sqrt(HEAD_DIM))
    mask = lax.broadcasted_iota(jnp.int32, (1, ATTN_K), 1) < jnp.minimum(row, _per_row(bounds)) - _per_row(starts)
    log_beta = jnp.minimum(z, 0.0) - jnp.log(1.0 + jnp.exp(jnp.minimum(z, -z)))
    log_rest = jnp.where(mask, log_beta - z, 0.0)
    return z, mask, log_beta, log_rest


def _attn_sweep_start(i, t):
    blks = tuple(jnp.maximum(i * ATTN_UNITS + u + 1 - ATTN_K // ATTN_SUB, 0) for u in range(ATTN_UNITS))
    return blks, tuple(jnp.int32(t) for _ in range(ATTN_UNITS))


def _attn_keys(blks):
    return [pl.multiple_of(b * ATTN_SUB, ATTN_SUB) for b in blks]


def _attn_next(blks):
    return tuple(jnp.maximum(b - ATTN_K // ATTN_SUB, 0) for b in blks), tuple(b * ATTN_SUB for b in blks)


def _attn_more(carry):
    return jnp.logical_and(carry[1][ATTN_UNITS - 1] > 0, carry[-1] > ATTN_SKIP_BELOW)


def _tri(strict):
    r = lax.broadcasted_iota(jnp.int32, (ATTN_SUB, ATTN_SUB), 0)
    c = lax.broadcasted_iota(jnp.int32, (ATTN_SUB, ATTN_SUB), 1)
    return (r > c if strict else r >= c).astype(BF)


REACH_TILE = (8, 128)


def _first_step_spec():
    return pl.BlockSpec((1, ATTN_ROWS, ATTN_K), lambda h, i: (h, i, 0))


def _reach_spec():
    return pl.BlockSpec((1, 1) + REACH_TILE, lambda h, i: (h, i, 0, 0))


def _head_cols(piece):
    return lambda t: pl.BlockSpec((1, t, HEAD_DIM), lambda h, i: (piece, 0, h))


def _attn_fwd(proj):
    t = proj.shape[1]
    nq = t // ATTN_ROWS
    tri = _tri(strict=True)

    def body(q_ref, k_ref, v_ref, tri_ref, o_ref, ob_ref, a_ref, beta_ref, reach_ref):
        i = pl.program_id(1)
        q = q_ref[0]
        row = i * ATTN_ROWS + lax.broadcasted_iota(jnp.int32, (ATTN_ROWS, 1), 0)

        def step(carry, keep=False):
            blks, bounds, acc, run, _ = carry
            starts = _attn_keys(blks)
            _, mask, log_beta, log_rest = _attn_step(q, k_ref, starts, bounds, row)
            tail, run = _suffix_sums(log_rest, tri_ref[...], run)
            a = jnp.where(mask, jnp.exp(log_beta + tail), 0.0).astype(BF)
            if keep:
                a_ref[0] = a
                beta_ref[0] = jnp.where(mask, jnp.exp(log_beta), 0.0).astype(BF)
            acc = acc + _per_unit(lambda u: _dot(_unit_rows(a, u), v_ref[0, pl.ds(starts[u], ATTN_K), :]))
            return (*_attn_next(blks), acc, run, jnp.max(run))

        first = (*_attn_sweep_start(i, t), jnp.zeros((ATTN_ROWS, HEAD_DIM), F32), jnp.zeros((ATTN_ROWS, 1), F32), jnp.float32(0.0))
        after_first = step(first, keep=True)
        reach_ref[...] = jnp.full(reach_ref.shape, after_first[-1], F32)
        o = lax.while_loop(_attn_more, step, after_first)[2]
        o_ref[...] = o
        ob_ref[...] = o.astype(BF)

    qspec = pl.BlockSpec((1, ATTN_ROWS, HEAD_DIM), lambda h, i: (3, i, h))
    rowblk = pl.BlockSpec((ATTN_ROWS, HEAD_DIM), lambda h, i: (i, h))
    return pl.pallas_call(
        body, name="attn_fwd", grid=(N_HEADS, nq),
        in_specs=[qspec, _head_cols(4)(t), _head_cols(5)(t), pl.BlockSpec((ATTN_SUB, ATTN_SUB), lambda h, i: (0, 0))],
        out_specs=[rowblk, rowblk, _first_step_spec(), _first_step_spec(), _reach_spec()],
        out_shape=[S((t, D_MODEL), F32), S((t, D_MODEL), BF), S((N_HEADS, t, ATTN_K), BF), S((N_HEADS, t, ATTN_K), BF),
                   S((N_HEADS, nq) + REACH_TILE, F32)],
    )(proj, proj, proj, tri)


def _attn_bwd(proj, o, d_o, a_first, beta_first, reach, ride=None):
    t = proj.shape[1]
    nq = t // ATTN_ROWS
    tri_strict, tri_incl = _tri(strict=True), _tri(strict=False)
    scale = 1.0 / math.sqrt(HEAD_DIM)

    def body(q_ref, k_ref, v_ref, o_ref, do_ref, a_ref, beta_ref, reach_ref, tris_ref, trii_ref, dq_ref, dk_ref, dv_ref, dk_acc, dv_acc):
        i = pl.program_id(1)

        @pl.when(i == 0)
        def _():
            dk_acc[...] = jnp.zeros_like(dk_acc)
            dv_acc[...] = jnp.zeros_like(dv_acc)

        q = q_ref[0]
        do = do_ref[...]
        total = jnp.sum(do.astype(F32) * o_ref[...], axis=1, keepdims=True)
        zero = jnp.zeros((ATTN_ROWS, 1), F32)
        blks0, bounds0 = _attn_sweep_start(i, t)

        def finish(starts, a, dz, dq):
            dzb = (dz * scale).astype(BF)
            for u in range(ATTN_UNITS):
                dv_acc[pl.ds(starts[u], ATTN_K), :] += _dot_tn(_unit_rows(a, u), _unit_rows(do, u))
                dk_acc[pl.ds(starts[u], ATTN_K), :] += _dot_tn(_unit_rows(dzb, u), _unit_rows(q, u))
            return dq + _per_unit(lambda u: _dot(_unit_rows(dzb, u), k_ref[0, pl.ds(starts[u], ATTN_K), :]))

        def grad_a(starts, a):
            return _per_unit(lambda u: _dot_nt(_unit_rows(do, u), v_ref[0, pl.ds(starts[u], ATTN_K), :])) * a.astype(F32)

        one_step = jnp.max(reach_ref[...]) <= ATTN_SKIP_BELOW

        @pl.when(one_step)
        def _():
            starts = _attn_keys(blks0)
            a = a_ref[0]
            beta = beta_ref[0].astype(F32)
            de = grad_a(starts, a)
            right, _ = _suffix_sums(de, trii_ref[...], zero)
            dz = de * (1.0 - beta) - (total - right) * beta
            dq_ref[...] = finish(starts, a, dz, jnp.zeros((ATTN_ROWS, HEAD_DIM), F32)).astype(BF)

        @pl.when(jnp.logical_not(one_step))
        def _():
            row = i * ATTN_ROWS + lax.broadcasted_iota(jnp.int32, (ATTN_ROWS, 1), 0)

            def step(carry):
                blks, bounds, dq, seen, run, _ = carry
                starts = _attn_keys(blks)
                z, mask, log_beta, log_rest = _attn_step(q, k_ref, starts, bounds, row)
                tail, run = _suffix_sums(log_rest, tris_ref[...], run)
                a = jnp.where(mask, jnp.exp(log_beta + tail), 0.0).astype(BF)
                de = grad_a(starts, a)
                right, seen = _suffix_sums(de, trii_ref[...], seen)
                beta = jax.nn.sigmoid(z)
                dz = jnp.where(mask, de * (1.0 - beta) - (total - right) * beta, 0.0)
                return (*_attn_next(blks), finish(starts, a, dz, dq), seen, run, jnp.max(run))

            first = (blks0, bounds0, jnp.zeros((ATTN_ROWS, HEAD_DIM), F32), zero, zero, jnp.float32(0.0))
            dq_ref[...] = lax.while_loop(_attn_more, step, step(first))[2].astype(BF)

        @pl.when(i == nq - 1)
        def _():
            dk_ref[...] = dk_acc[...].astype(BF)
            dv_ref[...] = dv_acc[...].astype(BF)

    qspec = pl.BlockSpec((1, ATTN_ROWS, HEAD_DIM), lambda h, i: (3, i, h))
    rowblk = pl.BlockSpec((ATTN_ROWS, HEAD_DIM), lambda h, i: (i, h))
    head = pl.BlockSpec((t, HEAD_DIM), lambda h, i: (0, h))
    trispec = pl.BlockSpec((ATTN_SUB, ATTN_SUB), lambda h, i: (0, 0))
    return _pallas(
        body, name="attn_bwd", grid=(N_HEADS, nq), ride=ride,
        in_specs=[qspec, _head_cols(4)(t), _head_cols(5)(t), rowblk, rowblk, _first_step_spec(), _first_step_spec(), _reach_spec(),
                  trispec, trispec],
        out_specs=[rowblk, head, head],
        out_shape=[S((t, D_MODEL), BF)] * 3,
        scratch_shapes=[pltpu.VMEM((t, HEAD_DIM), F32), pltpu.VMEM((t, HEAD_DIM), F32)],
        args=[proj, proj, proj, o, d_o, a_first, beta_first, reach, tri_strict, tri_incl])


def _tail(h3, n4, p, w_pg, w_pp, g_ple, g_final, target):
    t = h3.shape[0]
    tm = min(TOKEN_TILE, t)
    steps = t // tm

    def body(h_ref, n_ref, p_ref, wpg_ref, wpp_ref, gp_ref, gf_ref, tgt_ref,
             dh_ref, ds_ref, dpp_ref, loss_ref, dgf_ref, dgp_ref):
        pg = jax.nn.sigmoid(_dot(n_ref[...], wpg_ref[...]))
        pp = _dot(p_ref[...].astype(BF), wpp_ref[...])
        h3v = h_ref[...]
        h4 = h3v + pg * pp
        gf = gf_ref[...]
        diff = _rms(h4, gf) - tgt_ref[...]
        _accumulate(loss_ref, jnp.sum(diff * diff, axis=0, keepdims=True))
        dh4, dgf = _rms_bwd(diff * (1.0 / D_MODEL), h4, gf)
        _accumulate(dgf_ref, dgf)
        dpp_ref[...] = (dh4 * pg).astype(BF)
        ds = (dh4 * pp * pg * (1.0 - pg)).astype(BF)
        ds_ref[...] = ds
        dh3, dgp = _rms_bwd(_dot_nt(ds, wpg_ref[...]), h3v, gp_ref[...])
        _accumulate(dgp_ref, dgp)
        dh_ref[...] = dh4 + dh3

        @pl.when(pl.program_id(0) == steps - 1)
        def _():
            loss_ref[...] = jnp.full(loss_ref.shape, 0.5 / D_MODEL * jnp.sum(loss_ref[...]), F32)

    vec = (1, D_MODEL)
    return pl.pallas_call(
        body, name="tail", grid=(steps,),
        in_specs=[_rows(tm, D_MODEL), _rows(tm, D_MODEL), _rows(tm, PLE_DIM), _const_spec((D_MODEL, D_MODEL)),
                  _const_spec((PLE_DIM, D_MODEL)), _const_spec(vec), _const_spec(vec), _rows(tm, D_MODEL)],
        out_specs=[_rows(tm, D_MODEL)] * 3 + [_acc_spec(vec)] * 3,
        out_shape=[S((t, D_MODEL), F32), S((t, D_MODEL), BF), S((t, D_MODEL), BF)] + [S(vec, F32)] * 3,
    )(h3, n4, p, w_pg, w_pp, g_ple, g_final, target)


def _wgrad(xs, ys, name, ride=None, tile=None):
    bx, t, k = xs.shape
    by, _, n = ys.shape
    b = max(bx, by)
    tt = min(tile or WGRAD_TILE * 2 // xs.dtype.itemsize, t)
    steps = t // tt

    def body(x_ref, y_ref, o_ref, acc_ref):
        s = pl.program_id(1)

        @pl.when(s == 0)
        def _():
            acc_ref[...] = jnp.zeros_like(acc_ref)
        acc_ref[...] += _dot_tn(x_ref[0].astype(BF), y_ref[0].astype(BF))

        @pl.when(s == steps - 1)
        def _():
            o_ref[0] = acc_ref[...].astype(BF)

    (out,), landed = _pallas(
        body, name=name, grid=(b, steps), ride=ride,
        in_specs=[pl.BlockSpec((1, tt, k), (lambda j, s: (j, s, 0)) if bx > 1 else (lambda j, s: (0, s, 0))),
                  pl.BlockSpec((1, tt, n), (lambda j, s: (j, s, 0)) if by > 1 else (lambda j, s: (0, s, 0)))],
        out_specs=[pl.BlockSpec((1, k, n), lambda j, s: (j, 0, 0))],
        out_shape=[S((b, k, n), BF)],
        scratch_shapes=[pltpu.VMEM((k, n), F32)],
        args=[xs, ys])
    return (out, landed) if ride is not None else out


def _wgrad_pieces(x, ys, name, ride=None, tile=None, row_parts=1, transposed=False):
    t, k = x.shape
    n = ys[0].shape[2]
    counts = [y.shape[0] for y in ys]
    offsets = [sum(counts[:j]) for j in range(len(ys))]
    total = sum(counts)
    tt = min(tile or WGRAD_TILE, t)
    steps = t // tt
    rows, cols = (n, k) if transposed else (k, n)
    kp = rows // row_parts

    def body(x_ref, *refs):
        y_refs, o_refs, acc_ref = refs[:len(ys)], refs[len(ys):len(ys) + row_parts], refs[len(ys) + row_parts]
        p, s = pl.program_id(0), pl.program_id(1)

        @pl.when(s == 0)
        def _():
            acc_ref[...] = jnp.zeros_like(acc_ref)
        for j, y_ref in enumerate(y_refs):
            @pl.when(jnp.logical_and(p >= offsets[j], p < offsets[j] + counts[j]))
            def _(y_ref=y_ref):
                acc_ref[...] += _dot_tn(y_ref[0], x_ref[...]) if transposed else _dot_tn(x_ref[...], y_ref[0])

        @pl.when(s == steps - 1)
        def _():
            for part, o_ref in enumerate(o_refs):
                o_ref[0] = acc_ref[part * kp:(part + 1) * kp, :].astype(BF)

    def turn(j):
        lo, hi = offsets[j], offsets[j] + counts[j]
        return lambda p, s: (jnp.clip(p - lo, 0, counts[j] - 1), jnp.where(p < lo, 0, jnp.where(p >= hi, steps - 1, s)), 0)

    outs, landed = _pallas(
        body, name=name, grid=(total, steps), ride=ride,
        in_specs=[pl.BlockSpec((tt, k), lambda p, s: (s, 0))] + [pl.BlockSpec((1, tt, n), turn(j)) for j in range(len(ys))],
        out_specs=[pl.BlockSpec((1, kp, cols), lambda p, s: (p, 0, 0))] * row_parts,
        out_shape=[S((total, kp, cols), BF)] * row_parts,
        scratch_shapes=[pltpu.VMEM((rows, cols), F32)],
        args=[x, *ys])
    out = outs[0] if row_parts == 1 else outs
    return (out, landed) if ride is not None else out


def _ffn_bwd_hidden(dh, to_gate, to_up, w_out, name, ride=None):
    t = dh.shape[0]
    tm = min(TOKEN_TILE, t)

    def body(dh_ref, to_gate_ref, to_up_ref, wout_ref, df_ref, dgate_ref, dup_ref):
        df = (0.5 * dh_ref[...]).astype(BF)
        df_ref[...] = df
        for c in range(N_FF_CHUNKS):
            dact = _dot_nt(df, wout_ref[c])
            dgate_ref[c] = (dact * to_gate_ref[c].astype(F32)).astype(BF)
            dup_ref[c] = (dact * to_up_ref[c].astype(F32)).astype(BF)

    return _pallas(
        body, name=name, grid=(t // tm,), ride=ride,
        in_specs=[_rows(tm, D_MODEL), _chunks(tm), _chunks(tm), _const_spec(w_out.shape)],
        out_specs=[_rows(tm, D_MODEL), _chunks(tm), _chunks(tm)],
        out_shape=[S((t, D_MODEL), BF)] + [S((N_FF_CHUNKS, t, FF_CHUNK), BF)] * 2,
        args=[dh, to_gate, to_up, w_out])


def _ffn_bwd_input(dh, h_in, g, dgate, dup, w_in, name, ride=None):
    t = dh.shape[0]
    tm = min(TOKEN_TILE, t)

    def body(dh_ref, h_ref, g_ref, dgate_ref, dup_ref, win_ref, dhi_ref, dg_ref):
        dn = jnp.zeros((tm, D_MODEL), F32)
        for c in range(N_FF_CHUNKS):
            dn = dn + _dot_nt(dgate_ref[c], win_ref[c]) + _dot_nt(dup_ref[c], win_ref[N_FF_CHUNKS + c])
        dhi, dg = _rms_bwd(dn, h_ref[...], g_ref[...])
        _accumulate(dg_ref, dg)
        dhi_ref[...] = dh_ref[...] + dhi

    vec = (1, D_MODEL)
    return _pallas(
        body, name=name, grid=(t // tm,), ride=ride,
        in_specs=[_rows(tm, D_MODEL), _rows(tm, D_MODEL), _const_spec(vec), _chunks(tm), _chunks(tm), _const_spec(w_in.shape)],
        out_specs=[_rows(tm, D_MODEL), _acc_spec(vec)],
        out_shape=[S((t, D_MODEL), F32), S(vec, F32)],
        args=[dh, h_in, g, dgate, dup, w_in])


def _mixer_bwd(dh2, proj, yc, ya, conv_w, w_co, w_ao, w_mo, ride=None):
    t = dh2.shape[0]
    tm = min(TOKEN_TILE, t)

    def body(dh_ref, cb_ref, cc_ref, cx_ref, gc_ref, ga_ref, cch_ref, cxh_ref, yc_ref, ya_ref, cw_ref, wco_ref, wao_ref, wmo_ref,
             dhb_ref, dyc_ref, dya_ref, dgc_ref, dga_ref, dcb_ref, dcv_ref, do_ref):
        dhb = dh_ref[...].astype(BF)
        dhb_ref[...] = dhb
        dmerged = _dot_nt(dhb, wmo_ref[...])
        sc = jax.nn.sigmoid(gc_ref[0].astype(F32))
        sa = jax.nn.sigmoid(ga_ref[0].astype(F32))
        dyc = (dmerged * sc).astype(BF)
        dya = (dmerged * sa).astype(BF)
        dyc_ref[...] = dyc
        dya_ref[...] = dya
        dgc_ref[...] = (dmerged * yc_ref[...].astype(F32) * sc * (1.0 - sc)).astype(BF)
        dga_ref[...] = (dmerged * ya_ref[...].astype(F32) * sa * (1.0 - sa)).astype(BF)
        m, m1, m2 = _conv_inputs(cc_ref, cx_ref, cch_ref, cxh_ref)
        cw = cw_ref[...]
        cv = cw[0:1, :] * m2 + cw[1:2, :] * m1 + cw[2:3, :] * m
        dycin = _dot_nt(dyc, wco_ref[...])
        dcb_ref[...] = (dycin * cv).astype(BF)
        dcv_ref[...] = (dycin * cb_ref[0].astype(F32)).astype(BF)
        do_ref[...] = _dot_nt(dya, wao_ref[...]).astype(BF)

    sq = (D_MODEL, D_MODEL)
    return _pallas(
        body, name="mixer_bwd", grid=(t // tm,), ride=ride,
        in_specs=[_rows(tm, D_MODEL), _piece(0, tm), _piece(1, tm), _piece(2, tm), _piece(6, tm), _piece(7, tm),
                  _prev_halo(1, tm), _prev_halo(2, tm), _rows(tm, D_MODEL), _rows(tm, D_MODEL),
                  _const_spec((3, D_MODEL)), _const_spec(sq), _const_spec(sq), _const_spec(sq)],
        out_specs=[_rows(tm, D_MODEL)] * 8,
        out_shape=[S((t, D_MODEL), BF)] * 8,
        args=[dh2, proj, proj, proj, proj, proj, proj, proj, yc, ya, conv_w, w_co, w_ao, w_mo])


TAP_ROWS = 8


def _conv_bwd(dcv, proj, conv_w):
    t = dcv.shape[0]
    tm = min(TOKEN_TILE, t)
    steps = t // tm

    def body(dcv_ref, nxt_ref, cc_ref, cx_ref, cch_ref, cxh_ref, cw_ref, dcc_ref, dcx_ref, dw_ref):
        i = pl.program_id(0)
        m, m1, m2 = _conv_inputs(cc_ref, cx_ref, cch_ref, cxh_ref)
        d0 = dcv_ref[...].astype(F32)
        nxt = jnp.where(i == steps - 1, 0.0, nxt_ref[...].astype(F32))
        row = lax.broadcasted_iota(jnp.int32, (tm, 1), 0)
        d1 = jnp.where(row == tm - 1, nxt[0:1, :], pltpu.roll(d0, tm - 1, 0))
        d2 = pltpu.roll(d0, tm - 2, 0)
        d2 = jnp.where(row == tm - 2, nxt[0:1, :], jnp.where(row == tm - 1, nxt[1:2, :], d2))
        cw = cw_ref[...]
        dm = cw[2:3, :] * d0 + cw[1:2, :] * d1 + cw[0:1, :] * d2
        dcc_ref[...] = (dm * cx_ref[0].astype(F32)).astype(BF)
        dcx_ref[...] = (dm * cc_ref[0].astype(F32)).astype(BF)
        tap_row = lax.broadcasted_iota(jnp.int32, (TAP_ROWS, 1), 0)
        dw = jnp.zeros((TAP_ROWS, D_MODEL), F32)
        for j, mk in enumerate((m2, m1, m)):
            dw = jnp.where(tap_row == j, jnp.sum(d0 * mk, axis=0, keepdims=True), dw)
        _accumulate(dw_ref, dw)

    nxt_spec = pl.BlockSpec((HALO, D_MODEL), lambda i: (jnp.minimum((i + 1) * (tm // HALO), t // HALO - 1), 0))
    return pl.pallas_call(
        body, name="conv_bwd", grid=(steps,),
        in_specs=[_rows(tm, D_MODEL), nxt_spec, _piece(1, tm), _piece(2, tm), _prev_halo(1, tm), _prev_halo(2, tm),
                  _const_spec((3, D_MODEL))],
        out_specs=[_rows(tm, D_MODEL), _rows(tm, D_MODEL), _acc_spec((TAP_ROWS, D_MODEL))],
        out_shape=[S((t, D_MODEL), BF), S((t, D_MODEL), BF), S((TAP_ROWS, D_MODEL), F32)],
    )(dcv, dcv, proj, proj, proj, proj, conv_w)


def _mix_bwd(dpieces, w_mix, h1, dh2, g, ride=None):
    t = h1.shape[0]
    tm = min(TOKEN_TILE, t)

    def body(*refs):
        pieces, (w_ref, h_ref, dh_ref, g_ref, dhi_ref, dg_ref) = refs[:N_MIX], refs[N_MIX:]
        du = jnp.zeros((tm, D_MODEL), F32)
        for d in range(N_MIX):
            du = du + _dot_nt(pieces[d][...], w_ref[d])
        dhi, dg = _rms_bwd(du, h_ref[...], g_ref[...])
        _accumulate(dg_ref, dg)
        dhi_ref[...] = dh_ref[...] + dhi

    vec = (1, D_MODEL)
    return _pallas(
        body, name="mix_bwd", grid=(t // tm,), ride=ride,
        in_specs=[_rows(tm, D_MODEL)] * N_MIX + [_const_spec(w_mix.shape), _rows(tm, D_MODEL), _rows(tm, D_MODEL), _const_spec(vec)],
        out_specs=[_rows(tm, D_MODEL), _acc_spec(vec)],
        out_shape=[S((t, D_MODEL), F32), S(vec, F32)],
        args=[*dpieces, w_mix, h1, dh2, g])


def _adamw(partials, w, m, v, name):
    parts = list(partials) if isinstance(partials, (list, tuple)) else [partials]
    r, c = w.shape
    tr = next(d for d in (r, 512, 352, 256) if d <= 512 // len(parts) and r % d == 0)
    first_tile = [sum(p.shape[1] for p in parts[:j]) // tr for j in range(len(parts))]
    c1 = 1.0 - ADAM_B1 ** ADAM_STEP
    c2 = 1.0 - ADAM_B2 ** ADAM_STEP

    def body(*refs):
        p_refs, (w_ref, m_ref, v_ref, g_ref, d_ref, mo_ref, vo_ref) = refs[:len(parts)], refs[len(parts):]
        g = None
        for j, p_ref in enumerate(p_refs):
            gj = p_ref[0].astype(F32)
            for s in range(1, N_SHARDS):
                gj = gj + p_ref[s].astype(F32)
            g = gj if g is None else jnp.where(pl.program_id(0) >= first_tile[j], gj, g)
        mn = ADAM_B1 * m_ref[...] + (1.0 - ADAM_B1) * g
        vn = ADAM_B2 * v_ref[...] + (1.0 - ADAM_B2) * (g * g)
        g_ref[...] = g
        mo_ref[...] = mn
        vo_ref[...] = vn
        d_ref[...] = -ADAM_LR * ((mn / c1) / (jnp.sqrt(vn / c2) + ADAM_EPS) + ADAM_WD * w_ref[...])

    def rows_of(j):
        last = parts[j].shape[1] // tr - 1
        return lambda i: (0, jnp.clip(i - first_tile[j], 0, last), 0)

    blk = pl.BlockSpec((tr, c), lambda i: (i, 0))
    return pl.pallas_call(
        body, name=name, grid=(r // tr,),
        in_specs=[pl.BlockSpec((N_SHARDS, tr, c), rows_of(j)) for j in range(len(parts))] + [blk, blk, blk],
        out_specs=[blk] * 4, out_shape=[S((r, c), F32)] * 4,
    )(*parts, w, m, v)


_MATRICES = ("ffn1_w_in", "ffn1_w_out", "w_mix_in", "conv_w", "w_conv_out", "w_attn_out", "w_mix_out",
             "ffn2_w_in", "ffn2_w_out", "w_ple_gate", "w_ple_proj")
_GAINS = ("ffn1_norm", "mix_norm", "ffn2_norm", "ple_norm", "final_norm")
_WEIGHTS = ("ffn1_norm", "ffn1_w_in", "ffn1_w_out", "mix_norm", "w_mix_in", "conv_w", "w_conv_out", "w_attn_out", "w_mix_out",
            "ffn2_norm", "ffn2_w_in", "ffn2_w_out", "ple_norm", "w_ple_gate", "w_ple_proj", "final_norm")
CONV_ROWS = 8
_TRANSPOSED = ("ffn1_w_in", "ffn2_w_in")


def _columns_from_shards(g):
    return jnp.transpose(g, (1, 0, 2)).reshape(g.shape[1], N_SHARDS * g.shape[2])


def _shards_from_columns(a):
    r, c = a.shape
    return jnp.transpose(a.reshape(r, N_SHARDS, c // N_SHARDS), (1, 0, 2))


def kernel(x, p, ffn1_norm, ffn1_w_in, ffn1_w_out, mix_norm, w_mix_in, conv_w, w_conv_out, w_attn_out, w_mix_out, ffn2_norm, ffn2_w_in, ffn2_w_out, ple_norm, w_ple_gate, w_ple_proj, final_norm, loss_target, m_ffn1_norm, m_ffn1_w_in, m_ffn1_w_out, m_mix_norm, m_w_mix_in, m_conv_w, m_w_conv_out, m_w_attn_out, m_w_mix_out, m_ffn2_norm, m_ffn2_w_in, m_ffn2_w_out, m_ple_norm, m_w_ple_gate, m_w_ple_proj, m_final_norm, v_ffn1_norm, v_ffn1_w_in, v_ffn1_w_out, v_mix_norm, v_w_mix_in, v_conv_w, v_w_conv_out, v_w_attn_out, v_w_mix_out, v_ffn2_norm, v_ffn2_w_in, v_ffn2_w_out, v_ple_norm, v_w_ple_gate, v_w_ple_proj, v_final_norm):
    given = dict(locals())
    t = x.shape[1]
    xs = x.reshape(t, D_MODEL)
    ps = p.reshape(t, PLE_DIM)
    target = loss_target.reshape(t, D_MODEL)
    shard = {k: given[k].reshape(given[k].shape[-2:]) for k in _MATRICES}
    gain = {k: given[k].reshape(1, D_MODEL) for k in _GAINS}

    send = {k: shard[k].astype(BF) for k in _MATRICES}
    send["conv_w"] = jnp.pad(shard["conv_w"], ((0, CONV_ROWS - 3), (0, 0)))
    loss_vec, dx, landed, gain_grads = _forward_backward(xs, ps, target, gain, send)
    gain_rows = jnp.concatenate([gain_grads[k] for k in _GAINS] + [loss_vec, jnp.zeros((8 - len(_GAINS) - 1, D_MODEL), F32)], axis=0)
    gain_parts, = _exchange_alone("gather", [gain_rows], "gather_gain_gradients")

    out = {}
    for k in _MATRICES:
        w, m, v = shard[k], given["m_" + k].reshape(shard[k].shape), given["v_" + k].reshape(shard[k].shape)
        part = landed[k]
        if k == "conv_w":
            pad = ((0, CONV_ROWS - 3), (0, 0))
            w, m, v = jnp.pad(w, pad), jnp.pad(m, pad), jnp.pad(v, pad, constant_values=1.0)
        if k in _TRANSPOSED:
            w, m, v = w.T, m.T, v.T
        res = _adamw(part, w, m, v, "adamw_" + k)
        out[k] = [r[:3] if k == "conv_w" else (r.T if k in _TRANSPOSED else r) for r in res]
    stack = lambda pre: jnp.concatenate([given[pre + k].reshape(1, D_MODEL) for k in _GAINS] + [jnp.ones((8 - len(_GAINS), D_MODEL), F32)], axis=0)
    res = _adamw(gain_parts, stack(""), stack("m_"), stack("v_"), "adamw_gains")
    for j, k in enumerate(_GAINS):
        out[k] = [r[j:j + 1] for r in res]

    loss = jnp.sum(gain_parts[:, len(_GAINS), 0])
    per_kind = [[out[k][j].reshape(given[k].shape) for k in _WEIGHTS] for j in range(4)]
    return (loss, dx.reshape(x.shape), *per_kind[0], *per_kind[1], *per_kind[2], *per_kind[3])


def _forward_backward(xs, ps, target, gain, send, full=None):
    exchange = full is None
    full = dict(full or {})
    grads, landed = {}, {}

    def gather(names):
        return ("gather", [send[k] for k in names]) if exchange else None

    def scatter(names):
        return ("scatter", [grads[k] for k in names]) if exchange else None

    def keep(into, names, got):
        into.update(zip(names, got))

    first = ("ffn1_w_in",)
    (n1,), got = _prenorm(xs, gain["ffn1_norm"], ride=gather(first))
    keep(full, first, got)
    w1_in = full["ffn1_w_in"]
    second = ("ffn1_w_out", "w_mix_in")
    (act1, to_gate1, to_up1), got = _ffn_up(n1, w1_in, "ffn1_up", ride=gather(second))
    keep(full, second, got)
    w1_out = full["ffn1_w_out"].reshape(N_FF_CHUNKS, FF_CHUNK, D_MODEL)
    third = ("conv_w", "w_conv_out", "w_attn_out", "w_mix_out")
    (h1, u), got = _ffn_down(xs, act1, w1_out, gain["mix_norm"], "ffn1_down", ride=gather(third))
    keep(full, third, got)
    w_mix = full["w_mix_in"]
    w_co, w_ao, w_mo = (full[k].reshape(D_MODEL, D_MODEL) for k in ("w_conv_out", "w_attn_out", "w_mix_out"))
    taps = _columns_from_shards(full["conv_w"][:, :3, :])
    rest = ("ffn2_w_in", "ffn2_w_out", "w_ple_gate", "w_ple_proj")
    (proj,), got = _mix_proj(u, w_mix, ride=gather(rest))
    keep(full, rest, got)
    w2_in, w2_out = full["ffn2_w_in"], full["ffn2_w_out"].reshape(N_FF_CHUNKS, FF_CHUNK, D_MODEL)
    w_pg = full["w_ple_gate"].reshape(D_MODEL, D_MODEL)
    w_pp = _columns_from_shards(full["w_ple_proj"])
    o, o_bf, a_first, beta_first, reach = _attn_fwd(proj)
    h2, n3, ycin, yc, ya, merged = _mixer_out(proj, o_bf, h1, taps, w_co, w_ao, w_mo, gain["ffn2_norm"])
    (act2, to_gate2, to_up2), _ = _ffn_up(n3, w2_in, "ffn2_up")
    (h3, n4), _ = _ffn_down(h2, act2, w2_out, gain["ple_norm"], "ffn2_down")
    dh3, ds, dpp, loss_vec, dg_final, dg_ple = _tail(h3, n4, ps, w_pg, w_pp, gain["ple_norm"], gain["final_norm"], target)

    one = lambda a: a[None]
    by_rows = lambda g, rows: g.reshape(N_SHARDS, rows // N_SHARDS, D_MODEL)
    square = WGRAD_TILE // 2
    grads["w_ple_gate"] = by_rows(_wgrad(one(n4), one(ds), "wgrad_ple_gate", tile=square), D_MODEL)
    grads["w_ple_proj"] = _shards_from_columns(_wgrad(one(ps), one(dpp), "wgrad_ple_proj")[0])
    ple = ("w_ple_gate", "w_ple_proj")
    (df2, dgate2, dup2), got = _ffn_bwd_hidden(dh3, to_gate2, to_up2, w2_out, "ffn2_bwd_hidden", ride=scatter(ple))
    keep(landed, ple, got)
    grads["ffn2_w_out"] = by_rows(_wgrad(act2, one(df2), "wgrad_ffn2_out"), D_FF)
    grads["ffn2_w_in"] = _wgrad_pieces(n3, [dgate2, dup2], "wgrad_ffn2_in", transposed=True)
    (dh2, dg_ffn2), got = _ffn_bwd_input(dh3, h2, gain["ffn2_norm"], dgate2, dup2, w2_in, "ffn2_bwd_input", ride=scatter(("ffn2_w_out",)))
    keep(landed, ("ffn2_w_out",), got)
    (dh2b, dyc, dya, dgc, dga, dcb, dcv, d_o), _ = _mixer_bwd(dh2, proj, yc, ya, taps, w_co, w_ao, w_mo)
    grads["w_mix_out"] = by_rows(_wgrad(one(merged), one(dh2b), "wgrad_mix_out", tile=square), D_MODEL)
    grads["w_conv_out"] = by_rows(_wgrad(one(ycin), one(dyc), "wgrad_conv_out", tile=square), D_MODEL)
    grads["w_attn_out"] = by_rows(_wgrad(one(o_bf), one(dya), "wgrad_attn_out", tile=square), D_MODEL)
    dcc, dcx, dtaps = _conv_bwd(dcv, proj, taps)
    grads["conv_w"] = jnp.pad(_shards_from_columns(dtaps[:3]), ((0, 0), (0, CONV_ROWS - 3), (0, 0)))
    behind_attn = ("ffn2_w_in", "w_mix_out", "w_conv_out", "w_attn_out", "conv_w")
    (dq, dk, dv), got = _attn_bwd(proj, o, d_o, a_first, beta_first, reach, ride=scatter(behind_attn))
    keep(landed, behind_attn, got)
    dpieces = [dcb, dcc, dcx, dq, dk, dv, dgc, dga]
    half = N_MIX // 2
    tops, bottoms = zip(_wgrad_pieces(u, [one(dp) for dp in dpieces[:half]], "wgrad_mix_in_a", tile=WGRAD_TILE // 2, row_parts=2),
                        _wgrad_pieces(u, [one(dp) for dp in dpieces[half:]], "wgrad_mix_in_b", tile=WGRAD_TILE // 2, row_parts=2))
    grads["w_mix_in top"], grads["w_mix_in bottom"] = jnp.concatenate(tops, axis=0), jnp.concatenate(bottoms, axis=0)
    (dh1, dg_mix), top = _mix_bwd(dpieces, w_mix, h1, dh2, gain["mix_norm"], ride=scatter(("w_mix_in top",)))
    (df1, dgate1, dup1), bottom = _ffn_bwd_hidden(dh1, to_gate1, to_up1, w1_out, "ffn1_bwd_hidden", ride=scatter(("w_mix_in bottom",)))
    if exchange:
        landed["w_mix_in"] = [top[0], bottom[0]]
    else:
        grads["w_mix_in"] = jnp.concatenate([grads.pop("w_mix_in top"), grads.pop("w_mix_in bottom")], axis=1)
    grads["ffn1_w_out"] = by_rows(_wgrad(act1, one(df1), "wgrad_ffn1_out"), D_FF)
    if exchange:
        grads["ffn1_w_in"], got = _wgrad_pieces(n1, [dgate1, dup1], "wgrad_ffn1_in", transposed=True, ride=scatter(("ffn1_w_out",)))
        keep(landed, ("ffn1_w_out",), got)
    else:
        grads["ffn1_w_in"] = _wgrad_pieces(n1, [dgate1, dup1], "wgrad_ffn1_in", transposed=True)
    (dx, dg_ffn1), got = _ffn_bwd_input(dh1, xs, gain["ffn1_norm"], dgate1, dup1, w1_in, "ffn1_bwd_input", ride=scatter(("ffn1_w_in",)))
    keep(landed, ("ffn1_w_in",), got)
    gain_grads = dict(ffn1_norm=dg_ffn1, mix_norm=dg_mix, ffn2_norm=dg_ffn2, ple_norm=dg_ple, final_norm=dg_final)
    return loss_vec, dx, (landed if exchange else grads), gain_grads
```

```python
import functools
import math

import jax
import jax.numpy as jnp
from jax import lax
from jax.experimental import pallas as pl
from jax.experimental.pallas import tpu as pltpu

D_MODEL = 1024
D_FF = 2816
N_SHARDS = 8
FF_CHUNK = 2 * D_FF // N_SHARDS
N_FF_CHUNKS = D_FF // FF_CHUNK
N_HEADS = 8
HEAD_DIM = 128
PLE_DIM = 256
NORM_EPS = 1e-6
N_MIX = 8
ADAM_LR, ADAM_B1, ADAM_B2, ADAM_EPS, ADAM_WD, ADAM_STEP = 0.001, 0.9, 0.999, 1e-08, 0.01, 10

TOKEN_TILE = 512
WGRAD_TILE = 4096
PROJ_TILE = 2048
ATTN_ROWS = 512
ATTN_Q = 64
ATTN_SUB = 128
ATTN_K = 2 * ATTN_SUB
ATTN_SKIP_BELOW = -90.0

BF = jnp.bfloat16
F32 = jnp.float32
MESH = pl.DeviceIdType.MESH
NT = (((1,), (1,)), ((), ()))
TN = (((0,), (0,)), ((), ()))
S = jax.ShapeDtypeStruct
ANY = pl.BlockSpec(memory_space=pl.ANY)


def _const_spec(shape):
    nd = len(shape)
    return pl.BlockSpec(shape, lambda *_: (0,) * nd, pipeline_mode=pl.Buffered(1))


def _rows(tm, cols):
    return pl.BlockSpec((tm, cols), lambda i: (i, 0))


def _chunks(tm):
    return pl.BlockSpec((N_FF_CHUNKS, tm, FF_CHUNK), lambda i: (0, i, 0))


def _acc_spec(shape):
    nd = len(shape)
    return pl.BlockSpec(shape, lambda *_: (0,) * nd)


def _dot(a, b):
    return jnp.dot(a, b, preferred_element_type=F32)


def _dot_nt(a, b):
    return lax.dot_general(a, b, NT, preferred_element_type=F32)


def _dot_tn(a, b):
    return lax.dot_general(a, b, TN, preferred_element_type=F32)


def _rms(h, g):
    r = lax.rsqrt(jnp.mean(h * h, axis=-1, keepdims=True) + NORM_EPS)
    return h * r * g


def _rms_bwd(dn, h, g):
    r = lax.rsqrt(jnp.mean(h * h, axis=-1, keepdims=True) + NORM_EPS)
    nh = h * r
    gd = dn * g
    dh = r * (gd - nh * jnp.mean(gd * nh, axis=-1, keepdims=True))
    return dh, jnp.sum(dn * nh, axis=0, keepdims=True)


def _accumulate(ref, val):
    @pl.when(pl.program_id(0) == 0)
    def _():
        ref[...] = jnp.zeros_like(ref)
    ref[...] += val


def _place():
    x, y, c = lax.axis_index("x"), lax.axis_index("y"), lax.axis_index("c")
    return x, y, c


def _slot(px, py, pc):
    return 4 * px + 2 * py + pc


def _gather_phases(ins, outs, send_sems, recv_sems, local_sems):
    n = len(ins)

    def parties():
        x, y, c = _place()
        return (x, y, c), (x, y, 1 - c), [(1 - x, y), (x, 1 - y), (1 - x, 1 - y)], c

    def copy(a, k, block, to, src=None):
        dst = outs[a].at[_slot(*block)]
        return pltpu.make_async_remote_copy(
            src_ref=dst if src is None else src, dst_ref=dst,
            send_sem=send_sems.at[a, k], recv_sem=recv_sems.at[a, k],
            device_id=to, device_id_type=MESH)

    def own(a, me):
        return pltpu.make_async_copy(ins[a], outs[a].at[_slot(*me)], local_sems.at[a])

    def first(a, me, sibling, chips, c):
        return [copy(a, 0, me, sibling, src=ins[a])] + [copy(a, 1 + j, me, (*chip, c), src=ins[a]) for j, chip in enumerate(chips)]

    def start():
        me, sibling, chips, c = parties()
        for a in range(n):
            own(a, me).start()
        for a in range(n):
            for cp in first(a, me, sibling, chips, c):
                cp.start()

    def forward():
        me, sibling, chips, c = parties()
        for j, chip in enumerate(chips):
            for a in range(n):
                copy(a, 1 + j, (*chip, c), me).wait_recv()
                copy(a, 4 + j, (*chip, c), sibling).start()

    def finish():
        me, sibling, chips, c = parties()
        for a in range(n):
            copy(a, 0, sibling, me).wait_recv()
            for j, chip in enumerate(chips):
                copy(a, 4 + j, (*chip, 1 - c), me).wait_recv()
        for a in range(n):
            for cp in first(a, me, sibling, chips, c) + [copy(a, 4 + j, (*chip, c), sibling) for j, chip in enumerate(chips)]:
                cp.wait_send()
            own(a, me).wait()

    return [start, forward, finish]


def _scatter_phases(ins, outs, send_sems, recv_sems, local_sems):
    n = len(ins)

    def copies():
        x, y, c = _place()
        me = _slot(x, y, c)
        out = [pltpu.make_async_copy(ins[a].at[me], outs[a].at[me], local_sems.at[a]) for a in range(n)]
        for k in range(1, N_SHARDS):
            px = 1 - x if k & 4 else x
            py = 1 - y if k & 2 else y
            pc = 1 - c if k & 1 else c
            for a in range(n):
                out.append(pltpu.make_async_remote_copy(
                    src_ref=ins[a].at[_slot(px, py, pc)], dst_ref=outs[a].at[me],
                    send_sem=send_sems.at[a, k - 1], recv_sem=recv_sems.at[a, k - 1],
                    device_id=(px, py, pc), device_id_type=MESH))
        return out

    def start():
        for cp in copies():
            cp.start()

    def finish():
        for cp in copies():
            cp.wait()

    return [start, finish]


def _pallas(body, *, name, grid, in_specs, out_specs, out_shape, args, scratch_shapes=(), ride=None):
    if ride is None:
        outs = pl.pallas_call(body, name=name, grid=grid, in_specs=in_specs, out_specs=out_specs, out_shape=out_shape,
                              scratch_shapes=list(scratch_shapes))(*args)
        return list(outs), []
    kind, arrays = ride
    n, n_in, n_out, n_scr = len(arrays), len(in_specs), len(out_specs), len(scratch_shapes)
    total = math.prod(grid)
    middle = (9 * total) // 10
    landed_shape = [S((N_SHARDS,) + a.shape if kind == "gather" else a.shape, a.dtype) for a in arrays]

    def with_exchange(*refs):
        ins, riders_in = refs[:n_in], refs[n_in:n_in + n]
        outs, riders_out = refs[n_in + n:n_in + n + n_out], refs[n_in + n + n_out:n_in + 2 * n + n_out]
        scratch, sems = refs[n_in + 2 * n + n_out:n_in + 2 * n + n_out + n_scr], refs[n_in + 2 * n + n_out + n_scr:]
        step = 0
        for axis, size in enumerate(grid):
            step = step * size + pl.program_id(axis)
        phases = (_gather_phases if kind == "gather" else _scatter_phases)(riders_in, riders_out, *sems)
        pl.when(step == 0)(phases[0])
        body(*ins, *outs, *scratch)
        for phase in phases[1:-1]:
            pl.when(step == middle)(phase)
        pl.when(step == total - 1)(phases[-1])

    outs = pl.pallas_call(
        with_exchange, name=name, grid=grid,
        in_specs=list(in_specs) + [ANY] * n, out_specs=list(out_specs) + [ANY] * n,
        out_shape=list(out_shape) + landed_shape,
        scratch_shapes=list(scratch_shapes) + [pltpu.SemaphoreType.DMA((n, 7)), pltpu.SemaphoreType.DMA((n, 7)),
                                               pltpu.SemaphoreType.DMA((n,))],
    )(*args, *arrays)
    return list(outs[:n_out]), list(outs[n_out:])


def _exchange_alone(kind, arrays, name):
    return _pallas(lambda: None, name=name, grid=(1,), in_specs=[], out_specs=[], out_shape=[], args=[], ride=(kind, arrays))[1]


def _prenorm(x, g, ride=None):
    t = x.shape[0]
    tm = min(TOKEN_TILE, t)

    def body(x_ref, g_ref, n_ref):
        n_ref[...] = _rms(x_ref[...], g_ref[...]).astype(BF)

    return _pallas(
        body, name="prenorm", grid=(t // tm,), ride=ride,
        in_specs=[_rows(tm, D_MODEL), _const_spec((1, D_MODEL))], out_specs=[_rows(tm, D_MODEL)],
        out_shape=[S((t, D_MODEL), BF)], args=[x, g])


def _ffn_up(n, w_in, name, ride=None):
    t = n.shape[0]
    tm = min(TOKEN_TILE, t)

    def body(n_ref, win_ref, act_ref, to_gate_ref, to_up_ref):
        nb = n_ref[...]
        for c in range(N_FF_CHUNKS):
            gate = _dot(nb, win_ref[c])
            up = _dot(nb, win_ref[N_FF_CHUNKS + c])
            sg = jax.nn.sigmoid(gate)
            silu = gate * sg
            act_ref[c] = (silu * up).astype(BF)
            to_gate_ref[c] = (up * (sg * (1.0 + gate * (1.0 - sg)))).astype(BF)
            to_up_ref[c] = silu.astype(BF)

    return _pallas(
        body, name=name, grid=(t // tm,), ride=ride,
        in_specs=[_rows(tm, D_MODEL), _const_spec(w_in.shape)],
        out_specs=[_chunks(tm)] * 3, out_shape=[S((N_FF_CHUNKS, t, FF_CHUNK), BF)] * 3,
        args=[n, w_in])


def _ffn_down(h, act, w_out, g_next, name, ride=None):
    t = h.shape[0]
    tm = min(TOKEN_TILE, t)

    def body(h_ref, act_ref, wout_ref, g_ref, ho_ref, no_ref):
        acc = jnp.zeros((tm, D_MODEL), F32)
        for c in range(N_FF_CHUNKS):
            acc = acc + _dot(act_ref[c], wout_ref[c])
        ho = h_ref[...] + 0.5 * acc
        ho_ref[...] = ho
        no_ref[...] = _rms(ho, g_ref[...]).astype(BF)

    return _pallas(
        body, name=name, grid=(t // tm,), ride=ride,
        in_specs=[_rows(tm, D_MODEL), _chunks(tm), _const_spec(w_out.shape), _const_spec((1, D_MODEL))],
        out_specs=[_rows(tm, D_MODEL)] * 2, out_shape=[S((t, D_MODEL), F32), S((t, D_MODEL), BF)],
        args=[h, act, w_out, g_next])


def _mix_proj(u, w_mix, ride=None):
    t = u.shape[0]
    tm = min(PROJ_TILE, t)

    def body(u_ref, w_ref, o_ref):
        o_ref[0] = _dot(u_ref[...], w_ref[0]).astype(BF)

    return _pallas(
        body, name="mix_proj", grid=(N_MIX, t // tm), ride=ride,
        in_specs=[pl.BlockSpec((tm, D_MODEL), lambda d, i: (i, 0)), pl.BlockSpec((1, D_MODEL, D_MODEL), lambda d, i: (d, 0, 0))],
        out_specs=[pl.BlockSpec((1, tm, D_MODEL), lambda d, i: (d, i, 0))],
        out_shape=[S((N_MIX, t, D_MODEL), BF)], args=[u, w_mix])


HALO = 16


def _piece(d, tm):
    return pl.BlockSpec((1, tm, D_MODEL), lambda i: (d, i, 0))


def _prev_halo(d, tm):
    return pl.BlockSpec((1, HALO, D_MODEL), lambda i: (d, jnp.maximum(i * (tm // HALO) - 1, 0), 0))


def _shift_down(m, prev_tail, k):
    tm = m.shape[0]
    out = pltpu.roll(m, k, 0)
    row = lax.broadcasted_iota(jnp.int32, (tm, 1), 0)
    for j in range(k):
        out = jnp.where(row == j, prev_tail[HALO - k + j:HALO - k + j + 1, :], out)
    return out


def _conv_inputs(cc_ref, cx_ref, cch_ref, cxh_ref):
    m = cc_ref[0].astype(F32) * cx_ref[0].astype(F32)
    mh = cch_ref[0].astype(F32) * cxh_ref[0].astype(F32)
    mh = jnp.where(pl.program_id(0) == 0, 0.0, mh)
    return m, _shift_down(m, mh, 1), _shift_down(m, mh, 2)


def _mixer_out(proj, o, h1, conv_w, w_co, w_ao, w_mo, g_next):
    t = h1.shape[0]
    tm = min(TOKEN_TILE, t)

    def body(cb_ref, cc_ref, cx_ref, gc_ref, ga_ref, cch_ref, cxh_ref, o_ref, h_ref, cw_ref, wco_ref, wao_ref, wmo_ref,
             g_ref, ho_ref, no_ref, ycin_ref, yc_ref, ya_ref, mg_ref):
        m, m1, m2 = _conv_inputs(cc_ref, cx_ref, cch_ref, cxh_ref)
        cw = cw_ref[...]
        cv = cw[0:1, :] * m2 + cw[1:2, :] * m1 + cw[2:3, :] * m
        ycin = (cb_ref[0].astype(F32) * cv).astype(BF)
        ycin_ref[...] = ycin
        yc = _dot(ycin, wco_ref[...])
        ya = _dot(o_ref[...], wao_ref[...])
        yc_ref[...] = yc.astype(BF)
        ya_ref[...] = ya.astype(BF)
        merged = (jax.nn.sigmoid(gc_ref[0].astype(F32)) * yc + jax.nn.sigmoid(ga_ref[0].astype(F32)) * ya).astype(BF)
        mg_ref[...] = merged
        ho = h_ref[...] + _dot(merged, wmo_ref[...])
        ho_ref[...] = ho
        no_ref[...] = _rms(ho, g_ref[...]).astype(BF)

    sq = (D_MODEL, D_MODEL)
    return pl.pallas_call(
        body, name="mixer_out", grid=(t // tm,),
        in_specs=[_piece(0, tm), _piece(1, tm), _piece(2, tm), _piece(6, tm), _piece(7, tm), _prev_halo(1, tm), _prev_halo(2, tm),
                  _rows(tm, D_MODEL), _rows(tm, D_MODEL), _const_spec((3, D_MODEL)), _const_spec(sq), _const_spec(sq),
                  _const_spec(sq), _const_spec((1, D_MODEL))],
        out_specs=[_rows(tm, D_MODEL)] * 6,
        out_shape=[S((t, D_MODEL), F32)] + [S((t, D_MODEL), BF)] * 5,
    )(proj, proj, proj, proj, proj, proj, proj, o, h1, conv_w, w_co, w_ao, w_mo, g_next)


def _suffix_sums(vals, tri, before):
    out, right = [], before
    for b in reversed(range(ATTN_K // ATTN_SUB)):
        v = vals[:, b * ATTN_SUB:(b + 1) * ATTN_SUB]
        hi = v.astype(BF)
        lo = (v - hi.astype(F32)).astype(BF)
        out.append(_dot(hi, tri) + _dot(lo, tri) + right)
        right = right + jnp.sum(v, axis=1, keepdims=True)
    return jnp.concatenate(out[::-1], axis=1), right


ATTN_UNITS = ATTN_ROWS // ATTN_Q


def _unit_rows(x, u):
    return x[u * ATTN_Q:(u + 1) * ATTN_Q]


def _per_unit(fn):
    return jnp.concatenate([fn(u) for u in range(ATTN_UNITS)], axis=0)


def _per_row(vals):
    local = lax.broadcasted_iota(jnp.int32, (ATTN_ROWS, 1), 0)
    out = jnp.full((ATTN_ROWS, 1), vals[0], jnp.int32)
    for u in range(1, ATTN_UNITS):
        out = jnp.where(local >= u * ATTN_Q, vals[u], out)
    return out


def _attn_step(q, k_ref, starts, bounds, row):
    z = _per_unit(lambda u: _dot_nt(_unit_rows(q, u), k_ref[0, pl.ds(starts[u], ATTN_K), :])) * (1.0 / math.sqrt(HEAD_DIM))
    mask = lax.broadcasted_iota(jnp.int32, (1, ATTN_K), 1) < jnp.minimum(row, _per_row(bounds)) - _per_row(starts)
    log_beta = jnp.minimum(z, 0.0) - jnp.log(1.0 + jnp.exp(jnp.minimum(z, -z)))
    log_rest = jnp.where(mask, log_beta - z, 0.0)
    return z, mask, log_beta, log_rest


def _attn_sweep_start(i, t):
    blks = tuple(jnp.maximum(i * ATTN_UNITS + u + 1 - ATTN_K // ATTN_Q, 0) for u in range(ATTN_UNITS))
    return blks, tuple(jnp.int32(t) for _ in range(ATTN_UNITS))


def _attn_keys(blks):
    return [pl.multiple_of(b * ATTN_Q, ATTN_Q) for b in blks]


def _attn_next(blks):
    return tuple(jnp.maximum(b - ATTN_K // ATTN_Q, 0) for b in blks), tuple(b * ATTN_Q for b in blks)


def _attn_more(carry):
    return jnp.logical_and(carry[1][ATTN_UNITS - 1] > 0, carry[-1] > ATTN_SKIP_BELOW)


def _tri(strict):
    r = lax.broadcasted_iota(jnp.int32, (ATTN_SUB, ATTN_SUB), 0)
    c = lax.broadcasted_iota(jnp.int32, (ATTN_SUB, ATTN_SUB), 1)
    return (r > c if strict else r >= c).astype(BF)


REACH_TILE = (8, 128)


def _first_step_spec():
    return pl.BlockSpec((1, ATTN_ROWS, ATTN_K), lambda h, i: (h, i, 0))


def _reach_spec():
    return pl.BlockSpec((1, 1) + REACH_TILE, lambda h, i: (h, i, 0, 0))


def _head_cols(piece):
    return lambda t: pl.BlockSpec((1, t, HEAD_DIM), lambda h, i: (piece, 0, h))


def _attn_fwd(proj):
    t = proj.shape[1]
    nq = t // ATTN_ROWS
    tri = _tri(strict=True)

    def body(q_ref, k_ref, v_ref, tri_ref, o_ref, ob_ref, a_ref, beta_ref, reach_ref):
        i = pl.program_id(1)
        q = q_ref[0]
        row = i * ATTN_ROWS + lax.broadcasted_iota(jnp.int32, (ATTN_ROWS, 1), 0)

        def step(carry, keep=False):
            blks, bounds, acc, run, _ = carry
            starts = _attn_keys(blks)
            _, mask, log_beta, log_rest = _attn_step(q, k_ref, starts, bounds, row)
            tail, run = _suffix_sums(log_rest, tri_ref[...], run)
            a = jnp.where(mask, jnp.exp(log_beta + tail), 0.0).astype(BF)
            if keep:
                a_ref[0] = a
                beta_ref[0] = jnp.where(mask, jnp.exp(log_beta), 0.0).astype(BF)
            acc = acc + _per_unit(lambda u: _dot(_unit_rows(a, u), v_ref[0, pl.ds(starts[u], ATTN_K), :]))
            return (*_attn_next(blks), acc, run, jnp.max(run))

        first = (*_attn_sweep_start(i, t), jnp.zeros((ATTN_ROWS, HEAD_DIM), F32), jnp.zeros((ATTN_ROWS, 1), F32), jnp.float32(0.0))
        after_first = step(first, keep=True)
        reach_ref[...] = jnp.full(reach_ref.shape, after_first[-1], F32)
        o = lax.while_loop(_attn_more, step, after_first)[2]
        o_ref[...] = o
        ob_ref[...] = o.astype(BF)

    qspec = pl.BlockSpec((1, ATTN_ROWS, HEAD_DIM), lambda h, i: (3, i, h))
    rowblk = pl.BlockSpec((ATTN_ROWS, HEAD_DIM), lambda h, i: (i, h))
    return pl.pallas_call(
        body, name="attn_fwd", grid=(N_HEADS, nq),
        in_specs=[qspec, _head_cols(4)(t), _head_cols(5)(t), pl.BlockSpec((ATTN_SUB, ATTN_SUB), lambda h, i: (0, 0))],
        out_specs=[rowblk, rowblk, _first_step_spec(), _first_step_spec(), _reach_spec()],
        out_shape=[S((t, D_MODEL), F32), S((t, D_MODEL), BF), S((N_HEADS, t, ATTN_K), BF), S((N_HEADS, t, ATTN_K), BF),
                   S((N_HEADS, nq) + REACH_TILE, F32)],
    )(proj, proj, proj, tri)


def _attn_bwd(proj, o, d_o, a_first, beta_first, reach, ride=None):
    t = proj.shape[1]
    nq = t // ATTN_ROWS
    tri_strict, tri_incl = _tri(strict=True), _tri(strict=False)
    scale = 1.0 / math.sqrt(HEAD_DIM)

    def body(q_ref, k_ref, v_ref, o_ref, do_ref, a_ref, beta_ref, reach_ref, tris_ref, trii_ref, dq_ref, dk_ref, dv_ref, dk_acc, dv_acc):
        i = pl.program_id(1)

        @pl.when(i == 0)
        def _():
            dk_acc[...] = jnp.zeros_like(dk_acc)
            dv_acc[...] = jnp.zeros_like(dv_acc)

        q = q_ref[0]
        do = do_ref[...]
        total = jnp.sum(do.astype(F32) * o_ref[...], axis=1, keepdims=True)
        zero = jnp.zeros((ATTN_ROWS, 1), F32)
        blks0, bounds0 = _attn_sweep_start(i, t)

        def finish(starts, a, dz, dq):
            dzb = (dz * scale).astype(BF)
            for u in range(ATTN_UNITS):
                dv_acc[pl.ds(starts[u], ATTN_K), :] += _dot_tn(_unit_rows(a, u), _unit_rows(do, u))
                dk_acc[pl.ds(starts[u], ATTN_K), :] += _dot_tn(_unit_rows(dzb, u), _unit_rows(q, u))
            return dq + _per_unit(lambda u: _dot(_unit_rows(dzb, u), k_ref[0, pl.ds(starts[u], ATTN_K), :]))

        def grad_a(starts, a):
            return _per_unit(lambda u: _dot_nt(_unit_rows(do, u), v_ref[0, pl.ds(starts[u], ATTN_K), :])) * a.astype(F32)

        one_step = jnp.max(reach_ref[...]) <= ATTN_SKIP_BELOW

        @pl.when(one_step)
        def _():
            starts = _attn_keys(blks0)
            a = a_ref[0]
            beta = beta_ref[0].astype(F32)
            de = grad_a(starts, a)
            right, _ = _suffix_sums(de, trii_ref[...], zero)
            dz = de * (1.0 - beta) - (total - right) * beta
            dq_ref[...] = finish(starts, a, dz, jnp.zeros((ATTN_ROWS, HEAD_DIM), F32)).astype(BF)

        @pl.when(jnp.logical_not(one_step))
        def _():
            row = i * ATTN_ROWS + lax.broadcasted_iota(jnp.int32, (ATTN_ROWS, 1), 0)

            def step(carry):
                blks, bounds, dq, seen, run, _ = carry
                starts = _attn_keys(blks)
                z, mask, log_beta, log_rest = _attn_step(q, k_ref, starts, bounds, row)
                tail, run = _suffix_sums(log_rest, tris_ref[...], run)
                a = jnp.where(mask, jnp.exp(log_beta + tail), 0.0).astype(BF)
                de = grad_a(starts, a)
                right, seen = _suffix_sums(de, trii_ref[...], seen)
                beta = jax.nn.sigmoid(z)
                dz = jnp.where(mask, de * (1.0 - beta) - (total - right) * beta, 0.0)
                return (*_attn_next(blks), finish(starts, a, dz, dq), seen, run, jnp.max(run))

            first = (blks0, bounds0, jnp.zeros((ATTN_ROWS, HEAD_DIM), F32), zero, zero, jnp.float32(0.0))
            dq_ref[...] = lax.while_loop(_attn_more, step, step(first))[2].astype(BF)

        @pl.when(i == nq - 1)
        def _():
            dk_ref[...] = dk_acc[...].astype(BF)
            dv_ref[...] = dv_acc[...].astype(BF)

    qspec = pl.BlockSpec((1, ATTN_ROWS, HEAD_DIM), lambda h, i: (3, i, h))
    rowblk = pl.BlockSpec((ATTN_ROWS, HEAD_DIM), lambda h, i: (i, h))
    head = pl.BlockSpec((t, HEAD_DIM), lambda h, i: (0, h))
    trispec = pl.BlockSpec((ATTN_SUB, ATTN_SUB), lambda h, i: (0, 0))
    return _pallas(
        body, name="attn_bwd", grid=(N_HEADS, nq), ride=ride,
        in_specs=[qspec, _head_cols(4)(t), _head_cols(5)(t), rowblk, rowblk, _first_step_spec(), _first_step_spec(), _reach_spec(),
                  trispec, trispec],
        out_specs=[rowblk, head, head],
        out_shape=[S((t, D_MODEL), BF)] * 3,
        scratch_shapes=[pltpu.VMEM((t, HEAD_DIM), F32), pltpu.VMEM((t, HEAD_DIM), F32)],
        args=[proj, proj, proj, o, d_o, a_first, beta_first, reach, tri_strict, tri_incl])


def _tail(h3, n4, p, w_pg, w_pp, g_ple, g_final, target):
    t = h3.shape[0]
    tm = min(TOKEN_TILE, t)
    steps = t // tm

    def body(h_ref, n_ref, p_ref, wpg_ref, wpp_ref, gp_ref, gf_ref, tgt_ref,
             dh_ref, ds_ref, dpp_ref, loss_ref, dgf_ref, dgp_ref):
        pg = jax.nn.sigmoid(_dot(n_ref[...], wpg_ref[...]))
        pp = _dot(p_ref[...].astype(BF), wpp_ref[...])
        h3v = h_ref[...]
        h4 = h3v + pg * pp
        gf = gf_ref[...]
        diff = _rms(h4, gf) - tgt_ref[...]
        _accumulate(loss_ref, jnp.sum(diff * diff, axis=0, keepdims=True))
        dh4, dgf = _rms_bwd(diff * (1.0 / D_MODEL), h4, gf)
        _accumulate(dgf_ref, dgf)
        dpp_ref[...] = (dh4 * pg).astype(BF)
        ds = (dh4 * pp * pg * (1.0 - pg)).astype(BF)
        ds_ref[...] = ds
        dh3, dgp = _rms_bwd(_dot_nt(ds, wpg_ref[...]), h3v, gp_ref[...])
        _accumulate(dgp_ref, dgp)
        dh_ref[...] = dh4 + dh3

        @pl.when(pl.program_id(0) == steps - 1)
        def _():
            loss_ref[...] = jnp.full(loss_ref.shape, 0.5 / D_MODEL * jnp.sum(loss_ref[...]), F32)

    vec = (1, D_MODEL)
    return pl.pallas_call(
        body, name="tail", grid=(steps,),
        in_specs=[_rows(tm, D_MODEL), _rows(tm, D_MODEL), _rows(tm, PLE_DIM), _const_spec((D_MODEL, D_MODEL)),
                  _const_spec((PLE_DIM, D_MODEL)), _const_spec(vec), _const_spec(vec), _rows(tm, D_MODEL)],
        out_specs=[_rows(tm, D_MODEL)] * 3 + [_acc_spec(vec)] * 3,
        out_shape=[S((t, D_MODEL), F32), S((t, D_MODEL), BF), S((t, D_MODEL), BF)] + [S(vec, F32)] * 3,
    )(h3, n4, p, w_pg, w_pp, g_ple, g_final, target)


def _wgrad(xs, ys, name, ride=None, tile=None):
    bx, t, k = xs.shape
    by, _, n = ys.shape
    b = max(bx, by)
    tt = min(tile or WGRAD_TILE * 2 // xs.dtype.itemsize, t)
    steps = t // tt

    def body(x_ref, y_ref, o_ref, acc_ref):
        s = pl.program_id(1)

        @pl.when(s == 0)
        def _():
            acc_ref[...] = jnp.zeros_like(acc_ref)
        acc_ref[...] += _dot_tn(x_ref[0].astype(BF), y_ref[0].astype(BF))

        @pl.when(s == steps - 1)
        def _():
            o_ref[0] = acc_ref[...].astype(BF)

    (out,), landed = _pallas(
        body, name=name, grid=(b, steps), ride=ride,
        in_specs=[pl.BlockSpec((1, tt, k), (lambda j, s: (j, s, 0)) if bx > 1 else (lambda j, s: (0, s, 0))),
                  pl.BlockSpec((1, tt, n), (lambda j, s: (j, s, 0)) if by > 1 else (lambda j, s: (0, s, 0)))],
        out_specs=[pl.BlockSpec((1, k, n), lambda j, s: (j, 0, 0))],
        out_shape=[S((b, k, n), BF)],
        scratch_shapes=[pltpu.VMEM((k, n), F32)],
        args=[xs, ys])
    return (out, landed) if ride is not None else out


def _wgrad_pieces(x, ys, name, ride=None, tile=None, row_parts=1, transposed=False):
    t, k = x.shape
    n = ys[0].shape[2]
    counts = [y.shape[0] for y in ys]
    offsets = [sum(counts[:j]) for j in range(len(ys))]
    total = sum(counts)
    tt = min(tile or WGRAD_TILE, t)
    steps = t // tt
    rows, cols = (n, k) if transposed else (k, n)
    kp = rows // row_parts

    def body(x_ref, *refs):
        y_refs, o_refs, acc_ref = refs[:len(ys)], refs[len(ys):len(ys) + row_parts], refs[len(ys) + row_parts]
        p, s = pl.program_id(0), pl.program_id(1)

        @pl.when(s == 0)
        def _():
            acc_ref[...] = jnp.zeros_like(acc_ref)
        for j, y_ref in enumerate(y_refs):
            @pl.when(jnp.logical_and(p >= offsets[j], p < offsets[j] + counts[j]))
            def _(y_ref=y_ref):
                acc_ref[...] += _dot_tn(y_ref[0], x_ref[...]) if transposed else _dot_tn(x_ref[...], y_ref[0])

        @pl.when(s == steps - 1)
        def _():
            for part, o_ref in enumerate(o_refs):
                o_ref[0] = acc_ref[part * kp:(part + 1) * kp, :].astype(BF)

    def turn(j):
        lo, hi = offsets[j], offsets[j] + counts[j]
        return lambda p, s: (jnp.clip(p - lo, 0, counts[j] - 1), jnp.where(p < lo, 0, jnp.where(p >= hi, steps - 1, s)), 0)

    outs, landed = _pallas(
        body, name=name, grid=(total, steps), ride=ride,
        in_specs=[pl.BlockSpec((tt, k), lambda p, s: (s, 0))] + [pl.BlockSpec((1, tt, n), turn(j)) for j in range(len(ys))],
        out_specs=[pl.BlockSpec((1, kp, cols), lambda p, s: (p, 0, 0))] * row_parts,
        out_shape=[S((total, kp, cols), BF)] * row_parts,
        scratch_shapes=[pltpu.VMEM((rows, cols), F32)],
        args=[x, *ys])
    out = outs[0] if row_parts == 1 else outs
    return (out, landed) if ride is not None else out


def _ffn_bwd_hidden(dh, to_gate, to_up, w_out, name, ride=None):
    t = dh.shape[0]
    tm = min(TOKEN_TILE, t)

    def body(dh_ref, to_gate_ref, to_up_ref, wout_ref, df_ref, dgate_ref, dup_ref):
        df = (0.5 * dh_ref[...]).astype(BF)
        df_ref[...] = df
        for c in range(N_FF_CHUNKS):
            dact = _dot_nt(df, wout_ref[c])
            dgate_ref[c] = (dact * to_gate_ref[c].astype(F32)).astype(BF)
            dup_ref[c] = (dact * to_up_ref[c].astype(F32)).astype(BF)

    return _pallas(
        body, name=name, grid=(t // tm,), ride=ride,
        in_specs=[_rows(tm, D_MODEL), _chunks(tm), _chunks(tm), _const_spec(w_out.shape)],
        out_specs=[_rows(tm, D_MODEL), _chunks(tm), _chunks(tm)],
        out_shape=[S((t, D_MODEL), BF)] + [S((N_FF_CHUNKS, t, FF_CHUNK), BF)] * 2,
        args=[dh, to_gate, to_up, w_out])


def _ffn_bwd_input(dh, h_in, g, dgate, dup, w_in, name, ride=None):
    t = dh.shape[0]
    tm = min(TOKEN_TILE, t)

    def body(dh_ref, h_ref, g_ref, dgate_ref, dup_ref, win_ref, dhi_ref, dg_ref):
        dn = jnp.zeros((tm, D_MODEL), F32)
        for c in range(N_FF_CHUNKS):
            dn = dn + _dot_nt(dgate_ref[c], win_ref[c]) + _dot_nt(dup_ref[c], win_ref[N_FF_CHUNKS + c])
        dhi, dg = _rms_bwd(dn, h_ref[...], g_ref[...])
        _accumulate(dg_ref, dg)
        dhi_ref[...] = dh_ref[...] + dhi

    vec = (1, D_MODEL)
    return _pallas(
        body, name=name, grid=(t // tm,), ride=ride,
        in_specs=[_rows(tm, D_MODEL), _rows(tm, D_MODEL), _const_spec(vec), _chunks(tm), _chunks(tm), _const_spec(w_in.shape)],
        out_specs=[_rows(tm, D_MODEL), _acc_spec(vec)],
        out_shape=[S((t, D_MODEL), F32), S(vec, F32)],
        args=[dh, h_in, g, dgate, dup, w_in])


def _mixer_bwd(dh2, proj, yc, ya, conv_w, w_co, w_ao, w_mo, ride=None):
    t = dh2.shape[0]
    tm = min(TOKEN_TILE, t)

    def body(dh_ref, cb_ref, cc_ref, cx_ref, gc_ref, ga_ref, cch_ref, cxh_ref, yc_ref, ya_ref, cw_ref, wco_ref, wao_ref, wmo_ref,
             dhb_ref, dyc_ref, dya_ref, dgc_ref, dga_ref, dcb_ref, dcv_ref, do_ref):
        dhb = dh_ref[...].astype(BF)
        dhb_ref[...] = dhb
        dmerged = _dot_nt(dhb, wmo_ref[...])
        sc = jax.nn.sigmoid(gc_ref[0].astype(F32))
        sa = jax.nn.sigmoid(ga_ref[0].astype(F32))
        dyc = (dmerged * sc).astype(BF)
        dya = (dmerged * sa).astype(BF)
        dyc_ref[...] = dyc
        dya_ref[...] = dya
        dgc_ref[...] = (dmerged * yc_ref[...].astype(F32) * sc * (1.0 - sc)).astype(BF)
        dga_ref[...] = (dmerged * ya_ref[...].astype(F32) * sa * (1.0 - sa)).astype(BF)
        m, m1, m2 = _conv_inputs(cc_ref, cx_ref, cch_ref, cxh_ref)
        cw = cw_ref[...]
        cv = cw[0:1, :] * m2 + cw[1:2, :] * m1 + cw[2:3, :] * m
        dycin = _dot_nt(dyc, wco_ref[...])
        dcb_ref[...] = (dycin * cv).astype(BF)
        dcv_ref[...] = (dycin * cb_ref[0].astype(F32)).astype(BF)
        do_ref[...] = _dot_nt(dya, wao_ref[...]).astype(BF)

    sq = (D_MODEL, D_MODEL)
    return _pallas(
        body, name="mixer_bwd", grid=(t // tm,), ride=ride,
        in_specs=[_rows(tm, D_MODEL), _piece(0, tm), _piece(1, tm), _piece(2, tm), _piece(6, tm), _piece(7, tm),
                  _prev_halo(1, tm), _prev_halo(2, tm), _rows(tm, D_MODEL), _rows(tm, D_MODEL),
                  _const_spec((3, D_MODEL)), _const_spec(sq), _const_spec(sq), _const_spec(sq)],
        out_specs=[_rows(tm, D_MODEL)] * 8,
        out_shape=[S((t, D_MODEL), BF)] * 8,
        args=[dh2, proj, proj, proj, proj, proj, proj, proj, yc, ya, conv_w, w_co, w_ao, w_mo])


TAP_ROWS = 8


def _conv_bwd(dcv, proj, conv_w):
    t = dcv.shape[0]
    tm = min(TOKEN_TILE, t)
    steps = t // tm

    def body(dcv_ref, nxt_ref, cc_ref, cx_ref, cch_ref, cxh_ref, cw_ref, dcc_ref, dcx_ref, dw_ref):
        i = pl.program_id(0)
        m, m1, m2 = _conv_inputs(cc_ref, cx_ref, cch_ref, cxh_ref)
        d0 = dcv_ref[...].astype(F32)
        nxt = jnp.where(i == steps - 1, 0.0, nxt_ref[...].astype(F32))
        row = lax.broadcasted_iota(jnp.int32, (tm, 1), 0)
        d1 = jnp.where(row == tm - 1, nxt[0:1, :], pltpu.roll(d0, tm - 1, 0))
        d2 = pltpu.roll(d0, tm - 2, 0)
        d2 = jnp.where(row == tm - 2, nxt[0:1, :], jnp.where(row == tm - 1, nxt[1:2, :], d2))
        cw = cw_ref[...]
        dm = cw[2:3, :] * d0 + cw[1:2, :] * d1 + cw[0:1, :] * d2
        dcc_ref[...] = (dm * cx_ref[0].astype(F32)).astype(BF)
        dcx_ref[...] = (dm * cc_ref[0].astype(F32)).astype(BF)
        tap_row = lax.broadcasted_iota(jnp.int32, (TAP_ROWS, 1), 0)
        dw = jnp.zeros((TAP_ROWS, D_MODEL), F32)
        for j, mk in enumerate((m2, m1, m)):
            dw = jnp.where(tap_row == j, jnp.sum(d0 * mk, axis=0, keepdims=True), dw)
        _accumulate(dw_ref, dw)

    nxt_spec = pl.BlockSpec((HALO, D_MODEL), lambda i: (jnp.minimum((i + 1) * (tm // HALO), t // HALO - 1), 0))
    return pl.pallas_call(
        body, name="conv_bwd", grid=(steps,),
        in_specs=[_rows(tm, D_MODEL), nxt_spec, _piece(1, tm), _piece(2, tm), _prev_halo(1, tm), _prev_halo(2, tm),
                  _const_spec((3, D_MODEL))],
        out_specs=[_rows(tm, D_MODEL), _rows(tm, D_MODEL), _acc_spec((TAP_ROWS, D_MODEL))],
        out_shape=[S((t, D_MODEL), BF), S((t, D_MODEL), BF), S((TAP_ROWS, D_MODEL), F32)],
    )(dcv, dcv, proj, proj, proj, proj, conv_w)


def _mix_bwd(dpieces, w_mix, h1, dh2, g, ride=None):
    t = h1.shape[0]
    tm = min(TOKEN_TILE, t)

    def body(*refs):
        pieces, (w_ref, h_ref, dh_ref, g_ref, dhi_ref, dg_ref) = refs[:N_MIX], refs[N_MIX:]
        du = jnp.zeros((tm, D_MODEL), F32)
        for d in range(N_MIX):
            du = du + _dot_nt(pieces[d][...], w_ref[d])
        dhi, dg = _rms_bwd(du, h_ref[...], g_ref[...])
        _accumulate(dg_ref, dg)
        dhi_ref[...] = dh_ref[...] + dhi

    vec = (1, D_MODEL)
    return _pallas(
        body, name="mix_bwd", grid=(t // tm,), ride=ride,
        in_specs=[_rows(tm, D_MODEL)] * N_MIX + [_const_spec(w_mix.shape), _rows(tm, D_MODEL), _rows(tm, D_MODEL), _const_spec(vec)],
        out_specs=[_rows(tm, D_MODEL), _acc_spec(vec)],
        out_shape=[S((t, D_MODEL), F32), S(vec, F32)],
        args=[*dpieces, w_mix, h1, dh2, g])


def _adamw(partials, w, m, v, name):
    parts = list(partials) if isinstance(partials, (list, tuple)) else [partials]
    r, c = w.shape
    tr = next(d for d in (r, 512, 352, 256) if d <= 512 // len(parts) and r % d == 0)
    first_tile = [sum(p.shape[1] for p in parts[:j]) // tr for j in range(len(parts))]
    c1 = 1.0 - ADAM_B1 ** ADAM_STEP
    c2 = 1.0 - ADAM_B2 ** ADAM_STEP

    def body(*refs):
        p_refs, (w_ref, m_ref, v_ref, g_ref, d_ref, mo_ref, vo_ref) = refs[:len(parts)], refs[len(parts):]
        g = None
        for j, p_ref in enumerate(p_refs):
            gj = p_ref[0].astype(F32)
            for s in range(1, N_SHARDS):
                gj = gj + p_ref[s].astype(F32)
            g = gj if g is None else jnp.where(pl.program_id(0) >= first_tile[j], gj, g)
        mn = ADAM_B1 * m_ref[...] + (1.0 - ADAM_B1) * g
        vn = ADAM_B2 * v_ref[...] + (1.0 - ADAM_B2) * (g * g)
        g_ref[...] = g
        mo_ref[...] = mn
        vo_ref[...] = vn
        d_ref[...] = -ADAM_LR * ((mn / c1) / (jnp.sqrt(vn / c2) + ADAM_EPS) + ADAM_WD * w_ref[...])

    def rows_of(j):
        last = parts[j].shape[1] // tr - 1
        return lambda i: (0, jnp.clip(i - first_tile[j], 0, last), 0)

    blk = pl.BlockSpec((tr, c), lambda i: (i, 0))
    return pl.pallas_call(
        body, name=name, grid=(r // tr,),
        in_specs=[pl.BlockSpec((N_SHARDS, tr, c), rows_of(j)) for j in range(len(parts))] + [blk, blk, blk],
        out_specs=[blk] * 4, out_shape=[S((r, c), F32)] * 4,
    )(*parts, w, m, v)


_MATRICES = ("ffn1_w_in", "ffn1_w_out", "w_mix_in", "conv_w", "w_conv_out", "w_attn_out", "w_mix_out",
             "ffn2_w_in", "ffn2_w_out", "w_ple_gate", "w_ple_proj")
_GAINS = ("ffn1_norm", "mix_norm", "ffn2_norm", "ple_norm", "final_norm")
_WEIGHTS = ("ffn1_norm", "ffn1_w_in", "ffn1_w_out", "mix_norm", "w_mix_in", "conv_w", "w_conv_out", "w_attn_out", "w_mix_out",
            "ffn2_norm", "ffn2_w_in", "ffn2_w_out", "ple_norm", "w_ple_gate", "w_ple_proj", "final_norm")
CONV_ROWS = 8
_TRANSPOSED = ("ffn1_w_in", "ffn2_w_in")


def _columns_from_shards(g):
    return jnp.transpose(g, (1, 0, 2)).reshape(g.shape[1], N_SHARDS * g.shape[2])


def _shards_from_columns(a):
    r, c = a.shape
    return jnp.transpose(a.reshape(r, N_SHARDS, c // N_SHARDS), (1, 0, 2))


def kernel(x, p, ffn1_norm, ffn1_w_in, ffn1_w_out, mix_norm, w_mix_in, conv_w, w_conv_out, w_attn_out, w_mix_out, ffn2_norm, ffn2_w_in, ffn2_w_out, ple_norm, w_ple_gate, w_ple_proj, final_norm, loss_target, m_ffn1_norm, m_ffn1_w_in, m_ffn1_w_out, m_mix_norm, m_w_mix_in, m_conv_w, m_w_conv_out, m_w_attn_out, m_w_mix_out, m_ffn2_norm, m_ffn2_w_in, m_ffn2_w_out, m_ple_norm, m_w_ple_gate, m_w_ple_proj, m_final_norm, v_ffn1_norm, v_ffn1_w_in, v_ffn1_w_out, v_mix_norm, v_w_mix_in, v_conv_w, v_w_conv_out, v_w_attn_out, v_w_mix_out, v_ffn2_norm, v_ffn2_w_in, v_ffn2_w_out, v_ple_norm, v_w_ple_gate, v_w_ple_proj, v_final_norm):
    given = dict(locals())
    t = x.shape[1]
    xs = x.reshape(t, D_MODEL)
    ps = p.reshape(t, PLE_DIM)
    target = loss_target.reshape(t, D_MODEL)
    shard = {k: given[k].reshape(given[k].shape[-2:]) for k in _MATRICES}
    gain = {k: given[k].reshape(1, D_MODEL) for k in _GAINS}

    send = {k: shard[k].astype(BF) for k in _MATRICES}
    send["conv_w"] = jnp.pad(shard["conv_w"], ((0, CONV_ROWS - 3), (0, 0)))
    loss_vec, dx, landed, gain_grads = _forward_backward(xs, ps, target, gain, send)
    gain_rows = jnp.concatenate([gain_grads[k] for k in _GAINS] + [loss_vec, jnp.zeros((8 - len(_GAINS) - 1, D_MODEL), F32)], axis=0)
    gain_parts, = _exchange_alone("gather", [gain_rows], "gather_gain_gradients")

    out = {}
    for k in _MATRICES:
        w, m, v = shard[k], given["m_" + k].reshape(shard[k].shape), given["v_" + k].reshape(shard[k].shape)
        part = landed[k]
        if k == "conv_w":
            pad = ((0, CONV_ROWS - 3), (0, 0))
            w, m, v = jnp.pad(w, pad), jnp.pad(m, pad), jnp.pad(v, pad, constant_values=1.0)
        if k in _TRANSPOSED:
            w, m, v = w.T, m.T, v.T
        res = _adamw(part, w, m, v, "adamw_" + k)
        out[k] = [r[:3] if k == "conv_w" else (r.T if k in _TRANSPOSED else r) for r in res]
    stack = lambda pre: jnp.concatenate([given[pre + k].reshape(1, D_MODEL) for k in _GAINS] + [jnp.ones((8 - len(_GAINS), D_MODEL), F32)], axis=0)
    res = _adamw(gain_parts, stack(""), stack("m_"), stack("v_"), "adamw_gains")
    for j, k in enumerate(_GAINS):
        out[k] = [r[j:j + 1] for r in res]

    loss = jnp.sum(gain_parts[:, len(_GAINS), 0])
    per_kind = [[out[k][j].reshape(given[k].shape) for k in _WEIGHTS] for j in range(4)]
    return (loss, dx.reshape(x.shape), *per_kind[0], *per_kind[1], *per_kind[2], *per_kind[3])


def _forward_backward(xs, ps, target, gain, send, full=None):
    exchange = full is None
    full = dict(full or {})
    grads, landed = {}, {}

    def gather(names):
        return ("gather", [send[k] for k in names]) if exchange else None

    def scatter(names):
        return ("scatter", [grads[k] for k in names]) if exchange else None

    def keep(into, names, got):
        into.update(zip(names, got))

    first = ("ffn1_w_in",)
    (n1,), got = _prenorm(xs, gain["ffn1_norm"], ride=gather(first))
    keep(full, first, got)
    w1_in = full["ffn1_w_in"]
    second = ("ffn1_w_out", "w_mix_in")
    (act1, to_gate1, to_up1), got = _ffn_up(n1, w1_in, "ffn1_up", ride=gather(second))
    keep(full, second, got)
    w1_out = full["ffn1_w_out"].reshape(N_FF_CHUNKS, FF_CHUNK, D_MODEL)
    third = ("conv_w", "w_conv_out", "w_attn_out", "w_mix_out")
    (h1, u), got = _ffn_down(xs, act1, w1_out, gain["mix_norm"], "ffn1_down", ride=gather(third))
    keep(full, third, got)
    w_mix = full["w_mix_in"]
    w_co, w_ao, w_mo = (full[k].reshape(D_MODEL, D_MODEL) for k in ("w_conv_out", "w_attn_out", "w_mix_out"))
    taps = _columns_from_shards(full["conv_w"][:, :3, :])
    rest = ("ffn2_w_in", "ffn2_w_out", "w_ple_gate", "w_ple_proj")
    (proj,), got = _mix_proj(u, w_mix, ride=gather(rest))
    keep(full, rest, got)
    w2_in, w2_out = full["ffn2_w_in"], full["ffn2_w_out"].reshape(N_FF_CHUNKS, FF_CHUNK, D_MODEL)
    w_pg = full["w_ple_gate"].reshape(D_MODEL, D_MODEL)
    w_pp = _columns_from_shards(full["w_ple_proj"])
    o, o_bf, a_first, beta_first, reach = _attn_fwd(proj)
    h2, n3, ycin, yc, ya, merged = _mixer_out(proj, o_bf, h1, taps, w_co, w_ao, w_mo, gain["ffn2_norm"])
    (act2, to_gate2, to_up2), _ = _ffn_up(n3, w2_in, "ffn2_up")
    (h3, n4), _ = _ffn_down(h2, act2, w2_out, gain["ple_norm"], "ffn2_down")
    dh3, ds, dpp, loss_vec, dg_final, dg_ple = _tail(h3, n4, ps, w_pg, w_pp, gain["ple_norm"], gain["final_norm"], target)

    one = lambda a: a[None]
    by_rows = lambda g, rows: g.reshape(N_SHARDS, rows // N_SHARDS, D_MODEL)
    square = WGRAD_TILE // 2
    grads["w_ple_gate"] = by_rows(_wgrad(one(n4), one(ds), "wgrad_ple_gate", tile=square), D_MODEL)
    grads["w_ple_proj"] = _shards_from_columns(_wgrad(one(ps), one(dpp), "wgrad_ple_proj")[0])
    ple = ("w_ple_gate", "w_ple_proj")
    (df2, dgate2, dup2), got = _ffn_bwd_hidden(dh3, to_gate2, to_up2, w2_out, "ffn2_bwd_hidden", ride=scatter(ple))
    keep(landed, ple, got)
    grads["ffn2_w_out"] = by_rows(_wgrad(act2, one(df2), "wgrad_ffn2_out"), D_FF)
    grads["ffn2_w_in"] = _wgrad_pieces(n3, [dgate2, dup2], "wgrad_ffn2_in", transposed=True)
    (dh2, dg_ffn2), got = _ffn_bwd_input(dh3, h2, gain["ffn2_norm"], dgate2, dup2, w2_in, "ffn2_bwd_input", ride=scatter(("ffn2_w_out",)))
    keep(landed, ("ffn2_w_out",), got)
    (dh2b, dyc, dya, dgc, dga, dcb, dcv, d_o), _ = _mixer_bwd(dh2, proj, yc, ya, taps, w_co, w_ao, w_mo)
    grads["w_mix_out"] = by_rows(_wgrad(one(merged), one(dh2b), "wgrad_mix_out", tile=square), D_MODEL)
    grads["w_conv_out"] = by_rows(_wgrad(one(ycin), one(dyc), "wgrad_conv_out", tile=square), D_MODEL)
    grads["w_attn_out"] = by_rows(_wgrad(one(o_bf), one(dya), "wgrad_attn_out", tile=square), D_MODEL)
    dcc, dcx, dtaps = _conv_bwd(dcv, proj, taps)
    grads["conv_w"] = jnp.pad(_shards_from_columns(dtaps[:3]), ((0, 0), (0, CONV_ROWS - 3), (0, 0)))
    behind_attn = ("ffn2_w_in", "w_mix_out", "w_conv_out", "w_attn_out", "conv_w")
    (dq, dk, dv), got = _attn_bwd(proj, o, d_o, a_first, beta_first, reach, ride=scatter(behind_attn))
    keep(landed, behind_attn, got)
    dpieces = [dcb, dcc, dcx, dq, dk, dv, dgc, dga]
    half = N_MIX // 2
    tops, bottoms = zip(_wgrad_pieces(u, [one(dp) for dp in dpieces[:half]], "wgrad_mix_in_a", tile=WGRAD_TILE // 2, row_parts=2),
                        _wgrad_pieces(u, [one(dp) for dp in dpieces[half:]], "wgrad_mix_in_b", tile=WGRAD_TILE // 2, row_parts=2))
    grads["w_mix_in top"], grads["w_mix_in bottom"] = jnp.concatenate(tops, axis=0), jnp.concatenate(bottoms, axis=0)
    (dh1, dg_mix), top = _mix_bwd(dpieces, w_mix, h1, dh2, gain["mix_norm"], ride=scatter(("w_mix_in top",)))
    (df1, dgate1, dup1), bottom = _ffn_bwd_hidden(dh1, to_gate1, to_up1, w1_out, "ffn1_bwd_hidden", ride=scatter(("w_mix_in bottom",)))
    if exchange:
        landed["w_mix_in"] = [top[0], bottom[0]]
    else:
        grads["w_mix_in"] = jnp.concatenate([grads.pop("w_mix_in top"), grads.pop("w_mix_in bottom")], axis=1)
    grads["ffn1_w_out"] = by_rows(_wgrad(act1, one(df1), "wgrad_ffn1_out"), D_FF)
    if exchange:
        grads["ffn1_w_in"], got = _wgrad_pieces(n1, [dgate1, dup1], "wgrad_ffn1_in", transposed=True, ride=scatter(("ffn1_w_out",)))
        keep(landed, ("ffn1_w_out",), got)
    else:
        grads["ffn1_w_in"] = _wgrad_pieces(n1, [dgate1, dup1], "wgrad_ffn1_in", transposed=True)
    (dx, dg_ffn1), got = _ffn_bwd_input(dh1, xs, gain["ffn1_norm"], dgate1, dup1, w1_in, "ffn1_bwd_input", ride=scatter(("ffn1_w_in",)))
    keep(landed, ("ffn1_w_in",), got)
    gain_grads = dict(ffn1_norm=dg_ffn1, mix_norm=dg_mix, ffn2_norm=dg_ffn2, ple_norm=dg_ple, final_norm=dg_final)
    return loss_vec, dx, (landed if exchange else grads), gain_grads
```

```python
import functools
import math

import jax
import jax.numpy as jnp
from jax import lax
from jax.experimental import pallas as pl
from jax.experimental.pallas import tpu as pltpu

D_MODEL = 1024
D_FF = 2816
N_SHARDS = 8
FF_CHUNK = 2 * D_FF // N_SHARDS
N_FF_CHUNKS = D_FF // FF_CHUNK
N_HEADS = 8
HEAD_DIM = 128
PLE_DIM = 256
NORM_EPS = 1e-6
N_MIX = 8
ADAM_LR, ADAM_B1, ADAM_B2, ADAM_EPS, ADAM_WD, ADAM_STEP = 0.001, 0.9, 0.999, 1e-08, 0.01, 10

TOKEN_TILE = 512
WGRAD_TILE = 4096
PROJ_TILE = 2048
ATTN_ROWS = 512
ATTN_Q = 64
ATTN_SUB = 128
ATTN_K = 2 * ATTN_SUB
ATTN_SKIP_BELOW = -90.0

BF = jnp.bfloat16
F32 = jnp.float32
MESH = pl.DeviceIdType.MESH
NT = (((1,), (1,)), ((), ()))
TN = (((0,), (0,)), ((), ()))
S = jax.ShapeDtypeStruct
ANY = pl.BlockSpec(memory_space=pl.ANY)


def _const_spec(shape):
    nd = len(shape)
    return pl.BlockSpec(shape, lambda *_: (0,) * nd, pipeline_mode=pl.Buffered(1))


def _rows(tm, cols):
    return pl.BlockSpec((tm, cols), lambda i: (i, 0))


def _chunks(tm):
    return pl.BlockSpec((N_FF_CHUNKS, tm, FF_CHUNK), lambda i: (0, i, 0))


def _acc_spec(shape):
    nd = len(shape)
    return pl.BlockSpec(shape, lambda *_: (0,) * nd)


def _dot(a, b):
    return jnp.dot(a, b, preferred_element_type=F32)


def _dot_nt(a, b):
    return lax.dot_general(a, b, NT, preferred_element_type=F32)


def _dot_tn(a, b):
    return lax.dot_general(a, b, TN, preferred_element_type=F32)


def _rms(h, g):
    r = lax.rsqrt(jnp.mean(h * h, axis=-1, keepdims=True) + NORM_EPS)
    return h * r * g


def _rms_bwd(dn, h, g):
    r = lax.rsqrt(jnp.mean(h * h, axis=-1, keepdims=True) + NORM_EPS)
    nh = h * r
    gd = dn * g
    dh = r * (gd - nh * jnp.mean(gd * nh, axis=-1, keepdims=True))
    return dh, jnp.sum(dn * nh, axis=0, keepdims=True)


def _accumulate(ref, val):
    @pl.when(pl.program_id(0) == 0)
    def _():
        ref[...] = jnp.zeros_like(ref)
    ref[...] += val


def _place():
    x, y, c = lax.axis_index("x"), lax.axis_index("y"), lax.axis_index("c")
    return x, y, c


def _slot(px, py, pc):
    return 4 * px + 2 * py + pc


def _gather_phases(ins, outs, send_sems, recv_sems, local_sems):
    n = len(ins)

    def parties():
        x, y, c = _place()
        return (x, y, c), (x, y, 1 - c), [(1 - x, y), (x, 1 - y), (1 - x, 1 - y)], c

    def copy(a, k, block, to, src=None):
        dst = outs[a].at[_slot(*block)]
        return pltpu.make_async_remote_copy(
            src_ref=dst if src is None else src, dst_ref=dst,
            send_sem=send_sems.at[a, k], recv_sem=recv_sems.at[a, k],
            device_id=to, device_id_type=MESH)

    def own(a, me):
        return pltpu.make_async_copy(ins[a], outs[a].at[_slot(*me)], local_sems.at[a])

    def first(a, me, sibling, chips, c):
        return [copy(a, 0, me, sibling, src=ins[a])] + [copy(a, 1 + j, me, (*chip, c), src=ins[a]) for j, chip in enumerate(chips)]

    def start():
        me, sibling, chips, c = parties()
        for a in range(n):
            own(a, me).start()
        for a in range(n):
            for cp in first(a, me, sibling, chips, c):
                cp.start()

    def forward():
        me, sibling, chips, c = parties()
        for j, chip in enumerate(chips):
            for a in range(n):
                copy(a, 1 + j, (*chip, c), me).wait_recv()
                copy(a, 4 + j, (*chip, c), sibling).start()

    def finish():
        me, sibling, chips, c = parties()
        for a in range(n):
            copy(a, 0, sibling, me).wait_recv()
            for j, chip in enumerate(chips):
                copy(a, 4 + j, (*chip, 1 - c), me).wait_recv()
        for a in range(n):
            for cp in first(a, me, sibling, chips, c) + [copy(a, 4 + j, (*chip, c), sibling) for j, chip in enumerate(chips)]:
                cp.wait_send()
            own(a, me).wait()

    return [start, forward, finish]


def _scatter_phases(ins, outs, send_sems, recv_sems, local_sems):
    n = len(ins)

    def copies():
        x, y, c = _place()
        me = _slot(x, y, c)
        out = [pltpu.make_async_copy(ins[a].at[me], outs[a].at[me], local_sems.at[a]) for a in range(n)]
        for k in range(1, N_SHARDS):
            px = 1 - x if k & 4 else x
            py = 1 - y if k & 2 else y
            pc = 1 - c if k & 1 else c
            for a in range(n):
                out.append(pltpu.make_async_remote_copy(
                    src_ref=ins[a].at[_slot(px, py, pc)], dst_ref=outs[a].at[me],
                    send_sem=send_sems.at[a, k - 1], recv_sem=recv_sems.at[a, k - 1],
                    device_id=(px, py, pc), device_id_type=MESH))
        return out

    def start():
        for cp in copies():
            cp.start()

    def finish():
        for cp in copies():
            cp.wait()

    return [start, finish]


def _pallas(body, *, name, grid, in_specs, out_specs, out_shape, args, scratch_shapes=(), ride=None):
    if ride is None:
        outs = pl.pallas_call(body, name=name, grid=grid, in_specs=in_specs, out_specs=out_specs, out_shape=out_shape,
                              scratch_shapes=list(scratch_shapes))(*args)
        return list(outs), []
    kind, arrays = ride
    n, n_in, n_out, n_scr = len(arrays), len(in_specs), len(out_specs), len(scratch_shapes)
    total = math.prod(grid)
    middle = (9 * total) // 10
    landed_shape = [S((N_SHARDS,) + a.shape if kind == "gather" else a.shape, a.dtype) for a in arrays]

    def with_exchange(*refs):
        ins, riders_in = refs[:n_in], refs[n_in:n_in + n]
        outs, riders_out = refs[n_in + n:n_in + n + n_out], refs[n_in + n + n_out:n_in + 2 * n + n_out]
        scratch, sems = refs[n_in + 2 * n + n_out:n_in + 2 * n + n_out + n_scr], refs[n_in + 2 * n + n_out + n_scr:]
        step = 0
        for axis, size in enumerate(grid):
            step = step * size + pl.program_id(axis)
        phases = (_gather_phases if kind == "gather" else _scatter_phases)(riders_in, riders_out, *sems)
        pl.when(step == 0)(phases[0])
        body(*ins, *outs, *scratch)
        for phase in phases[1:-1]:
            pl.when(step == middle)(phase)
        pl.when(step == total - 1)(phases[-1])

    outs = pl.pallas_call(
        with_exchange, name=name, grid=grid,
        in_specs=list(in_specs) + [ANY] * n, out_specs=list(out_specs) + [ANY] * n,
        out_shape=list(out_shape) + landed_shape,
        scratch_shapes=list(scratch_shapes) + [pltpu.SemaphoreType.DMA((n, 7)), pltpu.SemaphoreType.DMA((n, 7)),
                                               pltpu.SemaphoreType.DMA((n,))],
    )(*args, *arrays)
    return list(outs[:n_out]), list(outs[n_out:])


def _exchange_alone(kind, arrays, name):
    return _pallas(lambda: None, name=name, grid=(1,), in_specs=[], out_specs=[], out_shape=[], args=[], ride=(kind, arrays))[1]


def _prenorm(x, g, ride=None):
    t = x.shape[0]
    tm = min(TOKEN_TILE, t)

    def body(x_ref, g_ref, n_ref):
        n_ref[...] = _rms(x_ref[...], g_ref[...]).astype(BF)

    return _pallas(
        body, name="prenorm", grid=(t // tm,), ride=ride,
        in_specs=[_rows(tm, D_MODEL), _const_spec((1, D_MODEL))], out_specs=[_rows(tm, D_MODEL)],
        out_shape=[S((t, D_MODEL), BF)], args=[x, g])


def _ffn_up(n, w_in, name, ride=None):
    t = n.shape[0]
    tm = min(TOKEN_TILE, t)

    def body(n_ref, win_ref, act_ref, to_gate_ref, to_up_ref):
        nb = n_ref[...]
        for c in range(N_FF_CHUNKS):
            gate = _dot(nb, win_ref[c])
            up = _dot(nb, win_ref[N_FF_CHUNKS + c])
            sg = jax.nn.sigmoid(gate)
            silu = gate * sg
            act_ref[c] = (silu * up).astype(BF)
            to_gate_ref[c] = (up * (sg * (1.0 + gate * (1.0 - sg)))).astype(BF)
            to_up_ref[c] = silu.astype(BF)

    return _pallas(
        body, name=name, grid=(t // tm,), ride=ride,
        in_specs=[_rows(tm, D_MODEL), _const_spec(w_in.shape)],
        out_specs=[_chunks(tm)] * 3, out_shape=[S((N_FF_CHUNKS, t, FF_CHUNK), BF)] * 3,
        args=[n, w_in])


def _ffn_down(h, act, w_out, g_next, name, ride=None):
    t = h.shape[0]
    tm = min(TOKEN_TILE, t)

    def body(h_ref, act_ref, wout_ref, g_ref, ho_ref, no_ref):
        acc = jnp.zeros((tm, D_MODEL), F32)
        for c in range(N_FF_CHUNKS):
            acc = acc + _dot(act_ref[c], wout_ref[c])
        ho = h_ref[...] + 0.5 * acc
        ho_ref[...] = ho
        no_ref[...] = _rms(ho, g_ref[...]).astype(BF)

    return _pallas(
        body, name=name, grid=(t // tm,), ride=ride,
        in_specs=[_rows(tm, D_MODEL), _chunks(tm), _const_spec(w_out.shape), _const_spec((1, D_MODEL))],
        out_specs=[_rows(tm, D_MODEL)] * 2, out_shape=[S((t, D_MODEL), F32), S((t, D_MODEL), BF)],
        args=[h, act, w_out, g_next])


def _mix_proj(u, w_mix, ride=None):
    t = u.shape[0]
    tm = min(PROJ_TILE, t)

    def body(u_ref, w_ref, o_ref):
        o_ref[0] = _dot(u_ref[...], w_ref[0]).astype(BF)

    return _pallas(
        body, name="mix_proj", grid=(N_MIX, t // tm), ride=ride,
        in_specs=[pl.BlockSpec((tm, D_MODEL), lambda d, i: (i, 0)), pl.BlockSpec((1, D_MODEL, D_MODEL), lambda d, i: (d, 0, 0))],
        out_specs=[pl.BlockSpec((1, tm, D_MODEL), lambda d, i: (d, i, 0))],
        out_shape=[S((N_MIX, t, D_MODEL), BF)], args=[u, w_mix])


HALO = 16


def _piece(d, tm):
    return pl.BlockSpec((1, tm, D_MODEL), lambda i: (d, i, 0))


def _prev_halo(d, tm):
    return pl.BlockSpec((1, HALO, D_MODEL), lambda i: (d, jnp.maximum(i * (tm // HALO) - 1, 0), 0))


def _shift_down(m, prev_tail, k):
    tm = m.shape[0]
    out = pltpu.roll(m, k, 0)
    row = lax.broadcasted_iota(jnp.int32, (tm, 1), 0)
    for j in range(k):
        out = jnp.where(row == j, prev_tail[HALO - k + j:HALO - k + j + 1, :], out)
    return out


def _conv_inputs(cc_ref, cx_ref, cch_ref, cxh_ref):
    m = cc_ref[0].astype(F32) * cx_ref[0].astype(F32)
    mh = cch_ref[0].astype(F32) * cxh_ref[0].astype(F32)
    mh = jnp.where(pl.program_id(0) == 0, 0.0, mh)
    return m, _shift_down(m, mh, 1), _shift_down(m, mh, 2)


def _mixer_out(proj, o, h1, conv_w, w_co, w_ao, w_mo, g_next):
    t = h1.shape[0]
    tm = min(TOKEN_TILE, t)

    def body(cb_ref, cc_ref, cx_ref, gc_ref, ga_ref, cch_ref, cxh_ref, o_ref, h_ref, cw_ref, wco_ref, wao_ref, wmo_ref,
             g_ref, ho_ref, no_ref, ycin_ref, yc_ref, ya_ref, mg_ref):
        m, m1, m2 = _conv_inputs(cc_ref, cx_ref, cch_ref, cxh_ref)
        cw = cw_ref[...]
        cv = cw[0:1, :] * m2 + cw[1:2, :] * m1 + cw[2:3, :] * m
        ycin = (cb_ref[0].astype(F32) * cv).astype(BF)
        ycin_ref[...] = ycin
        yc = _dot(ycin, wco_ref[...])
        ya = _dot(o_ref[...], wao_ref[...])
        yc_ref[...] = yc.astype(BF)
        ya_ref[...] = ya.astype(BF)
        merged = (jax.nn.sigmoid(gc_ref[0].astype(F32)) * yc + jax.nn.sigmoid(ga_ref[0].astype(F32)) * ya).astype(BF)
        mg_ref[...] = merged
        ho = h_ref[...] + _dot(merged, wmo_ref[...])
        ho_ref[...] = ho
        no_ref[...] = _rms(ho, g_ref[...]).astype(BF)

    sq = (D_MODEL, D_MODEL)
    return pl.pallas_call(
        body, name="mixer_out", grid=(t // tm,),
        in_specs=[_piece(0, tm), _piece(1, tm), _piece(2, tm), _piece(6, tm), _piece(7, tm), _prev_halo(1, tm), _prev_halo(2, tm),
                  _rows(tm, D_MODEL), _rows(tm, D_MODEL), _const_spec((3, D_MODEL)), _const_spec(sq), _const_spec(sq),
                  _const_spec(sq), _const_spec((1, D_MODEL))],
        out_specs=[_rows(tm, D_MODEL)] * 6,
        out_shape=[S((t, D_MODEL), F32)] + [S((t, D_MODEL), BF)] * 5,
    )(proj, proj, proj, proj, proj, proj, proj, o, h1, conv_w, w_co, w_ao, w_mo, g_next)


def _suffix_sums(vals, tri, before):
    out, right = [], before
    for b in reversed(range(ATTN_K // ATTN_SUB)):
        v = vals[:, b * ATTN_SUB:(b + 1) * ATTN_SUB]
        hi = v.astype(BF)
        lo = (v - hi.astype(F32)).astype(BF)
        out.append(_dot(hi, tri) + _dot(lo, tri) + right)
        right = right + jnp.sum(v, axis=1, keepdims=True)
    return jnp.concatenate(out[::-1], axis=1), right


ATTN_UNITS = ATTN_ROWS // ATTN_Q


def _unit_rows(x, u):
    return x[u * ATTN_Q:(u + 1) * ATTN_Q]


def _per_unit(fn):
    return jnp.concatenate([fn(u) for u in range(ATTN_UNITS)], axis=0)


def _per_row(vals):
    local = lax.broadcasted_iota(jnp.int32, (ATTN_ROWS, 1), 0)
    out = jnp.full((ATTN_ROWS, 1), vals[0], jnp.int32)
    for u in range(1, ATTN_UNITS):
        out = jnp.where(local >= u * ATTN_Q, vals[u], out)
    return out


def _attn_step(q, k_ref, starts, bounds, row):
    z = _per_unit(lambda u: _dot_nt(_unit_rows(q, u), k_ref[0, pl.ds(starts[u], ATTN_K), :])) * (1.0 / math.sqrt(HEAD_DIM))
    mask = lax.broadcasted_iota(jnp.int32, (1, ATTN_K), 1) < jnp.minimum(row, _per_row(bounds)) - _per_row(starts)
    log_beta = jnp.minimum(z, 0.0) - jnp.log(1.0 + jnp.exp(jnp.minimum(z, -z)))
    log_rest = jnp.where(mask, log_beta - z, 0.0)
    return z, mask, log_beta, log_rest


def _attn_sweep_start(i, t):
    blks = tuple(jnp.maximum(i * ATTN_UNITS + u + 1 - ATTN_K // ATTN_Q, 0) for u in range(ATTN_UNITS))
    return blks, tuple(jnp.int32(t) for _ in range(ATTN_UNITS))


def _attn_keys(blks):
    return [pl.multiple_of(b * ATTN_Q, ATTN_Q) for b in blks]


def _attn_next(blks):
    return tuple(jnp.maximum(b - ATTN_K // ATTN_Q, 0) for b in blks), tuple(b * ATTN_Q for b in blks)


def _attn_more(carry):
    return jnp.logical_and(carry[1][ATTN_UNITS - 1] > 0, carry[-1] > ATTN_SKIP_BELOW)


def _tri(strict):
    r = lax.broadcasted_iota(jnp.int32, (ATTN_SUB, ATTN_SUB), 0)
    c = lax.broadcasted_iota(jnp.int32, (ATTN_SUB, ATTN_SUB), 1)
    return (r > c if strict else r >= c).astype(BF)


REACH_TILE = (8, 128)


def _first_step_spec():
    return pl.BlockSpec((1, ATTN_ROWS, ATTN_K), lambda h, i: (h, i, 0))


def _reach_spec():
    return pl.BlockSpec((1, 1) + REACH_TILE, lambda h, i: (h, i, 0, 0))


def _head_cols(piece):
    return lambda t: pl.BlockSpec((1, t, HEAD_DIM), lambda h, i: (piece, 0, h))


def _attn_fwd(proj, ride=None):
    t = proj.shape[1]
    nq = t // ATTN_ROWS
    tri = _tri(strict=True)

    def body(q_ref, k_ref, v_ref, tri_ref, o_ref, ob_ref, a_ref, beta_ref, reach_ref):
        i = pl.program_id(1)
        q = q_ref[0]
        row = i * ATTN_ROWS + lax.broadcasted_iota(jnp.int32, (ATTN_ROWS, 1), 0)

        def step(carry, keep=False):
            blks, bounds, acc, run, _ = carry
            starts = _attn_keys(blks)
            _, mask, log_beta, log_rest = _attn_step(q, k_ref, starts, bounds, row)
            tail, run = _suffix_sums(log_rest, tri_ref[...], run)
            a = jnp.where(mask, jnp.exp(log_beta + tail), 0.0).astype(BF)
            if keep:
                a_ref[0] = a
                beta_ref[0] = jnp.where(mask, jnp.exp(log_beta), 0.0).astype(BF)
            acc = acc + _per_unit(lambda u: _dot(_unit_rows(a, u), v_ref[0, pl.ds(starts[u], ATTN_K), :]))
            return (*_attn_next(blks), acc, run, jnp.max(run))

        first = (*_attn_sweep_start(i, t), jnp.zeros((ATTN_ROWS, HEAD_DIM), F32), jnp.zeros((ATTN_ROWS, 1), F32), jnp.float32(0.0))
        after_first = step(first, keep=True)
        reach_ref[...] = jnp.full(reach_ref.shape, after_first[-1], F32)
        o = lax.while_loop(_attn_more, step, after_first)[2]
        o_ref[...] = o
        ob_ref[...] = o.astype(BF)

    qspec = pl.BlockSpec((1, ATTN_ROWS, HEAD_DIM), lambda h, i: (3, i, h))
    rowblk = pl.BlockSpec((ATTN_ROWS, HEAD_DIM), lambda h, i: (i, h))
    return _pallas(
        body, name="attn_fwd", grid=(N_HEADS, nq), ride=ride,
        in_specs=[qspec, _head_cols(4)(t), _head_cols(5)(t), pl.BlockSpec((ATTN_SUB, ATTN_SUB), lambda h, i: (0, 0))],
        out_specs=[rowblk, rowblk, _first_step_spec(), _first_step_spec(), _reach_spec()],
        out_shape=[S((t, D_MODEL), F32), S((t, D_MODEL), BF), S((N_HEADS, t, ATTN_K), BF), S((N_HEADS, t, ATTN_K), BF),
                   S((N_HEADS, nq) + REACH_TILE, F32)],
        args=[proj, proj, proj, tri])


def _attn_bwd(proj, o, d_o, a_first, beta_first, reach, ride=None):
    t = proj.shape[1]
    nq = t // ATTN_ROWS
    tri_strict, tri_incl = _tri(strict=True), _tri(strict=False)
    scale = 1.0 / math.sqrt(HEAD_DIM)

    def body(q_ref, k_ref, v_ref, o_ref, do_ref, a_ref, beta_ref, reach_ref, tris_ref, trii_ref, dq_ref, dk_ref, dv_ref, dk_acc, dv_acc):
        i = pl.program_id(1)

        @pl.when(i == 0)
        def _():
            dk_acc[...] = jnp.zeros_like(dk_acc)
            dv_acc[...] = jnp.zeros_like(dv_acc)

        q = q_ref[0]
        do = do_ref[...]
        total = jnp.sum(do.astype(F32) * o_ref[...], axis=1, keepdims=True)
        zero = jnp.zeros((ATTN_ROWS, 1), F32)
        blks0, bounds0 = _attn_sweep_start(i, t)

        def finish(starts, a, dz, dq):
            dzb = (dz * scale).astype(BF)
            for u in range(ATTN_UNITS):
                dv_acc[pl.ds(starts[u], ATTN_K), :] += _dot_tn(_unit_rows(a, u), _unit_rows(do, u))
                dk_acc[pl.ds(starts[u], ATTN_K), :] += _dot_tn(_unit_rows(dzb, u), _unit_rows(q, u))
            return dq + _per_unit(lambda u: _dot(_unit_rows(dzb, u), k_ref[0, pl.ds(starts[u], ATTN_K), :]))

        def grad_a(starts, a):
            return _per_unit(lambda u: _dot_nt(_unit_rows(do, u), v_ref[0, pl.ds(starts[u], ATTN_K), :])) * a.astype(F32)

        one_step = jnp.max(reach_ref[...]) <= ATTN_SKIP_BELOW

        @pl.when(one_step)
        def _():
            starts = _attn_keys(blks0)
            a = a_ref[0]
            beta = beta_ref[0].astype(F32)
            de = grad_a(starts, a)
            right, _ = _suffix_sums(de, trii_ref[...], zero)
            dz = de * (1.0 - beta) - (total - right) * beta
            dq_ref[...] = finish(starts, a, dz, jnp.zeros((ATTN_ROWS, HEAD_DIM), F32)).astype(BF)

        @pl.when(jnp.logical_not(one_step))
        def _():
            row = i * ATTN_ROWS + lax.broadcasted_iota(jnp.int32, (ATTN_ROWS, 1), 0)

            def step(carry):
                blks, bounds, dq, seen, run, _ = carry
                starts = _attn_keys(blks)
                z, mask, log_beta, log_rest = _attn_step(q, k_ref, starts, bounds, row)
                tail, run = _suffix_sums(log_rest, tris_ref[...], run)
                a = jnp.where(mask, jnp.exp(log_beta + tail), 0.0).astype(BF)
                de = grad_a(starts, a)
                right, seen = _suffix_sums(de, trii_ref[...], seen)
                beta = jax.nn.sigmoid(z)
                dz = jnp.where(mask, de * (1.0 - beta) - (total - right) * beta, 0.0)
                return (*_attn_next(blks), finish(starts, a, dz, dq), seen, run, jnp.max(run))

            first = (blks0, bounds0, jnp.zeros((ATTN_ROWS, HEAD_DIM), F32), zero, zero, jnp.float32(0.0))
            dq_ref[...] = lax.while_loop(_attn_more, step, step(first))[2].astype(BF)

        @pl.when(i == nq - 1)
        def _():
            dk_ref[...] = dk_acc[...].astype(BF)
            dv_ref[...] = dv_acc[...].astype(BF)

    qspec = pl.BlockSpec((1, ATTN_ROWS, HEAD_DIM), lambda h, i: (3, i, h))
    rowblk = pl.BlockSpec((ATTN_ROWS, HEAD_DIM), lambda h, i: (i, h))
    head = pl.BlockSpec((t, HEAD_DIM), lambda h, i: (0, h))
    trispec = pl.BlockSpec((ATTN_SUB, ATTN_SUB), lambda h, i: (0, 0))
    return _pallas(
        body, name="attn_bwd", grid=(N_HEADS, nq), ride=ride,
        in_specs=[qspec, _head_cols(4)(t), _head_cols(5)(t), rowblk, rowblk, _first_step_spec(), _first_step_spec(), _reach_spec(),
                  trispec, trispec],
        out_specs=[rowblk, head, head],
        out_shape=[S((t, D_MODEL), BF)] * 3,
        scratch_shapes=[pltpu.VMEM((t, HEAD_DIM), F32), pltpu.VMEM((t, HEAD_DIM), F32)],
        args=[proj, proj, proj, o, d_o, a_first, beta_first, reach, tri_strict, tri_incl])


def _tail(h3, n4, p, w_pg, w_pp, g_ple, g_final, target):
    t = h3.shape[0]
    tm = min(TOKEN_TILE, t)
    steps = t // tm

    def body(h_ref, n_ref, p_ref, wpg_ref, wpp_ref, gp_ref, gf_ref, tgt_ref,
             dh_ref, ds_ref, dpp_ref, loss_ref, dgf_ref, dgp_ref):
        pg = jax.nn.sigmoid(_dot(n_ref[...], wpg_ref[...]))
        pp = _dot(p_ref[...].astype(BF), wpp_ref[...])
        h3v = h_ref[...]
        h4 = h3v + pg * pp
        gf = gf_ref[...]
        diff = _rms(h4, gf) - tgt_ref[...]
        _accumulate(loss_ref, jnp.sum(diff * diff, axis=0, keepdims=True))
        dh4, dgf = _rms_bwd(diff * (1.0 / D_MODEL), h4, gf)
        _accumulate(dgf_ref, dgf)
        dpp_ref[...] = (dh4 * pg).astype(BF)
        ds = (dh4 * pp * pg * (1.0 - pg)).astype(BF)
        ds_ref[...] = ds
        dh3, dgp = _rms_bwd(_dot_nt(ds, wpg_ref[...]), h3v, gp_ref[...])
        _accumulate(dgp_ref, dgp)
        dh_ref[...] = dh4 + dh3

        @pl.when(pl.program_id(0) == steps - 1)
        def _():
            loss_ref[...] = jnp.full(loss_ref.shape, 0.5 / D_MODEL * jnp.sum(loss_ref[...]), F32)

    vec = (1, D_MODEL)
    return pl.pallas_call(
        body, name="tail", grid=(steps,),
        in_specs=[_rows(tm, D_MODEL), _rows(tm, D_MODEL), _rows(tm, PLE_DIM), _const_spec((D_MODEL, D_MODEL)),
                  _const_spec((PLE_DIM, D_MODEL)), _const_spec(vec), _const_spec(vec), _rows(tm, D_MODEL)],
        out_specs=[_rows(tm, D_MODEL)] * 3 + [_acc_spec(vec)] * 3,
        out_shape=[S((t, D_MODEL), F32), S((t, D_MODEL), BF), S((t, D_MODEL), BF)] + [S(vec, F32)] * 3,
    )(h3, n4, p, w_pg, w_pp, g_ple, g_final, target)


def _wgrad(xs, ys, name, ride=None, tile=None):
    bx, t, k = xs.shape
    by, _, n = ys.shape
    b = max(bx, by)
    tt = min(tile or WGRAD_TILE * 2 // xs.dtype.itemsize, t)
    steps = t // tt

    def body(x_ref, y_ref, o_ref, acc_ref):
        s = pl.program_id(1)

        @pl.when(s == 0)
        def _():
            acc_ref[...] = jnp.zeros_like(acc_ref)
        acc_ref[...] += _dot_tn(x_ref[0].astype(BF), y_ref[0].astype(BF))

        @pl.when(s == steps - 1)
        def _():
            o_ref[0] = acc_ref[...].astype(BF)

    (out,), landed = _pallas(
        body, name=name, grid=(b, steps), ride=ride,
        in_specs=[pl.BlockSpec((1, tt, k), (lambda j, s: (j, s, 0)) if bx > 1 else (lambda j, s: (0, s, 0))),
                  pl.BlockSpec((1, tt, n), (lambda j, s: (j, s, 0)) if by > 1 else (lambda j, s: (0, s, 0)))],
        out_specs=[pl.BlockSpec((1, k, n), lambda j, s: (j, 0, 0))],
        out_shape=[S((b, k, n), BF)],
        scratch_shapes=[pltpu.VMEM((k, n), F32)],
        args=[xs, ys])
    return (out, landed) if ride is not None else out


def _wgrad_pieces(x, ys, name, ride=None, tile=None, row_parts=1, transposed=False):
    t, k = x.shape
    n = ys[0].shape[2]
    counts = [y.shape[0] for y in ys]
    offsets = [sum(counts[:j]) for j in range(len(ys))]
    total = sum(counts)
    tt = min(tile or WGRAD_TILE, t)
    steps = t // tt
    rows, cols = (n, k) if transposed else (k, n)
    kp = rows // row_parts

    def body(x_ref, *refs):
        y_refs, o_refs, acc_ref = refs[:len(ys)], refs[len(ys):len(ys) + row_parts], refs[len(ys) + row_parts]
        p, s = pl.program_id(0), pl.program_id(1)

        @pl.when(s == 0)
        def _():
            acc_ref[...] = jnp.zeros_like(acc_ref)
        for j, y_ref in enumerate(y_refs):
            @pl.when(jnp.logical_and(p >= offsets[j], p < offsets[j] + counts[j]))
            def _(y_ref=y_ref):
                acc_ref[...] += _dot_tn(y_ref[0], x_ref[...]) if transposed else _dot_tn(x_ref[...], y_ref[0])

        @pl.when(s == steps - 1)
        def _():
            for part, o_ref in enumerate(o_refs):
                o_ref[0] = acc_ref[part * kp:(part + 1) * kp, :].astype(BF)

    def turn(j):
        lo, hi = offsets[j], offsets[j] + counts[j]
        return lambda p, s: (jnp.clip(p - lo, 0, counts[j] - 1), jnp.where(p < lo, 0, jnp.where(p >= hi, steps - 1, s)), 0)

    outs, landed = _pallas(
        body, name=name, grid=(total, steps), ride=ride,
        in_specs=[pl.BlockSpec((tt, k), lambda p, s: (s, 0))] + [pl.BlockSpec((1, tt, n), turn(j)) for j in range(len(ys))],
        out_specs=[pl.BlockSpec((1, kp, cols), lambda p, s: (p, 0, 0))] * row_parts,
        out_shape=[S((total, kp, cols), BF)] * row_parts,
        scratch_shapes=[pltpu.VMEM((rows, cols), F32)],
        args=[x, *ys])
    out = outs[0] if row_parts == 1 else outs
    return (out, landed) if ride is not None else out


def _ffn_bwd_hidden(dh, to_gate, to_up, w_out, name, ride=None):
    t = dh.shape[0]
    tm = min(TOKEN_TILE, t)

    def body(dh_ref, to_gate_ref, to_up_ref, wout_ref, df_ref, dgate_ref, dup_ref):
        df = (0.5 * dh_ref[...]).astype(BF)
        df_ref[...] = df
        for c in range(N_FF_CHUNKS):
            dact = _dot_nt(df, wout_ref[c])
            dgate_ref[c] = (dact * to_gate_ref[c].astype(F32)).astype(BF)
            dup_ref[c] = (dact * to_up_ref[c].astype(F32)).astype(BF)

    return _pallas(
        body, name=name, grid=(t // tm,), ride=ride,
        in_specs=[_rows(tm, D_MODEL), _chunks(tm), _chunks(tm), _const_spec(w_out.shape)],
        out_specs=[_rows(tm, D_MODEL), _chunks(tm), _chunks(tm)],
        out_shape=[S((t, D_MODEL), BF)] + [S((N_FF_CHUNKS, t, FF_CHUNK), BF)] * 2,
        args=[dh, to_gate, to_up, w_out])


def _ffn_bwd_input(dh, h_in, g, dgate, dup, w_in, name, ride=None):
    t = dh.shape[0]
    tm = min(TOKEN_TILE, t)

    def body(dh_ref, h_ref, g_ref, dgate_ref, dup_ref, win_ref, dhi_ref, dg_ref):
        dn = jnp.zeros((tm, D_MODEL), F32)
        for c in range(N_FF_CHUNKS):
            dn = dn + _dot_nt(dgate_ref[c], win_ref[c]) + _dot_nt(dup_ref[c], win_ref[N_FF_CHUNKS + c])
        dhi, dg = _rms_bwd(dn, h_ref[...], g_ref[...])
        _accumulate(dg_ref, dg)
        dhi_ref[...] = dh_ref[...] + dhi

    vec = (1, D_MODEL)
    return _pallas(
        body, name=name, grid=(t // tm,), ride=ride,
        in_specs=[_rows(tm, D_MODEL), _rows(tm, D_MODEL), _const_spec(vec), _chunks(tm), _chunks(tm), _const_spec(w_in.shape)],
        out_specs=[_rows(tm, D_MODEL), _acc_spec(vec)],
        out_shape=[S((t, D_MODEL), F32), S(vec, F32)],
        args=[dh, h_in, g, dgate, dup, w_in])


def _mixer_bwd(dh2, proj, yc, ya, conv_w, w_co, w_ao, w_mo, ride=None):
    t = dh2.shape[0]
    tm = min(TOKEN_TILE, t)

    def body(dh_ref, cb_ref, cc_ref, cx_ref, gc_ref, ga_ref, cch_ref, cxh_ref, yc_ref, ya_ref, cw_ref, wco_ref, wao_ref, wmo_ref,
             dhb_ref, dyc_ref, dya_ref, dgc_ref, dga_ref, dcb_ref, dcv_ref, do_ref):
        dhb = dh_ref[...].astype(BF)
        dhb_ref[...] = dhb
        dmerged = _dot_nt(dhb, wmo_ref[...])
        sc = jax.nn.sigmoid(gc_ref[0].astype(F32))
        sa = jax.nn.sigmoid(ga_ref[0].astype(F32))
        dyc = (dmerged * sc).astype(BF)
        dya = (dmerged * sa).astype(BF)
        dyc_ref[...] = dyc
        dya_ref[...] = dya
        dgc_ref[...] = (dmerged * yc_ref[...].astype(F32) * sc * (1.0 - sc)).astype(BF)
        dga_ref[...] = (dmerged * ya_ref[...].astype(F32) * sa * (1.0 - sa)).astype(BF)
        m, m1, m2 = _conv_inputs(cc_ref, cx_ref, cch_ref, cxh_ref)
        cw = cw_ref[...]
        cv = cw[0:1, :] * m2 + cw[1:2, :] * m1 + cw[2:3, :] * m
        dycin = _dot_nt(dyc, wco_ref[...])
        dcb_ref[...] = (dycin * cv).astype(BF)
        dcv_ref[...] = (dycin * cb_ref[0].astype(F32)).astype(BF)
        do_ref[...] = _dot_nt(dya, wao_ref[...]).astype(BF)

    sq = (D_MODEL, D_MODEL)
    return _pallas(
        body, name="mixer_bwd", grid=(t // tm,), ride=ride,
        in_specs=[_rows(tm, D_MODEL), _piece(0, tm), _piece(1, tm), _piece(2, tm), _piece(6, tm), _piece(7, tm),
                  _prev_halo(1, tm), _prev_halo(2, tm), _rows(tm, D_MODEL), _rows(tm, D_MODEL),
                  _const_spec((3, D_MODEL)), _const_spec(sq), _const_spec(sq), _const_spec(sq)],
        out_specs=[_rows(tm, D_MODEL)] * 8,
        out_shape=[S((t, D_MODEL), BF)] * 8,
        args=[dh2, proj, proj, proj, proj, proj, proj, proj, yc, ya, conv_w, w_co, w_ao, w_mo])


TAP_ROWS = 8


def _conv_bwd(dcv, proj, conv_w):
    t = dcv.shape[0]
    tm = min(TOKEN_TILE, t)
    steps = t // tm

    def body(dcv_ref, nxt_ref, cc_ref, cx_ref, cch_ref, cxh_ref, cw_ref, dcc_ref, dcx_ref, dw_ref):
        i = pl.program_id(0)
        m, m1, m2 = _conv_inputs(cc_ref, cx_ref, cch_ref, cxh_ref)
        d0 = dcv_ref[...].astype(F32)
        nxt = jnp.where(i == steps - 1, 0.0, nxt_ref[...].astype(F32))
        row = lax.broadcasted_iota(jnp.int32, (tm, 1), 0)
        d1 = jnp.where(row == tm - 1, nxt[0:1, :], pltpu.roll(d0, tm - 1, 0))
        d2 = pltpu.roll(d0, tm - 2, 0)
        d2 = jnp.where(row == tm - 2, nxt[0:1, :], jnp.where(row == tm - 1, nxt[1:2, :], d2))
        cw = cw_ref[...]
        dm = cw[2:3, :] * d0 + cw[1:2, :] * d1 + cw[0:1, :] * d2
        dcc_ref[...] = (dm * cx_ref[0].astype(F32)).astype(BF)
        dcx_ref[...] = (dm * cc_ref[0].astype(F32)).astype(BF)
        tap_row = lax.broadcasted_iota(jnp.int32, (TAP_ROWS, 1), 0)
        dw = jnp.zeros((TAP_ROWS, D_MODEL), F32)
        for j, mk in enumerate((m2, m1, m)):
            dw = jnp.where(tap_row == j, jnp.sum(d0 * mk, axis=0, keepdims=True), dw)
        _accumulate(dw_ref, dw)

    nxt_spec = pl.BlockSpec((HALO, D_MODEL), lambda i: (jnp.minimum((i + 1) * (tm // HALO), t // HALO - 1), 0))
    return pl.pallas_call(
        body, name="conv_bwd", grid=(steps,),
        in_specs=[_rows(tm, D_MODEL), nxt_spec, _piece(1, tm), _piece(2, tm), _prev_halo(1, tm), _prev_halo(2, tm),
                  _const_spec((3, D_MODEL))],
        out_specs=[_rows(tm, D_MODEL), _rows(tm, D_MODEL), _acc_spec((TAP_ROWS, D_MODEL))],
        out_shape=[S((t, D_MODEL), BF), S((t, D_MODEL), BF), S((TAP_ROWS, D_MODEL), F32)],
    )(dcv, dcv, proj, proj, proj, proj, conv_w)


def _mix_bwd(dpieces, w_mix, h1, dh2, g, ride=None):
    t = h1.shape[0]
    tm = min(TOKEN_TILE, t)

    def body(*refs):
        pieces, (w_ref, h_ref, dh_ref, g_ref, dhi_ref, dg_ref) = refs[:N_MIX], refs[N_MIX:]
        du = jnp.zeros((tm, D_MODEL), F32)
        for d in range(N_MIX):
            du = du + _dot_nt(pieces[d][...], w_ref[d])
        dhi, dg = _rms_bwd(du, h_ref[...], g_ref[...])
        _accumulate(dg_ref, dg)
        dhi_ref[...] = dh_ref[...] + dhi

    vec = (1, D_MODEL)
    return _pallas(
        body, name="mix_bwd", grid=(t // tm,), ride=ride,
        in_specs=[_rows(tm, D_MODEL)] * N_MIX + [_const_spec(w_mix.shape), _rows(tm, D_MODEL), _rows(tm, D_MODEL), _const_spec(vec)],
        out_specs=[_rows(tm, D_MODEL), _acc_spec(vec)],
        out_shape=[S((t, D_MODEL), F32), S(vec, F32)],
        args=[*dpieces, w_mix, h1, dh2, g])


def _adamw(partials, w, m, v, name):
    parts = list(partials) if isinstance(partials, (list, tuple)) else [partials]
    r, c = w.shape
    tr = next(d for d in (r, 512, 352, 256) if d <= 512 // len(parts) and r % d == 0)
    first_tile = [sum(p.shape[1] for p in parts[:j]) // tr for j in range(len(parts))]
    c1 = 1.0 - ADAM_B1 ** ADAM_STEP
    c2 = 1.0 - ADAM_B2 ** ADAM_STEP

    def body(*refs):
        p_refs, (w_ref, m_ref, v_ref, g_ref, d_ref, mo_ref, vo_ref) = refs[:len(parts)], refs[len(parts):]
        g = None
        for j, p_ref in enumerate(p_refs):
            gj = p_ref[0].astype(F32)
            for s in range(1, N_SHARDS):
                gj = gj + p_ref[s].astype(F32)
            g = gj if g is None else jnp.where(pl.program_id(0) >= first_tile[j], gj, g)
        mn = ADAM_B1 * m_ref[...] + (1.0 - ADAM_B1) * g
        vn = ADAM_B2 * v_ref[...] + (1.0 - ADAM_B2) * (g * g)
        g_ref[...] = g
        mo_ref[...] = mn
        vo_ref[...] = vn
        d_ref[...] = -ADAM_LR * ((mn / c1) / (jnp.sqrt(vn / c2) + ADAM_EPS) + ADAM_WD * w_ref[...])

    def rows_of(j):
        last = parts[j].shape[1] // tr - 1
        return lambda i: (0, jnp.clip(i - first_tile[j], 0, last), 0)

    blk = pl.BlockSpec((tr, c), lambda i: (i, 0))
    return pl.pallas_call(
        body, name=name, grid=(r // tr,),
        in_specs=[pl.BlockSpec((N_SHARDS, tr, c), rows_of(j)) for j in range(len(parts))] + [blk, blk, blk],
        out_specs=[blk] * 4, out_shape=[S((r, c), F32)] * 4,
    )(*parts, w, m, v)


_MATRICES = ("ffn1_w_in", "ffn1_w_out", "w_mix_in", "conv_w", "w_conv_out", "w_attn_out", "w_mix_out",
             "ffn2_w_in", "ffn2_w_out", "w_ple_gate", "w_ple_proj")
_GAINS = ("ffn1_norm", "mix_norm", "ffn2_norm", "ple_norm", "final_norm")
_WEIGHTS = ("ffn1_norm", "ffn1_w_in", "ffn1_w_out", "mix_norm", "w_mix_in", "conv_w", "w_conv_out", "w_attn_out", "w_mix_out",
            "ffn2_norm", "ffn2_w_in", "ffn2_w_out", "ple_norm", "w_ple_gate", "w_ple_proj", "final_norm")
CONV_ROWS = 8
_TRANSPOSED = ("ffn1_w_in", "ffn2_w_in")


def _columns_from_shards(g):
    return jnp.transpose(g, (1, 0, 2)).reshape(g.shape[1], N_SHARDS * g.shape[2])


def _shards_from_columns(a):
    r, c = a.shape
    return jnp.transpose(a.reshape(r, N_SHARDS, c // N_SHARDS), (1, 0, 2))


def kernel(x, p, ffn1_norm, ffn1_w_in, ffn1_w_out, mix_norm, w_mix_in, conv_w, w_conv_out, w_attn_out, w_mix_out, ffn2_norm, ffn2_w_in, ffn2_w_out, ple_norm, w_ple_gate, w_ple_proj, final_norm, loss_target, m_ffn1_norm, m_ffn1_w_in, m_ffn1_w_out, m_mix_norm, m_w_mix_in, m_conv_w, m_w_conv_out, m_w_attn_out, m_w_mix_out, m_ffn2_norm, m_ffn2_w_in, m_ffn2_w_out, m_ple_norm, m_w_ple_gate, m_w_ple_proj, m_final_norm, v_ffn1_norm, v_ffn1_w_in, v_ffn1_w_out, v_mix_norm, v_w_mix_in, v_conv_w, v_w_conv_out, v_w_attn_out, v_w_mix_out, v_ffn2_norm, v_ffn2_w_in, v_ffn2_w_out, v_ple_norm, v_w_ple_gate, v_w_ple_proj, v_final_norm):
    given = dict(locals())
    t = x.shape[1]
    xs = x.reshape(t, D_MODEL)
    ps = p.reshape(t, PLE_DIM)
    target = loss_target.reshape(t, D_MODEL)
    shard = {k: given[k].reshape(given[k].shape[-2:]) for k in _MATRICES}
    gain = {k: given[k].reshape(1, D_MODEL) for k in _GAINS}

    send = {k: shard[k].astype(BF) for k in _MATRICES}
    send["conv_w"] = jnp.pad(shard["conv_w"], ((0, CONV_ROWS - 3), (0, 0)))
    loss_vec, dx, landed, gain_grads = _forward_backward(xs, ps, target, gain, send)
    gain_rows = jnp.concatenate([gain_grads[k] for k in _GAINS] + [loss_vec, jnp.zeros((8 - len(_GAINS) - 1, D_MODEL), F32)], axis=0)
    gain_parts, = _exchange_alone("gather", [gain_rows], "gather_gain_gradients")

    out = {}
    for k in _MATRICES:
        w, m, v = shard[k], given["m_" + k].reshape(shard[k].shape), given["v_" + k].reshape(shard[k].shape)
        part = landed[k]
        if k == "conv_w":
            pad = ((0, CONV_ROWS - 3), (0, 0))
            w, m, v = jnp.pad(w, pad), jnp.pad(m, pad), jnp.pad(v, pad, constant_values=1.0)
        if k in _TRANSPOSED:
            w, m, v = w.T, m.T, v.T
        res = _adamw(part, w, m, v, "adamw_" + k)
        out[k] = [r[:3] if k == "conv_w" else (r.T if k in _TRANSPOSED else r) for r in res]
    stack = lambda pre: jnp.concatenate([given[pre + k].reshape(1, D_MODEL) for k in _GAINS] + [jnp.ones((8 - len(_GAINS), D_MODEL), F32)], axis=0)
    res = _adamw(gain_parts, stack(""), stack("m_"), stack("v_"), "adamw_gains")
    for j, k in enumerate(_GAINS):
        out[k] = [r[j:j + 1] for r in res]

    loss = jnp.sum(gain_parts[:, len(_GAINS), 0])
    per_kind = [[out[k][j].reshape(given[k].shape) for k in _WEIGHTS] for j in range(4)]
    return (loss, dx.reshape(x.shape), *per_kind[0], *per_kind[1], *per_kind[2], *per_kind[3])


def _forward_backward(xs, ps, target, gain, send, full=None):
    exchange = full is None
    full = dict(full or {})
    grads, landed = {}, {}

    def gather(names):
        return ("gather", [send[k] for k in names]) if exchange else None

    def scatter(names):
        return ("scatter", [grads[k] for k in names]) if exchange else None

    def keep(into, names, got):
        into.update(zip(names, got))

    first = ("ffn1_w_in",)
    (n1,), got = _prenorm(xs, gain["ffn1_norm"], ride=gather(first))
    keep(full, first, got)
    w1_in = full["ffn1_w_in"]
    second = ("ffn1_w_out", "w_mix_in")
    (act1, to_gate1, to_up1), got = _ffn_up(n1, w1_in, "ffn1_up", ride=gather(second))
    keep(full, second, got)
    w1_out = full["ffn1_w_out"].reshape(N_FF_CHUNKS, FF_CHUNK, D_MODEL)
    (h1, u), _ = _ffn_down(xs, act1, w1_out, gain["mix_norm"], "ffn1_down")
    w_mix = full["w_mix_in"]
    (proj,), _ = _mix_proj(u, w_mix)
    rest = ("conv_w", "w_conv_out", "w_attn_out", "w_mix_out", "ffn2_w_in", "ffn2_w_out", "w_ple_gate", "w_ple_proj")
    (o, o_bf, a_first, beta_first, reach), got = _attn_fwd(proj, ride=gather(rest))
    keep(full, rest, got)
    w_co, w_ao, w_mo = (full[k].reshape(D_MODEL, D_MODEL) for k in ("w_conv_out", "w_attn_out", "w_mix_out"))
    taps = _columns_from_shards(full["conv_w"][:, :3, :])
    w2_in, w2_out = full["ffn2_w_in"], full["ffn2_w_out"].reshape(N_FF_CHUNKS, FF_CHUNK, D_MODEL)
    w_pg = full["w_ple_gate"].reshape(D_MODEL, D_MODEL)
    w_pp = _columns_from_shards(full["w_ple_proj"])
    h2, n3, ycin, yc, ya, merged = _mixer_out(proj, o_bf, h1, taps, w_co, w_ao, w_mo, gain["ffn2_norm"])
    (act2, to_gate2, to_up2), _ = _ffn_up(n3, w2_in, "ffn2_up")
    (h3, n4), _ = _ffn_down(h2, act2, w2_out, gain["ple_norm"], "ffn2_down")
    dh3, ds, dpp, loss_vec, dg_final, dg_ple = _tail(h3, n4, ps, w_pg, w_pp, gain["ple_norm"], gain["final_norm"], target)

    one = lambda a: a[None]
    by_rows = lambda g, rows: g.reshape(N_SHARDS, rows // N_SHARDS, D_MODEL)
    square = WGRAD_TILE // 2
    grads["w_ple_gate"] = by_rows(_wgrad(one(n4), one(ds), "wgrad_ple_gate", tile=square), D_MODEL)
    grads["w_ple_proj"] = _shards_from_columns(_wgrad(one(ps), one(dpp), "wgrad_ple_proj")[0])
    ple = ("w_ple_gate", "w_ple_proj")
    (df2, dgate2, dup2), got = _ffn_bwd_hidden(dh3, to_gate2, to_up2, w2_out, "ffn2_bwd_hidden", ride=scatter(ple))
    keep(landed, ple, got)
    grads["ffn2_w_out"] = by_rows(_wgrad(act2, one(df2), "wgrad_ffn2_out"), D_FF)
    grads["ffn2_w_in"] = _wgrad_pieces(n3, [dgate2, dup2], "wgrad_ffn2_in", transposed=True)
    (dh2, dg_ffn2), got = _ffn_bwd_input(dh3, h2, gain["ffn2_norm"], dgate2, dup2, w2_in, "ffn2_bwd_input", ride=scatter(("ffn2_w_out",)))
    keep(landed, ("ffn2_w_out",), got)
    (dh2b, dyc, dya, dgc, dga, dcb, dcv, d_o), _ = _mixer_bwd(dh2, proj, yc, ya, taps, w_co, w_ao, w_mo)
    grads["w_mix_out"] = by_rows(_wgrad(one(merged), one(dh2b), "wgrad_mix_out", tile=square), D_MODEL)
    grads["w_conv_out"] = by_rows(_wgrad(one(ycin), one(dyc), "wgrad_conv_out", tile=square), D_MODEL)
    grads["w_attn_out"] = by_rows(_wgrad(one(o_bf), one(dya), "wgrad_attn_out", tile=square), D_MODEL)
    dcc, dcx, dtaps = _conv_bwd(dcv, proj, taps)
    grads["conv_w"] = jnp.pad(_shards_from_columns(dtaps[:3]), ((0, 0), (0, CONV_ROWS - 3), (0, 0)))
    behind_attn = ("ffn2_w_in", "w_mix_out", "w_conv_out", "w_attn_out", "conv_w")
    (dq, dk, dv), got = _attn_bwd(proj, o, d_o, a_first, beta_first, reach, ride=scatter(behind_attn))
    keep(landed, behind_attn, got)
    dpieces = [dcb, dcc, dcx, dq, dk, dv, dgc, dga]
    half = N_MIX // 2
    tops, bottoms = zip(_wgrad_pieces(u, [one(dp) for dp in dpieces[:half]], "wgrad_mix_in_a", tile=WGRAD_TILE // 2, row_parts=2),
                        _wgrad_pieces(u, [one(dp) for dp in dpieces[half:]], "wgrad_mix_in_b", tile=WGRAD_TILE // 2, row_parts=2))
    grads["w_mix_in top"], grads["w_mix_in bottom"] = jnp.concatenate(tops, axis=0), jnp.concatenate(bottoms, axis=0)
    (dh1, dg_mix), top = _mix_bwd(dpieces, w_mix, h1, dh2, gain["mix_norm"], ride=scatter(("w_mix_in top",)))
    (df1, dgate1, dup1), bottom = _ffn_bwd_hidden(dh1, to_gate1, to_up1, w1_out, "ffn1_bwd_hidden", ride=scatter(("w_mix_in bottom",)))
    if exchange:
        landed["w_mix_in"] = [top[0], bottom[0]]
    else:
        grads["w_mix_in"] = jnp.concatenate([grads.pop("w_mix_in top"), grads.pop("w_mix_in bottom")], axis=1)
    grads["ffn1_w_out"] = by_rows(_wgrad(act1, one(df1), "wgrad_ffn1_out"), D_FF)
    if exchange:
        grads["ffn1_w_in"], got = _wgrad_pieces(n1, [dgate1, dup1], "wgrad_ffn1_in", transposed=True, ride=scatter(("ffn1_w_out",)))
        keep(landed, ("ffn1_w_out",), got)
    else:
        grads["ffn1_w_in"] = _wgrad_pieces(n1, [dgate1, dup1], "wgrad_ffn1_in", transposed=True)
    (dx, dg_ffn1), got = _ffn_bwd_input(dh1, xs, gain["ffn1_norm"], dgate1, dup1, w1_in, "ffn1_bwd_input", ride=scatter(("ffn1_w_in",)))
    keep(landed, ("ffn1_w_in",), got)
    gain_grads = dict(ffn1_norm=dg_ffn1, mix_norm=dg_mix, ffn2_norm=dg_ffn2, ple_norm=dg_ple, final_norm=dg_final)
    return loss_vec, dx, (landed if exchange else grads), gain_grads
```

```python
import functools
import math

import jax
import jax.numpy as jnp
from jax import lax
from jax.experimental import pallas as pl
from jax.experimental.pallas import tpu as pltpu

D_MODEL = 1024
D_FF = 2816
N_SHARDS = 8
FF_CHUNK = 2 * D_FF // N_SHARDS
N_FF_CHUNKS = D_FF // FF_CHUNK
N_HEADS = 8
HEAD_DIM = 128
PLE_DIM = 256
NORM_EPS = 1e-6
N_MIX = 8
ADAM_LR, ADAM_B1, ADAM_B2, ADAM_EPS, ADAM_WD, ADAM_STEP = 0.001, 0.9, 0.999, 1e-08, 0.01, 10

TOKEN_TILE = 512
WGRAD_TILE = 4096
PROJ_TILE = 2048
ATTN_ROWS = 256
ATTN_Q = 64
ATTN_SUB = 128
ATTN_K = 2 * ATTN_SUB
ATTN_SKIP_BELOW = -90.0

BF = jnp.bfloat16
F32 = jnp.float32
MESH = pl.DeviceIdType.MESH
NT = (((1,), (1,)), ((), ()))
TN = (((0,), (0,)), ((), ()))
S = jax.ShapeDtypeStruct
ANY = pl.BlockSpec(memory_space=pl.ANY)


def _const_spec(shape):
    nd = len(shape)
    return pl.BlockSpec(shape, lambda *_: (0,) * nd, pipeline_mode=pl.Buffered(1))


def _rows(tm, cols):
    return pl.BlockSpec((tm, cols), lambda i: (i, 0))


def _chunks(tm):
    return pl.BlockSpec((N_FF_CHUNKS, tm, FF_CHUNK), lambda i: (0, i, 0))


def _acc_spec(shape):
    nd = len(shape)
    return pl.BlockSpec(shape, lambda *_: (0,) * nd)


def _dot(a, b):
    return jnp.dot(a, b, preferred_element_type=F32)


def _dot_nt(a, b):
    return lax.dot_general(a, b, NT, preferred_element_type=F32)


def _dot_tn(a, b):
    return lax.dot_general(a, b, TN, preferred_element_type=F32)


def _rms(h, g):
    r = lax.rsqrt(jnp.mean(h * h, axis=-1, keepdims=True) + NORM_EPS)
    return h * r * g


def _rms_bwd(dn, h, g):
    r = lax.rsqrt(jnp.mean(h * h, axis=-1, keepdims=True) + NORM_EPS)
    nh = h * r
    gd = dn * g
    dh = r * (gd - nh * jnp.mean(gd * nh, axis=-1, keepdims=True))
    return dh, jnp.sum(dn * nh, axis=0, keepdims=True)


def _accumulate(ref, val):
    @pl.when(pl.program_id(0) == 0)
    def _():
        ref[...] = jnp.zeros_like(ref)
    ref[...] += val


def _place():
    x, y, c = lax.axis_index("x"), lax.axis_index("y"), lax.axis_index("c")
    return x, y, c


def _slot(px, py, pc):
    return 4 * px + 2 * py + pc


def _gather_phases(ins, outs, send_sems, recv_sems, local_sems):
    n = len(ins)

    def parties():
        x, y, c = _place()
        return (x, y, c), (x, y, 1 - c), [(1 - x, y), (x, 1 - y), (1 - x, 1 - y)], c

    def copy(a, k, block, to, src=None):
        dst = outs[a].at[_slot(*block)]
        return pltpu.make_async_remote_copy(
            src_ref=dst if src is None else src, dst_ref=dst,
            send_sem=send_sems.at[a, k], recv_sem=recv_sems.at[a, k],
            device_id=to, device_id_type=MESH)

    def own(a, me):
        return pltpu.make_async_copy(ins[a], outs[a].at[_slot(*me)], local_sems.at[a])

    def first(a, me, sibling, chips, c):
        return [copy(a, 0, me, sibling, src=ins[a])] + [copy(a, 1 + j, me, (*chip, c), src=ins[a]) for j, chip in enumerate(chips)]

    def start():
        me, sibling, chips, c = parties()
        for a in range(n):
            own(a, me).start()
        for a in range(n):
            for cp in first(a, me, sibling, chips, c):
                cp.start()

    def forward():
        me, sibling, chips, c = parties()
        for j, chip in enumerate(chips):
            for a in range(n):
                copy(a, 1 + j, (*chip, c), me).wait_recv()
                copy(a, 4 + j, (*chip, c), sibling).start()

    def finish():
        me, sibling, chips, c = parties()
        for a in range(n):
            copy(a, 0, sibling, me).wait_recv()
            for j, chip in enumerate(chips):
                copy(a, 4 + j, (*chip, 1 - c), me).wait_recv()
        for a in range(n):
            for cp in first(a, me, sibling, chips, c) + [copy(a, 4 + j, (*chip, c), sibling) for j, chip in enumerate(chips)]:
                cp.wait_send()
            own(a, me).wait()

    return [start, forward, finish]


def _scatter_phases(ins, outs, send_sems, recv_sems, local_sems):
    n = len(ins)

    def copies():
        x, y, c = _place()
        me = _slot(x, y, c)
        out = [pltpu.make_async_copy(ins[a].at[me], outs[a].at[me], local_sems.at[a]) for a in range(n)]
        for k in range(1, N_SHARDS):
            px = 1 - x if k & 4 else x
            py = 1 - y if k & 2 else y
            pc = 1 - c if k & 1 else c
            for a in range(n):
                out.append(pltpu.make_async_remote_copy(
                    src_ref=ins[a].at[_slot(px, py, pc)], dst_ref=outs[a].at[me],
                    send_sem=send_sems.at[a, k - 1], recv_sem=recv_sems.at[a, k - 1],
                    device_id=(px, py, pc), device_id_type=MESH))
        return out

    def start():
        for cp in copies():
            cp.start()

    def finish():
        for cp in copies():
            cp.wait()

    return [start, finish]


def _pallas(body, *, name, grid, in_specs, out_specs, out_shape, args, scratch_shapes=(), ride=None):
    if ride is None:
        outs = pl.pallas_call(body, name=name, grid=grid, in_specs=in_specs, out_specs=out_specs, out_shape=out_shape,
                              scratch_shapes=list(scratch_shapes))(*args)
        return list(outs), []
    kind, arrays = ride
    n, n_in, n_out, n_scr = len(arrays), len(in_specs), len(out_specs), len(scratch_shapes)
    total = math.prod(grid)
    middle = (9 * total) // 10
    landed_shape = [S((N_SHARDS,) + a.shape if kind == "gather" else a.shape, a.dtype) for a in arrays]

    def with_exchange(*refs):
        ins, riders_in = refs[:n_in], refs[n_in:n_in + n]
        outs, riders_out = refs[n_in + n:n_in + n + n_out], refs[n_in + n + n_out:n_in + 2 * n + n_out]
        scratch, sems = refs[n_in + 2 * n + n_out:n_in + 2 * n + n_out + n_scr], refs[n_in + 2 * n + n_out + n_scr:]
        step = 0
        for axis, size in enumerate(grid):
            step = step * size + pl.program_id(axis)
        phases = (_gather_phases if kind == "gather" else _scatter_phases)(riders_in, riders_out, *sems)
        pl.when(step == 0)(phases[0])
        body(*ins, *outs, *scratch)
        for phase in phases[1:-1]:
            pl.when(step == middle)(phase)
        pl.when(step == total - 1)(phases[-1])

    outs = pl.pallas_call(
        with_exchange, name=name, grid=grid,
        in_specs=list(in_specs) + [ANY] * n, out_specs=list(out_specs) + [ANY] * n,
        out_shape=list(out_shape) + landed_shape,
        scratch_shapes=list(scratch_shapes) + [pltpu.SemaphoreType.DMA((n, 7)), pltpu.SemaphoreType.DMA((n, 7)),
                                               pltpu.SemaphoreType.DMA((n,))],
    )(*args, *arrays)
    return list(outs[:n_out]), list(outs[n_out:])


def _exchange_alone(kind, arrays, name):
    return _pallas(lambda: None, name=name, grid=(1,), in_specs=[], out_specs=[], out_shape=[], args=[], ride=(kind, arrays))[1]


def _prenorm(x, g, ride=None):
    t = x.shape[0]
    tm = min(TOKEN_TILE, t)

    def body(x_ref, g_ref, n_ref):
        n_ref[...] = _rms(x_ref[...], g_ref[...]).astype(BF)

    return _pallas(
        body, name="prenorm", grid=(t // tm,), ride=ride,
        in_specs=[_rows(tm, D_MODEL), _const_spec((1, D_MODEL))], out_specs=[_rows(tm, D_MODEL)],
        out_shape=[S((t, D_MODEL), BF)], args=[x, g])


def _ffn_up(n, w_in, name, ride=None):
    t = n.shape[0]
    tm = min(TOKEN_TILE, t)

    def body(n_ref, win_ref, act_ref, to_gate_ref, to_up_ref):
        nb = n_ref[...]
        for c in range(N_FF_CHUNKS):
            gate = _dot(nb, win_ref[c])
            up = _dot(nb, win_ref[N_FF_CHUNKS + c])
            sg = jax.nn.sigmoid(gate)
            silu = gate * sg
            act_ref[c] = (silu * up).astype(BF)
            to_gate_ref[c] = (up * (sg * (1.0 + gate * (1.0 - sg)))).astype(BF)
            to_up_ref[c] = silu.astype(BF)

    return _pallas(
        body, name=name, grid=(t // tm,), ride=ride,
        in_specs=[_rows(tm, D_MODEL), _const_spec(w_in.shape)],
        out_specs=[_chunks(tm)] * 3, out_shape=[S((N_FF_CHUNKS, t, FF_CHUNK), BF)] * 3,
        args=[n, w_in])


def _ffn_down(h, act, w_out, g_next, name, ride=None):
    t = h.shape[0]
    tm = min(TOKEN_TILE, t)

    def body(h_ref, act_ref, wout_ref, g_ref, ho_ref, no_ref):
        acc = jnp.zeros((tm, D_MODEL), F32)
        for c in range(N_FF_CHUNKS):
            acc = acc + _dot(act_ref[c], wout_ref[c])
        ho = h_ref[...] + 0.5 * acc
        ho_ref[...] = ho
        no_ref[...] = _rms(ho, g_ref[...]).astype(BF)

    return _pallas(
        body, name=name, grid=(t // tm,), ride=ride,
        in_specs=[_rows(tm, D_MODEL), _chunks(tm), _const_spec(w_out.shape), _const_spec((1, D_MODEL))],
        out_specs=[_rows(tm, D_MODEL)] * 2, out_shape=[S((t, D_MODEL), F32), S((t, D_MODEL), BF)],
        args=[h, act, w_out, g_next])


def _mix_proj(u, w_mix, ride=None):
    t = u.shape[0]
    tm = min(PROJ_TILE, t)

    def body(u_ref, w_ref, o_ref):
        o_ref[0] = _dot(u_ref[...], w_ref[0]).astype(BF)

    return _pallas(
        body, name="mix_proj", grid=(N_MIX, t // tm), ride=ride,
        in_specs=[pl.BlockSpec((tm, D_MODEL), lambda d, i: (i, 0)), pl.BlockSpec((1, D_MODEL, D_MODEL), lambda d, i: (d, 0, 0))],
        out_specs=[pl.BlockSpec((1, tm, D_MODEL), lambda d, i: (d, i, 0))],
        out_shape=[S((N_MIX, t, D_MODEL), BF)], args=[u, w_mix])


HALO = 16


def _piece(d, tm):
    return pl.BlockSpec((1, tm, D_MODEL), lambda i: (d, i, 0))


def _prev_halo(d, tm):
    return pl.BlockSpec((1, HALO, D_MODEL), lambda i: (d, jnp.maximum(i * (tm // HALO) - 1, 0), 0))


def _shift_down(m, prev_tail, k):
    tm = m.shape[0]
    out = pltpu.roll(m, k, 0)
    row = lax.broadcasted_iota(jnp.int32, (tm, 1), 0)
    for j in range(k):
        out = jnp.where(row == j, prev_tail[HALO - k + j:HALO - k + j + 1, :], out)
    return out


def _conv_inputs(cc_ref, cx_ref, cch_ref, cxh_ref):
    m = cc_ref[0].astype(F32) * cx_ref[0].astype(F32)
    mh = cch_ref[0].astype(F32) * cxh_ref[0].astype(F32)
    mh = jnp.where(pl.program_id(0) == 0, 0.0, mh)
    return m, _shift_down(m, mh, 1), _shift_down(m, mh, 2)


def _mixer_out(proj, o, h1, conv_w, w_co, w_ao, w_mo, g_next):
    t = h1.shape[0]
    tm = min(TOKEN_TILE, t)

    def body(cb_ref, cc_ref, cx_ref, gc_ref, ga_ref, cch_ref, cxh_ref, o_ref, h_ref, cw_ref, wco_ref, wao_ref, wmo_ref,
             g_ref, ho_ref, no_ref, ycin_ref, yc_ref, ya_ref, mg_ref):
        m, m1, m2 = _conv_inputs(cc_ref, cx_ref, cch_ref, cxh_ref)
        cw = cw_ref[...]
        cv = cw[0:1, :] * m2 + cw[1:2, :] * m1 + cw[2:3, :] * m
        ycin = (cb_ref[0].astype(F32) * cv).astype(BF)
        ycin_ref[...] = ycin
        yc = _dot(ycin, wco_ref[...])
        ya = _dot(o_ref[...], wao_ref[...])
        yc_ref[...] = yc.astype(BF)
        ya_ref[...] = ya.astype(BF)
        merged = (jax.nn.sigmoid(gc_ref[0].astype(F32)) * yc + jax.nn.sigmoid(ga_ref[0].astype(F32)) * ya).astype(BF)
        mg_ref[...] = merged
        ho = h_ref[...] + _dot(merged, wmo_ref[...])
        ho_ref[...] = ho
        no_ref[...] = _rms(ho, g_ref[...]).astype(BF)

    sq = (D_MODEL, D_MODEL)
    return pl.pallas_call(
        body, name="mixer_out", grid=(t // tm,),
        in_specs=[_piece(0, tm), _piece(1, tm), _piece(2, tm), _piece(6, tm), _piece(7, tm), _prev_halo(1, tm), _prev_halo(2, tm),
                  _rows(tm, D_MODEL), _rows(tm, D_MODEL), _const_spec((3, D_MODEL)), _const_spec(sq), _const_spec(sq),
                  _const_spec(sq), _const_spec((1, D_MODEL))],
        out_specs=[_rows(tm, D_MODEL)] * 6,
        out_shape=[S((t, D_MODEL), F32)] + [S((t, D_MODEL), BF)] * 5,
    )(proj, proj, proj, proj, proj, proj, proj, o, h1, conv_w, w_co, w_ao, w_mo, g_next)


def _suffix_sums(vals, tri, before):
    out, right = [], before
    for b in reversed(range(ATTN_K // ATTN_SUB)):
        v = vals[:, b * ATTN_SUB:(b + 1) * ATTN_SUB]
        hi = v.astype(BF)
        lo = (v - hi.astype(F32)).astype(BF)
        out.append(_dot(hi, tri) + _dot(lo, tri) + right)
        right = right + jnp.sum(v, axis=1, keepdims=True)
    return jnp.concatenate(out[::-1], axis=1), right


ATTN_UNITS = ATTN_ROWS // ATTN_Q


def _unit_rows(x, u):
    return x[u * ATTN_Q:(u + 1) * ATTN_Q]


def _per_unit(fn):
    return jnp.concatenate([fn(u) for u in range(ATTN_UNITS)], axis=0)


def _per_row(vals):
    local = lax.broadcasted_iota(jnp.int32, (ATTN_ROWS, 1), 0)
    out = jnp.full((ATTN_ROWS, 1), vals[0], jnp.int32)
    for u in range(1, ATTN_UNITS):
        out = jnp.where(local >= u * ATTN_Q, vals[u], out)
    return out


def _attn_step(q, k_ref, starts, bounds, row):
    z = _per_unit(lambda u: _dot_nt(_unit_rows(q, u), k_ref[0, pl.ds(starts[u], ATTN_K), :])) * (1.0 / math.sqrt(HEAD_DIM))
    mask = lax.broadcasted_iota(jnp.int32, (1, ATTN_K), 1) < jnp.minimum(row, _per_row(bounds)) - _per_row(starts)
    log_beta = jnp.minimum(z, 0.0) - jnp.log(1.0 + jnp.exp(jnp.minimum(z, -z)))
    log_rest = jnp.where(mask, log_beta - z, 0.0)
    return z, mask, log_beta, log_rest


def _attn_sweep_start(i, t):
    blks = tuple(jnp.maximum(i * ATTN_UNITS + u + 1 - ATTN_K // ATTN_Q, 0) for u in range(ATTN_UNITS))
    return blks, tuple(jnp.int32(t) for _ in range(ATTN_UNITS))


def _attn_keys(blks):
    return [pl.multiple_of(b * ATTN_Q, ATTN_Q) for b in blks]


def _attn_next(blks):
    return tuple(jnp.maximum(b - ATTN_K // ATTN_Q, 0) for b in blks), tuple(b * ATTN_Q for b in blks)


def _attn_more(carry):
    return jnp.logical_and(carry[1][ATTN_UNITS - 1] > 0, carry[-1] > ATTN_SKIP_BELOW)


def _tri(strict):
    r = lax.broadcasted_iota(jnp.int32, (ATTN_SUB, ATTN_SUB), 0)
    c = lax.broadcasted_iota(jnp.int32, (ATTN_SUB, ATTN_SUB), 1)
    return (r > c if strict else r >= c).astype(BF)


REACH_TILE = (8, 128)


def _first_step_spec():
    return pl.BlockSpec((1, ATTN_ROWS, ATTN_K), lambda h, i: (h, i, 0))


def _reach_spec():
    return pl.BlockSpec((1, 1) + REACH_TILE, lambda h, i: (h, i, 0, 0))


def _head_cols(piece):
    return lambda t: pl.BlockSpec((1, t, HEAD_DIM), lambda h, i: (piece, 0, h))


def _attn_fwd(proj):
    t = proj.shape[1]
    nq = t // ATTN_ROWS
    tri = _tri(strict=True)

    def body(q_ref, k_ref, v_ref, tri_ref, o_ref, ob_ref, a_ref, beta_ref, reach_ref):
        i = pl.program_id(1)
        q = q_ref[0]
        row = i * ATTN_ROWS + lax.broadcasted_iota(jnp.int32, (ATTN_ROWS, 1), 0)

        def step(carry, keep=False):
            blks, bounds, acc, run, _ = carry
            starts = _attn_keys(blks)
            _, mask, log_beta, log_rest = _attn_step(q, k_ref, starts, bounds, row)
            tail, run = _suffix_sums(log_rest, tri_ref[...], run)
            a = jnp.where(mask, jnp.exp(log_beta + tail), 0.0).astype(BF)
            if keep:
                a_ref[0] = a
                beta_ref[0] = jnp.where(mask, jnp.exp(log_beta), 0.0).astype(BF)
            acc = acc + _per_unit(lambda u: _dot(_unit_rows(a, u), v_ref[0, pl.ds(starts[u], ATTN_K), :]))
            return (*_attn_next(blks), acc, run, jnp.max(run))

        first = (*_attn_sweep_start(i, t), jnp.zeros((ATTN_ROWS, HEAD_DIM), F32), jnp.zeros((ATTN_ROWS, 1), F32), jnp.float32(0.0))
        after_first = step(first, keep=True)
        reach_ref[...] = jnp.full(reach_ref.shape, after_first[-1], F32)
        o = lax.while_loop(_attn_more, step, after_first)[2]
        o_ref[...] = o
        ob_ref[...] = o.astype(BF)

    qspec = pl.BlockSpec((1, ATTN_ROWS, HEAD_DIM), lambda h, i: (3, i, h))
    rowblk = pl.BlockSpec((ATTN_ROWS, HEAD_DIM), lambda h, i: (i, h))
    return pl.pallas_call(
        body, name="attn_fwd", grid=(N_HEADS, nq),
        in_specs=[qspec, _head_cols(4)(t), _head_cols(5)(t), pl.BlockSpec((ATTN_SUB, ATTN_SUB), lambda h, i: (0, 0))],
        out_specs=[rowblk, rowblk, _first_step_spec(), _first_step_spec(), _reach_spec()],
        out_shape=[S((t, D_MODEL), F32), S((t, D_MODEL), BF), S((N_HEADS, t, ATTN_K), BF), S((N_HEADS, t, ATTN_K), BF),
                   S((N_HEADS, nq) + REACH_TILE, F32)],
    )(proj, proj, proj, tri)


def _attn_bwd(proj, o, d_o, a_first, beta_first, reach, ride=None):
    t = proj.shape[1]
    nq = t // ATTN_ROWS
    tri_strict, tri_incl = _tri(strict=True), _tri(strict=False)
    scale = 1.0 / math.sqrt(HEAD_DIM)

    def body(q_ref, k_ref, v_ref, o_ref, do_ref, a_ref, beta_ref, reach_ref, tris_ref, trii_ref, dq_ref, dk_ref, dv_ref, dk_acc, dv_acc):
        i = pl.program_id(1)

        @pl.when(i == 0)
        def _():
            dk_acc[...] = jnp.zeros_like(dk_acc)
            dv_acc[...] = jnp.zeros_like(dv_acc)

        q = q_ref[0]
        do = do_ref[...]
        total = jnp.sum(do.astype(F32) * o_ref[...], axis=1, keepdims=True)
        zero = jnp.zeros((ATTN_ROWS, 1), F32)
        blks0, bounds0 = _attn_sweep_start(i, t)

        def finish(starts, a, dz, dq):
            dzb = (dz * scale).astype(BF)
            for u in range(ATTN_UNITS):
                dv_acc[pl.ds(starts[u], ATTN_K), :] += _dot_tn(_unit_rows(a, u), _unit_rows(do, u))
                dk_acc[pl.ds(starts[u], ATTN_K), :] += _dot_tn(_unit_rows(dzb, u), _unit_rows(q, u))
            return dq + _per_unit(lambda u: _dot(_unit_rows(dzb, u), k_ref[0, pl.ds(starts[u], ATTN_K), :]))

        def grad_a(starts, a):
            return _per_unit(lambda u: _dot_nt(_unit_rows(do, u), v_ref[0, pl.ds(starts[u], ATTN_K), :])) * a.astype(F32)

        one_step = jnp.max(reach_ref[...]) <= ATTN_SKIP_BELOW

        @pl.when(one_step)
        def _():
            starts = _attn_keys(blks0)
            a = a_ref[0]
            beta = beta_ref[0].astype(F32)
            de = grad_a(starts, a)
            right, _ = _suffix_sums(de, trii_ref[...], zero)
            dz = de * (1.0 - beta) - (total - right) * beta
            dq_ref[...] = finish(starts, a, dz, jnp.zeros((ATTN_ROWS, HEAD_DIM), F32)).astype(BF)

        @pl.when(jnp.logical_not(one_step))
        def _():
            row = i * ATTN_ROWS + lax.broadcasted_iota(jnp.int32, (ATTN_ROWS, 1), 0)

            def step(carry):
                blks, bounds, dq, seen, run, _ = carry
                starts = _attn_keys(blks)
                z, mask, log_beta, log_rest = _attn_step(q, k_ref, starts, bounds, row)
                tail, run = _suffix_sums(log_rest, tris_ref[...], run)
                a = jnp.where(mask, jnp.exp(log_beta + tail), 0.0).astype(BF)
                de = grad_a(starts, a)
                right, seen = _suffix_sums(de, trii_ref[...], seen)
                beta = jax.nn.sigmoid(z)
                dz = jnp.where(mask, de * (1.0 - beta) - (total - right) * beta, 0.0)
                return (*_attn_next(blks), finish(starts, a, dz, dq), seen, run, jnp.max(run))

            first = (blks0, bounds0, jnp.zeros((ATTN_ROWS, HEAD_DIM), F32), zero, zero, jnp.float32(0.0))
            dq_ref[...] = lax.while_loop(_attn_more, step, step(first))[2].astype(BF)

        @pl.when(i == nq - 1)
        def _():
            dk_ref[...] = dk_acc[...].astype(BF)
            dv_ref[...] = dv_acc[...].astype(BF)

    qspec = pl.BlockSpec((1, ATTN_ROWS, HEAD_DIM), lambda h, i: (3, i, h))
    rowblk = pl.BlockSpec((ATTN_ROWS, HEAD_DIM), lambda h, i: (i, h))
    head = pl.BlockSpec((t, HEAD_DIM), lambda h, i: (0, h))
    trispec = pl.BlockSpec((ATTN_SUB, ATTN_SUB), lambda h, i: (0, 0))
    return _pallas(
        body, name="attn_bwd", grid=(N_HEADS, nq), ride=ride,
        in_specs=[qspec, _head_cols(4)(t), _head_cols(5)(t), rowblk, rowblk, _first_step_spec(), _first_step_spec(), _reach_spec(),
                  trispec, trispec],
        out_specs=[rowblk, head, head],
        out_shape=[S((t, D_MODEL), BF)] * 3,
        scratch_shapes=[pltpu.VMEM((t, HEAD_DIM), F32), pltpu.VMEM((t, HEAD_DIM), F32)],
        args=[proj, proj, proj, o, d_o, a_first, beta_first, reach, tri_strict, tri_incl])


def _tail(h3, n4, p, w_pg, w_pp, g_ple, g_final, target):
    t = h3.shape[0]
    tm = min(TOKEN_TILE, t)
    steps = t // tm

    def body(h_ref, n_ref, p_ref, wpg_ref, wpp_ref, gp_ref, gf_ref, tgt_ref,
             dh_ref, ds_ref, dpp_ref, loss_ref, dgf_ref, dgp_ref):
        pg = jax.nn.sigmoid(_dot(n_ref[...], wpg_ref[...]))
        pp = _dot(p_ref[...].astype(BF), wpp_ref[...])
        h3v = h_ref[...]
        h4 = h3v + pg * pp
        gf = gf_ref[...]
        diff = _rms(h4, gf) - tgt_ref[...]
        _accumulate(loss_ref, jnp.sum(diff * diff, axis=0, keepdims=True))
        dh4, dgf = _rms_bwd(diff * (1.0 / D_MODEL), h4, gf)
        _accumulate(dgf_ref, dgf)
        dpp_ref[...] = (dh4 * pg).astype(BF)
        ds = (dh4 * pp * pg * (1.0 - pg)).astype(BF)
        ds_ref[...] = ds
        dh3, dgp = _rms_bwd(_dot_nt(ds, wpg_ref[...]), h3v, gp_ref[...])
        _accumulate(dgp_ref, dgp)
        dh_ref[...] = dh4 + dh3

        @pl.when(pl.program_id(0) == steps - 1)
        def _():
            loss_ref[...] = jnp.full(loss_ref.shape, 0.5 / D_MODEL * jnp.sum(loss_ref[...]), F32)

    vec = (1, D_MODEL)
    return pl.pallas_call(
        body, name="tail", grid=(steps,),
        in_specs=[_rows(tm, D_MODEL), _rows(tm, D_MODEL), _rows(tm, PLE_DIM), _const_spec((D_MODEL, D_MODEL)),
                  _const_spec((PLE_DIM, D_MODEL)), _const_spec(vec), _const_spec(vec), _rows(tm, D_MODEL)],
        out_specs=[_rows(tm, D_MODEL)] * 3 + [_acc_spec(vec)] * 3,
        out_shape=[S((t, D_MODEL), F32), S((t, D_MODEL), BF), S((t, D_MODEL), BF)] + [S(vec, F32)] * 3,
    )(h3, n4, p, w_pg, w_pp, g_ple, g_final, target)


def _wgrad(xs, ys, name, ride=None, tile=None):
    bx, t, k = xs.shape
    by, _, n = ys.shape
    b = max(bx, by)
    tt = min(tile or WGRAD_TILE * 2 // xs.dtype.itemsize, t)
    steps = t // tt

    def body(x_ref, y_ref, o_ref, acc_ref):
        s = pl.program_id(1)

        @pl.when(s == 0)
        def _():
            acc_ref[...] = jnp.zeros_like(acc_ref)
        acc_ref[...] += _dot_tn(x_ref[0].astype(BF), y_ref[0].astype(BF))

        @pl.when(s == steps - 1)
        def _():
            o_ref[0] = acc_ref[...].astype(BF)

    (out,), landed = _pallas(
        body, name=name, grid=(b, steps), ride=ride,
        in_specs=[pl.BlockSpec((1, tt, k), (lambda j, s: (j, s, 0)) if bx > 1 else (lambda j, s: (0, s, 0))),
                  pl.BlockSpec((1, tt, n), (lambda j, s: (j, s, 0)) if by > 1 else (lambda j, s: (0, s, 0)))],
        out_specs=[pl.BlockSpec((1, k, n), lambda j, s: (j, 0, 0))],
        out_shape=[S((b, k, n), BF)],
        scratch_shapes=[pltpu.VMEM((k, n), F32)],
        args=[xs, ys])
    return (out, landed) if ride is not None else out


def _wgrad_pieces(x, ys, name, ride=None, tile=None, row_parts=1, transposed=False):
    t, k = x.shape
    n = ys[0].shape[2]
    counts = [y.shape[0] for y in ys]
    offsets = [sum(counts[:j]) for j in range(len(ys))]
    total = sum(counts)
    tt = min(tile or WGRAD_TILE, t)
    steps = t // tt
    rows, cols = (n, k) if transposed else (k, n)
    kp = rows // row_parts

    def body(x_ref, *refs):
        y_refs, o_refs, acc_ref = refs[:len(ys)], refs[len(ys):len(ys) + row_parts], refs[len(ys) + row_parts]
        p, s = pl.program_id(0), pl.program_id(1)

        @pl.when(s == 0)
        def _():
            acc_ref[...] = jnp.zeros_like(acc_ref)
        for j, y_ref in enumerate(y_refs):
            @pl.when(jnp.logical_and(p >= offsets[j], p < offsets[j] + counts[j]))
            def _(y_ref=y_ref):
                acc_ref[...] += _dot_tn(y_ref[0], x_ref[...]) if transposed else _dot_tn(x_ref[...], y_ref[0])

        @pl.when(s == steps - 1)
        def _():
            for part, o_ref in enumerate(o_refs):
                o_ref[0] = acc_ref[part * kp:(part + 1) * kp, :].astype(BF)

    def turn(j):
        lo, hi = offsets[j], offsets[j] + counts[j]
        return lambda p, s: (jnp.clip(p - lo, 0, counts[j] - 1), jnp.where(p < lo, 0, jnp.where(p >= hi, steps - 1, s)), 0)

    outs, landed = _pallas(
        body, name=name, grid=(total, steps), ride=ride,
        in_specs=[pl.BlockSpec((tt, k), lambda p, s: (s, 0))] + [pl.BlockSpec((1, tt, n), turn(j)) for j in range(len(ys))],
        out_specs=[pl.BlockSpec((1, kp, cols), lambda p, s: (p, 0, 0))] * row_parts,
        out_shape=[S((total, kp, cols), BF)] * row_parts,
        scratch_shapes=[pltpu.VMEM((rows, cols), F32)],
        args=[x, *ys])
    out = outs[0] if row_parts == 1 else outs
    return (out, landed) if ride is not None else out


def _ffn_bwd_hidden(dh, to_gate, to_up, w_out, name, ride=None):
    t = dh.shape[0]
    tm = min(TOKEN_TILE, t)

    def body(dh_ref, to_gate_ref, to_up_ref, wout_ref, df_ref, dgate_ref, dup_ref):
        df = (0.5 * dh_ref[...]).astype(BF)
        df_ref[...] = df
        for c in range(N_FF_CHUNKS):
            dact = _dot_nt(df, wout_ref[c])
            dgate_ref[c] = (dact * to_gate_ref[c].astype(F32)).astype(BF)
            dup_ref[c] = (dact * to_up_ref[c].astype(F32)).astype(BF)

    return _pallas(
        body, name=name, grid=(t // tm,), ride=ride,
        in_specs=[_rows(tm, D_MODEL), _chunks(tm), _chunks(tm), _const_spec(w_out.shape)],
        out_specs=[_rows(tm, D_MODEL), _chunks(tm), _chunks(tm)],
        out_shape=[S((t, D_MODEL), BF)] + [S((N_FF_CHUNKS, t, FF_CHUNK), BF)] * 2,
        args=[dh, to_gate, to_up, w_out])


def _ffn_bwd_input(dh, h_in, g, dgate, dup, w_in, name, ride=None):
    t = dh.shape[0]
    tm = min(TOKEN_TILE, t)

    def body(dh_ref, h_ref, g_ref, dgate_ref, dup_ref, win_ref, dhi_ref, dg_ref):
        dn = jnp.zeros((tm, D_MODEL), F32)
        for c in range(N_FF_CHUNKS):
            dn = dn + _dot_nt(dgate_ref[c], win_ref[c]) + _dot_nt(dup_ref[c], win_ref[N_FF_CHUNKS + c])
        dhi, dg = _rms_bwd(dn, h_ref[...], g_ref[...])
        _accumulate(dg_ref, dg)
        dhi_ref[...] = dh_ref[...] + dhi

    vec = (1, D_MODEL)
    return _pallas(
        body, name=name, grid=(t // tm,), ride=ride,
        in_specs=[_rows(tm, D_MODEL), _rows(tm, D_MODEL), _const_spec(vec), _chunks(tm), _chunks(tm), _const_spec(w_in.shape)],
        out_specs=[_rows(tm, D_MODEL), _acc_spec(vec)],
        out_shape=[S((t, D_MODEL), F32), S(vec, F32)],
        args=[dh, h_in, g, dgate, dup, w_in])


def _mixer_bwd(dh2, proj, yc, ya, conv_w, w_co, w_ao, w_mo, ride=None):
    t = dh2.shape[0]
    tm = min(TOKEN_TILE, t)

    def body(dh_ref, cb_ref, cc_ref, cx_ref, gc_ref, ga_ref, cch_ref, cxh_ref, yc_ref, ya_ref, cw_ref, wco_ref, wao_ref, wmo_ref,
             dhb_ref, dyc_ref, dya_ref, dgc_ref, dga_ref, dcb_ref, dcv_ref, do_ref):
        dhb = dh_ref[...].astype(BF)
        dhb_ref[...] = dhb
        dmerged = _dot_nt(dhb, wmo_ref[...])
        sc = jax.nn.sigmoid(gc_ref[0].astype(F32))
        sa = jax.nn.sigmoid(ga_ref[0].astype(F32))
        dyc = (dmerged * sc).astype(BF)
        dya = (dmerged * sa).astype(BF)
        dyc_ref[...] = dyc
        dya_ref[...] = dya
        dgc_ref[...] = (dmerged * yc_ref[...].astype(F32) * sc * (1.0 - sc)).astype(BF)
        dga_ref[...] = (dmerged * ya_ref[...].astype(F32) * sa * (1.0 - sa)).astype(BF)
        m, m1, m2 = _conv_inputs(cc_ref, cx_ref, cch_ref, cxh_ref)
        cw = cw_ref[...]
        cv = cw[0:1, :] * m2 + cw[1:2, :] * m1 + cw[2:3, :] * m
        dycin = _dot_nt(dyc, wco_ref[...])
        dcb_ref[...] = (dycin * cv).astype(BF)
        dcv_ref[...] = (dycin * cb_ref[0].astype(F32)).astype(BF)
        do_ref[...] = _dot_nt(dya, wao_ref[...]).astype(BF)

    sq = (D_MODEL, D_MODEL)
    return _pallas(
        body, name="mixer_bwd", grid=(t // tm,), ride=ride,
        in_specs=[_rows(tm, D_MODEL), _piece(0, tm), _piece(1, tm), _piece(2, tm), _piece(6, tm), _piece(7, tm),
                  _prev_halo(1, tm), _prev_halo(2, tm), _rows(tm, D_MODEL), _rows(tm, D_MODEL),
                  _const_spec((3, D_MODEL)), _const_spec(sq), _const_spec(sq), _const_spec(sq)],
        out_specs=[_rows(tm, D_MODEL)] * 8,
        out_shape=[S((t, D_MODEL), BF)] * 8,
        args=[dh2, proj, proj, proj, proj, proj, proj, proj, yc, ya, conv_w, w_co, w_ao, w_mo])


TAP_ROWS = 8


def _conv_bwd(dcv, proj, conv_w):
    t = dcv.shape[0]
    tm = min(TOKEN_TILE, t)
    steps = t // tm

    def body(dcv_ref, nxt_ref, cc_ref, cx_ref, cch_ref, cxh_ref, cw_ref, dcc_ref, dcx_ref, dw_ref):
        i = pl.program_id(0)
        m, m1, m2 = _conv_inputs(cc_ref, cx_ref, cch_ref, cxh_ref)
        d0 = dcv_ref[...].astype(F32)
        nxt = jnp.where(i == steps - 1, 0.0, nxt_ref[...].astype(F32))
        row = lax.broadcasted_iota(jnp.int32, (tm, 1), 0)
        d1 = jnp.where(row == tm - 1, nxt[0:1, :], pltpu.roll(d0, tm - 1, 0))
        d2 = pltpu.roll(d0, tm - 2, 0)
        d2 = jnp.where(row == tm - 2, nxt[0:1, :], jnp.where(row == tm - 1, nxt[1:2, :], d2))
        cw = cw_ref[...]
        dm = cw[2:3, :] * d0 + cw[1:2, :] * d1 + cw[0:1, :] * d2
        dcc_ref[...] = (dm * cx_ref[0].astype(F32)).astype(BF)
        dcx_ref[...] = (dm * cc_ref[0].astype(F32)).astype(BF)
        tap_row = lax.broadcasted_iota(jnp.int32, (TAP_ROWS, 1), 0)
        dw = jnp.zeros((TAP_ROWS, D_MODEL), F32)
        for j, mk in enumerate((m2, m1, m)):
            dw = jnp.where(tap_row == j, jnp.sum(d0 * mk, axis=0, keepdims=True), dw)
        _accumulate(dw_ref, dw)

    nxt_spec = pl.BlockSpec((HALO, D_MODEL), lambda i: (jnp.minimum((i + 1) * (tm // HALO), t // HALO - 1), 0))
    return pl.pallas_call(
        body, name="conv_bwd", grid=(steps,),
        in_specs=[_rows(tm, D_MODEL), nxt_spec, _piece(1, tm), _piece(2, tm), _prev_halo(1, tm), _prev_halo(2, tm),
                  _const_spec((3, D_MODEL))],
        out_specs=[_rows(tm, D_MODEL), _rows(tm, D_MODEL), _acc_spec((TAP_ROWS, D_MODEL))],
        out_shape=[S((t, D_MODEL), BF), S((t, D_MODEL), BF), S((TAP_ROWS, D_MODEL), F32)],
    )(dcv, dcv, proj, proj, proj, proj, conv_w)


def _mix_bwd(dpieces, w_mix, h1, dh2, g, ride=None):
    t = h1.shape[0]
    tm = min(TOKEN_TILE, t)

    def body(*refs):
        pieces, (w_ref, h_ref, dh_ref, g_ref, dhi_ref, dg_ref) = refs[:N_MIX], refs[N_MIX:]
        du = jnp.zeros((tm, D_MODEL), F32)
        for d in range(N_MIX):
            du = du + _dot_nt(pieces[d][...], w_ref[d])
        dhi, dg = _rms_bwd(du, h_ref[...], g_ref[...])
        _accumulate(dg_ref, dg)
        dhi_ref[...] = dh_ref[...] + dhi

    vec = (1, D_MODEL)
    return _pallas(
        body, name="mix_bwd", grid=(t // tm,), ride=ride,
        in_specs=[_rows(tm, D_MODEL)] * N_MIX + [_const_spec(w_mix.shape), _rows(tm, D_MODEL), _rows(tm, D_MODEL), _const_spec(vec)],
        out_specs=[_rows(tm, D_MODEL), _acc_spec(vec)],
        out_shape=[S((t, D_MODEL), F32), S(vec, F32)],
        args=[*dpieces, w_mix, h1, dh2, g])


def _adamw(partials, w, m, v, name):
    parts = list(partials) if isinstance(partials, (list, tuple)) else [partials]
    r, c = w.shape
    tr = next(d for d in (r, 512, 352, 256) if d <= 512 // len(parts) and r % d == 0)
    first_tile = [sum(p.shape[1] for p in parts[:j]) // tr for j in range(len(parts))]
    c1 = 1.0 - ADAM_B1 ** ADAM_STEP
    c2 = 1.0 - ADAM_B2 ** ADAM_STEP

    def body(*refs):
        p_refs, (w_ref, m_ref, v_ref, g_ref, d_ref, mo_ref, vo_ref) = refs[:len(parts)], refs[len(parts):]
        g = None
        for j, p_ref in enumerate(p_refs):
            gj = p_ref[0].astype(F32)
            for s in range(1, N_SHARDS):
                gj = gj + p_ref[s].astype(F32)
            g = gj if g is None else jnp.where(pl.program_id(0) >= first_tile[j], gj, g)
        mn = ADAM_B1 * m_ref[...] + (1.0 - ADAM_B1) * g
        vn = ADAM_B2 * v_ref[...] + (1.0 - ADAM_B2) * (g * g)
        g_ref[...] = g
        mo_ref[...] = mn
        vo_ref[...] = vn
        d_ref[...] = -ADAM_LR * ((mn / c1) / (jnp.sqrt(vn / c2) + ADAM_EPS) + ADAM_WD * w_ref[...])

    def rows_of(j):
        last = parts[j].shape[1] // tr - 1
        return lambda i: (0, jnp.clip(i - first_tile[j], 0, last), 0)

    blk = pl.BlockSpec((tr, c), lambda i: (i, 0))
    return pl.pallas_call(
        body, name=name, grid=(r // tr,),
        in_specs=[pl.BlockSpec((N_SHARDS, tr, c), rows_of(j)) for j in range(len(parts))] + [blk, blk, blk],
        out_specs=[blk] * 4, out_shape=[S((r, c), F32)] * 4,
    )(*parts, w, m, v)


_MATRICES = ("ffn1_w_in", "ffn1_w_out", "w_mix_in", "conv_w", "w_conv_out", "w_attn_out", "w_mix_out",
             "ffn2_w_in", "ffn2_w_out", "w_ple_gate", "w_ple_proj")
_GAINS = ("ffn1_norm", "mix_norm", "ffn2_norm", "ple_norm", "final_norm")
_WEIGHTS = ("ffn1_norm", "ffn1_w_in", "ffn1_w_out", "mix_norm", "w_mix_in", "conv_w", "w_conv_out", "w_attn_out", "w_mix_out",
            "ffn2_norm", "ffn2_w_in", "ffn2_w_out", "ple_norm", "w_ple_gate", "w_ple_proj", "final_norm")
CONV_ROWS = 8
_TRANSPOSED = ("ffn1_w_in", "ffn2_w_in")


def _columns_from_shards(g):
    return jnp.transpose(g, (1, 0, 2)).reshape(g.shape[1], N_SHARDS * g.shape[2])


def _shards_from_columns(a):
    r, c = a.shape
    return jnp.transpose(a.reshape(r, N_SHARDS, c // N_SHARDS), (1, 0, 2))


def kernel(x, p, ffn1_norm, ffn1_w_in, ffn1_w_out, mix_norm, w_mix_in, conv_w, w_conv_out, w_attn_out, w_mix_out, ffn2_norm, ffn2_w_in, ffn2_w_out, ple_norm, w_ple_gate, w_ple_proj, final_norm, loss_target, m_ffn1_norm, m_ffn1_w_in, m_ffn1_w_out, m_mix_norm, m_w_mix_in, m_conv_w, m_w_conv_out, m_w_attn_out, m_w_mix_out, m_ffn2_norm, m_ffn2_w_in, m_ffn2_w_out, m_ple_norm, m_w_ple_gate, m_w_ple_proj, m_final_norm, v_ffn1_norm, v_ffn1_w_in, v_ffn1_w_out, v_mix_norm, v_w_mix_in, v_conv_w, v_w_conv_out, v_w_attn_out, v_w_mix_out, v_ffn2_norm, v_ffn2_w_in, v_ffn2_w_out, v_ple_norm, v_w_ple_gate, v_w_ple_proj, v_final_norm):
    given = dict(locals())
    t = x.shape[1]
    xs = x.reshape(t, D_MODEL)
    ps = p.reshape(t, PLE_DIM)
    target = loss_target.reshape(t, D_MODEL)
    shard = {k: given[k].reshape(given[k].shape[-2:]) for k in _MATRICES}
    gain = {k: given[k].reshape(1, D_MODEL) for k in _GAINS}

    send = {k: shard[k].astype(BF) for k in _MATRICES}
    send["conv_w"] = jnp.pad(shard["conv_w"], ((0, CONV_ROWS - 3), (0, 0)))
    loss_vec, dx, landed, gain_grads = _forward_backward(xs, ps, target, gain, send)
    gain_rows = jnp.concatenate([gain_grads[k] for k in _GAINS] + [loss_vec, jnp.zeros((8 - len(_GAINS) - 1, D_MODEL), F32)], axis=0)
    gain_parts, = _exchange_alone("gather", [gain_rows], "gather_gain_gradients")

    out = {}
    for k in _MATRICES:
        w, m, v = shard[k], given["m_" + k].reshape(shard[k].shape), given["v_" + k].reshape(shard[k].shape)
        part = landed[k]
        if k == "conv_w":
            pad = ((0, CONV_ROWS - 3), (0, 0))
            w, m, v = jnp.pad(w, pad), jnp.pad(m, pad), jnp.pad(v, pad, constant_values=1.0)
        if k in _TRANSPOSED:
            w, m, v = w.T, m.T, v.T
        res = _adamw(part, w, m, v, "adamw_" + k)
        out[k] = [r[:3] if k == "conv_w" else (r.T if k in _TRANSPOSED else r) for r in res]
    stack = lambda pre: jnp.concatenate([given[pre + k].reshape(1, D_MODEL) for k in _GAINS] + [jnp.ones((8 - len(_GAINS), D_MODEL), F32)], axis=0)
    res = _adamw(gain_parts, stack(""), stack("m_"), stack("v_"), "adamw_gains")
    for j, k in enumerate(_GAINS):
        out[k] = [r[j:j + 1] for r in res]

    loss = jnp.sum(gain_parts[:, len(_GAINS), 0])
    per_kind = [[out[k][j].reshape(given[k].shape) for k in _WEIGHTS] for j in range(4)]
    return (loss, dx.reshape(x.shape), *per_kind[0], *per_kind[1], *per_kind[2], *per_kind[3])


def _forward_backward(xs, ps, target, gain, send, full=None):
    exchange = full is None
    full = dict(full or {})
    grads, landed = {}, {}

    def gather(names):
        return ("gather", [send[k] for k in names]) if exchange else None

    def scatter(names):
        return ("scatter", [grads[k] for k in names]) if exchange else None

    def keep(into, names, got):
        into.update(zip(names, got))

    first = ("ffn1_w_in",)
    (n1,), got = _prenorm(xs, gain["ffn1_norm"], ride=gather(first))
    keep(full, first, got)
    w1_in = full["ffn1_w_in"]
    second = ("ffn1_w_out", "w_mix_in")
    (act1, to_gate1, to_up1), got = _ffn_up(n1, w1_in, "ffn1_up", ride=gather(second))
    keep(full, second, got)
    w1_out = full["ffn1_w_out"].reshape(N_FF_CHUNKS, FF_CHUNK, D_MODEL)
    third = ("conv_w", "w_conv_out", "w_attn_out", "w_mix_out")
    (h1, u), got = _ffn_down(xs, act1, w1_out, gain["mix_norm"], "ffn1_down", ride=gather(third))
    keep(full, third, got)
    w_mix = full["w_mix_in"]
    w_co, w_ao, w_mo = (full[k].reshape(D_MODEL, D_MODEL) for k in ("w_conv_out", "w_attn_out", "w_mix_out"))
    taps = _columns_from_shards(full["conv_w"][:, :3, :])
    rest = ("ffn2_w_in", "ffn2_w_out", "w_ple_gate", "w_ple_proj")
    (proj,), got = _mix_proj(u, w_mix, ride=gather(rest))
    keep(full, rest, got)
    w2_in, w2_out = full["ffn2_w_in"], full["ffn2_w_out"].reshape(N_FF_CHUNKS, FF_CHUNK, D_MODEL)
    w_pg = full["w_ple_gate"].reshape(D_MODEL, D_MODEL)
    w_pp = _columns_from_shards(full["w_ple_proj"])
    o, o_bf, a_first, beta_first, reach = _attn_fwd(proj)
    h2, n3, ycin, yc, ya, merged = _mixer_out(proj, o_bf, h1, taps, w_co, w_ao, w_mo, gain["ffn2_norm"])
    (act2, to_gate2, to_up2), _ = _ffn_up(n3, w2_in, "ffn2_up")
    (h3, n4), _ = _ffn_down(h2, act2, w2_out, gain["ple_norm"], "ffn2_down")
    dh3, ds, dpp, loss_vec, dg_final, dg_ple = _tail(h3, n4, ps, w_pg, w_pp, gain["ple_norm"], gain["final_norm"], target)

    one = lambda a: a[None]
    by_rows = lambda g, rows: g.reshape(N_SHARDS, rows // N_SHARDS, D_MODEL)
    square = WGRAD_TILE // 2
    grads["w_ple_gate"] = by_rows(_wgrad(one(n4), one(ds), "wgrad_ple_gate", tile=square), D_MODEL)
    grads["w_ple_proj"] = _shards_from_columns(_wgrad(one(ps), one(dpp), "wgrad_ple_proj")[0])
    ple = ("w_ple_gate", "w_ple_proj")
    (df2, dgate2, dup2), got = _ffn_bwd_hidden(dh3, to_gate2, to_up2, w2_out, "ffn2_bwd_hidden", ride=scatter(ple))
    keep(landed, ple, got)
    grads["ffn2_w_out"] = by_rows(_wgrad(act2, one(df2), "wgrad_ffn2_out"), D_FF)
    grads["ffn2_w_in"] = _wgrad_pieces(n3, [dgate2, dup2], "wgrad_ffn2_in", transposed=True)
    (dh2, dg_ffn2), got = _ffn_bwd_input(dh3, h2, gain["ffn2_norm"], dgate2, dup2, w2_in, "ffn2_bwd_input", ride=scatter(("ffn2_w_out",)))
    keep(landed, ("ffn2_w_out",), got)
    (dh2b, dyc, dya, dgc, dga, dcb, dcv, d_o), _ = _mixer_bwd(dh2, proj, yc, ya, taps, w_co, w_ao, w_mo)
    grads["w_mix_out"] = by_rows(_wgrad(one(merged), one(dh2b), "wgrad_mix_out", tile=square), D_MODEL)
    grads["w_conv_out"] = by_rows(_wgrad(one(ycin), one(dyc), "wgrad_conv_out", tile=square), D_MODEL)
    grads["w_attn_out"] = by_rows(_wgrad(one(o_bf), one(dya), "wgrad_attn_out", tile=square), D_MODEL)
    dcc, dcx, dtaps = _conv_bwd(dcv, proj, taps)
    grads["conv_w"] = jnp.pad(_shards_from_columns(dtaps[:3]), ((0, 0), (0, CONV_ROWS - 3), (0, 0)))
    behind_attn = ("ffn2_w_in", "w_mix_out", "w_conv_out", "w_attn_out", "conv_w")
    (dq, dk, dv), got = _attn_bwd(proj, o, d_o, a_first, beta_first, reach, ride=scatter(behind_attn))
    keep(landed, behind_attn, got)
    dpieces = [dcb, dcc, dcx, dq, dk, dv, dgc, dga]
    half = N_MIX // 2
    tops, bottoms = zip(_wgrad_pieces(u, [one(dp) for dp in dpieces[:half]], "wgrad_mix_in_a", tile=WGRAD_TILE // 2, row_parts=2),
                        _wgrad_pieces(u, [one(dp) for dp in dpieces[half:]], "wgrad_mix_in_b", tile=WGRAD_TILE // 2, row_parts=2))
    grads["w_mix_in top"], grads["w_mix_in bottom"] = jnp.concatenate(tops, axis=0), jnp.concatenate(bottoms, axis=0)
    (dh1, dg_mix), top = _mix_bwd(dpieces, w_mix, h1, dh2, gain["mix_norm"], ride=scatter(("w_mix_in top",)))
    (df1, dgate1, dup1), bottom = _ffn_bwd_hidden(dh1, to_gate1, to_up1, w1_out, "ffn1_bwd_hidden", ride=scatter(("w_mix_in bottom",)))
    if exchange:
        landed["w_mix_in"] = [top[0], bottom[0]]
    else:
        grads["w_mix_in"] = jnp.concatenate([grads.pop("w_mix_in top"), grads.pop("w_mix_in bottom")], axis=1)
    grads["ffn1_w_out"] = by_rows(_wgrad(act1, one(df1), "wgrad_ffn1_out"), D_FF)
    if exchange:
        grads["ffn1_w_in"], got = _wgrad_pieces(n1, [dgate1, dup1], "wgrad_ffn1_in", transposed=True, ride=scatter(("ffn1_w_out",)))
        keep(landed, ("ffn1_w_out",), got)
    else:
        grads["ffn1_w_in"] = _wgrad_pieces(n1, [dgate1, dup1], "wgrad_ffn1_in", transposed=True)
    (dx, dg_ffn1), got = _ffn_bwd_input(dh1, xs, gain["ffn1_norm"], dgate1, dup1, w1_in, "ffn1_bwd_input", ride=scatter(("ffn1_w_in",)))
    keep(landed, ("ffn1_w_in",), got)
    gain_grads = dict(ffn1_norm=dg_ffn1, mix_norm=dg_mix, ffn2_norm=dg_ffn2, ple_norm=dg_ple, final_norm=dg_final)
    return loss_vec, dx, (landed if exchange else grads), gain_grads
```

```python
import functools
import math

import jax
import jax.numpy as jnp
from jax import lax
from jax.experimental import pallas as pl
from jax.experimental.pallas import tpu as pltpu

D_MODEL = 1024
D_FF = 2816
N_SHARDS = 8
FF_CHUNK = 2 * D_FF // N_SHARDS
N_FF_CHUNKS = D_FF // FF_CHUNK
N_HEADS = 8
HEAD_DIM = 128
PLE_DIM = 256
NORM_EPS = 1e-6
N_MIX = 8
ADAM_LR, ADAM_B1, ADAM_B2, ADAM_EPS, ADAM_WD, ADAM_STEP = 0.001, 0.9, 0.999, 1e-08, 0.01, 10

TOKEN_TILE = 512
WGRAD_TILE = 4096
PROJ_TILE = 2048
ATTN_ROWS = 1024
ATTN_Q = 64
ATTN_SUB = 128
ATTN_K = 2 * ATTN_SUB
ATTN_SKIP_BELOW = -90.0

BF = jnp.bfloat16
F32 = jnp.float32
MESH = pl.DeviceIdType.MESH
NT = (((1,), (1,)), ((), ()))
TN = (((0,), (0,)), ((), ()))
S = jax.ShapeDtypeStruct
ANY = pl.BlockSpec(memory_space=pl.ANY)


def _const_spec(shape):
    nd = len(shape)
    return pl.BlockSpec(shape, lambda *_: (0,) * nd, pipeline_mode=pl.Buffered(1))


def _rows(tm, cols):
    return pl.BlockSpec((tm, cols), lambda i: (i, 0))


def _chunks(tm):
    return pl.BlockSpec((N_FF_CHUNKS, tm, FF_CHUNK), lambda i: (0, i, 0))


def _acc_spec(shape):
    nd = len(shape)
    return pl.BlockSpec(shape, lambda *_: (0,) * nd)


def _dot(a, b):
    return jnp.dot(a, b, preferred_element_type=F32)


def _dot_nt(a, b):
    return lax.dot_general(a, b, NT, preferred_element_type=F32)


def _dot_tn(a, b):
    return lax.dot_general(a, b, TN, preferred_element_type=F32)


def _rms(h, g):
    r = lax.rsqrt(jnp.mean(h * h, axis=-1, keepdims=True) + NORM_EPS)
    return h * r * g


def _rms_bwd(dn, h, g):
    r = lax.rsqrt(jnp.mean(h * h, axis=-1, keepdims=True) + NORM_EPS)
    nh = h * r
    gd = dn * g
    dh = r * (gd - nh * jnp.mean(gd * nh, axis=-1, keepdims=True))
    return dh, jnp.sum(dn * nh, axis=0, keepdims=True)


def _accumulate(ref, val):
    @pl.when(pl.program_id(0) == 0)
    def _():
        ref[...] = jnp.zeros_like(ref)
    ref[...] += val


def _place():
    x, y, c = lax.axis_index("x"), lax.axis_index("y"), lax.axis_index("c")
    return x, y, c


def _slot(px, py, pc):
    return 4 * px + 2 * py + pc


def _gather_phases(ins, outs, send_sems, recv_sems, local_sems):
    n = len(ins)

    def parties():
        x, y, c = _place()
        return (x, y, c), (x, y, 1 - c), [(1 - x, y), (x, 1 - y), (1 - x, 1 - y)], c

    def copy(a, k, block, to, src=None):
        dst = outs[a].at[_slot(*block)]
        return pltpu.make_async_remote_copy(
            src_ref=dst if src is None else src, dst_ref=dst,
            send_sem=send_sems.at[a, k], recv_sem=recv_sems.at[a, k],
            device_id=to, device_id_type=MESH)

    def own(a, me):
        return pltpu.make_async_copy(ins[a], outs[a].at[_slot(*me)], local_sems.at[a])

    def first(a, me, sibling, chips, c):
        return [copy(a, 0, me, sibling, src=ins[a])] + [copy(a, 1 + j, me, (*chip, c), src=ins[a]) for j, chip in enumerate(chips)]

    def start():
        me, sibling, chips, c = parties()
        for a in range(n):
            own(a, me).start()
        for a in range(n):
            for cp in first(a, me, sibling, chips, c):
                cp.start()

    def forward():
        me, sibling, chips, c = parties()
        for j, chip in enumerate(chips):
            for a in range(n):
                copy(a, 1 + j, (*chip, c), me).wait_recv()
                copy(a, 4 + j, (*chip, c), sibling).start()

    def finish():
        me, sibling, chips, c = parties()
        for a in range(n):
            copy(a, 0, sibling, me).wait_recv()
            for j, chip in enumerate(chips):
                copy(a, 4 + j, (*chip, 1 - c), me).wait_recv()
        for a in range(n):
            for cp in first(a, me, sibling, chips, c) + [copy(a, 4 + j, (*chip, c), sibling) for j, chip in enumerate(chips)]:
                cp.wait_send()
            own(a, me).wait()

    return [start, forward, finish]


def _scatter_phases(ins, outs, send_sems, recv_sems, local_sems):
    n = len(ins)

    def copies():
        x, y, c = _place()
        me = _slot(x, y, c)
        out = [pltpu.make_async_copy(ins[a].at[me], outs[a].at[me], local_sems.at[a]) for a in range(n)]
        for k in range(1, N_SHARDS):
            px = 1 - x if k & 4 else x
            py = 1 - y if k & 2 else y
            pc = 1 - c if k & 1 else c
            for a in range(n):
                out.append(pltpu.make_async_remote_copy(
                    src_ref=ins[a].at[_slot(px, py, pc)], dst_ref=outs[a].at[me],
                    send_sem=send_sems.at[a, k - 1], recv_sem=recv_sems.at[a, k - 1],
                    device_id=(px, py, pc), device_id_type=MESH))
        return out

    def start():
        for cp in copies():
            cp.start()

    def finish():
        for cp in copies():
            cp.wait()

    return [start, finish]


def _pallas(body, *, name, grid, in_specs, out_specs, out_shape, args, scratch_shapes=(), ride=None):
    if ride is None:
        outs = pl.pallas_call(body, name=name, grid=grid, in_specs=in_specs, out_specs=out_specs, out_shape=out_shape,
                              scratch_shapes=list(scratch_shapes))(*args)
        return list(outs), []
    kind, arrays = ride
    n, n_in, n_out, n_scr = len(arrays), len(in_specs), len(out_specs), len(scratch_shapes)
    total = math.prod(grid)
    middle = (9 * total) // 10
    landed_shape = [S((N_SHARDS,) + a.shape if kind == "gather" else a.shape, a.dtype) for a in arrays]

    def with_exchange(*refs):
        ins, riders_in = refs[:n_in], refs[n_in:n_in + n]
        outs, riders_out = refs[n_in + n:n_in + n + n_out], refs[n_in + n + n_out:n_in + 2 * n + n_out]
        scratch, sems = refs[n_in + 2 * n + n_out:n_in + 2 * n + n_out + n_scr], refs[n_in + 2 * n + n_out + n_scr:]
        step = 0
        for axis, size in enumerate(grid):
            step = step * size + pl.program_id(axis)
        phases = (_gather_phases if kind == "gather" else _scatter_phases)(riders_in, riders_out, *sems)
        pl.when(step == 0)(phases[0])
        body(*ins, *outs, *scratch)
        for phase in phases[1:-1]:
            pl.when(step == middle)(phase)
        pl.when(step == total - 1)(phases[-1])

    outs = pl.pallas_call(
        with_exchange, name=name, grid=grid,
        in_specs=list(in_specs) + [ANY] * n, out_specs=list(out_specs) + [ANY] * n,
        out_shape=list(out_shape) + landed_shape,
        scratch_shapes=list(scratch_shapes) + [pltpu.SemaphoreType.DMA((n, 7)), pltpu.SemaphoreType.DMA((n, 7)),
                                               pltpu.SemaphoreType.DMA((n,))],
    )(*args, *arrays)
    return list(outs[:n_out]), list(outs[n_out:])


def _exchange_alone(kind, arrays, name):
    return _pallas(lambda: None, name=name, grid=(1,), in_specs=[], out_specs=[], out_shape=[], args=[], ride=(kind, arrays))[1]


def _prenorm(x, g, ride=None):
    t = x.shape[0]
    tm = min(TOKEN_TILE, t)

    def body(x_ref, g_ref, n_ref):
        n_ref[...] = _rms(x_ref[...], g_ref[...]).astype(BF)

    return _pallas(
        body, name="prenorm", grid=(t // tm,), ride=ride,
        in_specs=[_rows(tm, D_MODEL), _const_spec((1, D_MODEL))], out_specs=[_rows(tm, D_MODEL)],
        out_shape=[S((t, D_MODEL), BF)], args=[x, g])


def _ffn_up(n, w_in, name, ride=None):
    t = n.shape[0]
    tm = min(TOKEN_TILE, t)

    def body(n_ref, win_ref, act_ref, to_gate_ref, to_up_ref):
        nb = n_ref[...]
        for c in range(N_FF_CHUNKS):
            gate = _dot(nb, win_ref[c])
            up = _dot(nb, win_ref[N_FF_CHUNKS + c])
            sg = jax.nn.sigmoid(gate)
            silu = gate * sg
            act_ref[c] = (silu * up).astype(BF)
            to_gate_ref[c] = (up * (sg * (1.0 + gate * (1.0 - sg)))).astype(BF)
            to_up_ref[c] = silu.astype(BF)

    return _pallas(
        body, name=name, grid=(t // tm,), ride=ride,
        in_specs=[_rows(tm, D_MODEL), _const_spec(w_in.shape)],
        out_specs=[_chunks(tm)] * 3, out_shape=[S((N_FF_CHUNKS, t, FF_CHUNK), BF)] * 3,
        args=[n, w_in])


def _ffn_down(h, act, w_out, g_next, name, ride=None):
    t = h.shape[0]
    tm = min(TOKEN_TILE, t)

    def body(h_ref, act_ref, wout_ref, g_ref, ho_ref, no_ref):
        acc = jnp.zeros((tm, D_MODEL), F32)
        for c in range(N_FF_CHUNKS):
            acc = acc + _dot(act_ref[c], wout_ref[c])
        ho = h_ref[...] + 0.5 * acc
        ho_ref[...] = ho
        no_ref[...] = _rms(ho, g_ref[...]).astype(BF)

    return _pallas(
        body, name=name, grid=(t // tm,), ride=ride,
        in_specs=[_rows(tm, D_MODEL), _chunks(tm), _const_spec(w_out.shape), _const_spec((1, D_MODEL))],
        out_specs=[_rows(tm, D_MODEL)] * 2, out_shape=[S((t, D_MODEL), F32), S((t, D_MODEL), BF)],
        args=[h, act, w_out, g_next])


def _mix_proj(u, w_mix, ride=None):
    t = u.shape[0]
    tm = min(PROJ_TILE, t)

    def body(u_ref, w_ref, o_ref):
        o_ref[0] = _dot(u_ref[...], w_ref[0]).astype(BF)

    return _pallas(
        body, name="mix_proj", grid=(N_MIX, t // tm), ride=ride,
        in_specs=[pl.BlockSpec((tm, D_MODEL), lambda d, i: (i, 0)), pl.BlockSpec((1, D_MODEL, D_MODEL), lambda d, i: (d, 0, 0))],
        out_specs=[pl.BlockSpec((1, tm, D_MODEL), lambda d, i: (d, i, 0))],
        out_shape=[S((N_MIX, t, D_MODEL), BF)], args=[u, w_mix])


HALO = 16


def _piece(d, tm):
    return pl.BlockSpec((1, tm, D_MODEL), lambda i: (d, i, 0))


def _prev_halo(d, tm):
    return pl.BlockSpec((1, HALO, D_MODEL), lambda i: (d, jnp.maximum(i * (tm // HALO) - 1, 0), 0))


def _shift_down(m, prev_tail, k):
    tm = m.shape[0]
    out = pltpu.roll(m, k, 0)
    row = lax.broadcasted_iota(jnp.int32, (tm, 1), 0)
    for j in range(k):
        out = jnp.where(row == j, prev_tail[HALO - k + j:HALO - k + j + 1, :], out)
    return out


def _conv_inputs(cc_ref, cx_ref, cch_ref, cxh_ref):
    m = cc_ref[0].astype(F32) * cx_ref[0].astype(F32)
    mh = cch_ref[0].astype(F32) * cxh_ref[0].astype(F32)
    mh = jnp.where(pl.program_id(0) == 0, 0.0, mh)
    return m, _shift_down(m, mh, 1), _shift_down(m, mh, 2)


def _mixer_out(proj, o, h1, conv_w, w_co, w_ao, w_mo, g_next):
    t = h1.shape[0]
    tm = min(TOKEN_TILE, t)

    def body(cb_ref, cc_ref, cx_ref, gc_ref, ga_ref, cch_ref, cxh_ref, o_ref, h_ref, cw_ref, wco_ref, wao_ref, wmo_ref,
             g_ref, ho_ref, no_ref, ycin_ref, yc_ref, ya_ref, mg_ref):
        m, m1, m2 = _conv_inputs(cc_ref, cx_ref, cch_ref, cxh_ref)
        cw = cw_ref[...]
        cv = cw[0:1, :] * m2 + cw[1:2, :] * m1 + cw[2:3, :] * m
        ycin = (cb_ref[0].astype(F32) * cv).astype(BF)
        ycin_ref[...] = ycin
        yc = _dot(ycin, wco_ref[...])
        ya = _dot(o_ref[...], wao_ref[...])
        yc_ref[...] = yc.astype(BF)
        ya_ref[...] = ya.astype(BF)
        merged = (jax.nn.sigmoid(gc_ref[0].astype(F32)) * yc + jax.nn.sigmoid(ga_ref[0].astype(F32)) * ya).astype(BF)
        mg_ref[...] = merged
        ho = h_ref[...] + _dot(merged, wmo_ref[...])
        ho_ref[...] = ho
        no_ref[...] = _rms(ho, g_ref[...]).astype(BF)

    sq = (D_MODEL, D_MODEL)
    return pl.pallas_call(
        body, name="mixer_out", grid=(t // tm,),
        in_specs=[_piece(0, tm), _piece(1, tm), _piece(2, tm), _piece(6, tm), _piece(7, tm), _prev_halo(1, tm), _prev_halo(2, tm),
                  _rows(tm, D_MODEL), _rows(tm, D_MODEL), _const_spec((3, D_MODEL)), _const_spec(sq), _const_spec(sq),
                  _const_spec(sq), _const_spec((1, D_MODEL))],
        out_specs=[_rows(tm, D_MODEL)] * 6,
        out_shape=[S((t, D_MODEL), F32)] + [S((t, D_MODEL), BF)] * 5,
    )(proj, proj, proj, proj, proj, proj, proj, o, h1, conv_w, w_co, w_ao, w_mo, g_next)


def _suffix_sums(vals, tri, before):
    out, right = [], before
    for b in reversed(range(ATTN_K // ATTN_SUB)):
        v = vals[:, b * ATTN_SUB:(b + 1) * ATTN_SUB]
        hi = v.astype(BF)
        lo = (v - hi.astype(F32)).astype(BF)
        out.append(_dot(hi, tri) + _dot(lo, tri) + right)
        right = right + jnp.sum(v, axis=1, keepdims=True)
    return jnp.concatenate(out[::-1], axis=1), right


ATTN_UNITS = ATTN_ROWS // ATTN_Q


def _unit_rows(x, u):
    return x[u * ATTN_Q:(u + 1) * ATTN_Q]


def _per_unit(fn):
    return jnp.concatenate([fn(u) for u in range(ATTN_UNITS)], axis=0)


def _per_row(vals):
    local = lax.broadcasted_iota(jnp.int32, (ATTN_ROWS, 1), 0)
    out = jnp.full((ATTN_ROWS, 1), vals[0], jnp.int32)
    for u in range(1, ATTN_UNITS):
        out = jnp.where(local >= u * ATTN_Q, vals[u], out)
    return out


def _attn_step(q, k_ref, starts, bounds, row):
    z = _per_unit(lambda u: _dot_nt(_unit_rows(q, u), k_ref[0, pl.ds(starts[u], ATTN_K), :])) * (1.0 / math.sqrt(HEAD_DIM))
    mask = lax.broadcasted_iota(jnp.int32, (1, ATTN_K), 1) < jnp.minimum(row, _per_row(bounds)) - _per_row(starts)
    log_beta = jnp.minimum(z, 0.0) - jnp.log(1.0 + jnp.exp(jnp.minimum(z, -z)))
    log_rest = jnp.where(mask, log_beta - z, 0.0)
    return z, mask, log_beta, log_rest


def _attn_sweep_start(i, t):
    blks = tuple(jnp.maximum(i * ATTN_UNITS + u + 1 - ATTN_K // ATTN_Q, 0) for u in range(ATTN_UNITS))
    return blks, tuple(jnp.int32(t) for _ in range(ATTN_UNITS))


def _attn_keys(blks):
    return [pl.multiple_of(b * ATTN_Q, ATTN_Q) for b in blks]


def _attn_next(blks):
    return tuple(jnp.maximum(b - ATTN_K // ATTN_Q, 0) for b in blks), tuple(b * ATTN_Q for b in blks)


def _attn_more(carry):
    return jnp.logical_and(carry[1][ATTN_UNITS - 1] > 0, carry[-1] > ATTN_SKIP_BELOW)


def _tri(strict):
    r = lax.broadcasted_iota(jnp.int32, (ATTN_SUB, ATTN_SUB), 0)
    c = lax.broadcasted_iota(jnp.int32, (ATTN_SUB, ATTN_SUB), 1)
    return (r > c if strict else r >= c).astype(BF)


REACH_TILE = (8, 128)


def _first_step_spec():
    return pl.BlockSpec((1, ATTN_ROWS, ATTN_K), lambda h, i: (h, i, 0))


def _reach_spec():
    return pl.BlockSpec((1, 1) + REACH_TILE, lambda h, i: (h, i, 0, 0))


def _head_cols(piece):
    return lambda t: pl.BlockSpec((1, t, HEAD_DIM), lambda h, i: (piece, 0, h))


def _attn_fwd(proj):
    t = proj.shape[1]
    nq = t // ATTN_ROWS
    tri = _tri(strict=True)

    def body(q_ref, k_ref, v_ref, tri_ref, o_ref, ob_ref, a_ref, beta_ref, reach_ref):
        i = pl.program_id(1)
        q = q_ref[0]
        row = i * ATTN_ROWS + lax.broadcasted_iota(jnp.int32, (ATTN_ROWS, 1), 0)

        def step(carry, keep=False):
            blks, bounds, acc, run, _ = carry
            starts = _attn_keys(blks)
            _, mask, log_beta, log_rest = _attn_step(q, k_ref, starts, bounds, row)
            tail, run = _suffix_sums(log_rest, tri_ref[...], run)
            a = jnp.where(mask, jnp.exp(log_beta + tail), 0.0).astype(BF)
            if keep:
                a_ref[0] = a
                beta_ref[0] = jnp.where(mask, jnp.exp(log_beta), 0.0).astype(BF)
            acc = acc + _per_unit(lambda u: _dot(_unit_rows(a, u), v_ref[0, pl.ds(starts[u], ATTN_K), :]))
            return (*_attn_next(blks), acc, run, jnp.max(run))

        first = (*_attn_sweep_start(i, t), jnp.zeros((ATTN_ROWS, HEAD_DIM), F32), jnp.zeros((ATTN_ROWS, 1), F32), jnp.float32(0.0))
        after_first = step(first, keep=True)
        reach_ref[...] = jnp.full(reach_ref.shape, after_first[-1], F32)
        o = lax.while_loop(_attn_more, step, after_first)[2]
        o_ref[...] = o
        ob_ref[...] = o.astype(BF)

    qspec = pl.BlockSpec((1, ATTN_ROWS, HEAD_DIM), lambda h, i: (3, i, h))
    rowblk = pl.BlockSpec((ATTN_ROWS, HEAD_DIM), lambda h, i: (i, h))
    return pl.pallas_call(
        body, name="attn_fwd", grid=(N_HEADS, nq),
        in_specs=[qspec, _head_cols(4)(t), _head_cols(5)(t), pl.BlockSpec((ATTN_SUB, ATTN_SUB), lambda h, i: (0, 0))],
        out_specs=[rowblk, rowblk, _first_step_spec(), _first_step_spec(), _reach_spec()],
        out_shape=[S((t, D_MODEL), F32), S((t, D_MODEL), BF), S((N_HEADS, t, ATTN_K), BF), S((N_HEADS, t, ATTN_K), BF),
                   S((N_HEADS, nq) + REACH_TILE, F32)],
    )(proj, proj, proj, tri)


def _attn_bwd(proj, o, d_o, a_first, beta_first, reach, ride=None):
    t = proj.shape[1]
    nq = t // ATTN_ROWS
    tri_strict, tri_incl = _tri(strict=True), _tri(strict=False)
    scale = 1.0 / math.sqrt(HEAD_DIM)

    def body(q_ref, k_ref, v_ref, o_ref, do_ref, a_ref, beta_ref, reach_ref, tris_ref, trii_ref, dq_ref, dk_ref, dv_ref, dk_acc, dv_acc):
        i = pl.program_id(1)

        @pl.when(i == 0)
        def _():
            dk_acc[...] = jnp.zeros_like(dk_acc)
            dv_acc[...] = jnp.zeros_like(dv_acc)

        q = q_ref[0]
        do = do_ref[...]
        total = jnp.sum(do.astype(F32) * o_ref[...], axis=1, keepdims=True)
        zero = jnp.zeros((ATTN_ROWS, 1), F32)
        blks0, bounds0 = _attn_sweep_start(i, t)

        def finish(starts, a, dz, dq):
            dzb = (dz * scale).astype(BF)
            for u in range(ATTN_UNITS):
                dv_acc[pl.ds(starts[u], ATTN_K), :] += _dot_tn(_unit_rows(a, u), _unit_rows(do, u))
                dk_acc[pl.ds(starts[u], ATTN_K), :] += _dot_tn(_unit_rows(dzb, u), _unit_rows(q, u))
            return dq + _per_unit(lambda u: _dot(_unit_rows(dzb, u), k_ref[0, pl.ds(starts[u], ATTN_K), :]))

        def grad_a(starts, a):
            return _per_unit(lambda u: _dot_nt(_unit_rows(do, u), v_ref[0, pl.ds(starts[u], ATTN_K), :])) * a.astype(F32)

        one_step = jnp.max(reach_ref[...]) <= ATTN_SKIP_BELOW

        @pl.when(one_step)
        def _():
            starts = _attn_keys(blks0)
            a = a_ref[0]
            beta = beta_ref[0].astype(F32)
            de = grad_a(starts, a)
            right, _ = _suffix_sums(de, trii_ref[...], zero)
            dz = de * (1.0 - beta) - (total - right) * beta
            dq_ref[...] = finish(starts, a, dz, jnp.zeros((ATTN_ROWS, HEAD_DIM), F32)).astype(BF)

        @pl.when(jnp.logical_not(one_step))
        def _():
            row = i * ATTN_ROWS + lax.broadcasted_iota(jnp.int32, (ATTN_ROWS, 1), 0)

            def step(carry):
                blks, bounds, dq, seen, run, _ = carry
                starts = _attn_keys(blks)
                z, mask, log_beta, log_rest = _attn_step(q, k_ref, starts, bounds, row)
                tail, run = _suffix_sums(log_rest, tris_ref[...], run)
                a = jnp.where(mask, jnp.exp(log_beta + tail), 0.0).astype(BF)
                de = grad_a(starts, a)
                right, seen = _suffix_sums(de, trii_ref[...], seen)
                beta = jax.nn.sigmoid(z)
                dz = jnp.where(mask, de * (1.0 - beta) - (total - right) * beta, 0.0)
                return (*_attn_next(blks), finish(starts, a, dz, dq), seen, run, jnp.max(run))

            first = (blks0, bounds0, jnp.zeros((ATTN_ROWS, HEAD_DIM), F32), zero, zero, jnp.float32(0.0))
            dq_ref[...] = lax.while_loop(_attn_more, step, step(first))[2].astype(BF)

        @pl.when(i == nq - 1)
        def _():
            dk_ref[...] = dk_acc[...].astype(BF)
            dv_ref[...] = dv_acc[...].astype(BF)

    qspec = pl.BlockSpec((1, ATTN_ROWS, HEAD_DIM), lambda h, i: (3, i, h))
    rowblk = pl.BlockSpec((ATTN_ROWS, HEAD_DIM), lambda h, i: (i, h))
    head = pl.BlockSpec((t, HEAD_DIM), lambda h, i: (0, h))
    trispec = pl.BlockSpec((ATTN_SUB, ATTN_SUB), lambda h, i: (0, 0))
    return _pallas(
        body, name="attn_bwd", grid=(N_HEADS, nq), ride=ride,
        in_specs=[qspec, _head_cols(4)(t), _head_cols(5)(t), rowblk, rowblk, _first_step_spec(), _first_step_spec(), _reach_spec(),
                  trispec, trispec],
        out_specs=[rowblk, head, head],
        out_shape=[S((t, D_MODEL), BF)] * 3,
        scratch_shapes=[pltpu.VMEM((t, HEAD_DIM), F32), pltpu.VMEM((t, HEAD_DIM), F32)],
        args=[proj, proj, proj, o, d_o, a_first, beta_first, reach, tri_strict, tri_incl])


def _tail(h3, n4, p, w_pg, w_pp, g_ple, g_final, target):
    t = h3.shape[0]
    tm = min(TOKEN_TILE, t)
    steps = t // tm

    def body(h_ref, n_ref, p_ref, wpg_ref, wpp_ref, gp_ref, gf_ref, tgt_ref,
             dh_ref, ds_ref, dpp_ref, loss_ref, dgf_ref, dgp_ref):
        pg = jax.nn.sigmoid(_dot(n_ref[...], wpg_ref[...]))
        pp = _dot(p_ref[...].astype(BF), wpp_ref[...])
        h3v = h_ref[...]
        h4 = h3v + pg * pp
        gf = gf_ref[...]
        diff = _rms(h4, gf) - tgt_ref[...]
        _accumulate(loss_ref, jnp.sum(diff * diff, axis=0, keepdims=True))
        dh4, dgf = _rms_bwd(diff * (1.0 / D_MODEL), h4, gf)
        _accumulate(dgf_ref, dgf)
        dpp_ref[...] = (dh4 * pg).astype(BF)
        ds = (dh4 * pp * pg * (1.0 - pg)).astype(BF)
        ds_ref[...] = ds
        dh3, dgp = _rms_bwd(_dot_nt(ds, wpg_ref[...]), h3v, gp_ref[...])
        _accumulate(dgp_ref, dgp)
        dh_ref[...] = dh4 + dh3

        @pl.when(pl.program_id(0) == steps - 1)
        def _():
            loss_ref[...] = jnp.full(loss_ref.shape, 0.5 / D_MODEL * jnp.sum(loss_ref[...]), F32)

    vec = (1, D_MODEL)
    return pl.pallas_call(
        body, name="tail", grid=(steps,),
        in_specs=[_rows(tm, D_MODEL), _rows(tm, D_MODEL), _rows(tm, PLE_DIM), _const_spec((D_MODEL, D_MODEL)),
                  _const_spec((PLE_DIM, D_MODEL)), _const_spec(vec), _const_spec(vec), _rows(tm, D_MODEL)],
        out_specs=[_rows(tm, D_MODEL)] * 3 + [_acc_spec(vec)] * 3,
        out_shape=[S((t, D_MODEL), F32), S((t, D_MODEL), BF), S((t, D_MODEL), BF)] + [S(vec, F32)] * 3,
    )(h3, n4, p, w_pg, w_pp, g_ple, g_final, target)


def _wgrad(xs, ys, name, ride=None, tile=None):
    bx, t, k = xs.shape
    by, _, n = ys.shape
    b = max(bx, by)
    tt = min(tile or WGRAD_TILE * 2 // xs.dtype.itemsize, t)
    steps = t // tt

    def body(x_ref, y_ref, o_ref, acc_ref):
        s = pl.program_id(1)

        @pl.when(s == 0)
        def _():
            acc_ref[...] = jnp.zeros_like(acc_ref)
        acc_ref[...] += _dot_tn(x_ref[0].astype(BF), y_ref[0].astype(BF))

        @pl.when(s == steps - 1)
        def _():
            o_ref[0] = acc_ref[...].astype(BF)

    (out,), landed = _pallas(
        body, name=name, grid=(b, steps), ride=ride,
        in_specs=[pl.BlockSpec((1, tt, k), (lambda j, s: (j, s, 0)) if bx > 1 else (lambda j, s: (0, s, 0))),
                  pl.BlockSpec((1, tt, n), (lambda j, s: (j, s, 0)) if by > 1 else (lambda j, s: (0, s, 0)))],
        out_specs=[pl.BlockSpec((1, k, n), lambda j, s: (j, 0, 0))],
        out_shape=[S((b, k, n), BF)],
        scratch_shapes=[pltpu.VMEM((k, n), F32)],
        args=[xs, ys])
    return (out, landed) if ride is not None else out


def _wgrad_pieces(x, ys, name, ride=None, tile=None, row_parts=1, transposed=False):
    t, k = x.shape
    n = ys[0].shape[2]
    counts = [y.shape[0] for y in ys]
    offsets = [sum(counts[:j]) for j in range(len(ys))]
    total = sum(counts)
    tt = min(tile or WGRAD_TILE, t)
    steps = t // tt
    rows, cols = (n, k) if transposed else (k, n)
    kp = rows // row_parts

    def body(x_ref, *refs):
        y_refs, o_refs, acc_ref = refs[:len(ys)], refs[len(ys):len(ys) + row_parts], refs[len(ys) + row_parts]
        p, s = pl.program_id(0), pl.program_id(1)

        @pl.when(s == 0)
        def _():
            acc_ref[...] = jnp.zeros_like(acc_ref)
        for j, y_ref in enumerate(y_refs):
            @pl.when(jnp.logical_and(p >= offsets[j], p < offsets[j] + counts[j]))
            def _(y_ref=y_ref):
                acc_ref[...] += _dot_tn(y_ref[0], x_ref[...]) if transposed else _dot_tn(x_ref[...], y_ref[0])

        @pl.when(s == steps - 1)
        def _():
            for part, o_ref in enumerate(o_refs):
                o_ref[0] = acc_ref[part * kp:(part + 1) * kp, :].astype(BF)

    def turn(j):
        lo, hi = offsets[j], offsets[j] + counts[j]
        return lambda p, s: (jnp.clip(p - lo, 0, counts[j] - 1), jnp.where(p < lo, 0, jnp.where(p >= hi, steps - 1, s)), 0)

    outs, landed = _pallas(
        body, name=name, grid=(total, steps), ride=ride,
        in_specs=[pl.BlockSpec((tt, k), lambda p, s: (s, 0))] + [pl.BlockSpec((1, tt, n), turn(j)) for j in range(len(ys))],
        out_specs=[pl.BlockSpec((1, kp, cols), lambda p, s: (p, 0, 0))] * row_parts,
        out_shape=[S((total, kp, cols), BF)] * row_parts,
        scratch_shapes=[pltpu.VMEM((rows, cols), F32)],
        args=[x, *ys])
    out = outs[0] if row_parts == 1 else outs
    return (out, landed) if ride is not None else out


def _ffn_bwd_hidden(dh, to_gate, to_up, w_out, name, ride=None):
    t = dh.shape[0]
    tm = min(TOKEN_TILE, t)

    def body(dh_ref, to_gate_ref, to_up_ref, wout_ref, df_ref, dgate_ref, dup_ref):
        df = (0.5 * dh_ref[...]).astype(BF)
        df_ref[...] = df
        for c in range(N_FF_CHUNKS):
            dact = _dot_nt(df, wout_ref[c])
            dgate_ref[c] = (dact * to_gate_ref[c].astype(F32)).astype(BF)
            dup_ref[c] = (dact * to_up_ref[c].astype(F32)).astype(BF)

    return _pallas(
        body, name=name, grid=(t // tm,), ride=ride,
        in_specs=[_rows(tm, D_MODEL), _chunks(tm), _chunks(tm), _const_spec(w_out.shape)],
        out_specs=[_rows(tm, D_MODEL), _chunks(tm), _chunks(tm)],
        out_shape=[S((t, D_MODEL), BF)] + [S((N_FF_CHUNKS, t, FF_CHUNK), BF)] * 2,
        args=[dh, to_gate, to_up, w_out])


def _ffn_bwd_input(dh, h_in, g, dgate, dup, w_in, name, ride=None):
    t = dh.shape[0]
    tm = min(TOKEN_TILE, t)

    def body(dh_ref, h_ref, g_ref, dgate_ref, dup_ref, win_ref, dhi_ref, dg_ref):
        dn = jnp.zeros((tm, D_MODEL), F32)
        for c in range(N_FF_CHUNKS):
            dn = dn + _dot_nt(dgate_ref[c], win_ref[c]) + _dot_nt(dup_ref[c], win_ref[N_FF_CHUNKS + c])
        dhi, dg = _rms_bwd(dn, h_ref[...], g_ref[...])
        _accumulate(dg_ref, dg)
        dhi_ref[...] = dh_ref[...] + dhi

    vec = (1, D_MODEL)
    return _pallas(
        body, name=name, grid=(t // tm,), ride=ride,
        in_specs=[_rows(tm, D_MODEL), _rows(tm, D_MODEL), _const_spec(vec), _chunks(tm), _chunks(tm), _const_spec(w_in.shape)],
        out_specs=[_rows(tm, D_MODEL), _acc_spec(vec)],
        out_shape=[S((t, D_MODEL), F32), S(vec, F32)],
        args=[dh, h_in, g, dgate, dup, w_in])


def _mixer_bwd(dh2, proj, yc, ya, conv_w, w_co, w_ao, w_mo, ride=None):
    t = dh2.shape[0]
    tm = min(TOKEN_TILE, t)

    def body(dh_ref, cb_ref, cc_ref, cx_ref, gc_ref, ga_ref, cch_ref, cxh_ref, yc_ref, ya_ref, cw_ref, wco_ref, wao_ref, wmo_ref,
             dhb_ref, dyc_ref, dya_ref, dgc_ref, dga_ref, dcb_ref, dcv_ref, do_ref):
        dhb = dh_ref[...].astype(BF)
        dhb_ref[...] = dhb
        dmerged = _dot_nt(dhb, wmo_ref[...])
        sc = jax.nn.sigmoid(gc_ref[0].astype(F32))
        sa = jax.nn.sigmoid(ga_ref[0].astype(F32))
        dyc = (dmerged * sc).astype(BF)
        dya = (dmerged * sa).astype(BF)
        dyc_ref[...] = dyc
        dya_ref[...] = dya
        dgc_ref[...] = (dmerged * yc_ref[...].astype(F32) * sc * (1.0 - sc)).astype(BF)
        dga_ref[...] = (dmerged * ya_ref[...].astype(F32) * sa * (1.0 - sa)).astype(BF)
        m, m1, m2 = _conv_inputs(cc_ref, cx_ref, cch_ref, cxh_ref)
        cw = cw_ref[...]
        cv = cw[0:1, :] * m2 + cw[1:2, :] * m1 + cw[2:3, :] * m
        dycin = _dot_nt(dyc, wco_ref[...])
        dcb_ref[...] = (dycin * cv).astype(BF)
        dcv_ref[...] = (dycin * cb_ref[0].astype(F32)).astype(BF)
        do_ref[...] = _dot_nt(dya, wao_ref[...]).astype(BF)

    sq = (D_MODEL, D_MODEL)
    return _pallas(
        body, name="mixer_bwd", grid=(t // tm,), ride=ride,
        in_specs=[_rows(tm, D_MODEL), _piece(0, tm), _piece(1, tm), _piece(2, tm), _piece(6, tm), _piece(7, tm),
                  _prev_halo(1, tm), _prev_halo(2, tm), _rows(tm, D_MODEL), _rows(tm, D_MODEL),
                  _const_spec((3, D_MODEL)), _const_spec(sq), _const_spec(sq), _const_spec(sq)],
        out_specs=[_rows(tm, D_MODEL)] * 8,
        out_shape=[S((t, D_MODEL), BF)] * 8,
        args=[dh2, proj, proj, proj, proj, proj, proj, proj, yc, ya, conv_w, w_co, w_ao, w_mo])


TAP_ROWS = 8


def _conv_bwd(dcv, proj, conv_w):
    t = dcv.shape[0]
    tm = min(TOKEN_TILE, t)
    steps = t // tm

    def body(dcv_ref, nxt_ref, cc_ref, cx_ref, cch_ref, cxh_ref, cw_ref, dcc_ref, dcx_ref, dw_ref):
        i = pl.program_id(0)
        m, m1, m2 = _conv_inputs(cc_ref, cx_ref, cch_ref, cxh_ref)
        d0 = dcv_ref[...].astype(F32)
        nxt = jnp.where(i == steps - 1, 0.0, nxt_ref[...].astype(F32))
        row = lax.broadcasted_iota(jnp.int32, (tm, 1), 0)
        d1 = jnp.where(row == tm - 1, nxt[0:1, :], pltpu.roll(d0, tm - 1, 0))
        d2 = pltpu.roll(d0, tm - 2, 0)
        d2 = jnp.where(row == tm - 2, nxt[0:1, :], jnp.where(row == tm - 1, nxt[1:2, :], d2))
        cw = cw_ref[...]
        dm = cw[2:3, :] * d0 + cw[1:2, :] * d1 + cw[0:1, :] * d2
        dcc_ref[...] = (dm * cx_ref[0].astype(F32)).astype(BF)
        dcx_ref[...] = (dm * cc_ref[0].astype(F32)).astype(BF)
        tap_row = lax.broadcasted_iota(jnp.int32, (TAP_ROWS, 1), 0)
        dw = jnp.zeros((TAP_ROWS, D_MODEL), F32)
        for j, mk in enumerate((m2, m1, m)):
            dw = jnp.where(tap_row == j, jnp.sum(d0 * mk, axis=0, keepdims=True), dw)
        _accumulate(dw_ref, dw)

    nxt_spec = pl.BlockSpec((HALO, D_MODEL), lambda i: (jnp.minimum((i + 1) * (tm // HALO), t // HALO - 1), 0))
    return pl.pallas_call(
        body, name="conv_bwd", grid=(steps,),
        in_specs=[_rows(tm, D_MODEL), nxt_spec, _piece(1, tm), _piece(2, tm), _prev_halo(1, tm), _prev_halo(2, tm),
                  _const_spec((3, D_MODEL))],
        out_specs=[_rows(tm, D_MODEL), _rows(tm, D_MODEL), _acc_spec((TAP_ROWS, D_MODEL))],
        out_shape=[S((t, D_MODEL), BF), S((t, D_MODEL), BF), S((TAP_ROWS, D_MODEL), F32)],
    )(dcv, dcv, proj, proj, proj, proj, conv_w)


def _mix_bwd(dpieces, w_mix, h1, dh2, g, ride=None):
    t = h1.shape[0]
    tm = min(TOKEN_TILE, t)

    def body(*refs):
        pieces, (w_ref, h_ref, dh_ref, g_ref, dhi_ref, dg_ref) = refs[:N_MIX], refs[N_MIX:]
        du = jnp.zeros((tm, D_MODEL), F32)
        for d in range(N_MIX):
            du = du + _dot_nt(pieces[d][...], w_ref[d])
        dhi, dg = _rms_bwd(du, h_ref[...], g_ref[...])
        _accumulate(dg_ref, dg)
        dhi_ref[...] = dh_ref[...] + dhi

    vec = (1, D_MODEL)
    return _pallas(
        body, name="mix_bwd", grid=(t // tm,), ride=ride,
        in_specs=[_rows(tm, D_MODEL)] * N_MIX + [_const_spec(w_mix.shape), _rows(tm, D_MODEL), _rows(tm, D_MODEL), _const_spec(vec)],
        out_specs=[_rows(tm, D_MODEL), _acc_spec(vec)],
        out_shape=[S((t, D_MODEL), F32), S(vec, F32)],
        args=[*dpieces, w_mix, h1, dh2, g])


def _adamw(partials, w, m, v, name):
    parts = list(partials) if isinstance(partials, (list, tuple)) else [partials]
    r, c = w.shape
    tr = next(d for d in (r, 512, 352, 256) if d <= 512 // len(parts) and r % d == 0)
    first_tile = [sum(p.shape[1] for p in parts[:j]) // tr for j in range(len(parts))]
    c1 = 1.0 - ADAM_B1 ** ADAM_STEP
    c2 = 1.0 - ADAM_B2 ** ADAM_STEP

    def body(*refs):
        p_refs, (w_ref, m_ref, v_ref, g_ref, d_ref, mo_ref, vo_ref) = refs[:len(parts)], refs[len(parts):]
        g = None
        for j, p_ref in enumerate(p_refs):
            gj = p_ref[0].astype(F32)
            for s in range(1, N_SHARDS):
                gj = gj + p_ref[s].astype(F32)
            g = gj if g is None else jnp.where(pl.program_id(0) >= first_tile[j], gj, g)
        mn = ADAM_B1 * m_ref[...] + (1.0 - ADAM_B1) * g
        vn = ADAM_B2 * v_ref[...] + (1.0 - ADAM_B2) * (g * g)
        g_ref[...] = g
        mo_ref[...] = mn
        vo_ref[...] = vn
        d_ref[...] = -ADAM_LR * ((mn / c1) / (jnp.sqrt(vn / c2) + ADAM_EPS) + ADAM_WD * w_ref[...])

    def rows_of(j):
        last = parts[j].shape[1] // tr - 1
        return lambda i: (0, jnp.clip(i - first_tile[j], 0, last), 0)

    blk = pl.BlockSpec((tr, c), lambda i: (i, 0))
    return pl.pallas_call(
        body, name=name, grid=(r // tr,),
        in_specs=[pl.BlockSpec((N_SHARDS, tr, c), rows_of(j)) for j in range(len(parts))] + [blk, blk, blk],
        out_specs=[blk] * 4, out_shape=[S((r, c), F32)] * 4,
    )(*parts, w, m, v)


_MATRICES = ("ffn1_w_in", "ffn1_w_out", "w_mix_in", "conv_w", "w_conv_out", "w_attn_out", "w_mix_out",
             "ffn2_w_in", "ffn2_w_out", "w_ple_gate", "w_ple_proj")
_GAINS = ("ffn1_norm", "mix_norm", "ffn2_norm", "ple_norm", "final_norm")
_WEIGHTS = ("ffn1_norm", "ffn1_w_in", "ffn1_w_out", "mix_norm", "w_mix_in", "conv_w", "w_conv_out", "w_attn_out", "w_mix_out",
            "ffn2_norm", "ffn2_w_in", "ffn2_w_out", "ple_norm", "w_ple_gate", "w_ple_proj", "final_norm")
CONV_ROWS = 8
_TRANSPOSED = ("ffn1_w_in", "ffn2_w_in")


def _columns_from_shards(g):
    return jnp.transpose(g, (1, 0, 2)).reshape(g.shape[1], N_SHARDS * g.shape[2])


def _shards_from_columns(a):
    r, c = a.shape
    return jnp.transpose(a.reshape(r, N_SHARDS, c // N_SHARDS), (1, 0, 2))


def kernel(x, p, ffn1_norm, ffn1_w_in, ffn1_w_out, mix_norm, w_mix_in, conv_w, w_conv_out, w_attn_out, w_mix_out, ffn2_norm, ffn2_w_in, ffn2_w_out, ple_norm, w_ple_gate, w_ple_proj, final_norm, loss_target, m_ffn1_norm, m_ffn1_w_in, m_ffn1_w_out, m_mix_norm, m_w_mix_in, m_conv_w, m_w_conv_out, m_w_attn_out, m_w_mix_out, m_ffn2_norm, m_ffn2_w_in, m_ffn2_w_out, m_ple_norm, m_w_ple_gate, m_w_ple_proj, m_final_norm, v_ffn1_norm, v_ffn1_w_in, v_ffn1_w_out, v_mix_norm, v_w_mix_in, v_conv_w, v_w_conv_out, v_w_attn_out, v_w_mix_out, v_ffn2_norm, v_ffn2_w_in, v_ffn2_w_out, v_ple_norm, v_w_ple_gate, v_w_ple_proj, v_final_norm):
    given = dict(locals())
    t = x.shape[1]
    xs = x.reshape(t, D_MODEL)
    ps = p.reshape(t, PLE_DIM)
    target = loss_target.reshape(t, D_MODEL)
    shard = {k: given[k].reshape(given[k].shape[-2:]) for k in _MATRICES}
    gain = {k: given[k].reshape(1, D_MODEL) for k in _GAINS}

    send = {k: shard[k].astype(BF) for k in _MATRICES}
    send["conv_w"] = jnp.pad(shard["conv_w"], ((0, CONV_ROWS - 3), (0, 0)))
    loss_vec, dx, landed, gain_grads = _forward_backward(xs, ps, target, gain, send)
    gain_rows = jnp.concatenate([gain_grads[k] for k in _GAINS] + [loss_vec, jnp.zeros((8 - len(_GAINS) - 1, D_MODEL), F32)], axis=0)
    gain_parts, = _exchange_alone("gather", [gain_rows], "gather_gain_gradients")

    out = {}
    for k in _MATRICES:
        w, m, v = shard[k], given["m_" + k].reshape(shard[k].shape), given["v_" + k].reshape(shard[k].shape)
        part = landed[k]
        if k == "conv_w":
            pad = ((0, CONV_ROWS - 3), (0, 0))
            w, m, v = jnp.pad(w, pad), jnp.pad(m, pad), jnp.pad(v, pad, constant_values=1.0)
        if k in _TRANSPOSED:
            w, m, v = w.T, m.T, v.T
        res = _adamw(part, w, m, v, "adamw_" + k)
        out[k] = [r[:3] if k == "conv_w" else (r.T if k in _TRANSPOSED else r) for r in res]
    stack = lambda pre: jnp.concatenate([given[pre + k].reshape(1, D_MODEL) for k in _GAINS] + [jnp.ones((8 - len(_GAINS), D_MODEL), F32)], axis=0)
    res = _adamw(gain_parts, stack(""), stack("m_"), stack("v_"), "adamw_gains")
    for j, k in enumerate(_GAINS):
        out[k] = [r[j:j + 1] for r in res]

    loss = jnp.sum(gain_parts[:, len(_GAINS), 0])
    per_kind = [[out[k][j].reshape(given[k].shape) for k in _WEIGHTS] for j in range(4)]
    return (loss, dx.reshape(x.shape), *per_kind[0], *per_kind[1], *per_kind[2], *per_kind[3])


def _forward_backward(xs, ps, target, gain, send, full=None):
    exchange = full is None
    full = dict(full or {})
    grads, landed = {}, {}

    def gather(names):
        return ("gather", [send[k] for k in names]) if exchange else None

    def scatter(names):
        return ("scatter", [grads[k] for k in names]) if exchange else None

    def keep(into, names, got):
        into.update(zip(names, got))

    first = ("ffn1_w_in",)
    (n1,), got = _prenorm(xs, gain["ffn1_norm"], ride=gather(first))
    keep(full, first, got)
    w1_in = full["ffn1_w_in"]
    second = ("ffn1_w_out", "w_mix_in")
    (act1, to_gate1, to_up1), got = _ffn_up(n1, w1_in, "ffn1_up", ride=gather(second))
    keep(full, second, got)
    w1_out = full["ffn1_w_out"].reshape(N_FF_CHUNKS, FF_CHUNK, D_MODEL)
    third = ("conv_w", "w_conv_out", "w_attn_out", "w_mix_out")
    (h1, u), got = _ffn_down(xs, act1, w1_out, gain["mix_norm"], "ffn1_down", ride=gather(third))
    keep(full, third, got)
    w_mix = full["w_mix_in"]
    w_co, w_ao, w_mo = (full[k].reshape(D_MODEL, D_MODEL) for k in ("w_conv_out", "w_attn_out", "w_mix_out"))
    taps = _columns_from_shards(full["conv_w"][:, :3, :])
    rest = ("ffn2_w_in", "ffn2_w_out", "w_ple_gate", "w_ple_proj")
    (proj,), got = _mix_proj(u, w_mix, ride=gather(rest))
    keep(full, rest, got)
    w2_in, w2_out = full["ffn2_w_in"], full["ffn2_w_out"].reshape(N_FF_CHUNKS, FF_CHUNK, D_MODEL)
    w_pg = full["w_ple_gate"].reshape(D_MODEL, D_MODEL)
    w_pp = _columns_from_shards(full["w_ple_proj"])
    o, o_bf, a_first, beta_first, reach = _attn_fwd(proj)
    h2, n3, ycin, yc, ya, merged = _mixer_out(proj, o_bf, h1, taps, w_co, w_ao, w_mo, gain["ffn2_norm"])
    (act2, to_gate2, to_up2), _ = _ffn_up(n3, w2_in, "ffn2_up")
    (h3, n4), _ = _ffn_down(h2, act2, w2_out, gain["ple_norm"], "ffn2_down")
    dh3, ds, dpp, loss_vec, dg_final, dg_ple = _tail(h3, n4, ps, w_pg, w_pp, gain["ple_norm"], gain["final_norm"], target)

    one = lambda a: a[None]
    by_rows = lambda g, rows: g.reshape(N_SHARDS, rows // N_SHARDS, D_MODEL)
    square = WGRAD_TILE // 2
    grads["w_ple_gate"] = by_rows(_wgrad(one(n4), one(ds), "wgrad_ple_gate", tile=square), D_MODEL)
    grads["w_ple_proj"] = _shards_from_columns(_wgrad(one(ps), one(dpp), "wgrad_ple_proj")[0])
    ple = ("w_ple_gate", "w_ple_proj")
    (df2, dgate2, dup2), got = _ffn_bwd_hidden(dh3, to_gate2, to_up2, w2_out, "ffn2_bwd_hidden", ride=scatter(ple))
    keep(landed, ple, got)
    grads["ffn2_w_out"] = by_rows(_wgrad(act2, one(df2), "wgrad_ffn2_out"), D_FF)
    grads["ffn2_w_in"] = _wgrad_pieces(n3, [dgate2, dup2], "wgrad_ffn2_in", transposed=True)
    (dh2, dg_ffn2), got = _ffn_bwd_input(dh3, h2, gain["ffn2_norm"], dgate2, dup2, w2_in, "ffn2_bwd_input", ride=scatter(("ffn2_w_out",)))
    keep(landed, ("ffn2_w_out",), got)
    (dh2b, dyc, dya, dgc, dga, dcb, dcv, d_o), _ = _mixer_bwd(dh2, proj, yc, ya, taps, w_co, w_ao, w_mo)
    grads["w_mix_out"] = by_rows(_wgrad(one(merged), one(dh2b), "wgrad_mix_out", tile=square), D_MODEL)
    grads["w_conv_out"] = by_rows(_wgrad(one(ycin), one(dyc), "wgrad_conv_out", tile=square), D_MODEL)
    grads["w_attn_out"] = by_rows(_wgrad(one(o_bf), one(dya), "wgrad_attn_out", tile=square), D_MODEL)
    dcc, dcx, dtaps = _conv_bwd(dcv, proj, taps)
    grads["conv_w"] = jnp.pad(_shards_from_columns(dtaps[:3]), ((0, 0), (0, CONV_ROWS - 3), (0, 0)))
    behind_attn = ("ffn2_w_in", "w_mix_out", "w_conv_out", "w_attn_out", "conv_w")
    (dq, dk, dv), got = _attn_bwd(proj, o, d_o, a_first, beta_first, reach, ride=scatter(behind_attn))
    keep(landed, behind_attn, got)
    dpieces = [dcb, dcc, dcx, dq, dk, dv, dgc, dga]
    half = N_MIX // 2
    tops, bottoms = zip(_wgrad_pieces(u, [one(dp) for dp in dpieces[:half]], "wgrad_mix_in_a", tile=WGRAD_TILE // 2, row_parts=2),
                        _wgrad_pieces(u, [one(dp) for dp in dpieces[half:]], "wgrad_mix_in_b", tile=WGRAD_TILE // 2, row_parts=2))
    grads["w_mix_in top"], grads["w_mix_in bottom"] = jnp.concatenate(tops, axis=0), jnp.concatenate(bottoms, axis=0)
    (dh1, dg_mix), top = _mix_bwd(dpieces, w_mix, h1, dh2, gain["mix_norm"], ride=scatter(("w_mix_in top",)))
    (df1, dgate1, dup1), bottom = _ffn_bwd_hidden(dh1, to_gate1, to_up1, w1_out, "ffn1_bwd_hidden", ride=scatter(("w_mix_in bottom",)))
    if exchange:
        landed["w_mix_in"] = [top[0], bottom[0]]
    else:
        grads["w_mix_in"] = jnp.concatenate([grads.pop("w_mix_in top"), grads.pop("w_mix_in bottom")], axis=1)
    grads["ffn1_w_out"] = by_rows(_wgrad(act1, one(df1), "wgrad_ffn1_out"), D_FF)
    if exchange:
        grads["ffn1_w_in"], got = _wgrad_pieces(n1, [dgate1, dup1], "wgrad_ffn1_in", transposed=True, ride=scatter(("ffn1_w_out",)))
        keep(landed, ("ffn1_w_out",), got)
    else:
        grads["ffn1_w_in"] = _wgrad_pieces(n1, [dgate1, dup1], "wgrad_ffn1_in", transposed=True)
    (dx, dg_ffn1), got = _ffn_bwd_input(dh1, xs, gain["ffn1_norm"], dgate1, dup1, w1_in, "ffn1_bwd_input", ride=scatter(("ffn1_w_in",)))
    keep(landed, ("ffn1_w_in",), got)
    gain_grads = dict(ffn1_norm=dg_ffn1, mix_norm=dg_mix, ffn2_norm=dg_ffn2, ple_norm=dg_ple, final_norm=dg_final)
    return loss_vec, dx, (landed if exchange else grads), gain_grads
```

```python
import functools
import math

import jax
import jax.numpy as jnp
from jax import lax
from jax.experimental import pallas as pl
from jax.experimental.pallas import tpu as pltpu

D_MODEL = 1024
D_FF = 2816
N_SHARDS = 8
FF_CHUNK = 2 * D_FF // N_SHARDS
N_FF_CHUNKS = D_FF // FF_CHUNK
N_HEADS = 8
HEAD_DIM = 128
PLE_DIM = 256
NORM_EPS = 1e-6
N_MIX = 8
ADAM_LR, ADAM_B1, ADAM_B2, ADAM_EPS, ADAM_WD, ADAM_STEP = 0.001, 0.9, 0.999, 1e-08, 0.01, 10

TOKEN_TILE = 512
WGRAD_TILE = 4096
PROJ_TILE = 2048
ATTN_ROWS = 1024
ATTN_Q = 64
ATTN_SUB = 128
ATTN_K = 2 * ATTN_SUB
ATTN_SKIP_BELOW = -90.0

BF = jnp.bfloat16
F32 = jnp.float32
MESH = pl.DeviceIdType.MESH
NT = (((1,), (1,)), ((), ()))
TN = (((0,), (0,)), ((), ()))
S = jax.ShapeDtypeStruct
ANY = pl.BlockSpec(memory_space=pl.ANY)


def _const_spec(shape):
    nd = len(shape)
    return pl.BlockSpec(shape, lambda *_: (0,) * nd, pipeline_mode=pl.Buffered(1))


def _rows(tm, cols):
    return pl.BlockSpec((tm, cols), lambda i: (i, 0))


def _chunks(tm):
    return pl.BlockSpec((N_FF_CHUNKS, tm, FF_CHUNK), lambda i: (0, i, 0))


def _acc_spec(shape):
    nd = len(shape)
    return pl.BlockSpec(shape, lambda *_: (0,) * nd)


def _dot(a, b):
    return jnp.dot(a, b, preferred_element_type=F32)


def _dot_nt(a, b):
    return lax.dot_general(a, b, NT, preferred_element_type=F32)


def _dot_tn(a, b):
    return lax.dot_general(a, b, TN, preferred_element_type=F32)


def _rms(h, g):
    r = lax.rsqrt(jnp.mean(h * h, axis=-1, keepdims=True) + NORM_EPS)
    return h * r * g


def _rms_bwd(dn, h, g):
    r = lax.rsqrt(jnp.mean(h * h, axis=-1, keepdims=True) + NORM_EPS)
    nh = h * r
    gd = dn * g
    dh = r * (gd - nh * jnp.mean(gd * nh, axis=-1, keepdims=True))
    return dh, jnp.sum(dn * nh, axis=0, keepdims=True)


def _accumulate(ref, val):
    @pl.when(pl.program_id(0) == 0)
    def _():
        ref[...] = jnp.zeros_like(ref)
    ref[...] += val


def _place():
    x, y, c = lax.axis_index("x"), lax.axis_index("y"), lax.axis_index("c")
    return x, y, c


def _slot(px, py, pc):
    return 4 * px + 2 * py + pc


def _gather_phases(ins, outs, send_sems, recv_sems, local_sems):
    n = len(ins)

    def parties():
        x, y, c = _place()
        return (x, y, c), (x, y, 1 - c), [(1 - x, y), (x, 1 - y), (1 - x, 1 - y)], c

    def copy(a, k, block, to, src=None):
        dst = outs[a].at[_slot(*block)]
        return pltpu.make_async_remote_copy(
            src_ref=dst if src is None else src, dst_ref=dst,
            send_sem=send_sems.at[a, k], recv_sem=recv_sems.at[a, k],
            device_id=to, device_id_type=MESH)

    def own(a, me):
        return pltpu.make_async_copy(ins[a], outs[a].at[_slot(*me)], local_sems.at[a])

    def first(a, me, sibling, chips, c):
        return [copy(a, 0, me, sibling, src=ins[a])] + [copy(a, 1 + j, me, (*chip, c), src=ins[a]) for j, chip in enumerate(chips)]

    def start():
        me, sibling, chips, c = parties()
        for a in range(n):
            own(a, me).start()
        for a in range(n):
            for cp in first(a, me, sibling, chips, c):
                cp.start()

    def forward():
        me, sibling, chips, c = parties()
        for j, chip in enumerate(chips):
            for a in range(n):
                copy(a, 1 + j, (*chip, c), me).wait_recv()
                copy(a, 4 + j, (*chip, c), sibling).start()

    def finish():
        me, sibling, chips, c = parties()
        for a in range(n):
            copy(a, 0, sibling, me).wait_recv()
            for j, chip in enumerate(chips):
                copy(a, 4 + j, (*chip, 1 - c), me).wait_recv()
        for a in range(n):
            for cp in first(a, me, sibling, chips, c) + [copy(a, 4 + j, (*chip, c), sibling) for j, chip in enumerate(chips)]:
                cp.wait_send()
            own(a, me).wait()

    return [start, forward, finish]


def _scatter_phases(ins, outs, send_sems, recv_sems, local_sems):
    n = len(ins)

    def copies():
        x, y, c = _place()
        me = _slot(x, y, c)
        out = [pltpu.make_async_copy(ins[a].at[me], outs[a].at[me], local_sems.at[a]) for a in range(n)]
        for k in range(1, N_SHARDS):
            px = 1 - x if k & 4 else x
            py = 1 - y if k & 2 else y
            pc = 1 - c if k & 1 else c
            for a in range(n):
                out.append(pltpu.make_async_remote_copy(
                    src_ref=ins[a].at[_slot(px, py, pc)], dst_ref=outs[a].at[me],
                    send_sem=send_sems.at[a, k - 1], recv_sem=recv_sems.at[a, k - 1],
                    device_id=(px, py, pc), device_id_type=MESH))
        return out

    def start():
        for cp in copies():
            cp.start()

    def finish():
        for cp in copies():
            cp.wait()

    return [start, finish]


def _pallas(body, *, name, grid, in_specs, out_specs, out_shape, args, scratch_shapes=(), ride=None):
    if ride is None:
        outs = pl.pallas_call(body, name=name, grid=grid, in_specs=in_specs, out_specs=out_specs, out_shape=out_shape,
                              scratch_shapes=list(scratch_shapes))(*args)
        return list(outs), []
    kind, arrays = ride
    n, n_in, n_out, n_scr = len(arrays), len(in_specs), len(out_specs), len(scratch_shapes)
    total = math.prod(grid)
    middle = (9 * total) // 10
    landed_shape = [S((N_SHARDS,) + a.shape if kind == "gather" else a.shape, a.dtype) for a in arrays]

    def with_exchange(*refs):
        ins, riders_in = refs[:n_in], refs[n_in:n_in + n]
        outs, riders_out = refs[n_in + n:n_in + n + n_out], refs[n_in + n + n_out:n_in + 2 * n + n_out]
        scratch, sems = refs[n_in + 2 * n + n_out:n_in + 2 * n + n_out + n_scr], refs[n_in + 2 * n + n_out + n_scr:]
        step = 0
        for axis, size in enumerate(grid):
            step = step * size + pl.program_id(axis)
        phases = (_gather_phases if kind == "gather" else _scatter_phases)(riders_in, riders_out, *sems)
        pl.when(step == 0)(phases[0])
        body(*ins, *outs, *scratch)
        for phase in phases[1:-1]:
            pl.when(step == middle)(phase)
        pl.when(step == total - 1)(phases[-1])

    outs = pl.pallas_call(
        with_exchange, name=name, grid=grid,
        in_specs=list(in_specs) + [ANY] * n, out_specs=list(out_specs) + [ANY] * n,
        out_shape=list(out_shape) + landed_shape,
        scratch_shapes=list(scratch_shapes) + [pltpu.SemaphoreType.DMA((n, 7)), pltpu.SemaphoreType.DMA((n, 7)),
                                               pltpu.SemaphoreType.DMA((n,))],
    )(*args, *arrays)
    return list(outs[:n_out]), list(outs[n_out:])


def _exchange_alone(kind, arrays, name):
    return _pallas(lambda: None, name=name, grid=(1,), in_specs=[], out_specs=[], out_shape=[], args=[], ride=(kind, arrays))[1]


def _prenorm(x, g, ride=None):
    t = x.shape[0]
    tm = min(TOKEN_TILE, t)

    def body(x_ref, g_ref, n_ref):
        n_ref[...] = _rms(x_ref[...], g_ref[...]).astype(BF)

    return _pallas(
        body, name="prenorm", grid=(t // tm,), ride=ride,
        in_specs=[_rows(tm, D_MODEL), _const_spec((1, D_MODEL))], out_specs=[_rows(tm, D_MODEL)],
        out_shape=[S((t, D_MODEL), BF)], args=[x, g])


def _ffn_up(n, w_in, name, ride=None):
    t = n.shape[0]
    tm = min(TOKEN_TILE, t)

    def body(n_ref, win_ref, act_ref, to_gate_ref, to_up_ref):
        nb = n_ref[...]
        for c in range(N_FF_CHUNKS):
            gate = _dot(nb, win_ref[c])
            up = _dot(nb, win_ref[N_FF_CHUNKS + c])
            sg = jax.nn.sigmoid(gate)
            silu = gate * sg
            act_ref[c] = (silu * up).astype(BF)
            to_gate_ref[c] = (up * (sg * (1.0 + gate * (1.0 - sg)))).astype(BF)
            to_up_ref[c] = silu.astype(BF)

    return _pallas(
        body, name=name, grid=(t // tm,), ride=ride,
        in_specs=[_rows(tm, D_MODEL), _const_spec(w_in.shape)],
        out_specs=[_chunks(tm)] * 3, out_shape=[S((N_FF_CHUNKS, t, FF_CHUNK), BF)] * 3,
        args=[n, w_in])


def _ffn_down(h, act, w_out, g_next, name, ride=None):
    t = h.shape[0]
    tm = min(TOKEN_TILE, t)

    def body(h_ref, act_ref, wout_ref, g_ref, ho_ref, no_ref):
        acc = jnp.zeros((tm, D_MODEL), F32)
        for c in range(N_FF_CHUNKS):
            acc = acc + _dot(act_ref[c], wout_ref[c])
        ho = h_ref[...] + 0.5 * acc
        ho_ref[...] = ho
        no_ref[...] = _rms(ho, g_ref[...]).astype(BF)

    return _pallas(
        body, name=name, grid=(t // tm,), ride=ride,
        in_specs=[_rows(tm, D_MODEL), _chunks(tm), _const_spec(w_out.shape), _const_spec((1, D_MODEL))],
        out_specs=[_rows(tm, D_MODEL)] * 2, out_shape=[S((t, D_MODEL), F32), S((t, D_MODEL), BF)],
        args=[h, act, w_out, g_next])


def _mix_proj(u, w_mix, ride=None):
    t = u.shape[0]
    tm = min(PROJ_TILE, t)

    def body(u_ref, w_ref, o_ref):
        o_ref[0] = _dot(u_ref[...], w_ref[0]).astype(BF)

    return _pallas(
        body, name="mix_proj", grid=(N_MIX, t // tm), ride=ride,
        in_specs=[pl.BlockSpec((tm, D_MODEL), lambda d, i: (i, 0)), pl.BlockSpec((1, D_MODEL, D_MODEL), lambda d, i: (d, 0, 0))],
        out_specs=[pl.BlockSpec((1, tm, D_MODEL), lambda d, i: (d, i, 0))],
        out_shape=[S((N_MIX, t, D_MODEL), BF)], args=[u, w_mix])


HALO = 16


def _piece(d, tm):
    return pl.BlockSpec((1, tm, D_MODEL), lambda i: (d, i, 0))


def _prev_halo(d, tm):
    return pl.BlockSpec((1, HALO, D_MODEL), lambda i: (d, jnp.maximum(i * (tm // HALO) - 1, 0), 0))


def _shift_down(m, prev_tail, k):
    tm = m.shape[0]
    out = pltpu.roll(m, k, 0)
    row = lax.broadcasted_iota(jnp.int32, (tm, 1), 0)
    for j in range(k):
        out = jnp.where(row == j, prev_tail[HALO - k + j:HALO - k + j + 1, :], out)
    return out


def _conv_inputs(cc_ref, cx_ref, cch_ref, cxh_ref):
    m = cc_ref[0].astype(F32) * cx_ref[0].astype(F32)
    mh = cch_ref[0].astype(F32) * cxh_ref[0].astype(F32)
    mh = jnp.where(pl.program_id(0) == 0, 0.0, mh)
    return m, _shift_down(m, mh, 1), _shift_down(m, mh, 2)


def _mixer_out(proj, o, h1, conv_w, w_co, w_ao, w_mo, g_next):
    t = h1.shape[0]
    tm = min(TOKEN_TILE, t)

    def body(cb_ref, cc_ref, cx_ref, gc_ref, ga_ref, cch_ref, cxh_ref, o_ref, h_ref, cw_ref, wco_ref, wao_ref, wmo_ref,
             g_ref, ho_ref, no_ref, ycin_ref, yc_ref, ya_ref, mg_ref):
        m, m1, m2 = _conv_inputs(cc_ref, cx_ref, cch_ref, cxh_ref)
        cw = cw_ref[...]
        cv = cw[0:1, :] * m2 + cw[1:2, :] * m1 + cw[2:3, :] * m
        ycin = (cb_ref[0].astype(F32) * cv).astype(BF)
        ycin_ref[...] = ycin
        yc = _dot(ycin, wco_ref[...])
        ya = _dot(o_ref[...], wao_ref[...])
        yc_ref[...] = yc.astype(BF)
        ya_ref[...] = ya.astype(BF)
        merged = (jax.nn.sigmoid(gc_ref[0].astype(F32)) * yc + jax.nn.sigmoid(ga_ref[0].astype(F32)) * ya).astype(BF)
        mg_ref[...] = merged
        ho = h_ref[...] + _dot(merged, wmo_ref[...])
        ho_ref[...] = ho
        no_ref[...] = _rms(ho, g_ref[...]).astype(BF)

    sq = (D_MODEL, D_MODEL)
    return pl.pallas_call(
        body, name="mixer_out", grid=(t // tm,),
        in_specs=[_piece(0, tm), _piece(1, tm), _piece(2, tm), _piece(6, tm), _piece(7, tm), _prev_halo(1, tm), _prev_halo(2, tm),
                  _rows(tm, D_MODEL), _rows(tm, D_MODEL), _const_spec((3, D_MODEL)), _const_spec(sq), _const_spec(sq),
                  _const_spec(sq), _const_spec((1, D_MODEL))],
        out_specs=[_rows(tm, D_MODEL)] * 6,
        out_shape=[S((t, D_MODEL), F32)] + [S((t, D_MODEL), BF)] * 5,
    )(proj, proj, proj, proj, proj, proj, proj, o, h1, conv_w, w_co, w_ao, w_mo, g_next)


def _suffix_sums(vals, tri, before):
    out, right = [], before
    for b in reversed(range(ATTN_K // ATTN_SUB)):
        v = vals[:, b * ATTN_SUB:(b + 1) * ATTN_SUB]
        hi = v.astype(BF)
        lo = (v - hi.astype(F32)).astype(BF)
        out.append(_dot(hi, tri) + _dot(lo, tri) + right)
        right = right + jnp.sum(v, axis=1, keepdims=True)
    return jnp.concatenate(out[::-1], axis=1), right


ATTN_UNITS = ATTN_ROWS // ATTN_Q


def _unit_rows(x, u):
    return x[u * ATTN_Q:(u + 1) * ATTN_Q]


def _per_unit(fn):
    return jnp.concatenate([fn(u) for u in range(ATTN_UNITS)], axis=0)


def _per_row(vals):
    local = lax.broadcasted_iota(jnp.int32, (ATTN_ROWS, 1), 0)
    out = jnp.full((ATTN_ROWS, 1), vals[0], jnp.int32)
    for u in range(1, ATTN_UNITS):
        out = jnp.where(local >= u * ATTN_Q, vals[u], out)
    return out


def _attn_step(q, k_ref, starts, bounds, row):
    z = _per_unit(lambda u: _dot_nt(_unit_rows(q, u), k_ref[0, pl.ds(starts[u], ATTN_K), :])) * (1.0 / math.sqrt(HEAD_DIM))
    mask = lax.broadcasted_iota(jnp.int32, (1, ATTN_K), 1) < jnp.minimum(row, _per_row(bounds)) - _per_row(starts)
    log_beta = jnp.minimum(z, 0.0) - jnp.log(1.0 + jnp.exp(jnp.minimum(z, -z)))
    log_rest = jnp.where(mask, log_beta - z, 0.0)
    return z, mask, log_beta, log_rest


def _attn_sweep_start(i, t):
    blks = tuple(jnp.maximum(i * ATTN_UNITS + u + 1 - ATTN_K // ATTN_Q, 0) for u in range(ATTN_UNITS))
    return blks, tuple(jnp.int32(t) for _ in range(ATTN_UNITS))


def _attn_keys(blks):
    return [pl.multiple_of(b * ATTN_Q, ATTN_Q) for b in blks]


def _attn_next(blks):
    return tuple(jnp.maximum(b - ATTN_K // ATTN_Q, 0) for b in blks), tuple(b * ATTN_Q for b in blks)


def _attn_reach(run, blks):
    done_rows = sum(jnp.where(b > 0, 0, ATTN_Q) for b in blks)
    local = lax.broadcasted_iota(jnp.int32, (ATTN_ROWS, 1), 0)
    return jnp.max(jnp.where(local >= done_rows, run, float(jnp.finfo(F32).min)))


def _attn_more(carry):
    return carry[-1] > ATTN_SKIP_BELOW


def _tri(strict):
    r = lax.broadcasted_iota(jnp.int32, (ATTN_SUB, ATTN_SUB), 0)
    c = lax.broadcasted_iota(jnp.int32, (ATTN_SUB, ATTN_SUB), 1)
    return (r > c if strict else r >= c).astype(BF)


REACH_TILE = (8, 128)


def _first_step_spec():
    return pl.BlockSpec((1, ATTN_ROWS, ATTN_K), lambda h, i: (h, i, 0))


def _reach_spec():
    return pl.BlockSpec((1, 1) + REACH_TILE, lambda h, i: (h, i, 0, 0))


def _head_cols(piece):
    return lambda t: pl.BlockSpec((1, t, HEAD_DIM), lambda h, i: (piece, 0, h))


def _attn_fwd(proj):
    t = proj.shape[1]
    nq = t // ATTN_ROWS
    tri = _tri(strict=True)

    def body(q_ref, k_ref, v_ref, tri_ref, o_ref, ob_ref, a_ref, beta_ref, reach_ref):
        i = pl.program_id(1)
        q = q_ref[0]
        row = i * ATTN_ROWS + lax.broadcasted_iota(jnp.int32, (ATTN_ROWS, 1), 0)

        def step(carry, keep=False):
            blks, bounds, acc, run, _ = carry
            starts = _attn_keys(blks)
            _, mask, log_beta, log_rest = _attn_step(q, k_ref, starts, bounds, row)
            tail, run = _suffix_sums(log_rest, tri_ref[...], run)
            a = jnp.where(mask, jnp.exp(log_beta + tail), 0.0).astype(BF)
            if keep:
                a_ref[0] = a
                beta_ref[0] = jnp.where(mask, jnp.exp(log_beta), 0.0).astype(BF)
            acc = acc + _per_unit(lambda u: _dot(_unit_rows(a, u), v_ref[0, pl.ds(starts[u], ATTN_K), :]))
            return (*_attn_next(blks), acc, run, _attn_reach(run, blks))

        first = (*_attn_sweep_start(i, t), jnp.zeros((ATTN_ROWS, HEAD_DIM), F32), jnp.zeros((ATTN_ROWS, 1), F32), jnp.float32(0.0))
        after_first = step(first, keep=True)
        reach_ref[...] = jnp.full(reach_ref.shape, after_first[-1], F32)
        o = lax.while_loop(_attn_more, step, after_first)[2]
        o_ref[...] = o
        ob_ref[...] = o.astype(BF)

    qspec = pl.BlockSpec((1, ATTN_ROWS, HEAD_DIM), lambda h, i: (3, i, h))
    rowblk = pl.BlockSpec((ATTN_ROWS, HEAD_DIM), lambda h, i: (i, h))
    return pl.pallas_call(
        body, name="attn_fwd", grid=(N_HEADS, nq),
        in_specs=[qspec, _head_cols(4)(t), _head_cols(5)(t), pl.BlockSpec((ATTN_SUB, ATTN_SUB), lambda h, i: (0, 0))],
        out_specs=[rowblk, rowblk, _first_step_spec(), _first_step_spec(), _reach_spec()],
        out_shape=[S((t, D_MODEL), F32), S((t, D_MODEL), BF), S((N_HEADS, t, ATTN_K), BF), S((N_HEADS, t, ATTN_K), BF),
                   S((N_HEADS, nq) + REACH_TILE, F32)],
    )(proj, proj, proj, tri)


def _attn_bwd(proj, o, d_o, a_first, beta_first, reach, ride=None):
    t = proj.shape[1]
    nq = t // ATTN_ROWS
    tri_strict, tri_incl = _tri(strict=True), _tri(strict=False)
    scale = 1.0 / math.sqrt(HEAD_DIM)

    def body(q_ref, k_ref, v_ref, o_ref, do_ref, a_ref, beta_ref, reach_ref, tris_ref, trii_ref, dq_ref, dk_ref, dv_ref, dk_acc, dv_acc):
        i = pl.program_id(1)

        @pl.when(i == 0)
        def _():
            dk_acc[...] = jnp.zeros_like(dk_acc)
            dv_acc[...] = jnp.zeros_like(dv_acc)

        q = q_ref[0]
        do = do_ref[...]
        total = jnp.sum(do.astype(F32) * o_ref[...], axis=1, keepdims=True)
        zero = jnp.zeros((ATTN_ROWS, 1), F32)
        blks0, bounds0 = _attn_sweep_start(i, t)

        def finish(starts, a, dz, dq):
            dzb = (dz * scale).astype(BF)
            for u in range(ATTN_UNITS):
                dv_acc[pl.ds(starts[u], ATTN_K), :] += _dot_tn(_unit_rows(a, u), _unit_rows(do, u))
                dk_acc[pl.ds(starts[u], ATTN_K), :] += _dot_tn(_unit_rows(dzb, u), _unit_rows(q, u))
            return dq + _per_unit(lambda u: _dot(_unit_rows(dzb, u), k_ref[0, pl.ds(starts[u], ATTN_K), :]))

        def grad_a(starts, a):
            return _per_unit(lambda u: _dot_nt(_unit_rows(do, u), v_ref[0, pl.ds(starts[u], ATTN_K), :])) * a.astype(F32)

        one_step = jnp.max(reach_ref[...]) <= ATTN_SKIP_BELOW

        @pl.when(one_step)
        def _():
            starts = _attn_keys(blks0)
            a = a_ref[0]
            beta = beta_ref[0].astype(F32)
            de = grad_a(starts, a)
            right, _ = _suffix_sums(de, trii_ref[...], zero)
            dz = de * (1.0 - beta) - (total - right) * beta
            dq_ref[...] = finish(starts, a, dz, jnp.zeros((ATTN_ROWS, HEAD_DIM), F32)).astype(BF)

        @pl.when(jnp.logical_not(one_step))
        def _():
            row = i * ATTN_ROWS + lax.broadcasted_iota(jnp.int32, (ATTN_ROWS, 1), 0)

            def step(carry):
                blks, bounds, dq, seen, run, _ = carry
                starts = _attn_keys(blks)
                z, mask, log_beta, log_rest = _attn_step(q, k_ref, starts, bounds, row)
                tail, run = _suffix_sums(log_rest, tris_ref[...], run)
                a = jnp.where(mask, jnp.exp(log_beta + tail), 0.0).astype(BF)
                de = grad_a(starts, a)
                right, seen = _suffix_sums(de, trii_ref[...], seen)
                beta = jax.nn.sigmoid(z)
                dz = jnp.where(mask, de * (1.0 - beta) - (total - right) * beta, 0.0)
                return (*_attn_next(blks), finish(starts, a, dz, dq), seen, run, _attn_reach(run, blks))

            first = (blks0, bounds0, jnp.zeros((ATTN_ROWS, HEAD_DIM), F32), zero, zero, jnp.float32(0.0))
            dq_ref[...] = lax.while_loop(_attn_more, step, step(first))[2].astype(BF)

        @pl.when(i == nq - 1)
        def _():
            dk_ref[...] = dk_acc[...].astype(BF)
            dv_ref[...] = dv_acc[...].astype(BF)

    qspec = pl.BlockSpec((1, ATTN_ROWS, HEAD_DIM), lambda h, i: (3, i, h))
    rowblk = pl.BlockSpec((ATTN_ROWS, HEAD_DIM), lambda h, i: (i, h))
    head = pl.BlockSpec((t, HEAD_DIM), lambda h, i: (0, h))
    trispec = pl.BlockSpec((ATTN_SUB, ATTN_SUB), lambda h, i: (0, 0))
    return _pallas(
        body, name="attn_bwd", grid=(N_HEADS, nq), ride=ride,
        in_specs=[qspec, _head_cols(4)(t), _head_cols(5)(t), rowblk, rowblk, _first_step_spec(), _first_step_spec(), _reach_spec(),
                  trispec, trispec],
        out_specs=[rowblk, head, head],
        out_shape=[S((t, D_MODEL), BF)] * 3,
        scratch_shapes=[pltpu.VMEM((t, HEAD_DIM), F32), pltpu.VMEM((t, HEAD_DIM), F32)],
        args=[proj, proj, proj, o, d_o, a_first, beta_first, reach, tri_strict, tri_incl])


def _tail(h3, n4, p, w_pg, w_pp, g_ple, g_final, target):
    t = h3.shape[0]
    tm = min(TOKEN_TILE, t)
    steps = t // tm

    def body(h_ref, n_ref, p_ref, wpg_ref, wpp_ref, gp_ref, gf_ref, tgt_ref,
             dh_ref, ds_ref, dpp_ref, loss_ref, dgf_ref, dgp_ref):
        pg = jax.nn.sigmoid(_dot(n_ref[...], wpg_ref[...]))
        pp = _dot(p_ref[...].astype(BF), wpp_ref[...])
        h3v = h_ref[...]
        h4 = h3v + pg * pp
        gf = gf_ref[...]
        diff = _rms(h4, gf) - tgt_ref[...]
        _accumulate(loss_ref, jnp.sum(diff * diff, axis=0, keepdims=True))
        dh4, dgf = _rms_bwd(diff * (1.0 / D_MODEL), h4, gf)
        _accumulate(dgf_ref, dgf)
        dpp_ref[...] = (dh4 * pg).astype(BF)
        ds = (dh4 * pp * pg * (1.0 - pg)).astype(BF)
        ds_ref[...] = ds
        dh3, dgp = _rms_bwd(_dot_nt(ds, wpg_ref[...]), h3v, gp_ref[...])
        _accumulate(dgp_ref, dgp)
        dh_ref[...] = dh4 + dh3

        @pl.when(pl.program_id(0) == steps - 1)
        def _():
            loss_ref[...] = jnp.full(loss_ref.shape, 0.5 / D_MODEL * jnp.sum(loss_ref[...]), F32)

    vec = (1, D_MODEL)
    return pl.pallas_call(
        body, name="tail", grid=(steps,),
        in_specs=[_rows(tm, D_MODEL), _rows(tm, D_MODEL), _rows(tm, PLE_DIM), _const_spec((D_MODEL, D_MODEL)),
                  _const_spec((PLE_DIM, D_MODEL)), _const_spec(vec), _const_spec(vec), _rows(tm, D_MODEL)],
        out_specs=[_rows(tm, D_MODEL)] * 3 + [_acc_spec(vec)] * 3,
        out_shape=[S((t, D_MODEL), F32), S((t, D_MODEL), BF), S((t, D_MODEL), BF)] + [S(vec, F32)] * 3,
    )(h3, n4, p, w_pg, w_pp, g_ple, g_final, target)


def _wgrad(xs, ys, name, ride=None, tile=None):
    bx, t, k = xs.shape
    by, _, n = ys.shape
    b = max(bx, by)
    tt = min(tile or WGRAD_TILE * 2 // xs.dtype.itemsize, t)
    steps = t // tt

    def body(x_ref, y_ref, o_ref, acc_ref):
        s = pl.program_id(1)

        @pl.when(s == 0)
        def _():
            acc_ref[...] = jnp.zeros_like(acc_ref)
        acc_ref[...] += _dot_tn(x_ref[0].astype(BF), y_ref[0].astype(BF))

        @pl.when(s == steps - 1)
        def _():
            o_ref[0] = acc_ref[...].astype(BF)

    (out,), landed = _pallas(
        body, name=name, grid=(b, steps), ride=ride,
        in_specs=[pl.BlockSpec((1, tt, k), (lambda j, s: (j, s, 0)) if bx > 1 else (lambda j, s: (0, s, 0))),
                  pl.BlockSpec((1, tt, n), (lambda j, s: (j, s, 0)) if by > 1 else (lambda j, s: (0, s, 0)))],
        out_specs=[pl.BlockSpec((1, k, n), lambda j, s: (j, 0, 0))],
        out_shape=[S((b, k, n), BF)],
        scratch_shapes=[pltpu.VMEM((k, n), F32)],
        args=[xs, ys])
    return (out, landed) if ride is not None else out


def _wgrad_pieces(x, ys, name, ride=None, tile=None, row_parts=1, transposed=False):
    t, k = x.shape
    n = ys[0].shape[2]
    counts = [y.shape[0] for y in ys]
    offsets = [sum(counts[:j]) for j in range(len(ys))]
    total = sum(counts)
    tt = min(tile or WGRAD_TILE, t)
    steps = t // tt
    rows, cols = (n, k) if transposed else (k, n)
    kp = rows // row_parts

    def body(x_ref, *refs):
        y_refs, o_refs, acc_ref = refs[:len(ys)], refs[len(ys):len(ys) + row_parts], refs[len(ys) + row_parts]
        p, s = pl.program_id(0), pl.program_id(1)

        @pl.when(s == 0)
        def _():
            acc_ref[...] = jnp.zeros_like(acc_ref)
        for j, y_ref in enumerate(y_refs):
            @pl.when(jnp.logical_and(p >= offsets[j], p < offsets[j] + counts[j]))
            def _(y_ref=y_ref):
                acc_ref[...] += _dot_tn(y_ref[0], x_ref[...]) if transposed else _dot_tn(x_ref[...], y_ref[0])

        @pl.when(s == steps - 1)
        def _():
            for part, o_ref in enumerate(o_refs):
                o_ref[0] = acc_ref[part * kp:(part + 1) * kp, :].astype(BF)

    def turn(j):
        lo, hi = offsets[j], offsets[j] + counts[j]
        return lambda p, s: (jnp.clip(p - lo, 0, counts[j] - 1), jnp.where(p < lo, 0, jnp.where(p >= hi, steps - 1, s)), 0)

    outs, landed = _pallas(
        body, name=name, grid=(total, steps), ride=ride,
        in_specs=[pl.BlockSpec((tt, k), lambda p, s: (s, 0))] + [pl.BlockSpec((1, tt, n), turn(j)) for j in range(len(ys))],
        out_specs=[pl.BlockSpec((1, kp, cols), lambda p, s: (p, 0, 0))] * row_parts,
        out_shape=[S((total, kp, cols), BF)] * row_parts,
        scratch_shapes=[pltpu.VMEM((rows, cols), F32)],
        args=[x, *ys])
    out = outs[0] if row_parts == 1 else outs
    return (out, landed) if ride is not None else out


def _ffn_bwd_hidden(dh, to_gate, to_up, w_out, name, ride=None):
    t = dh.shape[0]
    tm = min(TOKEN_TILE, t)

    def body(dh_ref, to_gate_ref, to_up_ref, wout_ref, df_ref, dgate_ref, dup_ref):
        df = (0.5 * dh_ref[...]).astype(BF)
        df_ref[...] = df
        for c in range(N_FF_CHUNKS):
            dact = _dot_nt(df, wout_ref[c])
            dgate_ref[c] = (dact * to_gate_ref[c].astype(F32)).astype(BF)
            dup_ref[c] = (dact * to_up_ref[c].astype(F32)).astype(BF)

    return _pallas(
        body, name=name, grid=(t // tm,), ride=ride,
        in_specs=[_rows(tm, D_MODEL), _chunks(tm), _chunks(tm), _const_spec(w_out.shape)],
        out_specs=[_rows(tm, D_MODEL), _chunks(tm), _chunks(tm)],
        out_shape=[S((t, D_MODEL), BF)] + [S((N_FF_CHUNKS, t, FF_CHUNK), BF)] * 2,
        args=[dh, to_gate, to_up, w_out])


def _ffn_bwd_input(dh, h_in, g, dgate, dup, w_in, name, ride=None):
    t = dh.shape[0]
    tm = min(TOKEN_TILE, t)

    def body(dh_ref, h_ref, g_ref, dgate_ref, dup_ref, win_ref, dhi_ref, dg_ref):
        dn = jnp.zeros((tm, D_MODEL), F32)
        for c in range(N_FF_CHUNKS):
            dn = dn + _dot_nt(dgate_ref[c], win_ref[c]) + _dot_nt(dup_ref[c], win_ref[N_FF_CHUNKS + c])
        dhi, dg = _rms_bwd(dn, h_ref[...], g_ref[...])
        _accumulate(dg_ref, dg)
        dhi_ref[...] = dh_ref[...] + dhi

    vec = (1, D_MODEL)
    return _pallas(
        body, name=name, grid=(t // tm,), ride=ride,
        in_specs=[_rows(tm, D_MODEL), _rows(tm, D_MODEL), _const_spec(vec), _chunks(tm), _chunks(tm), _const_spec(w_in.shape)],
        out_specs=[_rows(tm, D_MODEL), _acc_spec(vec)],
        out_shape=[S((t, D_MODEL), F32), S(vec, F32)],
        args=[dh, h_in, g, dgate, dup, w_in])


def _mixer_bwd(dh2, proj, yc, ya, conv_w, w_co, w_ao, w_mo, ride=None):
    t = dh2.shape[0]
    tm = min(TOKEN_TILE, t)

    def body(dh_ref, cb_ref, cc_ref, cx_ref, gc_ref, ga_ref, cch_ref, cxh_ref, yc_ref, ya_ref, cw_ref, wco_ref, wao_ref, wmo_ref,
             dhb_ref, dyc_ref, dya_ref, dgc_ref, dga_ref, dcb_ref, dcv_ref, do_ref):
        dhb = dh_ref[...].astype(BF)
        dhb_ref[...] = dhb
        dmerged = _dot_nt(dhb, wmo_ref[...])
        sc = jax.nn.sigmoid(gc_ref[0].astype(F32))
        sa = jax.nn.sigmoid(ga_ref[0].astype(F32))
        dyc = (dmerged * sc).astype(BF)
        dya = (dmerged * sa).astype(BF)
        dyc_ref[...] = dyc
        dya_ref[...] = dya
        dgc_ref[...] = (dmerged * yc_ref[...].astype(F32) * sc * (1.0 - sc)).astype(BF)
        dga_ref[...] = (dmerged * ya_ref[...].astype(F32) * sa * (1.0 - sa)).astype(BF)
        m, m1, m2 = _conv_inputs(cc_ref, cx_ref, cch_ref, cxh_ref)
        cw = cw_ref[...]
        cv = cw[0:1, :] * m2 + cw[1:2, :] * m1 + cw[2:3, :] * m
        dycin = _dot_nt(dyc, wco_ref[...])
        dcb_ref[...] = (dycin * cv).astype(BF)
        dcv_ref[...] = (dycin * cb_ref[0].astype(F32)).astype(BF)
        do_ref[...] = _dot_nt(dya, wao_ref[...]).astype(BF)

    sq = (D_MODEL, D_MODEL)
    return _pallas(
        body, name="mixer_bwd", grid=(t // tm,), ride=ride,
        in_specs=[_rows(tm, D_MODEL), _piece(0, tm), _piece(1, tm), _piece(2, tm), _piece(6, tm), _piece(7, tm),
                  _prev_halo(1, tm), _prev_halo(2, tm), _rows(tm, D_MODEL), _rows(tm, D_MODEL),
                  _const_spec((3, D_MODEL)), _const_spec(sq), _const_spec(sq), _const_spec(sq)],
        out_specs=[_rows(tm, D_MODEL)] * 8,
        out_shape=[S((t, D_MODEL), BF)] * 8,
        args=[dh2, proj, proj, proj, proj, proj, proj, proj, yc, ya, conv_w, w_co, w_ao, w_mo])


TAP_ROWS = 8


def _conv_bwd(dcv, proj, conv_w):
    t = dcv.shape[0]
    tm = min(TOKEN_TILE, t)
    steps = t // tm

    def body(dcv_ref, nxt_ref, cc_ref, cx_ref, cch_ref, cxh_ref, cw_ref, dcc_ref, dcx_ref, dw_ref):
        i = pl.program_id(0)
        m, m1, m2 = _conv_inputs(cc_ref, cx_ref, cch_ref, cxh_ref)
        d0 = dcv_ref[...].astype(F32)
        nxt = jnp.where(i == steps - 1, 0.0, nxt_ref[...].astype(F32))
        row = lax.broadcasted_iota(jnp.int32, (tm, 1), 0)
        d1 = jnp.where(row == tm - 1, nxt[0:1, :], pltpu.roll(d0, tm - 1, 0))
        d2 = pltpu.roll(d0, tm - 2, 0)
        d2 = jnp.where(row == tm - 2, nxt[0:1, :], jnp.where(row == tm - 1, nxt[1:2, :], d2))
        cw = cw_ref[...]
        dm = cw[2:3, :] * d0 + cw[1:2, :] * d1 + cw[0:1, :] * d2
        dcc_ref[...] = (dm * cx_ref[0].astype(F32)).astype(BF)
        dcx_ref[...] = (dm * cc_ref[0].astype(F32)).astype(BF)
        tap_row = lax.broadcasted_iota(jnp.int32, (TAP_ROWS, 1), 0)
        dw = jnp.zeros((TAP_ROWS, D_MODEL), F32)
        for j, mk in enumerate((m2, m1, m)):
            dw = jnp.where(tap_row == j, jnp.sum(d0 * mk, axis=0, keepdims=True), dw)
        _accumulate(dw_ref, dw)

    nxt_spec = pl.BlockSpec((HALO, D_MODEL), lambda i: (jnp.minimum((i + 1) * (tm // HALO), t // HALO - 1), 0))
    return pl.pallas_call(
        body, name="conv_bwd", grid=(steps,),
        in_specs=[_rows(tm, D_MODEL), nxt_spec, _piece(1, tm), _piece(2, tm), _prev_halo(1, tm), _prev_halo(2, tm),
                  _const_spec((3, D_MODEL))],
        out_specs=[_rows(tm, D_MODEL), _rows(tm, D_MODEL), _acc_spec((TAP_ROWS, D_MODEL))],
        out_shape=[S((t, D_MODEL), BF), S((t, D_MODEL), BF), S((TAP_ROWS, D_MODEL), F32)],
    )(dcv, dcv, proj, proj, proj, proj, conv_w)


def _mix_bwd(dpieces, w_mix, h1, dh2, g, ride=None):
    t = h1.shape[0]
    tm = min(TOKEN_TILE, t)

    def body(*refs):
        pieces, (w_ref, h_ref, dh_ref, g_ref, dhi_ref, dg_ref) = refs[:N_MIX], refs[N_MIX:]
        du = jnp.zeros((tm, D_MODEL), F32)
        for d in range(N_MIX):
            du = du + _dot_nt(pieces[d][...], w_ref[d])
        dhi, dg = _rms_bwd(du, h_ref[...], g_ref[...])
        _accumulate(dg_ref, dg)
        dhi_ref[...] = dh_ref[...] + dhi

    vec = (1, D_MODEL)
    return _pallas(
        body, name="mix_bwd", grid=(t // tm,), ride=ride,
        in_specs=[_rows(tm, D_MODEL)] * N_MIX + [_const_spec(w_mix.shape), _rows(tm, D_MODEL), _rows(tm, D_MODEL), _const_spec(vec)],
        out_specs=[_rows(tm, D_MODEL), _acc_spec(vec)],
        out_shape=[S((t, D_MODEL), F32), S(vec, F32)],
        args=[*dpieces, w_mix, h1, dh2, g])


def _adamw(partials, w, m, v, name):
    parts = list(partials) if isinstance(partials, (list, tuple)) else [partials]
    r, c = w.shape
    tr = next(d for d in (r, 512, 352, 256) if d <= 512 // len(parts) and r % d == 0)
    first_tile = [sum(p.shape[1] for p in parts[:j]) // tr for j in range(len(parts))]
    c1 = 1.0 - ADAM_B1 ** ADAM_STEP
    c2 = 1.0 - ADAM_B2 ** ADAM_STEP

    def body(*refs):
        p_refs, (w_ref, m_ref, v_ref, g_ref, d_ref, mo_ref, vo_ref) = refs[:len(parts)], refs[len(parts):]
        g = None
        for j, p_ref in enumerate(p_refs):
            gj = p_ref[0].astype(F32)
            for s in range(1, N_SHARDS):
                gj = gj + p_ref[s].astype(F32)
            g = gj if g is None else jnp.where(pl.program_id(0) >= first_tile[j], gj, g)
        mn = ADAM_B1 * m_ref[...] + (1.0 - ADAM_B1) * g
        vn = ADAM_B2 * v_ref[...] + (1.0 - ADAM_B2) * (g * g)
        g_ref[...] = g
        mo_ref[...] = mn
        vo_ref[...] = vn
        d_ref[...] = -ADAM_LR * ((mn / c1) / (jnp.sqrt(vn / c2) + ADAM_EPS) + ADAM_WD * w_ref[...])

    def rows_of(j):
        last = parts[j].shape[1] // tr - 1
        return lambda i: (0, jnp.clip(i - first_tile[j], 0, last), 0)

    blk = pl.BlockSpec((tr, c), lambda i: (i, 0))
    return pl.pallas_call(
        body, name=name, grid=(r // tr,),
        in_specs=[pl.BlockSpec((N_SHARDS, tr, c), rows_of(j)) for j in range(len(parts))] + [blk, blk, blk],
        out_specs=[blk] * 4, out_shape=[S((r, c), F32)] * 4,
    )(*parts, w, m, v)


_MATRICES = ("ffn1_w_in", "ffn1_w_out", "w_mix_in", "conv_w", "w_conv_out", "w_attn_out", "w_mix_out",
             "ffn2_w_in", "ffn2_w_out", "w_ple_gate", "w_ple_proj")
_GAINS = ("ffn1_norm", "mix_norm", "ffn2_norm", "ple_norm", "final_norm")
_WEIGHTS = ("ffn1_norm", "ffn1_w_in", "ffn1_w_out", "mix_norm", "w_mix_in", "conv_w", "w_conv_out", "w_attn_out", "w_mix_out",
            "ffn2_norm", "ffn2_w_in", "ffn2_w_out", "ple_norm", "w_ple_gate", "w_ple_proj", "final_norm")
CONV_ROWS = 8
_TRANSPOSED = ("ffn1_w_in", "ffn2_w_in")


def _columns_from_shards(g):
    return jnp.transpose(g, (1, 0, 2)).reshape(g.shape[1], N_SHARDS * g.shape[2])


def _shards_from_columns(a):
    r, c = a.shape
    return jnp.transpose(a.reshape(r, N_SHARDS, c // N_SHARDS), (1, 0, 2))


def kernel(x, p, ffn1_norm, ffn1_w_in, ffn1_w_out, mix_norm, w_mix_in, conv_w, w_conv_out, w_attn_out, w_mix_out, ffn2_norm, ffn2_w_in, ffn2_w_out, ple_norm, w_ple_gate, w_ple_proj, final_norm, loss_target, m_ffn1_norm, m_ffn1_w_in, m_ffn1_w_out, m_mix_norm, m_w_mix_in, m_conv_w, m_w_conv_out, m_w_attn_out, m_w_mix_out, m_ffn2_norm, m_ffn2_w_in, m_ffn2_w_out, m_ple_norm, m_w_ple_gate, m_w_ple_proj, m_final_norm, v_ffn1_norm, v_ffn1_w_in, v_ffn1_w_out, v_mix_norm, v_w_mix_in, v_conv_w, v_w_conv_out, v_w_attn_out, v_w_mix_out, v_ffn2_norm, v_ffn2_w_in, v_ffn2_w_out, v_ple_norm, v_w_ple_gate, v_w_ple_proj, v_final_norm):
    given = dict(locals())
    t = x.shape[1]
    xs = x.reshape(t, D_MODEL)
    ps = p.reshape(t, PLE_DIM)
    target = loss_target.reshape(t, D_MODEL)
    shard = {k: given[k].reshape(given[k].shape[-2:]) for k in _MATRICES}
    gain = {k: given[k].reshape(1, D_MODEL) for k in _GAINS}

    send = {k: shard[k].astype(BF) for k in _MATRICES}
    send["conv_w"] = jnp.pad(shard["conv_w"], ((0, CONV_ROWS - 3), (0, 0)))
    loss_vec, dx, landed, gain_grads = _forward_backward(xs, ps, target, gain, send)
    gain_rows = jnp.concatenate([gain_grads[k] for k in _GAINS] + [loss_vec, jnp.zeros((8 - len(_GAINS) - 1, D_MODEL), F32)], axis=0)
    gain_parts, = _exchange_alone("gather", [gain_rows], "gather_gain_gradients")

    out = {}
    for k in _MATRICES:
        w, m, v = shard[k], given["m_" + k].reshape(shard[k].shape), given["v_" + k].reshape(shard[k].shape)
        part = landed[k]
        if k == "conv_w":
            pad = ((0, CONV_ROWS - 3), (0, 0))
            w, m, v = jnp.pad(w, pad), jnp.pad(m, pad), jnp.pad(v, pad, constant_values=1.0)
        if k in _TRANSPOSED:
            w, m, v = w.T, m.T, v.T
        res = _adamw(part, w, m, v, "adamw_" + k)
        out[k] = [r[:3] if k == "conv_w" else (r.T if k in _TRANSPOSED else r) for r in res]
    stack = lambda pre: jnp.concatenate([given[pre + k].reshape(1, D_MODEL) for k in _GAINS] + [jnp.ones((8 - len(_GAINS), D_MODEL), F32)], axis=0)
    res = _adamw(gain_parts, stack(""), stack("m_"), stack("v_"), "adamw_gains")
    for j, k in enumerate(_GAINS):
        out[k] = [r[j:j + 1] for r in res]

    loss = jnp.sum(gain_parts[:, len(_GAINS), 0])
    per_kind = [[out[k][j].reshape(given[k].shape) for k in _WEIGHTS] for j in range(4)]
    return (loss, dx.reshape(x.shape), *per_kind[0], *per_kind[1], *per_kind[2], *per_kind[3])


def _forward_backward(xs, ps, target, gain, send, full=None):
    exchange = full is None
    full = dict(full or {})
    grads, landed = {}, {}

    def gather(names):
        return ("gather", [send[k] for k in names]) if exchange else None

    def scatter(names):
        return ("scatter", [grads[k] for k in names]) if exchange else None

    def keep(into, names, got):
        into.update(zip(names, got))

    first = ("ffn1_w_in",)
    (n1,), got = _prenorm(xs, gain["ffn1_norm"], ride=gather(first))
    keep(full, first, got)
    w1_in = full["ffn1_w_in"]
    second = ("ffn1_w_out", "w_mix_in")
    (act1, to_gate1, to_up1), got = _ffn_up(n1, w1_in, "ffn1_up", ride=gather(second))
    keep(full, second, got)
    w1_out = full["ffn1_w_out"].reshape(N_FF_CHUNKS, FF_CHUNK, D_MODEL)
    third = ("conv_w", "w_conv_out", "w_attn_out", "w_mix_out")
    (h1, u), got = _ffn_down(xs, act1, w1_out, gain["mix_norm"], "ffn1_down", ride=gather(third))
    keep(full, third, got)
    w_mix = full["w_mix_in"]
    w_co, w_ao, w_mo = (full[k].reshape(D_MODEL, D_MODEL) for k in ("w_conv_out", "w_attn_out", "w_mix_out"))
    taps = _columns_from_shards(full["conv_w"][:, :3, :])
    rest = ("ffn2_w_in", "ffn2_w_out", "w_ple_gate", "w_ple_proj")
    (proj,), got = _mix_proj(u, w_mix, ride=gather(rest))
    keep(full, rest, got)
    w2_in, w2_out = full["ffn2_w_in"], full["ffn2_w_out"].reshape(N_FF_CHUNKS, FF_CHUNK, D_MODEL)
    w_pg = full["w_ple_gate"].reshape(D_MODEL, D_MODEL)
    w_pp = _columns_from_shards(full["w_ple_proj"])
    o, o_bf, a_first, beta_first, reach = _attn_fwd(proj)
    h2, n3, ycin, yc, ya, merged = _mixer_out(proj, o_bf, h1, taps, w_co, w_ao, w_mo, gain["ffn2_norm"])
    (act2, to_gate2, to_up2), _ = _ffn_up(n3, w2_in, "ffn2_up")
    (h3, n4), _ = _ffn_down(h2, act2, w2_out, gain["ple_norm"], "ffn2_down")
    dh3, ds, dpp, loss_vec, dg_final, dg_ple = _tail(h3, n4, ps, w_pg, w_pp, gain["ple_norm"], gain["final_norm"], target)

    one = lambda a: a[None]
    by_rows = lambda g, rows: g.reshape(N_SHARDS, rows // N_SHARDS, D_MODEL)
    square = WGRAD_TILE // 2
    grads["w_ple_gate"] = by_rows(_wgrad(one(n4), one(ds), "wgrad_ple_gate", tile=square), D_MODEL)
    grads["w_ple_proj"] = _shards_from_columns(_wgrad(one(ps), one(dpp), "wgrad_ple_proj")[0])
    ple = ("w_ple_gate", "w_ple_proj")
    (df2, dgate2, dup2), got = _ffn_bwd_hidden(dh3, to_gate2, to_up2, w2_out, "ffn2_bwd_hidden", ride=scatter(ple))
    keep(landed, ple, got)
    grads["ffn2_w_out"] = by_rows(_wgrad(act2, one(df2), "wgrad_ffn2_out"), D_FF)
    grads["ffn2_w_in"] = _wgrad_pieces(n3, [dgate2, dup2], "wgrad_ffn2_in", transposed=True)
    (dh2, dg_ffn2), got = _ffn_bwd_input(dh3, h2, gain["ffn2_norm"], dgate2, dup2, w2_in, "ffn2_bwd_input", ride=scatter(("ffn2_w_out",)))
    keep(landed, ("ffn2_w_out",), got)
    (dh2b, dyc, dya, dgc, dga, dcb, dcv, d_o), _ = _mixer_bwd(dh2, proj, yc, ya, taps, w_co, w_ao, w_mo)
    grads["w_mix_out"] = by_rows(_wgrad(one(merged), one(dh2b), "wgrad_mix_out", tile=square), D_MODEL)
    grads["w_conv_out"] = by_rows(_wgrad(one(ycin), one(dyc), "wgrad_conv_out", tile=square), D_MODEL)
    grads["w_attn_out"] = by_rows(_wgrad(one(o_bf), one(dya), "wgrad_attn_out", tile=square), D_MODEL)
    dcc, dcx, dtaps = _conv_bwd(dcv, proj, taps)
    grads["conv_w"] = jnp.pad(_shards_from_columns(dtaps[:3]), ((0, 0), (0, CONV_ROWS - 3), (0, 0)))
    behind_attn = ("ffn2_w_in", "w_mix_out", "w_conv_out", "w_attn_out", "conv_w")
    (dq, dk, dv), got = _attn_bwd(proj, o, d_o, a_first, beta_first, reach, ride=scatter(behind_attn))
    keep(landed, behind_attn, got)
    dpieces = [dcb, dcc, dcx, dq, dk, dv, dgc, dga]
    half = N_MIX // 2
    tops, bottoms = zip(_wgrad_pieces(u, [one(dp) for dp in dpieces[:half]], "wgrad_mix_in_a", tile=WGRAD_TILE // 2, row_parts=2),
                        _wgrad_pieces(u, [one(dp) for dp in dpieces[half:]], "wgrad_mix_in_b", tile=WGRAD_TILE // 2, row_parts=2))
    grads["w_mix_in top"], grads["w_mix_in bottom"] = jnp.concatenate(tops, axis=0), jnp.concatenate(bottoms, axis=0)
    (dh1, dg_mix), top = _mix_bwd(dpieces, w_mix, h1, dh2, gain["mix_norm"], ride=scatter(("w_mix_in top",)))
    (df1, dgate1, dup1), bottom = _ffn_bwd_hidden(dh1, to_gate1, to_up1, w1_out, "ffn1_bwd_hidden", ride=scatter(("w_mix_in bottom",)))
    if exchange:
        landed["w_mix_in"] = [top[0], bottom[0]]
    else:
        grads["w_mix_in"] = jnp.concatenate([grads.pop("w_mix_in top"), grads.pop("w_mix_in bottom")], axis=1)
    grads["ffn1_w_out"] = by_rows(_wgrad(act1, one(df1), "wgrad_ffn1_out"), D_FF)
    if exchange:
        grads["ffn1_w_in"], got = _wgrad_pieces(n1, [dgate1, dup1], "wgrad_ffn1_in", transposed=True, ride=scatter(("ffn1_w_out",)))
        keep(landed, ("ffn1_w_out",), got)
    else:
        grads["ffn1_w_in"] = _wgrad_pieces(n1, [dgate1, dup1], "wgrad_ffn1_in", transposed=True)
    (dx, dg_ffn1), got = _ffn_bwd_input(dh1, xs, gain["ffn1_norm"], dgate1, dup1, w1_in, "ffn1_bwd_input", ride=scatter(("ffn1_w_in",)))
    keep(landed, ("ffn1_w_in",), got)
    gain_grads = dict(ffn1_norm=dg_ffn1, mix_norm=dg_mix, ffn2_norm=dg_ffn2, ple_norm=dg_ple, final_norm=dg_final)
    return loss_vec, dx, (landed if exchange else grads), gain_grads
```

```python
import functools
import math

import jax
import jax.numpy as jnp
from jax import lax
from jax.experimental import pallas as pl
from jax.experimental.pallas import tpu as pltpu

D_MODEL = 1024
D_FF = 2816
N_SHARDS = 8
FF_CHUNK = 2 * D_FF // N_SHARDS
N_FF_CHUNKS = D_FF // FF_CHUNK
N_HEADS = 8
HEAD_DIM = 128
PLE_DIM = 256
NORM_EPS = 1e-6
N_MIX = 8
ADAM_LR, ADAM_B1, ADAM_B2, ADAM_EPS, ADAM_WD, ADAM_STEP = 0.001, 0.9, 0.999, 1e-08, 0.01, 10

TOKEN_TILE = 512
WGRAD_TILE = 4096
PROJ_TILE = 2048
ATTN_ROWS = 1024
ATTN_Q = 64
ATTN_SUB = 128
ATTN_K = 2 * ATTN_SUB
ATTN_SKIP_BELOW = -90.0

BF = jnp.bfloat16
F32 = jnp.float32
MESH = pl.DeviceIdType.MESH
NT = (((1,), (1,)), ((), ()))
TN = (((0,), (0,)), ((), ()))
S = jax.ShapeDtypeStruct
ANY = pl.BlockSpec(memory_space=pl.ANY)


def _const_spec(shape):
    nd = len(shape)
    return pl.BlockSpec(shape, lambda *_: (0,) * nd, pipeline_mode=pl.Buffered(1))


def _rows(tm, cols):
    return pl.BlockSpec((tm, cols), lambda i: (i, 0))


def _chunks(tm):
    return pl.BlockSpec((N_FF_CHUNKS, tm, FF_CHUNK), lambda i: (0, i, 0))


def _acc_spec(shape):
    nd = len(shape)
    return pl.BlockSpec(shape, lambda *_: (0,) * nd)


def _dot(a, b):
    return jnp.dot(a, b, preferred_element_type=F32)


def _dot_nt(a, b):
    return lax.dot_general(a, b, NT, preferred_element_type=F32)


def _dot_tn(a, b):
    return lax.dot_general(a, b, TN, preferred_element_type=F32)


def _rms(h, g):
    r = lax.rsqrt(jnp.mean(h * h, axis=-1, keepdims=True) + NORM_EPS)
    return h * r * g


def _rms_bwd(dn, h, g):
    r = lax.rsqrt(jnp.mean(h * h, axis=-1, keepdims=True) + NORM_EPS)
    nh = h * r
    gd = dn * g
    dh = r * (gd - nh * jnp.mean(gd * nh, axis=-1, keepdims=True))
    return dh, jnp.sum(dn * nh, axis=0, keepdims=True)


def _accumulate(ref, val):
    @pl.when(pl.program_id(0) == 0)
    def _():
        ref[...] = jnp.zeros_like(ref)
    ref[...] += val


def _place():
    x, y, c = lax.axis_index("x"), lax.axis_index("y"), lax.axis_index("c")
    return x, y, c


def _slot(px, py, pc):
    return 4 * px + 2 * py + pc


def _gather_phases(ins, outs, send_sems, recv_sems, local_sems):
    n = len(ins)

    def parties():
        x, y, c = _place()
        return (x, y, c), (x, y, 1 - c), [(1 - x, y), (x, 1 - y), (1 - x, 1 - y)], c

    def copy(a, k, block, to, src=None):
        dst = outs[a].at[_slot(*block)]
        return pltpu.make_async_remote_copy(
            src_ref=dst if src is None else src, dst_ref=dst,
            send_sem=send_sems.at[a, k], recv_sem=recv_sems.at[a, k],
            device_id=to, device_id_type=MESH)

    def own(a, me):
        return pltpu.make_async_copy(ins[a], outs[a].at[_slot(*me)], local_sems.at[a])

    def first(a, me, sibling, chips, c):
        return [copy(a, 0, me, sibling, src=ins[a])] + [copy(a, 1 + j, me, (*chip, c), src=ins[a]) for j, chip in enumerate(chips)]

    def start():
        me, sibling, chips, c = parties()
        for a in range(n):
            own(a, me).start()
        for a in range(n):
            for cp in first(a, me, sibling, chips, c):
                cp.start()

    def forward():
        me, sibling, chips, c = parties()
        for j, chip in enumerate(chips):
            for a in range(n):
                copy(a, 1 + j, (*chip, c), me).wait_recv()
                copy(a, 4 + j, (*chip, c), sibling).start()

    def finish():
        me, sibling, chips, c = parties()
        for a in range(n):
            copy(a, 0, sibling, me).wait_recv()
            for j, chip in enumerate(chips):
                copy(a, 4 + j, (*chip, 1 - c), me).wait_recv()
        for a in range(n):
            for cp in first(a, me, sibling, chips, c) + [copy(a, 4 + j, (*chip, c), sibling) for j, chip in enumerate(chips)]:
                cp.wait_send()
            own(a, me).wait()

    return [start, forward, finish]


def _scatter_phases(ins, outs, send_sems, recv_sems, local_sems):
    n = len(ins)

    def copies():
        x, y, c = _place()
        me = _slot(x, y, c)
        out = [pltpu.make_async_copy(ins[a].at[me], outs[a].at[me], local_sems.at[a]) for a in range(n)]
        for k in range(1, N_SHARDS):
            px = 1 - x if k & 4 else x
            py = 1 - y if k & 2 else y
            pc = 1 - c if k & 1 else c
            for a in range(n):
                out.append(pltpu.make_async_remote_copy(
                    src_ref=ins[a].at[_slot(px, py, pc)], dst_ref=outs[a].at[me],
                    send_sem=send_sems.at[a, k - 1], recv_sem=recv_sems.at[a, k - 1],
                    device_id=(px, py, pc), device_id_type=MESH))
        return out

    def start():
        for cp in copies():
            cp.start()

    def finish():
        for cp in copies():
            cp.wait()

    return [start, finish]


def _pallas(body, *, name, grid, in_specs, out_specs, out_shape, args, scratch_shapes=(), ride=None):
    if ride is None:
        outs = pl.pallas_call(body, name=name, grid=grid, in_specs=in_specs, out_specs=out_specs, out_shape=out_shape,
                              scratch_shapes=list(scratch_shapes))(*args)
        return list(outs), []
    kind, arrays = ride
    n, n_in, n_out, n_scr = len(arrays), len(in_specs), len(out_specs), len(scratch_shapes)
    total = math.prod(grid)
    middle = (9 * total) // 10
    landed_shape = [S((N_SHARDS,) + a.shape if kind == "gather" else a.shape, a.dtype) for a in arrays]

    def with_exchange(*refs):
        ins, riders_in = refs[:n_in], refs[n_in:n_in + n]
        outs, riders_out = refs[n_in + n:n_in + n + n_out], refs[n_in + n + n_out:n_in + 2 * n + n_out]
        scratch, sems = refs[n_in + 2 * n + n_out:n_in + 2 * n + n_out + n_scr], refs[n_in + 2 * n + n_out + n_scr:]
        step = 0
        for axis, size in enumerate(grid):
            step = step * size + pl.program_id(axis)
        phases = (_gather_phases if kind == "gather" else _scatter_phases)(riders_in, riders_out, *sems)
        pl.when(step == 0)(phases[0])
        body(*ins, *outs, *scratch)
        for phase in phases[1:-1]:
            pl.when(step == middle)(phase)
        pl.when(step == total - 1)(phases[-1])

    outs = pl.pallas_call(
        with_exchange, name=name, grid=grid,
        in_specs=list(in_specs) + [ANY] * n, out_specs=list(out_specs) + [ANY] * n,
        out_shape=list(out_shape) + landed_shape,
        scratch_shapes=list(scratch_shapes) + [pltpu.SemaphoreType.DMA((n, 7)), pltpu.SemaphoreType.DMA((n, 7)),
                                               pltpu.SemaphoreType.DMA((n,))],
    )(*args, *arrays)
    return list(outs[:n_out]), list(outs[n_out:])


def _exchange_alone(kind, arrays, name):
    return _pallas(lambda: None, name=name, grid=(1,), in_specs=[], out_specs=[], out_shape=[], args=[], ride=(kind, arrays))[1]


def _prenorm(x, g, ride=None):
    t = x.shape[0]
    tm = min(TOKEN_TILE, t)

    def body(x_ref, g_ref, n_ref):
        n_ref[...] = _rms(x_ref[...], g_ref[...]).astype(BF)

    return _pallas(
        body, name="prenorm", grid=(t // tm,), ride=ride,
        in_specs=[_rows(tm, D_MODEL), _const_spec((1, D_MODEL))], out_specs=[_rows(tm, D_MODEL)],
        out_shape=[S((t, D_MODEL), BF)], args=[x, g])


def _ffn_up(n, w_in, name, ride=None):
    t = n.shape[0]
    tm = min(TOKEN_TILE, t)

    def body(n_ref, win_ref, act_ref, to_gate_ref, to_up_ref):
        nb = n_ref[...]
        for c in range(N_FF_CHUNKS):
            gate = _dot(nb, win_ref[c])
            up = _dot(nb, win_ref[N_FF_CHUNKS + c])
            sg = jax.nn.sigmoid(gate)
            silu = gate * sg
            act_ref[c] = (silu * up).astype(BF)
            to_gate_ref[c] = (up * (sg * (1.0 + gate * (1.0 - sg)))).astype(BF)
            to_up_ref[c] = silu.astype(BF)

    return _pallas(
        body, name=name, grid=(t // tm,), ride=ride,
        in_specs=[_rows(tm, D_MODEL), _const_spec(w_in.shape)],
        out_specs=[_chunks(tm)] * 3, out_shape=[S((N_FF_CHUNKS, t, FF_CHUNK), BF)] * 3,
        args=[n, w_in])


def _ffn_down(h, act, w_out, g_next, name, ride=None):
    t = h.shape[0]
    tm = min(TOKEN_TILE, t)

    def body(h_ref, act_ref, wout_ref, g_ref, ho_ref, no_ref):
        acc = jnp.zeros((tm, D_MODEL), F32)
        for c in range(N_FF_CHUNKS):
            acc = acc + _dot(act_ref[c], wout_ref[c])
        ho = h_ref[...] + 0.5 * acc
        ho_ref[...] = ho
        no_ref[...] = _rms(ho, g_ref[...]).astype(BF)

    return _pallas(
        body, name=name, grid=(t // tm,), ride=ride,
        in_specs=[_rows(tm, D_MODEL), _chunks(tm), _const_spec(w_out.shape), _const_spec((1, D_MODEL))],
        out_specs=[_rows(tm, D_MODEL)] * 2, out_shape=[S((t, D_MODEL), F32), S((t, D_MODEL), BF)],
        args=[h, act, w_out, g_next])


def _mix_proj(u, w_mix, ride=None):
    t = u.shape[0]
    tm = min(PROJ_TILE, t)

    def body(u_ref, w_ref, o_ref):
        o_ref[0] = _dot(u_ref[...], w_ref[0]).astype(BF)

    return _pallas(
        body, name="mix_proj", grid=(N_MIX, t // tm), ride=ride,
        in_specs=[pl.BlockSpec((tm, D_MODEL), lambda d, i: (i, 0)), pl.BlockSpec((1, D_MODEL, D_MODEL), lambda d, i: (d, 0, 0))],
        out_specs=[pl.BlockSpec((1, tm, D_MODEL), lambda d, i: (d, i, 0))],
        out_shape=[S((N_MIX, t, D_MODEL), BF)], args=[u, w_mix])


HALO = 16


def _piece(d, tm):
    return pl.BlockSpec((1, tm, D_MODEL), lambda i: (d, i, 0))


def _prev_halo(d, tm):
    return pl.BlockSpec((1, HALO, D_MODEL), lambda i: (d, jnp.maximum(i * (tm // HALO) - 1, 0), 0))


def _shift_down(m, prev_tail, k):
    tm = m.shape[0]
    out = pltpu.roll(m, k, 0)
    row = lax.broadcasted_iota(jnp.int32, (tm, 1), 0)
    for j in range(k):
        out = jnp.where(row == j, prev_tail[HALO - k + j:HALO - k + j + 1, :], out)
    return out


def _conv_inputs(cc_ref, cx_ref, cch_ref, cxh_ref):
    m = cc_ref[0].astype(F32) * cx_ref[0].astype(F32)
    mh = cch_ref[0].astype(F32) * cxh_ref[0].astype(F32)
    mh = jnp.where(pl.program_id(0) == 0, 0.0, mh)
    return m, _shift_down(m, mh, 1), _shift_down(m, mh, 2)


def _mixer_out(proj, o, h1, conv_w, w_co, w_ao, w_mo, g_next):
    t = h1.shape[0]
    tm = min(TOKEN_TILE, t)

    def body(cb_ref, cc_ref, cx_ref, gc_ref, ga_ref, cch_ref, cxh_ref, o_ref, h_ref, cw_ref, wco_ref, wao_ref, wmo_ref,
             g_ref, ho_ref, no_ref, ycin_ref, yc_ref, ya_ref, mg_ref):
        m, m1, m2 = _conv_inputs(cc_ref, cx_ref, cch_ref, cxh_ref)
        cw = cw_ref[...]
        cv = cw[0:1, :] * m2 + cw[1:2, :] * m1 + cw[2:3, :] * m
        ycin = (cb_ref[0].astype(F32) * cv).astype(BF)
        ycin_ref[...] = ycin
        yc = _dot(ycin, wco_ref[...])
        ya = _dot(o_ref[...], wao_ref[...])
        yc_ref[...] = yc.astype(BF)
        ya_ref[...] = ya.astype(BF)
        merged = (jax.nn.sigmoid(gc_ref[0].astype(F32)) * yc + jax.nn.sigmoid(ga_ref[0].astype(F32)) * ya).astype(BF)
        mg_ref[...] = merged
        ho = h_ref[...] + _dot(merged, wmo_ref[...])
        ho_ref[...] = ho
        no_ref[...] = _rms(ho, g_ref[...]).astype(BF)

    sq = (D_MODEL, D_MODEL)
    return pl.pallas_call(
        body, name="mixer_out", grid=(t // tm,),
        in_specs=[_piece(0, tm), _piece(1, tm), _piece(2, tm), _piece(6, tm), _piece(7, tm), _prev_halo(1, tm), _prev_halo(2, tm),
                  _rows(tm, D_MODEL), _rows(tm, D_MODEL), _const_spec((3, D_MODEL)), _const_spec(sq), _const_spec(sq),
                  _const_spec(sq), _const_spec((1, D_MODEL))],
        out_specs=[_rows(tm, D_MODEL)] * 6,
        out_shape=[S((t, D_MODEL), F32)] + [S((t, D_MODEL), BF)] * 5,
    )(proj, proj, proj, proj, proj, proj, proj, o, h1, conv_w, w_co, w_ao, w_mo, g_next)


def _suffix_sums(vals, tri, before):
    out, right = [], before
    for b in reversed(range(ATTN_K // ATTN_SUB)):
        v = vals[:, b * ATTN_SUB:(b + 1) * ATTN_SUB]
        hi = v.astype(BF)
        lo = (v - hi.astype(F32)).astype(BF)
        out.append(_dot(hi, tri) + _dot(lo, tri) + right)
        right = right + jnp.sum(v, axis=1, keepdims=True)
    return jnp.concatenate(out[::-1], axis=1), right


ATTN_UNITS = ATTN_ROWS // ATTN_Q


def _unit_rows(x, u):
    return x[u * ATTN_Q:(u + 1) * ATTN_Q]


def _per_unit(fn):
    return jnp.concatenate([fn(u) for u in range(ATTN_UNITS)], axis=0)


def _per_row(vals):
    local = lax.broadcasted_iota(jnp.int32, (ATTN_ROWS, 1), 0)
    out = jnp.full((ATTN_ROWS, 1), vals[0], jnp.int32)
    for u in range(1, ATTN_UNITS):
        out = jnp.where(local >= u * ATTN_Q, vals[u], out)
    return out


def _attn_step(q, k_ref, starts, bounds, row):
    z = _per_unit(lambda u: _dot_nt(_unit_rows(q, u), k_ref[0, pl.ds(starts[u], ATTN_K), :])) * (1.0 / math.sqrt(HEAD_DIM))
    mask = lax.broadcasted_iota(jnp.int32, (1, ATTN_K), 1) < jnp.minimum(row, _per_row(bounds)) - _per_row(starts)
    log_beta = jnp.minimum(z, 0.0) - jnp.log(1.0 + jnp.exp(jnp.minimum(z, -z)))
    log_rest = jnp.where(mask, log_beta - z, 0.0)
    return z, mask, log_beta, log_rest


def _attn_sweep_start(i, t):
    blks = tuple(jnp.maximum(i * ATTN_UNITS + u + 1 - ATTN_K // ATTN_Q, 0) for u in range(ATTN_UNITS))
    return blks, tuple(jnp.int32(t) for _ in range(ATTN_UNITS))


def _attn_keys(blks):
    return [pl.multiple_of(b * ATTN_Q, ATTN_Q) for b in blks]


def _attn_next(blks):
    return tuple(jnp.maximum(b - ATTN_K // ATTN_Q, 0) for b in blks), tuple(b * ATTN_Q for b in blks)


def _attn_reach(run, blks):
    done_rows = sum(jnp.where(b > 0, 0, ATTN_Q) for b in blks)
    local = lax.broadcasted_iota(jnp.int32, (ATTN_ROWS, 1), 0)
    return jnp.max(jnp.where(local >= done_rows, run, float(jnp.finfo(F32).min)))


def _attn_more(carry):
    return carry[-1] > ATTN_SKIP_BELOW


def _tri(strict):
    r = lax.broadcasted_iota(jnp.int32, (ATTN_SUB, ATTN_SUB), 0)
    c = lax.broadcasted_iota(jnp.int32, (ATTN_SUB, ATTN_SUB), 1)
    return (r > c if strict else r >= c).astype(BF)


REACH_TILE = (8, 128)


def _first_step_spec():
    return pl.BlockSpec((1, ATTN_ROWS, ATTN_K), lambda h, i: (h, i, 0))


def _reach_spec():
    return pl.BlockSpec((1, 1) + REACH_TILE, lambda h, i: (h, i, 0, 0))


def _head_cols(piece):
    return lambda t: pl.BlockSpec((1, t, HEAD_DIM), lambda h, i: (piece, 0, h))


def _attn_fwd(proj):
    t = proj.shape[1]
    nq = t // ATTN_ROWS
    tri = _tri(strict=True)

    def body(q_ref, k_ref, v_ref, tri_ref, o_ref, ob_ref, a_ref, beta_ref, reach_ref):
        i = pl.program_id(1)
        q = q_ref[0]
        row = i * ATTN_ROWS + lax.broadcasted_iota(jnp.int32, (ATTN_ROWS, 1), 0)

        def step(carry, keep=False):
            blks, bounds, acc, run, _ = carry
            starts = _attn_keys(blks)
            _, mask, log_beta, log_rest = _attn_step(q, k_ref, starts, bounds, row)
            tail, run = _suffix_sums(log_rest, tri_ref[...], run)
            a = jnp.where(mask, jnp.exp(log_beta + tail), 0.0).astype(BF)
            if keep:
                a_ref[0] = a
                beta_ref[0] = jnp.where(mask, jnp.exp(log_beta), 0.0).astype(BF)
            acc = acc + _per_unit(lambda u: _dot(_unit_rows(a, u), v_ref[0, pl.ds(starts[u], ATTN_K), :]))
            return (*_attn_next(blks), acc, run, _attn_reach(run, blks))

        first = (*_attn_sweep_start(i, t), jnp.zeros((ATTN_ROWS, HEAD_DIM), F32), jnp.zeros((ATTN_ROWS, 1), F32), jnp.float32(0.0))
        after_first = step(first, keep=True)
        reach_ref[...] = jnp.full(reach_ref.shape, after_first[-1], F32)
        o = lax.while_loop(_attn_more, step, after_first)[2]
        o_ref[...] = o
        ob_ref[...] = o.astype(BF)

    qspec = pl.BlockSpec((1, ATTN_ROWS, HEAD_DIM), lambda h, i: (3, i, h))
    rowblk = pl.BlockSpec((ATTN_ROWS, HEAD_DIM), lambda h, i: (i, h))
    return pl.pallas_call(
        body, name="attn_fwd", grid=(N_HEADS, nq),
        in_specs=[qspec, _head_cols(4)(t), _head_cols(5)(t), pl.BlockSpec((ATTN_SUB, ATTN_SUB), lambda h, i: (0, 0))],
        out_specs=[rowblk, rowblk, _first_step_spec(), _first_step_spec(), _reach_spec()],
        out_shape=[S((t, D_MODEL), F32), S((t, D_MODEL), BF), S((N_HEADS, t, ATTN_K), BF), S((N_HEADS, t, ATTN_K), BF),
                   S((N_HEADS, nq) + REACH_TILE, F32)],
    )(proj, proj, proj, tri)


def _attn_bwd(proj, o, d_o, a_first, beta_first, reach, ride=None):
    t = proj.shape[1]
    nq = t // ATTN_ROWS
    tri_strict, tri_incl = _tri(strict=True), _tri(strict=False)
    scale = 1.0 / math.sqrt(HEAD_DIM)

    def body(q_ref, k_ref, v_ref, o_ref, do_ref, a_ref, beta_ref, reach_ref, tris_ref, trii_ref, dq_ref, dk_ref, dv_ref, dk_acc, dv_acc):
        i = pl.program_id(1)

        @pl.when(i == 0)
        def _():
            dk_acc[...] = jnp.zeros_like(dk_acc)
            dv_acc[...] = jnp.zeros_like(dv_acc)

        q = q_ref[0]
        do = do_ref[...]
        total = jnp.sum(do.astype(F32) * o_ref[...], axis=1, keepdims=True)
        zero = jnp.zeros((ATTN_ROWS, 1), F32)
        blks0, bounds0 = _attn_sweep_start(i, t)

        def finish(starts, a, dz, dq):
            dzb = (dz * scale).astype(BF)
            for u in range(ATTN_UNITS):
                dv_acc[pl.ds(starts[u], ATTN_K), :] += _dot_tn(_unit_rows(a, u), _unit_rows(do, u))
                dk_acc[pl.ds(starts[u], ATTN_K), :] += _dot_tn(_unit_rows(dzb, u), _unit_rows(q, u))
            return dq + _per_unit(lambda u: _dot(_unit_rows(dzb, u), k_ref[0, pl.ds(starts[u], ATTN_K), :]))

        def grad_a(starts, a):
            return _per_unit(lambda u: _dot_nt(_unit_rows(do, u), v_ref[0, pl.ds(starts[u], ATTN_K), :])) * a.astype(F32)

        one_step = jnp.max(reach_ref[...]) <= ATTN_SKIP_BELOW

        @pl.when(one_step)
        def _():
            starts = _attn_keys(blks0)
            a = a_ref[0]
            beta = beta_ref[0].astype(F32)
            de = grad_a(starts, a)
            right, _ = _suffix_sums(de, trii_ref[...], zero)
            dz = de * (1.0 - beta) - (total - right) * beta
            dq_ref[...] = finish(starts, a, dz, jnp.zeros((ATTN_ROWS, HEAD_DIM), F32)).astype(BF)

        @pl.when(jnp.logical_not(one_step))
        def _():
            row = i * ATTN_ROWS + lax.broadcasted_iota(jnp.int32, (ATTN_ROWS, 1), 0)

            def step(carry):
                blks, bounds, dq, seen, run, _ = carry
                starts = _attn_keys(blks)
                z, mask, log_beta, log_rest = _attn_step(q, k_ref, starts, bounds, row)
                tail, run = _suffix_sums(log_rest, tris_ref[...], run)
                a = jnp.where(mask, jnp.exp(log_beta + tail), 0.0).astype(BF)
                de = grad_a(starts, a)
                right, seen = _suffix_sums(de, trii_ref[...], seen)
                beta = jax.nn.sigmoid(z)
                dz = jnp.where(mask, de * (1.0 - beta) - (total - right) * beta, 0.0)
                return (*_attn_next(blks), finish(starts, a, dz, dq), seen, run, _attn_reach(run, blks))

            first = (blks0, bounds0, jnp.zeros((ATTN_ROWS, HEAD_DIM), F32), zero, zero, jnp.float32(0.0))
            dq_ref[...] = lax.while_loop(_attn_more, step, step(first))[2].astype(BF)

        @pl.when(i == nq - 1)
        def _():
            dk_ref[...] = dk_acc[...].astype(BF)
            dv_ref[...] = dv_acc[...].astype(BF)

    qspec = pl.BlockSpec((1, ATTN_ROWS, HEAD_DIM), lambda h, i: (3, i, h))
    rowblk = pl.BlockSpec((ATTN_ROWS, HEAD_DIM), lambda h, i: (i, h))
    head = pl.BlockSpec((t, HEAD_DIM), lambda h, i: (0, h))
    trispec = pl.BlockSpec((ATTN_SUB, ATTN_SUB), lambda h, i: (0, 0))
    return _pallas(
        body, name="attn_bwd", grid=(N_HEADS, nq), ride=ride,
        in_specs=[qspec, _head_cols(4)(t), _head_cols(5)(t), rowblk, rowblk, _first_step_spec(), _first_step_spec(), _reach_spec(),
                  trispec, trispec],
        out_specs=[rowblk, head, head],
        out_shape=[S((t, D_MODEL), BF)] * 3,
        scratch_shapes=[pltpu.VMEM((t, HEAD_DIM), F32), pltpu.VMEM((t, HEAD_DIM), F32)],
        args=[proj, proj, proj, o, d_o, a_first, beta_first, reach, tri_strict, tri_incl])


def _tail(h3, n4, p, w_pg, w_pp, g_ple, g_final, target):
    t = h3.shape[0]
    tm = min(TOKEN_TILE, t)
    steps = t // tm

    def body(h_ref, n_ref, p_ref, wpg_ref, wpp_ref, gp_ref, gf_ref, tgt_ref,
             dh_ref, ds_ref, dpp_ref, loss_ref, dgf_ref, dgp_ref):
        pg = jax.nn.sigmoid(_dot(n_ref[...], wpg_ref[...]))
        pp = _dot(p_ref[...].astype(BF), wpp_ref[...])
        h3v = h_ref[...]
        h4 = h3v + pg * pp
        gf = gf_ref[...]
        diff = _rms(h4, gf) - tgt_ref[...]
        _accumulate(loss_ref, jnp.sum(diff * diff, axis=0, keepdims=True))
        dh4, dgf = _rms_bwd(diff * (1.0 / D_MODEL), h4, gf)
        _accumulate(dgf_ref, dgf)
        dpp_ref[...] = (dh4 * pg).astype(BF)
        ds = (dh4 * pp * pg * (1.0 - pg)).astype(BF)
        ds_ref[...] = ds
        dh3, dgp = _rms_bwd(_dot_nt(ds, wpg_ref[...]), h3v, gp_ref[...])
        _accumulate(dgp_ref, dgp)
        dh_ref[...] = dh4 + dh3

        @pl.when(pl.program_id(0) == steps - 1)
        def _():
            loss_ref[...] = jnp.full(loss_ref.shape, 0.5 / D_MODEL * jnp.sum(loss_ref[...]), F32)

    vec = (1, D_MODEL)
    return pl.pallas_call(
        body, name="tail", grid=(steps,),
        in_specs=[_rows(tm, D_MODEL), _rows(tm, D_MODEL), _rows(tm, PLE_DIM), _const_spec((D_MODEL, D_MODEL)),
                  _const_spec((PLE_DIM, D_MODEL)), _const_spec(vec), _const_spec(vec), _rows(tm, D_MODEL)],
        out_specs=[_rows(tm, D_MODEL)] * 3 + [_acc_spec(vec)] * 3,
        out_shape=[S((t, D_MODEL), F32), S((t, D_MODEL), BF), S((t, D_MODEL), BF)] + [S(vec, F32)] * 3,
    )(h3, n4, p, w_pg, w_pp, g_ple, g_final, target)


def _wgrad(xs, ys, name, ride=None, tile=None):
    bx, t, k = xs.shape
    by, _, n = ys.shape
    b = max(bx, by)
    tt = min(tile or WGRAD_TILE * 2 // xs.dtype.itemsize, t)
    steps = t // tt

    def body(x_ref, y_ref, o_ref, acc_ref):
        s = pl.program_id(1)

        @pl.when(s == 0)
        def _():
            acc_ref[...] = jnp.zeros_like(acc_ref)
        acc_ref[...] += _dot_tn(x_ref[0].astype(BF), y_ref[0].astype(BF))

        @pl.when(s == steps - 1)
        def _():
            o_ref[0] = acc_ref[...].astype(BF)

    (out,), landed = _pallas(
        body, name=name, grid=(b, steps), ride=ride,
        in_specs=[pl.BlockSpec((1, tt, k), (lambda j, s: (j, s, 0)) if bx > 1 else (lambda j, s: (0, s, 0))),
                  pl.BlockSpec((1, tt, n), (lambda j, s: (j, s, 0)) if by > 1 else (lambda j, s: (0, s, 0)))],
        out_specs=[pl.BlockSpec((1, k, n), lambda j, s: (j, 0, 0))],
        out_shape=[S((b, k, n), BF)],
        scratch_shapes=[pltpu.VMEM((k, n), F32)],
        args=[xs, ys])
    return (out, landed) if ride is not None else out


def _wgrad_pieces(x, ys, name, ride=None, tile=None, row_parts=1, transposed=False):
    t, k = x.shape
    n = ys[0].shape[2]
    counts = [y.shape[0] for y in ys]
    offsets = [sum(counts[:j]) for j in range(len(ys))]
    total = sum(counts)
    tt = min(tile or WGRAD_TILE, t)
    steps = t // tt
    rows, cols = (n, k) if transposed else (k, n)
    kp = rows // row_parts

    def body(x_ref, *refs):
        y_refs, o_refs, acc_ref = refs[:len(ys)], refs[len(ys):len(ys) + row_parts], refs[len(ys) + row_parts]
        p, s = pl.program_id(0), pl.program_id(1)

        @pl.when(s == 0)
        def _():
            acc_ref[...] = jnp.zeros_like(acc_ref)
        for j, y_ref in enumerate(y_refs):
            @pl.when(jnp.logical_and(p >= offsets[j], p < offsets[j] + counts[j]))
            def _(y_ref=y_ref):
                acc_ref[...] += _dot_tn(y_ref[0], x_ref[...]) if transposed else _dot_tn(x_ref[...], y_ref[0])

        @pl.when(s == steps - 1)
        def _():
            for part, o_ref in enumerate(o_refs):
                o_ref[0] = acc_ref[part * kp:(part + 1) * kp, :].astype(BF)

    def turn(j):
        lo, hi = offsets[j], offsets[j] + counts[j]
        return lambda p, s: (jnp.clip(p - lo, 0, counts[j] - 1), jnp.where(p < lo, 0, jnp.where(p >= hi, steps - 1, s)), 0)

    outs, landed = _pallas(
        body, name=name, grid=(total, steps), ride=ride,
        in_specs=[pl.BlockSpec((tt, k), lambda p, s: (s, 0))] + [pl.BlockSpec((1, tt, n), turn(j)) for j in range(len(ys))],
        out_specs=[pl.BlockSpec((1, kp, cols), lambda p, s: (p, 0, 0))] * row_parts,
        out_shape=[S((total, kp, cols), BF)] * row_parts,
        scratch_shapes=[pltpu.VMEM((rows, cols), F32)],
        args=[x, *ys])
    out = outs[0] if row_parts == 1 else outs
    return (out, landed) if ride is not None else out


def _ffn_bwd_hidden(dh, to_gate, to_up, w_out, name, ride=None):
    t = dh.shape[0]
    tm = min(TOKEN_TILE, t)

    def body(dh_ref, to_gate_ref, to_up_ref, wout_ref, df_ref, dgate_ref, dup_ref):
        df = (0.5 * dh_ref[...]).astype(BF)
        df_ref[...] = df
        for c in range(N_FF_CHUNKS):
            dact = _dot_nt(df, wout_ref[c])
            dgate_ref[c] = (dact * to_gate_ref[c].astype(F32)).astype(BF)
            dup_ref[c] = (dact * to_up_ref[c].astype(F32)).astype(BF)

    return _pallas(
        body, name=name, grid=(t // tm,), ride=ride,
        in_specs=[_rows(tm, D_MODEL), _chunks(tm), _chunks(tm), _const_spec(w_out.shape)],
        out_specs=[_rows(tm, D_MODEL), _chunks(tm), _chunks(tm)],
        out_shape=[S((t, D_MODEL), BF)] + [S((N_FF_CHUNKS, t, FF_CHUNK), BF)] * 2,
        args=[dh, to_gate, to_up, w_out])


def _ffn_bwd_input(dh, h_in, g, dgate, dup, w_in, name, ride=None):
    t = dh.shape[0]
    tm = min(TOKEN_TILE, t)

    def body(dh_ref, h_ref, g_ref, dgate_ref, dup_ref, win_ref, dhi_ref, dg_ref):
        dn = jnp.zeros((tm, D_MODEL), F32)
        for c in range(N_FF_CHUNKS):
            dn = dn + _dot_nt(dgate_ref[c], win_ref[c]) + _dot_nt(dup_ref[c], win_ref[N_FF_CHUNKS + c])
        dhi, dg = _rms_bwd(dn, h_ref[...], g_ref[...])
        _accumulate(dg_ref, dg)
        dhi_ref[...] = dh_ref[...] + dhi

    vec = (1, D_MODEL)
    return _pallas(
        body, name=name, grid=(t // tm,), ride=ride,
        in_specs=[_rows(tm, D_MODEL), _rows(tm, D_MODEL), _const_spec(vec), _chunks(tm), _chunks(tm), _const_spec(w_in.shape)],
        out_specs=[_rows(tm, D_MODEL), _acc_spec(vec)],
        out_shape=[S((t, D_MODEL), F32), S(vec, F32)],
        args=[dh, h_in, g, dgate, dup, w_in])


def _mixer_bwd(dh2, proj, yc, ya, conv_w, w_co, w_ao, w_mo, ride=None):
    t = dh2.shape[0]
    tm = min(TOKEN_TILE, t)

    def body(dh_ref, cb_ref, cc_ref, cx_ref, gc_ref, ga_ref, cch_ref, cxh_ref, yc_ref, ya_ref, cw_ref, wco_ref, wao_ref, wmo_ref,
             dhb_ref, dyc_ref, dya_ref, dgc_ref, dga_ref, dcb_ref, dcv_ref, do_ref):
        dhb = dh_ref[...].astype(BF)
        dhb_ref[...] = dhb
        dmerged = _dot_nt(dhb, wmo_ref[...])
        sc = jax.nn.sigmoid(gc_ref[0].astype(F32))
        sa = jax.nn.sigmoid(ga_ref[0].astype(F32))
        dyc = (dmerged * sc).astype(BF)
        dya = (dmerged * sa).astype(BF)
        dyc_ref[...] = dyc
        dya_ref[...] = dya
        dgc_ref[...] = (dmerged * yc_ref[...].astype(F32) * sc * (1.0 - sc)).astype(BF)
        dga_ref[...] = (dmerged * ya_ref[...].astype(F32) * sa * (1.0 - sa)).astype(BF)
        m, m1, m2 = _conv_inputs(cc_ref, cx_ref, cch_ref, cxh_ref)
        cw = cw_ref[...]
        cv = cw[0:1, :] * m2 + cw[1:2, :] * m1 + cw[2:3, :] * m
        dycin = _dot_nt(dyc, wco_ref[...])
        dcb_ref[...] = (dycin * cv).astype(BF)
        dcv_ref[...] = (dycin * cb_ref[0].astype(F32)).astype(BF)
        do_ref[...] = _dot_nt(dya, wao_ref[...]).astype(BF)

    sq = (D_MODEL, D_MODEL)
    return _pallas(
        body, name="mixer_bwd", grid=(t // tm,), ride=ride,
        in_specs=[_rows(tm, D_MODEL), _piece(0, tm), _piece(1, tm), _piece(2, tm), _piece(6, tm), _piece(7, tm),
                  _prev_halo(1, tm), _prev_halo(2, tm), _rows(tm, D_MODEL), _rows(tm, D_MODEL),
                  _const_spec((3, D_MODEL)), _const_spec(sq), _const_spec(sq), _const_spec(sq)],
        out_specs=[_rows(tm, D_MODEL)] * 8,
        out_shape=[S((t, D_MODEL), BF)] * 8,
        args=[dh2, proj, proj, proj, proj, proj, proj, proj, yc, ya, conv_w, w_co, w_ao, w_mo])


TAP_ROWS = 8


def _conv_bwd(dcv, proj, conv_w):
    t = dcv.shape[0]
    tm = min(TOKEN_TILE, t)
    steps = t // tm

    def body(dcv_ref, nxt_ref, cc_ref, cx_ref, cch_ref, cxh_ref, cw_ref, dcc_ref, dcx_ref, dw_ref):
        i = pl.program_id(0)
        m, m1, m2 = _conv_inputs(cc_ref, cx_ref, cch_ref, cxh_ref)
        d0 = dcv_ref[...].astype(F32)
        nxt = jnp.where(i == steps - 1, 0.0, nxt_ref[...].astype(F32))
        row = lax.broadcasted_iota(jnp.int32, (tm, 1), 0)
        d1 = jnp.where(row == tm - 1, nxt[0:1, :], pltpu.roll(d0, tm - 1, 0))
        d2 = pltpu.roll(d0, tm - 2, 0)
        d2 = jnp.where(row == tm - 2, nxt[0:1, :], jnp.where(row == tm - 1, nxt[1:2, :], d2))
        cw = cw_ref[...]
        dm = cw[2:3, :] * d0 + cw[1:2, :] * d1 + cw[0:1, :] * d2
        dcc_ref[...] = (dm * cx_ref[0].astype(F32)).astype(BF)
        dcx_ref[...] = (dm * cc_ref[0].astype(F32)).astype(BF)
        tap_row = lax.broadcasted_iota(jnp.int32, (TAP_ROWS, 1), 0)
        dw = jnp.zeros((TAP_ROWS, D_MODEL), F32)
        for j, mk in enumerate((m2, m1, m)):
            dw = jnp.where(tap_row == j, jnp.sum(d0 * mk, axis=0, keepdims=True), dw)
        _accumulate(dw_ref, dw)

    nxt_spec = pl.BlockSpec((HALO, D_MODEL), lambda i: (jnp.minimum((i + 1) * (tm // HALO), t // HALO - 1), 0))
    return pl.pallas_call(
        body, name="conv_bwd", grid=(steps,),
        in_specs=[_rows(tm, D_MODEL), nxt_spec, _piece(1, tm), _piece(2, tm), _prev_halo(1, tm), _prev_halo(2, tm),
                  _const_spec((3, D_MODEL))],
        out_specs=[_rows(tm, D_MODEL), _rows(tm, D_MODEL), _acc_spec((TAP_ROWS, D_MODEL))],
        out_shape=[S((t, D_MODEL), BF), S((t, D_MODEL), BF), S((TAP_ROWS, D_MODEL), F32)],
    )(dcv, dcv, proj, proj, proj, proj, conv_w)


def _mix_bwd(dpieces, w_mix, h1, dh2, g, ride=None):
    t = h1.shape[0]
    tm = min(TOKEN_TILE, t)

    def body(*refs):
        pieces, (w_ref, h_ref, dh_ref, g_ref, dhi_ref, dg_ref) = refs[:N_MIX], refs[N_MIX:]
        du = jnp.zeros((tm, D_MODEL), F32)
        for d in range(N_MIX):
            du = du + _dot_nt(pieces[d][...], w_ref[d])
        dhi, dg = _rms_bwd(du, h_ref[...], g_ref[...])
        _accumulate(dg_ref, dg)
        dhi_ref[...] = dh_ref[...] + dhi

    vec = (1, D_MODEL)
    return _pallas(
        body, name="mix_bwd", grid=(t // tm,), ride=ride,
        in_specs=[_rows(tm, D_MODEL)] * N_MIX + [_const_spec(w_mix.shape), _rows(tm, D_MODEL), _rows(tm, D_MODEL), _const_spec(vec)],
        out_specs=[_rows(tm, D_MODEL), _acc_spec(vec)],
        out_shape=[S((t, D_MODEL), F32), S(vec, F32)],
        args=[*dpieces, w_mix, h1, dh2, g])


def _adamw(partials, w, m, v, name):
    parts = list(partials) if isinstance(partials, (list, tuple)) else [partials]
    r, c = w.shape
    tr = next(d for d in (r, 512, 352, 256) if d <= 512 // len(parts) and r % d == 0)
    first_tile = [sum(p.shape[1] for p in parts[:j]) // tr for j in range(len(parts))]
    c1 = 1.0 - ADAM_B1 ** ADAM_STEP
    c2 = 1.0 - ADAM_B2 ** ADAM_STEP

    def body(*refs):
        p_refs, (w_ref, m_ref, v_ref, g_ref, d_ref, mo_ref, vo_ref) = refs[:len(parts)], refs[len(parts):]
        g = None
        for j, p_ref in enumerate(p_refs):
            gj = p_ref[0].astype(F32)
            for s in range(1, N_SHARDS):
                gj = gj + p_ref[s].astype(F32)
            g = gj if g is None else jnp.where(pl.program_id(0) >= first_tile[j], gj, g)
        mn = ADAM_B1 * m_ref[...] + (1.0 - ADAM_B1) * g
        vn = ADAM_B2 * v_ref[...] + (1.0 - ADAM_B2) * (g * g)
        g_ref[...] = g
        mo_ref[...] = mn
        vo_ref[...] = vn
        d_ref[...] = -ADAM_LR * ((mn / c1) / (jnp.sqrt(vn / c2) + ADAM_EPS) + ADAM_WD * w_ref[...])

    def rows_of(j):
        last = parts[j].shape[1] // tr - 1
        return lambda i: (0, jnp.clip(i - first_tile[j], 0, last), 0)

    blk = pl.BlockSpec((tr, c), lambda i: (i, 0))
    return pl.pallas_call(
        body, name=name, grid=(r // tr,),
        in_specs=[pl.BlockSpec((N_SHARDS, tr, c), rows_of(j)) for j in range(len(parts))] + [blk, blk, blk],
        out_specs=[blk] * 4, out_shape=[S((r, c), F32)] * 4,
    )(*parts, w, m, v)


_MATRICES = ("ffn1_w_in", "ffn1_w_out", "w_mix_in", "conv_w", "w_conv_out", "w_attn_out", "w_mix_out",
             "ffn2_w_in", "ffn2_w_out", "w_ple_gate", "w_ple_proj")
_GAINS = ("ffn1_norm", "mix_norm", "ffn2_norm", "ple_norm", "final_norm")
_WEIGHTS = ("ffn1_norm", "ffn1_w_in", "ffn1_w_out", "mix_norm", "w_mix_in", "conv_w", "w_conv_out", "w_attn_out", "w_mix_out",
            "ffn2_norm", "ffn2_w_in", "ffn2_w_out", "ple_norm", "w_ple_gate", "w_ple_proj", "final_norm")
CONV_ROWS = 8
_TRANSPOSED = ("ffn1_w_in", "ffn2_w_in")


def _columns_from_shards(g):
    return jnp.transpose(g, (1, 0, 2)).reshape(g.shape[1], N_SHARDS * g.shape[2])


def _shards_from_columns(a):
    r, c = a.shape
    return jnp.transpose(a.reshape(r, N_SHARDS, c // N_SHARDS), (1, 0, 2))


def kernel(x, p, ffn1_norm, ffn1_w_in, ffn1_w_out, mix_norm, w_mix_in, conv_w, w_conv_out, w_attn_out, w_mix_out, ffn2_norm, ffn2_w_in, ffn2_w_out, ple_norm, w_ple_gate, w_ple_proj, final_norm, loss_target, m_ffn1_norm, m_ffn1_w_in, m_ffn1_w_out, m_mix_norm, m_w_mix_in, m_conv_w, m_w_conv_out, m_w_attn_out, m_w_mix_out, m_ffn2_norm, m_ffn2_w_in, m_ffn2_w_out, m_ple_norm, m_w_ple_gate, m_w_ple_proj, m_final_norm, v_ffn1_norm, v_ffn1_w_in, v_ffn1_w_out, v_mix_norm, v_w_mix_in, v_conv_w, v_w_conv_out, v_w_attn_out, v_w_mix_out, v_ffn2_norm, v_ffn2_w_in, v_ffn2_w_out, v_ple_norm, v_w_ple_gate, v_w_ple_proj, v_final_norm):
    given = dict(locals())
    t = x.shape[1]
    xs = x.reshape(t, D_MODEL)
    ps = p.reshape(t, PLE_DIM)
    target = loss_target.reshape(t, D_MODEL)
    shard = {k: given[k].reshape(given[k].shape[-2:]) for k in _MATRICES}
    gain = {k: given[k].reshape(1, D_MODEL) for k in _GAINS}

    send = {k: shard[k].astype(BF) for k in _MATRICES}
    send["conv_w"] = jnp.pad(shard["conv_w"], ((0, CONV_ROWS - 3), (0, 0)))
    loss_vec, dx, landed, gain_grads = _forward_backward(xs, ps, target, gain, send)
    gain_rows = jnp.concatenate([gain_grads[k] for k in _GAINS] + [loss_vec, jnp.zeros((8 - len(_GAINS) - 1, D_MODEL), F32)], axis=0)
    gain_parts, = _exchange_alone("gather", [gain_rows], "gather_gain_gradients")

    out = {}
    for k in _MATRICES:
        w, m, v = shard[k], given["m_" + k].reshape(shard[k].shape), given["v_" + k].reshape(shard[k].shape)
        part = landed[k]
        if k == "conv_w":
            pad = ((0, CONV_ROWS - 3), (0, 0))
            w, m, v = jnp.pad(w, pad), jnp.pad(m, pad), jnp.pad(v, pad, constant_values=1.0)
        if k in _TRANSPOSED:
            w, m, v = w.T, m.T, v.T
        res = _adamw(part, w, m, v, "adamw_" + k)
        out[k] = [r[:3] if k == "conv_w" else (r.T if k in _TRANSPOSED else r) for r in res]
    stack = lambda pre: jnp.concatenate([given[pre + k].reshape(1, D_MODEL) for k in _GAINS] + [jnp.ones((8 - len(_GAINS), D_MODEL), F32)], axis=0)
    res = _adamw(gain_parts, stack(""), stack("m_"), stack("v_"), "adamw_gains")
    for j, k in enumerate(_GAINS):
        out[k] = [r[j:j + 1] for r in res]

    loss = jnp.sum(gain_parts[:, len(_GAINS), 0])
    per_kind = [[out[k][j].reshape(given[k].shape) for k in _WEIGHTS] for j in range(4)]
    return (loss, dx.reshape(x.shape), *per_kind[0], *per_kind[1], *per_kind[2], *per_kind[3])


def _forward_backward(xs, ps, target, gain, send, full=None):
    exchange = full is None
    full = dict(full or {})
    grads, landed = {}, {}

    def gather(names):
        return ("gather", [send[k] for k in names]) if exchange else None

    def scatter(names):
        return ("scatter", [grads[k] for k in names]) if exchange else None

    def keep(into, names, got):
        into.update(zip(names, got))

    first = ("ffn1_w_in",)
    (n1,), got = _prenorm(xs, gain["ffn1_norm"], ride=gather(first))
    keep(full, first, got)
    w1_in = full["ffn1_w_in"]
    second = ("ffn1_w_out", "w_mix_in")
    (act1, to_gate1, to_up1), got = _ffn_up(n1, w1_in, "ffn1_up", ride=gather(second))
    keep(full, second, got)
    w1_out = full["ffn1_w_out"].reshape(N_FF_CHUNKS, FF_CHUNK, D_MODEL)
    third = ("conv_w", "w_conv_out", "w_attn_out", "w_mix_out")
    (h1, u), got = _ffn_down(xs, act1, w1_out, gain["mix_norm"], "ffn1_down", ride=gather(third))
    keep(full, third, got)
    w_mix = full["w_mix_in"]
    w_co, w_ao, w_mo = (full[k].reshape(D_MODEL, D_MODEL) for k in ("w_conv_out", "w_attn_out", "w_mix_out"))
    taps = _columns_from_shards(full["conv_w"][:, :3, :])
    rest = ("ffn2_w_in", "ffn2_w_out", "w_ple_gate", "w_ple_proj")
    (proj,), got = _mix_proj(u, w_mix, ride=gather(rest))
    keep(full, rest, got)
    w2_in, w2_out = full["ffn2_w_in"], full["ffn2_w_out"].reshape(N_FF_CHUNKS, FF_CHUNK, D_MODEL)
    w_pg = full["w_ple_gate"].reshape(D_MODEL, D_MODEL)
    w_pp = _columns_from_shards(full["w_ple_proj"])
    o, o_bf, a_first, beta_first, reach = _attn_fwd(proj)
    h2, n3, ycin, yc, ya, merged = _mixer_out(proj, o_bf, h1, taps, w_co, w_ao, w_mo, gain["ffn2_norm"])
    (act2, to_gate2, to_up2), _ = _ffn_up(n3, w2_in, "ffn2_up")
    (h3, n4), _ = _ffn_down(h2, act2, w2_out, gain["ple_norm"], "ffn2_down")
    dh3, ds, dpp, loss_vec, dg_final, dg_ple = _tail(h3, n4, ps, w_pg, w_pp, gain["ple_norm"], gain["final_norm"], target)

    one = lambda a: a[None]
    by_rows = lambda g, rows: g.reshape(N_SHARDS, rows // N_SHARDS, D_MODEL)
    square = WGRAD_TILE // 2
    grads["w_ple_gate"] = by_rows(_wgrad(one(n4), one(ds), "wgrad_ple_gate", tile=square), D_MODEL)
    grads["w_ple_proj"] = _shards_from_columns(_wgrad(one(ps), one(dpp), "wgrad_ple_proj")[0])
    ple = ("w_ple_gate", "w_ple_proj")
    (df2, dgate2, dup2), got = _ffn_bwd_hidden(dh3, to_gate2, to_up2, w2_out, "ffn2_bwd_hidden", ride=scatter(ple))
    keep(landed, ple, got)
    grads["ffn2_w_out"] = by_rows(_wgrad(act2, one(df2), "wgrad_ffn2_out"), D_FF)
    grads["ffn2_w_in"] = _wgrad_pieces(n3, [dgate2, dup2], "wgrad_ffn2_in", transposed=True)
    (dh2, dg_ffn2), got = _ffn_bwd_input(dh3, h2, gain["ffn2_norm"], dgate2, dup2, w2_in, "ffn2_bwd_input", ride=scatter(("ffn2_w_out",)))
    keep(landed, ("ffn2_w_out",), got)
    (dh2b, dyc, dya, dgc, dga, dcb, dcv, d_o), _ = _mixer_bwd(dh2, proj, yc, ya, taps, w_co, w_ao, w_mo)
    grads["w_mix_out"] = by_rows(_wgrad(one(merged), one(dh2b), "wgrad_mix_out", tile=square), D_MODEL)
    grads["w_conv_out"] = by_rows(_wgrad(one(ycin), one(dyc), "wgrad_conv_out", tile=square), D_MODEL)
    grads["w_attn_out"] = by_rows(_wgrad(one(o_bf), one(dya), "wgrad_attn_out", tile=square), D_MODEL)
    dcc, dcx, dtaps = _conv_bwd(dcv, proj, taps)
    grads["conv_w"] = jnp.pad(_shards_from_columns(dtaps[:3]), ((0, 0), (0, CONV_ROWS - 3), (0, 0)))
    behind_attn = ("ffn2_w_in",)
    (dq, dk, dv), got = _attn_bwd(proj, o, d_o, a_first, beta_first, reach, ride=scatter(behind_attn))
    keep(landed, behind_attn, got)
    dpieces = [dcb, dcc, dcx, dq, dk, dv, dgc, dga]
    half = N_MIX // 2
    squares = ("w_mix_out", "w_conv_out", "w_attn_out", "conv_w")
    first_half = _wgrad_pieces(u, [one(dp) for dp in dpieces[:half]], "wgrad_mix_in_a", tile=WGRAD_TILE // 2, row_parts=2, ride=scatter(squares))
    if exchange:
        first_half, got = first_half
        keep(landed, squares, got)
    tops, bottoms = zip(first_half,
                        _wgrad_pieces(u, [one(dp) for dp in dpieces[half:]], "wgrad_mix_in_b", tile=WGRAD_TILE // 2, row_parts=2))
    grads["w_mix_in top"], grads["w_mix_in bottom"] = jnp.concatenate(tops, axis=0), jnp.concatenate(bottoms, axis=0)
    (dh1, dg_mix), top = _mix_bwd(dpieces, w_mix, h1, dh2, gain["mix_norm"], ride=scatter(("w_mix_in top",)))
    (df1, dgate1, dup1), bottom = _ffn_bwd_hidden(dh1, to_gate1, to_up1, w1_out, "ffn1_bwd_hidden", ride=scatter(("w_mix_in bottom",)))
    if exchange:
        landed["w_mix_in"] = [top[0], bottom[0]]
    else:
        grads["w_mix_in"] = jnp.concatenate([grads.pop("w_mix_in top"), grads.pop("w_mix_in bottom")], axis=1)
    grads["ffn1_w_out"] = by_rows(_wgrad(act1, one(df1), "wgrad_ffn1_out"), D_FF)
    if exchange:
        grads["ffn1_w_in"], got = _wgrad_pieces(n1, [dgate1, dup1], "wgrad_ffn1_in", transposed=True, ride=scatter(("ffn1_w_out",)))
        keep(landed, ("ffn1_w_out",), got)
    else:
        grads["ffn1_w_in"] = _wgrad_pieces(n1, [dgate1, dup1], "wgrad_ffn1_in", transposed=True)
    (dx, dg_ffn1), got = _ffn_bwd_input(dh1, xs, gain["ffn1_norm"], dgate1, dup1, w1_in, "ffn1_bwd_input", ride=scatter(("ffn1_w_in",)))
    keep(landed, ("ffn1_w_in",), got)
    gain_grads = dict(ffn1_norm=dg_ffn1, mix_norm=dg_mix, ffn2_norm=dg_ffn2, ple_norm=dg_ple, final_norm=dg_final)
    return loss_vec, dx, (landed if exchange else grads), gain_grads
```

```python
import functools
import math

import jax
import jax.numpy as jnp
from jax import lax
from jax.experimental import pallas as pl
from jax.experimental.pallas import tpu as pltpu

D_MODEL = 1024
D_FF = 2816
N_SHARDS = 8
FF_CHUNK = 2 * D_FF // N_SHARDS
N_FF_CHUNKS = D_FF // FF_CHUNK
N_HEADS = 8
HEAD_DIM = 128
PLE_DIM = 256
NORM_EPS = 1e-6
N_MIX = 8
ADAM_LR, ADAM_B1, ADAM_B2, ADAM_EPS, ADAM_WD, ADAM_STEP = 0.001, 0.9, 0.999, 1e-08, 0.01, 10

TOKEN_TILE = 512
WGRAD_TILE = 4096
PROJ_TILE = 2048
ATTN_ROWS = 1024
ATTN_Q = 64
ATTN_SUB = 128
ATTN_K = 2 * ATTN_SUB
ATTN_SKIP_BELOW = -90.0

BF = jnp.bfloat16
F32 = jnp.float32
MESH = pl.DeviceIdType.MESH
NT = (((1,), (1,)), ((), ()))
TN = (((0,), (0,)), ((), ()))
S = jax.ShapeDtypeStruct
ANY = pl.BlockSpec(memory_space=pl.ANY)


def _const_spec(shape):
    nd = len(shape)
    return pl.BlockSpec(shape, lambda *_: (0,) * nd, pipeline_mode=pl.Buffered(1))


def _rows(tm, cols):
    return pl.BlockSpec((tm, cols), lambda i: (i, 0))


def _chunks(tm):
    return pl.BlockSpec((N_FF_CHUNKS, tm, FF_CHUNK), lambda i: (0, i, 0))


def _acc_spec(shape):
    nd = len(shape)
    return pl.BlockSpec(shape, lambda *_: (0,) * nd)


def _dot(a, b):
    return jnp.dot(a, b, preferred_element_type=F32)


def _dot_nt(a, b):
    return lax.dot_general(a, b, NT, preferred_element_type=F32)


def _dot_tn(a, b):
    return lax.dot_general(a, b, TN, preferred_element_type=F32)


def _rms(h, g):
    r = lax.rsqrt(jnp.mean(h * h, axis=-1, keepdims=True) + NORM_EPS)
    return h * r * g


def _rms_bwd(dn, h, g):
    r = lax.rsqrt(jnp.mean(h * h, axis=-1, keepdims=True) + NORM_EPS)
    nh = h * r
    gd = dn * g
    dh = r * (gd - nh * jnp.mean(gd * nh, axis=-1, keepdims=True))
    return dh, jnp.sum(dn * nh, axis=0, keepdims=True)


def _accumulate(ref, val):
    @pl.when(pl.program_id(0) == 0)
    def _():
        ref[...] = jnp.zeros_like(ref)
    ref[...] += val


def _place():
    x, y, c = lax.axis_index("x"), lax.axis_index("y"), lax.axis_index("c")
    return x, y, c


def _slot(px, py, pc):
    return 4 * px + 2 * py + pc


def _gather_phases(ins, outs, send_sems, recv_sems, local_sems):
    n = len(ins)

    def parties():
        x, y, c = _place()
        return (x, y, c), (x, y, 1 - c), [(1 - x, y), (x, 1 - y), (1 - x, 1 - y)], c

    def copy(a, k, block, to, src=None):
        dst = outs[a].at[_slot(*block)]
        return pltpu.make_async_remote_copy(
            src_ref=dst if src is None else src, dst_ref=dst,
            send_sem=send_sems.at[a, k], recv_sem=recv_sems.at[a, k],
            device_id=to, device_id_type=MESH)

    def own(a, me):
        return pltpu.make_async_copy(ins[a], outs[a].at[_slot(*me)], local_sems.at[a])

    def first(a, me, sibling, chips, c):
        return [copy(a, 0, me, sibling, src=ins[a])] + [copy(a, 1 + j, me, (*chip, c), src=ins[a]) for j, chip in enumerate(chips)]

    def start():
        me, sibling, chips, c = parties()
        for a in range(n):
            own(a, me).start()
        for a in range(n):
            for cp in first(a, me, sibling, chips, c):
                cp.start()

    def forward():
        me, sibling, chips, c = parties()
        for j, chip in enumerate(chips):
            for a in range(n):
                copy(a, 1 + j, (*chip, c), me).wait_recv()
                copy(a, 4 + j, (*chip, c), sibling).start()

    def finish():
        me, sibling, chips, c = parties()
        for a in range(n):
            copy(a, 0, sibling, me).wait_recv()
            for j, chip in enumerate(chips):
                copy(a, 4 + j, (*chip, 1 - c), me).wait_recv()
        for a in range(n):
            for cp in first(a, me, sibling, chips, c) + [copy(a, 4 + j, (*chip, c), sibling) for j, chip in enumerate(chips)]:
                cp.wait_send()
            own(a, me).wait()

    return [start, forward, finish]


def _scatter_phases(ins, outs, send_sems, recv_sems, local_sems):
    n = len(ins)

    def copies():
        x, y, c = _place()
        me = _slot(x, y, c)
        out = [pltpu.make_async_copy(ins[a].at[me], outs[a].at[me], local_sems.at[a]) for a in range(n)]
        for k in range(1, N_SHARDS):
            px = 1 - x if k & 4 else x
            py = 1 - y if k & 2 else y
            pc = 1 - c if k & 1 else c
            for a in range(n):
                out.append(pltpu.make_async_remote_copy(
                    src_ref=ins[a].at[_slot(px, py, pc)], dst_ref=outs[a].at[me],
                    send_sem=send_sems.at[a, k - 1], recv_sem=recv_sems.at[a, k - 1],
                    device_id=(px, py, pc), device_id_type=MESH))
        return out

    def start():
        for cp in copies():
            cp.start()

    def finish():
        for cp in copies():
            cp.wait()

    return [start, finish]


def _pallas(body, *, name, grid, in_specs, out_specs, out_shape, args, scratch_shapes=(), ride=None):
    if ride is None:
        outs = pl.pallas_call(body, name=name, grid=grid, in_specs=in_specs, out_specs=out_specs, out_shape=out_shape,
                              scratch_shapes=list(scratch_shapes))(*args)
        return list(outs), []
    kind, arrays = ride
    n, n_in, n_out, n_scr = len(arrays), len(in_specs), len(out_specs), len(scratch_shapes)
    total = math.prod(grid)
    middle = (9 * total) // 10
    landed_shape = [S((N_SHARDS,) + a.shape if kind == "gather" else a.shape, a.dtype) for a in arrays]

    def with_exchange(*refs):
        ins, riders_in = refs[:n_in], refs[n_in:n_in + n]
        outs, riders_out = refs[n_in + n:n_in + n + n_out], refs[n_in + n + n_out:n_in + 2 * n + n_out]
        scratch, sems = refs[n_in + 2 * n + n_out:n_in + 2 * n + n_out + n_scr], refs[n_in + 2 * n + n_out + n_scr:]
        step = 0
        for axis, size in enumerate(grid):
            step = step * size + pl.program_id(axis)
        phases = (_gather_phases if kind == "gather" else _scatter_phases)(riders_in, riders_out, *sems)
        pl.when(step == 0)(phases[0])
        body(*ins, *outs, *scratch)
        for phase in phases[1:-1]:
            pl.when(step == middle)(phase)
        pl.when(step == total - 1)(phases[-1])

    outs = pl.pallas_call(
        with_exchange, name=name, grid=grid,
        in_specs=list(in_specs) + [ANY] * n, out_specs=list(out_specs) + [ANY] * n,
        out_shape=list(out_shape) + landed_shape,
        scratch_shapes=list(scratch_shapes) + [pltpu.SemaphoreType.DMA((n, 7)), pltpu.SemaphoreType.DMA((n, 7)),
                                               pltpu.SemaphoreType.DMA((n,))],
    )(*args, *arrays)
    return list(outs[:n_out]), list(outs[n_out:])


def _exchange_alone(kind, arrays, name):
    return _pallas(lambda: None, name=name, grid=(1,), in_specs=[], out_specs=[], out_shape=[], args=[], ride=(kind, arrays))[1]


def _prenorm(x, g, ride=None):
    t = x.shape[0]
    tm = min(TOKEN_TILE, t)

    def body(x_ref, g_ref, n_ref):
        n_ref[...] = _rms(x_ref[...], g_ref[...]).astype(BF)

    return _pallas(
        body, name="prenorm", grid=(t // tm,), ride=ride,
        in_specs=[_rows(tm, D_MODEL), _const_spec((1, D_MODEL))], out_specs=[_rows(tm, D_MODEL)],
        out_shape=[S((t, D_MODEL), BF)], args=[x, g])


def _ffn_up(n, w_in, name, ride=None):
    t = n.shape[0]
    tm = min(TOKEN_TILE, t)

    def body(n_ref, win_ref, act_ref, to_gate_ref, to_up_ref):
        nb = n_ref[...]
        for c in range(N_FF_CHUNKS):
            gate = _dot(nb, win_ref[c])
            up = _dot(nb, win_ref[N_FF_CHUNKS + c])
            sg = jax.nn.sigmoid(gate)
            silu = gate * sg
            act_ref[c] = (silu * up).astype(BF)
            to_gate_ref[c] = (up * (sg * (1.0 + gate * (1.0 - sg)))).astype(BF)
            to_up_ref[c] = silu.astype(BF)

    return _pallas(
        body, name=name, grid=(t // tm,), ride=ride,
        in_specs=[_rows(tm, D_MODEL), _const_spec(w_in.shape)],
        out_specs=[_chunks(tm)] * 3, out_shape=[S((N_FF_CHUNKS, t, FF_CHUNK), BF)] * 3,
        args=[n, w_in])


def _ffn_down(h, act, w_out, g_next, name, ride=None):
    t = h.shape[0]
    tm = min(TOKEN_TILE, t)

    def body(h_ref, act_ref, wout_ref, g_ref, ho_ref, no_ref):
        acc = jnp.zeros((tm, D_MODEL), F32)
        for c in range(N_FF_CHUNKS):
            acc = acc + _dot(act_ref[c], wout_ref[c])
        ho = h_ref[...] + 0.5 * acc
        ho_ref[...] = ho
        no_ref[...] = _rms(ho, g_ref[...]).astype(BF)

    return _pallas(
        body, name=name, grid=(t // tm,), ride=ride,
        in_specs=[_rows(tm, D_MODEL), _chunks(tm), _const_spec(w_out.shape), _const_spec((1, D_MODEL))],
        out_specs=[_rows(tm, D_MODEL)] * 2, out_shape=[S((t, D_MODEL), F32), S((t, D_MODEL), BF)],
        args=[h, act, w_out, g_next])


def _mix_proj(u, w_mix, ride=None):
    t = u.shape[0]
    tm = min(PROJ_TILE, t)

    def body(u_ref, w_ref, o_ref):
        o_ref[0] = _dot(u_ref[...], w_ref[0]).astype(BF)

    return _pallas(
        body, name="mix_proj", grid=(N_MIX, t // tm), ride=ride,
        in_specs=[pl.BlockSpec((tm, D_MODEL), lambda d, i: (i, 0)), pl.BlockSpec((1, D_MODEL, D_MODEL), lambda d, i: (d, 0, 0))],
        out_specs=[pl.BlockSpec((1, tm, D_MODEL), lambda d, i: (d, i, 0))],
        out_shape=[S((N_MIX, t, D_MODEL), BF)], args=[u, w_mix])


HALO = 16


def _piece(d, tm):
    return pl.BlockSpec((1, tm, D_MODEL), lambda i: (d, i, 0))


def _prev_halo(d, tm):
    return pl.BlockSpec((1, HALO, D_MODEL), lambda i: (d, jnp.maximum(i * (tm // HALO) - 1, 0), 0))


def _shift_down(m, prev_tail, k):
    tm = m.shape[0]
    out = pltpu.roll(m, k, 0)
    row = lax.broadcasted_iota(jnp.int32, (tm, 1), 0)
    for j in range(k):
        out = jnp.where(row == j, prev_tail[HALO - k + j:HALO - k + j + 1, :], out)
    return out


def _conv_inputs(cc_ref, cx_ref, cch_ref, cxh_ref):
    m = cc_ref[0].astype(F32) * cx_ref[0].astype(F32)
    mh = cch_ref[0].astype(F32) * cxh_ref[0].astype(F32)
    mh = jnp.where(pl.program_id(0) == 0, 0.0, mh)
    return m, _shift_down(m, mh, 1), _shift_down(m, mh, 2)


def _mixer_out(proj, o, h1, conv_w, w_co, w_ao, w_mo, g_next):
    t = h1.shape[0]
    tm = min(TOKEN_TILE, t)

    def body(cb_ref, cc_ref, cx_ref, gc_ref, ga_ref, cch_ref, cxh_ref, o_ref, h_ref, cw_ref, wco_ref, wao_ref, wmo_ref,
             g_ref, ho_ref, no_ref, ycin_ref, yc_ref, ya_ref, mg_ref):
        m, m1, m2 = _conv_inputs(cc_ref, cx_ref, cch_ref, cxh_ref)
        cw = cw_ref[...]
        cv = cw[0:1, :] * m2 + cw[1:2, :] * m1 + cw[2:3, :] * m
        ycin = (cb_ref[0].astype(F32) * cv).astype(BF)
        ycin_ref[...] = ycin
        yc = _dot(ycin, wco_ref[...])
        ya = _dot(o_ref[...], wao_ref[...])
        yc_ref[...] = yc.astype(BF)
        ya_ref[...] = ya.astype(BF)
        merged = (jax.nn.sigmoid(gc_ref[0].astype(F32)) * yc + jax.nn.sigmoid(ga_ref[0].astype(F32)) * ya).astype(BF)
        mg_ref[...] = merged
        ho = h_ref[...] + _dot(merged, wmo_ref[...])
        ho_ref[...] = ho
        no_ref[...] = _rms(ho, g_ref[...]).astype(BF)

    sq = (D_MODEL, D_MODEL)
    return pl.pallas_call(
        body, name="mixer_out", grid=(t // tm,),
        in_specs=[_piece(0, tm), _piece(1, tm), _piece(2, tm), _piece(6, tm), _piece(7, tm), _prev_halo(1, tm), _prev_halo(2, tm),
                  _rows(tm, D_MODEL), _rows(tm, D_MODEL), _const_spec((3, D_MODEL)), _const_spec(sq), _const_spec(sq),
                  _const_spec(sq), _const_spec((1, D_MODEL))],
        out_specs=[_rows(tm, D_MODEL)] * 6,
        out_shape=[S((t, D_MODEL), F32)] + [S((t, D_MODEL), BF)] * 5,
    )(proj, proj, proj, proj, proj, proj, proj, o, h1, conv_w, w_co, w_ao, w_mo, g_next)


def _suffix_sums(vals, tri, before):
    out, right = [], before
    for b in reversed(range(ATTN_K // ATTN_SUB)):
        v = vals[:, b * ATTN_SUB:(b + 1) * ATTN_SUB]
        hi = v.astype(BF)
        lo = (v - hi.astype(F32)).astype(BF)
        out.append(_dot(hi, tri) + _dot(lo, tri) + right)
        right = right + jnp.sum(v, axis=1, keepdims=True)
    return jnp.concatenate(out[::-1], axis=1), right


ATTN_UNITS = ATTN_ROWS // ATTN_Q


def _unit_rows(x, u):
    return x[u * ATTN_Q:(u + 1) * ATTN_Q]


def _per_unit(fn):
    return jnp.concatenate([fn(u) for u in range(ATTN_UNITS)], axis=0)


def _per_row(vals):
    local = lax.broadcasted_iota(jnp.int32, (ATTN_ROWS, 1), 0)
    out = jnp.full((ATTN_ROWS, 1), vals[0], jnp.int32)
    for u in range(1, ATTN_UNITS):
        out = jnp.where(local >= u * ATTN_Q, vals[u], out)
    return out


def _attn_step(q, k_ref, starts, bounds, row):
    z = _per_unit(lambda u: _dot_nt(_unit_rows(q, u), k_ref[0, pl.ds(starts[u], ATTN_K), :])) * (1.0 / math.sqrt(HEAD_DIM))
    mask = lax.broadcasted_iota(jnp.int32, (1, ATTN_K), 1) < jnp.minimum(row, _per_row(bounds)) - _per_row(starts)
    log_beta = jnp.minimum(z, 0.0) - jnp.log(1.0 + jnp.exp(jnp.minimum(z, -z)))
    log_rest = jnp.where(mask, log_beta - z, 0.0)
    return z, mask, log_beta, log_rest


def _attn_sweep_start(i, t):
    blks = tuple(jnp.maximum(i * ATTN_UNITS + u + 1 - ATTN_K // ATTN_Q, 0) for u in range(ATTN_UNITS))
    return blks, tuple(jnp.int32(t) for _ in range(ATTN_UNITS))


def _attn_keys(blks):
    return [pl.multiple_of(b * ATTN_Q, ATTN_Q) for b in blks]


def _attn_next(blks):
    return tuple(jnp.maximum(b - ATTN_K // ATTN_Q, 0) for b in blks), tuple(b * ATTN_Q for b in blks)


def _attn_reach(run, blks):
    done_rows = sum(jnp.where(b > 0, 0, ATTN_Q) for b in blks)
    local = lax.broadcasted_iota(jnp.int32, (ATTN_ROWS, 1), 0)
    return jnp.max(jnp.where(local >= done_rows, run, float(jnp.finfo(F32).min)))


def _attn_more(carry):
    return carry[-1] > ATTN_SKIP_BELOW


def _tri(strict):
    r = lax.broadcasted_iota(jnp.int32, (ATTN_SUB, ATTN_SUB), 0)
    c = lax.broadcasted_iota(jnp.int32, (ATTN_SUB, ATTN_SUB), 1)
    return (r > c if strict else r >= c).astype(BF)


REACH_TILE = (8, 128)


def _first_step_spec():
    return pl.BlockSpec((1, ATTN_ROWS, ATTN_K), lambda h, i: (h, i, 0))


def _reach_spec():
    return pl.BlockSpec((1, 1) + REACH_TILE, lambda h, i: (h, i, 0, 0))


def _head_cols(piece):
    return lambda t: pl.BlockSpec((1, t, HEAD_DIM), lambda h, i: (piece, 0, h))


def _attn_fwd(proj):
    t = proj.shape[1]
    nq = t // ATTN_ROWS
    tri = _tri(strict=True)

    def body(q_ref, k_ref, v_ref, tri_ref, o_ref, ob_ref, a_ref, beta_ref, reach_ref):
        i = pl.program_id(1)
        q = q_ref[0]
        row = i * ATTN_ROWS + lax.broadcasted_iota(jnp.int32, (ATTN_ROWS, 1), 0)

        def step(carry, keep=False):
            blks, bounds, acc, run, _ = carry
            starts = _attn_keys(blks)
            _, mask, log_beta, log_rest = _attn_step(q, k_ref, starts, bounds, row)
            tail, run = _suffix_sums(log_rest, tri_ref[...], run)
            a = jnp.where(mask, jnp.exp(log_beta + tail), 0.0).astype(BF)
            if keep:
                a_ref[0] = a
                beta_ref[0] = jnp.where(mask, jnp.exp(log_beta), 0.0).astype(BF)
            acc = acc + _per_unit(lambda u: _dot(_unit_rows(a, u), v_ref[0, pl.ds(starts[u], ATTN_K), :]))
            return (*_attn_next(blks), acc, run, _attn_reach(run, blks))

        first = (*_attn_sweep_start(i, t), jnp.zeros((ATTN_ROWS, HEAD_DIM), F32), jnp.zeros((ATTN_ROWS, 1), F32), jnp.float32(0.0))
        after_first = step(first, keep=True)
        reach_ref[...] = jnp.full(reach_ref.shape, after_first[-1], F32)
        o = lax.while_loop(_attn_more, step, after_first)[2]
        o_ref[...] = o
        ob_ref[...] = o.astype(BF)

    qspec = pl.BlockSpec((1, ATTN_ROWS, HEAD_DIM), lambda h, i: (3, i, h))
    rowblk = pl.BlockSpec((ATTN_ROWS, HEAD_DIM), lambda h, i: (i, h))
    return pl.pallas_call(
        body, name="attn_fwd", grid=(N_HEADS, nq),
        in_specs=[qspec, _head_cols(4)(t), _head_cols(5)(t), pl.BlockSpec((ATTN_SUB, ATTN_SUB), lambda h, i: (0, 0))],
        out_specs=[rowblk, rowblk, _first_step_spec(), _first_step_spec(), _reach_spec()],
        out_shape=[S((t, D_MODEL), F32), S((t, D_MODEL), BF), S((N_HEADS, t, ATTN_K), BF), S((N_HEADS, t, ATTN_K), BF),
                   S((N_HEADS, nq) + REACH_TILE, F32)],
    )(proj, proj, proj, tri)


def _attn_bwd(proj, o, d_o, a_first, beta_first, reach, ride=None):
    t = proj.shape[1]
    nq = t // ATTN_ROWS
    tri_strict, tri_incl = _tri(strict=True), _tri(strict=False)
    scale = 1.0 / math.sqrt(HEAD_DIM)

    def body(q_ref, k_ref, v_ref, o_ref, do_ref, a_ref, beta_ref, reach_ref, tris_ref, trii_ref, dq_ref, dk_ref, dv_ref, dk_acc, dv_acc):
        i = pl.program_id(1)

        @pl.when(i == 0)
        def _():
            dk_acc[...] = jnp.zeros_like(dk_acc)
            dv_acc[...] = jnp.zeros_like(dv_acc)

        q = q_ref[0]
        do = do_ref[...]
        total = jnp.sum(do.astype(F32) * o_ref[...], axis=1, keepdims=True)
        zero = jnp.zeros((ATTN_ROWS, 1), F32)
        blks0, bounds0 = _attn_sweep_start(i, t)

        def finish(starts, a, dz, dq):
            dzb = (dz * scale).astype(BF)
            for u in range(ATTN_UNITS):
                dv_acc[pl.ds(starts[u], ATTN_K), :] += _dot_tn(_unit_rows(a, u), _unit_rows(do, u))
                dk_acc[pl.ds(starts[u], ATTN_K), :] += _dot_tn(_unit_rows(dzb, u), _unit_rows(q, u))
            return dq + _per_unit(lambda u: _dot(_unit_rows(dzb, u), k_ref[0, pl.ds(starts[u], ATTN_K), :]))

        def grad_a(starts, a):
            return _per_unit(lambda u: _dot_nt(_unit_rows(do, u), v_ref[0, pl.ds(starts[u], ATTN_K), :])) * a.astype(F32)

        one_step = jnp.max(reach_ref[...]) <= ATTN_SKIP_BELOW

        @pl.when(one_step)
        def _():
            starts = _attn_keys(blks0)
            a = a_ref[0]
            beta = beta_ref[0].astype(F32)
            de = grad_a(starts, a)
            right, _ = _suffix_sums(de, trii_ref[...], zero)
            dz = de * (1.0 - beta) - (total - right) * beta
            dq_ref[...] = finish(starts, a, dz, jnp.zeros((ATTN_ROWS, HEAD_DIM), F32)).astype(BF)

        @pl.when(jnp.logical_not(one_step))
        def _():
            row = i * ATTN_ROWS + lax.broadcasted_iota(jnp.int32, (ATTN_ROWS, 1), 0)

            def step(carry):
                blks, bounds, dq, seen, run, _ = carry
                starts = _attn_keys(blks)
                z, mask, log_beta, log_rest = _attn_step(q, k_ref, starts, bounds, row)
                tail, run = _suffix_sums(log_rest, tris_ref[...], run)
                a = jnp.where(mask, jnp.exp(log_beta + tail), 0.0).astype(BF)
                de = grad_a(starts, a)
                right, seen = _suffix_sums(de, trii_ref[...], seen)
                beta = jax.nn.sigmoid(z)
                dz = jnp.where(mask, de * (1.0 - beta) - (total - right) * beta, 0.0)
                return (*_attn_next(blks), finish(starts, a, dz, dq), seen, run, _attn_reach(run, blks))

            first = (blks0, bounds0, jnp.zeros((ATTN_ROWS, HEAD_DIM), F32), zero, zero, jnp.float32(0.0))
            dq_ref[...] = lax.while_loop(_attn_more, step, step(first))[2].astype(BF)

        @pl.when(i == nq - 1)
        def _():
            dk_ref[...] = dk_acc[...].astype(BF)
            dv_ref[...] = dv_acc[...].astype(BF)

    qspec = pl.BlockSpec((1, ATTN_ROWS, HEAD_DIM), lambda h, i: (3, i, h))
    rowblk = pl.BlockSpec((ATTN_ROWS, HEAD_DIM), lambda h, i: (i, h))
    head = pl.BlockSpec((t, HEAD_DIM), lambda h, i: (0, h))
    trispec = pl.BlockSpec((ATTN_SUB, ATTN_SUB), lambda h, i: (0, 0))
    return _pallas(
        body, name="attn_bwd", grid=(N_HEADS, nq), ride=ride,
        in_specs=[qspec, _head_cols(4)(t), _head_cols(5)(t), rowblk, rowblk, _first_step_spec(), _first_step_spec(), _reach_spec(),
                  trispec, trispec],
        out_specs=[rowblk, head, head],
        out_shape=[S((t, D_MODEL), BF)] * 3,
        scratch_shapes=[pltpu.VMEM((t, HEAD_DIM), F32), pltpu.VMEM((t, HEAD_DIM), F32)],
        args=[proj, proj, proj, o, d_o, a_first, beta_first, reach, tri_strict, tri_incl])


def _tail(h3, n4, p, w_pg, w_pp, g_ple, g_final, target):
    t = h3.shape[0]
    tm = min(TOKEN_TILE, t)
    steps = t // tm

    def body(h_ref, n_ref, p_ref, wpg_ref, wpp_ref, gp_ref, gf_ref, tgt_ref,
             dh_ref, ds_ref, dpp_ref, loss_ref, dgf_ref, dgp_ref):
        pg = jax.nn.sigmoid(_dot(n_ref[...], wpg_ref[...]))
        pp = _dot(p_ref[...].astype(BF), wpp_ref[...])
        h3v = h_ref[...]
        h4 = h3v + pg * pp
        gf = gf_ref[...]
        diff = _rms(h4, gf) - tgt_ref[...]
        _accumulate(loss_ref, jnp.sum(diff * diff, axis=0, keepdims=True))
        dh4, dgf = _rms_bwd(diff * (1.0 / D_MODEL), h4, gf)
        _accumulate(dgf_ref, dgf)
        dpp_ref[...] = (dh4 * pg).astype(BF)
        ds = (dh4 * pp * pg * (1.0 - pg)).astype(BF)
        ds_ref[...] = ds
        dh3, dgp = _rms_bwd(_dot_nt(ds, wpg_ref[...]), h3v, gp_ref[...])
        _accumulate(dgp_ref, dgp)
        dh_ref[...] = dh4 + dh3

        @pl.when(pl.program_id(0) == steps - 1)
        def _():
            loss_ref[...] = jnp.full(loss_ref.shape, 0.5 / D_MODEL * jnp.sum(loss_ref[...]), F32)

    vec = (1, D_MODEL)
    return pl.pallas_call(
        body, name="tail", grid=(steps,),
        in_specs=[_rows(tm, D_MODEL), _rows(tm, D_MODEL), _rows(tm, PLE_DIM), _const_spec((D_MODEL, D_MODEL)),
                  _const_spec((PLE_DIM, D_MODEL)), _const_spec(vec), _const_spec(vec), _rows(tm, D_MODEL)],
        out_specs=[_rows(tm, D_MODEL)] * 3 + [_acc_spec(vec)] * 3,
        out_shape=[S((t, D_MODEL), F32), S((t, D_MODEL), BF), S((t, D_MODEL), BF)] + [S(vec, F32)] * 3,
    )(h3, n4, p, w_pg, w_pp, g_ple, g_final, target)


def _wgrad(xs, ys, name, ride=None, tile=None):
    bx, t, k = xs.shape
    by, _, n = ys.shape
    b = max(bx, by)
    tt = min(tile or WGRAD_TILE * 2 // xs.dtype.itemsize, t)
    steps = t // tt

    def body(x_ref, y_ref, o_ref, acc_ref):
        s = pl.program_id(1)

        @pl.when(s == 0)
        def _():
            acc_ref[...] = jnp.zeros_like(acc_ref)
        acc_ref[...] += _dot_tn(x_ref[0].astype(BF), y_ref[0].astype(BF))

        @pl.when(s == steps - 1)
        def _():
            o_ref[0] = acc_ref[...].astype(BF)

    (out,), landed = _pallas(
        body, name=name, grid=(b, steps), ride=ride,
        in_specs=[pl.BlockSpec((1, tt, k), (lambda j, s: (j, s, 0)) if bx > 1 else (lambda j, s: (0, s, 0))),
                  pl.BlockSpec((1, tt, n), (lambda j, s: (j, s, 0)) if by > 1 else (lambda j, s: (0, s, 0)))],
        out_specs=[pl.BlockSpec((1, k, n), lambda j, s: (j, 0, 0))],
        out_shape=[S((b, k, n), BF)],
        scratch_shapes=[pltpu.VMEM((k, n), F32)],
        args=[xs, ys])
    return (out, landed) if ride is not None else out


def _wgrad_pieces(x, ys, name, ride=None, tile=None, row_parts=1, transposed=False):
    t, k = x.shape
    n = ys[0].shape[2]
    counts = [y.shape[0] for y in ys]
    offsets = [sum(counts[:j]) for j in range(len(ys))]
    total = sum(counts)
    tt = min(tile or WGRAD_TILE, t)
    steps = t // tt
    rows, cols = (n, k) if transposed else (k, n)
    kp = rows // row_parts

    def body(x_ref, *refs):
        y_refs, o_refs, acc_ref = refs[:len(ys)], refs[len(ys):len(ys) + row_parts], refs[len(ys) + row_parts]
        p, s = pl.program_id(0), pl.program_id(1)

        @pl.when(s == 0)
        def _():
            acc_ref[...] = jnp.zeros_like(acc_ref)
        for j, y_ref in enumerate(y_refs):
            @pl.when(jnp.logical_and(p >= offsets[j], p < offsets[j] + counts[j]))
            def _(y_ref=y_ref):
                acc_ref[...] += _dot_tn(y_ref[0], x_ref[...]) if transposed else _dot_tn(x_ref[...], y_ref[0])

        @pl.when(s == steps - 1)
        def _():
            for part, o_ref in enumerate(o_refs):
                o_ref[0] = acc_ref[part * kp:(part + 1) * kp, :].astype(BF)

    def turn(j):
        lo, hi = offsets[j], offsets[j] + counts[j]
        return lambda p, s: (jnp.clip(p - lo, 0, counts[j] - 1), jnp.where(p < lo, 0, jnp.where(p >= hi, steps - 1, s)), 0)

    outs, landed = _pallas(
        body, name=name, grid=(total, steps), ride=ride,
        in_specs=[pl.BlockSpec((tt, k), lambda p, s: (s, 0))] + [pl.BlockSpec((1, tt, n), turn(j)) for j in range(len(ys))],
        out_specs=[pl.BlockSpec((1, kp, cols), lambda p, s: (p, 0, 0))] * row_parts,
        out_shape=[S((total, kp, cols), BF)] * row_parts,
        scratch_shapes=[pltpu.VMEM((rows, cols), F32)],
        args=[x, *ys])
    out = outs[0] if row_parts == 1 else outs
    return (out, landed) if ride is not None else out


def _ffn_bwd_hidden(dh, to_gate, to_up, w_out, name, ride=None):
    t = dh.shape[0]
    tm = min(TOKEN_TILE, t)

    def body(dh_ref, to_gate_ref, to_up_ref, wout_ref, df_ref, dgate_ref, dup_ref):
        df = (0.5 * dh_ref[...]).astype(BF)
        df_ref[...] = df
        for c in range(N_FF_CHUNKS):
            dact = _dot_nt(df, wout_ref[c])
            dgate_ref[c] = (dact * to_gate_ref[c].astype(F32)).astype(BF)
            dup_ref[c] = (dact * to_up_ref[c].astype(F32)).astype(BF)

    return _pallas(
        body, name=name, grid=(t // tm,), ride=ride,
        in_specs=[_rows(tm, D_MODEL), _chunks(tm), _chunks(tm), _const_spec(w_out.shape)],
        out_specs=[_rows(tm, D_MODEL), _chunks(tm), _chunks(tm)],
        out_shape=[S((t, D_MODEL), BF)] + [S((N_FF_CHUNKS, t, FF_CHUNK), BF)] * 2,
        args=[dh, to_gate, to_up, w_out])


def _ffn_bwd_input(dh, h_in, g, dgate, dup, w_in, name, ride=None):
    t = dh.shape[0]
    tm = min(TOKEN_TILE, t)

    def body(dh_ref, h_ref, g_ref, dgate_ref, dup_ref, win_ref, dhi_ref, dg_ref):
        dn = jnp.zeros((tm, D_MODEL), F32)
        for c in range(N_FF_CHUNKS):
            dn = dn + _dot_nt(dgate_ref[c], win_ref[c]) + _dot_nt(dup_ref[c], win_ref[N_FF_CHUNKS + c])
        dhi, dg = _rms_bwd(dn, h_ref[...], g_ref[...])
        _accumulate(dg_ref, dg)
        dhi_ref[...] = dh_ref[...] + dhi

    vec = (1, D_MODEL)
    return _pallas(
        body, name=name, grid=(t // tm,), ride=ride,
        in_specs=[_rows(tm, D_MODEL), _rows(tm, D_MODEL), _const_spec(vec), _chunks(tm), _chunks(tm), _const_spec(w_in.shape)],
        out_specs=[_rows(tm, D_MODEL), _acc_spec(vec)],
        out_shape=[S((t, D_MODEL), F32), S(vec, F32)],
        args=[dh, h_in, g, dgate, dup, w_in])


def _mixer_bwd(dh2, proj, yc, ya, conv_w, w_co, w_ao, w_mo, ride=None):
    t = dh2.shape[0]
    tm = min(TOKEN_TILE, t)

    def body(dh_ref, cb_ref, cc_ref, cx_ref, gc_ref, ga_ref, cch_ref, cxh_ref, yc_ref, ya_ref, cw_ref, wco_ref, wao_ref, wmo_ref,
             dhb_ref, dyc_ref, dya_ref, dgc_ref, dga_ref, dcb_ref, dcv_ref, do_ref):
        dhb = dh_ref[...].astype(BF)
        dhb_ref[...] = dhb
        dmerged = _dot_nt(dhb, wmo_ref[...])
        sc = jax.nn.sigmoid(gc_ref[0].astype(F32))
        sa = jax.nn.sigmoid(ga_ref[0].astype(F32))
        dyc = (dmerged * sc).astype(BF)
        dya = (dmerged * sa).astype(BF)
        dyc_ref[...] = dyc
        dya_ref[...] = dya
        dgc_ref[...] = (dmerged * yc_ref[...].astype(F32) * sc * (1.0 - sc)).astype(BF)
        dga_ref[...] = (dmerged * ya_ref[...].astype(F32) * sa * (1.0 - sa)).astype(BF)
        m, m1, m2 = _conv_inputs(cc_ref, cx_ref, cch_ref, cxh_ref)
        cw = cw_ref[...]
        cv = cw[0:1, :] * m2 + cw[1:2, :] * m1 + cw[2:3, :] * m
        dycin = _dot_nt(dyc, wco_ref[...])
        dcb_ref[...] = (dycin * cv).astype(BF)
        dcv_ref[...] = (dycin * cb_ref[0].astype(F32)).astype(BF)
        do_ref[...] = _dot_nt(dya, wao_ref[...]).astype(BF)

    sq = (D_MODEL, D_MODEL)
    return _pallas(
        body, name="mixer_bwd", grid=(t // tm,), ride=ride,
        in_specs=[_rows(tm, D_MODEL), _piece(0, tm), _piece(1, tm), _piece(2, tm), _piece(6, tm), _piece(7, tm),
                  _prev_halo(1, tm), _prev_halo(2, tm), _rows(tm, D_MODEL), _rows(tm, D_MODEL),
                  _const_spec((3, D_MODEL)), _const_spec(sq), _const_spec(sq), _const_spec(sq)],
        out_specs=[_rows(tm, D_MODEL)] * 8,
        out_shape=[S((t, D_MODEL), BF)] * 8,
        args=[dh2, proj, proj, proj, proj, proj, proj, proj, yc, ya, conv_w, w_co, w_ao, w_mo])


TAP_ROWS = 8


def _conv_bwd(dcv, proj, conv_w):
    t = dcv.shape[0]
    tm = min(TOKEN_TILE, t)
    steps = t // tm

    def body(dcv_ref, nxt_ref, cc_ref, cx_ref, cch_ref, cxh_ref, cw_ref, dcc_ref, dcx_ref, dw_ref):
        i = pl.program_id(0)
        m, m1, m2 = _conv_inputs(cc_ref, cx_ref, cch_ref, cxh_ref)
        d0 = dcv_ref[...].astype(F32)
        nxt = jnp.where(i == steps - 1, 0.0, nxt_ref[...].astype(F32))
        row = lax.broadcasted_iota(jnp.int32, (tm, 1), 0)
        d1 = jnp.where(row == tm - 1, nxt[0:1, :], pltpu.roll(d0, tm - 1, 0))
        d2 = pltpu.roll(d0, tm - 2, 0)
        d2 = jnp.where(row == tm - 2, nxt[0:1, :], jnp.where(row == tm - 1, nxt[1:2, :], d2))
        cw = cw_ref[...]
        dm = cw[2:3, :] * d0 + cw[1:2, :] * d1 + cw[0:1, :] * d2
        dcc_ref[...] = (dm * cx_ref[0].astype(F32)).astype(BF)
        dcx_ref[...] = (dm * cc_ref[0].astype(F32)).astype(BF)
        tap_row = lax.broadcasted_iota(jnp.int32, (TAP_ROWS, 1), 0)
        dw = jnp.zeros((TAP_ROWS, D_MODEL), F32)
        for j, mk in enumerate((m2, m1, m)):
            dw = jnp.where(tap_row == j, jnp.sum(d0 * mk, axis=0, keepdims=True), dw)
        _accumulate(dw_ref, dw)

    nxt_spec = pl.BlockSpec((HALO, D_MODEL), lambda i: (jnp.minimum((i + 1) * (tm // HALO), t // HALO - 1), 0))
    return pl.pallas_call(
        body, name="conv_bwd", grid=(steps,),
        in_specs=[_rows(tm, D_MODEL), nxt_spec, _piece(1, tm), _piece(2, tm), _prev_halo(1, tm), _prev_halo(2, tm),
                  _const_spec((3, D_MODEL))],
        out_specs=[_rows(tm, D_MODEL), _rows(tm, D_MODEL), _acc_spec((TAP_ROWS, D_MODEL))],
        out_shape=[S((t, D_MODEL), BF), S((t, D_MODEL), BF), S((TAP_ROWS, D_MODEL), F32)],
    )(dcv, dcv, proj, proj, proj, proj, conv_w)


def _mix_bwd(dpieces, w_mix, h1, dh2, g, ride=None):
    t = h1.shape[0]
    tm = min(TOKEN_TILE, t)

    def body(*refs):
        pieces, (w_ref, h_ref, dh_ref, g_ref, dhi_ref, dg_ref) = refs[:N_MIX], refs[N_MIX:]
        du = jnp.zeros((tm, D_MODEL), F32)
        for d in range(N_MIX):
            du = du + _dot_nt(pieces[d][...], w_ref[d])
        dhi, dg = _rms_bwd(du, h_ref[...], g_ref[...])
        _accumulate(dg_ref, dg)
        dhi_ref[...] = dh_ref[...] + dhi

    vec = (1, D_MODEL)
    return _pallas(
        body, name="mix_bwd", grid=(t // tm,), ride=ride,
        in_specs=[_rows(tm, D_MODEL)] * N_MIX + [_const_spec(w_mix.shape), _rows(tm, D_MODEL), _rows(tm, D_MODEL), _const_spec(vec)],
        out_specs=[_rows(tm, D_MODEL), _acc_spec(vec)],
        out_shape=[S((t, D_MODEL), F32), S(vec, F32)],
        args=[*dpieces, w_mix, h1, dh2, g])


def _adamw(partials, w, m, v, name):
    parts = list(partials) if isinstance(partials, (list, tuple)) else [partials]
    r, c = w.shape
    tr = next(d for d in (r, 512, 352, 256) if d <= 512 // len(parts) and r % d == 0)
    first_tile = [sum(p.shape[1] for p in parts[:j]) // tr for j in range(len(parts))]
    c1 = 1.0 - ADAM_B1 ** ADAM_STEP
    c2 = 1.0 - ADAM_B2 ** ADAM_STEP

    def body(*refs):
        p_refs, (w_ref, m_ref, v_ref, g_ref, d_ref, mo_ref, vo_ref) = refs[:len(parts)], refs[len(parts):]
        g = None
        for j, p_ref in enumerate(p_refs):
            gj = p_ref[0].astype(F32)
            for s in range(1, N_SHARDS):
                gj = gj + p_ref[s].astype(F32)
            g = gj if g is None else jnp.where(pl.program_id(0) >= first_tile[j], gj, g)
        mn = ADAM_B1 * m_ref[...] + (1.0 - ADAM_B1) * g
        vn = ADAM_B2 * v_ref[...] + (1.0 - ADAM_B2) * (g * g)
        g_ref[...] = g
        mo_ref[...] = mn
        vo_ref[...] = vn
        d_ref[...] = -ADAM_LR * ((mn / c1) / (jnp.sqrt(vn / c2) + ADAM_EPS) + ADAM_WD * w_ref[...])

    def rows_of(j):
        last = parts[j].shape[1] // tr - 1
        return lambda i: (0, jnp.clip(i - first_tile[j], 0, last), 0)

    blk = pl.BlockSpec((tr, c), lambda i: (i, 0))
    return pl.pallas_call(
        body, name=name, grid=(r // tr,),
        in_specs=[pl.BlockSpec((N_SHARDS, tr, c), rows_of(j)) for j in range(len(parts))] + [blk, blk, blk],
        out_specs=[blk] * 4, out_shape=[S((r, c), F32)] * 4,
    )(*parts, w, m, v)


_MATRICES = ("ffn1_w_in", "ffn1_w_out", "w_mix_in", "conv_w", "w_conv_out", "w_attn_out", "w_mix_out",
             "ffn2_w_in", "ffn2_w_out", "w_ple_gate", "w_ple_proj")
_GAINS = ("ffn1_norm", "mix_norm", "ffn2_norm", "ple_norm", "final_norm")
_WEIGHTS = ("ffn1_norm", "ffn1_w_in", "ffn1_w_out", "mix_norm", "w_mix_in", "conv_w", "w_conv_out", "w_attn_out", "w_mix_out",
            "ffn2_norm", "ffn2_w_in", "ffn2_w_out", "ple_norm", "w_ple_gate", "w_ple_proj", "final_norm")
CONV_ROWS = 8
_TRANSPOSED = ("ffn1_w_in", "ffn2_w_in")


def _columns_from_shards(g):
    return jnp.transpose(g, (1, 0, 2)).reshape(g.shape[1], N_SHARDS * g.shape[2])


def _shards_from_columns(a):
    r, c = a.shape
    return jnp.transpose(a.reshape(r, N_SHARDS, c // N_SHARDS), (1, 0, 2))


def kernel(x, p, ffn1_norm, ffn1_w_in, ffn1_w_out, mix_norm, w_mix_in, conv_w, w_conv_out, w_attn_out, w_mix_out, ffn2_norm, ffn2_w_in, ffn2_w_out, ple_norm, w_ple_gate, w_ple_proj, final_norm, loss_target, m_ffn1_norm, m_ffn1_w_in, m_ffn1_w_out, m_mix_norm, m_w_mix_in, m_conv_w, m_w_conv_out, m_w_attn_out, m_w_mix_out, m_ffn2_norm, m_ffn2_w_in, m_ffn2_w_out, m_ple_norm, m_w_ple_gate, m_w_ple_proj, m_final_norm, v_ffn1_norm, v_ffn1_w_in, v_ffn1_w_out, v_mix_norm, v_w_mix_in, v_conv_w, v_w_conv_out, v_w_attn_out, v_w_mix_out, v_ffn2_norm, v_ffn2_w_in, v_ffn2_w_out, v_ple_norm, v_w_ple_gate, v_w_ple_proj, v_final_norm):
    given = dict(locals())
    t = x.shape[1]
    xs = x.reshape(t, D_MODEL)
    ps = p.reshape(t, PLE_DIM)
    target = loss_target.reshape(t, D_MODEL)
    shard = {k: given[k].reshape(given[k].shape[-2:]) for k in _MATRICES}
    gain = {k: given[k].reshape(1, D_MODEL) for k in _GAINS}

    send = {k: shard[k].astype(BF) for k in _MATRICES}
    send["conv_w"] = jnp.pad(shard["conv_w"], ((0, CONV_ROWS - 3), (0, 0)))
    loss_vec, dx, landed, gain_grads = _forward_backward(xs, ps, target, gain, send)
    gain_rows = jnp.concatenate([gain_grads[k] for k in _GAINS] + [loss_vec, jnp.zeros((8 - len(_GAINS) - 1, D_MODEL), F32)], axis=0)
    gain_parts, = _exchange_alone("gather", [gain_rows], "gather_gain_gradients")

    out = {}
    for k in _MATRICES:
        w, m, v = shard[k], given["m_" + k].reshape(shard[k].shape), given["v_" + k].reshape(shard[k].shape)
        part = landed[k]
        if k == "conv_w":
            pad = ((0, CONV_ROWS - 3), (0, 0))
            w, m, v = jnp.pad(w, pad), jnp.pad(m, pad), jnp.pad(v, pad, constant_values=1.0)
        if k in _TRANSPOSED:
            w, m, v = w.T, m.T, v.T
        res = _adamw(part, w, m, v, "adamw_" + k)
        out[k] = [r[:3] if k == "conv_w" else (r.T if k in _TRANSPOSED else r) for r in res]
    stack = lambda pre: jnp.concatenate([given[pre + k].reshape(1, D_MODEL) for k in _GAINS] + [jnp.ones((8 - len(_GAINS), D_MODEL), F32)], axis=0)
    res = _adamw(gain_parts, stack(""), stack("m_"), stack("v_"), "adamw_gains")
    for j, k in enumerate(_GAINS):
        out[k] = [r[j:j + 1] for r in res]

    loss = jnp.sum(gain_parts[:, len(_GAINS), 0])
    per_kind = [[out[k][j].reshape(given[k].shape) for k in _WEIGHTS] for j in range(4)]
    return (loss, dx.reshape(x.shape), *per_kind[0], *per_kind[1], *per_kind[2], *per_kind[3])


def _forward_backward(xs, ps, target, gain, send, full=None):
    exchange = full is None
    full = dict(full or {})
    grads, landed = {}, {}

    def gather(names):
        return ("gather", [send[k] for k in names]) if exchange else None

    def scatter(names):
        return ("scatter", [grads[k] for k in names]) if exchange else None

    def keep(into, names, got):
        into.update(zip(names, got))

    first = ("ffn1_w_in",)
    (n1,), got = _prenorm(xs, gain["ffn1_norm"], ride=gather(first))
    keep(full, first, got)
    w1_in = full["ffn1_w_in"]
    second = ("ffn1_w_out", "w_mix_in")
    (act1, to_gate1, to_up1), got = _ffn_up(n1, w1_in, "ffn1_up", ride=gather(second))
    keep(full, second, got)
    w1_out = full["ffn1_w_out"].reshape(N_FF_CHUNKS, FF_CHUNK, D_MODEL)
    third = ("conv_w", "w_conv_out", "w_attn_out", "w_mix_out")
    (h1, u), got = _ffn_down(xs, act1, w1_out, gain["mix_norm"], "ffn1_down", ride=gather(third))
    keep(full, third, got)
    w_mix = full["w_mix_in"]
    w_co, w_ao, w_mo = (full[k].reshape(D_MODEL, D_MODEL) for k in ("w_conv_out", "w_attn_out", "w_mix_out"))
    taps = _columns_from_shards(full["conv_w"][:, :3, :])
    rest = ("ffn2_w_in", "ffn2_w_out", "w_ple_gate", "w_ple_proj")
    (proj,), got = _mix_proj(u, w_mix, ride=gather(rest))
    keep(full, rest, got)
    w2_in, w2_out = full["ffn2_w_in"], full["ffn2_w_out"].reshape(N_FF_CHUNKS, FF_CHUNK, D_MODEL)
    w_pg = full["w_ple_gate"].reshape(D_MODEL, D_MODEL)
    w_pp = _columns_from_shards(full["w_ple_proj"])
    o, o_bf, a_first, beta_first, reach = _attn_fwd(proj)
    h2, n3, ycin, yc, ya, merged = _mixer_out(proj, o_bf, h1, taps, w_co, w_ao, w_mo, gain["ffn2_norm"])
    (act2, to_gate2, to_up2), _ = _ffn_up(n3, w2_in, "ffn2_up")
    (h3, n4), _ = _ffn_down(h2, act2, w2_out, gain["ple_norm"], "ffn2_down")
    dh3, ds, dpp, loss_vec, dg_final, dg_ple = _tail(h3, n4, ps, w_pg, w_pp, gain["ple_norm"], gain["final_norm"], target)

    one = lambda a: a[None]
    by_rows = lambda g, rows: g.reshape(N_SHARDS, rows // N_SHARDS, D_MODEL)
    square = WGRAD_TILE // 2
    grads["w_ple_gate"] = by_rows(_wgrad(one(n4), one(ds), "wgrad_ple_gate", tile=square), D_MODEL)
    grads["w_ple_proj"] = _shards_from_columns(_wgrad(one(ps), one(dpp), "wgrad_ple_proj")[0])
    (df2, dgate2, dup2), _ = _ffn_bwd_hidden(dh3, to_gate2, to_up2, w2_out, "ffn2_bwd_hidden")
    grads["ffn2_w_out"] = by_rows(_wgrad(act2, one(df2), "wgrad_ffn2_out"), D_FF)
    early = ("w_ple_gate", "w_ple_proj", "ffn2_w_out")
    grads["ffn2_w_in"] = _wgrad_pieces(n3, [dgate2, dup2], "wgrad_ffn2_in", transposed=True, ride=scatter(early))
    if exchange:
        grads["ffn2_w_in"], got = grads["ffn2_w_in"]
        keep(landed, early, got)
    (dh2, dg_ffn2), got = _ffn_bwd_input(dh3, h2, gain["ffn2_norm"], dgate2, dup2, w2_in, "ffn2_bwd_input", ride=scatter(("ffn2_w_in",)))
    keep(landed, ("ffn2_w_in",), got)
    (dh2b, dyc, dya, dgc, dga, dcb, dcv, d_o), _ = _mixer_bwd(dh2, proj, yc, ya, taps, w_co, w_ao, w_mo)
    grads["w_mix_out"] = by_rows(_wgrad(one(merged), one(dh2b), "wgrad_mix_out", tile=square), D_MODEL)
    grads["w_conv_out"] = by_rows(_wgrad(one(ycin), one(dyc), "wgrad_conv_out", tile=square), D_MODEL)
    grads["w_attn_out"] = by_rows(_wgrad(one(o_bf), one(dya), "wgrad_attn_out", tile=square), D_MODEL)
    dcc, dcx, dtaps = _conv_bwd(dcv, proj, taps)
    grads["conv_w"] = jnp.pad(_shards_from_columns(dtaps[:3]), ((0, 0), (0, CONV_ROWS - 3), (0, 0)))
    (dq, dk, dv), _ = _attn_bwd(proj, o, d_o, a_first, beta_first, reach)
    dpieces = [dcb, dcc, dcx, dq, dk, dv, dgc, dga]
    half = N_MIX // 2
    squares = ("w_mix_out", "w_conv_out", "w_attn_out", "conv_w")
    first_half = _wgrad_pieces(u, [one(dp) for dp in dpieces[:half]], "wgrad_mix_in_a", tile=WGRAD_TILE // 2, row_parts=2, ride=scatter(squares))
    if exchange:
        first_half, got = first_half
        keep(landed, squares, got)
    tops, bottoms = zip(first_half,
                        _wgrad_pieces(u, [one(dp) for dp in dpieces[half:]], "wgrad_mix_in_b", tile=WGRAD_TILE // 2, row_parts=2))
    grads["w_mix_in top"], grads["w_mix_in bottom"] = jnp.concatenate(tops, axis=0), jnp.concatenate(bottoms, axis=0)
    (dh1, dg_mix), top = _mix_bwd(dpieces, w_mix, h1, dh2, gain["mix_norm"], ride=scatter(("w_mix_in top",)))
    (df1, dgate1, dup1), bottom = _ffn_bwd_hidden(dh1, to_gate1, to_up1, w1_out, "ffn1_bwd_hidden", ride=scatter(("w_mix_in bottom",)))
    if exchange:
        landed["w_mix_in"] = [top[0], bottom[0]]
    else:
        grads["w_mix_in"] = jnp.concatenate([grads.pop("w_mix_in top"), grads.pop("w_mix_in bottom")], axis=1)
    grads["ffn1_w_out"] = by_rows(_wgrad(act1, one(df1), "wgrad_ffn1_out"), D_FF)
    if exchange:
        grads["ffn1_w_in"], got = _wgrad_pieces(n1, [dgate1, dup1], "wgrad_ffn1_in", transposed=True, ride=scatter(("ffn1_w_out",)))
        keep(landed, ("ffn1_w_out",), got)
    else:
        grads["ffn1_w_in"] = _wgrad_pieces(n1, [dgate1, dup1], "wgrad_ffn1_in", transposed=True)
    (dx, dg_ffn1), got = _ffn_bwd_input(dh1, xs, gain["ffn1_norm"], dgate1, dup1, w1_in, "ffn1_bwd_input", ride=scatter(("ffn1_w_in",)))
    keep(landed, ("ffn1_w_in",), got)
    gain_grads = dict(ffn1_norm=dg_ffn1, mix_norm=dg_mix, ffn2_norm=dg_ffn2, ple_norm=dg_ple, final_norm=dg_final)
    return loss_vec, dx, (landed if exchange else grads), gain_grads
```

```python
import functools
import math

import jax
import jax.numpy as jnp
from jax import lax
from jax.experimental import pallas as pl
from jax.experimental.pallas import tpu as pltpu

D_MODEL = 1024
D_FF = 2816
N_SHARDS = 8
FF_CHUNK = 2 * D_FF // N_SHARDS
N_FF_CHUNKS = D_FF // FF_CHUNK
N_HEADS = 8
HEAD_DIM = 128
PLE_DIM = 256
NORM_EPS = 1e-6
N_MIX = 8
ADAM_LR, ADAM_B1, ADAM_B2, ADAM_EPS, ADAM_WD, ADAM_STEP = 0.001, 0.9, 0.999, 1e-08, 0.01, 10

TOKEN_TILE = 512
WGRAD_TILE = 4096
PROJ_TILE = 2048
ATTN_ROWS = 1024
ATTN_Q = 64
ATTN_SUB = 128
ATTN_K = 2 * ATTN_SUB
ATTN_SKIP_BELOW = -90.0

BF = jnp.bfloat16
F32 = jnp.float32
MESH = pl.DeviceIdType.MESH
NT = (((1,), (1,)), ((), ()))
TN = (((0,), (0,)), ((), ()))
S = jax.ShapeDtypeStruct
ANY = pl.BlockSpec(memory_space=pl.ANY)


def _const_spec(shape):
    nd = len(shape)
    return pl.BlockSpec(shape, lambda *_: (0,) * nd, pipeline_mode=pl.Buffered(1))


def _rows(tm, cols):
    return pl.BlockSpec((tm, cols), lambda i: (i, 0))


def _chunks(tm):
    return pl.BlockSpec((N_FF_CHUNKS, tm, FF_CHUNK), lambda i: (0, i, 0))


def _acc_spec(shape):
    nd = len(shape)
    return pl.BlockSpec(shape, lambda *_: (0,) * nd)


def _dot(a, b):
    return jnp.dot(a, b, preferred_element_type=F32)


def _dot_nt(a, b):
    return lax.dot_general(a, b, NT, preferred_element_type=F32)


def _dot_tn(a, b):
    return lax.dot_general(a, b, TN, preferred_element_type=F32)


def _rms(h, g):
    r = lax.rsqrt(jnp.mean(h * h, axis=-1, keepdims=True) + NORM_EPS)
    return h * r * g


def _rms_bwd(dn, h, g):
    r = lax.rsqrt(jnp.mean(h * h, axis=-1, keepdims=True) + NORM_EPS)
    nh = h * r
    gd = dn * g
    dh = r * (gd - nh * jnp.mean(gd * nh, axis=-1, keepdims=True))
    return dh, jnp.sum(dn * nh, axis=0, keepdims=True)


def _accumulate(ref, val):
    @pl.when(pl.program_id(0) == 0)
    def _():
        ref[...] = jnp.zeros_like(ref)
    ref[...] += val


def _place():
    x, y, c = lax.axis_index("x"), lax.axis_index("y"), lax.axis_index("c")
    return x, y, c


def _slot(px, py, pc):
    return 4 * px + 2 * py + pc


def _gather_phases(ins, outs, send_sems, recv_sems, local_sems):
    n = len(ins)

    def parties():
        x, y, c = _place()
        return (x, y, c), (x, y, 1 - c), [(1 - x, y), (x, 1 - y), (1 - x, 1 - y)], c

    def copy(a, k, block, to, src=None):
        dst = outs[a].at[_slot(*block)]
        return pltpu.make_async_remote_copy(
            src_ref=dst if src is None else src, dst_ref=dst,
            send_sem=send_sems.at[a, k], recv_sem=recv_sems.at[a, k],
            device_id=to, device_id_type=MESH)

    def own(a, me):
        return pltpu.make_async_copy(ins[a], outs[a].at[_slot(*me)], local_sems.at[a])

    def first(a, me, sibling, chips, c):
        return [copy(a, 0, me, sibling, src=ins[a])] + [copy(a, 1 + j, me, (*chip, c), src=ins[a]) for j, chip in enumerate(chips)]

    def start():
        me, sibling, chips, c = parties()
        for a in range(n):
            own(a, me).start()
        for a in range(n):
            for cp in first(a, me, sibling, chips, c):
                cp.start()

    def forward():
        me, sibling, chips, c = parties()
        for j, chip in enumerate(chips):
            for a in range(n):
                copy(a, 1 + j, (*chip, c), me).wait_recv()
                copy(a, 4 + j, (*chip, c), sibling).start()

    def finish():
        me, sibling, chips, c = parties()
        for a in range(n):
            copy(a, 0, sibling, me).wait_recv()
            for j, chip in enumerate(chips):
                copy(a, 4 + j, (*chip, 1 - c), me).wait_recv()
        for a in range(n):
            for cp in first(a, me, sibling, chips, c) + [copy(a, 4 + j, (*chip, c), sibling) for j, chip in enumerate(chips)]:
                cp.wait_send()
            own(a, me).wait()

    return [start, forward, finish]


def _scatter_phases(ins, outs, send_sems, recv_sems, local_sems):
    n = len(ins)

    def copies():
        x, y, c = _place()
        me = _slot(x, y, c)
        out = [pltpu.make_async_copy(ins[a].at[me], outs[a].at[me], local_sems.at[a]) for a in range(n)]
        for k in range(1, N_SHARDS):
            px = 1 - x if k & 4 else x
            py = 1 - y if k & 2 else y
            pc = 1 - c if k & 1 else c
            for a in range(n):
                out.append(pltpu.make_async_remote_copy(
                    src_ref=ins[a].at[_slot(px, py, pc)], dst_ref=outs[a].at[me],
                    send_sem=send_sems.at[a, k - 1], recv_sem=recv_sems.at[a, k - 1],
                    device_id=(px, py, pc), device_id_type=MESH))
        return out

    def start():
        for cp in copies():
            cp.start()

    def finish():
        for cp in copies():
            cp.wait()

    return [start, finish]


def _pallas(body, *, name, grid, in_specs, out_specs, out_shape, args, scratch_shapes=(), ride=None):
    if ride is None:
        outs = pl.pallas_call(body, name=name, grid=grid, in_specs=in_specs, out_specs=out_specs, out_shape=out_shape,
                              scratch_shapes=list(scratch_shapes))(*args)
        return list(outs), []
    kind, arrays = ride
    n, n_in, n_out, n_scr = len(arrays), len(in_specs), len(out_specs), len(scratch_shapes)
    total = math.prod(grid)
    middle = (9 * total) // 10
    landed_shape = [S((N_SHARDS,) + a.shape if kind == "gather" else a.shape, a.dtype) for a in arrays]

    def with_exchange(*refs):
        ins, riders_in = refs[:n_in], refs[n_in:n_in + n]
        outs, riders_out = refs[n_in + n:n_in + n + n_out], refs[n_in + n + n_out:n_in + 2 * n + n_out]
        scratch, sems = refs[n_in + 2 * n + n_out:n_in + 2 * n + n_out + n_scr], refs[n_in + 2 * n + n_out + n_scr:]
        step = 0
        for axis, size in enumerate(grid):
            step = step * size + pl.program_id(axis)
        phases = (_gather_phases if kind == "gather" else _scatter_phases)(riders_in, riders_out, *sems)
        pl.when(step == 0)(phases[0])
        body(*ins, *outs, *scratch)
        for phase in phases[1:-1]:
            pl.when(step == middle)(phase)
        pl.when(step == total - 1)(phases[-1])

    outs = pl.pallas_call(
        with_exchange, name=name, grid=grid,
        in_specs=list(in_specs) + [ANY] * n, out_specs=list(out_specs) + [ANY] * n,
        out_shape=list(out_shape) + landed_shape,
        scratch_shapes=list(scratch_shapes) + [pltpu.SemaphoreType.DMA((n, 7)), pltpu.SemaphoreType.DMA((n, 7)),
                                               pltpu.SemaphoreType.DMA((n,))],
    )(*args, *arrays)
    return list(outs[:n_out]), list(outs[n_out:])


def _exchange_alone(kind, arrays, name):
    return _pallas(lambda: None, name=name, grid=(1,), in_specs=[], out_specs=[], out_shape=[], args=[], ride=(kind, arrays))[1]


def _prenorm(x, g, ride=None):
    t = x.shape[0]
    tm = min(TOKEN_TILE, t)

    def body(x_ref, g_ref, n_ref):
        n_ref[...] = _rms(x_ref[...], g_ref[...]).astype(BF)

    return _pallas(
        body, name="prenorm", grid=(t // tm,), ride=ride,
        in_specs=[_rows(tm, D_MODEL), _const_spec((1, D_MODEL))], out_specs=[_rows(tm, D_MODEL)],
        out_shape=[S((t, D_MODEL), BF)], args=[x, g])


def _ffn_up(n, w_in, name, ride=None):
    t = n.shape[0]
    tm = min(TOKEN_TILE, t)

    def body(n_ref, win_ref, act_ref, to_gate_ref, to_up_ref):
        nb = n_ref[...]
        for c in range(N_FF_CHUNKS):
            gate = _dot(nb, win_ref[c])
            up = _dot(nb, win_ref[N_FF_CHUNKS + c])
            sg = jax.nn.sigmoid(gate)
            silu = gate * sg
            act_ref[c] = (silu * up).astype(BF)
            to_gate_ref[c] = (up * (sg * (1.0 + gate * (1.0 - sg)))).astype(BF)
            to_up_ref[c] = silu.astype(BF)

    return _pallas(
        body, name=name, grid=(t // tm,), ride=ride,
        in_specs=[_rows(tm, D_MODEL), _const_spec(w_in.shape)],
        out_specs=[_chunks(tm)] * 3, out_shape=[S((N_FF_CHUNKS, t, FF_CHUNK), BF)] * 3,
        args=[n, w_in])


def _ffn_down(h, act, w_out, g_next, name, ride=None):
    t = h.shape[0]
    tm = min(TOKEN_TILE, t)

    def body(h_ref, act_ref, wout_ref, g_ref, ho_ref, no_ref):
        acc = jnp.zeros((tm, D_MODEL), F32)
        for c in range(N_FF_CHUNKS):
            acc = acc + _dot(act_ref[c], wout_ref[c])
        ho = h_ref[...] + 0.5 * acc
        ho_ref[...] = ho
        no_ref[...] = _rms(ho, g_ref[...]).astype(BF)

    return _pallas(
        body, name=name, grid=(t // tm,), ride=ride,
        in_specs=[_rows(tm, D_MODEL), _chunks(tm), _const_spec(w_out.shape), _const_spec((1, D_MODEL))],
        out_specs=[_rows(tm, D_MODEL)] * 2, out_shape=[S((t, D_MODEL), F32), S((t, D_MODEL), BF)],
        args=[h, act, w_out, g_next])


def _mix_proj(u, w_mix, ride=None):
    t = u.shape[0]
    tm = min(PROJ_TILE, t)

    def body(u_ref, w_ref, o_ref):
        o_ref[0] = _dot(u_ref[...], w_ref[0]).astype(BF)

    return _pallas(
        body, name="mix_proj", grid=(N_MIX, t // tm), ride=ride,
        in_specs=[pl.BlockSpec((tm, D_MODEL), lambda d, i: (i, 0)), pl.BlockSpec((1, D_MODEL, D_MODEL), lambda d, i: (d, 0, 0))],
        out_specs=[pl.BlockSpec((1, tm, D_MODEL), lambda d, i: (d, i, 0))],
        out_shape=[S((N_MIX, t, D_MODEL), BF)], args=[u, w_mix])


HALO = 16


def _piece(d, tm):
    return pl.BlockSpec((1, tm, D_MODEL), lambda i: (d, i, 0))


def _prev_halo(d, tm):
    return pl.BlockSpec((1, HALO, D_MODEL), lambda i: (d, jnp.maximum(i * (tm // HALO) - 1, 0), 0))


def _shift_down(m, prev_tail, k):
    tm = m.shape[0]
    out = pltpu.roll(m, k, 0)
    row = lax.broadcasted_iota(jnp.int32, (tm, 1), 0)
    for j in range(k):
        out = jnp.where(row == j, prev_tail[HALO - k + j:HALO - k + j + 1, :], out)
    return out


def _conv_inputs(cc_ref, cx_ref, cch_ref, cxh_ref):
    m = cc_ref[0].astype(F32) * cx_ref[0].astype(F32)
    mh = cch_ref[0].astype(F32) * cxh_ref[0].astype(F32)
    mh = jnp.where(pl.program_id(0) == 0, 0.0, mh)
    return m, _shift_down(m, mh, 1), _shift_down(m, mh, 2)


def _mixer_out(proj, o, h1, conv_w, w_co, w_ao, w_mo, g_next):
    t = h1.shape[0]
    tm = min(TOKEN_TILE, t)

    def body(cb_ref, cc_ref, cx_ref, gc_ref, ga_ref, cch_ref, cxh_ref, o_ref, h_ref, cw_ref, wco_ref, wao_ref, wmo_ref,
             g_ref, ho_ref, no_ref, ycin_ref, yc_ref, ya_ref, mg_ref):
        m, m1, m2 = _conv_inputs(cc_ref, cx_ref, cch_ref, cxh_ref)
        cw = cw_ref[...]
        cv = cw[0:1, :] * m2 + cw[1:2, :] * m1 + cw[2:3, :] * m
        ycin = (cb_ref[0].astype(F32) * cv).astype(BF)
        ycin_ref[...] = ycin
        yc = _dot(ycin, wco_ref[...])
        ya = _dot(o_ref[...], wao_ref[...])
        yc_ref[...] = yc.astype(BF)
        ya_ref[...] = ya.astype(BF)
        merged = (jax.nn.sigmoid(gc_ref[0].astype(F32)) * yc + jax.nn.sigmoid(ga_ref[0].astype(F32)) * ya).astype(BF)
        mg_ref[...] = merged
        ho = h_ref[...] + _dot(merged, wmo_ref[...])
        ho_ref[...] = ho
        no_ref[...] = _rms(ho, g_ref[...]).astype(BF)

    sq = (D_MODEL, D_MODEL)
    return pl.pallas_call(
        body, name="mixer_out", grid=(t // tm,),
        in_specs=[_piece(0, tm), _piece(1, tm), _piece(2, tm), _piece(6, tm), _piece(7, tm), _prev_halo(1, tm), _prev_halo(2, tm),
                  _rows(tm, D_MODEL), _rows(tm, D_MODEL), _const_spec((3, D_MODEL)), _const_spec(sq), _const_spec(sq),
                  _const_spec(sq), _const_spec((1, D_MODEL))],
        out_specs=[_rows(tm, D_MODEL)] * 6,
        out_shape=[S((t, D_MODEL), F32)] + [S((t, D_MODEL), BF)] * 5,
    )(proj, proj, proj, proj, proj, proj, proj, o, h1, conv_w, w_co, w_ao, w_mo, g_next)


def _suffix_sums(vals, tri, before):
    out, right = [], before
    for b in reversed(range(ATTN_K // ATTN_SUB)):
        v = vals[:, b * ATTN_SUB:(b + 1) * ATTN_SUB]
        hi = v.astype(BF)
        lo = (v - hi.astype(F32)).astype(BF)
        out.append(_dot(hi, tri) + _dot(lo, tri) + right)
        right = right + jnp.sum(v, axis=1, keepdims=True)
    return jnp.concatenate(out[::-1], axis=1), right


ATTN_UNITS = ATTN_ROWS // ATTN_Q


def _unit_rows(x, u):
    return x[u * ATTN_Q:(u + 1) * ATTN_Q]


def _per_unit(fn):
    return jnp.concatenate([fn(u) for u in range(ATTN_UNITS)], axis=0)


def _per_row(vals):
    local = lax.broadcasted_iota(jnp.int32, (ATTN_ROWS, 1), 0)
    out = jnp.full((ATTN_ROWS, 1), vals[0], jnp.int32)
    for u in range(1, ATTN_UNITS):
        out = jnp.where(local >= u * ATTN_Q, vals[u], out)
    return out


def _attn_step(q, k_ref, starts, bounds, row):
    z = _per_unit(lambda u: _dot_nt(_unit_rows(q, u), k_ref[0, pl.ds(starts[u], ATTN_K), :])) * (1.0 / math.sqrt(HEAD_DIM))
    mask = lax.broadcasted_iota(jnp.int32, (1, ATTN_K), 1) < jnp.minimum(row, _per_row(bounds)) - _per_row(starts)
    log_beta = jnp.minimum(z, 0.0) - jnp.log(1.0 + jnp.exp(jnp.minimum(z, -z)))
    log_rest = jnp.where(mask, log_beta - z, 0.0)
    return z, mask, log_beta, log_rest


def _attn_sweep_start(i, t):
    blks = tuple(jnp.maximum(i * ATTN_UNITS + u + 1 - ATTN_K // ATTN_Q, 0) for u in range(ATTN_UNITS))
    return blks, tuple(jnp.int32(t) for _ in range(ATTN_UNITS))


def _attn_keys(blks):
    return [pl.multiple_of(b * ATTN_Q, ATTN_Q) for b in blks]


def _attn_next(blks):
    return tuple(jnp.maximum(b - ATTN_K // ATTN_Q, 0) for b in blks), tuple(b * ATTN_Q for b in blks)


def _attn_reach(run, blks):
    done_rows = sum(jnp.where(b > 0, 0, ATTN_Q) for b in blks)
    local = lax.broadcasted_iota(jnp.int32, (ATTN_ROWS, 1), 0)
    return jnp.max(jnp.where(local >= done_rows, run, float(jnp.finfo(F32).min)))


def _attn_more(carry):
    return carry[-1] > ATTN_SKIP_BELOW


def _tri(strict):
    r = lax.broadcasted_iota(jnp.int32, (ATTN_SUB, ATTN_SUB), 0)
    c = lax.broadcasted_iota(jnp.int32, (ATTN_SUB, ATTN_SUB), 1)
    return (r > c if strict else r >= c).astype(BF)


REACH_TILE = (8, 128)


def _first_step_spec():
    return pl.BlockSpec((1, ATTN_ROWS, ATTN_K), lambda h, i: (h, i, 0))


def _reach_spec():
    return pl.BlockSpec((1, 1) + REACH_TILE, lambda h, i: (h, i, 0, 0))


def _head_cols(piece):
    return lambda t: pl.BlockSpec((1, t, HEAD_DIM), lambda h, i: (piece, 0, h))


def _attn_fwd(proj):
    t = proj.shape[1]
    nq = t // ATTN_ROWS
    tri = _tri(strict=True)

    def body(q_ref, k_ref, v_ref, tri_ref, o_ref, ob_ref, a_ref, beta_ref, reach_ref):
        i = pl.program_id(1)
        q = q_ref[0]
        row = i * ATTN_ROWS + lax.broadcasted_iota(jnp.int32, (ATTN_ROWS, 1), 0)

        def step(carry, keep=False):
            blks, bounds, acc, run, _ = carry
            starts = _attn_keys(blks)
            _, mask, log_beta, log_rest = _attn_step(q, k_ref, starts, bounds, row)
            tail, run = _suffix_sums(log_rest, tri_ref[...], run)
            a = jnp.where(mask, jnp.exp(log_beta + tail), 0.0).astype(BF)
            if keep:
                a_ref[0] = a
                beta_ref[0] = jnp.where(mask, jnp.exp(log_beta), 0.0).astype(BF)
            acc = acc + _per_unit(lambda u: _dot(_unit_rows(a, u), v_ref[0, pl.ds(starts[u], ATTN_K), :]))
            return (*_attn_next(blks), acc, run, _attn_reach(run, blks))

        first = (*_attn_sweep_start(i, t), jnp.zeros((ATTN_ROWS, HEAD_DIM), F32), jnp.zeros((ATTN_ROWS, 1), F32), jnp.float32(0.0))
        after_first = step(first, keep=True)
        reach_ref[...] = jnp.full(reach_ref.shape, after_first[-1], F32)
        o = lax.while_loop(_attn_more, step, after_first)[2]
        o_ref[...] = o
        ob_ref[...] = o.astype(BF)

    qspec = pl.BlockSpec((1, ATTN_ROWS, HEAD_DIM), lambda h, i: (3, i, h))
    rowblk = pl.BlockSpec((ATTN_ROWS, HEAD_DIM), lambda h, i: (i, h))
    return pl.pallas_call(
        body, name="attn_fwd", grid=(N_HEADS, nq),
        in_specs=[qspec, _head_cols(4)(t), _head_cols(5)(t), pl.BlockSpec((ATTN_SUB, ATTN_SUB), lambda h, i: (0, 0))],
        out_specs=[rowblk, rowblk, _first_step_spec(), _first_step_spec(), _reach_spec()],
        out_shape=[S((t, D_MODEL), F32), S((t, D_MODEL), BF), S((N_HEADS, t, ATTN_K), BF), S((N_HEADS, t, ATTN_K), BF),
                   S((N_HEADS, nq) + REACH_TILE, F32)],
    )(proj, proj, proj, tri)


def _attn_bwd(proj, o, d_o, a_first, beta_first, reach, ride=None):
    t = proj.shape[1]
    nq = t // ATTN_ROWS
    tri_strict, tri_incl = _tri(strict=True), _tri(strict=False)
    scale = 1.0 / math.sqrt(HEAD_DIM)

    def body(q_ref, k_ref, v_ref, o_ref, do_ref, a_ref, beta_ref, reach_ref, tris_ref, trii_ref, dq_ref, dk_ref, dv_ref, dk_acc, dv_acc):
        i = pl.program_id(1)

        @pl.when(i == 0)
        def _():
            dk_acc[...] = jnp.zeros_like(dk_acc)
            dv_acc[...] = jnp.zeros_like(dv_acc)

        q = q_ref[0]
        do = do_ref[...]
        total = jnp.sum(do.astype(F32) * o_ref[...], axis=1, keepdims=True)
        zero = jnp.zeros((ATTN_ROWS, 1), F32)
        blks0, bounds0 = _attn_sweep_start(i, t)

        def finish(starts, a, dz, dq):
            dzb = (dz * scale).astype(BF)
            for u in range(ATTN_UNITS):
                dv_acc[pl.ds(starts[u], ATTN_K), :] += _dot_tn(_unit_rows(a, u), _unit_rows(do, u))
                dk_acc[pl.ds(starts[u], ATTN_K), :] += _dot_tn(_unit_rows(dzb, u), _unit_rows(q, u))
            return dq + _per_unit(lambda u: _dot(_unit_rows(dzb, u), k_ref[0, pl.ds(starts[u], ATTN_K), :]))

        def grad_a(starts, a):
            return _per_unit(lambda u: _dot_nt(_unit_rows(do, u), v_ref[0, pl.ds(starts[u], ATTN_K), :])) * a.astype(F32)

        one_step = jnp.max(reach_ref[...]) <= ATTN_SKIP_BELOW

        @pl.when(one_step)
        def _():
            starts = _attn_keys(blks0)
            a = a_ref[0]
            beta = beta_ref[0].astype(F32)
            de = grad_a(starts, a)
            right, _ = _suffix_sums(de, trii_ref[...], zero)
            dz = de * (1.0 - beta) - (total - right) * beta
            dq_ref[...] = finish(starts, a, dz, jnp.zeros((ATTN_ROWS, HEAD_DIM), F32)).astype(BF)

        @pl.when(jnp.logical_not(one_step))
        def _():
            row = i * ATTN_ROWS + lax.broadcasted_iota(jnp.int32, (ATTN_ROWS, 1), 0)

            def step(carry):
                blks, bounds, dq, seen, run, _ = carry
                starts = _attn_keys(blks)
                z, mask, log_beta, log_rest = _attn_step(q, k_ref, starts, bounds, row)
                tail, run = _suffix_sums(log_rest, tris_ref[...], run)
                a = jnp.where(mask, jnp.exp(log_beta + tail), 0.0).astype(BF)
                de = grad_a(starts, a)
                right, seen = _suffix_sums(de, trii_ref[...], seen)
                beta = jax.nn.sigmoid(z)
                dz = jnp.where(mask, de * (1.0 - beta) - (total - right) * beta, 0.0)
                return (*_attn_next(blks), finish(starts, a, dz, dq), seen, run, _attn_reach(run, blks))

            first = (blks0, bounds0, jnp.zeros((ATTN_ROWS, HEAD_DIM), F32), zero, zero, jnp.float32(0.0))
            dq_ref[...] = lax.while_loop(_attn_more, step, step(first))[2].astype(BF)

        @pl.when(i == nq - 1)
        def _():
            dk_ref[...] = dk_acc[...].astype(BF)
            dv_ref[...] = dv_acc[...].astype(BF)

    qspec = pl.BlockSpec((1, ATTN_ROWS, HEAD_DIM), lambda h, i: (3, i, h))
    rowblk = pl.BlockSpec((ATTN_ROWS, HEAD_DIM), lambda h, i: (i, h))
    head = pl.BlockSpec((t, HEAD_DIM), lambda h, i: (0, h))
    trispec = pl.BlockSpec((ATTN_SUB, ATTN_SUB), lambda h, i: (0, 0))
    return _pallas(
        body, name="attn_bwd", grid=(N_HEADS, nq), ride=ride,
        in_specs=[qspec, _head_cols(4)(t), _head_cols(5)(t), rowblk, rowblk, _first_step_spec(), _first_step_spec(), _reach_spec(),
                  trispec, trispec],
        out_specs=[rowblk, head, head],
        out_shape=[S((t, D_MODEL), BF)] * 3,
        scratch_shapes=[pltpu.VMEM((t, HEAD_DIM), F32), pltpu.VMEM((t, HEAD_DIM), F32)],
        args=[proj, proj, proj, o, d_o, a_first, beta_first, reach, tri_strict, tri_incl])


def _tail(h3, n4, p, w_pg, w_pp, g_ple, g_final, target):
    t = h3.shape[0]
    tm = min(TOKEN_TILE, t)
    steps = t // tm

    def body(h_ref, n_ref, p_ref, wpg_ref, wpp_ref, gp_ref, gf_ref, tgt_ref,
             dh_ref, ds_ref, dpp_ref, loss_ref, dgf_ref, dgp_ref):
        pg = jax.nn.sigmoid(_dot(n_ref[...], wpg_ref[...]))
        pp = _dot(p_ref[...].astype(BF), wpp_ref[...])
        h3v = h_ref[...]
        h4 = h3v + pg * pp
        gf = gf_ref[...]
        diff = _rms(h4, gf) - tgt_ref[...]
        _accumulate(loss_ref, jnp.sum(diff * diff, axis=0, keepdims=True))
        dh4, dgf = _rms_bwd(diff * (1.0 / D_MODEL), h4, gf)
        _accumulate(dgf_ref, dgf)
        dpp_ref[...] = (dh4 * pg).astype(BF)
        ds = (dh4 * pp * pg * (1.0 - pg)).astype(BF)
        ds_ref[...] = ds
        dh3, dgp = _rms_bwd(_dot_nt(ds, wpg_ref[...]), h3v, gp_ref[...])
        _accumulate(dgp_ref, dgp)
        dh_ref[...] = dh4 + dh3

        @pl.when(pl.program_id(0) == steps - 1)
        def _():
            loss_ref[...] = jnp.full(loss_ref.shape, 0.5 / D_MODEL * jnp.sum(loss_ref[...]), F32)

    vec = (1, D_MODEL)
    return pl.pallas_call(
        body, name="tail", grid=(steps,),
        in_specs=[_rows(tm, D_MODEL), _rows(tm, D_MODEL), _rows(tm, PLE_DIM), _const_spec((D_MODEL, D_MODEL)),
                  _const_spec((PLE_DIM, D_MODEL)), _const_spec(vec), _const_spec(vec), _rows(tm, D_MODEL)],
        out_specs=[_rows(tm, D_MODEL)] * 3 + [_acc_spec(vec)] * 3,
        out_shape=[S((t, D_MODEL), F32), S((t, D_MODEL), BF), S((t, D_MODEL), BF)] + [S(vec, F32)] * 3,
    )(h3, n4, p, w_pg, w_pp, g_ple, g_final, target)


def _wgrad(xs, ys, name, ride=None, tile=None):
    bx, t, k = xs.shape
    by, _, n = ys.shape
    b = max(bx, by)
    tt = min(tile or WGRAD_TILE * 2 // xs.dtype.itemsize, t)
    steps = t // tt

    def body(x_ref, y_ref, o_ref, acc_ref):
        s = pl.program_id(1)

        @pl.when(s == 0)
        def _():
            acc_ref[...] = jnp.zeros_like(acc_ref)
        acc_ref[...] += _dot_tn(x_ref[0].astype(BF), y_ref[0].astype(BF))

        @pl.when(s == steps - 1)
        def _():
            o_ref[0] = acc_ref[...].astype(BF)

    (out,), landed = _pallas(
        body, name=name, grid=(b, steps), ride=ride,
        in_specs=[pl.BlockSpec((1, tt, k), (lambda j, s: (j, s, 0)) if bx > 1 else (lambda j, s: (0, s, 0))),
                  pl.BlockSpec((1, tt, n), (lambda j, s: (j, s, 0)) if by > 1 else (lambda j, s: (0, s, 0)))],
        out_specs=[pl.BlockSpec((1, k, n), lambda j, s: (j, 0, 0))],
        out_shape=[S((b, k, n), BF)],
        scratch_shapes=[pltpu.VMEM((k, n), F32)],
        args=[xs, ys])
    return (out, landed) if ride is not None else out


def _wgrad_pieces(x, ys, name, ride=None, tile=None, row_parts=1, transposed=False):
    t, k = x.shape
    n = ys[0].shape[2]
    counts = [y.shape[0] for y in ys]
    offsets = [sum(counts[:j]) for j in range(len(ys))]
    total = sum(counts)
    tt = min(tile or WGRAD_TILE, t)
    steps = t // tt
    rows, cols = (n, k) if transposed else (k, n)
    kp = rows // row_parts

    def body(x_ref, *refs):
        y_refs, o_refs, acc_ref = refs[:len(ys)], refs[len(ys):len(ys) + row_parts], refs[len(ys) + row_parts]
        p, s = pl.program_id(0), pl.program_id(1)

        @pl.when(s == 0)
        def _():
            acc_ref[...] = jnp.zeros_like(acc_ref)
        for j, y_ref in enumerate(y_refs):
            @pl.when(jnp.logical_and(p >= offsets[j], p < offsets[j] + counts[j]))
            def _(y_ref=y_ref):
                acc_ref[...] += _dot_tn(y_ref[0], x_ref[...]) if transposed else _dot_tn(x_ref[...], y_ref[0])

        @pl.when(s == steps - 1)
        def _():
            for part, o_ref in enumerate(o_refs):
                o_ref[0] = acc_ref[part * kp:(part + 1) * kp, :].astype(BF)

    def turn(j):
        lo, hi = offsets[j], offsets[j] + counts[j]
        return lambda p, s: (jnp.clip(p - lo, 0, counts[j] - 1), jnp.where(p < lo, 0, jnp.where(p >= hi, steps - 1, s)), 0)

    outs, landed = _pallas(
        body, name=name, grid=(total, steps), ride=ride,
        in_specs=[pl.BlockSpec((tt, k), lambda p, s: (s, 0))] + [pl.BlockSpec((1, tt, n), turn(j)) for j in range(len(ys))],
        out_specs=[pl.BlockSpec((1, kp, cols), lambda p, s: (p, 0, 0))] * row_parts,
        out_shape=[S((total, kp, cols), BF)] * row_parts,
        scratch_shapes=[pltpu.VMEM((rows, cols), F32)],
        args=[x, *ys])
    out = outs[0] if row_parts == 1 else outs
    return (out, landed) if ride is not None else out


def _ffn_bwd_hidden(dh, to_gate, to_up, w_out, name, ride=None):
    t = dh.shape[0]
    tm = min(TOKEN_TILE, t)

    def body(dh_ref, to_gate_ref, to_up_ref, wout_ref, df_ref, dgate_ref, dup_ref):
        df = (0.5 * dh_ref[...]).astype(BF)
        df_ref[...] = df
        for c in range(N_FF_CHUNKS):
            dact = _dot_nt(df, wout_ref[c])
            dgate_ref[c] = (dact * to_gate_ref[c].astype(F32)).astype(BF)
            dup_ref[c] = (dact * to_up_ref[c].astype(F32)).astype(BF)

    return _pallas(
        body, name=name, grid=(t // tm,), ride=ride,
        in_specs=[_rows(tm, D_MODEL), _chunks(tm), _chunks(tm), _const_spec(w_out.shape)],
        out_specs=[_rows(tm, D_MODEL), _chunks(tm), _chunks(tm)],
        out_shape=[S((t, D_MODEL), BF)] + [S((N_FF_CHUNKS, t, FF_CHUNK), BF)] * 2,
        args=[dh, to_gate, to_up, w_out])


def _ffn_bwd_input(dh, h_in, g, dgate, dup, w_in, name, ride=None):
    t = dh.shape[0]
    tm = min(TOKEN_TILE, t)

    def body(dh_ref, h_ref, g_ref, dgate_ref, dup_ref, win_ref, dhi_ref, dg_ref):
        dn = jnp.zeros((tm, D_MODEL), F32)
        for c in range(N_FF_CHUNKS):
            dn = dn + _dot_nt(dgate_ref[c], win_ref[c]) + _dot_nt(dup_ref[c], win_ref[N_FF_CHUNKS + c])
        dhi, dg = _rms_bwd(dn, h_ref[...], g_ref[...])
        _accumulate(dg_ref, dg)
        dhi_ref[...] = dh_ref[...] + dhi

    vec = (1, D_MODEL)
    return _pallas(
        body, name=name, grid=(t // tm,), ride=ride,
        in_specs=[_rows(tm, D_MODEL), _rows(tm, D_MODEL), _const_spec(vec), _chunks(tm), _chunks(tm), _const_spec(w_in.shape)],
        out_specs=[_rows(tm, D_MODEL), _acc_spec(vec)],
        out_shape=[S((t, D_MODEL), F32), S(vec, F32)],
        args=[dh, h_in, g, dgate, dup, w_in])


def _mixer_bwd(dh2, proj, yc, ya, conv_w, w_co, w_ao, w_mo, ride=None):
    t = dh2.shape[0]
    tm = min(TOKEN_TILE, t)

    def body(dh_ref, cb_ref, cc_ref, cx_ref, gc_ref, ga_ref, cch_ref, cxh_ref, yc_ref, ya_ref, cw_ref, wco_ref, wao_ref, wmo_ref,
             dhb_ref, dyc_ref, dya_ref, dgc_ref, dga_ref, dcb_ref, dcv_ref, do_ref):
        dhb = dh_ref[...].astype(BF)
        dhb_ref[...] = dhb
        dmerged = _dot_nt(dhb, wmo_ref[...])
        sc = jax.nn.sigmoid(gc_ref[0].astype(F32))
        sa = jax.nn.sigmoid(ga_ref[0].astype(F32))
        dyc = (dmerged * sc).astype(BF)
        dya = (dmerged * sa).astype(BF)
        dyc_ref[...] = dyc
        dya_ref[...] = dya
        dgc_ref[...] = (dmerged * yc_ref[...].astype(F32) * sc * (1.0 - sc)).astype(BF)
        dga_ref[...] = (dmerged * ya_ref[...].astype(F32) * sa * (1.0 - sa)).astype(BF)
        m, m1, m2 = _conv_inputs(cc_ref, cx_ref, cch_ref, cxh_ref)
        cw = cw_ref[...]
        cv = cw[0:1, :] * m2 + cw[1:2, :] * m1 + cw[2:3, :] * m
        dycin = _dot_nt(dyc, wco_ref[...])
        dcb_ref[...] = (dycin * cv).astype(BF)
        dcv_ref[...] = (dycin * cb_ref[0].astype(F32)).astype(BF)
        do_ref[...] = _dot_nt(dya, wao_ref[...]).astype(BF)

    sq = (D_MODEL, D_MODEL)
    return _pallas(
        body, name="mixer_bwd", grid=(t // tm,), ride=ride,
        in_specs=[_rows(tm, D_MODEL), _piece(0, tm), _piece(1, tm), _piece(2, tm), _piece(6, tm), _piece(7, tm),
                  _prev_halo(1, tm), _prev_halo(2, tm), _rows(tm, D_MODEL), _rows(tm, D_MODEL),
                  _const_spec((3, D_MODEL)), _const_spec(sq), _const_spec(sq), _const_spec(sq)],
        out_specs=[_rows(tm, D_MODEL)] * 8,
        out_shape=[S((t, D_MODEL), BF)] * 8,
        args=[dh2, proj, proj, proj, proj, proj, proj, proj, yc, ya, conv_w, w_co, w_ao, w_mo])


TAP_ROWS = 8


def _conv_bwd(dcv, proj, conv_w):
    t = dcv.shape[0]
    tm = min(TOKEN_TILE, t)
    steps = t // tm

    def body(dcv_ref, nxt_ref, cc_ref, cx_ref, cch_ref, cxh_ref, cw_ref, dcc_ref, dcx_ref, dw_ref):
        i = pl.program_id(0)
        m, m1, m2 = _conv_inputs(cc_ref, cx_ref, cch_ref, cxh_ref)
        d0 = dcv_ref[...].astype(F32)
        nxt = jnp.where(i == steps - 1, 0.0, nxt_ref[...].astype(F32))
        row = lax.broadcasted_iota(jnp.int32, (tm, 1), 0)
        d1 = jnp.where(row == tm - 1, nxt[0:1, :], pltpu.roll(d0, tm - 1, 0))
        d2 = pltpu.roll(d0, tm - 2, 0)
        d2 = jnp.where(row == tm - 2, nxt[0:1, :], jnp.where(row == tm - 1, nxt[1:2, :], d2))
        cw = cw_ref[...]
        dm = cw[2:3, :] * d0 + cw[1:2, :] * d1 + cw[0:1, :] * d2
        dcc_ref[...] = (dm * cx_ref[0].astype(F32)).astype(BF)
        dcx_ref[...] = (dm * cc_ref[0].astype(F32)).astype(BF)
        tap_row = lax.broadcasted_iota(jnp.int32, (TAP_ROWS, 1), 0)
        dw = jnp.zeros((TAP_ROWS, D_MODEL), F32)
        for j, mk in enumerate((m2, m1, m)):
            dw = jnp.where(tap_row == j, jnp.sum(d0 * mk, axis=0, keepdims=True), dw)
        _accumulate(dw_ref, dw)

    nxt_spec = pl.BlockSpec((HALO, D_MODEL), lambda i: (jnp.minimum((i + 1) * (tm // HALO), t // HALO - 1), 0))
    return pl.pallas_call(
        body, name="conv_bwd", grid=(steps,),
        in_specs=[_rows(tm, D_MODEL), nxt_spec, _piece(1, tm), _piece(2, tm), _prev_halo(1, tm), _prev_halo(2, tm),
                  _const_spec((3, D_MODEL))],
        out_specs=[_rows(tm, D_MODEL), _rows(tm, D_MODEL), _acc_spec((TAP_ROWS, D_MODEL))],
        out_shape=[S((t, D_MODEL), BF), S((t, D_MODEL), BF), S((TAP_ROWS, D_MODEL), F32)],
    )(dcv, dcv, proj, proj, proj, proj, conv_w)


def _mix_bwd(dpieces, w_mix, h1, dh2, g, ride=None):
    t = h1.shape[0]
    tm = min(TOKEN_TILE, t)

    def body(*refs):
        pieces, (w_ref, h_ref, dh_ref, g_ref, dhi_ref, dg_ref) = refs[:N_MIX], refs[N_MIX:]
        du = jnp.zeros((tm, D_MODEL), F32)
        for d in range(N_MIX):
            du = du + _dot_nt(pieces[d][...], w_ref[d])
        dhi, dg = _rms_bwd(du, h_ref[...], g_ref[...])
        _accumulate(dg_ref, dg)
        dhi_ref[...] = dh_ref[...] + dhi

    vec = (1, D_MODEL)
    return _pallas(
        body, name="mix_bwd", grid=(t // tm,), ride=ride,
        in_specs=[_rows(tm, D_MODEL)] * N_MIX + [_const_spec(w_mix.shape), _rows(tm, D_MODEL), _rows(tm, D_MODEL), _const_spec(vec)],
        out_specs=[_rows(tm, D_MODEL), _acc_spec(vec)],
        out_shape=[S((t, D_MODEL), F32), S(vec, F32)],
        args=[*dpieces, w_mix, h1, dh2, g])


def _adamw(partials, w, m, v, name):
    parts = list(partials) if isinstance(partials, (list, tuple)) else [partials]
    r, c = w.shape
    tr = next(d for d in (r, 512, 352, 256) if d <= 512 // len(parts) and r % d == 0)
    first_tile = [sum(p.shape[1] for p in parts[:j]) // tr for j in range(len(parts))]
    c1 = 1.0 - ADAM_B1 ** ADAM_STEP
    c2 = 1.0 - ADAM_B2 ** ADAM_STEP

    def body(*refs):
        p_refs, (w_ref, m_ref, v_ref, g_ref, d_ref, mo_ref, vo_ref) = refs[:len(parts)], refs[len(parts):]
        g = None
        for j, p_ref in enumerate(p_refs):
            gj = p_ref[0].astype(F32)
            for s in range(1, N_SHARDS):
                gj = gj + p_ref[s].astype(F32)
            g = gj if g is None else jnp.where(pl.program_id(0) >= first_tile[j], gj, g)
        mn = ADAM_B1 * m_ref[...] + (1.0 - ADAM_B1) * g
        vn = ADAM_B2 * v_ref[...] + (1.0 - ADAM_B2) * (g * g)
        g_ref[...] = g
        mo_ref[...] = mn
        vo_ref[...] = vn
        d_ref[...] = -ADAM_LR * ((mn / c1) / (jnp.sqrt(vn / c2) + ADAM_EPS) + ADAM_WD * w_ref[...])

    def rows_of(j):
        last = parts[j].shape[1] // tr - 1
        return lambda i: (0, jnp.clip(i - first_tile[j], 0, last), 0)

    blk = pl.BlockSpec((tr, c), lambda i: (i, 0))
    return pl.pallas_call(
        body, name=name, grid=(r // tr,),
        in_specs=[pl.BlockSpec((N_SHARDS, tr, c), rows_of(j)) for j in range(len(parts))] + [blk, blk, blk],
        out_specs=[blk] * 4, out_shape=[S((r, c), F32)] * 4,
    )(*parts, w, m, v)


_MATRICES = ("ffn1_w_in", "ffn1_w_out", "w_mix_in", "conv_w", "w_conv_out", "w_attn_out", "w_mix_out",
             "ffn2_w_in", "ffn2_w_out", "w_ple_gate", "w_ple_proj")
_GAINS = ("ffn1_norm", "mix_norm", "ffn2_norm", "ple_norm", "final_norm")
_WEIGHTS = ("ffn1_norm", "ffn1_w_in", "ffn1_w_out", "mix_norm", "w_mix_in", "conv_w", "w_conv_out", "w_attn_out", "w_mix_out",
            "ffn2_norm", "ffn2_w_in", "ffn2_w_out", "ple_norm", "w_ple_gate", "w_ple_proj", "final_norm")
CONV_ROWS = 8
_TRANSPOSED = ("ffn1_w_in", "ffn2_w_in")


def _columns_from_shards(g):
    return jnp.transpose(g, (1, 0, 2)).reshape(g.shape[1], N_SHARDS * g.shape[2])


def _shards_from_columns(a):
    r, c = a.shape
    return jnp.transpose(a.reshape(r, N_SHARDS, c // N_SHARDS), (1, 0, 2))


def kernel(x, p, ffn1_norm, ffn1_w_in, ffn1_w_out, mix_norm, w_mix_in, conv_w, w_conv_out, w_attn_out, w_mix_out, ffn2_norm, ffn2_w_in, ffn2_w_out, ple_norm, w_ple_gate, w_ple_proj, final_norm, loss_target, m_ffn1_norm, m_ffn1_w_in, m_ffn1_w_out, m_mix_norm, m_w_mix_in, m_conv_w, m_w_conv_out, m_w_attn_out, m_w_mix_out, m_ffn2_norm, m_ffn2_w_in, m_ffn2_w_out, m_ple_norm, m_w_ple_gate, m_w_ple_proj, m_final_norm, v_ffn1_norm, v_ffn1_w_in, v_ffn1_w_out, v_mix_norm, v_w_mix_in, v_conv_w, v_w_conv_out, v_w_attn_out, v_w_mix_out, v_ffn2_norm, v_ffn2_w_in, v_ffn2_w_out, v_ple_norm, v_w_ple_gate, v_w_ple_proj, v_final_norm):
    given = dict(locals())
    t = x.shape[1]
    xs = x.reshape(t, D_MODEL)
    ps = p.reshape(t, PLE_DIM)
    target = loss_target.reshape(t, D_MODEL)
    shard = {k: given[k].reshape(given[k].shape[-2:]) for k in _MATRICES}
    gain = {k: given[k].reshape(1, D_MODEL) for k in _GAINS}

    send = {k: shard[k].astype(BF) for k in _MATRICES}
    send["conv_w"] = jnp.pad(shard["conv_w"], ((0, CONV_ROWS - 3), (0, 0)))
    loss_vec, dx, landed, gain_grads = _forward_backward(xs, ps, target, gain, send)
    gain_rows = jnp.concatenate([gain_grads[k] for k in _GAINS] + [loss_vec, jnp.zeros((8 - len(_GAINS) - 1, D_MODEL), F32)], axis=0)
    gain_parts, = _exchange_alone("gather", [gain_rows], "gather_gain_gradients")

    out = {}
    for k in _MATRICES:
        w, m, v = shard[k], given["m_" + k].reshape(shard[k].shape), given["v_" + k].reshape(shard[k].shape)
        part = landed[k]
        if k == "conv_w":
            pad = ((0, CONV_ROWS - 3), (0, 0))
            w, m, v = jnp.pad(w, pad), jnp.pad(m, pad), jnp.pad(v, pad, constant_values=1.0)
        if k in _TRANSPOSED:
            w, m, v = w.T, m.T, v.T
        res = _adamw(part, w, m, v, "adamw_" + k)
        out[k] = [r[:3] if k == "conv_w" else (r.T if k in _TRANSPOSED else r) for r in res]
    stack = lambda pre: jnp.concatenate([given[pre + k].reshape(1, D_MODEL) for k in _GAINS] + [jnp.ones((8 - len(_GAINS), D_MODEL), F32)], axis=0)
    res = _adamw(gain_parts, stack(""), stack("m_"), stack("v_"), "adamw_gains")
    for j, k in enumerate(_GAINS):
        out[k] = [r[j:j + 1] for r in res]

    loss = jnp.sum(gain_parts[:, len(_GAINS), 0])
    per_kind = [[out[k][j].reshape(given[k].shape) for k in _WEIGHTS] for j in range(4)]
    return (loss, dx.reshape(x.shape), *per_kind[0], *per_kind[1], *per_kind[2], *per_kind[3])


def _forward_backward(xs, ps, target, gain, send, full=None):
    exchange = full is None
    full = dict(full or {})
    grads, landed = {}, {}

    def gather(names):
        return ("gather", [send[k] for k in names]) if exchange else None

    def scatter(names):
        return ("scatter", [grads[k] for k in names]) if exchange else None

    def keep(into, names, got):
        into.update(zip(names, got))

    first = ("ffn1_w_in",)
    (n1,), got = _prenorm(xs, gain["ffn1_norm"], ride=gather(first))
    keep(full, first, got)
    w1_in = full["ffn1_w_in"]
    second = ("ffn1_w_out", "w_mix_in")
    (act1, to_gate1, to_up1), got = _ffn_up(n1, w1_in, "ffn1_up", ride=gather(second))
    keep(full, second, got)
    w1_out = full["ffn1_w_out"].reshape(N_FF_CHUNKS, FF_CHUNK, D_MODEL)
    third = ("conv_w", "w_conv_out", "w_attn_out", "w_mix_out")
    (h1, u), got = _ffn_down(xs, act1, w1_out, gain["mix_norm"], "ffn1_down", ride=gather(third))
    keep(full, third, got)
    w_mix = full["w_mix_in"]
    w_co, w_ao, w_mo = (full[k].reshape(D_MODEL, D_MODEL) for k in ("w_conv_out", "w_attn_out", "w_mix_out"))
    taps = _columns_from_shards(full["conv_w"][:, :3, :])
    rest = ("ffn2_w_in", "ffn2_w_out", "w_ple_gate", "w_ple_proj")
    (proj,), got = _mix_proj(u, w_mix, ride=gather(rest))
    keep(full, rest, got)
    w2_in, w2_out = full["ffn2_w_in"], full["ffn2_w_out"].reshape(N_FF_CHUNKS, FF_CHUNK, D_MODEL)
    w_pg = full["w_ple_gate"].reshape(D_MODEL, D_MODEL)
    w_pp = _columns_from_shards(full["w_ple_proj"])
    o, o_bf, a_first, beta_first, reach = _attn_fwd(proj)
    h2, n3, ycin, yc, ya, merged = _mixer_out(proj, o_bf, h1, taps, w_co, w_ao, w_mo, gain["ffn2_norm"])
    (act2, to_gate2, to_up2), _ = _ffn_up(n3, w2_in, "ffn2_up")
    (h3, n4), _ = _ffn_down(h2, act2, w2_out, gain["ple_norm"], "ffn2_down")
    dh3, ds, dpp, loss_vec, dg_final, dg_ple = _tail(h3, n4, ps, w_pg, w_pp, gain["ple_norm"], gain["final_norm"], target)

    one = lambda a: a[None]
    by_rows = lambda g, rows: g.reshape(N_SHARDS, rows // N_SHARDS, D_MODEL)
    square = WGRAD_TILE // 2
    grads["w_ple_gate"] = by_rows(_wgrad(one(n4), one(ds), "wgrad_ple_gate", tile=square), D_MODEL)
    grads["w_ple_proj"] = _shards_from_columns(_wgrad(one(ps), one(dpp), "wgrad_ple_proj")[0])
    (df2, dgate2, dup2), _ = _ffn_bwd_hidden(dh3, to_gate2, to_up2, w2_out, "ffn2_bwd_hidden")
    grads["ffn2_w_out"] = by_rows(_wgrad(act2, one(df2), "wgrad_ffn2_out"), D_FF)
    early = ("w_ple_gate", "w_ple_proj", "ffn2_w_out")
    grads["ffn2_w_in"] = _wgrad_pieces(n3, [dgate2, dup2], "wgrad_ffn2_in", transposed=True, ride=scatter(early))
    if exchange:
        grads["ffn2_w_in"], got = grads["ffn2_w_in"]
        keep(landed, early, got)
    (dh2, dg_ffn2), _ = _ffn_bwd_input(dh3, h2, gain["ffn2_norm"], dgate2, dup2, w2_in, "ffn2_bwd_input")
    (dh2b, dyc, dya, dgc, dga, dcb, dcv, d_o), _ = _mixer_bwd(dh2, proj, yc, ya, taps, w_co, w_ao, w_mo)
    grads["w_mix_out"] = by_rows(_wgrad(one(merged), one(dh2b), "wgrad_mix_out", tile=square), D_MODEL)
    grads["w_conv_out"] = by_rows(_wgrad(one(ycin), one(dyc), "wgrad_conv_out", tile=square), D_MODEL)
    grads["w_attn_out"] = by_rows(_wgrad(one(o_bf), one(dya), "wgrad_attn_out", tile=square), D_MODEL)
    dcc, dcx, dtaps = _conv_bwd(dcv, proj, taps)
    grads["conv_w"] = jnp.pad(_shards_from_columns(dtaps[:3]), ((0, 0), (0, CONV_ROWS - 3), (0, 0)))
    behind_attn = ("ffn2_w_in",)
    (dq, dk, dv), got = _attn_bwd(proj, o, d_o, a_first, beta_first, reach, ride=scatter(behind_attn))
    keep(landed, behind_attn, got)
    dpieces = [dcb, dcc, dcx, dq, dk, dv, dgc, dga]
    half = N_MIX // 2
    squares = ("w_mix_out", "w_conv_out", "w_attn_out", "conv_w")
    first_half = _wgrad_pieces(u, [one(dp) for dp in dpieces[:half]], "wgrad_mix_in_a", tile=WGRAD_TILE // 2, row_parts=2, ride=scatter(squares))
    if exchange:
        first_half, got = first_half
        keep(landed, squares, got)
    tops, bottoms = zip(first_half,
                        _wgrad_pieces(u, [one(dp) for dp in dpieces[half:]], "wgrad_mix_in_b", tile=WGRAD_TILE // 2, row_parts=2))
    grads["w_mix_in top"], grads["w_mix_in bottom"] = jnp.concatenate(tops, axis=0), jnp.concatenate(bottoms, axis=0)
    (dh1, dg_mix), top = _mix_bwd(dpieces, w_mix, h1, dh2, gain["mix_norm"], ride=scatter(("w_mix_in top",)))
    (df1, dgate1, dup1), bottom = _ffn_bwd_hidden(dh1, to_gate1, to_up1, w1_out, "ffn1_bwd_hidden", ride=scatter(("w_mix_in bottom",)))
    if exchange:
        landed["w_mix_in"] = [top[0], bottom[0]]
    else:
        grads["w_mix_in"] = jnp.concatenate([grads.pop("w_mix_in top"), grads.pop("w_mix_in bottom")], axis=1)
    grads["ffn1_w_out"] = by_rows(_wgrad(act1, one(df1), "wgrad_ffn1_out"), D_FF)
    if exchange:
        grads["ffn1_w_in"], got = _wgrad_pieces(n1, [dgate1, dup1], "wgrad_ffn1_in", transposed=True, ride=scatter(("ffn1_w_out",)))
        keep(landed, ("ffn1_w_out",), got)
    else:
        grads["ffn1_w_in"] = _wgrad_pieces(n1, [dgate1, dup1], "wgrad_ffn1_in", transposed=True)
    (dx, dg_ffn1), got = _ffn_bwd_input(dh1, xs, gain["ffn1_norm"], dgate1, dup1, w1_in, "ffn1_bwd_input", ride=scatter(("ffn1_w_in",)))
    keep(landed, ("ffn1_w_in",), got)
    gain_grads = dict(ffn1_norm=dg_ffn1, mix_norm=dg_mix, ffn2_norm=dg_ffn2, ple_norm=dg_ple, final_norm=dg_final)
    return loss_vec, dx, (landed if exchange else grads), gain_grads
```

```python
import functools
import math

import jax
import jax.numpy as jnp
from jax import lax
from jax.experimental import pallas as pl
from jax.experimental.pallas import tpu as pltpu

D_MODEL = 1024
D_FF = 2816
N_SHARDS = 8
FF_CHUNK = 2 * D_FF // N_SHARDS
N_FF_CHUNKS = D_FF // FF_CHUNK
N_HEADS = 8
HEAD_DIM = 128
PLE_DIM = 256
NORM_EPS = 1e-6
N_MIX = 8
ADAM_LR, ADAM_B1, ADAM_B2, ADAM_EPS, ADAM_WD, ADAM_STEP = 0.001, 0.9, 0.999, 1e-08, 0.01, 10

TOKEN_TILE = 512
WGRAD_TILE = 4096
PROJ_TILE = 2048
ATTN_ROWS = 1024
ATTN_Q = 64
ATTN_SUB = 128
ATTN_K = 2 * ATTN_SUB
ATTN_SKIP_BELOW = -90.0

BF = jnp.bfloat16
F32 = jnp.float32
MESH = pl.DeviceIdType.MESH
NT = (((1,), (1,)), ((), ()))
TN = (((0,), (0,)), ((), ()))
S = jax.ShapeDtypeStruct
ANY = pl.BlockSpec(memory_space=pl.ANY)


def _const_spec(shape):
    nd = len(shape)
    return pl.BlockSpec(shape, lambda *_: (0,) * nd, pipeline_mode=pl.Buffered(1))


def _rows(tm, cols):
    return pl.BlockSpec((tm, cols), lambda i: (i, 0))


def _chunks(tm):
    return pl.BlockSpec((N_FF_CHUNKS, tm, FF_CHUNK), lambda i: (0, i, 0))


def _acc_spec(shape):
    nd = len(shape)
    return pl.BlockSpec(shape, lambda *_: (0,) * nd)


def _dot(a, b):
    return jnp.dot(a, b, preferred_element_type=F32)


def _dot_nt(a, b):
    return lax.dot_general(a, b, NT, preferred_element_type=F32)


def _dot_tn(a, b):
    return lax.dot_general(a, b, TN, preferred_element_type=F32)


def _rms(h, g):
    r = lax.rsqrt(jnp.mean(h * h, axis=-1, keepdims=True) + NORM_EPS)
    return h * r * g


def _rms_bwd(dn, h, g):
    r = lax.rsqrt(jnp.mean(h * h, axis=-1, keepdims=True) + NORM_EPS)
    nh = h * r
    gd = dn * g
    dh = r * (gd - nh * jnp.mean(gd * nh, axis=-1, keepdims=True))
    return dh, jnp.sum(dn * nh, axis=0, keepdims=True)


def _accumulate(ref, val):
    @pl.when(pl.program_id(0) == 0)
    def _():
        ref[...] = jnp.zeros_like(ref)
    ref[...] += val


def _place():
    x, y, c = lax.axis_index("x"), lax.axis_index("y"), lax.axis_index("c")
    return x, y, c


def _slot(px, py, pc):
    return 4 * px + 2 * py + pc


def _gather_phases(ins, outs, send_sems, recv_sems, local_sems):
    n = len(ins)

    def parties():
        x, y, c = _place()
        return (x, y, c), (x, y, 1 - c), [(1 - x, y), (x, 1 - y), (1 - x, 1 - y)], c

    def copy(a, k, block, to, src=None):
        dst = outs[a].at[_slot(*block)]
        return pltpu.make_async_remote_copy(
            src_ref=dst if src is None else src, dst_ref=dst,
            send_sem=send_sems.at[a, k], recv_sem=recv_sems.at[a, k],
            device_id=to, device_id_type=MESH)

    def own(a, me):
        return pltpu.make_async_copy(ins[a], outs[a].at[_slot(*me)], local_sems.at[a])

    def first(a, me, sibling, chips, c):
        return [copy(a, 0, me, sibling, src=ins[a])] + [copy(a, 1 + j, me, (*chip, c), src=ins[a]) for j, chip in enumerate(chips)]

    def start():
        me, sibling, chips, c = parties()
        for a in range(n):
            own(a, me).start()
        for a in range(n):
            for cp in first(a, me, sibling, chips, c):
                cp.start()

    def forward():
        me, sibling, chips, c = parties()
        for j, chip in enumerate(chips):
            for a in range(n):
                copy(a, 1 + j, (*chip, c), me).wait_recv()
                copy(a, 4 + j, (*chip, c), sibling).start()

    def finish():
        me, sibling, chips, c = parties()
        for a in range(n):
            copy(a, 0, sibling, me).wait_recv()
            for j, chip in enumerate(chips):
                copy(a, 4 + j, (*chip, 1 - c), me).wait_recv()
        for a in range(n):
            for cp in first(a, me, sibling, chips, c) + [copy(a, 4 + j, (*chip, c), sibling) for j, chip in enumerate(chips)]:
                cp.wait_send()
            own(a, me).wait()

    return [start, forward, finish]


def _scatter_phases(ins, outs, send_sems, recv_sems, local_sems):
    n = len(ins)

    def copies():
        x, y, c = _place()
        me = _slot(x, y, c)
        out = [pltpu.make_async_copy(ins[a].at[me], outs[a].at[me], local_sems.at[a]) for a in range(n)]
        for k in range(1, N_SHARDS):
            px = 1 - x if k & 4 else x
            py = 1 - y if k & 2 else y
            pc = 1 - c if k & 1 else c
            for a in range(n):
                out.append(pltpu.make_async_remote_copy(
                    src_ref=ins[a].at[_slot(px, py, pc)], dst_ref=outs[a].at[me],
                    send_sem=send_sems.at[a, k - 1], recv_sem=recv_sems.at[a, k - 1],
                    device_id=(px, py, pc), device_id_type=MESH))
        return out

    def start():
        for cp in copies():
            cp.start()

    def finish():
        for cp in copies():
            cp.wait()

    return [start, finish]


def _pallas(body, *, name, grid, in_specs, out_specs, out_shape, args, scratch_shapes=(), ride=None):
    if ride is None:
        outs = pl.pallas_call(body, name=name, grid=grid, in_specs=in_specs, out_specs=out_specs, out_shape=out_shape,
                              scratch_shapes=list(scratch_shapes))(*args)
        return list(outs), []
    kind, arrays = ride
    n, n_in, n_out, n_scr = len(arrays), len(in_specs), len(out_specs), len(scratch_shapes)
    total = math.prod(grid)
    middle = (9 * total) // 10
    landed_shape = [S((N_SHARDS,) + a.shape if kind == "gather" else a.shape, a.dtype) for a in arrays]

    def with_exchange(*refs):
        ins, riders_in = refs[:n_in], refs[n_in:n_in + n]
        outs, riders_out = refs[n_in + n:n_in + n + n_out], refs[n_in + n + n_out:n_in + 2 * n + n_out]
        scratch, sems = refs[n_in + 2 * n + n_out:n_in + 2 * n + n_out + n_scr], refs[n_in + 2 * n + n_out + n_scr:]
        step = 0
        for axis, size in enumerate(grid):
            step = step * size + pl.program_id(axis)
        phases = (_gather_phases if kind == "gather" else _scatter_phases)(riders_in, riders_out, *sems)
        pl.when(step == 0)(phases[0])
        body(*ins, *outs, *scratch)
        for phase in phases[1:-1]:
            pl.when(step == middle)(phase)
        pl.when(step == total - 1)(phases[-1])

    outs = pl.pallas_call(
        with_exchange, name=name, grid=grid,
        in_specs=list(in_specs) + [ANY] * n, out_specs=list(out_specs) + [ANY] * n,
        out_shape=list(out_shape) + landed_shape,
        scratch_shapes=list(scratch_shapes) + [pltpu.SemaphoreType.DMA((n, 7)), pltpu.SemaphoreType.DMA((n, 7)),
                                               pltpu.SemaphoreType.DMA((n,))],
    )(*args, *arrays)
    return list(outs[:n_out]), list(outs[n_out:])


def _exchange_alone(kind, arrays, name):
    return _pallas(lambda: None, name=name, grid=(1,), in_specs=[], out_specs=[], out_shape=[], args=[], ride=(kind, arrays))[1]


def _prenorm(x, g, ride=None):
    t = x.shape[0]
    tm = min(TOKEN_TILE, t)

    def body(x_ref, g_ref, n_ref):
        n_ref[...] = _rms(x_ref[...], g_ref[...]).astype(BF)

    return _pallas(
        body, name="prenorm", grid=(t // tm,), ride=ride,
        in_specs=[_rows(tm, D_MODEL), _const_spec((1, D_MODEL))], out_specs=[_rows(tm, D_MODEL)],
        out_shape=[S((t, D_MODEL), BF)], args=[x, g])


def _ffn_up(n, w_in, name, ride=None):
    t = n.shape[0]
    tm = min(TOKEN_TILE, t)

    def body(n_ref, win_ref, act_ref, to_gate_ref, to_up_ref):
        nb = n_ref[...]
        for c in range(N_FF_CHUNKS):
            gate = _dot(nb, win_ref[c])
            up = _dot(nb, win_ref[N_FF_CHUNKS + c])
            sg = jax.nn.sigmoid(gate)
            silu = gate * sg
            act_ref[c] = (silu * up).astype(BF)
            to_gate_ref[c] = (up * (sg * (1.0 + gate * (1.0 - sg)))).astype(BF)
            to_up_ref[c] = silu.astype(BF)

    return _pallas(
        body, name=name, grid=(t // tm,), ride=ride,
        in_specs=[_rows(tm, D_MODEL), _const_spec(w_in.shape)],
        out_specs=[_chunks(tm)] * 3, out_shape=[S((N_FF_CHUNKS, t, FF_CHUNK), BF)] * 3,
        args=[n, w_in])


def _ffn_down(h, act, w_out, g_next, name, ride=None):
    t = h.shape[0]
    tm = min(TOKEN_TILE, t)

    def body(h_ref, act_ref, wout_ref, g_ref, ho_ref, no_ref):
        acc = jnp.zeros((tm, D_MODEL), F32)
        for c in range(N_FF_CHUNKS):
            acc = acc + _dot(act_ref[c], wout_ref[c])
        ho = h_ref[...] + 0.5 * acc
        ho_ref[...] = ho
        no_ref[...] = _rms(ho, g_ref[...]).astype(BF)

    return _pallas(
        body, name=name, grid=(t // tm,), ride=ride,
        in_specs=[_rows(tm, D_MODEL), _chunks(tm), _const_spec(w_out.shape), _const_spec((1, D_MODEL))],
        out_specs=[_rows(tm, D_MODEL)] * 2, out_shape=[S((t, D_MODEL), F32), S((t, D_MODEL), BF)],
        args=[h, act, w_out, g_next])


def _mix_proj(u, w_mix, ride=None):
    t = u.shape[0]
    tm = min(PROJ_TILE, t)

    def body(u_ref, w_ref, o_ref):
        o_ref[0] = _dot(u_ref[...], w_ref[0]).astype(BF)

    return _pallas(
        body, name="mix_proj", grid=(N_MIX, t // tm), ride=ride,
        in_specs=[pl.BlockSpec((tm, D_MODEL), lambda d, i: (i, 0)), pl.BlockSpec((1, D_MODEL, D_MODEL), lambda d, i: (d, 0, 0))],
        out_specs=[pl.BlockSpec((1, tm, D_MODEL), lambda d, i: (d, i, 0))],
        out_shape=[S((N_MIX, t, D_MODEL), BF)], args=[u, w_mix])


HALO = 16


def _piece(d, tm):
    return pl.BlockSpec((1, tm, D_MODEL), lambda i: (d, i, 0))


def _prev_halo(d, tm):
    return pl.BlockSpec((1, HALO, D_MODEL), lambda i: (d, jnp.maximum(i * (tm // HALO) - 1, 0), 0))


def _shift_down(m, prev_tail, k):
    tm = m.shape[0]
    out = pltpu.roll(m, k, 0)
    row = lax.broadcasted_iota(jnp.int32, (tm, 1), 0)
    for j in range(k):
        out = jnp.where(row == j, prev_tail[HALO - k + j:HALO - k + j + 1, :], out)
    return out


def _conv_inputs(cc_ref, cx_ref, cch_ref, cxh_ref):
    m = cc_ref[0].astype(F32) * cx_ref[0].astype(F32)
    mh = cch_ref[0].astype(F32) * cxh_ref[0].astype(F32)
    mh = jnp.where(pl.program_id(0) == 0, 0.0, mh)
    return m, _shift_down(m, mh, 1), _shift_down(m, mh, 2)


def _mixer_out(proj, o, h1, conv_w, w_co, w_ao, w_mo, g_next):
    t = h1.shape[0]
    tm = min(TOKEN_TILE, t)

    def body(cb_ref, cc_ref, cx_ref, gc_ref, ga_ref, cch_ref, cxh_ref, o_ref, h_ref, cw_ref, wco_ref, wao_ref, wmo_ref,
             g_ref, ho_ref, no_ref, ycin_ref, yc_ref, ya_ref, mg_ref):
        m, m1, m2 = _conv_inputs(cc_ref, cx_ref, cch_ref, cxh_ref)
        cw = cw_ref[...]
        cv = cw[0:1, :] * m2 + cw[1:2, :] * m1 + cw[2:3, :] * m
        ycin = (cb_ref[0].astype(F32) * cv).astype(BF)
        ycin_ref[...] = ycin
        yc = _dot(ycin, wco_ref[...])
        ya = _dot(o_ref[...], wao_ref[...])
        yc_ref[...] = yc.astype(BF)
        ya_ref[...] = ya.astype(BF)
        merged = (jax.nn.sigmoid(gc_ref[0].astype(F32)) * yc + jax.nn.sigmoid(ga_ref[0].astype(F32)) * ya).astype(BF)
        mg_ref[...] = merged
        ho = h_ref[...] + _dot(merged, wmo_ref[...])
        ho_ref[...] = ho
        no_ref[...] = _rms(ho, g_ref[...]).astype(BF)

    sq = (D_MODEL, D_MODEL)
    return pl.pallas_call(
        body, name="mixer_out", grid=(t // tm,),
        in_specs=[_piece(0, tm), _piece(1, tm), _piece(2, tm), _piece(6, tm), _piece(7, tm), _prev_halo(1, tm), _prev_halo(2, tm),
                  _rows(tm, D_MODEL), _rows(tm, D_MODEL), _const_spec((3, D_MODEL)), _const_spec(sq), _const_spec(sq),
                  _const_spec(sq), _const_spec((1, D_MODEL))],
        out_specs=[_rows(tm, D_MODEL)] * 6,
        out_shape=[S((t, D_MODEL), F32)] + [S((t, D_MODEL), BF)] * 5,
    )(proj, proj, proj, proj, proj, proj, proj, o, h1, conv_w, w_co, w_ao, w_mo, g_next)


def _suffix_sums(vals, tri, before):
    out, right = [], before
    for b in reversed(range(ATTN_K // ATTN_SUB)):
        v = vals[:, b * ATTN_SUB:(b + 1) * ATTN_SUB]
        hi = v.astype(BF)
        lo = (v - hi.astype(F32)).astype(BF)
        out.append(_dot(hi, tri) + _dot(lo, tri) + right)
        right = right + jnp.sum(v, axis=1, keepdims=True)
    return jnp.concatenate(out[::-1], axis=1), right


ATTN_UNITS = ATTN_ROWS // ATTN_Q


def _unit_rows(x, u):
    return x[u * ATTN_Q:(u + 1) * ATTN_Q]


def _per_unit(fn):
    return jnp.concatenate([fn(u) for u in range(ATTN_UNITS)], axis=0)


def _per_row(vals):
    local = lax.broadcasted_iota(jnp.int32, (ATTN_ROWS, 1), 0)
    out = jnp.full((ATTN_ROWS, 1), vals[0], jnp.int32)
    for u in range(1, ATTN_UNITS):
        out = jnp.where(local >= u * ATTN_Q, vals[u], out)
    return out


def _attn_step(q, k_ref, starts, bounds, row):
    z = _per_unit(lambda u: _dot_nt(_unit_rows(q, u), k_ref[0, pl.ds(starts[u], ATTN_K), :])) * (1.0 / math.sqrt(HEAD_DIM))
    mask = lax.broadcasted_iota(jnp.int32, (1, ATTN_K), 1) < jnp.minimum(row, _per_row(bounds)) - _per_row(starts)
    log_beta = jnp.minimum(z, 0.0) - jnp.log(1.0 + jnp.exp(jnp.minimum(z, -z)))
    log_rest = jnp.where(mask, log_beta - z, 0.0)
    return z, mask, log_beta, log_rest


def _attn_sweep_start(i, t):
    blks = tuple(jnp.maximum(i * ATTN_UNITS + u + 1 - ATTN_K // ATTN_Q, 0) for u in range(ATTN_UNITS))
    return blks, tuple(jnp.int32(t) for _ in range(ATTN_UNITS))


def _attn_keys(blks):
    return [pl.multiple_of(b * ATTN_Q, ATTN_Q) for b in blks]


def _attn_next(blks):
    return tuple(jnp.maximum(b - ATTN_K // ATTN_Q, 0) for b in blks), tuple(b * ATTN_Q for b in blks)


def _attn_reach(run, blks):
    done_rows = sum(jnp.where(b > 0, 0, ATTN_Q) for b in blks)
    local = lax.broadcasted_iota(jnp.int32, (ATTN_ROWS, 1), 0)
    return jnp.max(jnp.where(local >= done_rows, run, float(jnp.finfo(F32).min)))


def _attn_more(carry):
    return carry[-1] > ATTN_SKIP_BELOW


def _tri(strict):
    r = lax.broadcasted_iota(jnp.int32, (ATTN_SUB, ATTN_SUB), 0)
    c = lax.broadcasted_iota(jnp.int32, (ATTN_SUB, ATTN_SUB), 1)
    return (r > c if strict else r >= c).astype(BF)


REACH_TILE = (8, 128)


def _first_step_spec():
    return pl.BlockSpec((1, ATTN_ROWS, ATTN_K), lambda h, i: (h, i, 0))


def _reach_spec():
    return pl.BlockSpec((1, 1) + REACH_TILE, lambda h, i: (h, i, 0, 0))


def _head_cols(piece):
    return lambda t: pl.BlockSpec((1, t, HEAD_DIM), lambda h, i: (piece, 0, h))


def _attn_fwd(proj):
    t = proj.shape[1]
    nq = t // ATTN_ROWS
    tri = _tri(strict=True)

    def body(q_ref, k_ref, v_ref, tri_ref, o_ref, ob_ref, a_ref, beta_ref, reach_ref):
        i = pl.program_id(1)
        q = q_ref[0]
        row = i * ATTN_ROWS + lax.broadcasted_iota(jnp.int32, (ATTN_ROWS, 1), 0)

        def step(carry, keep=False):
            blks, bounds, acc, run, _ = carry
            starts = _attn_keys(blks)
            _, mask, log_beta, log_rest = _attn_step(q, k_ref, starts, bounds, row)
            tail, run = _suffix_sums(log_rest, tri_ref[...], run)
            a = jnp.where(mask, jnp.exp(log_beta + tail), 0.0).astype(BF)
            if keep:
                a_ref[0] = a
                beta_ref[0] = jnp.where(mask, jnp.exp(log_beta), 0.0).astype(BF)
            acc = acc + _per_unit(lambda u: _dot(_unit_rows(a, u), v_ref[0, pl.ds(starts[u], ATTN_K), :]))
            return (*_attn_next(blks), acc, run, _attn_reach(run, blks))

        first = (*_attn_sweep_start(i, t), jnp.zeros((ATTN_ROWS, HEAD_DIM), F32), jnp.zeros((ATTN_ROWS, 1), F32), jnp.float32(0.0))
        after_first = step(first, keep=True)
        reach_ref[...] = jnp.full(reach_ref.shape, after_first[-1], F32)
        o = lax.while_loop(_attn_more, step, after_first)[2]
        o_ref[...] = o
        ob_ref[...] = o.astype(BF)

    qspec = pl.BlockSpec((1, ATTN_ROWS, HEAD_DIM), lambda h, i: (3, i, h))
    rowblk = pl.BlockSpec((ATTN_ROWS, HEAD_DIM), lambda h, i: (i, h))
    return pl.pallas_call(
        body, name="attn_fwd", grid=(N_HEADS, nq),
        in_specs=[qspec, _head_cols(4)(t), _head_cols(5)(t), pl.BlockSpec((ATTN_SUB, ATTN_SUB), lambda h, i: (0, 0))],
        out_specs=[rowblk, rowblk, _first_step_spec(), _first_step_spec(), _reach_spec()],
        out_shape=[S((t, D_MODEL), F32), S((t, D_MODEL), BF), S((N_HEADS, t, ATTN_K), BF), S((N_HEADS, t, ATTN_K), BF),
                   S((N_HEADS, nq) + REACH_TILE, F32)],
    )(proj, proj, proj, tri)


def _attn_bwd(proj, o, d_o, a_first, beta_first, reach, ride=None):
    t = proj.shape[1]
    nq = t // ATTN_ROWS
    tri_strict, tri_incl = _tri(strict=True), _tri(strict=False)
    scale = 1.0 / math.sqrt(HEAD_DIM)

    def body(q_ref, k_ref, v_ref, o_ref, do_ref, a_ref, beta_ref, reach_ref, tris_ref, trii_ref, dq_ref, dk_ref, dv_ref, dk_acc, dv_acc):
        i = pl.program_id(1)

        @pl.when(i == 0)
        def _():
            dk_acc[...] = jnp.zeros_like(dk_acc)
            dv_acc[...] = jnp.zeros_like(dv_acc)

        q = q_ref[0]
        do = do_ref[...]
        total = jnp.sum(do.astype(F32) * o_ref[...], axis=1, keepdims=True)
        zero = jnp.zeros((ATTN_ROWS, 1), F32)
        blks0, bounds0 = _attn_sweep_start(i, t)

        def finish(starts, a, dz, dq):
            dzb = (dz * scale).astype(BF)
            for u in range(ATTN_UNITS):
                dv_acc[pl.ds(starts[u], ATTN_K), :] += _dot_tn(_unit_rows(a, u), _unit_rows(do, u))
                dk_acc[pl.ds(starts[u], ATTN_K), :] += _dot_tn(_unit_rows(dzb, u), _unit_rows(q, u))
            return dq + _per_unit(lambda u: _dot(_unit_rows(dzb, u), k_ref[0, pl.ds(starts[u], ATTN_K), :]))

        def grad_a(starts, a):
            return _per_unit(lambda u: _dot_nt(_unit_rows(do, u), v_ref[0, pl.ds(starts[u], ATTN_K), :])) * a.astype(F32)

        one_step = jnp.max(reach_ref[...]) <= ATTN_SKIP_BELOW

        @pl.when(one_step)
        def _():
            starts = _attn_keys(blks0)
            a = a_ref[0]
            beta = beta_ref[0].astype(F32)
            de = grad_a(starts, a)
            right, _ = _suffix_sums(de, trii_ref[...], zero)
            dz = de * (1.0 - beta) - (total - right) * beta
            dq_ref[...] = finish(starts, a, dz, jnp.zeros((ATTN_ROWS, HEAD_DIM), F32)).astype(BF)

        @pl.when(jnp.logical_not(one_step))
        def _():
            row = i * ATTN_ROWS + lax.broadcasted_iota(jnp.int32, (ATTN_ROWS, 1), 0)

            def step(carry):
                blks, bounds, dq, seen, run, _ = carry
                starts = _attn_keys(blks)
                z, mask, log_beta, log_rest = _attn_step(q, k_ref, starts, bounds, row)
                tail, run = _suffix_sums(log_rest, tris_ref[...], run)
                a = jnp.where(mask, jnp.exp(log_beta + tail), 0.0).astype(BF)
                de = grad_a(starts, a)
                right, seen = _suffix_sums(de, trii_ref[...], seen)
                beta = jax.nn.sigmoid(z)
                dz = jnp.where(mask, de * (1.0 - beta) - (total - right) * beta, 0.0)
                return (*_attn_next(blks), finish(starts, a, dz, dq), seen, run, _attn_reach(run, blks))

            first = (blks0, bounds0, jnp.zeros((ATTN_ROWS, HEAD_DIM), F32), zero, zero, jnp.float32(0.0))
            dq_ref[...] = lax.while_loop(_attn_more, step, step(first))[2].astype(BF)

        @pl.when(i == nq - 1)
        def _():
            dk_ref[...] = dk_acc[...].astype(BF)
            dv_ref[...] = dv_acc[...].astype(BF)

    qspec = pl.BlockSpec((1, ATTN_ROWS, HEAD_DIM), lambda h, i: (3, i, h))
    rowblk = pl.BlockSpec((ATTN_ROWS, HEAD_DIM), lambda h, i: (i, h))
    head = pl.BlockSpec((t, HEAD_DIM), lambda h, i: (0, h))
    trispec = pl.BlockSpec((ATTN_SUB, ATTN_SUB), lambda h, i: (0, 0))
    return _pallas(
        body, name="attn_bwd", grid=(N_HEADS, nq), ride=ride,
        in_specs=[qspec, _head_cols(4)(t), _head_cols(5)(t), rowblk, rowblk, _first_step_spec(), _first_step_spec(), _reach_spec(),
                  trispec, trispec],
        out_specs=[rowblk, head, head],
        out_shape=[S((t, D_MODEL), BF)] * 3,
        scratch_shapes=[pltpu.VMEM((t, HEAD_DIM), F32), pltpu.VMEM((t, HEAD_DIM), F32)],
        args=[proj, proj, proj, o, d_o, a_first, beta_first, reach, tri_strict, tri_incl])


def _tail(h3, n4, p, w_pg, w_pp, g_ple, g_final, target):
    t = h3.shape[0]
    tm = min(TOKEN_TILE, t)
    steps = t // tm

    def body(h_ref, n_ref, p_ref, wpg_ref, wpp_ref, gp_ref, gf_ref, tgt_ref,
             dh_ref, ds_ref, dpp_ref, loss_ref, dgf_ref, dgp_ref):
        pg = jax.nn.sigmoid(_dot(n_ref[...], wpg_ref[...]))
        pp = _dot(p_ref[...].astype(BF), wpp_ref[...])
        h3v = h_ref[...]
        h4 = h3v + pg * pp
        gf = gf_ref[...]
        diff = _rms(h4, gf) - tgt_ref[...]
        _accumulate(loss_ref, jnp.sum(diff * diff, axis=0, keepdims=True))
        dh4, dgf = _rms_bwd(diff * (1.0 / D_MODEL), h4, gf)
        _accumulate(dgf_ref, dgf)
        dpp_ref[...] = (dh4 * pg).astype(BF)
        ds = (dh4 * pp * pg * (1.0 - pg)).astype(BF)
        ds_ref[...] = ds
        dh3, dgp = _rms_bwd(_dot_nt(ds, wpg_ref[...]), h3v, gp_ref[...])
        _accumulate(dgp_ref, dgp)
        dh_ref[...] = dh4 + dh3

        @pl.when(pl.program_id(0) == steps - 1)
        def _():
            loss_ref[...] = jnp.full(loss_ref.shape, 0.5 / D_MODEL * jnp.sum(loss_ref[...]), F32)

    vec = (1, D_MODEL)
    return pl.pallas_call(
        body, name="tail", grid=(steps,),
        in_specs=[_rows(tm, D_MODEL), _rows(tm, D_MODEL), _rows(tm, PLE_DIM), _const_spec((D_MODEL, D_MODEL)),
                  _const_spec((PLE_DIM, D_MODEL)), _const_spec(vec), _const_spec(vec), _rows(tm, D_MODEL)],
        out_specs=[_rows(tm, D_MODEL)] * 3 + [_acc_spec(vec)] * 3,
        out_shape=[S((t, D_MODEL), F32), S((t, D_MODEL), BF), S((t, D_MODEL), BF)] + [S(vec, F32)] * 3,
    )(h3, n4, p, w_pg, w_pp, g_ple, g_final, target)


def _wgrad(xs, ys, name, ride=None, tile=None):
    bx, t, k = xs.shape
    by, _, n = ys.shape
    b = max(bx, by)
    tt = min(tile or WGRAD_TILE * 2 // xs.dtype.itemsize, t)
    steps = t // tt

    def body(x_ref, y_ref, o_ref, acc_ref):
        s = pl.program_id(1)

        @pl.when(s == 0)
        def _():
            acc_ref[...] = jnp.zeros_like(acc_ref)
        acc_ref[...] += _dot_tn(x_ref[0].astype(BF), y_ref[0].astype(BF))

        @pl.when(s == steps - 1)
        def _():
            o_ref[0] = acc_ref[...].astype(BF)

    (out,), landed = _pallas(
        body, name=name, grid=(b, steps), ride=ride,
        in_specs=[pl.BlockSpec((1, tt, k), (lambda j, s: (j, s, 0)) if bx > 1 else (lambda j, s: (0, s, 0))),
                  pl.BlockSpec((1, tt, n), (lambda j, s: (j, s, 0)) if by > 1 else (lambda j, s: (0, s, 0)))],
        out_specs=[pl.BlockSpec((1, k, n), lambda j, s: (j, 0, 0))],
        out_shape=[S((b, k, n), BF)],
        scratch_shapes=[pltpu.VMEM((k, n), F32)],
        args=[xs, ys])
    return (out, landed) if ride is not None else out


def _wgrad_pairs(xs, ys, name, tile):
    t, k = xs[0].shape
    n = ys[0].shape[1]
    pairs = len(xs)
    tt = min(tile, t)
    steps = t // tt

    def body(*refs):
        x_refs, y_refs, o_refs, acc_ref = refs[:pairs], refs[pairs:2 * pairs], refs[2 * pairs:3 * pairs], refs[3 * pairs]
        p, s = pl.program_id(0), pl.program_id(1)

        @pl.when(s == 0)
        def _():
            acc_ref[...] = jnp.zeros_like(acc_ref)
        for j in range(pairs):
            @pl.when(p == j)
            def _(j=j):
                acc_ref[...] += _dot_tn(x_refs[j][...], y_refs[j][...])

            @pl.when(jnp.logical_and(p == j, s == steps - 1))
            def _(j=j):
                o_refs[j][...] = acc_ref[...].astype(BF)

    def turn(j):
        return lambda p, s: (jnp.where(p < j, 0, jnp.where(p > j, steps - 1, s)), 0)

    return pl.pallas_call(
        body, name=name, grid=(pairs, steps),
        in_specs=[pl.BlockSpec((tt, k), turn(j)) for j in range(pairs)] + [pl.BlockSpec((tt, n), turn(j)) for j in range(pairs)],
        out_specs=[pl.BlockSpec((k, n), lambda p, s: (0, 0))] * pairs,
        out_shape=[S((k, n), BF)] * pairs,
        scratch_shapes=[pltpu.VMEM((k, n), F32)],
    )(*xs, *ys)


def _wgrad_pieces(x, ys, name, ride=None, tile=None, row_parts=1, transposed=False):
    t, k = x.shape
    n = ys[0].shape[2]
    counts = [y.shape[0] for y in ys]
    offsets = [sum(counts[:j]) for j in range(len(ys))]
    total = sum(counts)
    tt = min(tile or WGRAD_TILE, t)
    steps = t // tt
    rows, cols = (n, k) if transposed else (k, n)
    kp = rows // row_parts

    def body(x_ref, *refs):
        y_refs, o_refs, acc_ref = refs[:len(ys)], refs[len(ys):len(ys) + row_parts], refs[len(ys) + row_parts]
        p, s = pl.program_id(0), pl.program_id(1)

        @pl.when(s == 0)
        def _():
            acc_ref[...] = jnp.zeros_like(acc_ref)
        for j, y_ref in enumerate(y_refs):
            @pl.when(jnp.logical_and(p >= offsets[j], p < offsets[j] + counts[j]))
            def _(y_ref=y_ref):
                acc_ref[...] += _dot_tn(y_ref[0], x_ref[...]) if transposed else _dot_tn(x_ref[...], y_ref[0])

        @pl.when(s == steps - 1)
        def _():
            for part, o_ref in enumerate(o_refs):
                o_ref[0] = acc_ref[part * kp:(part + 1) * kp, :].astype(BF)

    def turn(j):
        lo, hi = offsets[j], offsets[j] + counts[j]
        return lambda p, s: (jnp.clip(p - lo, 0, counts[j] - 1), jnp.where(p < lo, 0, jnp.where(p >= hi, steps - 1, s)), 0)

    outs, landed = _pallas(
        body, name=name, grid=(total, steps), ride=ride,
        in_specs=[pl.BlockSpec((tt, k), lambda p, s: (s, 0))] + [pl.BlockSpec((1, tt, n), turn(j)) for j in range(len(ys))],
        out_specs=[pl.BlockSpec((1, kp, cols), lambda p, s: (p, 0, 0))] * row_parts,
        out_shape=[S((total, kp, cols), BF)] * row_parts,
        scratch_shapes=[pltpu.VMEM((rows, cols), F32)],
        args=[x, *ys])
    out = outs[0] if row_parts == 1 else outs
    return (out, landed) if ride is not None else out


def _ffn_bwd_hidden(dh, to_gate, to_up, w_out, name, ride=None):
    t = dh.shape[0]
    tm = min(TOKEN_TILE, t)

    def body(dh_ref, to_gate_ref, to_up_ref, wout_ref, df_ref, dgate_ref, dup_ref):
        df = (0.5 * dh_ref[...]).astype(BF)
        df_ref[...] = df
        for c in range(N_FF_CHUNKS):
            dact = _dot_nt(df, wout_ref[c])
            dgate_ref[c] = (dact * to_gate_ref[c].astype(F32)).astype(BF)
            dup_ref[c] = (dact * to_up_ref[c].astype(F32)).astype(BF)

    return _pallas(
        body, name=name, grid=(t // tm,), ride=ride,
        in_specs=[_rows(tm, D_MODEL), _chunks(tm), _chunks(tm), _const_spec(w_out.shape)],
        out_specs=[_rows(tm, D_MODEL), _chunks(tm), _chunks(tm)],
        out_shape=[S((t, D_MODEL), BF)] + [S((N_FF_CHUNKS, t, FF_CHUNK), BF)] * 2,
        args=[dh, to_gate, to_up, w_out])


def _ffn_bwd_input(dh, h_in, g, dgate, dup, w_in, name, ride=None):
    t = dh.shape[0]
    tm = min(TOKEN_TILE, t)

    def body(dh_ref, h_ref, g_ref, dgate_ref, dup_ref, win_ref, dhi_ref, dg_ref):
        dn = jnp.zeros((tm, D_MODEL), F32)
        for c in range(N_FF_CHUNKS):
            dn = dn + _dot_nt(dgate_ref[c], win_ref[c]) + _dot_nt(dup_ref[c], win_ref[N_FF_CHUNKS + c])
        dhi, dg = _rms_bwd(dn, h_ref[...], g_ref[...])
        _accumulate(dg_ref, dg)
        dhi_ref[...] = dh_ref[...] + dhi

    vec = (1, D_MODEL)
    return _pallas(
        body, name=name, grid=(t // tm,), ride=ride,
        in_specs=[_rows(tm, D_MODEL), _rows(tm, D_MODEL), _const_spec(vec), _chunks(tm), _chunks(tm), _const_spec(w_in.shape)],
        out_specs=[_rows(tm, D_MODEL), _acc_spec(vec)],
        out_shape=[S((t, D_MODEL), F32), S(vec, F32)],
        args=[dh, h_in, g, dgate, dup, w_in])


def _mixer_bwd(dh2, proj, yc, ya, conv_w, w_co, w_ao, w_mo, ride=None):
    t = dh2.shape[0]
    tm = min(TOKEN_TILE, t)

    def body(dh_ref, cb_ref, cc_ref, cx_ref, gc_ref, ga_ref, cch_ref, cxh_ref, yc_ref, ya_ref, cw_ref, wco_ref, wao_ref, wmo_ref,
             dhb_ref, dyc_ref, dya_ref, dgc_ref, dga_ref, dcb_ref, dcv_ref, do_ref):
        dhb = dh_ref[...].astype(BF)
        dhb_ref[...] = dhb
        dmerged = _dot_nt(dhb, wmo_ref[...])
        sc = jax.nn.sigmoid(gc_ref[0].astype(F32))
        sa = jax.nn.sigmoid(ga_ref[0].astype(F32))
        dyc = (dmerged * sc).astype(BF)
        dya = (dmerged * sa).astype(BF)
        dyc_ref[...] = dyc
        dya_ref[...] = dya
        dgc_ref[...] = (dmerged * yc_ref[...].astype(F32) * sc * (1.0 - sc)).astype(BF)
        dga_ref[...] = (dmerged * ya_ref[...].astype(F32) * sa * (1.0 - sa)).astype(BF)
        m, m1, m2 = _conv_inputs(cc_ref, cx_ref, cch_ref, cxh_ref)
        cw = cw_ref[...]
        cv = cw[0:1, :] * m2 + cw[1:2, :] * m1 + cw[2:3, :] * m
        dycin = _dot_nt(dyc, wco_ref[...])
        dcb_ref[...] = (dycin * cv).astype(BF)
        dcv_ref[...] = (dycin * cb_ref[0].astype(F32)).astype(BF)
        do_ref[...] = _dot_nt(dya, wao_ref[...]).astype(BF)

    sq = (D_MODEL, D_MODEL)
    return _pallas(
        body, name="mixer_bwd", grid=(t // tm,), ride=ride,
        in_specs=[_rows(tm, D_MODEL), _piece(0, tm), _piece(1, tm), _piece(2, tm), _piece(6, tm), _piece(7, tm),
                  _prev_halo(1, tm), _prev_halo(2, tm), _rows(tm, D_MODEL), _rows(tm, D_MODEL),
                  _const_spec((3, D_MODEL)), _const_spec(sq), _const_spec(sq), _const_spec(sq)],
        out_specs=[_rows(tm, D_MODEL)] * 8,
        out_shape=[S((t, D_MODEL), BF)] * 8,
        args=[dh2, proj, proj, proj, proj, proj, proj, proj, yc, ya, conv_w, w_co, w_ao, w_mo])


TAP_ROWS = 8


def _conv_bwd(dcv, proj, conv_w):
    t = dcv.shape[0]
    tm = min(TOKEN_TILE, t)
    steps = t // tm

    def body(dcv_ref, nxt_ref, cc_ref, cx_ref, cch_ref, cxh_ref, cw_ref, dcc_ref, dcx_ref, dw_ref):
        i = pl.program_id(0)
        m, m1, m2 = _conv_inputs(cc_ref, cx_ref, cch_ref, cxh_ref)
        d0 = dcv_ref[...].astype(F32)
        nxt = jnp.where(i == steps - 1, 0.0, nxt_ref[...].astype(F32))
        row = lax.broadcasted_iota(jnp.int32, (tm, 1), 0)
        d1 = jnp.where(row == tm - 1, nxt[0:1, :], pltpu.roll(d0, tm - 1, 0))
        d2 = pltpu.roll(d0, tm - 2, 0)
        d2 = jnp.where(row == tm - 2, nxt[0:1, :], jnp.where(row == tm - 1, nxt[1:2, :], d2))
        cw = cw_ref[...]
        dm = cw[2:3, :] * d0 + cw[1:2, :] * d1 + cw[0:1, :] * d2
        dcc_ref[...] = (dm * cx_ref[0].astype(F32)).astype(BF)
        dcx_ref[...] = (dm * cc_ref[0].astype(F32)).astype(BF)
        tap_row = lax.broadcasted_iota(jnp.int32, (TAP_ROWS, 1), 0)
        dw = jnp.zeros((TAP_ROWS, D_MODEL), F32)
        for j, mk in enumerate((m2, m1, m)):
            dw = jnp.where(tap_row == j, jnp.sum(d0 * mk, axis=0, keepdims=True), dw)
        _accumulate(dw_ref, dw)

    nxt_spec = pl.BlockSpec((HALO, D_MODEL), lambda i: (jnp.minimum((i + 1) * (tm // HALO), t // HALO - 1), 0))
    return pl.pallas_call(
        body, name="conv_bwd", grid=(steps,),
        in_specs=[_rows(tm, D_MODEL), nxt_spec, _piece(1, tm), _piece(2, tm), _prev_halo(1, tm), _prev_halo(2, tm),
                  _const_spec((3, D_MODEL))],
        out_specs=[_rows(tm, D_MODEL), _rows(tm, D_MODEL), _acc_spec((TAP_ROWS, D_MODEL))],
        out_shape=[S((t, D_MODEL), BF), S((t, D_MODEL), BF), S((TAP_ROWS, D_MODEL), F32)],
    )(dcv, dcv, proj, proj, proj, proj, conv_w)


def _mix_bwd(dpieces, w_mix, h1, dh2, g, ride=None):
    t = h1.shape[0]
    tm = min(TOKEN_TILE, t)

    def body(*refs):
        pieces, (w_ref, h_ref, dh_ref, g_ref, dhi_ref, dg_ref) = refs[:N_MIX], refs[N_MIX:]
        du = jnp.zeros((tm, D_MODEL), F32)
        for d in range(N_MIX):
            du = du + _dot_nt(pieces[d][...], w_ref[d])
        dhi, dg = _rms_bwd(du, h_ref[...], g_ref[...])
        _accumulate(dg_ref, dg)
        dhi_ref[...] = dh_ref[...] + dhi

    vec = (1, D_MODEL)
    return _pallas(
        body, name="mix_bwd", grid=(t // tm,), ride=ride,
        in_specs=[_rows(tm, D_MODEL)] * N_MIX + [_const_spec(w_mix.shape), _rows(tm, D_MODEL), _rows(tm, D_MODEL), _const_spec(vec)],
        out_specs=[_rows(tm, D_MODEL), _acc_spec(vec)],
        out_shape=[S((t, D_MODEL), F32), S(vec, F32)],
        args=[*dpieces, w_mix, h1, dh2, g])


def _adamw(partials, w, m, v, name):
    parts = list(partials) if isinstance(partials, (list, tuple)) else [partials]
    r, c = w.shape
    tr = next(d for d in (r, 512, 352, 256) if d <= 512 // len(parts) and r % d == 0)
    first_tile = [sum(p.shape[1] for p in parts[:j]) // tr for j in range(len(parts))]
    c1 = 1.0 - ADAM_B1 ** ADAM_STEP
    c2 = 1.0 - ADAM_B2 ** ADAM_STEP

    def body(*refs):
        p_refs, (w_ref, m_ref, v_ref, g_ref, d_ref, mo_ref, vo_ref) = refs[:len(parts)], refs[len(parts):]
        g = None
        for j, p_ref in enumerate(p_refs):
            gj = p_ref[0].astype(F32)
            for s in range(1, N_SHARDS):
                gj = gj + p_ref[s].astype(F32)
            g = gj if g is None else jnp.where(pl.program_id(0) >= first_tile[j], gj, g)
        mn = ADAM_B1 * m_ref[...] + (1.0 - ADAM_B1) * g
        vn = ADAM_B2 * v_ref[...] + (1.0 - ADAM_B2) * (g * g)
        g_ref[...] = g
        mo_ref[...] = mn
        vo_ref[...] = vn
        d_ref[...] = -ADAM_LR * ((mn / c1) / (jnp.sqrt(vn / c2) + ADAM_EPS) + ADAM_WD * w_ref[...])

    def rows_of(j):
        last = parts[j].shape[1] // tr - 1
        return lambda i: (0, jnp.clip(i - first_tile[j], 0, last), 0)

    blk = pl.BlockSpec((tr, c), lambda i: (i, 0))
    return pl.pallas_call(
        body, name=name, grid=(r // tr,),
        in_specs=[pl.BlockSpec((N_SHARDS, tr, c), rows_of(j)) for j in range(len(parts))] + [blk, blk, blk],
        out_specs=[blk] * 4, out_shape=[S((r, c), F32)] * 4,
    )(*parts, w, m, v)


_MATRICES = ("ffn1_w_in", "ffn1_w_out", "w_mix_in", "conv_w", "w_conv_out", "w_attn_out", "w_mix_out",
             "ffn2_w_in", "ffn2_w_out", "w_ple_gate", "w_ple_proj")
_GAINS = ("ffn1_norm", "mix_norm", "ffn2_norm", "ple_norm", "final_norm")
_WEIGHTS = ("ffn1_norm", "ffn1_w_in", "ffn1_w_out", "mix_norm", "w_mix_in", "conv_w", "w_conv_out", "w_attn_out", "w_mix_out",
            "ffn2_norm", "ffn2_w_in", "ffn2_w_out", "ple_norm", "w_ple_gate", "w_ple_proj", "final_norm")
CONV_ROWS = 8
_TRANSPOSED = ("ffn1_w_in", "ffn2_w_in")


def _columns_from_shards(g):
    return jnp.transpose(g, (1, 0, 2)).reshape(g.shape[1], N_SHARDS * g.shape[2])


def _shards_from_columns(a):
    r, c = a.shape
    return jnp.transpose(a.reshape(r, N_SHARDS, c // N_SHARDS), (1, 0, 2))


def kernel(x, p, ffn1_norm, ffn1_w_in, ffn1_w_out, mix_norm, w_mix_in, conv_w, w_conv_out, w_attn_out, w_mix_out, ffn2_norm, ffn2_w_in, ffn2_w_out, ple_norm, w_ple_gate, w_ple_proj, final_norm, loss_target, m_ffn1_norm, m_ffn1_w_in, m_ffn1_w_out, m_mix_norm, m_w_mix_in, m_conv_w, m_w_conv_out, m_w_attn_out, m_w_mix_out, m_ffn2_norm, m_ffn2_w_in, m_ffn2_w_out, m_ple_norm, m_w_ple_gate, m_w_ple_proj, m_final_norm, v_ffn1_norm, v_ffn1_w_in, v_ffn1_w_out, v_mix_norm, v_w_mix_in, v_conv_w, v_w_conv_out, v_w_attn_out, v_w_mix_out, v_ffn2_norm, v_ffn2_w_in, v_ffn2_w_out, v_ple_norm, v_w_ple_gate, v_w_ple_proj, v_final_norm):
    given = dict(locals())
    t = x.shape[1]
    xs = x.reshape(t, D_MODEL)
    ps = p.reshape(t, PLE_DIM)
    target = loss_target.reshape(t, D_MODEL)
    shard = {k: given[k].reshape(given[k].shape[-2:]) for k in _MATRICES}
    gain = {k: given[k].reshape(1, D_MODEL) for k in _GAINS}

    send = {k: shard[k].astype(BF) for k in _MATRICES}
    send["conv_w"] = jnp.pad(shard["conv_w"], ((0, CONV_ROWS - 3), (0, 0)))
    loss_vec, dx, landed, gain_grads = _forward_backward(xs, ps, target, gain, send)
    gain_rows = jnp.concatenate([gain_grads[k] for k in _GAINS] + [loss_vec, jnp.zeros((8 - len(_GAINS) - 1, D_MODEL), F32)], axis=0)
    gain_parts, = _exchange_alone("gather", [gain_rows], "gather_gain_gradients")

    out = {}
    for k in _MATRICES:
        w, m, v = shard[k], given["m_" + k].reshape(shard[k].shape), given["v_" + k].reshape(shard[k].shape)
        part = landed[k]
        if k == "conv_w":
            pad = ((0, CONV_ROWS - 3), (0, 0))
            w, m, v = jnp.pad(w, pad), jnp.pad(m, pad), jnp.pad(v, pad, constant_values=1.0)
        if k in _TRANSPOSED:
            w, m, v = w.T, m.T, v.T
        res = _adamw(part, w, m, v, "adamw_" + k)
        out[k] = [r[:3] if k == "conv_w" else (r.T if k in _TRANSPOSED else r) for r in res]
    stack = lambda pre: jnp.concatenate([given[pre + k].reshape(1, D_MODEL) for k in _GAINS] + [jnp.ones((8 - len(_GAINS), D_MODEL), F32)], axis=0)
    res = _adamw(gain_parts, stack(""), stack("m_"), stack("v_"), "adamw_gains")
    for j, k in enumerate(_GAINS):
        out[k] = [r[j:j + 1] for r in res]

    loss = jnp.sum(gain_parts[:, len(_GAINS), 0])
    per_kind = [[out[k][j].reshape(given[k].shape) for k in _WEIGHTS] for j in range(4)]
    return (loss, dx.reshape(x.shape), *per_kind[0], *per_kind[1], *per_kind[2], *per_kind[3])


def _forward_backward(xs, ps, target, gain, send, full=None):
    exchange = full is None
    full = dict(full or {})
    grads, landed = {}, {}

    def gather(names):
        return ("gather", [send[k] for k in names]) if exchange else None

    def scatter(names):
        return ("scatter", [grads[k] for k in names]) if exchange else None

    def keep(into, names, got):
        into.update(zip(names, got))

    first = ("ffn1_w_in",)
    (n1,), got = _prenorm(xs, gain["ffn1_norm"], ride=gather(first))
    keep(full, first, got)
    w1_in = full["ffn1_w_in"]
    second = ("ffn1_w_out", "w_mix_in")
    (act1, to_gate1, to_up1), got = _ffn_up(n1, w1_in, "ffn1_up", ride=gather(second))
    keep(full, second, got)
    w1_out = full["ffn1_w_out"].reshape(N_FF_CHUNKS, FF_CHUNK, D_MODEL)
    third = ("conv_w", "w_conv_out", "w_attn_out", "w_mix_out")
    (h1, u), got = _ffn_down(xs, act1, w1_out, gain["mix_norm"], "ffn1_down", ride=gather(third))
    keep(full, third, got)
    w_mix = full["w_mix_in"]
    w_co, w_ao, w_mo = (full[k].reshape(D_MODEL, D_MODEL) for k in ("w_conv_out", "w_attn_out", "w_mix_out"))
    taps = _columns_from_shards(full["conv_w"][:, :3, :])
    rest = ("ffn2_w_in", "ffn2_w_out", "w_ple_gate", "w_ple_proj")
    (proj,), got = _mix_proj(u, w_mix, ride=gather(rest))
    keep(full, rest, got)
    w2_in, w2_out = full["ffn2_w_in"], full["ffn2_w_out"].reshape(N_FF_CHUNKS, FF_CHUNK, D_MODEL)
    w_pg = full["w_ple_gate"].reshape(D_MODEL, D_MODEL)
    w_pp = _columns_from_shards(full["w_ple_proj"])
    o, o_bf, a_first, beta_first, reach = _attn_fwd(proj)
    h2, n3, ycin, yc, ya, merged = _mixer_out(proj, o_bf, h1, taps, w_co, w_ao, w_mo, gain["ffn2_norm"])
    (act2, to_gate2, to_up2), _ = _ffn_up(n3, w2_in, "ffn2_up")
    (h3, n4), _ = _ffn_down(h2, act2, w2_out, gain["ple_norm"], "ffn2_down")
    dh3, ds, dpp, loss_vec, dg_final, dg_ple = _tail(h3, n4, ps, w_pg, w_pp, gain["ple_norm"], gain["final_norm"], target)

    one = lambda a: a[None]
    by_rows = lambda g, rows: g.reshape(N_SHARDS, rows // N_SHARDS, D_MODEL)
    square = WGRAD_TILE // 2
    grads["w_ple_gate"] = by_rows(_wgrad(one(n4), one(ds), "wgrad_ple_gate", tile=square), D_MODEL)
    grads["w_ple_proj"] = _shards_from_columns(_wgrad(one(ps), one(dpp), "wgrad_ple_proj")[0])
    ple = ("w_ple_gate", "w_ple_proj")
    (df2, dgate2, dup2), got = _ffn_bwd_hidden(dh3, to_gate2, to_up2, w2_out, "ffn2_bwd_hidden", ride=scatter(ple))
    keep(landed, ple, got)
    grads["ffn2_w_out"] = by_rows(_wgrad(act2, one(df2), "wgrad_ffn2_out"), D_FF)
    grads["ffn2_w_in"] = _wgrad_pieces(n3, [dgate2, dup2], "wgrad_ffn2_in", transposed=True)
    (dh2, dg_ffn2), got = _ffn_bwd_input(dh3, h2, gain["ffn2_norm"], dgate2, dup2, w2_in, "ffn2_bwd_input", ride=scatter(("ffn2_w_out",)))
    keep(landed, ("ffn2_w_out",), got)
    (dh2b, dyc, dya, dgc, dga, dcb, dcv, d_o), _ = _mixer_bwd(dh2, proj, yc, ya, taps, w_co, w_ao, w_mo)
    square_grads = _wgrad_pairs([merged, ycin, o_bf], [dh2b, dyc, dya], "wgrad_mixer_out", tile=square // 2)
    grads["w_mix_out"], grads["w_conv_out"], grads["w_attn_out"] = (by_rows(g, D_MODEL) for g in square_grads)
    dcc, dcx, dtaps = _conv_bwd(dcv, proj, taps)
    grads["conv_w"] = jnp.pad(_shards_from_columns(dtaps[:3]), ((0, 0), (0, CONV_ROWS - 3), (0, 0)))
    behind_attn = ("ffn2_w_in",)
    (dq, dk, dv), got = _attn_bwd(proj, o, d_o, a_first, beta_first, reach, ride=scatter(behind_attn))
    keep(landed, behind_attn, got)
    dpieces = [dcb, dcc, dcx, dq, dk, dv, dgc, dga]
    half = N_MIX // 2
    squares = ("w_mix_out", "w_conv_out", "w_attn_out", "conv_w")
    first_half = _wgrad_pieces(u, [one(dp) for dp in dpieces[:half]], "wgrad_mix_in_a", tile=WGRAD_TILE // 2, row_parts=2, ride=scatter(squares))
    if exchange:
        first_half, got = first_half
        keep(landed, squares, got)
    tops, bottoms = zip(first_half,
                        _wgrad_pieces(u, [one(dp) for dp in dpieces[half:]], "wgrad_mix_in_b", tile=WGRAD_TILE // 2, row_parts=2))
    grads["w_mix_in top"], grads["w_mix_in bottom"] = jnp.concatenate(tops, axis=0), jnp.concatenate(bottoms, axis=0)
    (dh1, dg_mix), top = _mix_bwd(dpieces, w_mix, h1, dh2, gain["mix_norm"], ride=scatter(("w_mix_in top",)))
    (df1, dgate1, dup1), bottom = _ffn_bwd_hidden(dh1, to_gate1, to_up1, w1_out, "ffn1_bwd_hidden", ride=scatter(("w_mix_in bottom",)))
    if exchange:
        landed["w_mix_in"] = [top[0], bottom[0]]
    else:
        grads["w_mix_in"] = jnp.concatenate([grads.pop("w_mix_in top"), grads.pop("w_mix_in bottom")], axis=1)
    grads["ffn1_w_out"] = by_rows(_wgrad(act1, one(df1), "wgrad_ffn1_out"), D_FF)
    if exchange:
        grads["ffn1_w_in"], got = _wgrad_pieces(n1, [dgate1, dup1], "wgrad_ffn1_in", transposed=True, ride=scatter(("ffn1_w_out",)))
        keep(landed, ("ffn1_w_out",), got)
    else:
        grads["ffn1_w_in"] = _wgrad_pieces(n1, [dgate1, dup1], "wgrad_ffn1_in", transposed=True)
    (dx, dg_ffn1), got = _ffn_bwd_input(dh1, xs, gain["ffn1_norm"], dgate1, dup1, w1_in, "ffn1_bwd_input", ride=scatter(("ffn1_w_in",)))
    keep(landed, ("ffn1_w_in",), got)
    gain_grads = dict(ffn1_norm=dg_ffn1, mix_norm=dg_mix, ffn2_norm=dg_ffn2, ple_norm=dg_ple, final_norm=dg_final)
    return loss_vec, dx, (landed if exchange else grads), gain_grads
```

```python
import functools
import math

import jax
import jax.numpy as jnp
from jax import lax
from jax.experimental import pallas as pl
from jax.experimental.pallas import tpu as pltpu

D_MODEL = 1024
D_FF = 2816
N_SHARDS = 8
FF_CHUNK = 2 * D_FF // N_SHARDS
N_FF_CHUNKS = D_FF // FF_CHUNK
N_HEADS = 8
HEAD_DIM = 128
PLE_DIM = 256
NORM_EPS = 1e-6
N_MIX = 8
ADAM_LR, ADAM_B1, ADAM_B2, ADAM_EPS, ADAM_WD, ADAM_STEP = 0.001, 0.9, 0.999, 1e-08, 0.01, 10

TOKEN_TILE = 512
WGRAD_TILE = 4096
PROJ_TILE = 2048
ATTN_ROWS = 1024
ATTN_Q = 64
ATTN_SUB = 128
ATTN_K = 2 * ATTN_SUB
ATTN_SKIP_BELOW = -90.0

BF = jnp.bfloat16
F32 = jnp.float32
MESH = pl.DeviceIdType.MESH
NT = (((1,), (1,)), ((), ()))
TN = (((0,), (0,)), ((), ()))
S = jax.ShapeDtypeStruct
ANY = pl.BlockSpec(memory_space=pl.ANY)


def _const_spec(shape):
    nd = len(shape)
    return pl.BlockSpec(shape, lambda *_: (0,) * nd, pipeline_mode=pl.Buffered(1))


def _rows(tm, cols):
    return pl.BlockSpec((tm, cols), lambda i: (i, 0))


def _chunks(tm):
    return pl.BlockSpec((N_FF_CHUNKS, tm, FF_CHUNK), lambda i: (0, i, 0))


def _acc_spec(shape):
    nd = len(shape)
    return pl.BlockSpec(shape, lambda *_: (0,) * nd)


def _dot(a, b):
    return jnp.dot(a, b, preferred_element_type=F32)


def _dot_nt(a, b):
    return lax.dot_general(a, b, NT, preferred_element_type=F32)


def _dot_tn(a, b):
    return lax.dot_general(a, b, TN, preferred_element_type=F32)


def _rms(h, g):
    r = lax.rsqrt(jnp.mean(h * h, axis=-1, keepdims=True) + NORM_EPS)
    return h * r * g


def _rms_bwd(dn, h, g):
    r = lax.rsqrt(jnp.mean(h * h, axis=-1, keepdims=True) + NORM_EPS)
    nh = h * r
    gd = dn * g
    dh = r * (gd - nh * jnp.mean(gd * nh, axis=-1, keepdims=True))
    return dh, jnp.sum(dn * nh, axis=0, keepdims=True)


def _accumulate(ref, val):
    @pl.when(pl.program_id(0) == 0)
    def _():
        ref[...] = jnp.zeros_like(ref)
    ref[...] += val


def _place():
    x, y, c = lax.axis_index("x"), lax.axis_index("y"), lax.axis_index("c")
    return x, y, c


def _slot(px, py, pc):
    return 4 * px + 2 * py + pc


def _gather_phases(ins, outs, send_sems, recv_sems, local_sems):
    n = len(ins)

    def parties():
        x, y, c = _place()
        return (x, y, c), (x, y, 1 - c), [(1 - x, y), (x, 1 - y), (1 - x, 1 - y)], c

    def copy(a, k, block, to, src=None):
        dst = outs[a].at[_slot(*block)]
        return pltpu.make_async_remote_copy(
            src_ref=dst if src is None else src, dst_ref=dst,
            send_sem=send_sems.at[a, k], recv_sem=recv_sems.at[a, k],
            device_id=to, device_id_type=MESH)

    def own(a, me):
        return pltpu.make_async_copy(ins[a], outs[a].at[_slot(*me)], local_sems.at[a])

    def first(a, me, sibling, chips, c):
        return [copy(a, 0, me, sibling, src=ins[a])] + [copy(a, 1 + j, me, (*chip, c), src=ins[a]) for j, chip in enumerate(chips)]

    def start():
        me, sibling, chips, c = parties()
        for a in range(n):
            own(a, me).start()
        for a in range(n):
            for cp in first(a, me, sibling, chips, c):
                cp.start()

    def forward():
        me, sibling, chips, c = parties()
        for j, chip in enumerate(chips):
            for a in range(n):
                copy(a, 1 + j, (*chip, c), me).wait_recv()
                copy(a, 4 + j, (*chip, c), sibling).start()

    def finish():
        me, sibling, chips, c = parties()
        for a in range(n):
            copy(a, 0, sibling, me).wait_recv()
            for j, chip in enumerate(chips):
                copy(a, 4 + j, (*chip, 1 - c), me).wait_recv()
        for a in range(n):
            for cp in first(a, me, sibling, chips, c) + [copy(a, 4 + j, (*chip, c), sibling) for j, chip in enumerate(chips)]:
                cp.wait_send()
            own(a, me).wait()

    return [start, forward, finish]


def _scatter_phases(ins, outs, send_sems, recv_sems, local_sems):
    n = len(ins)

    def copies():
        x, y, c = _place()
        me = _slot(x, y, c)
        out = [pltpu.make_async_copy(ins[a].at[me], outs[a].at[me], local_sems.at[a]) for a in range(n)]
        for k in range(1, N_SHARDS):
            px = 1 - x if k & 4 else x
            py = 1 - y if k & 2 else y
            pc = 1 - c if k & 1 else c
            for a in range(n):
                out.append(pltpu.make_async_remote_copy(
                    src_ref=ins[a].at[_slot(px, py, pc)], dst_ref=outs[a].at[me],
                    send_sem=send_sems.at[a, k - 1], recv_sem=recv_sems.at[a, k - 1],
                    device_id=(px, py, pc), device_id_type=MESH))
        return out

    def start():
        for cp in copies():
            cp.start()

    def finish():
        for cp in copies():
            cp.wait()

    return [start, finish]


def _pallas(body, *, name, grid, in_specs, out_specs, out_shape, args, scratch_shapes=(), ride=None):
    if ride is None:
        outs = pl.pallas_call(body, name=name, grid=grid, in_specs=in_specs, out_specs=out_specs, out_shape=out_shape,
                              scratch_shapes=list(scratch_shapes))(*args)
        return list(outs), []
    kind, arrays = ride
    n, n_in, n_out, n_scr = len(arrays), len(in_specs), len(out_specs), len(scratch_shapes)
    total = math.prod(grid)
    middle = (9 * total) // 10
    landed_shape = [S((N_SHARDS,) + a.shape if kind == "gather" else a.shape, a.dtype) for a in arrays]

    def with_exchange(*refs):
        ins, riders_in = refs[:n_in], refs[n_in:n_in + n]
        outs, riders_out = refs[n_in + n:n_in + n + n_out], refs[n_in + n + n_out:n_in + 2 * n + n_out]
        scratch, sems = refs[n_in + 2 * n + n_out:n_in + 2 * n + n_out + n_scr], refs[n_in + 2 * n + n_out + n_scr:]
        step = 0
        for axis, size in enumerate(grid):
            step = step * size + pl.program_id(axis)
        phases = (_gather_phases if kind == "gather" else _scatter_phases)(riders_in, riders_out, *sems)
        pl.when(step == 0)(phases[0])
        body(*ins, *outs, *scratch)
        for phase in phases[1:-1]:
            pl.when(step == middle)(phase)
        pl.when(step == total - 1)(phases[-1])

    outs = pl.pallas_call(
        with_exchange, name=name, grid=grid,
        in_specs=list(in_specs) + [ANY] * n, out_specs=list(out_specs) + [ANY] * n,
        out_shape=list(out_shape) + landed_shape,
        scratch_shapes=list(scratch_shapes) + [pltpu.SemaphoreType.DMA((n, 7)), pltpu.SemaphoreType.DMA((n, 7)),
                                               pltpu.SemaphoreType.DMA((n,))],
    )(*args, *arrays)
    return list(outs[:n_out]), list(outs[n_out:])


def _exchange_alone(kind, arrays, name):
    return _pallas(lambda: None, name=name, grid=(1,), in_specs=[], out_specs=[], out_shape=[], args=[], ride=(kind, arrays))[1]


def _prenorm(x, g, ride=None):
    t = x.shape[0]
    tm = min(TOKEN_TILE, t)

    def body(x_ref, g_ref, n_ref):
        n_ref[...] = _rms(x_ref[...], g_ref[...]).astype(BF)

    return _pallas(
        body, name="prenorm", grid=(t // tm,), ride=ride,
        in_specs=[_rows(tm, D_MODEL), _const_spec((1, D_MODEL))], out_specs=[_rows(tm, D_MODEL)],
        out_shape=[S((t, D_MODEL), BF)], args=[x, g])


def _ffn_up(n, w_in, name, ride=None):
    t = n.shape[0]
    tm = min(TOKEN_TILE, t)

    def body(n_ref, win_ref, act_ref, to_gate_ref, to_up_ref):
        nb = n_ref[...]
        for c in range(N_FF_CHUNKS):
            gate = _dot(nb, win_ref[c])
            up = _dot(nb, win_ref[N_FF_CHUNKS + c])
            sg = jax.nn.sigmoid(gate)
            silu = gate * sg
            act_ref[c] = (silu * up).astype(BF)
            to_gate_ref[c] = (up * (sg * (1.0 + gate * (1.0 - sg)))).astype(BF)
            to_up_ref[c] = silu.astype(BF)

    return _pallas(
        body, name=name, grid=(t // tm,), ride=ride,
        in_specs=[_rows(tm, D_MODEL), _const_spec(w_in.shape)],
        out_specs=[_chunks(tm)] * 3, out_shape=[S((N_FF_CHUNKS, t, FF_CHUNK), BF)] * 3,
        args=[n, w_in])


def _ffn_down(h, act, w_out, g_next, name, ride=None):
    t = h.shape[0]
    tm = min(TOKEN_TILE, t)

    def body(h_ref, act_ref, wout_ref, g_ref, ho_ref, no_ref):
        acc = jnp.zeros((tm, D_MODEL), F32)
        for c in range(N_FF_CHUNKS):
            acc = acc + _dot(act_ref[c], wout_ref[c])
        ho = h_ref[...] + 0.5 * acc
        ho_ref[...] = ho
        no_ref[...] = _rms(ho, g_ref[...]).astype(BF)

    return _pallas(
        body, name=name, grid=(t // tm,), ride=ride,
        in_specs=[_rows(tm, D_MODEL), _chunks(tm), _const_spec(w_out.shape), _const_spec((1, D_MODEL))],
        out_specs=[_rows(tm, D_MODEL)] * 2, out_shape=[S((t, D_MODEL), F32), S((t, D_MODEL), BF)],
        args=[h, act, w_out, g_next])


def _mix_proj(u, w_mix, ride=None):
    t = u.shape[0]
    tm = min(PROJ_TILE, t)

    def body(u_ref, w_ref, o_ref):
        o_ref[0] = _dot(u_ref[...], w_ref[0]).astype(BF)

    return _pallas(
        body, name="mix_proj", grid=(N_MIX, t // tm), ride=ride,
        in_specs=[pl.BlockSpec((tm, D_MODEL), lambda d, i: (i, 0)), pl.BlockSpec((1, D_MODEL, D_MODEL), lambda d, i: (d, 0, 0))],
        out_specs=[pl.BlockSpec((1, tm, D_MODEL), lambda d, i: (d, i, 0))],
        out_shape=[S((N_MIX, t, D_MODEL), BF)], args=[u, w_mix])


HALO = 16


def _piece(d, tm):
    return pl.BlockSpec((1, tm, D_MODEL), lambda i: (d, i, 0))


def _prev_halo(d, tm):
    return pl.BlockSpec((1, HALO, D_MODEL), lambda i: (d, jnp.maximum(i * (tm // HALO) - 1, 0), 0))


def _shift_down(m, prev_tail, k):
    tm = m.shape[0]
    out = pltpu.roll(m, k, 0)
    row = lax.broadcasted_iota(jnp.int32, (tm, 1), 0)
    for j in range(k):
        out = jnp.where(row == j, prev_tail[HALO - k + j:HALO - k + j + 1, :], out)
    return out


def _conv_inputs(cc_ref, cx_ref, cch_ref, cxh_ref):
    m = cc_ref[0].astype(F32) * cx_ref[0].astype(F32)
    mh = cch_ref[0].astype(F32) * cxh_ref[0].astype(F32)
    mh = jnp.where(pl.program_id(0) == 0, 0.0, mh)
    return m, _shift_down(m, mh, 1), _shift_down(m, mh, 2)


def _mixer_out(proj, o, h1, conv_w, w_co, w_ao, w_mo, g_next):
    t = h1.shape[0]
    tm = min(TOKEN_TILE, t)

    def body(cb_ref, cc_ref, cx_ref, gc_ref, ga_ref, cch_ref, cxh_ref, o_ref, h_ref, cw_ref, wco_ref, wao_ref, wmo_ref,
             g_ref, ho_ref, no_ref, ycin_ref, yc_ref, ya_ref, mg_ref):
        m, m1, m2 = _conv_inputs(cc_ref, cx_ref, cch_ref, cxh_ref)
        cw = cw_ref[...]
        cv = cw[0:1, :] * m2 + cw[1:2, :] * m1 + cw[2:3, :] * m
        ycin = (cb_ref[0].astype(F32) * cv).astype(BF)
        ycin_ref[...] = ycin
        yc = _dot(ycin, wco_ref[...])
        ya = _dot(o_ref[...], wao_ref[...])
        yc_ref[...] = yc.astype(BF)
        ya_ref[...] = ya.astype(BF)
        merged = (jax.nn.sigmoid(gc_ref[0].astype(F32)) * yc + jax.nn.sigmoid(ga_ref[0].astype(F32)) * ya).astype(BF)
        mg_ref[...] = merged
        ho = h_ref[...] + _dot(merged, wmo_ref[...])
        ho_ref[...] = ho
        no_ref[...] = _rms(ho, g_ref[...]).astype(BF)

    sq = (D_MODEL, D_MODEL)
    return pl.pallas_call(
        body, name="mixer_out", grid=(t // tm,),
        in_specs=[_piece(0, tm), _piece(1, tm), _piece(2, tm), _piece(6, tm), _piece(7, tm), _prev_halo(1, tm), _prev_halo(2, tm),
                  _rows(tm, D_MODEL), _rows(tm, D_MODEL), _const_spec((3, D_MODEL)), _const_spec(sq), _const_spec(sq),
                  _const_spec(sq), _const_spec((1, D_MODEL))],
        out_specs=[_rows(tm, D_MODEL)] * 6,
        out_shape=[S((t, D_MODEL), F32)] + [S((t, D_MODEL), BF)] * 5,
    )(proj, proj, proj, proj, proj, proj, proj, o, h1, conv_w, w_co, w_ao, w_mo, g_next)


def _suffix_sums(vals, tri, before):
    out, right = [], before
    for b in reversed(range(ATTN_K // ATTN_SUB)):
        v = vals[:, b * ATTN_SUB:(b + 1) * ATTN_SUB]
        hi = v.astype(BF)
        lo = (v - hi.astype(F32)).astype(BF)
        out.append(_dot(hi, tri) + _dot(lo, tri) + right)
        right = right + jnp.sum(v, axis=1, keepdims=True)
    return jnp.concatenate(out[::-1], axis=1), right


ATTN_UNITS = ATTN_ROWS // ATTN_Q


def _unit_rows(x, u):
    return x[u * ATTN_Q:(u + 1) * ATTN_Q]


def _per_unit(fn):
    return jnp.concatenate([fn(u) for u in range(ATTN_UNITS)], axis=0)


def _per_row(vals):
    local = lax.broadcasted_iota(jnp.int32, (ATTN_ROWS, 1), 0)
    out = jnp.full((ATTN_ROWS, 1), vals[0], jnp.int32)
    for u in range(1, ATTN_UNITS):
        out = jnp.where(local >= u * ATTN_Q, vals[u], out)
    return out


def _attn_step(q, k_ref, starts, bounds, row):
    z = _per_unit(lambda u: _dot_nt(_unit_rows(q, u), k_ref[0, pl.ds(starts[u], ATTN_K), :])) * (1.0 / math.sqrt(HEAD_DIM))
    mask = lax.broadcasted_iota(jnp.int32, (1, ATTN_K), 1) < jnp.minimum(row, _per_row(bounds)) - _per_row(starts)
    log_beta = jnp.minimum(z, 0.0) - jnp.log(1.0 + jnp.exp(jnp.minimum(z, -z)))
    log_rest = jnp.where(mask, log_beta - z, 0.0)
    return z, mask, log_beta, log_rest


def _attn_sweep_start(i, t):
    blks = tuple(jnp.maximum(i * ATTN_UNITS + u + 1 - ATTN_K // ATTN_Q, 0) for u in range(ATTN_UNITS))
    return blks, tuple(jnp.int32(t) for _ in range(ATTN_UNITS))


def _attn_keys(blks):
    return [pl.multiple_of(b * ATTN_Q, ATTN_Q) for b in blks]


def _attn_next(blks):
    return tuple(jnp.maximum(b - ATTN_K // ATTN_Q, 0) for b in blks), tuple(b * ATTN_Q for b in blks)


def _attn_reach(run, blks):
    done_rows = sum(jnp.where(b > 0, 0, ATTN_Q) for b in blks)
    local = lax.broadcasted_iota(jnp.int32, (ATTN_ROWS, 1), 0)
    return jnp.max(jnp.where(local >= done_rows, run, float(jnp.finfo(F32).min)))


def _attn_more(carry):
    return carry[-1] > ATTN_SKIP_BELOW


def _tri(strict):
    r = lax.broadcasted_iota(jnp.int32, (ATTN_SUB, ATTN_SUB), 0)
    c = lax.broadcasted_iota(jnp.int32, (ATTN_SUB, ATTN_SUB), 1)
    return (r > c if strict else r >= c).astype(BF)


REACH_TILE = (8, 128)


def _first_step_spec():
    return pl.BlockSpec((1, ATTN_ROWS, ATTN_K), lambda h, i: (h, i, 0))


def _reach_spec():
    return pl.BlockSpec((1, 1) + REACH_TILE, lambda h, i: (h, i, 0, 0))


def _head_cols(piece):
    return lambda t: pl.BlockSpec((1, t, HEAD_DIM), lambda h, i: (piece, 0, h))


def _attn_fwd(proj):
    t = proj.shape[1]
    nq = t // ATTN_ROWS
    tri = _tri(strict=True)

    def body(q_ref, k_hbm, v_hbm, tri_ref, o_ref, ob_ref, a_ref, beta_ref, reach_ref, k_buf, v_buf, sems):
        h, i = pl.program_id(0), pl.program_id(1)
        slot = h % 2

        def fetch(head, into):
            cols = pl.ds(pl.multiple_of(head * HEAD_DIM, HEAD_DIM), HEAD_DIM)
            return [pltpu.make_async_copy(k_hbm.at[4, :, cols], k_buf.at[into], sems.at[0, into]),
                    pltpu.make_async_copy(v_hbm.at[5, :, cols], v_buf.at[into], sems.at[1, into])]

        @pl.when(jnp.logical_and(h == 0, i == 0))
        def _():
            for c in fetch(0, 0):
                c.start()

        @pl.when(i == 0)
        def _():
            for c in fetch(h, slot):
                c.wait()

        @pl.when(jnp.logical_and(i == 0, h + 1 < N_HEADS))
        def _():
            for c in fetch(h + 1, 1 - slot):
                c.start()

        k_ref, v_ref = k_buf.at[pl.ds(slot, 1)], v_buf.at[pl.ds(slot, 1)]
        q = q_ref[0]
        row = i * ATTN_ROWS + lax.broadcasted_iota(jnp.int32, (ATTN_ROWS, 1), 0)

        def step(carry, keep=False):
            blks, bounds, acc, run, _ = carry
            starts = _attn_keys(blks)
            _, mask, log_beta, log_rest = _attn_step(q, k_ref, starts, bounds, row)
            tail, run = _suffix_sums(log_rest, tri_ref[...], run)
            a = jnp.where(mask, jnp.exp(log_beta + tail), 0.0).astype(BF)
            if keep:
                a_ref[0] = a
                beta_ref[0] = jnp.where(mask, jnp.exp(log_beta), 0.0).astype(BF)
            acc = acc + _per_unit(lambda u: _dot(_unit_rows(a, u), v_ref[0, pl.ds(starts[u], ATTN_K), :]))
            return (*_attn_next(blks), acc, run, _attn_reach(run, blks))

        first = (*_attn_sweep_start(i, t), jnp.zeros((ATTN_ROWS, HEAD_DIM), F32), jnp.zeros((ATTN_ROWS, 1), F32), jnp.float32(0.0))
        after_first = step(first, keep=True)
        reach_ref[...] = jnp.full(reach_ref.shape, after_first[-1], F32)
        o = lax.while_loop(_attn_more, step, after_first)[2]
        o_ref[...] = o
        ob_ref[...] = o.astype(BF)

    qspec = pl.BlockSpec((1, ATTN_ROWS, HEAD_DIM), lambda h, i: (3, i, h))
    rowblk = pl.BlockSpec((ATTN_ROWS, HEAD_DIM), lambda h, i: (i, h))
    return pl.pallas_call(
        body, name="attn_fwd", grid=(N_HEADS, nq),
        in_specs=[qspec, pl.BlockSpec(memory_space=pl.ANY), pl.BlockSpec(memory_space=pl.ANY),
                  pl.BlockSpec((ATTN_SUB, ATTN_SUB), lambda h, i: (0, 0))],
        out_specs=[rowblk, rowblk, _first_step_spec(), _first_step_spec(), _reach_spec()],
        out_shape=[S((t, D_MODEL), F32), S((t, D_MODEL), BF), S((N_HEADS, t, ATTN_K), BF), S((N_HEADS, t, ATTN_K), BF),
                   S((N_HEADS, nq) + REACH_TILE, F32)],
        scratch_shapes=[pltpu.VMEM((2, t, HEAD_DIM), BF), pltpu.VMEM((2, t, HEAD_DIM), BF), pltpu.SemaphoreType.DMA((2, 2))],
    )(proj, proj, proj, tri)


def _attn_bwd(proj, o, d_o, a_first, beta_first, reach, ride=None):
    t = proj.shape[1]
    nq = t // ATTN_ROWS
    tri_strict, tri_incl = _tri(strict=True), _tri(strict=False)
    scale = 1.0 / math.sqrt(HEAD_DIM)

    def body(q_ref, k_ref, v_ref, o_ref, do_ref, a_ref, beta_ref, reach_ref, tris_ref, trii_ref, dq_ref, dk_ref, dv_ref, dk_acc, dv_acc):
        i = pl.program_id(1)

        @pl.when(i == 0)
        def _():
            dk_acc[...] = jnp.zeros_like(dk_acc)
            dv_acc[...] = jnp.zeros_like(dv_acc)

        q = q_ref[0]
        do = do_ref[...]
        total = jnp.sum(do.astype(F32) * o_ref[...], axis=1, keepdims=True)
        zero = jnp.zeros((ATTN_ROWS, 1), F32)
        blks0, bounds0 = _attn_sweep_start(i, t)

        def finish(starts, a, dz, dq):
            dzb = (dz * scale).astype(BF)
            for u in range(ATTN_UNITS):
                dv_acc[pl.ds(starts[u], ATTN_K), :] += _dot_tn(_unit_rows(a, u), _unit_rows(do, u))
                dk_acc[pl.ds(starts[u], ATTN_K), :] += _dot_tn(_unit_rows(dzb, u), _unit_rows(q, u))
            return dq + _per_unit(lambda u: _dot(_unit_rows(dzb, u), k_ref[0, pl.ds(starts[u], ATTN_K), :]))

        def grad_a(starts, a):
            return _per_unit(lambda u: _dot_nt(_unit_rows(do, u), v_ref[0, pl.ds(starts[u], ATTN_K), :])) * a.astype(F32)

        one_step = jnp.max(reach_ref[...]) <= ATTN_SKIP_BELOW

        @pl.when(one_step)
        def _():
            starts = _attn_keys(blks0)
            a = a_ref[0]
            beta = beta_ref[0].astype(F32)
            de = grad_a(starts, a)
            right, _ = _suffix_sums(de, trii_ref[...], zero)
            dz = de * (1.0 - beta) - (total - right) * beta
            dq_ref[...] = finish(starts, a, dz, jnp.zeros((ATTN_ROWS, HEAD_DIM), F32)).astype(BF)

        @pl.when(jnp.logical_not(one_step))
        def _():
            row = i * ATTN_ROWS + lax.broadcasted_iota(jnp.int32, (ATTN_ROWS, 1), 0)

            def step(carry):
                blks, bounds, dq, seen, run, _ = carry
                starts = _attn_keys(blks)
                z, mask, log_beta, log_rest = _attn_step(q, k_ref, starts, bounds, row)
                tail, run = _suffix_sums(log_rest, tris_ref[...], run)
                a = jnp.where(mask, jnp.exp(log_beta + tail), 0.0).astype(BF)
                de = grad_a(starts, a)
                right, seen = _suffix_sums(de, trii_ref[...], seen)
                beta = jax.nn.sigmoid(z)
                dz = jnp.where(mask, de * (1.0 - beta) - (total - right) * beta, 0.0)
                return (*_attn_next(blks), finish(starts, a, dz, dq), seen, run, _attn_reach(run, blks))

            first = (blks0, bounds0, jnp.zeros((ATTN_ROWS, HEAD_DIM), F32), zero, zero, jnp.float32(0.0))
            dq_ref[...] = lax.while_loop(_attn_more, step, step(first))[2].astype(BF)

        @pl.when(i == nq - 1)
        def _():
            dk_ref[...] = dk_acc[...].astype(BF)
            dv_ref[...] = dv_acc[...].astype(BF)

    qspec = pl.BlockSpec((1, ATTN_ROWS, HEAD_DIM), lambda h, i: (3, i, h))
    rowblk = pl.BlockSpec((ATTN_ROWS, HEAD_DIM), lambda h, i: (i, h))
    head = pl.BlockSpec((t, HEAD_DIM), lambda h, i: (0, h))
    trispec = pl.BlockSpec((ATTN_SUB, ATTN_SUB), lambda h, i: (0, 0))
    return _pallas(
        body, name="attn_bwd", grid=(N_HEADS, nq), ride=ride,
        in_specs=[qspec, _head_cols(4)(t), _head_cols(5)(t), rowblk, rowblk, _first_step_spec(), _first_step_spec(), _reach_spec(),
                  trispec, trispec],
        out_specs=[rowblk, head, head],
        out_shape=[S((t, D_MODEL), BF)] * 3,
        scratch_shapes=[pltpu.VMEM((t, HEAD_DIM), F32), pltpu.VMEM((t, HEAD_DIM), F32)],
        args=[proj, proj, proj, o, d_o, a_first, beta_first, reach, tri_strict, tri_incl])


def _tail(h3, n4, p, w_pg, w_pp, g_ple, g_final, target):
    t = h3.shape[0]
    tm = min(TOKEN_TILE, t)
    steps = t // tm

    def body(h_ref, n_ref, p_ref, wpg_ref, wpp_ref, gp_ref, gf_ref, tgt_ref,
             dh_ref, ds_ref, dpp_ref, loss_ref, dgf_ref, dgp_ref):
        pg = jax.nn.sigmoid(_dot(n_ref[...], wpg_ref[...]))
        pp = _dot(p_ref[...].astype(BF), wpp_ref[...])
        h3v = h_ref[...]
        h4 = h3v + pg * pp
        gf = gf_ref[...]
        diff = _rms(h4, gf) - tgt_ref[...]
        _accumulate(loss_ref, jnp.sum(diff * diff, axis=0, keepdims=True))
        dh4, dgf = _rms_bwd(diff * (1.0 / D_MODEL), h4, gf)
        _accumulate(dgf_ref, dgf)
        dpp_ref[...] = (dh4 * pg).astype(BF)
        ds = (dh4 * pp * pg * (1.0 - pg)).astype(BF)
        ds_ref[...] = ds
        dh3, dgp = _rms_bwd(_dot_nt(ds, wpg_ref[...]), h3v, gp_ref[...])
        _accumulate(dgp_ref, dgp)
        dh_ref[...] = dh4 + dh3

        @pl.when(pl.program_id(0) == steps - 1)
        def _():
            loss_ref[...] = jnp.full(loss_ref.shape, 0.5 / D_MODEL * jnp.sum(loss_ref[...]), F32)

    vec = (1, D_MODEL)
    return pl.pallas_call(
        body, name="tail", grid=(steps,),
        in_specs=[_rows(tm, D_MODEL), _rows(tm, D_MODEL), _rows(tm, PLE_DIM), _const_spec((D_MODEL, D_MODEL)),
                  _const_spec((PLE_DIM, D_MODEL)), _const_spec(vec), _const_spec(vec), _rows(tm, D_MODEL)],
        out_specs=[_rows(tm, D_MODEL)] * 3 + [_acc_spec(vec)] * 3,
        out_shape=[S((t, D_MODEL), F32), S((t, D_MODEL), BF), S((t, D_MODEL), BF)] + [S(vec, F32)] * 3,
    )(h3, n4, p, w_pg, w_pp, g_ple, g_final, target)


def _wgrad(xs, ys, name, ride=None, tile=None):
    bx, t, k = xs.shape
    by, _, n = ys.shape
    b = max(bx, by)
    tt = min(tile or WGRAD_TILE * 2 // xs.dtype.itemsize, t)
    steps = t // tt

    def body(x_ref, y_ref, o_ref, acc_ref):
        s = pl.program_id(1)

        @pl.when(s == 0)
        def _():
            acc_ref[...] = jnp.zeros_like(acc_ref)
        acc_ref[...] += _dot_tn(x_ref[0].astype(BF), y_ref[0].astype(BF))

        @pl.when(s == steps - 1)
        def _():
            o_ref[0] = acc_ref[...].astype(BF)

    (out,), landed = _pallas(
        body, name=name, grid=(b, steps), ride=ride,
        in_specs=[pl.BlockSpec((1, tt, k), (lambda j, s: (j, s, 0)) if bx > 1 else (lambda j, s: (0, s, 0))),
                  pl.BlockSpec((1, tt, n), (lambda j, s: (j, s, 0)) if by > 1 else (lambda j, s: (0, s, 0)))],
        out_specs=[pl.BlockSpec((1, k, n), lambda j, s: (j, 0, 0))],
        out_shape=[S((b, k, n), BF)],
        scratch_shapes=[pltpu.VMEM((k, n), F32)],
        args=[xs, ys])
    return (out, landed) if ride is not None else out


def _wgrad_pairs(xs, ys, name, tile):
    t, k = xs[0].shape
    n = ys[0].shape[1]
    pairs = len(xs)
    tt = min(tile, t)
    steps = t // tt

    def body(*refs):
        x_refs, y_refs, o_refs, acc_ref = refs[:pairs], refs[pairs:2 * pairs], refs[2 * pairs:3 * pairs], refs[3 * pairs]
        p, s = pl.program_id(0), pl.program_id(1)

        @pl.when(s == 0)
        def _():
            acc_ref[...] = jnp.zeros_like(acc_ref)
        for j in range(pairs):
            @pl.when(p == j)
            def _(j=j):
                acc_ref[...] += _dot_tn(x_refs[j][...], y_refs[j][...])

            @pl.when(jnp.logical_and(p == j, s == steps - 1))
            def _(j=j):
                o_refs[j][...] = acc_ref[...].astype(BF)

    def turn(j):
        return lambda p, s: (jnp.where(p < j, 0, jnp.where(p > j, steps - 1, s)), 0)

    return pl.pallas_call(
        body, name=name, grid=(pairs, steps),
        in_specs=[pl.BlockSpec((tt, k), turn(j)) for j in range(pairs)] + [pl.BlockSpec((tt, n), turn(j)) for j in range(pairs)],
        out_specs=[pl.BlockSpec((k, n), lambda p, s: (0, 0))] * pairs,
        out_shape=[S((k, n), BF)] * pairs,
        scratch_shapes=[pltpu.VMEM((k, n), F32)],
    )(*xs, *ys)


def _wgrad_pieces(x, ys, name, ride=None, tile=None, row_parts=1, transposed=False):
    t, k = x.shape
    n = ys[0].shape[2]
    counts = [y.shape[0] for y in ys]
    offsets = [sum(counts[:j]) for j in range(len(ys))]
    total = sum(counts)
    tt = min(tile or WGRAD_TILE, t)
    steps = t // tt
    rows, cols = (n, k) if transposed else (k, n)
    kp = rows // row_parts

    def body(x_ref, *refs):
        y_refs, o_refs, acc_ref = refs[:len(ys)], refs[len(ys):len(ys) + row_parts], refs[len(ys) + row_parts]
        p, s = pl.program_id(0), pl.program_id(1)

        @pl.when(s == 0)
        def _():
            acc_ref[...] = jnp.zeros_like(acc_ref)
        for j, y_ref in enumerate(y_refs):
            @pl.when(jnp.logical_and(p >= offsets[j], p < offsets[j] + counts[j]))
            def _(y_ref=y_ref):
                acc_ref[...] += _dot_tn(y_ref[0], x_ref[...]) if transposed else _dot_tn(x_ref[...], y_ref[0])

        @pl.when(s == steps - 1)
        def _():
            for part, o_ref in enumerate(o_refs):
                o_ref[0] = acc_ref[part * kp:(part + 1) * kp, :].astype(BF)

    def turn(j):
        lo, hi = offsets[j], offsets[j] + counts[j]
        return lambda p, s: (jnp.clip(p - lo, 0, counts[j] - 1), jnp.where(p < lo, 0, jnp.where(p >= hi, steps - 1, s)), 0)

    outs, landed = _pallas(
        body, name=name, grid=(total, steps), ride=ride,
        in_specs=[pl.BlockSpec((tt, k), lambda p, s: (s, 0))] + [pl.BlockSpec((1, tt, n), turn(j)) for j in range(len(ys))],
        out_specs=[pl.BlockSpec((1, kp, cols), lambda p, s: (p, 0, 0))] * row_parts,
        out_shape=[S((total, kp, cols), BF)] * row_parts,
        scratch_shapes=[pltpu.VMEM((rows, cols), F32)],
        args=[x, *ys])
    out = outs[0] if row_parts == 1 else outs
    return (out, landed) if ride is not None else out


def _ffn_bwd_hidden(dh, to_gate, to_up, w_out, name, ride=None):
    t = dh.shape[0]
    tm = min(TOKEN_TILE, t)

    def body(dh_ref, to_gate_ref, to_up_ref, wout_ref, df_ref, dgate_ref, dup_ref):
        df = (0.5 * dh_ref[...]).astype(BF)
        df_ref[...] = df
        for c in range(N_FF_CHUNKS):
            dact = _dot_nt(df, wout_ref[c])
            dgate_ref[c] = (dact * to_gate_ref[c].astype(F32)).astype(BF)
            dup_ref[c] = (dact * to_up_ref[c].astype(F32)).astype(BF)

    return _pallas(
        body, name=name, grid=(t // tm,), ride=ride,
        in_specs=[_rows(tm, D_MODEL), _chunks(tm), _chunks(tm), _const_spec(w_out.shape)],
        out_specs=[_rows(tm, D_MODEL), _chunks(tm), _chunks(tm)],
        out_shape=[S((t, D_MODEL), BF)] + [S((N_FF_CHUNKS, t, FF_CHUNK), BF)] * 2,
        args=[dh, to_gate, to_up, w_out])


def _ffn_bwd_input(dh, h_in, g, dgate, dup, w_in, name, ride=None):
    t = dh.shape[0]
    tm = min(TOKEN_TILE, t)

    def body(dh_ref, h_ref, g_ref, dgate_ref, dup_ref, win_ref, dhi_ref, dg_ref):
        dn = jnp.zeros((tm, D_MODEL), F32)
        for c in range(N_FF_CHUNKS):
            dn = dn + _dot_nt(dgate_ref[c], win_ref[c]) + _dot_nt(dup_ref[c], win_ref[N_FF_CHUNKS + c])
        dhi, dg = _rms_bwd(dn, h_ref[...], g_ref[...])
        _accumulate(dg_ref, dg)
        dhi_ref[...] = dh_ref[...] + dhi

    vec = (1, D_MODEL)
    return _pallas(
        body, name=name, grid=(t // tm,), ride=ride,
        in_specs=[_rows(tm, D_MODEL), _rows(tm, D_MODEL), _const_spec(vec), _chunks(tm), _chunks(tm), _const_spec(w_in.shape)],
        out_specs=[_rows(tm, D_MODEL), _acc_spec(vec)],
        out_shape=[S((t, D_MODEL), F32), S(vec, F32)],
        args=[dh, h_in, g, dgate, dup, w_in])


def _mixer_bwd(dh2, proj, yc, ya, conv_w, w_co, w_ao, w_mo, ride=None):
    t = dh2.shape[0]
    tm = min(TOKEN_TILE, t)

    def body(dh_ref, cb_ref, cc_ref, cx_ref, gc_ref, ga_ref, cch_ref, cxh_ref, yc_ref, ya_ref, cw_ref, wco_ref, wao_ref, wmo_ref,
             dhb_ref, dyc_ref, dya_ref, dgc_ref, dga_ref, dcb_ref, dcv_ref, do_ref):
        dhb = dh_ref[...].astype(BF)
        dhb_ref[...] = dhb
        dmerged = _dot_nt(dhb, wmo_ref[...])
        sc = jax.nn.sigmoid(gc_ref[0].astype(F32))
        sa = jax.nn.sigmoid(ga_ref[0].astype(F32))
        dyc = (dmerged * sc).astype(BF)
        dya = (dmerged * sa).astype(BF)
        dyc_ref[...] = dyc
        dya_ref[...] = dya
        dgc_ref[...] = (dmerged * yc_ref[...].astype(F32) * sc * (1.0 - sc)).astype(BF)
        dga_ref[...] = (dmerged * ya_ref[...].astype(F32) * sa * (1.0 - sa)).astype(BF)
        m, m1, m2 = _conv_inputs(cc_ref, cx_ref, cch_ref, cxh_ref)
        cw = cw_ref[...]
        cv = cw[0:1, :] * m2 + cw[1:2, :] * m1 + cw[2:3, :] * m
        dycin = _dot_nt(dyc, wco_ref[...])
        dcb_ref[...] = (dycin * cv).astype(BF)
        dcv_ref[...] = (dycin * cb_ref[0].astype(F32)).astype(BF)
        do_ref[...] = _dot_nt(dya, wao_ref[...]).astype(BF)

    sq = (D_MODEL, D_MODEL)
    return _pallas(
        body, name="mixer_bwd", grid=(t // tm,), ride=ride,
        in_specs=[_rows(tm, D_MODEL), _piece(0, tm), _piece(1, tm), _piece(2, tm), _piece(6, tm), _piece(7, tm),
                  _prev_halo(1, tm), _prev_halo(2, tm), _rows(tm, D_MODEL), _rows(tm, D_MODEL),
                  _const_spec((3, D_MODEL)), _const_spec(sq), _const_spec(sq), _const_spec(sq)],
        out_specs=[_rows(tm, D_MODEL)] * 8,
        out_shape=[S((t, D_MODEL), BF)] * 8,
        args=[dh2, proj, proj, proj, proj, proj, proj, proj, yc, ya, conv_w, w_co, w_ao, w_mo])


TAP_ROWS = 8


def _conv_bwd(dcv, proj, conv_w):
    t = dcv.shape[0]
    tm = min(TOKEN_TILE, t)
    steps = t // tm

    def body(dcv_ref, nxt_ref, cc_ref, cx_ref, cch_ref, cxh_ref, cw_ref, dcc_ref, dcx_ref, dw_ref):
        i = pl.program_id(0)
        m, m1, m2 = _conv_inputs(cc_ref, cx_ref, cch_ref, cxh_ref)
        d0 = dcv_ref[...].astype(F32)
        nxt = jnp.where(i == steps - 1, 0.0, nxt_ref[...].astype(F32))
        row = lax.broadcasted_iota(jnp.int32, (tm, 1), 0)
        d1 = jnp.where(row == tm - 1, nxt[0:1, :], pltpu.roll(d0, tm - 1, 0))
        d2 = pltpu.roll(d0, tm - 2, 0)
        d2 = jnp.where(row == tm - 2, nxt[0:1, :], jnp.where(row == tm - 1, nxt[1:2, :], d2))
        cw = cw_ref[...]
        dm = cw[2:3, :] * d0 + cw[1:2, :] * d1 + cw[0:1, :] * d2
        dcc_ref[...] = (dm * cx_ref[0].astype(F32)).astype(BF)
        dcx_ref[...] = (dm * cc_ref[0].astype(F32)).astype(BF)
        tap_row = lax.broadcasted_iota(jnp.int32, (TAP_ROWS, 1), 0)
        dw = jnp.zeros((TAP_ROWS, D_MODEL), F32)
        for j, mk in enumerate((m2, m1, m)):
            dw = jnp.where(tap_row == j, jnp.sum(d0 * mk, axis=0, keepdims=True), dw)
        _accumulate(dw_ref, dw)

    nxt_spec = pl.BlockSpec((HALO, D_MODEL), lambda i: (jnp.minimum((i + 1) * (tm // HALO), t // HALO - 1), 0))
    return pl.pallas_call(
        body, name="conv_bwd", grid=(steps,),
        in_specs=[_rows(tm, D_MODEL), nxt_spec, _piece(1, tm), _piece(2, tm), _prev_halo(1, tm), _prev_halo(2, tm),
                  _const_spec((3, D_MODEL))],
        out_specs=[_rows(tm, D_MODEL), _rows(tm, D_MODEL), _acc_spec((TAP_ROWS, D_MODEL))],
        out_shape=[S((t, D_MODEL), BF), S((t, D_MODEL), BF), S((TAP_ROWS, D_MODEL), F32)],
    )(dcv, dcv, proj, proj, proj, proj, conv_w)


def _mix_bwd(dpieces, w_mix, h1, dh2, g, ride=None):
    t = h1.shape[0]
    tm = min(TOKEN_TILE, t)

    def body(*refs):
        pieces, (w_ref, h_ref, dh_ref, g_ref, dhi_ref, dg_ref) = refs[:N_MIX], refs[N_MIX:]
        du = jnp.zeros((tm, D_MODEL), F32)
        for d in range(N_MIX):
            du = du + _dot_nt(pieces[d][...], w_ref[d])
        dhi, dg = _rms_bwd(du, h_ref[...], g_ref[...])
        _accumulate(dg_ref, dg)
        dhi_ref[...] = dh_ref[...] + dhi

    vec = (1, D_MODEL)
    return _pallas(
        body, name="mix_bwd", grid=(t // tm,), ride=ride,
        in_specs=[_rows(tm, D_MODEL)] * N_MIX + [_const_spec(w_mix.shape), _rows(tm, D_MODEL), _rows(tm, D_MODEL), _const_spec(vec)],
        out_specs=[_rows(tm, D_MODEL), _acc_spec(vec)],
        out_shape=[S((t, D_MODEL), F32), S(vec, F32)],
        args=[*dpieces, w_mix, h1, dh2, g])


def _adamw(partials, w, m, v, name):
    parts = list(partials) if isinstance(partials, (list, tuple)) else [partials]
    r, c = w.shape
    tr = next(d for d in (r, 512, 352, 256) if d <= 512 // len(parts) and r % d == 0)
    first_tile = [sum(p.shape[1] for p in parts[:j]) // tr for j in range(len(parts))]
    c1 = 1.0 - ADAM_B1 ** ADAM_STEP
    c2 = 1.0 - ADAM_B2 ** ADAM_STEP

    def body(*refs):
        p_refs, (w_ref, m_ref, v_ref, g_ref, d_ref, mo_ref, vo_ref) = refs[:len(parts)], refs[len(parts):]
        g = None
        for j, p_ref in enumerate(p_refs):
            gj = p_ref[0].astype(F32)
            for s in range(1, N_SHARDS):
                gj = gj + p_ref[s].astype(F32)
            g = gj if g is None else jnp.where(pl.program_id(0) >= first_tile[j], gj, g)
        mn = ADAM_B1 * m_ref[...] + (1.0 - ADAM_B1) * g
        vn = ADAM_B2 * v_ref[...] + (1.0 - ADAM_B2) * (g * g)
        g_ref[...] = g
        mo_ref[...] = mn
        vo_ref[...] = vn
        d_ref[...] = -ADAM_LR * ((mn / c1) / (jnp.sqrt(vn / c2) + ADAM_EPS) + ADAM_WD * w_ref[...])

    def rows_of(j):
        last = parts[j].shape[1] // tr - 1
        return lambda i: (0, jnp.clip(i - first_tile[j], 0, last), 0)

    blk = pl.BlockSpec((tr, c), lambda i: (i, 0))
    return pl.pallas_call(
        body, name=name, grid=(r // tr,),
        in_specs=[pl.BlockSpec((N_SHARDS, tr, c), rows_of(j)) for j in range(len(parts))] + [blk, blk, blk],
        out_specs=[blk] * 4, out_shape=[S((r, c), F32)] * 4,
    )(*parts, w, m, v)


_MATRICES = ("ffn1_w_in", "ffn1_w_out", "w_mix_in", "conv_w", "w_conv_out", "w_attn_out", "w_mix_out",
             "ffn2_w_in", "ffn2_w_out", "w_ple_gate", "w_ple_proj")
_GAINS = ("ffn1_norm", "mix_norm", "ffn2_norm", "ple_norm", "final_norm")
_WEIGHTS = ("ffn1_norm", "ffn1_w_in", "ffn1_w_out", "mix_norm", "w_mix_in", "conv_w", "w_conv_out", "w_attn_out", "w_mix_out",
            "ffn2_norm", "ffn2_w_in", "ffn2_w_out", "ple_norm", "w_ple_gate", "w_ple_proj", "final_norm")
CONV_ROWS = 8
_TRANSPOSED = ("ffn1_w_in", "ffn2_w_in")


def _columns_from_shards(g):
    return jnp.transpose(g, (1, 0, 2)).reshape(g.shape[1], N_SHARDS * g.shape[2])


def _shards_from_columns(a):
    r, c = a.shape
    return jnp.transpose(a.reshape(r, N_SHARDS, c // N_SHARDS), (1, 0, 2))


def kernel(x, p, ffn1_norm, ffn1_w_in, ffn1_w_out, mix_norm, w_mix_in, conv_w, w_conv_out, w_attn_out, w_mix_out, ffn2_norm, ffn2_w_in, ffn2_w_out, ple_norm, w_ple_gate, w_ple_proj, final_norm, loss_target, m_ffn1_norm, m_ffn1_w_in, m_ffn1_w_out, m_mix_norm, m_w_mix_in, m_conv_w, m_w_conv_out, m_w_attn_out, m_w_mix_out, m_ffn2_norm, m_ffn2_w_in, m_ffn2_w_out, m_ple_norm, m_w_ple_gate, m_w_ple_proj, m_final_norm, v_ffn1_norm, v_ffn1_w_in, v_ffn1_w_out, v_mix_norm, v_w_mix_in, v_conv_w, v_w_conv_out, v_w_attn_out, v_w_mix_out, v_ffn2_norm, v_ffn2_w_in, v_ffn2_w_out, v_ple_norm, v_w_ple_gate, v_w_ple_proj, v_final_norm):
    given = dict(locals())
    t = x.shape[1]
    xs = x.reshape(t, D_MODEL)
    ps = p.reshape(t, PLE_DIM)
    target = loss_target.reshape(t, D_MODEL)
    shard = {k: given[k].reshape(given[k].shape[-2:]) for k in _MATRICES}
    gain = {k: given[k].reshape(1, D_MODEL) for k in _GAINS}

    send = {k: shard[k].astype(BF) for k in _MATRICES}
    send["conv_w"] = jnp.pad(shard["conv_w"], ((0, CONV_ROWS - 3), (0, 0)))
    loss_vec, dx, landed, gain_grads = _forward_backward(xs, ps, target, gain, send)
    gain_rows = jnp.concatenate([gain_grads[k] for k in _GAINS] + [loss_vec, jnp.zeros((8 - len(_GAINS) - 1, D_MODEL), F32)], axis=0)
    gain_parts, = _exchange_alone("gather", [gain_rows], "gather_gain_gradients")

    out = {}
    for k in _MATRICES:
        w, m, v = shard[k], given["m_" + k].reshape(shard[k].shape), given["v_" + k].reshape(shard[k].shape)
        part = landed[k]
        if k == "conv_w":
            pad = ((0, CONV_ROWS - 3), (0, 0))
            w, m, v = jnp.pad(w, pad), jnp.pad(m, pad), jnp.pad(v, pad, constant_values=1.0)
        if k in _TRANSPOSED:
            w, m, v = w.T, m.T, v.T
        res = _adamw(part, w, m, v, "adamw_" + k)
        out[k] = [r[:3] if k == "conv_w" else (r.T if k in _TRANSPOSED else r) for r in res]
    stack = lambda pre: jnp.concatenate([given[pre + k].reshape(1, D_MODEL) for k in _GAINS] + [jnp.ones((8 - len(_GAINS), D_MODEL), F32)], axis=0)
    res = _adamw(gain_parts, stack(""), stack("m_"), stack("v_"), "adamw_gains")
    for j, k in enumerate(_GAINS):
        out[k] = [r[j:j + 1] for r in res]

    loss = jnp.sum(gain_parts[:, len(_GAINS), 0])
    per_kind = [[out[k][j].reshape(given[k].shape) for k in _WEIGHTS] for j in range(4)]
    return (loss, dx.reshape(x.shape), *per_kind[0], *per_kind[1], *per_kind[2], *per_kind[3])


def _forward_backward(xs, ps, target, gain, send, full=None):
    exchange = full is None
    full = dict(full or {})
    grads, landed = {}, {}

    def gather(names):
        return ("gather", [send[k] for k in names]) if exchange else None

    def scatter(names):
        return ("scatter", [grads[k] for k in names]) if exchange else None

    def keep(into, names, got):
        into.update(zip(names, got))

    first = ("ffn1_w_in",)
    (n1,), got = _prenorm(xs, gain["ffn1_norm"], ride=gather(first))
    keep(full, first, got)
    w1_in = full["ffn1_w_in"]
    second = ("ffn1_w_out", "w_mix_in")
    (act1, to_gate1, to_up1), got = _ffn_up(n1, w1_in, "ffn1_up", ride=gather(second))
    keep(full, second, got)
    w1_out = full["ffn1_w_out"].reshape(N_FF_CHUNKS, FF_CHUNK, D_MODEL)
    third = ("conv_w", "w_conv_out", "w_attn_out", "w_mix_out")
    (h1, u), got = _ffn_down(xs, act1, w1_out, gain["mix_norm"], "ffn1_down", ride=gather(third))
    keep(full, third, got)
    w_mix = full["w_mix_in"]
    w_co, w_ao, w_mo = (full[k].reshape(D_MODEL, D_MODEL) for k in ("w_conv_out", "w_attn_out", "w_mix_out"))
    taps = _columns_from_shards(full["conv_w"][:, :3, :])
    rest = ("ffn2_w_in", "ffn2_w_out", "w_ple_gate", "w_ple_proj")
    (proj,), got = _mix_proj(u, w_mix, ride=gather(rest))
    keep(full, rest, got)
    w2_in, w2_out = full["ffn2_w_in"], full["ffn2_w_out"].reshape(N_FF_CHUNKS, FF_CHUNK, D_MODEL)
    w_pg = full["w_ple_gate"].reshape(D_MODEL, D_MODEL)
    w_pp = _columns_from_shards(full["w_ple_proj"])
    o, o_bf, a_first, beta_first, reach = _attn_fwd(proj)
    h2, n3, ycin, yc, ya, merged = _mixer_out(proj, o_bf, h1, taps, w_co, w_ao, w_mo, gain["ffn2_norm"])
    (act2, to_gate2, to_up2), _ = _ffn_up(n3, w2_in, "ffn2_up")
    (h3, n4), _ = _ffn_down(h2, act2, w2_out, gain["ple_norm"], "ffn2_down")
    dh3, ds, dpp, loss_vec, dg_final, dg_ple = _tail(h3, n4, ps, w_pg, w_pp, gain["ple_norm"], gain["final_norm"], target)

    one = lambda a: a[None]
    by_rows = lambda g, rows: g.reshape(N_SHARDS, rows // N_SHARDS, D_MODEL)
    square = WGRAD_TILE // 2
    grads["w_ple_gate"] = by_rows(_wgrad(one(n4), one(ds), "wgrad_ple_gate", tile=square), D_MODEL)
    grads["w_ple_proj"] = _shards_from_columns(_wgrad(one(ps), one(dpp), "wgrad_ple_proj")[0])
    ple = ("w_ple_gate", "w_ple_proj")
    (df2, dgate2, dup2), got = _ffn_bwd_hidden(dh3, to_gate2, to_up2, w2_out, "ffn2_bwd_hidden", ride=scatter(ple))
    keep(landed, ple, got)
    grads["ffn2_w_out"] = by_rows(_wgrad(act2, one(df2), "wgrad_ffn2_out"), D_FF)
    grads["ffn2_w_in"] = _wgrad_pieces(n3, [dgate2, dup2], "wgrad_ffn2_in", transposed=True)
    (dh2, dg_ffn2), got = _ffn_bwd_input(dh3, h2, gain["ffn2_norm"], dgate2, dup2, w2_in, "ffn2_bwd_input", ride=scatter(("ffn2_w_out",)))
    keep(landed, ("ffn2_w_out",), got)
    (dh2b, dyc, dya, dgc, dga, dcb, dcv, d_o), _ = _mixer_bwd(dh2, proj, yc, ya, taps, w_co, w_ao, w_mo)
    square_grads = _wgrad_pairs([merged, ycin, o_bf], [dh2b, dyc, dya], "wgrad_mixer_out", tile=square // 2)
    grads["w_mix_out"], grads["w_conv_out"], grads["w_attn_out"] = (by_rows(g, D_MODEL) for g in square_grads)
    dcc, dcx, dtaps = _conv_bwd(dcv, proj, taps)
    grads["conv_w"] = jnp.pad(_shards_from_columns(dtaps[:3]), ((0, 0), (0, CONV_ROWS - 3), (0, 0)))
    behind_attn = ("ffn2_w_in",)
    (dq, dk, dv), got = _attn_bwd(proj, o, d_o, a_first, beta_first, reach, ride=scatter(behind_attn))
    keep(landed, behind_attn, got)
    dpieces = [dcb, dcc, dcx, dq, dk, dv, dgc, dga]
    half = N_MIX // 2
    squares = ("w_mix_out", "w_conv_out", "w_attn_out", "conv_w")
    first_half = _wgrad_pieces(u, [one(dp) for dp in dpieces[:half]], "wgrad_mix_in_a", tile=WGRAD_TILE // 2, row_parts=2, ride=scatter(squares))
    if exchange:
        first_half, got = first_half
        keep(landed, squares, got)
    tops, bottoms = zip(first_half,
                        _wgrad_pieces(u, [one(dp) for dp in dpieces[half:]], "wgrad_mix_in_b", tile=WGRAD_TILE // 2, row_parts=2))
    grads["w_mix_in top"], grads["w_mix_in bottom"] = jnp.concatenate(tops, axis=0), jnp.concatenate(bottoms, axis=0)
    (dh1, dg_mix), top = _mix_bwd(dpieces, w_mix, h1, dh2, gain["mix_norm"], ride=scatter(("w_mix_in top",)))
    (df1, dgate1, dup1), bottom = _ffn_bwd_hidden(dh1, to_gate1, to_up1, w1_out, "ffn1_bwd_hidden", ride=scatter(("w_mix_in bottom",)))
    if exchange:
        landed["w_mix_in"] = [top[0], bottom[0]]
    else:
        grads["w_mix_in"] = jnp.concatenate([grads.pop("w_mix_in top"), grads.pop("w_mix_in bottom")], axis=1)
    grads["ffn1_w_out"] = by_rows(_wgrad(act1, one(df1), "wgrad_ffn1_out"), D_FF)
    if exchange:
        grads["ffn1_w_in"], got = _wgrad_pieces(n1, [dgate1, dup1], "wgrad_ffn1_in", transposed=True, ride=scatter(("ffn1_w_out",)))
        keep(landed, ("ffn1_w_out",), got)
    else:
        grads["ffn1_w_in"] = _wgrad_pieces(n1, [dgate1, dup1], "wgrad_ffn1_in", transposed=True)
    (dx, dg_ffn1), got = _ffn_bwd_input(dh1, xs, gain["ffn1_norm"], dgate1, dup1, w1_in, "ffn1_bwd_input", ride=scatter(("ffn1_w_in",)))
    keep(landed, ("ffn1_w_in",), got)
    gain_grads = dict(ffn1_norm=dg_ffn1, mix_norm=dg_mix, ffn2_norm=dg_ffn2, ple_norm=dg_ple, final_norm=dg_final)
    return loss_vec, dx, (landed if exchange else grads), gain_grads
```

```python
import functools
import math

import jax
import jax.numpy as jnp
from jax import lax
from jax.experimental import pallas as pl
from jax.experimental.pallas import tpu as pltpu

D_MODEL = 1024
D_FF = 2816
N_SHARDS = 8
FF_CHUNK = 2 * D_FF // N_SHARDS
N_FF_CHUNKS = D_FF // FF_CHUNK
N_HEADS = 8
HEAD_DIM = 128
PLE_DIM = 256
NORM_EPS = 1e-6
N_MIX = 8
ADAM_LR, ADAM_B1, ADAM_B2, ADAM_EPS, ADAM_WD, ADAM_STEP = 0.001, 0.9, 0.999, 1e-08, 0.01, 10

TOKEN_TILE = 512
WGRAD_TILE = 4096
PROJ_TILE = 2048
ATTN_ROWS = 1024
ATTN_Q = 64
ATTN_SUB = 128
ATTN_K = 2 * ATTN_SUB
ATTN_SKIP_BELOW = -90.0

BF = jnp.bfloat16
F32 = jnp.float32
MESH = pl.DeviceIdType.MESH
NT = (((1,), (1,)), ((), ()))
TN = (((0,), (0,)), ((), ()))
S = jax.ShapeDtypeStruct
ANY = pl.BlockSpec(memory_space=pl.ANY)


def _const_spec(shape):
    nd = len(shape)
    return pl.BlockSpec(shape, lambda *_: (0,) * nd, pipeline_mode=pl.Buffered(1))


def _rows(tm, cols):
    return pl.BlockSpec((tm, cols), lambda i: (i, 0))


def _chunks(tm):
    return pl.BlockSpec((N_FF_CHUNKS, tm, FF_CHUNK), lambda i: (0, i, 0))


def _acc_spec(shape):
    nd = len(shape)
    return pl.BlockSpec(shape, lambda *_: (0,) * nd)


def _dot(a, b):
    return jnp.dot(a, b, preferred_element_type=F32)


def _dot_nt(a, b):
    return lax.dot_general(a, b, NT, preferred_element_type=F32)


def _dot_tn(a, b):
    return lax.dot_general(a, b, TN, preferred_element_type=F32)


def _rms(h, g):
    r = lax.rsqrt(jnp.mean(h * h, axis=-1, keepdims=True) + NORM_EPS)
    return h * r * g


def _rms_bwd(dn, h, g):
    r = lax.rsqrt(jnp.mean(h * h, axis=-1, keepdims=True) + NORM_EPS)
    nh = h * r
    gd = dn * g
    dh = r * (gd - nh * jnp.mean(gd * nh, axis=-1, keepdims=True))
    return dh, jnp.sum(dn * nh, axis=0, keepdims=True)


def _accumulate(ref, val):
    @pl.when(pl.program_id(0) == 0)
    def _():
        ref[...] = jnp.zeros_like(ref)
    ref[...] += val


def _place():
    x, y, c = lax.axis_index("x"), lax.axis_index("y"), lax.axis_index("c")
    return x, y, c


def _slot(px, py, pc):
    return 4 * px + 2 * py + pc


def _gather_phases(ins, outs, send_sems, recv_sems, local_sems):
    n = len(ins)

    def parties():
        x, y, c = _place()
        return (x, y, c), (x, y, 1 - c), [(1 - x, y), (x, 1 - y), (1 - x, 1 - y)], c

    def copy(a, k, block, to, src=None):
        dst = outs[a].at[_slot(*block)]
        return pltpu.make_async_remote_copy(
            src_ref=dst if src is None else src, dst_ref=dst,
            send_sem=send_sems.at[a, k], recv_sem=recv_sems.at[a, k],
            device_id=to, device_id_type=MESH)

    def own(a, me):
        return pltpu.make_async_copy(ins[a], outs[a].at[_slot(*me)], local_sems.at[a])

    def first(a, me, sibling, chips, c):
        return [copy(a, 0, me, sibling, src=ins[a])] + [copy(a, 1 + j, me, (*chip, c), src=ins[a]) for j, chip in enumerate(chips)]

    def start():
        me, sibling, chips, c = parties()
        for a in range(n):
            own(a, me).start()
        for a in range(n):
            for cp in first(a, me, sibling, chips, c):
                cp.start()

    def forward():
        me, sibling, chips, c = parties()
        for j, chip in enumerate(chips):
            for a in range(n):
                copy(a, 1 + j, (*chip, c), me).wait_recv()
                copy(a, 4 + j, (*chip, c), sibling).start()

    def finish():
        me, sibling, chips, c = parties()
        for a in range(n):
            copy(a, 0, sibling, me).wait_recv()
            for j, chip in enumerate(chips):
                copy(a, 4 + j, (*chip, 1 - c), me).wait_recv()
        for a in range(n):
            for cp in first(a, me, sibling, chips, c) + [copy(a, 4 + j, (*chip, c), sibling) for j, chip in enumerate(chips)]:
                cp.wait_send()
            own(a, me).wait()

    return [start, forward, finish]


def _scatter_phases(ins, outs, send_sems, recv_sems, local_sems):
    n = len(ins)

    def copies():
        x, y, c = _place()
        me = _slot(x, y, c)
        out = [pltpu.make_async_copy(ins[a].at[me], outs[a].at[me], local_sems.at[a]) for a in range(n)]
        for k in range(1, N_SHARDS):
            px = 1 - x if k & 4 else x
            py = 1 - y if k & 2 else y
            pc = 1 - c if k & 1 else c
            for a in range(n):
                out.append(pltpu.make_async_remote_copy(
                    src_ref=ins[a].at[_slot(px, py, pc)], dst_ref=outs[a].at[me],
                    send_sem=send_sems.at[a, k - 1], recv_sem=recv_sems.at[a, k - 1],
                    device_id=(px, py, pc), device_id_type=MESH))
        return out

    def start():
        for cp in copies():
            cp.start()

    def finish():
        for cp in copies():
            cp.wait()

    return [start, finish]


def _pallas(body, *, name, grid, in_specs, out_specs, out_shape, args, scratch_shapes=(), ride=None):
    if ride is None:
        outs = pl.pallas_call(body, name=name, grid=grid, in_specs=in_specs, out_specs=out_specs, out_shape=out_shape,
                              scratch_shapes=list(scratch_shapes))(*args)
        return list(outs), []
    kind, arrays = ride
    n, n_in, n_out, n_scr = len(arrays), len(in_specs), len(out_specs), len(scratch_shapes)
    total = math.prod(grid)
    middle = (9 * total) // 10
    landed_shape = [S((N_SHARDS,) + a.shape if kind == "gather" else a.shape, a.dtype) for a in arrays]

    def with_exchange(*refs):
        ins, riders_in = refs[:n_in], refs[n_in:n_in + n]
        outs, riders_out = refs[n_in + n:n_in + n + n_out], refs[n_in + n + n_out:n_in + 2 * n + n_out]
        scratch, sems = refs[n_in + 2 * n + n_out:n_in + 2 * n + n_out + n_scr], refs[n_in + 2 * n + n_out + n_scr:]
        step = 0
        for axis, size in enumerate(grid):
            step = step * size + pl.program_id(axis)
        phases = (_gather_phases if kind == "gather" else _scatter_phases)(riders_in, riders_out, *sems)
        pl.when(step == 0)(phases[0])
        body(*ins, *outs, *scratch)
        for phase in phases[1:-1]:
            pl.when(step == middle)(phase)
        pl.when(step == total - 1)(phases[-1])

    outs = pl.pallas_call(
        with_exchange, name=name, grid=grid,
        in_specs=list(in_specs) + [ANY] * n, out_specs=list(out_specs) + [ANY] * n,
        out_shape=list(out_shape) + landed_shape,
        scratch_shapes=list(scratch_shapes) + [pltpu.SemaphoreType.DMA((n, 7)), pltpu.SemaphoreType.DMA((n, 7)),
                                               pltpu.SemaphoreType.DMA((n,))],
    )(*args, *arrays)
    return list(outs[:n_out]), list(outs[n_out:])


def _exchange_alone(kind, arrays, name):
    return _pallas(lambda: None, name=name, grid=(1,), in_specs=[], out_specs=[], out_shape=[], args=[], ride=(kind, arrays))[1]


def _prenorm(x, g, ride=None):
    t = x.shape[0]
    tm = min(TOKEN_TILE, t)

    def body(x_ref, g_ref, n_ref):
        n_ref[...] = _rms(x_ref[...], g_ref[...]).astype(BF)

    return _pallas(
        body, name="prenorm", grid=(t // tm,), ride=ride,
        in_specs=[_rows(tm, D_MODEL), _const_spec((1, D_MODEL))], out_specs=[_rows(tm, D_MODEL)],
        out_shape=[S((t, D_MODEL), BF)], args=[x, g])


def _ffn_up(n, w_in, name, ride=None):
    t = n.shape[0]
    tm = min(TOKEN_TILE, t)

    def body(n_ref, win_ref, act_ref, to_gate_ref, to_up_ref):
        nb = n_ref[...]
        for c in range(N_FF_CHUNKS):
            gate = _dot(nb, win_ref[c])
            up = _dot(nb, win_ref[N_FF_CHUNKS + c])
            sg = jax.nn.sigmoid(gate)
            silu = gate * sg
            act_ref[c] = (silu * up).astype(BF)
            to_gate_ref[c] = (up * (sg * (1.0 + gate * (1.0 - sg)))).astype(BF)
            to_up_ref[c] = silu.astype(BF)

    return _pallas(
        body, name=name, grid=(t // tm,), ride=ride,
        in_specs=[_rows(tm, D_MODEL), _const_spec(w_in.shape)],
        out_specs=[_chunks(tm)] * 3, out_shape=[S((N_FF_CHUNKS, t, FF_CHUNK), BF)] * 3,
        args=[n, w_in])


def _ffn_down(h, act, w_out, g_next, name, ride=None):
    t = h.shape[0]
    tm = min(TOKEN_TILE, t)

    def body(h_ref, act_ref, wout_ref, g_ref, ho_ref, no_ref):
        acc = jnp.zeros((tm, D_MODEL), F32)
        for c in range(N_FF_CHUNKS):
            acc = acc + _dot(act_ref[c], wout_ref[c])
        ho = h_ref[...] + 0.5 * acc
        ho_ref[...] = ho
        no_ref[...] = _rms(ho, g_ref[...]).astype(BF)

    return _pallas(
        body, name=name, grid=(t // tm,), ride=ride,
        in_specs=[_rows(tm, D_MODEL), _chunks(tm), _const_spec(w_out.shape), _const_spec((1, D_MODEL))],
        out_specs=[_rows(tm, D_MODEL)] * 2, out_shape=[S((t, D_MODEL), F32), S((t, D_MODEL), BF)],
        args=[h, act, w_out, g_next])


def _mix_proj(u, w_mix, ride=None):
    t = u.shape[0]
    tm = min(PROJ_TILE, t)

    def body(u_ref, w_ref, o_ref):
        o_ref[0] = _dot(u_ref[...], w_ref[0]).astype(BF)

    return _pallas(
        body, name="mix_proj", grid=(N_MIX, t // tm), ride=ride,
        in_specs=[pl.BlockSpec((tm, D_MODEL), lambda d, i: (i, 0)), pl.BlockSpec((1, D_MODEL, D_MODEL), lambda d, i: (d, 0, 0))],
        out_specs=[pl.BlockSpec((1, tm, D_MODEL), lambda d, i: (d, i, 0))],
        out_shape=[S((N_MIX, t, D_MODEL), BF)], args=[u, w_mix])


HALO = 16


def _piece(d, tm):
    return pl.BlockSpec((1, tm, D_MODEL), lambda i: (d, i, 0))


def _prev_halo(d, tm):
    return pl.BlockSpec((1, HALO, D_MODEL), lambda i: (d, jnp.maximum(i * (tm // HALO) - 1, 0), 0))


def _shift_down(m, prev_tail, k):
    tm = m.shape[0]
    out = pltpu.roll(m, k, 0)
    row = lax.broadcasted_iota(jnp.int32, (tm, 1), 0)
    for j in range(k):
        out = jnp.where(row == j, prev_tail[HALO - k + j:HALO - k + j + 1, :], out)
    return out


def _conv_inputs(cc_ref, cx_ref, cch_ref, cxh_ref):
    m = cc_ref[0].astype(F32) * cx_ref[0].astype(F32)
    mh = cch_ref[0].astype(F32) * cxh_ref[0].astype(F32)
    mh = jnp.where(pl.program_id(0) == 0, 0.0, mh)
    return m, _shift_down(m, mh, 1), _shift_down(m, mh, 2)


def _mixer_out(proj, o, h1, conv_w, w_co, w_ao, w_mo, g_next):
    t = h1.shape[0]
    tm = min(TOKEN_TILE, t)

    def body(cb_ref, cc_ref, cx_ref, gc_ref, ga_ref, cch_ref, cxh_ref, o_ref, h_ref, cw_ref, wco_ref, wao_ref, wmo_ref,
             g_ref, ho_ref, no_ref, ycin_ref, yc_ref, ya_ref, mg_ref):
        m, m1, m2 = _conv_inputs(cc_ref, cx_ref, cch_ref, cxh_ref)
        cw = cw_ref[...]
        cv = cw[0:1, :] * m2 + cw[1:2, :] * m1 + cw[2:3, :] * m
        ycin = (cb_ref[0].astype(F32) * cv).astype(BF)
        ycin_ref[...] = ycin
        yc = _dot(ycin, wco_ref[...])
        ya = _dot(o_ref[...], wao_ref[...])
        yc_ref[...] = yc.astype(BF)
        ya_ref[...] = ya.astype(BF)
        merged = (jax.nn.sigmoid(gc_ref[0].astype(F32)) * yc + jax.nn.sigmoid(ga_ref[0].astype(F32)) * ya).astype(BF)
        mg_ref[...] = merged
        ho = h_ref[...] + _dot(merged, wmo_ref[...])
        ho_ref[...] = ho
        no_ref[...] = _rms(ho, g_ref[...]).astype(BF)

    sq = (D_MODEL, D_MODEL)
    return pl.pallas_call(
        body, name="mixer_out", grid=(t // tm,),
        in_specs=[_piece(0, tm), _piece(1, tm), _piece(2, tm), _piece(6, tm), _piece(7, tm), _prev_halo(1, tm), _prev_halo(2, tm),
                  _rows(tm, D_MODEL), _rows(tm, D_MODEL), _const_spec((3, D_MODEL)), _const_spec(sq), _const_spec(sq),
                  _const_spec(sq), _const_spec((1, D_MODEL))],
        out_specs=[_rows(tm, D_MODEL)] * 6,
        out_shape=[S((t, D_MODEL), F32)] + [S((t, D_MODEL), BF)] * 5,
    )(proj, proj, proj, proj, proj, proj, proj, o, h1, conv_w, w_co, w_ao, w_mo, g_next)


def _suffix_sums(vals, tri, before):
    out, right = [], before
    for b in reversed(range(ATTN_K // ATTN_SUB)):
        v = vals[:, b * ATTN_SUB:(b + 1) * ATTN_SUB]
        hi = v.astype(BF)
        lo = (v - hi.astype(F32)).astype(BF)
        out.append(_dot(hi, tri) + _dot(lo, tri) + right)
        right = right + jnp.sum(v, axis=1, keepdims=True)
    return jnp.concatenate(out[::-1], axis=1), right


ATTN_UNITS = ATTN_ROWS // ATTN_Q


def _unit_rows(x, u):
    return x[u * ATTN_Q:(u + 1) * ATTN_Q]


def _per_unit(fn):
    return jnp.concatenate([fn(u) for u in range(ATTN_UNITS)], axis=0)


def _per_row(vals):
    local = lax.broadcasted_iota(jnp.int32, (ATTN_ROWS, 1), 0)
    out = jnp.full((ATTN_ROWS, 1), vals[0], jnp.int32)
    for u in range(1, ATTN_UNITS):
        out = jnp.where(local >= u * ATTN_Q, vals[u], out)
    return out


def _attn_step(q, k_ref, starts, bounds, row):
    z = _per_unit(lambda u: _dot_nt(_unit_rows(q, u), k_ref[0, pl.ds(starts[u], ATTN_K), :])) * (1.0 / math.sqrt(HEAD_DIM))
    mask = lax.broadcasted_iota(jnp.int32, (1, ATTN_K), 1) < jnp.minimum(row, _per_row(bounds)) - _per_row(starts)
    log_beta = jnp.minimum(z, 0.0) - jnp.log(1.0 + jnp.exp(jnp.minimum(z, -z)))
    log_rest = jnp.where(mask, log_beta - z, 0.0)
    return z, mask, log_beta, log_rest


def _attn_sweep_start(i, t):
    blks = tuple(jnp.maximum(i * ATTN_UNITS + u + 1 - ATTN_K // ATTN_Q, 0) for u in range(ATTN_UNITS))
    return blks, tuple(jnp.int32(t) for _ in range(ATTN_UNITS))


def _attn_keys(blks):
    return [pl.multiple_of(b * ATTN_Q, ATTN_Q) for b in blks]


def _attn_next(blks):
    return tuple(jnp.maximum(b - ATTN_K // ATTN_Q, 0) for b in blks), tuple(b * ATTN_Q for b in blks)


def _attn_reach(run, blks):
    done_rows = sum(jnp.where(b > 0, 0, ATTN_Q) for b in blks)
    local = lax.broadcasted_iota(jnp.int32, (ATTN_ROWS, 1), 0)
    return jnp.max(jnp.where(local >= done_rows, run, float(jnp.finfo(F32).min)))


def _attn_more(carry):
    return carry[-1] > ATTN_SKIP_BELOW


def _tri(strict):
    r = lax.broadcasted_iota(jnp.int32, (ATTN_SUB, ATTN_SUB), 0)
    c = lax.broadcasted_iota(jnp.int32, (ATTN_SUB, ATTN_SUB), 1)
    return (r > c if strict else r >= c).astype(BF)


REACH_TILE = (8, 128)


def _first_step_spec():
    return pl.BlockSpec((1, ATTN_ROWS, ATTN_K), lambda h, i: (h, i, 0))


def _reach_spec():
    return pl.BlockSpec((1, 1) + REACH_TILE, lambda h, i: (h, i, 0, 0))


def _attn_keys_values(k_hbm, v_hbm, k_buf, v_buf, sems):
    h, i = pl.program_id(0), pl.program_id(1)
    slot = h % 2

    def fetch(head, into):
        cols = pl.ds(pl.multiple_of(head * HEAD_DIM, HEAD_DIM), HEAD_DIM)
        return [pltpu.make_async_copy(k_hbm.at[4, :, cols], k_buf.at[into], sems.at[0, into]),
                pltpu.make_async_copy(v_hbm.at[5, :, cols], v_buf.at[into], sems.at[1, into])]

    @pl.when(jnp.logical_and(h == 0, i == 0))
    def _():
        for c in fetch(0, 0):
            c.start()

    @pl.when(i == 0)
    def _():
        for c in fetch(h, slot):
            c.wait()

    @pl.when(jnp.logical_and(i == 0, h + 1 < N_HEADS))
    def _():
        for c in fetch(h + 1, 1 - slot):
            c.start()

    return k_buf.at[pl.ds(slot, 1)], v_buf.at[pl.ds(slot, 1)]


def _attn_kv_scratch(t):
    return [pltpu.VMEM((2, t, HEAD_DIM), BF), pltpu.VMEM((2, t, HEAD_DIM), BF), pltpu.SemaphoreType.DMA((2, 2))]


def _attn_fwd(proj):
    t = proj.shape[1]
    nq = t // ATTN_ROWS
    tri = _tri(strict=True)

    def body(q_ref, k_hbm, v_hbm, tri_ref, o_ref, ob_ref, a_ref, beta_ref, reach_ref, k_buf, v_buf, sems):
        i = pl.program_id(1)
        k_ref, v_ref = _attn_keys_values(k_hbm, v_hbm, k_buf, v_buf, sems)
        q = q_ref[0]
        row = i * ATTN_ROWS + lax.broadcasted_iota(jnp.int32, (ATTN_ROWS, 1), 0)

        def step(carry, keep=False):
            blks, bounds, acc, run, _ = carry
            starts = _attn_keys(blks)
            _, mask, log_beta, log_rest = _attn_step(q, k_ref, starts, bounds, row)
            tail, run = _suffix_sums(log_rest, tri_ref[...], run)
            a = jnp.where(mask, jnp.exp(log_beta + tail), 0.0).astype(BF)
            if keep:
                a_ref[0] = a
                beta_ref[0] = jnp.where(mask, jnp.exp(log_beta), 0.0).astype(BF)
            acc = acc + _per_unit(lambda u: _dot(_unit_rows(a, u), v_ref[0, pl.ds(starts[u], ATTN_K), :]))
            return (*_attn_next(blks), acc, run, _attn_reach(run, blks))

        first = (*_attn_sweep_start(i, t), jnp.zeros((ATTN_ROWS, HEAD_DIM), F32), jnp.zeros((ATTN_ROWS, 1), F32), jnp.float32(0.0))
        after_first = step(first, keep=True)
        reach_ref[...] = jnp.full(reach_ref.shape, after_first[-1], F32)
        o = lax.while_loop(_attn_more, step, after_first)[2]
        o_ref[...] = o
        ob_ref[...] = o.astype(BF)

    qspec = pl.BlockSpec((1, ATTN_ROWS, HEAD_DIM), lambda h, i: (3, i, h))
    rowblk = pl.BlockSpec((ATTN_ROWS, HEAD_DIM), lambda h, i: (i, h))
    return pl.pallas_call(
        body, name="attn_fwd", grid=(N_HEADS, nq),
        in_specs=[qspec, pl.BlockSpec(memory_space=pl.ANY), pl.BlockSpec(memory_space=pl.ANY),
                  pl.BlockSpec((ATTN_SUB, ATTN_SUB), lambda h, i: (0, 0))],
        out_specs=[rowblk, rowblk, _first_step_spec(), _first_step_spec(), _reach_spec()],
        out_shape=[S((t, D_MODEL), F32), S((t, D_MODEL), BF), S((N_HEADS, t, ATTN_K), BF), S((N_HEADS, t, ATTN_K), BF),
                   S((N_HEADS, nq) + REACH_TILE, F32)],
        scratch_shapes=_attn_kv_scratch(t),
    )(proj, proj, proj, tri)


def _attn_bwd(proj, o, d_o, a_first, beta_first, reach, ride=None):
    t = proj.shape[1]
    nq = t // ATTN_ROWS
    tri_strict, tri_incl = _tri(strict=True), _tri(strict=False)
    scale = 1.0 / math.sqrt(HEAD_DIM)

    def body(q_ref, k_hbm, v_hbm, o_ref, do_ref, a_ref, beta_ref, reach_ref, tris_ref, trii_ref, dq_ref, dk_ref, dv_ref, dk_acc, dv_acc,
             k_buf, v_buf, sems):
        i = pl.program_id(1)
        k_ref, v_ref = _attn_keys_values(k_hbm, v_hbm, k_buf, v_buf, sems)

        @pl.when(i == 0)
        def _():
            dk_acc[...] = jnp.zeros_like(dk_acc)
            dv_acc[...] = jnp.zeros_like(dv_acc)

        q = q_ref[0]
        do = do_ref[...]
        total = jnp.sum(do.astype(F32) * o_ref[...], axis=1, keepdims=True)
        zero = jnp.zeros((ATTN_ROWS, 1), F32)
        blks0, bounds0 = _attn_sweep_start(i, t)

        def finish(starts, a, dz, dq):
            dzb = (dz * scale).astype(BF)
            for u in range(ATTN_UNITS):
                dv_acc[pl.ds(starts[u], ATTN_K), :] += _dot_tn(_unit_rows(a, u), _unit_rows(do, u))
                dk_acc[pl.ds(starts[u], ATTN_K), :] += _dot_tn(_unit_rows(dzb, u), _unit_rows(q, u))
            return dq + _per_unit(lambda u: _dot(_unit_rows(dzb, u), k_ref[0, pl.ds(starts[u], ATTN_K), :]))

        def grad_a(starts, a):
            return _per_unit(lambda u: _dot_nt(_unit_rows(do, u), v_ref[0, pl.ds(starts[u], ATTN_K), :])) * a.astype(F32)

        one_step = jnp.max(reach_ref[...]) <= ATTN_SKIP_BELOW

        @pl.when(one_step)
        def _():
            starts = _attn_keys(blks0)
            a = a_ref[0]
            beta = beta_ref[0].astype(F32)
            de = grad_a(starts, a)
            right, _ = _suffix_sums(de, trii_ref[...], zero)
            dz = de * (1.0 - beta) - (total - right) * beta
            dq_ref[...] = finish(starts, a, dz, jnp.zeros((ATTN_ROWS, HEAD_DIM), F32)).astype(BF)

        @pl.when(jnp.logical_not(one_step))
        def _():
            row = i * ATTN_ROWS + lax.broadcasted_iota(jnp.int32, (ATTN_ROWS, 1), 0)

            def step(carry):
                blks, bounds, dq, seen, run, _ = carry
                starts = _attn_keys(blks)
                z, mask, log_beta, log_rest = _attn_step(q, k_ref, starts, bounds, row)
                tail, run = _suffix_sums(log_rest, tris_ref[...], run)
                a = jnp.where(mask, jnp.exp(log_beta + tail), 0.0).astype(BF)
                de = grad_a(starts, a)
                right, seen = _suffix_sums(de, trii_ref[...], seen)
                beta = jax.nn.sigmoid(z)
                dz = jnp.where(mask, de * (1.0 - beta) - (total - right) * beta, 0.0)
                return (*_attn_next(blks), finish(starts, a, dz, dq), seen, run, _attn_reach(run, blks))

            first = (blks0, bounds0, jnp.zeros((ATTN_ROWS, HEAD_DIM), F32), zero, zero, jnp.float32(0.0))
            dq_ref[...] = lax.while_loop(_attn_more, step, step(first))[2].astype(BF)

        @pl.when(i == nq - 1)
        def _():
            dk_ref[...] = dk_acc[...].astype(BF)
            dv_ref[...] = dv_acc[...].astype(BF)

    qspec = pl.BlockSpec((1, ATTN_ROWS, HEAD_DIM), lambda h, i: (3, i, h))
    rowblk = pl.BlockSpec((ATTN_ROWS, HEAD_DIM), lambda h, i: (i, h))
    head = pl.BlockSpec((t, HEAD_DIM), lambda h, i: (0, h))
    trispec = pl.BlockSpec((ATTN_SUB, ATTN_SUB), lambda h, i: (0, 0))
    return _pallas(
        body, name="attn_bwd", grid=(N_HEADS, nq), ride=ride,
        in_specs=[qspec, pl.BlockSpec(memory_space=pl.ANY), pl.BlockSpec(memory_space=pl.ANY), rowblk, rowblk,
                  _first_step_spec(), _first_step_spec(), _reach_spec(), trispec, trispec],
        out_specs=[rowblk, head, head],
        out_shape=[S((t, D_MODEL), BF)] * 3,
        scratch_shapes=[pltpu.VMEM((t, HEAD_DIM), F32), pltpu.VMEM((t, HEAD_DIM), F32)] + _attn_kv_scratch(t),
        args=[proj, proj, proj, o, d_o, a_first, beta_first, reach, tri_strict, tri_incl])


def _tail(h3, n4, p, w_pg, w_pp, g_ple, g_final, target):
    t = h3.shape[0]
    tm = min(TOKEN_TILE, t)
    steps = t // tm

    def body(h_ref, n_ref, p_ref, wpg_ref, wpp_ref, gp_ref, gf_ref, tgt_ref,
             dh_ref, ds_ref, dpp_ref, loss_ref, dgf_ref, dgp_ref):
        pg = jax.nn.sigmoid(_dot(n_ref[...], wpg_ref[...]))
        pp = _dot(p_ref[...].astype(BF), wpp_ref[...])
        h3v = h_ref[...]
        h4 = h3v + pg * pp
        gf = gf_ref[...]
        diff = _rms(h4, gf) - tgt_ref[...]
        _accumulate(loss_ref, jnp.sum(diff * diff, axis=0, keepdims=True))
        dh4, dgf = _rms_bwd(diff * (1.0 / D_MODEL), h4, gf)
        _accumulate(dgf_ref, dgf)
        dpp_ref[...] = (dh4 * pg).astype(BF)
        ds = (dh4 * pp * pg * (1.0 - pg)).astype(BF)
        ds_ref[...] = ds
        dh3, dgp = _rms_bwd(_dot_nt(ds, wpg_ref[...]), h3v, gp_ref[...])
        _accumulate(dgp_ref, dgp)
        dh_ref[...] = dh4 + dh3

        @pl.when(pl.program_id(0) == steps - 1)
        def _():
            loss_ref[...] = jnp.full(loss_ref.shape, 0.5 / D_MODEL * jnp.sum(loss_ref[...]), F32)

    vec = (1, D_MODEL)
    return pl.pallas_call(
        body, name="tail", grid=(steps,),
        in_specs=[_rows(tm, D_MODEL), _rows(tm, D_MODEL), _rows(tm, PLE_DIM), _const_spec((D_MODEL, D_MODEL)),
                  _const_spec((PLE_DIM, D_MODEL)), _const_spec(vec), _const_spec(vec), _rows(tm, D_MODEL)],
        out_specs=[_rows(tm, D_MODEL)] * 3 + [_acc_spec(vec)] * 3,
        out_shape=[S((t, D_MODEL), F32), S((t, D_MODEL), BF), S((t, D_MODEL), BF)] + [S(vec, F32)] * 3,
    )(h3, n4, p, w_pg, w_pp, g_ple, g_final, target)


def _wgrad(xs, ys, name, ride=None, tile=None):
    bx, t, k = xs.shape
    by, _, n = ys.shape
    b = max(bx, by)
    tt = min(tile or WGRAD_TILE * 2 // xs.dtype.itemsize, t)
    steps = t // tt

    def body(x_ref, y_ref, o_ref, acc_ref):
        s = pl.program_id(1)

        @pl.when(s == 0)
        def _():
            acc_ref[...] = jnp.zeros_like(acc_ref)
        acc_ref[...] += _dot_tn(x_ref[0].astype(BF), y_ref[0].astype(BF))

        @pl.when(s == steps - 1)
        def _():
            o_ref[0] = acc_ref[...].astype(BF)

    (out,), landed = _pallas(
        body, name=name, grid=(b, steps), ride=ride,
        in_specs=[pl.BlockSpec((1, tt, k), (lambda j, s: (j, s, 0)) if bx > 1 else (lambda j, s: (0, s, 0))),
                  pl.BlockSpec((1, tt, n), (lambda j, s: (j, s, 0)) if by > 1 else (lambda j, s: (0, s, 0)))],
        out_specs=[pl.BlockSpec((1, k, n), lambda j, s: (j, 0, 0))],
        out_shape=[S((b, k, n), BF)],
        scratch_shapes=[pltpu.VMEM((k, n), F32)],
        args=[xs, ys])
    return (out, landed) if ride is not None else out


def _wgrad_pairs(xs, ys, name, tile):
    t, k = xs[0].shape
    n = ys[0].shape[1]
    pairs = len(xs)
    tt = min(tile, t)
    steps = t // tt

    def body(*refs):
        x_refs, y_refs, o_refs, acc_ref = refs[:pairs], refs[pairs:2 * pairs], refs[2 * pairs:3 * pairs], refs[3 * pairs]
        p, s = pl.program_id(0), pl.program_id(1)

        @pl.when(s == 0)
        def _():
            acc_ref[...] = jnp.zeros_like(acc_ref)
        for j in range(pairs):
            @pl.when(p == j)
            def _(j=j):
                acc_ref[...] += _dot_tn(x_refs[j][...], y_refs[j][...])

            @pl.when(jnp.logical_and(p == j, s == steps - 1))
            def _(j=j):
                o_refs[j][...] = acc_ref[...].astype(BF)

    def turn(j):
        return lambda p, s: (jnp.where(p < j, 0, jnp.where(p > j, steps - 1, s)), 0)

    return pl.pallas_call(
        body, name=name, grid=(pairs, steps),
        in_specs=[pl.BlockSpec((tt, k), turn(j)) for j in range(pairs)] + [pl.BlockSpec((tt, n), turn(j)) for j in range(pairs)],
        out_specs=[pl.BlockSpec((k, n), lambda p, s: (0, 0))] * pairs,
        out_shape=[S((k, n), BF)] * pairs,
        scratch_shapes=[pltpu.VMEM((k, n), F32)],
    )(*xs, *ys)


def _wgrad_pieces(x, ys, name, ride=None, tile=None, row_parts=1, transposed=False):
    t, k = x.shape
    n = ys[0].shape[2]
    counts = [y.shape[0] for y in ys]
    offsets = [sum(counts[:j]) for j in range(len(ys))]
    total = sum(counts)
    tt = min(tile or WGRAD_TILE, t)
    steps = t // tt
    rows, cols = (n, k) if transposed else (k, n)
    kp = rows // row_parts

    def body(x_ref, *refs):
        y_refs, o_refs, acc_ref = refs[:len(ys)], refs[len(ys):len(ys) + row_parts], refs[len(ys) + row_parts]
        p, s = pl.program_id(0), pl.program_id(1)

        @pl.when(s == 0)
        def _():
            acc_ref[...] = jnp.zeros_like(acc_ref)
        for j, y_ref in enumerate(y_refs):
            @pl.when(jnp.logical_and(p >= offsets[j], p < offsets[j] + counts[j]))
            def _(y_ref=y_ref):
                acc_ref[...] += _dot_tn(y_ref[0], x_ref[...]) if transposed else _dot_tn(x_ref[...], y_ref[0])

        @pl.when(s == steps - 1)
        def _():
            for part, o_ref in enumerate(o_refs):
                o_ref[0] = acc_ref[part * kp:(part + 1) * kp, :].astype(BF)

    def turn(j):
        lo, hi = offsets[j], offsets[j] + counts[j]
        return lambda p, s: (jnp.clip(p - lo, 0, counts[j] - 1), jnp.where(p < lo, 0, jnp.where(p >= hi, steps - 1, s)), 0)

    outs, landed = _pallas(
        body, name=name, grid=(total, steps), ride=ride,
        in_specs=[pl.BlockSpec((tt, k), lambda p, s: (s, 0))] + [pl.BlockSpec((1, tt, n), turn(j)) for j in range(len(ys))],
        out_specs=[pl.BlockSpec((1, kp, cols), lambda p, s: (p, 0, 0))] * row_parts,
        out_shape=[S((total, kp, cols), BF)] * row_parts,
        scratch_shapes=[pltpu.VMEM((rows, cols), F32)],
        args=[x, *ys])
    out = outs[0] if row_parts == 1 else outs
    return (out, landed) if ride is not None else out


def _ffn_bwd_hidden(dh, to_gate, to_up, w_out, name, ride=None):
    t = dh.shape[0]
    tm = min(TOKEN_TILE, t)

    def body(dh_ref, to_gate_ref, to_up_ref, wout_ref, df_ref, dgate_ref, dup_ref):
        df = (0.5 * dh_ref[...]).astype(BF)
        df_ref[...] = df
        for c in range(N_FF_CHUNKS):
            dact = _dot_nt(df, wout_ref[c])
            dgate_ref[c] = (dact * to_gate_ref[c].astype(F32)).astype(BF)
            dup_ref[c] = (dact * to_up_ref[c].astype(F32)).astype(BF)

    return _pallas(
        body, name=name, grid=(t // tm,), ride=ride,
        in_specs=[_rows(tm, D_MODEL), _chunks(tm), _chunks(tm), _const_spec(w_out.shape)],
        out_specs=[_rows(tm, D_MODEL), _chunks(tm), _chunks(tm)],
        out_shape=[S((t, D_MODEL), BF)] + [S((N_FF_CHUNKS, t, FF_CHUNK), BF)] * 2,
        args=[dh, to_gate, to_up, w_out])


def _ffn_bwd_input(dh, h_in, g, dgate, dup, w_in, name, ride=None):
    t = dh.shape[0]
    tm = min(TOKEN_TILE, t)

    def body(dh_ref, h_ref, g_ref, dgate_ref, dup_ref, win_ref, dhi_ref, dg_ref):
        dn = jnp.zeros((tm, D_MODEL), F32)
        for c in range(N_FF_CHUNKS):
            dn = dn + _dot_nt(dgate_ref[c], win_ref[c]) + _dot_nt(dup_ref[c], win_ref[N_FF_CHUNKS + c])
        dhi, dg = _rms_bwd(dn, h_ref[...], g_ref[...])
        _accumulate(dg_ref, dg)
        dhi_ref[...] = dh_ref[...] + dhi

    vec = (1, D_MODEL)
    return _pallas(
        body, name=name, grid=(t // tm,), ride=ride,
        in_specs=[_rows(tm, D_MODEL), _rows(tm, D_MODEL), _const_spec(vec), _chunks(tm), _chunks(tm), _const_spec(w_in.shape)],
        out_specs=[_rows(tm, D_MODEL), _acc_spec(vec)],
        out_shape=[S((t, D_MODEL), F32), S(vec, F32)],
        args=[dh, h_in, g, dgate, dup, w_in])


def _mixer_bwd(dh2, proj, yc, ya, conv_w, w_co, w_ao, w_mo, ride=None):
    t = dh2.shape[0]
    tm = min(TOKEN_TILE, t)

    def body(dh_ref, cb_ref, cc_ref, cx_ref, gc_ref, ga_ref, cch_ref, cxh_ref, yc_ref, ya_ref, cw_ref, wco_ref, wao_ref, wmo_ref,
             dhb_ref, dyc_ref, dya_ref, dgc_ref, dga_ref, dcb_ref, dcv_ref, do_ref):
        dhb = dh_ref[...].astype(BF)
        dhb_ref[...] = dhb
        dmerged = _dot_nt(dhb, wmo_ref[...])
        sc = jax.nn.sigmoid(gc_ref[0].astype(F32))
        sa = jax.nn.sigmoid(ga_ref[0].astype(F32))
        dyc = (dmerged * sc).astype(BF)
        dya = (dmerged * sa).astype(BF)
        dyc_ref[...] = dyc
        dya_ref[...] = dya
        dgc_ref[...] = (dmerged * yc_ref[...].astype(F32) * sc * (1.0 - sc)).astype(BF)
        dga_ref[...] = (dmerged * ya_ref[...].astype(F32) * sa * (1.0 - sa)).astype(BF)
        m, m1, m2 = _conv_inputs(cc_ref, cx_ref, cch_ref, cxh_ref)
        cw = cw_ref[...]
        cv = cw[0:1, :] * m2 + cw[1:2, :] * m1 + cw[2:3, :] * m
        dycin = _dot_nt(dyc, wco_ref[...])
        dcb_ref[...] = (dycin * cv).astype(BF)
        dcv_ref[...] = (dycin * cb_ref[0].astype(F32)).astype(BF)
        do_ref[...] = _dot_nt(dya, wao_ref[...]).astype(BF)

    sq = (D_MODEL, D_MODEL)
    return _pallas(
        body, name="mixer_bwd", grid=(t // tm,), ride=ride,
        in_specs=[_rows(tm, D_MODEL), _piece(0, tm), _piece(1, tm), _piece(2, tm), _piece(6, tm), _piece(7, tm),
                  _prev_halo(1, tm), _prev_halo(2, tm), _rows(tm, D_MODEL), _rows(tm, D_MODEL),
                  _const_spec((3, D_MODEL)), _const_spec(sq), _const_spec(sq), _const_spec(sq)],
        out_specs=[_rows(tm, D_MODEL)] * 8,
        out_shape=[S((t, D_MODEL), BF)] * 8,
        args=[dh2, proj, proj, proj, proj, proj, proj, proj, yc, ya, conv_w, w_co, w_ao, w_mo])


TAP_ROWS = 8


def _conv_bwd(dcv, proj, conv_w):
    t = dcv.shape[0]
    tm = min(TOKEN_TILE, t)
    steps = t // tm

    def body(dcv_ref, nxt_ref, cc_ref, cx_ref, cch_ref, cxh_ref, cw_ref, dcc_ref, dcx_ref, dw_ref):
        i = pl.program_id(0)
        m, m1, m2 = _conv_inputs(cc_ref, cx_ref, cch_ref, cxh_ref)
        d0 = dcv_ref[...].astype(F32)
        nxt = jnp.where(i == steps - 1, 0.0, nxt_ref[...].astype(F32))
        row = lax.broadcasted_iota(jnp.int32, (tm, 1), 0)
        d1 = jnp.where(row == tm - 1, nxt[0:1, :], pltpu.roll(d0, tm - 1, 0))
        d2 = pltpu.roll(d0, tm - 2, 0)
        d2 = jnp.where(row == tm - 2, nxt[0:1, :], jnp.where(row == tm - 1, nxt[1:2, :], d2))
        cw = cw_ref[...]
        dm = cw[2:3, :] * d0 + cw[1:2, :] * d1 + cw[0:1, :] * d2
        dcc_ref[...] = (dm * cx_ref[0].astype(F32)).astype(BF)
        dcx_ref[...] = (dm * cc_ref[0].astype(F32)).astype(BF)
        tap_row = lax.broadcasted_iota(jnp.int32, (TAP_ROWS, 1), 0)
        dw = jnp.zeros((TAP_ROWS, D_MODEL), F32)
        for j, mk in enumerate((m2, m1, m)):
            dw = jnp.where(tap_row == j, jnp.sum(d0 * mk, axis=0, keepdims=True), dw)
        _accumulate(dw_ref, dw)

    nxt_spec = pl.BlockSpec((HALO, D_MODEL), lambda i: (jnp.minimum((i + 1) * (tm // HALO), t // HALO - 1), 0))
    return pl.pallas_call(
        body, name="conv_bwd", grid=(steps,),
        in_specs=[_rows(tm, D_MODEL), nxt_spec, _piece(1, tm), _piece(2, tm), _prev_halo(1, tm), _prev_halo(2, tm),
                  _const_spec((3, D_MODEL))],
        out_specs=[_rows(tm, D_MODEL), _rows(tm, D_MODEL), _acc_spec((TAP_ROWS, D_MODEL))],
        out_shape=[S((t, D_MODEL), BF), S((t, D_MODEL), BF), S((TAP_ROWS, D_MODEL), F32)],
    )(dcv, dcv, proj, proj, proj, proj, conv_w)


def _mix_bwd(dpieces, w_mix, h1, dh2, g, ride=None):
    t = h1.shape[0]
    tm = min(TOKEN_TILE, t)

    def body(*refs):
        pieces, (w_ref, h_ref, dh_ref, g_ref, dhi_ref, dg_ref) = refs[:N_MIX], refs[N_MIX:]
        du = jnp.zeros((tm, D_MODEL), F32)
        for d in range(N_MIX):
            du = du + _dot_nt(pieces[d][...], w_ref[d])
        dhi, dg = _rms_bwd(du, h_ref[...], g_ref[...])
        _accumulate(dg_ref, dg)
        dhi_ref[...] = dh_ref[...] + dhi

    vec = (1, D_MODEL)
    return _pallas(
        body, name="mix_bwd", grid=(t // tm,), ride=ride,
        in_specs=[_rows(tm, D_MODEL)] * N_MIX + [_const_spec(w_mix.shape), _rows(tm, D_MODEL), _rows(tm, D_MODEL), _const_spec(vec)],
        out_specs=[_rows(tm, D_MODEL), _acc_spec(vec)],
        out_shape=[S((t, D_MODEL), F32), S(vec, F32)],
        args=[*dpieces, w_mix, h1, dh2, g])


def _adamw(partials, w, m, v, name):
    parts = list(partials) if isinstance(partials, (list, tuple)) else [partials]
    r, c = w.shape
    tr = next(d for d in (r, 512, 352, 256) if d <= 512 // len(parts) and r % d == 0)
    first_tile = [sum(p.shape[1] for p in parts[:j]) // tr for j in range(len(parts))]
    c1 = 1.0 - ADAM_B1 ** ADAM_STEP
    c2 = 1.0 - ADAM_B2 ** ADAM_STEP

    def body(*refs):
        p_refs, (w_ref, m_ref, v_ref, g_ref, d_ref, mo_ref, vo_ref) = refs[:len(parts)], refs[len(parts):]
        g = None
        for j, p_ref in enumerate(p_refs):
            gj = p_ref[0].astype(F32)
            for s in range(1, N_SHARDS):
                gj = gj + p_ref[s].astype(F32)
            g = gj if g is None else jnp.where(pl.program_id(0) >= first_tile[j], gj, g)
        mn = ADAM_B1 * m_ref[...] + (1.0 - ADAM_B1) * g
        vn = ADAM_B2 * v_ref[...] + (1.0 - ADAM_B2) * (g * g)
        g_ref[...] = g
        mo_ref[...] = mn
        vo_ref[...] = vn
        d_ref[...] = -ADAM_LR * ((mn / c1) / (jnp.sqrt(vn / c2) + ADAM_EPS) + ADAM_WD * w_ref[...])

    def rows_of(j):
        last = parts[j].shape[1] // tr - 1
        return lambda i: (0, jnp.clip(i - first_tile[j], 0, last), 0)

    blk = pl.BlockSpec((tr, c), lambda i: (i, 0))
    return pl.pallas_call(
        body, name=name, grid=(r // tr,),
        in_specs=[pl.BlockSpec((N_SHARDS, tr, c), rows_of(j)) for j in range(len(parts))] + [blk, blk, blk],
        out_specs=[blk] * 4, out_shape=[S((r, c), F32)] * 4,
    )(*parts, w, m, v)


_MATRICES = ("ffn1_w_in", "ffn1_w_out", "w_mix_in", "conv_w", "w_conv_out", "w_attn_out", "w_mix_out",
             "ffn2_w_in", "ffn2_w_out", "w_ple_gate", "w_ple_proj")
_GAINS = ("ffn1_norm", "mix_norm", "ffn2_norm", "ple_norm", "final_norm")
_WEIGHTS = ("ffn1_norm", "ffn1_w_in", "ffn1_w_out", "mix_norm", "w_mix_in", "conv_w", "w_conv_out", "w_attn_out", "w_mix_out",
            "ffn2_norm", "ffn2_w_in", "ffn2_w_out", "ple_norm", "w_ple_gate", "w_ple_proj", "final_norm")
CONV_ROWS = 8
_TRANSPOSED = ("ffn1_w_in", "ffn2_w_in")


def _columns_from_shards(g):
    return jnp.transpose(g, (1, 0, 2)).reshape(g.shape[1], N_SHARDS * g.shape[2])


def _shards_from_columns(a):
    r, c = a.shape
    return jnp.transpose(a.reshape(r, N_SHARDS, c // N_SHARDS), (1, 0, 2))


def kernel(x, p, ffn1_norm, ffn1_w_in, ffn1_w_out, mix_norm, w_mix_in, conv_w, w_conv_out, w_attn_out, w_mix_out, ffn2_norm, ffn2_w_in, ffn2_w_out, ple_norm, w_ple_gate, w_ple_proj, final_norm, loss_target, m_ffn1_norm, m_ffn1_w_in, m_ffn1_w_out, m_mix_norm, m_w_mix_in, m_conv_w, m_w_conv_out, m_w_attn_out, m_w_mix_out, m_ffn2_norm, m_ffn2_w_in, m_ffn2_w_out, m_ple_norm, m_w_ple_gate, m_w_ple_proj, m_final_norm, v_ffn1_norm, v_ffn1_w_in, v_ffn1_w_out, v_mix_norm, v_w_mix_in, v_conv_w, v_w_conv_out, v_w_attn_out, v_w_mix_out, v_ffn2_norm, v_ffn2_w_in, v_ffn2_w_out, v_ple_norm, v_w_ple_gate, v_w_ple_proj, v_final_norm):
    given = dict(locals())
    t = x.shape[1]
    xs = x.reshape(t, D_MODEL)
    ps = p.reshape(t, PLE_DIM)
    target = loss_target.reshape(t, D_MODEL)
    shard = {k: given[k].reshape(given[k].shape[-2:]) for k in _MATRICES}
    gain = {k: given[k].reshape(1, D_MODEL) for k in _GAINS}

    send = {k: shard[k].astype(BF) for k in _MATRICES}
    send["conv_w"] = jnp.pad(shard["conv_w"], ((0, CONV_ROWS - 3), (0, 0)))
    loss_vec, dx, landed, gain_grads = _forward_backward(xs, ps, target, gain, send)
    gain_rows = jnp.concatenate([gain_grads[k] for k in _GAINS] + [loss_vec, jnp.zeros((8 - len(_GAINS) - 1, D_MODEL), F32)], axis=0)
    gain_parts, = _exchange_alone("gather", [gain_rows], "gather_gain_gradients")

    out = {}
    for k in _MATRICES:
        w, m, v = shard[k], given["m_" + k].reshape(shard[k].shape), given["v_" + k].reshape(shard[k].shape)
        part = landed[k]
        if k == "conv_w":
            pad = ((0, CONV_ROWS - 3), (0, 0))
            w, m, v = jnp.pad(w, pad), jnp.pad(m, pad), jnp.pad(v, pad, constant_values=1.0)
        if k in _TRANSPOSED:
            w, m, v = w.T, m.T, v.T
        res = _adamw(part, w, m, v, "adamw_" + k)
        out[k] = [r[:3] if k == "conv_w" else (r.T if k in _TRANSPOSED else r) for r in res]
    stack = lambda pre: jnp.concatenate([given[pre + k].reshape(1, D_MODEL) for k in _GAINS] + [jnp.ones((8 - len(_GAINS), D_MODEL), F32)], axis=0)
    res = _adamw(gain_parts, stack(""), stack("m_"), stack("v_"), "adamw_gains")
    for j, k in enumerate(_GAINS):
        out[k] = [r[j:j + 1] for r in res]

    loss = jnp.sum(gain_parts[:, len(_GAINS), 0])
    per_kind = [[out[k][j].reshape(given[k].shape) for k in _WEIGHTS] for j in range(4)]
    return (loss, dx.reshape(x.shape), *per_kind[0], *per_kind[1], *per_kind[2], *per_kind[3])


def _forward_backward(xs, ps, target, gain, send, full=None):
    exchange = full is None
    full = dict(full or {})
    grads, landed = {}, {}

    def gather(names):
        return ("gather", [send[k] for k in names]) if exchange else None

    def scatter(names):
        return ("scatter", [grads[k] for k in names]) if exchange else None

    def keep(into, names, got):
        into.update(zip(names, got))

    first = ("ffn1_w_in",)
    (n1,), got = _prenorm(xs, gain["ffn1_norm"], ride=gather(first))
    keep(full, first, got)
    w1_in = full["ffn1_w_in"]
    second = ("ffn1_w_out", "w_mix_in")
    (act1, to_gate1, to_up1), got = _ffn_up(n1, w1_in, "ffn1_up", ride=gather(second))
    keep(full, second, got)
    w1_out = full["ffn1_w_out"].reshape(N_FF_CHUNKS, FF_CHUNK, D_MODEL)
    third = ("conv_w", "w_conv_out", "w_attn_out", "w_mix_out")
    (h1, u), got = _ffn_down(xs, act1, w1_out, gain["mix_norm"], "ffn1_down", ride=gather(third))
    keep(full, third, got)
    w_mix = full["w_mix_in"]
    w_co, w_ao, w_mo = (full[k].reshape(D_MODEL, D_MODEL) for k in ("w_conv_out", "w_attn_out", "w_mix_out"))
    taps = _columns_from_shards(full["conv_w"][:, :3, :])
    rest = ("ffn2_w_in", "ffn2_w_out", "w_ple_gate", "w_ple_proj")
    (proj,), got = _mix_proj(u, w_mix, ride=gather(rest))
    keep(full, rest, got)
    w2_in, w2_out = full["ffn2_w_in"], full["ffn2_w_out"].reshape(N_FF_CHUNKS, FF_CHUNK, D_MODEL)
    w_pg = full["w_ple_gate"].reshape(D_MODEL, D_MODEL)
    w_pp = _columns_from_shards(full["w_ple_proj"])
    o, o_bf, a_first, beta_first, reach = _attn_fwd(proj)
    h2, n3, ycin, yc, ya, merged = _mixer_out(proj, o_bf, h1, taps, w_co, w_ao, w_mo, gain["ffn2_norm"])
    (act2, to_gate2, to_up2), _ = _ffn_up(n3, w2_in, "ffn2_up")
    (h3, n4), _ = _ffn_down(h2, act2, w2_out, gain["ple_norm"], "ffn2_down")
    dh3, ds, dpp, loss_vec, dg_final, dg_ple = _tail(h3, n4, ps, w_pg, w_pp, gain["ple_norm"], gain["final_norm"], target)

    one = lambda a: a[None]
    by_rows = lambda g, rows: g.reshape(N_SHARDS, rows // N_SHARDS, D_MODEL)
    square = WGRAD_TILE // 2
    grads["w_ple_gate"] = by_rows(_wgrad(one(n4), one(ds), "wgrad_ple_gate", tile=square), D_MODEL)
    grads["w_ple_proj"] = _shards_from_columns(_wgrad(one(ps), one(dpp), "wgrad_ple_proj")[0])
    ple = ("w_ple_gate", "w_ple_proj")
    (df2, dgate2, dup2), got = _ffn_bwd_hidden(dh3, to_gate2, to_up2, w2_out, "ffn2_bwd_hidden", ride=scatter(ple))
    keep(landed, ple, got)
    grads["ffn2_w_out"] = by_rows(_wgrad(act2, one(df2), "wgrad_ffn2_out"), D_FF)
    grads["ffn2_w_in"] = _wgrad_pieces(n3, [dgate2, dup2], "wgrad_ffn2_in", transposed=True)
    (dh2, dg_ffn2), got = _ffn_bwd_input(dh3, h2, gain["ffn2_norm"], dgate2, dup2, w2_in, "ffn2_bwd_input", ride=scatter(("ffn2_w_out",)))
    keep(landed, ("ffn2_w_out",), got)
    (dh2b, dyc, dya, dgc, dga, dcb, dcv, d_o), _ = _mixer_bwd(dh2, proj, yc, ya, taps, w_co, w_ao, w_mo)
    square_grads = _wgrad_pairs([merged, ycin, o_bf], [dh2b, dyc, dya], "wgrad_mixer_out", tile=square // 2)
    grads["w_mix_out"], grads["w_conv_out"], grads["w_attn_out"] = (by_rows(g, D_MODEL) for g in square_grads)
    dcc, dcx, dtaps = _conv_bwd(dcv, proj, taps)
    grads["conv_w"] = jnp.pad(_shards_from_columns(dtaps[:3]), ((0, 0), (0, CONV_ROWS - 3), (0, 0)))
    behind_attn = ("ffn2_w_in",)
    (dq, dk, dv), got = _attn_bwd(proj, o, d_o, a_first, beta_first, reach, ride=scatter(behind_attn))
    keep(landed, behind_attn, got)
    dpieces = [dcb, dcc, dcx, dq, dk, dv, dgc, dga]
    half = N_MIX // 2
    squares = ("w_mix_out", "w_conv_out", "w_attn_out", "conv_w")
    first_half = _wgrad_pieces(u, [one(dp) for dp in dpieces[:half]], "wgrad_mix_in_a", tile=WGRAD_TILE // 2, row_parts=2, ride=scatter(squares))
    if exchange:
        first_half, got = first_half
        keep(landed, squares, got)
    tops, bottoms = zip(first_half,
                        _wgrad_pieces(u, [one(dp) for dp in dpieces[half:]], "wgrad_mix_in_b", tile=WGRAD_TILE // 2, row_parts=2))
    grads["w_mix_in top"], grads["w_mix_in bottom"] = jnp.concatenate(tops, axis=0), jnp.concatenate(bottoms, axis=0)
    (dh1, dg_mix), top = _mix_bwd(dpieces, w_mix, h1, dh2, gain["mix_norm"], ride=scatter(("w_mix_in top",)))
    (df1, dgate1, dup1), bottom = _ffn_bwd_hidden(dh1, to_gate1, to_up1, w1_out, "ffn1_bwd_hidden", ride=scatter(("w_mix_in bottom",)))
    if exchange:
        landed["w_mix_in"] = [top[0], bottom[0]]
    else:
        grads["w_mix_in"] = jnp.concatenate([grads.pop("w_mix_in top"), grads.pop("w_mix_in bottom")], axis=1)
    grads["ffn1_w_out"] = by_rows(_wgrad(act1, one(df1), "wgrad_ffn1_out"), D_FF)
    if exchange:
        grads["ffn1_w_in"], got = _wgrad_pieces(n1, [dgate1, dup1], "wgrad_ffn1_in", transposed=True, ride=scatter(("ffn1_w_out",)))
        keep(landed, ("ffn1_w_out",), got)
    else:
        grads["ffn1_w_in"] = _wgrad_pieces(n1, [dgate1, dup1], "wgrad_ffn1_in", transposed=True)
    (dx, dg_ffn1), got = _ffn_bwd_input(dh1, xs, gain["ffn1_norm"], dgate1, dup1, w1_in, "ffn1_bwd_input", ride=scatter(("ffn1_w_in",)))
    keep(landed, ("ffn1_w_in",), got)
    gain_grads = dict(ffn1_norm=dg_ffn1, mix_norm=dg_mix, ffn2_norm=dg_ffn2, ple_norm=dg_ple, final_norm=dg_final)
    return loss_vec, dx, (landed if exchange else grads), gain_grads
```
